```python
import math
import jax, jax.numpy as jnp
from jax import lax
import numpy as np

D_MODEL = 1024
BATCH = 8
SEQ = 2048
DEPTH = 1

NORM_EPS = 1e-6
D_FF = 2816
RWKV_HEADS = 8
RWKV_HEAD_DIM = 64
RWKV_DIM = RWKV_HEADS * RWKV_HEAD_DIM
DECAY_LORA = 64
AAA_LORA = 64
GATE_LORA = 128
RWKV_GN_EPS = 64e-5
RWKV_PROJ = 3 * RWKV_DIM + DECAY_LORA + AAA_LORA + GATE_LORA
NSA_HEADS = 8
NSA_KV_GROUPS = 2
NSA_HEAD_DIM = 64
NSA_DIM = NSA_HEADS * NSA_HEAD_DIM
NSA_KV_DIM = NSA_KV_GROUPS * NSA_HEAD_DIM
CMP_LEN = 32
CMP_STRIDE = 16
CMP_HIDDEN = 256
SEL_BLOCK = 64
SEL_TOP_N = 16
WINDOW = 512
Q_BLOCK = 128
SEL_Q_CHUNK = 64
REL_BUCKETS = 32
REL_MAX_DIST = 128
MEM_TOKENS = 256
MEM_HEADS = 4
MEM_HEAD_DIM = 128
MEM_DIM = MEM_HEADS * MEM_HEAD_DIM
N_BRANCH = 3
IN_SPLITS = (RWKV_PROJ, NSA_DIM, 6 * NSA_KV_DIM, 3 * NSA_HEADS, MEM_DIM, N_BRANCH * D_MODEL)
IN_DIM = RWKV_PROJ + NSA_DIM + 6 * NSA_KV_DIM + 3 * NSA_HEADS + MEM_DIM + N_BRANCH * D_MODEL

kernel_name = 'hybrid_rwkv7_nsa_memxattn_macaron'


def _split(a, sizes):
    offs = np.cumsum(sizes)[:-1].tolist()
    return jnp.split(a, offs, axis=-1)


def _rmsnorm(x, g):
    xf = x.astype(jnp.float32)
    y = xf * lax.rsqrt(jnp.mean(xf * xf, axis=-1, keepdims=True) + NORM_EPS)
    return (y * g.astype(jnp.float32)).astype(x.dtype)


def _swiglu(h, w_gate, w_up, w_down):
    return (jax.nn.silu(h @ w_gate) * (h @ w_up)) @ w_down


def _token_shift(p):
    return jnp.pad(p[:, :-1], ((0, 0), (1, 0), (0, 0)))


def _masked_softmax(s, mask):
    s = jnp.where(mask, s.astype(jnp.float32), -1e30)
    p = jax.nn.softmax(s, axis=-1)
    return jnp.where(mask, p, 0.0)


def _t5_bucket(dist):
    n = jnp.maximum(dist, 0)
    exact = REL_BUCKETS // 2
    nf = jnp.maximum(n, 1).astype(jnp.float32)
    large = exact + (jnp.log(nf / exact) / math.log(REL_MAX_DIST / exact)
                     * (REL_BUCKETS - exact)).astype(jnp.int32)
    large = jnp.minimum(large, REL_BUCKETS - 1)
    return jnp.where(n < exact, n, large)


def _rel_bias(dist, tab):
    b = tab[_t5_bucket(dist)]
    return jnp.moveaxis(b, (-2, -1), (0, 1))


def _rwkv7_scan(r, decay, k, v, kk, a):
    B, S, H, N = r.shape

    def step(state, inp):
        r_t, w_t, k_t, v_t, kk_t, a_t = inp
        sa = jnp.einsum('bhvk,bhk->bhv', state, -kk_t)
        state = (state * w_t[:, :, None, :]
                 + sa[..., None] * (kk_t * a_t)[:, :, None, :]
                 + v_t[..., None] * k_t[:, :, None, :])
        y = jnp.einsum('bhvk,bhk->bhv', state, r_t)
        return state, y

    xs = tuple(jnp.moveaxis(t.astype(jnp.float32), 1, 0) for t in (r, decay, k, v, kk, a))
    state0 = jnp.zeros((B, H, N, N), jnp.float32)
    _, ys = lax.scan(step, state0, xs)
    return jnp.moveaxis(ys, 0, 1)


def _rwkv7_branch(p, mu, w0, w2, a0, a2, g2, k_k, k_a, r_k, gn_gain, gn_bias):
    B, S, _ = p.shape
    H, N = RWKV_HEADS, RWKV_HEAD_DIM
    p = p + (_token_shift(p) - p) * mu
    r, k, v, xw, xa, xg = _split(p, (RWKV_DIM, RWKV_DIM, RWKV_DIM, DECAY_LORA, AAA_LORA, GATE_LORA))
    w = -jax.nn.softplus(-(w0 + jnp.tanh(xw) @ w2)) - 0.5
    decay = jnp.exp(-jnp.exp(w.astype(jnp.float32)))
    a = jax.nn.sigmoid(a0 + xa @ a2)
    g = jax.nn.sigmoid(xg) @ g2
    heads = lambda t: t.reshape(B, S, H, N)
    kk = heads(k * k_k).astype(jnp.float32)
    kk = kk / jnp.maximum(jnp.sqrt(jnp.sum(kk * kk, axis=-1, keepdims=True)), 1e-12)
    k = k * (1.0 + (a - 1.0) * k_a)
    r, k, v, a, decay = heads(r), heads(k), heads(v), heads(a), heads(decay)
    y = _rwkv7_scan(r, decay, k, v, kk, a)
    mean = jnp.mean(y, axis=-1, keepdims=True)
    var = jnp.mean(jnp.square(y - mean), axis=-1, keepdims=True)
    y = ((y - mean) * lax.rsqrt(var + RWKV_GN_EPS)).reshape(B, S, RWKV_DIM)
    y = (y * gn_gain + gn_bias).astype(p.dtype)
    bonus = jnp.sum(r * k * r_k, axis=-1, keepdims=True) * v
    y = y + bonus.reshape(B, S, RWKV_DIM)
    return y * g


def _nsa_branch(q, kv, gate_logits, pe_k, cmp_k_w1, cmp_k_w2, pe_v, cmp_v_w1, cmp_v_w2, rel_bias):
    B, S, _ = q.shape
    G, HPG, DH = NSA_KV_GROUPS, NSA_HEADS // NSA_KV_GROUPS, NSA_HEAD_DIM
    qg = q.reshape(B, S, G, HPG, DH).transpose(0, 2, 3, 1, 4) * (DH ** -0.5)
    k_cmp, v_cmp, k_sel, v_sel, k_win, v_win = [
        t.reshape(B, S, G, DH).transpose(0, 2, 1, 3) for t in _split(kv, (NSA_KV_DIM,) * 6)]
    t_pos = jnp.arange(S)
    bias_tab = rel_bias.reshape(REL_BUCKETS, G, HPG)

    n_cmp = (S - CMP_LEN) // CMP_STRIDE + 1
    blk_idx = jnp.arange(n_cmp)[:, None] * CMP_STRIDE + jnp.arange(CMP_LEN)[None, :]

    def compress(t, pe, w1, w2):
        blocks = t[:, :, blk_idx] + pe
        return jax.nn.gelu(blocks.reshape(B, G, n_cmp, CMP_LEN * DH) @ w1) @ w2

    kc = compress(k_cmp, pe_k, cmp_k_w1, cmp_k_w2)
    vc = compress(v_cmp, pe_v, cmp_v_w1, cmp_v_w2)
    cmp_end = jnp.arange(n_cmp) * CMP_STRIDE + CMP_LEN - 1
    dist_c = t_pos[:, None] - cmp_end[None, :]
    s_c = jnp.einsum('bghsd,bgnd->bghsn', qg, kc) + _rel_bias(dist_c, bias_tab)
    p_c = _masked_softmax(s_c, dist_c >= 0)
    o_cmp = jnp.einsum('bghsn,bgnd->bghsd', p_c.astype(vc.dtype), vc)

    n_blk = S // SEL_BLOCK
    n_sel = min(SEL_TOP_N, n_blk)
    jb = jnp.arange(n_blk)
    ic = jnp.arange(n_cmp)
    overlap = ((ic[:, None] * CMP_STRIDE <= jb[None, :] * SEL_BLOCK + SEL_BLOCK - 1)
               & (ic[:, None] * CMP_STRIDE + CMP_LEN - 1 >= jb[None, :] * SEL_BLOCK)).astype(jnp.float32)
    imp = jnp.einsum('bghsn,nj->bgsj', p_c, overlap)
    cur = t_pos[:, None] // SEL_BLOCK
    forced = (jb[None, :] == 0) | (jb[None, :] == cur) | (jb[None, :] == cur - 1)
    imp = jnp.where(jb[None, :] > cur, -1e6, jnp.where(forced, 1e6, imp))
    _, sel_idx = lax.top_k(imp, n_sel)

    kb = k_sel.reshape(B, G, n_blk, SEL_BLOCK, DH)
    vb = v_sel.reshape(B, G, n_blk, SEL_BLOCK, DH)
    gather = jax.vmap(jax.vmap(lambda blocks, idx: blocks[idx]))
    n_keys = n_sel * SEL_BLOCK
    tab_lookup = jax.vmap(lambda bk, tb: tb[bk], in_axes=(1, 1), out_axes=1)

    def sel_chunk(c):
        t0 = c * SEL_Q_CHUNK
        q_c = lax.dynamic_slice_in_dim(qg, t0, SEL_Q_CHUNK, axis=3)
        idx_c = lax.dynamic_slice_in_dim(sel_idx, t0, SEL_Q_CHUNK, axis=2)
        k_c = gather(kb, idx_c).reshape(B, G, SEL_Q_CHUNK, n_keys, DH)
        v_c = gather(vb, idx_c).reshape(B, G, SEL_Q_CHUNK, n_keys, DH)
        key_pos = (idx_c[..., None] * SEL_BLOCK + jnp.arange(SEL_BLOCK)).reshape(B, G, SEL_Q_CHUNK, n_keys)
        dist = (t0 + jnp.arange(SEL_Q_CHUNK))[:, None] - key_pos
        bias = jnp.moveaxis(tab_lookup(_t5_bucket(dist), bias_tab), -1, 2)
        s = jnp.einsum('bghqd,bgqkd->bghqk', q_c, k_c) + bias
        p = _masked_softmax(s, (dist >= 0)[:, :, None])
        return jnp.einsum('bghqk,bgqkd->bghqd', p.astype(v_c.dtype), v_c)

    o_sel = lax.map(sel_chunk, jnp.arange(S // SEL_Q_CHUNK))
    o_sel = jnp.moveaxis(o_sel, 0, 3).reshape(B, G, HPG, S, DH)

    nqb = S // Q_BLOCK
    nwb = WINDOW // Q_BLOCK

    def band(t):
        tp = jnp.pad(t, ((0, 0), (0, 0), (WINDOW, 0), (0, 0))).reshape(B, G, nqb + nwb, Q_BLOCK, DH)
        return jnp.concatenate([tp[:, :, o:o + nqb] for o in range(nwb + 1)], axis=3)

    kw, vw = band(k_win), band(v_win)
    kj = jnp.arange((nwb + 1) * Q_BLOCK)[None, :]
    dist_w = WINDOW + jnp.arange(Q_BLOCK)[:, None] - kj
    key_pos_w = jnp.arange(nqb)[:, None, None] * Q_BLOCK - WINDOW + kj[None]
    mask_w = (dist_w >= 0) & (dist_w < WINDOW) & (key_pos_w >= 0)
    bias_w = _rel_bias(dist_w, bias_tab)[:, :, None]
    qw = qg.reshape(B, G, HPG, nqb, Q_BLOCK, DH)
    s_w = jnp.einsum('bghnqd,bgnkd->bghnqk', qw, kw) + bias_w
    p_w = _masked_softmax(s_w, mask_w)
    o_win = jnp.einsum('bghnqk,bgnkd->bghnqd', p_w.astype(vw.dtype), vw).reshape(B, G, HPG, S, DH)

    gates = jax.nn.sigmoid(gate_logits).reshape(B, S, 3, G, HPG).transpose(2, 0, 3, 4, 1)[..., None]
    o = gates[0] * o_cmp + gates[1] * o_sel + gates[2] * o_win
    return o.transpose(0, 3, 1, 2, 4).reshape(B, S, NSA_DIM)


def _memory_branch(q, mem_n, w_k, w_v):
    B, S, _ = q.shape
    M = mem_n.shape[1]
    qh = q.reshape(B, S, MEM_HEADS, MEM_HEAD_DIM) * (MEM_HEAD_DIM ** -0.5)
    kh = (mem_n @ w_k).reshape(B, M, MEM_HEADS, MEM_HEAD_DIM)
    vh = (mem_n @ w_v).reshape(B, M, MEM_HEADS, MEM_HEAD_DIM)
    s = jnp.einsum('bshd,bmhd->bhsm', qh, kh)
    p = jax.nn.softmax(s.astype(jnp.float32), axis=-1).astype(vh.dtype)
    return jnp.einsum('bhsm,bmhd->bshd', p, vh).reshape(B, S, MEM_DIM)


def setup_inputs(seed: int = 0) -> dict:
    key = jax.random.key(seed)
    ks = iter(jax.random.split(key, 40))
    L, D = DEPTH, D_MODEL

    def nrm(shape, scale):
        return scale * jax.random.normal(next(ks), shape, jnp.float32)

    def gain(shape):
        return 1.0 + nrm(shape, 0.02)

    return {
        'x': nrm((BATCH, SEQ, D), 1.0),
        'mem': nrm((BATCH, MEM_TOKENS, D), 1.0),
        'ffn1_norm': gain((L, D)),
        'ffn1_w_gate': nrm((L, D, D_FF), D ** -0.5),
        'ffn1_w_up': nrm((L, D, D_FF), D ** -0.5),
        'ffn1_w_down': nrm((L, D_FF, D), D_FF ** -0.5),
        'mix_norm': gain((L, D)),
        'w_in': nrm((L, D, IN_DIM), D ** -0.5),
        'rwkv_mu': jax.random.uniform(next(ks), (L, RWKV_PROJ), jnp.float32),
        'rwkv_w0': jax.random.uniform(next(ks), (L, RWKV_DIM), jnp.float32, -6.0, -1.0),
        'rwkv_w2': nrm((L, DECAY_LORA, RWKV_DIM), DECAY_LORA ** -0.5),
        'rwkv_a0': nrm((L, RWKV_DIM), 0.1),
        'rwkv_a2': nrm((L, AAA_LORA, RWKV_DIM), AAA_LORA ** -0.5),
        'rwkv_g2': nrm((L, GATE_LORA, RWKV_DIM), GATE_LORA ** -0.5),
        'rwkv_k_k': 0.85 + nrm((L, RWKV_DIM), 0.05),
        'rwkv_k_a': 1.0 + nrm((L, RWKV_DIM), 0.05),
        'rwkv_r_k': nrm((L, RWKV_HEADS, RWKV_HEAD_DIM), 0.1),
        'rwkv_gn_gain': gain((L, RWKV_DIM)),
        'rwkv_gn_bias': nrm((L, RWKV_DIM), 0.02),
        'cmp_pe_k': nrm((L, CMP_LEN, NSA_HEAD_DIM), 0.02),
        'cmp_k_w1': nrm((L, CMP_LEN * NSA_HEAD_DIM, CMP_HIDDEN), (CMP_LEN * NSA_HEAD_DIM) ** -0.5),
        'cmp_k_w2': nrm((L, CMP_HIDDEN, NSA_HEAD_DIM), CMP_HIDDEN ** -0.5),
        'cmp_pe_v': nrm((L, CMP_LEN, NSA_HEAD_DIM), 0.02),
        'cmp_v_w1': nrm((L, CMP_LEN * NSA_HEAD_DIM, CMP_HIDDEN), (CMP_LEN * NSA_HEAD_DIM) ** -0.5),
        'cmp_v_w2': nrm((L, CMP_HIDDEN, NSA_HEAD_DIM), CMP_HIDDEN ** -0.5),
        'rel_bias': nrm((REL_BUCKETS, NSA_HEADS), 0.1),
        'mem_norm': gain((L, D)),
        'mem_w_k': nrm((L, D, MEM_DIM), D ** -0.5),
        'mem_w_v': nrm((L, D, MEM_DIM), D ** -0.5),
        'w_br_rwkv': nrm((L, RWKV_DIM, D), RWKV_DIM ** -0.5),
        'w_br_nsa': nrm((L, NSA_DIM, D), NSA_DIM ** -0.5),
        'w_br_mem': nrm((L, MEM_DIM, D), MEM_DIM ** -0.5),
        'w_out': nrm((L, D, D), D ** -0.5),
        'ffn2_norm': gain((L, D)),
        'ffn2_w_gate': nrm((L, D, D_FF), D ** -0.5),
        'ffn2_w_up': nrm((L, D, D_FF), D ** -0.5),
        'ffn2_w_down': nrm((L, D_FF, D), D_FF ** -0.5),
        'final_norm': gain((D,)),
    }


def reference(x, mem, ffn1_norm, ffn1_w_gate, ffn1_w_up, ffn1_w_down, mix_norm, w_in,
              rwkv_mu, rwkv_w0, rwkv_w2, rwkv_a0, rwkv_a2, rwkv_g2, rwkv_k_k, rwkv_k_a, rwkv_r_k,
              rwkv_gn_gain, rwkv_gn_bias, cmp_pe_k, cmp_k_w1, cmp_k_w2, cmp_pe_v, cmp_v_w1, cmp_v_w2,
              rel_bias, mem_norm, mem_w_k, mem_w_v, w_br_rwkv, w_br_nsa, w_br_mem, w_out,
              ffn2_norm, ffn2_w_gate, ffn2_w_up, ffn2_w_down, final_norm):
    B, S, _ = x.shape
    for l in range(DEPTH):
        x = x + 0.5 * _swiglu(_rmsnorm(x, ffn1_norm[l]), ffn1_w_gate[l], ffn1_w_up[l], ffn1_w_down[l])
        h = _rmsnorm(x, mix_norm[l])
        p_rwkv, q_nsa, kv_nsa, g_nsa, q_mem, g_branch = _split(h @ w_in[l], IN_SPLITS)
        y_rwkv = _rwkv7_branch(p_rwkv, rwkv_mu[l], rwkv_w0[l], rwkv_w2[l], rwkv_a0[l], rwkv_a2[l],
                               rwkv_g2[l], rwkv_k_k[l], rwkv_k_a[l], rwkv_r_k[l],
                               rwkv_gn_gain[l], rwkv_gn_bias[l])
        y_nsa = _nsa_branch(q_nsa, kv_nsa, g_nsa, cmp_pe_k[l], cmp_k_w1[l], cmp_k_w2[l],
                            cmp_pe_v[l], cmp_v_w1[l], cmp_v_w2[l], rel_bias)
        y_mem = _memory_branch(q_mem, _rmsnorm(mem, mem_norm[l]), mem_w_k[l], mem_w_v[l])
        gb = jax.nn.sigmoid(g_branch).reshape(B, S, N_BRANCH, D_MODEL)
        merged = (gb[:, :, 0] * (y_rwkv @ w_br_rwkv[l])
                  + gb[:, :, 1] * (y_nsa @ w_br_nsa[l])
                  + gb[:, :, 2] * (y_mem @ w_br_mem[l]))
        x = x + merged @ w_out[l]
        x = x + 0.5 * _swiglu(_rmsnorm(x, ffn2_norm[l]), ffn2_w_gate[l], ffn2_w_up[l], ffn2_w_down[l])
    return _rmsnorm(x, final_norm)
```

```python
import functools
import math

import jax
import jax.numpy as jnp
from jax import lax
from jax.experimental import pallas as pl
from jax.experimental.pallas import tpu as pltpu

F32 = jnp.float32
BF16 = jnp.bfloat16
HI = lax.Precision.HIGHEST

D_MODEL = 1024
NORM_EPS = 1e-6
D_FF = 2816
RWKV_HEADS = 8
RWKV_HEAD_DIM = 64
RWKV_DIM = RWKV_HEADS * RWKV_HEAD_DIM
DECAY_LORA = 64
AAA_LORA = 64
GATE_LORA = 128
LORA_DIM = DECAY_LORA + AAA_LORA + GATE_LORA
RWKV_GN_EPS = 64e-5
RWKV_PROJ = 3 * RWKV_DIM + LORA_DIM
NSA_HEADS = 8
NSA_KV_GROUPS = 2
NSA_HPG = NSA_HEADS // NSA_KV_GROUPS
NSA_HEAD_DIM = 64
NSA_DIM = NSA_HEADS * NSA_HEAD_DIM
NSA_KV_DIM = NSA_KV_GROUPS * NSA_HEAD_DIM
CMP_LEN = 32
CMP_STRIDE = 16
CMP_HIDDEN = 256
SEL_BLOCK = 64
SEL_TOP_N = 16
WINDOW = 512
REL_BUCKETS = 32
REL_MAX_DIST = 128
MEM_HEADS = 4
MEM_HEAD_DIM = 128
MEM_DIM = MEM_HEADS * MEM_HEAD_DIM
N_BRANCH = 3

COL_RWKV = 0
COL_QNSA = COL_RWKV + RWKV_PROJ
COL_KV = COL_QNSA + NSA_DIM
COL_GBR = COL_KV + 6 * NSA_KV_DIM
COL_QMEM = COL_GBR + N_BRANCH * D_MODEL
COL_GNSA = COL_QMEM + MEM_DIM
PROJ_TN = 768
PROJ_COLS = 9 * PROJ_TN

RWKV_CHUNK = 64
NSA_TQ = 128
VMEM_LIMIT = 56 * 1024 * 1024


def _dot(a, b, precision=None):
    return jnp.dot(a, b, preferred_element_type=F32, precision=precision)


def _dot_nt(a, b, precision=None):
    return lax.dot_general(a, b, (((1,), (1,)), ((), ())), preferred_element_type=F32,
                           precision=precision)


def _params(semantics):
    return pltpu.CompilerParams(dimension_semantics=semantics, vmem_limit_bytes=VMEM_LIMIT)


def _rms(x, g):
    return x * lax.rsqrt(jnp.mean(x * x, axis=-1, keepdims=True) + NORM_EPS) * g


def _ffn_body(x_ref, g_ref, wg_ref, wu_ref, wd_ref, fg_ref, o_ref, h_ref, acc_ref, *, nf, final):
    j = pl.program_id(1)

    @pl.when(j == 0)
    def _():
        h_ref[...] = _rms(x_ref[...], g_ref[...]).astype(BF16)
        acc_ref[...] = jnp.zeros_like(acc_ref)

    h = h_ref[...]
    gate = _dot(h, wg_ref[...])
    up = _dot(h, wu_ref[...])
    act = (jax.nn.silu(gate) * up).astype(BF16)
    acc_ref[...] += _dot(act, wd_ref[...])

    @pl.when(j == nf - 1)
    def _():
        y = x_ref[...] + 0.5 * acc_ref[...]
        if final:
            y = _rms(y, fg_ref[...])
        o_ref[...] = y


def _ffn(x, gain, wg, wu, wd, final_gain, *, final, tm=1024, tf=256):
    t, d = x.shape
    f = wg.shape[1]
    nf = f // tf
    return pl.pallas_call(
        functools.partial(_ffn_body, nf=nf, final=final),
        out_shape=jax.ShapeDtypeStruct((t, d), F32),
        grid=(t // tm, nf),
        in_specs=[
            pl.BlockSpec((tm, d), lambda i, j: (i, 0)),
            pl.BlockSpec((1, d), lambda i, j: (0, 0)),
            pl.BlockSpec((d, tf), lambda i, j: (0, j)),
            pl.BlockSpec((d, tf), lambda i, j: (0, j)),
            pl.BlockSpec((tf, d), lambda i, j: (j, 0)),
            pl.BlockSpec((1, d), lambda i, j: (0, 0)),
        ],
        out_specs=pl.BlockSpec((tm, d), lambda i, j: (i, 0)),
        scratch_shapes=[pltpu.VMEM((tm, d), BF16), pltpu.VMEM((tm, d), F32)],
        compiler_params=_params(("parallel", "arbitrary")),
        name="ffn_final" if final else "ffn",
    )(x, gain, wg, wu, wd, final_gain)


def _norm_matmul_body(x_ref, g_ref, w_ref, o_ref, h_ref):
    @pl.when(pl.program_id(1) == 0)
    def _():
        h_ref[...] = _rms(x_ref[...], g_ref[...]).astype(BF16)

    o_ref[...] = _dot(h_ref[...], w_ref[...])


def _norm_matmul(x, gain, w, *, tm, tn, name):
    t, d = x.shape
    n = w.shape[1]
    return pl.pallas_call(
        _norm_matmul_body,
        out_shape=jax.ShapeDtypeStruct((t, n), F32),
        grid=(t // tm, n // tn),
        in_specs=[
            pl.BlockSpec((tm, d), lambda i, j: (i, 0)),
            pl.BlockSpec((1, d), lambda i, j: (0, 0)),
            pl.BlockSpec((d, tn), lambda i, j: (0, j)),
        ],
        out_specs=pl.BlockSpec((tm, tn), lambda i, j: (i, j)),
        scratch_shapes=[pltpu.VMEM((tm, d), BF16)],
        compiler_params=_params(("parallel", "arbitrary")),
        name=name,
    )(x, gain, w)


def _rwkv_prep_body(p_ref, prev_ref, mu_ref, wl_ref, w0_ref, a0_ref, kk_ref, ka_ref,
                    r_o, k_o, v_o, kk_o, b_o, lw_o, g_o, *, tm, tiles_per_seq):
    i = pl.program_id(0)
    p = p_ref[...]
    keep = jnp.where(i % tiles_per_seq == 0, 0.0, 1.0)
    prev_last = prev_ref[7:8, :] * keep
    rows = lax.broadcasted_iota(jnp.int32, p.shape, 0)
    shifted = jnp.where(rows == 0, prev_last, pltpu.roll(p, 1, 0))
    x = p + (shifted - p) * mu_ref[...]

    r = x[:, 0:RWKV_DIM]
    k = x[:, RWKV_DIM:2 * RWKV_DIM]
    v = x[:, 2 * RWKV_DIM:3 * RWKV_DIM]
    s = x[:, 3 * RWKV_DIM:RWKV_PROJ]
    lane = lax.broadcasted_iota(jnp.int32, s.shape, 1)
    z = jnp.where(lane < DECAY_LORA, jnp.tanh(s),
                  jnp.where(lane < DECAY_LORA + AAA_LORA, s, jax.nn.sigmoid(s)))
    lo = _dot(z.astype(BF16), wl_ref[...])
    w = -jax.nn.softplus(-(w0_ref[...] + lo[:, 0:RWKV_DIM])) - 0.5
    a = jax.nn.sigmoid(a0_ref[...] + lo[:, RWKV_DIM:2 * RWKV_DIM])
    g = lo[:, 2 * RWKV_DIM:3 * RWKV_DIM]

    kkr = k * kk_ref[...]
    hi = lax.broadcasted_iota(jnp.int32, (RWKV_DIM, RWKV_DIM), 0) // RWKV_HEAD_DIM
    hj = lax.broadcasted_iota(jnp.int32, (RWKV_DIM, RWKV_DIM), 1) // RWKV_HEAD_DIM
    same_head = (hi == hj).astype(F32)
    ssq = _dot(kkr * kkr, same_head, HI)
    kk = kkr / jnp.maximum(jnp.sqrt(ssq), 1e-12)

    r_o[...] = r
    k_o[...] = k * (1.0 + (a - 1.0) * ka_ref[...])
    v_o[...] = v
    kk_o[...] = kk
    b_o[...] = kk * a
    lw_o[...] = -jnp.exp(w)
    g_o[...] = g


def _rwkv_prep(proj, seq, mu, w_lora, w0, a0, k_k, k_a, *, tm=512):
    t = proj.shape[0]
    tiles_per_seq = seq // tm
    row = lambda i: (i, 0)
    const = lambda i: (0, 0)
    out = jax.ShapeDtypeStruct((t, RWKV_DIM), F32)
    return pl.pallas_call(
        functools.partial(_rwkv_prep_body, tm=tm, tiles_per_seq=tiles_per_seq),
        out_shape=[out] * 7,
        grid=(t // tm,),
        in_specs=[
            pl.BlockSpec((tm, RWKV_PROJ), row),
            pl.BlockSpec((8, RWKV_PROJ), lambda i: (jnp.maximum(i * (tm // 8) - 1, 0), 0)),
            pl.BlockSpec((1, RWKV_PROJ), const),
            pl.BlockSpec((LORA_DIM, 3 * RWKV_DIM), const),
            pl.BlockSpec((1, RWKV_DIM), const),
            pl.BlockSpec((1, RWKV_DIM), const),
            pl.BlockSpec((1, RWKV_DIM), const),
            pl.BlockSpec((1, RWKV_DIM), const),
        ],
        out_specs=[pl.BlockSpec((tm, RWKV_DIM), row)] * 7,
        compiler_params=_params(("parallel",)),
        name="rwkv_prep",
    )(proj, proj, mu, w_lora, w0, a0, k_k, k_a)


def _rwkv_scan_body(r_ref, k_ref, v_ref, kk_ref, b_ref, lw_ref, g_ref, rk_ref, gg_ref, gb_ref,
                    o_ref, st_ref):
    c_sz, n = RWKV_CHUNK, RWKV_HEAD_DIM

    @pl.when(pl.program_id(1) == 0)
    def _():
        st_ref[...] = jnp.zeros_like(st_ref)

    ri = lax.broadcasted_iota(jnp.int32, (c_sz, c_sz), 0)
    ci = lax.broadcasted_iota(jnp.int32, (c_sz, c_sz), 1)
    strict = ci < ri
    incl = ci <= ri
    eye = (ci == ri).astype(F32)
    eye_b = eye.astype(BF16)

    lw = lw_ref[...]
    cum = _dot(incl.astype(F32), lw, HI)
    cum_last = cum[c_sz - 1:c_sz, :]
    r = r_ref[...]
    k = k_ref[...]
    v = v_ref[...]
    b = b_ref[...]
    p_inv = jnp.exp(-cum)
    p_end = jnp.exp(cum_last - cum)
    a_t = -(kk_ref[...] * jnp.exp(cum - lw))
    r_t = r * jnp.exp(cum)
    b_t = b * p_inv
    k_t = k * p_inv
    b_h = b * p_end
    k_h = k * p_end
    d_p = jnp.exp(cum_last)
    rk_all = r * k * rk_ref[...]

    outs = []
    for h in range(RWKV_HEADS):
        sl = slice(h * n, (h + 1) * n)
        at_b = a_t[:, sl].astype(BF16)
        rt_b = r_t[:, sl].astype(BF16)
        left = jnp.concatenate([at_b, rt_b], axis=0)
        a_l = _dot_nt(left, b_t[:, sl].astype(BF16))
        a_r = _dot_nt(left, k_t[:, sl].astype(BF16))
        a_ab = jnp.where(strict, a_l[:c_sz], 0.0)
        a_rb = jnp.where(incl, a_l[c_sz:], 0.0)
        a_ak = jnp.where(strict, a_r[:c_sz], 0.0)
        a_rk = jnp.where(incl, a_r[c_sz:], 0.0)

        m = _dot(a_ab, a_ab, HI)
        t_inv = eye + a_ab
        for step in range(5):
            t_inv = t_inv + _dot(m, t_inv, HI)
            if step < 4:
                m = _dot(m, m, HI)

        s0 = st_ref[h]
        s0_b = s0.astype(BF16)
        v_h = v[:, sl]
        v_b = v_h.astype(BF16)
        rhs = _dot_nt(at_b, s0_b) + _dot(a_ak.astype(BF16), v_b)
        u = _dot(t_inv, rhs, HI)
        u_b = u.astype(BF16)
        y = _dot_nt(rt_b, s0_b) + _dot(a_rb.astype(BF16), u_b) + _dot(a_rk.astype(BF16), v_b)
        u_t = _dot_nt(eye_b, u_b).astype(BF16)
        v_t = _dot_nt(eye_b, v_b).astype(BF16)
        st_ref[h] = (s0 * d_p[:, sl] + _dot(u_t, b_h[:, sl].astype(BF16))
                     + _dot(v_t, k_h[:, sl].astype(BF16)))

        mean = jnp.mean(y, axis=-1, keepdims=True)
        var = jnp.mean(jnp.square(y - mean), axis=-1, keepdims=True)
        yn = (y - mean) * lax.rsqrt(var + RWKV_GN_EPS)
        yn = yn * gg_ref[:, sl] + gb_ref[:, sl]
        bonus = jnp.sum(rk_all[:, sl], axis=-1, keepdims=True) * v_h
        outs.append((yn + bonus) * g_ref[:, sl])
    o_ref[...] = jnp.concatenate(outs, axis=1)


def _rwkv_scan(r, k, v, kk, b, lw, g, r_k, gn_gain, gn_bias, *, batch, seq):
    t = r.shape[0]
    nc = seq // RWKV_CHUNK
    row = lambda bi, c: (bi * nc + c, 0)
    const = lambda bi, c: (0, 0)
    tok = pl.BlockSpec((RWKV_CHUNK, RWKV_DIM), row)
    par = pl.BlockSpec((1, RWKV_DIM), const)
    return pl.pallas_call(
        _rwkv_scan_body,
        out_shape=jax.ShapeDtypeStruct((t, RWKV_DIM), F32),
        grid=(batch, nc),
        in_specs=[tok] * 7 + [par] * 3,
        out_specs=tok,
        scratch_shapes=[pltpu.VMEM((RWKV_HEADS, RWKV_HEAD_DIM, RWKV_HEAD_DIM), F32)],
        compiler_params=_params(("parallel", "arbitrary")),
        name="rwkv_scan",
    )(r, k, v, kk, b, lw, g, r_k, gn_gain, gn_bias)


def _compress_body(x_ref, w1_ref, pe_ref, w2_ref, o_ref):
    half = CMP_STRIDE * NSA_HEAD_DIM
    x = x_ref[0, 0].astype(BF16)
    first = _dot(x, w1_ref[0, 0:half, :])
    second = _dot(x, w1_ref[0, half:2 * half, :])
    n_rows = x.shape[0]
    second_next = pltpu.roll(second, n_rows - 1, 0)
    pe = jnp.broadcast_to(pe_ref[0], (8, 2 * half)).astype(BF16)
    pe_term = _dot(pe, w1_ref[0])[0:1, :]
    hid = first + second_next + pe_term
    o_ref[0, 0] = _dot(jax.nn.gelu(hid).astype(BF16), w2_ref[0])


def _compress(x, w1, pe, w2):
    _, bg, rows, width = x.shape
    return pl.pallas_call(
        _compress_body,
        out_shape=jax.ShapeDtypeStruct((2, bg, rows, NSA_HEAD_DIM), F32),
        grid=(2, bg),
        in_specs=[
            pl.BlockSpec((1, 1, rows, width), lambda s, i: (s, i, 0, 0)),
            pl.BlockSpec((1, 2 * width, CMP_HIDDEN), lambda s, i: (s, 0, 0)),
            pl.BlockSpec((1, 1, 2 * width), lambda s, i: (s, 0, 0)),
            pl.BlockSpec((1, CMP_HIDDEN, NSA_HEAD_DIM), lambda s, i: (s, 0, 0)),
        ],
        out_specs=pl.BlockSpec((1, 1, rows, NSA_HEAD_DIM), lambda s, i: (s, i, 0, 0)),
        compiler_params=_params(("parallel", "parallel")),
        name="nsa_compress",
    )(x, w1, pe, w2)


def _t5_bucket(dist):
    n = jnp.maximum(dist, 0)
    exact = REL_BUCKETS // 2
    nf = jnp.maximum(n, 1).astype(F32)
    large = exact + (jnp.log(nf / exact) / math.log(REL_MAX_DIST / exact)
                     * (REL_BUCKETS - exact)).astype(jnp.int32)
    large = jnp.minimum(large, REL_BUCKETS - 1)
    return jnp.where(n < exact, n, large)


def _bias_body(tab_ref, bc_ref, bt_ref, *, seq):
    h = pl.program_id(0)
    tq = NSA_TQ

    def lookup(dist):
        bucket = _t5_bucket(dist)
        out = jnp.zeros(dist.shape, F32)
        for bkt in range(REL_BUCKETS):
            out = jnp.where(bucket == bkt, tab_ref[bkt, h], out)
        return out

    rows = lax.broadcasted_iota(jnp.int32, (tq, tq), 0)
    cols = lax.broadcasted_iota(jnp.int32, (tq, tq), 1)
    for delta in range(3):
        bt_ref[0, delta, 0] = lookup(delta * tq + rows - cols)

    def cmp_tile(i, carry):
        t_pos = i * tq + rows
        cmp_end = cols * CMP_STRIDE + CMP_LEN - 1
        bc_ref[0, pl.ds(pl.multiple_of(i * tq, tq), tq), :] = lookup(t_pos - cmp_end)
        return carry

    lax.fori_loop(0, seq // tq, cmp_tile, 0)


def _bias_tables(rel_bias, seq):
    g, hpg, tq = NSA_KV_GROUPS, NSA_HPG, NSA_TQ
    return pl.pallas_call(
        functools.partial(_bias_body, seq=seq),
        out_shape=[jax.ShapeDtypeStruct((NSA_HEADS, seq, tq), F32),
                   jax.ShapeDtypeStruct((g, 3, hpg, tq, tq), F32)],
        grid=(NSA_HEADS,),
        in_specs=[pl.BlockSpec(memory_space=pltpu.SMEM)],
        out_specs=[pl.BlockSpec((1, seq, tq), lambda h: (h, 0, 0)),
                   pl.BlockSpec((1, 3, 1, tq, tq), lambda h: (h // hpg, 0, h % hpg, 0, 0))],
        compiler_params=_params(("parallel",)),
        name="nsa_bias",
    )(rel_bias)


def _nsa_body(q_ref, kc_ref, vc_ref, ks_ref, vs_ref, kw_ref, vw_ref, bc_ref, bt_ref, gate_ref,
              o_ref):
    tq, hpg, dh = NSA_TQ, NSA_HPG, NSA_HEAD_DIM
    rws = hpg * tq
    n_blk_log2 = SEL_BLOCK.bit_length() - 1
    i = pl.program_id(2)
    neg = -1e30

    q = (q_ref[0, 0, 0] * (dh ** -0.5)).astype(BF16)
    rowi = lax.broadcasted_iota(jnp.int32, (rws, tq), 0)
    lane = lax.broadcasted_iota(jnp.int32, (rws, tq), 1)
    t_row = i * tq + (rowi & (tq - 1))

    kc = kc_ref[0, 0].astype(BF16)
    vc = vc_ref[0, 0].astype(BF16)
    s = _dot_nt(q, kc) + bc_ref[...].reshape(rws, tq)
    n_cmp = kc.shape[0] - 1
    valid = (t_row - (lane * CMP_STRIDE + CMP_LEN - 1) >= 0) & (lane < n_cmp)
    s = jnp.where(valid, s, neg)
    e = jnp.where(valid, jnp.exp(s - jnp.max(s, axis=-1, keepdims=True)), 0.0)
    den = jnp.sum(e, axis=-1, keepdims=True)
    p_c = e / jnp.where(den > 0.0, den, 1.0)
    o_cmp = _dot(p_c.astype(BF16), vc)

    p_sum = p_c[0:tq]
    for hh in range(1, hpg):
        p_sum = p_sum + p_c[hh * tq:(hh + 1) * tq]
    n_blk = 32
    jj = lax.broadcasted_iota(jnp.int32, (n_blk, tq), 0)
    nn = lax.broadcasted_iota(jnp.int32, (n_blk, tq), 1)
    overlap_t = ((nn * CMP_STRIDE <= jj * SEL_BLOCK + SEL_BLOCK - 1)
                 & (nn * CMP_STRIDE + CMP_LEN - 1 >= jj * SEL_BLOCK)).astype(F32)
    imp = _dot_nt(overlap_t, p_sum, HI)
    cur = (i * tq + nn) >> n_blk_log2
    forced = (jj == 0) | (jj == cur) | (jj == cur - 1)
    imp = jnp.where(jj > cur, -1e6, jnp.where(forced, 1e6, imp))
    rank = jnp.zeros((n_blk, tq), jnp.int32)
    for a in range(n_blk):
        row = imp[a:a + 1, :]
        beats = (row > imp) | ((row == imp) & (a < jj))
        rank = rank + beats.astype(jnp.int32)
    sel = jnp.where(rank < SEL_TOP_N, 1.0, 0.0).astype(BF16)
    eye_b = (lax.broadcasted_iota(jnp.int32, (tq, tq), 0)
             == lax.broadcasted_iota(jnp.int32, (tq, tq), 1)).astype(BF16)
    sel_t = _dot_nt(eye_b, sel).astype(BF16)
    sel_rows = jnp.concatenate([sel_t] * hpg, axis=0)

    def attend(kt, bias, k_ref, v_ref, mask, carry):
        m_prev, l_prev, acc = carry
        off = pl.multiple_of(kt * tq, tq)
        kb = k_ref[0, 0, pl.ds(off, tq), :]
        vb = v_ref[0, 0, pl.ds(off, tq), :]
        sc = jnp.where(mask, _dot_nt(q, kb) + bias, neg)
        m_new = jnp.maximum(m_prev, jnp.max(sc, axis=-1, keepdims=True))
        alpha = jnp.exp(m_prev - m_new)
        pr = jnp.where(mask, jnp.exp(sc - m_new), 0.0)
        l_new = alpha * l_prev + jnp.sum(pr, axis=-1, keepdims=True)
        acc = alpha * acc + _dot(pr.astype(BF16), vb)
        return m_new, l_new, acc

    init = (jnp.full((rws, 1), neg, F32), jnp.zeros((rws, 1), F32), jnp.zeros((rws, dh), F32))

    blk_id = lax.broadcasted_iota(jnp.int32, (n_blk, tq), 0)
    blk_lane = lax.broadcasted_iota(jnp.int32, (n_blk, tq), 1)

    def sel_step(kt, carry):
        expand = jnp.where(((kt * tq + blk_lane) >> n_blk_log2) == blk_id, 1.0, 0.0).astype(BF16)
        chosen = _dot(sel_rows, expand) > 0.5
        mask = chosen & (kt * tq + lane <= t_row)
        bias = bt_ref[0, jnp.minimum(i - kt, 2)].reshape(rws, tq)
        return attend(kt, bias, ks_ref, vs_ref, mask, carry)

    _, l_s, acc_s = lax.fori_loop(0, i + 1, sel_step, init)
    o_sel = acc_s / l_s

    carry = init
    for delta in range(WINDOW // tq + 1):
        kt = jnp.maximum(i - delta, 0)
        dist = t_row - (kt * tq + lane) + jnp.where(i - delta >= 0, 0, 2 * WINDOW)
        mask = (dist >= 0) & (dist < WINDOW)
        bias = bt_ref[0, min(delta, 2)].reshape(rws, tq)
        carry = attend(kt, bias, kw_ref, vw_ref, mask, carry)
    o_win = carry[2] / carry[1]

    gates = jax.nn.sigmoid(gate_ref[0, 0, 0])
    o_ref[0, 0, 0] = gates[:, 0:1] * o_cmp + gates[:, 1:2] * o_sel + gates[:, 2:3] * o_win


def _nsa_attention(q, kvc, k_sel, v_sel, k_win, v_win, bias_c, bias_t, gates):
    bsz, g, nq, rws, dh = q.shape
    seq = k_sel.shape[2]
    tq, hpg = NSA_TQ, NSA_HPG
    n_rows = kvc.shape[2]
    kv_spec = pl.BlockSpec((1, 1, seq, dh), lambda b, gi, i: (b, gi, 0, 0))
    return pl.pallas_call(
        _nsa_body,
        out_shape=jax.ShapeDtypeStruct((bsz, g, nq, rws, dh), F32),
        grid=(bsz, g, nq),
        in_specs=[
            pl.BlockSpec((1, 1, 1, rws, dh), lambda b, gi, i: (b, gi, i, 0, 0)),
            pl.BlockSpec((1, 1, n_rows, dh), lambda b, gi, i: (0, b * g + gi, 0, 0)),
            pl.BlockSpec((1, 1, n_rows, dh), lambda b, gi, i: (1, b * g + gi, 0, 0)),
            kv_spec, kv_spec, kv_spec, kv_spec,
            pl.BlockSpec((hpg, tq, tq), lambda b, gi, i: (gi, i, 0)),
            pl.BlockSpec((1, 3, hpg, tq, tq), lambda b, gi, i: (gi, 0, 0, 0, 0)),
            pl.BlockSpec((1, 1, 1, rws, 3), lambda b, gi, i: (b, gi, i, 0, 0)),
        ],
        out_specs=pl.BlockSpec((1, 1, 1, rws, dh), lambda b, gi, i: (b, gi, i, 0, 0)),
        compiler_params=_params(("parallel", "parallel", "arbitrary")),
        name="nsa_attention",
    )(q, kvc, kvc, k_sel, v_sel, k_win, v_win, bias_c, bias_t, gates)


def _mem_body(q_ref, kv_ref, o_ref):
    outs = []
    for h in range(MEM_HEADS):
        sl = slice(h * MEM_HEAD_DIM, (h + 1) * MEM_HEAD_DIM)
        qh = (q_ref[:, sl] * (MEM_HEAD_DIM ** -0.5)).astype(BF16)
        kh = kv_ref[0, :, sl].astype(BF16)
        vh = kv_ref[0, :, MEM_DIM + h * MEM_HEAD_DIM:MEM_DIM + (h + 1) * MEM_HEAD_DIM].astype(BF16)
        s = _dot_nt(qh, kh)
        e = jnp.exp(s - jnp.max(s, axis=-1, keepdims=True))
        p = e / jnp.sum(e, axis=-1, keepdims=True)
        outs.append(_dot(p.astype(BF16), vh))
    o_ref[...] = jnp.concatenate(outs, axis=1)


def _mem_attention(proj, kv, *, seq, tq=512):
    t = proj.shape[0]
    bsz, m, _ = kv.shape
    per_seq = seq // tq
    return pl.pallas_call(
        _mem_body,
        out_shape=jax.ShapeDtypeStruct((t, MEM_DIM), F32),
        grid=(t // tq,),
        in_specs=[
            pl.BlockSpec((tq, MEM_DIM), lambda i: (i, COL_QMEM // MEM_DIM)),
            pl.BlockSpec((1, m, 2 * MEM_DIM), lambda i: (i // per_seq, 0, 0)),
        ],
        out_specs=pl.BlockSpec((tq, MEM_DIM), lambda i: (i, 0)),
        compiler_params=_params(("parallel",)),
        name="mem_attention",
    )(proj, kv)


def _merge_body(x_ref, yr_ref, yn_ref, ym_ref, g0_ref, g1_ref, g2_ref, wr_ref, wn_ref, wm_ref,
                wo_ref, o_ref):
    merged = (jax.nn.sigmoid(g0_ref[...]) * _dot(yr_ref[...].astype(BF16), wr_ref[...])
              + jax.nn.sigmoid(g1_ref[...]) * _dot(yn_ref[...].astype(BF16), wn_ref[...])
              + jax.nn.sigmoid(g2_ref[...]) * _dot(ym_ref[...].astype(BF16), wm_ref[...]))
    o_ref[...] = x_ref[...] + _dot(merged.astype(BF16), wo_ref[...])


def _merge(x, y_rwkv, y_nsa, y_mem, proj, w_r, w_n, w_m, w_o, *, tm=512):
    t, d = x.shape
    row = lambda i: (i, 0)
    const = lambda i: (0, 0)
    gate0 = COL_GBR // d
    return pl.pallas_call(
        _merge_body,
        out_shape=jax.ShapeDtypeStruct((t, d), F32),
        grid=(t // tm,),
        in_specs=[
            pl.BlockSpec((tm, d), row),
            pl.BlockSpec((tm, RWKV_DIM), row),
            pl.BlockSpec((tm, NSA_DIM), row),
            pl.BlockSpec((tm, MEM_DIM), row),
            pl.BlockSpec((tm, d), lambda i: (i, gate0)),
            pl.BlockSpec((tm, d), lambda i: (i, gate0 + 1)),
            pl.BlockSpec((tm, d), lambda i: (i, gate0 + 2)),
            pl.BlockSpec((RWKV_DIM, d), const),
            pl.BlockSpec((NSA_DIM, d), const),
            pl.BlockSpec((MEM_DIM, d), const),
            pl.BlockSpec((d, d), const),
        ],
        out_specs=pl.BlockSpec((tm, d), row),
        compiler_params=_params(("parallel",)),
        name="merge",
    )(x, y_rwkv, y_nsa, y_mem, proj, proj, proj, w_r, w_n, w_m, w_o)


def _layer(x, mem, l, bias_c, bias_t, p):
    bsz, seq, d = x.shape
    t = bsz * seq
    g, hpg, dh, tq = NSA_KV_GROUPS, NSA_HPG, NSA_HEAD_DIM, NSA_TQ
    nq = seq // tq
    row = lambda a: a.reshape(1, -1)
    x = x.reshape(t, d)

    x = _ffn(x, row(p['ffn1_norm'][l]), p['ffn1_w_gate'][l].astype(BF16),
             p['ffn1_w_up'][l].astype(BF16), p['ffn1_w_down'][l].astype(BF16),
             row(p['final_norm']), final=False)

    w_in = p['w_in'][l]
    o = 0
    parts = {}
    for name, size in (('rwkv', RWKV_PROJ), ('q', NSA_DIM), ('kv', 6 * NSA_KV_DIM),
                       ('g_nsa', 3 * NSA_HEADS), ('q_mem', MEM_DIM), ('g_br', N_BRANCH * d)):
        parts[name] = w_in[:, o:o + size]
        o += size
    pad = jnp.zeros((d, PROJ_COLS - COL_GNSA - 3 * NSA_HEADS), w_in.dtype)
    w_perm = jnp.concatenate([parts['rwkv'], parts['q'], parts['kv'], parts['g_br'],
                              parts['q_mem'], parts['g_nsa'], pad], axis=1).astype(BF16)
    proj = _norm_matmul(x, row(p['mix_norm'][l]), w_perm, tm=1024, tn=PROJ_TN, name="in_proj")

    w_lora = jnp.zeros((LORA_DIM, 3 * RWKV_DIM), F32)
    w_lora = w_lora.at[0:DECAY_LORA, 0:RWKV_DIM].set(p['rwkv_w2'][l])
    w_lora = w_lora.at[DECAY_LORA:DECAY_LORA + AAA_LORA, RWKV_DIM:2 * RWKV_DIM].set(p['rwkv_a2'][l])
    w_lora = w_lora.at[DECAY_LORA + AAA_LORA:, 2 * RWKV_DIM:].set(p['rwkv_g2'][l])
    r, k, v, kk, b, lw, gate = _rwkv_prep(
        proj, seq, row(p['rwkv_mu'][l]), w_lora.astype(BF16), row(p['rwkv_w0'][l]),
        row(p['rwkv_a0'][l]), row(p['rwkv_k_k'][l]), row(p['rwkv_k_a'][l]))
    y_rwkv = _rwkv_scan(r, k, v, kk, b, lw, gate, row(p['rwkv_r_k'][l]),
                        row(p['rwkv_gn_gain'][l]), row(p['rwkv_gn_bias'][l]), batch=bsz, seq=seq)

    kv6 = proj[:, COL_KV:COL_KV + 6 * NSA_KV_DIM].reshape(bsz, seq, 6, g, dh)
    kv6 = kv6.transpose(2, 0, 3, 1, 4)
    cmp_in = kv6[0:2].reshape(2, bsz * g, seq // CMP_STRIDE, CMP_STRIDE * dh)
    w1 = jnp.stack([p['cmp_k_w1'][l], p['cmp_v_w1'][l]]).astype(BF16)
    pe = jnp.stack([p['cmp_pe_k'][l].reshape(1, -1), p['cmp_pe_v'][l].reshape(1, -1)])
    w2 = jnp.stack([p['cmp_k_w2'][l], p['cmp_v_w2'][l]]).astype(BF16)
    kvc = _compress(cmp_in, w1, pe, w2)
    kv_b = kv6[2:6].astype(BF16)
    q_nsa = proj[:, COL_QNSA:COL_QNSA + NSA_DIM].reshape(bsz, nq, tq, g, hpg, dh)
    q_nsa = q_nsa.transpose(0, 3, 1, 4, 2, 5).reshape(bsz, g, nq, hpg * tq, dh)
    g_nsa = proj[:, COL_GNSA:COL_GNSA + 3 * NSA_HEADS].reshape(bsz, nq, tq, 3, g, hpg)
    g_nsa = g_nsa.transpose(0, 4, 1, 5, 2, 3).reshape(bsz, g, nq, hpg * tq, 3)
    o_nsa = _nsa_attention(q_nsa, kvc, kv_b[0], kv_b[1], kv_b[2], kv_b[3], bias_c, bias_t, g_nsa)
    y_nsa = o_nsa.reshape(bsz, g, nq, hpg, tq, dh).transpose(0, 2, 4, 1, 3, 5).reshape(t, NSA_DIM)

    w_kv = jnp.concatenate([p['mem_w_k'][l], p['mem_w_v'][l]], axis=1).astype(BF16)
    m_tok = mem.shape[1]
    kv_mem = _norm_matmul(mem.reshape(bsz * m_tok, d), row(p['mem_norm'][l]), w_kv,
                          tm=min(1024, bsz * m_tok), tn=512, name="mem_kv")
    y_mem = _mem_attention(proj, kv_mem.reshape(bsz, m_tok, 2 * MEM_DIM), seq=seq)

    x = _merge(x, y_rwkv, y_nsa, y_mem, proj, p['w_br_rwkv'][l].astype(BF16),
               p['w_br_nsa'][l].astype(BF16), p['w_br_mem'][l].astype(BF16),
               p['w_out'][l].astype(BF16))

    last = l == p['ffn1_norm'].shape[0] - 1
    x = _ffn(x, row(p['ffn2_norm'][l]), p['ffn2_w_gate'][l].astype(BF16),
             p['ffn2_w_up'][l].astype(BF16), p['ffn2_w_down'][l].astype(BF16),
             row(p['final_norm']), final=last)
    return x.reshape(bsz, seq, d)


def kernel(x, mem, ffn1_norm, ffn1_w_gate, ffn1_w_up, ffn1_w_down, mix_norm, w_in, rwkv_mu, rwkv_w0, rwkv_w2, rwkv_a0, rwkv_a2, rwkv_g2, rwkv_k_k, rwkv_k_a, rwkv_r_k, rwkv_gn_gain, rwkv_gn_bias, cmp_pe_k, cmp_k_w1, cmp_k_w2, cmp_pe_v, cmp_v_w1, cmp_v_w2, rel_bias, mem_norm, mem_w_k, mem_w_v, w_br_rwkv, w_br_nsa, w_br_mem, w_out, ffn2_norm, ffn2_w_gate, ffn2_w_up, ffn2_w_down, final_norm):
    p = dict(ffn1_norm=ffn1_norm, ffn1_w_gate=ffn1_w_gate, ffn1_w_up=ffn1_w_up,
             ffn1_w_down=ffn1_w_down, mix_norm=mix_norm, w_in=w_in, rwkv_mu=rwkv_mu,
             rwkv_w0=rwkv_w0, rwkv_w2=rwkv_w2, rwkv_a0=rwkv_a0, rwkv_a2=rwkv_a2, rwkv_g2=rwkv_g2,
             rwkv_k_k=rwkv_k_k, rwkv_k_a=rwkv_k_a, rwkv_r_k=rwkv_r_k, rwkv_gn_gain=rwkv_gn_gain,
             rwkv_gn_bias=rwkv_gn_bias, cmp_pe_k=cmp_pe_k, cmp_k_w1=cmp_k_w1, cmp_k_w2=cmp_k_w2,
             cmp_pe_v=cmp_pe_v, cmp_v_w1=cmp_v_w1, cmp_v_w2=cmp_v_w2, mem_norm=mem_norm,
             mem_w_k=mem_w_k, mem_w_v=mem_w_v, w_br_rwkv=w_br_rwkv, w_br_nsa=w_br_nsa,
             w_br_mem=w_br_mem, w_out=w_out, ffn2_norm=ffn2_norm, ffn2_w_gate=ffn2_w_gate,
             ffn2_w_up=ffn2_w_up, ffn2_w_down=ffn2_w_down, final_norm=final_norm)
    bias_c, bias_t = _bias_tables(rel_bias, x.shape[1])
    for l in range(ffn1_norm.shape[0]):
        x = _layer(x, mem, l, bias_c, bias_t, p)
    return x
```

```python
import functools
import math

import jax
import jax.numpy as jnp
from jax import lax
from jax.experimental import pallas as pl
from jax.experimental.pallas import tpu as pltpu

F32 = jnp.float32
BF16 = jnp.bfloat16
HI = lax.Precision.HIGHEST

D_MODEL = 1024
NORM_EPS = 1e-6
D_FF = 2816
RWKV_HEADS = 8
RWKV_HEAD_DIM = 64
RWKV_DIM = RWKV_HEADS * RWKV_HEAD_DIM
DECAY_LORA = 64
AAA_LORA = 64
GATE_LORA = 128
LORA_DIM = DECAY_LORA + AAA_LORA + GATE_LORA
RWKV_GN_EPS = 64e-5
RWKV_PROJ = 3 * RWKV_DIM + LORA_DIM
NSA_HEADS = 8
NSA_KV_GROUPS = 2
NSA_HPG = NSA_HEADS // NSA_KV_GROUPS
NSA_HEAD_DIM = 64
NSA_DIM = NSA_HEADS * NSA_HEAD_DIM
NSA_KV_DIM = NSA_KV_GROUPS * NSA_HEAD_DIM
CMP_LEN = 32
CMP_STRIDE = 16
CMP_HIDDEN = 256
SEL_BLOCK = 64
SEL_TOP_N = 16
WINDOW = 512
REL_BUCKETS = 32
REL_MAX_DIST = 128
MEM_HEADS = 4
MEM_HEAD_DIM = 128
MEM_DIM = MEM_HEADS * MEM_HEAD_DIM
N_BRANCH = 3

COL_RWKV = 0
COL_QNSA = COL_RWKV + RWKV_PROJ
COL_KV = COL_QNSA + NSA_DIM
COL_GBR = COL_KV + 6 * NSA_KV_DIM
COL_QMEM = COL_GBR + N_BRANCH * D_MODEL
COL_GNSA = COL_QMEM + MEM_DIM
PROJ_TN = 768
PROJ_COLS = 9 * PROJ_TN

RWKV_CHUNK = 64
NSA_TQ = 256
NSA_KW = 128
MASKED = -1e30
BT_DIAG, BT_PREV, BT_FAR, BT_WIN_EDGE, BT_NONE, BT_COUNT = 0, 1, 2, 3, 4, 5
VMEM_LIMIT = 56 * 1024 * 1024


def _dot(a, b, precision=None):
    return jnp.dot(a, b, preferred_element_type=F32, precision=precision)


def _dot_nt(a, b, precision=None):
    return lax.dot_general(a, b, (((1,), (1,)), ((), ())), preferred_element_type=F32,
                           precision=precision)


def _params(semantics):
    return pltpu.CompilerParams(dimension_semantics=semantics, vmem_limit_bytes=VMEM_LIMIT)


def _rms(x, g):
    return x * lax.rsqrt(jnp.mean(x * x, axis=-1, keepdims=True) + NORM_EPS) * g


def _ffn_body(x_ref, g_ref, wg_ref, wu_ref, wd_ref, fg_ref, o_ref, h_ref, acc_ref, *, nf, final):
    j = pl.program_id(1)

    @pl.when(j == 0)
    def _():
        h_ref[...] = _rms(x_ref[...], g_ref[...]).astype(BF16)
        acc_ref[...] = jnp.zeros_like(acc_ref)

    h = h_ref[...]
    gate = _dot(h, wg_ref[...])
    up = _dot(h, wu_ref[...])
    act = (jax.nn.silu(gate) * up).astype(BF16)
    acc_ref[...] += _dot(act, wd_ref[...])

    @pl.when(j == nf - 1)
    def _():
        y = x_ref[...] + 0.5 * acc_ref[...]
        if final:
            y = _rms(y, fg_ref[...])
        o_ref[...] = y


def _ffn(x, gain, wg, wu, wd, final_gain, *, final, tm=1024, tf=256):
    t, d = x.shape
    f = wg.shape[1]
    nf = f // tf
    return pl.pallas_call(
        functools.partial(_ffn_body, nf=nf, final=final),
        out_shape=jax.ShapeDtypeStruct((t, d), F32),
        grid=(t // tm, nf),
        in_specs=[
            pl.BlockSpec((tm, d), lambda i, j: (i, 0)),
            pl.BlockSpec((1, d), lambda i, j: (0, 0)),
            pl.BlockSpec((d, tf), lambda i, j: (0, j)),
            pl.BlockSpec((d, tf), lambda i, j: (0, j)),
            pl.BlockSpec((tf, d), lambda i, j: (j, 0)),
            pl.BlockSpec((1, d), lambda i, j: (0, 0)),
        ],
        out_specs=pl.BlockSpec((tm, d), lambda i, j: (i, 0)),
        scratch_shapes=[pltpu.VMEM((tm, d), BF16), pltpu.VMEM((tm, d), F32)],
        compiler_params=_params(("parallel", "arbitrary")),
        name="ffn_final" if final else "ffn",
    )(x, gain, wg, wu, wd, final_gain)


def _norm_matmul_body(x_ref, g_ref, w_ref, o_ref, h_ref):
    @pl.when(pl.program_id(1) == 0)
    def _():
        h_ref[...] = _rms(x_ref[...], g_ref[...]).astype(BF16)

    o_ref[...] = _dot(h_ref[...], w_ref[...])


def _norm_matmul(x, gain, w, *, tm, tn, name):
    t, d = x.shape
    n = w.shape[1]
    return pl.pallas_call(
        _norm_matmul_body,
        out_shape=jax.ShapeDtypeStruct((t, n), F32),
        grid=(t // tm, n // tn),
        in_specs=[
            pl.BlockSpec((tm, d), lambda i, j: (i, 0)),
            pl.BlockSpec((1, d), lambda i, j: (0, 0)),
            pl.BlockSpec((d, tn), lambda i, j: (0, j)),
        ],
        out_specs=pl.BlockSpec((tm, tn), lambda i, j: (i, j)),
        scratch_shapes=[pltpu.VMEM((tm, d), BF16)],
        compiler_params=_params(("parallel", "arbitrary")),
        name=name,
    )(x, gain, w)


def _rwkv_prep_body(p_ref, prev_ref, mu_ref, wl_ref, w0_ref, a0_ref, kk_ref, ka_ref,
                    r_o, k_o, v_o, kk_o, b_o, lw_o, g_o, *, tm, tiles_per_seq):
    i = pl.program_id(0)
    p = p_ref[...]
    keep = jnp.where(i % tiles_per_seq == 0, 0.0, 1.0)
    prev_last = prev_ref[7:8, :] * keep
    rows = lax.broadcasted_iota(jnp.int32, p.shape, 0)
    shifted = jnp.where(rows == 0, prev_last, pltpu.roll(p, 1, 0))
    x = p + (shifted - p) * mu_ref[...]

    r = x[:, 0:RWKV_DIM]
    k = x[:, RWKV_DIM:2 * RWKV_DIM]
    v = x[:, 2 * RWKV_DIM:3 * RWKV_DIM]
    s = x[:, 3 * RWKV_DIM:RWKV_PROJ]
    lane = lax.broadcasted_iota(jnp.int32, s.shape, 1)
    z = jnp.where(lane < DECAY_LORA, jnp.tanh(s),
                  jnp.where(lane < DECAY_LORA + AAA_LORA, s, jax.nn.sigmoid(s)))
    lo = _dot(z.astype(BF16), wl_ref[...])
    w = -jax.nn.softplus(-(w0_ref[...] + lo[:, 0:RWKV_DIM])) - 0.5
    a = jax.nn.sigmoid(a0_ref[...] + lo[:, RWKV_DIM:2 * RWKV_DIM])
    g = lo[:, 2 * RWKV_DIM:3 * RWKV_DIM]

    kkr = k * kk_ref[...]
    hi = lax.broadcasted_iota(jnp.int32, (RWKV_DIM, RWKV_DIM), 0) // RWKV_HEAD_DIM
    hj = lax.broadcasted_iota(jnp.int32, (RWKV_DIM, RWKV_DIM), 1) // RWKV_HEAD_DIM
    same_head = (hi == hj).astype(F32)
    ssq = _dot(kkr * kkr, same_head, HI)
    kk = kkr / jnp.maximum(jnp.sqrt(ssq), 1e-12)

    r_o[...] = r
    k_o[...] = k * (1.0 + (a - 1.0) * ka_ref[...])
    v_o[...] = v
    kk_o[...] = kk
    b_o[...] = kk * a
    lw_o[...] = -jnp.exp(w)
    g_o[...] = g


def _rwkv_prep(proj, seq, mu, w_lora, w0, a0, k_k, k_a, *, tm=512):
    t = proj.shape[0]
    tiles_per_seq = seq // tm
    row = lambda i: (i, 0)
    const = lambda i: (0, 0)
    out = jax.ShapeDtypeStruct((t, RWKV_DIM), F32)
    return pl.pallas_call(
        functools.partial(_rwkv_prep_body, tm=tm, tiles_per_seq=tiles_per_seq),
        out_shape=[out] * 7,
        grid=(t // tm,),
        in_specs=[
            pl.BlockSpec((tm, RWKV_PROJ), row),
            pl.BlockSpec((8, RWKV_PROJ), lambda i: (jnp.maximum(i * (tm // 8) - 1, 0), 0)),
            pl.BlockSpec((1, RWKV_PROJ), const),
            pl.BlockSpec((LORA_DIM, 3 * RWKV_DIM), const),
            pl.BlockSpec((1, RWKV_DIM), const),
            pl.BlockSpec((1, RWKV_DIM), const),
            pl.BlockSpec((1, RWKV_DIM), const),
            pl.BlockSpec((1, RWKV_DIM), const),
        ],
        out_specs=[pl.BlockSpec((tm, RWKV_DIM), row)] * 7,
        compiler_params=_params(("parallel",)),
        name="rwkv_prep",
    )(proj, proj, mu, w_lora, w0, a0, k_k, k_a)


def _rwkv_scan_body(r_ref, k_ref, v_ref, kk_ref, b_ref, lw_ref, g_ref, rk_ref, gg_ref, gb_ref,
                    o_ref, st_ref):
    c_sz, n, nh = RWKV_CHUNK, RWKV_HEAD_DIM, RWKV_HEADS

    @pl.when(pl.program_id(1) == 0)
    def _():
        st_ref[...] = jnp.zeros_like(st_ref)

    ri = lax.broadcasted_iota(jnp.int32, (c_sz, c_sz), 0)
    ci = lax.broadcasted_iota(jnp.int32, (c_sz, c_sz), 1)
    strict = ci < ri
    incl = ci <= ri
    eye_b = (ci == ri).astype(BF16)
    row2 = lax.broadcasted_iota(jnp.int32, (c_sz, 2 * c_sz), 0)
    lane2 = lax.broadcasted_iota(jnp.int32, (c_sz, 2 * c_sz), 1)
    right_half = lane2 >= c_sz
    zeros_b = jnp.zeros((c_sz, n), BF16)

    lw = lw_ref[...]
    cum = _dot(incl.astype(F32), lw, HI)
    cum_last = cum[c_sz - 1:c_sz, :]
    r = r_ref[...]
    k = k_ref[...]
    v = v_ref[...]
    b = b_ref[...]
    p_inv = jnp.exp(-cum)
    p_end = jnp.exp(cum_last - cum)
    left = jnp.concatenate([(-(kk_ref[...] * jnp.exp(cum - lw))).astype(BF16),
                            (r * jnp.exp(cum)).astype(BF16)], axis=0)
    bt_b = (b * p_inv).astype(BF16)
    kt_b = (k * p_inv).astype(BF16)
    bh_b = (b * p_end).astype(BF16)
    kh_b = (k * p_end).astype(BF16)
    v_b = v.astype(BF16)
    d_p = jnp.exp(cum_last)
    rk_all = r * k * rk_ref[...]

    heads = range(nh)
    sls = [slice(h * n, (h + 1) * n) for h in heads]
    a_l = [_dot_nt(left[:, sls[h]], jnp.concatenate([bt_b[:, sls[h]], zeros_b], axis=0))
           for h in heads]
    a_r = [_dot_nt(left[:, sls[h]], kt_b[:, sls[h]]) for h in heads]
    v_t = [_dot_nt(eye_b, v_b[:, sls[h]]).astype(BF16) for h in heads]

    x = [jnp.where(lane2 < row2, a_l[h][:c_sz], jnp.where(lane2 == row2 + c_sz, 1.0, 0.0))
         for h in heads]
    for _ in range(6):
        hi = [x[h].astype(BF16) for h in heads]
        lo = [(x[h] - hi[h].astype(F32)).astype(BF16) for h in heads]
        x = [_dot(hi[h][:, :c_sz], hi[h]) + _dot(hi[h][:, :c_sz], lo[h])
             + _dot(lo[h][:, :c_sz], hi[h]) + jnp.where(right_half, x[h], 0.0) for h in heads]
    x_b = [x[h].astype(BF16) for h in heads]

    s0 = [st_ref[h] for h in heads]
    s0_b = [s0[h].astype(BF16) for h in heads]
    a_ak = [jnp.where(strict, a_r[h][:c_sz], 0.0).astype(BF16) for h in heads]
    a_rk = [jnp.where(incl, a_r[h][c_sz:], 0.0).astype(BF16) for h in heads]
    a_rb = [jnp.where(incl, a_l[h][c_sz:, :c_sz], 0.0).astype(BF16) for h in heads]
    rhs = [_dot_nt(left[:c_sz, sls[h]], s0_b[h]) + _dot(a_ak[h], v_b[:, sls[h]]) for h in heads]
    u_b = [_dot(x_b[h], jnp.concatenate([zeros_b, rhs[h].astype(BF16)], axis=0)).astype(BF16)
           for h in heads]
    y = [_dot_nt(left[c_sz:, sls[h]], s0_b[h]) + _dot(a_rb[h], u_b[h])
         + _dot(a_rk[h], v_b[:, sls[h]]) for h in heads]
    u_t = [_dot_nt(eye_b, u_b[h]).astype(BF16) for h in heads]
    for h in heads:
        st_ref[h] = (s0[h] * d_p[:, sls[h]] + _dot(u_t[h], bh_b[:, sls[h]])
                     + _dot(v_t[h], kh_b[:, sls[h]]))

    outs = []
    for h in heads:
        mean = jnp.mean(y[h], axis=-1, keepdims=True)
        var = jnp.mean(jnp.square(y[h] - mean), axis=-1, keepdims=True)
        yn = (y[h] - mean) * lax.rsqrt(var + RWKV_GN_EPS)
        yn = yn * gg_ref[:, sls[h]] + gb_ref[:, sls[h]]
        bonus = jnp.sum(rk_all[:, sls[h]], axis=-1, keepdims=True) * v[:, sls[h]]
        outs.append((yn + bonus) * g_ref[:, sls[h]])
    o_ref[...] = jnp.concatenate(outs, axis=1)


def _rwkv_scan(r, k, v, kk, b, lw, g, r_k, gn_gain, gn_bias, *, batch, seq):
    t = r.shape[0]
    nc = seq // RWKV_CHUNK
    row = lambda bi, c: (bi * nc + c, 0)
    const = lambda bi, c: (0, 0)
    tok = pl.BlockSpec((RWKV_CHUNK, RWKV_DIM), row)
    par = pl.BlockSpec((1, RWKV_DIM), const)
    return pl.pallas_call(
        _rwkv_scan_body,
        out_shape=jax.ShapeDtypeStruct((t, RWKV_DIM), F32),
        grid=(batch, nc),
        in_specs=[tok] * 7 + [par] * 3,
        out_specs=tok,
        scratch_shapes=[pltpu.VMEM((RWKV_HEADS, RWKV_HEAD_DIM, RWKV_HEAD_DIM), F32)],
        compiler_params=_params(("parallel", "arbitrary")),
        name="rwkv_scan",
    )(r, k, v, kk, b, lw, g, r_k, gn_gain, gn_bias)


def _compress_body(x_ref, w1_ref, pe_ref, w2_ref, o_ref):
    half = CMP_STRIDE * NSA_HEAD_DIM
    x = x_ref[0, 0].astype(BF16)
    first = _dot(x, w1_ref[0, 0:half, :])
    second = _dot(x, w1_ref[0, half:2 * half, :])
    n_rows = x.shape[0]
    second_next = pltpu.roll(second, n_rows - 1, 0)
    pe = jnp.broadcast_to(pe_ref[0], (8, 2 * half)).astype(BF16)
    pe_term = _dot(pe, w1_ref[0])[0:1, :]
    hid = first + second_next + pe_term
    o_ref[0, 0] = _dot(jax.nn.gelu(hid).astype(BF16), w2_ref[0])


def _compress(x, w1, pe, w2):
    _, bg, rows, width = x.shape
    return pl.pallas_call(
        _compress_body,
        out_shape=jax.ShapeDtypeStruct((2, bg, rows, NSA_HEAD_DIM), F32),
        grid=(2, bg),
        in_specs=[
            pl.BlockSpec((1, 1, rows, width), lambda s, i: (s, i, 0, 0)),
            pl.BlockSpec((1, 2 * width, CMP_HIDDEN), lambda s, i: (s, 0, 0)),
            pl.BlockSpec((1, 1, 2 * width), lambda s, i: (s, 0, 0)),
            pl.BlockSpec((1, CMP_HIDDEN, NSA_HEAD_DIM), lambda s, i: (s, 0, 0)),
        ],
        out_specs=pl.BlockSpec((1, 1, rows, NSA_HEAD_DIM), lambda s, i: (s, i, 0, 0)),
        compiler_params=_params(("parallel", "parallel")),
        name="nsa_compress",
    )(x, w1, pe, w2)


def _t5_bucket(dist):
    n = jnp.maximum(dist, 0)
    exact = REL_BUCKETS // 2
    nf = jnp.maximum(n, 1).astype(F32)
    large = exact + (jnp.log(nf / exact) / math.log(REL_MAX_DIST / exact)
                     * (REL_BUCKETS - exact)).astype(jnp.int32)
    large = jnp.minimum(large, REL_BUCKETS - 1)
    return jnp.where(n < exact, n, large)


def _bias_body(tab_ref, bc_ref, bt_ref, *, seq, n_cmp_pad):
    h = pl.program_id(0)
    tq = NSA_TQ

    def lookup(dist):
        bucket = _t5_bucket(dist)
        out = jnp.zeros(dist.shape, F32)
        for bkt in range(REL_BUCKETS):
            out = jnp.where(bucket == bkt, tab_ref[bkt, h], out)
        return out

    rows = lax.broadcasted_iota(jnp.int32, (tq, tq), 0)
    cols = lax.broadcasted_iota(jnp.int32, (tq, tq), 1)
    bt_ref[0, BT_DIAG, 0] = jnp.where(rows >= cols, lookup(rows - cols), MASKED)
    bt_ref[0, BT_PREV, 0] = lookup(tq + rows - cols)
    bt_ref[0, BT_FAR, 0] = lookup(2 * tq + rows - cols)
    bt_ref[0, BT_WIN_EDGE, 0] = jnp.where(rows < cols, lookup(WINDOW + rows - cols), MASKED)
    bt_ref[0, BT_NONE, 0] = jnp.full((tq, tq), MASKED, F32)

    rows_c = lax.broadcasted_iota(jnp.int32, (tq, n_cmp_pad), 0)
    cmp_end = lax.broadcasted_iota(jnp.int32, (tq, n_cmp_pad), 1) * CMP_STRIDE + CMP_LEN - 1

    def cmp_tile(i, carry):
        bc_ref[0, pl.ds(pl.multiple_of(i * tq, tq), tq), :] = lookup(i * tq + rows_c - cmp_end)
        return carry

    lax.fori_loop(0, seq // tq, cmp_tile, 0)


def _bias_tables(rel_bias, seq):
    g, hpg, tq = NSA_KV_GROUPS, NSA_HPG, NSA_TQ
    n_cmp_pad = seq // CMP_STRIDE
    return pl.pallas_call(
        functools.partial(_bias_body, seq=seq, n_cmp_pad=n_cmp_pad),
        out_shape=[jax.ShapeDtypeStruct((NSA_HEADS, seq, n_cmp_pad), F32),
                   jax.ShapeDtypeStruct((g, BT_COUNT, hpg, tq, tq), F32)],
        grid=(NSA_HEADS,),
        in_specs=[pl.BlockSpec(memory_space=pltpu.SMEM)],
        out_specs=[pl.BlockSpec((1, seq, n_cmp_pad), lambda h: (h, 0, 0)),
                   pl.BlockSpec((1, BT_COUNT, 1, tq, tq),
                                lambda h: (h // hpg, 0, h % hpg, 0, 0))],
        compiler_params=_params(("parallel",)),
        name="nsa_bias",
    )(rel_bias)


def _nsa_body(q_ref, kc_ref, vc_ref, ks_ref, vs_ref, kw_ref, vw_ref, bc_ref, bt_ref, gate_ref,
              o_ref):
    tq, hpg, dh = NSA_TQ, NSA_HPG, NSA_HEAD_DIM
    rws = hpg * tq
    n_blk_log2 = SEL_BLOCK.bit_length() - 1
    n_blk = ks_ref.shape[2] // SEL_BLOCK
    n_cmp_pad = kc_ref.shape[2]
    i = pl.program_id(2)

    q = (q_ref[0, 0, 0] * (dh ** -0.5)).astype(BF16)

    rowi = lax.broadcasted_iota(jnp.int32, (rws, n_cmp_pad), 0)
    lane = lax.broadcasted_iota(jnp.int32, (rws, n_cmp_pad), 1)
    t_row = i * tq + (rowi & (tq - 1))
    kc = kc_ref[0, 0].astype(BF16)
    vc = vc_ref[0, 0].astype(BF16)
    s = _dot_nt(q, kc) + bc_ref[...].reshape(rws, n_cmp_pad)
    valid = (t_row - (lane * CMP_STRIDE + CMP_LEN - 1) >= 0) & (lane < n_cmp_pad - 1)
    s = jnp.where(valid, s, MASKED)
    e = jnp.where(valid, jnp.exp(s - jnp.max(s, axis=-1, keepdims=True)), 0.0)
    den = jnp.sum(e, axis=-1, keepdims=True)
    p_c = e / jnp.where(den > 0.0, den, 1.0)
    o_cmp = _dot(p_c.astype(BF16), vc)

    p_sum = p_c[0:tq]
    for hh in range(1, hpg):
        p_sum = p_sum + p_c[hh * tq:(hh + 1) * tq]
    blk_o = lax.broadcasted_iota(jnp.int32, (n_blk, n_cmp_pad), 0)
    cmp_o = lax.broadcasted_iota(jnp.int32, (n_blk, n_cmp_pad), 1)
    overlap_t = ((cmp_o * CMP_STRIDE <= blk_o * SEL_BLOCK + SEL_BLOCK - 1)
                 & (cmp_o * CMP_STRIDE + CMP_LEN - 1 >= blk_o * SEL_BLOCK)).astype(F32)
    imp = _dot_nt(overlap_t, p_sum, HI)
    jj = lax.broadcasted_iota(jnp.int32, (n_blk, tq), 0)
    cur = (i * tq + lax.broadcasted_iota(jnp.int32, (n_blk, tq), 1)) >> n_blk_log2
    forced = (jj == 0) | (jj == cur) | (jj == cur - 1)
    imp = jnp.where(jj > cur, -1e6, jnp.where(forced, 1e6, imp))
    rank = jnp.zeros((n_blk, tq), jnp.int32)
    for a in range(n_blk):
        row = imp[a:a + 1, :]
        beats = (row > imp) | ((row == imp) & (a < jj))
        rank = rank + beats.astype(jnp.int32)
    not_sel = jnp.where(rank < SEL_TOP_N, 0.0, 1.0).astype(BF16)
    eye_b = (lax.broadcasted_iota(jnp.int32, (tq, tq), 0)
             == lax.broadcasted_iota(jnp.int32, (tq, tq), 1)).astype(BF16)
    not_sel_t = _dot_nt(eye_b, not_sel).astype(BF16)
    not_sel_rows = jnp.concatenate([not_sel_t] * hpg, axis=0)
    kw = ks_ref.shape[3]
    place_q = (lax.broadcasted_iota(jnp.int32, (dh, kw), 0)
               == lax.broadcasted_iota(jnp.int32, (dh, kw), 1)).astype(BF16)
    place_m = (lax.broadcasted_iota(jnp.int32, (n_blk, kw), 0) + dh
               == lax.broadcasted_iota(jnp.int32, (n_blk, kw), 1)).astype(BF16)
    q_aug = (_dot(q, place_q) + _dot(not_sel_rows, place_m)).astype(BF16)

    def attend(scores, v_tile, carry):
        m_prev, acc = carry
        m_new = jnp.maximum(m_prev, jnp.max(scores, axis=-1, keepdims=True))
        pr = jnp.exp(scores - m_new)
        return m_new, jnp.exp(m_prev - m_new) * acc + _dot(pr.astype(BF16), v_tile)

    init = (jnp.full((rws, 1), MASKED, F32), jnp.zeros((rws, vs_ref.shape[3]), F32))

    def sel_step(kt, carry):
        off = pl.multiple_of(kt * tq, tq)
        bias = bt_ref[0, jnp.minimum(i - kt, BT_FAR)].reshape(rws, tq)
        scores = _dot_nt(q_aug, ks_ref[0, 0, pl.ds(off, tq), :]) + bias
        return attend(scores, vs_ref[0, 0, pl.ds(off, tq), :], carry)

    _, acc_s = lax.fori_loop(0, i + 1, sel_step, init)
    o_sel = acc_s[:, 0:dh] / acc_s[:, dh:dh + 1]

    n_win = WINDOW // tq
    offs, scores = [], []
    for delta in range(n_win + 1):
        entry = {0: BT_DIAG, 1: BT_PREV, n_win: BT_WIN_EDGE}.get(delta, BT_FAR)
        if delta > 0:
            entry = jnp.where(i - delta >= 0, entry, BT_NONE)
        offs.append(pl.multiple_of(jnp.maximum(i - delta, 0) * tq, tq))
        scores.append(_dot_nt(q, kw_ref[0, 0, pl.ds(offs[-1], tq), :])
                      + bt_ref[0, entry].reshape(rws, tq))
    m_all = scores[0]
    for sc in scores[1:]:
        m_all = jnp.maximum(m_all, sc)
    m_w = jnp.max(m_all, axis=-1, keepdims=True)
    acc_w = jnp.zeros((rws, vw_ref.shape[3]), F32)
    for off, sc in zip(offs, scores):
        acc_w = acc_w + _dot(jnp.exp(sc - m_w).astype(BF16), vw_ref[0, 0, pl.ds(off, tq), :])
    o_win = acc_w[:, 0:dh] / acc_w[:, dh:dh + 1]

    gates = jax.nn.sigmoid(gate_ref[0, 0, 0])
    o_ref[0, 0, 0] = gates[:, 0:1] * o_cmp + gates[:, 1:2] * o_sel + gates[:, 2:3] * o_win


def _nsa_attention(q, kvc, k_sel, v_sel, k_win, v_win, bias_c, bias_t, gates):
    bsz, g, nq, rws, dh = q.shape
    seq = k_sel.shape[2]
    tq, hpg = NSA_TQ, NSA_HPG
    n_rows = kvc.shape[2]
    full = lambda a: pl.BlockSpec((1, 1, seq, a.shape[3]), lambda b, gi, i: (b, gi, 0, 0))
    return pl.pallas_call(
        _nsa_body,
        out_shape=jax.ShapeDtypeStruct((bsz, g, nq, rws, dh), F32),
        grid=(bsz, g, nq),
        in_specs=[
            pl.BlockSpec((1, 1, 1, rws, dh), lambda b, gi, i: (b, gi, i, 0, 0)),
            pl.BlockSpec((1, 1, n_rows, dh), lambda b, gi, i: (0, b * g + gi, 0, 0)),
            pl.BlockSpec((1, 1, n_rows, dh), lambda b, gi, i: (1, b * g + gi, 0, 0)),
            full(k_sel), full(v_sel), full(k_win), full(v_win),
            pl.BlockSpec((hpg, tq, n_rows), lambda b, gi, i: (gi, i, 0)),
            pl.BlockSpec((1, BT_COUNT, hpg, tq, tq), lambda b, gi, i: (gi, 0, 0, 0, 0)),
            pl.BlockSpec((1, 1, 1, rws, 3), lambda b, gi, i: (b, gi, i, 0, 0)),
        ],
        out_specs=pl.BlockSpec((1, 1, 1, rws, dh), lambda b, gi, i: (b, gi, i, 0, 0)),
        compiler_params=_params(("parallel", "parallel", "arbitrary")),
        name="nsa_attention",
    )(q, kvc, kvc, k_sel, v_sel, k_win, v_win, bias_c, bias_t, gates)


def _mem_body(q_ref, kv_ref, o_ref):
    outs = []
    for h in range(MEM_HEADS):
        sl = slice(h * MEM_HEAD_DIM, (h + 1) * MEM_HEAD_DIM)
        qh = (q_ref[:, sl] * (MEM_HEAD_DIM ** -0.5)).astype(BF16)
        kh = kv_ref[0, :, sl].astype(BF16)
        vh = kv_ref[0, :, MEM_DIM + h * MEM_HEAD_DIM:MEM_DIM + (h + 1) * MEM_HEAD_DIM].astype(BF16)
        s = _dot_nt(qh, kh)
        e = jnp.exp(s - jnp.max(s, axis=-1, keepdims=True))
        p = e / jnp.sum(e, axis=-1, keepdims=True)
        outs.append(_dot(p.astype(BF16), vh))
    o_ref[...] = jnp.concatenate(outs, axis=1)


def _mem_attention(proj, kv, *, seq, tq=512):
    t = proj.shape[0]
    bsz, m, _ = kv.shape
    per_seq = seq // tq
    return pl.pallas_call(
        _mem_body,
        out_shape=jax.ShapeDtypeStruct((t, MEM_DIM), F32),
        grid=(t // tq,),
        in_specs=[
            pl.BlockSpec((tq, MEM_DIM), lambda i: (i, COL_QMEM // MEM_DIM)),
            pl.BlockSpec((1, m, 2 * MEM_DIM), lambda i: (i // per_seq, 0, 0)),
        ],
        out_specs=pl.BlockSpec((tq, MEM_DIM), lambda i: (i, 0)),
        compiler_params=_params(("parallel",)),
        name="mem_attention",
    )(proj, kv)


def _merge_body(x_ref, yr_ref, yn_ref, ym_ref, g0_ref, g1_ref, g2_ref, wr_ref, wn_ref, wm_ref,
                wo_ref, o_ref):
    merged = (jax.nn.sigmoid(g0_ref[...]) * _dot(yr_ref[...].astype(BF16), wr_ref[...])
              + jax.nn.sigmoid(g1_ref[...]) * _dot(yn_ref[...].astype(BF16), wn_ref[...])
              + jax.nn.sigmoid(g2_ref[...]) * _dot(ym_ref[...].astype(BF16), wm_ref[...]))
    o_ref[...] = x_ref[...] + _dot(merged.astype(BF16), wo_ref[...])


def _merge(x, y_rwkv, y_nsa, y_mem, proj, w_r, w_n, w_m, w_o, *, tm=512):
    t, d = x.shape
    row = lambda i: (i, 0)
    const = lambda i: (0, 0)
    gate0 = COL_GBR // d
    return pl.pallas_call(
        _merge_body,
        out_shape=jax.ShapeDtypeStruct((t, d), F32),
        grid=(t // tm,),
        in_specs=[
            pl.BlockSpec((tm, d), row),
            pl.BlockSpec((tm, RWKV_DIM), row),
            pl.BlockSpec((tm, NSA_DIM), row),
            pl.BlockSpec((tm, MEM_DIM), row),
            pl.BlockSpec((tm, d), lambda i: (i, gate0)),
            pl.BlockSpec((tm, d), lambda i: (i, gate0 + 1)),
            pl.BlockSpec((tm, d), lambda i: (i, gate0 + 2)),
            pl.BlockSpec((RWKV_DIM, d), const),
            pl.BlockSpec((NSA_DIM, d), const),
            pl.BlockSpec((MEM_DIM, d), const),
            pl.BlockSpec((d, d), const),
        ],
        out_specs=pl.BlockSpec((tm, d), row),
        compiler_params=_params(("parallel",)),
        name="merge",
    )(x, y_rwkv, y_nsa, y_mem, proj, proj, proj, w_r, w_n, w_m, w_o)


def _row(a):
    return a.reshape(1, -1)


def _rwkv_branch(proj, p, l, bsz, seq):
    w_lora = jnp.zeros((LORA_DIM, 3 * RWKV_DIM), F32)
    w_lora = w_lora.at[0:DECAY_LORA, 0:RWKV_DIM].set(p['rwkv_w2'][l])
    w_lora = w_lora.at[DECAY_LORA:DECAY_LORA + AAA_LORA, RWKV_DIM:2 * RWKV_DIM].set(p['rwkv_a2'][l])
    w_lora = w_lora.at[DECAY_LORA + AAA_LORA:, 2 * RWKV_DIM:].set(p['rwkv_g2'][l])
    r, k, v, kk, b, lw, gate = _rwkv_prep(
        proj, seq, _row(p['rwkv_mu'][l]), w_lora.astype(BF16), _row(p['rwkv_w0'][l]),
        _row(p['rwkv_a0'][l]), _row(p['rwkv_k_k'][l]), _row(p['rwkv_k_a'][l]))
    return _rwkv_scan(r, k, v, kk, b, lw, gate, _row(p['rwkv_r_k'][l]),
                      _row(p['rwkv_gn_gain'][l]), _row(p['rwkv_gn_bias'][l]), batch=bsz, seq=seq)


def _nsa_branch(proj, bias_c, bias_t, p, l, bsz, seq):
    g, hpg, dh, tq = NSA_KV_GROUPS, NSA_HPG, NSA_HEAD_DIM, NSA_TQ
    nq = seq // tq
    kv6 = proj[:, COL_KV:COL_KV + 6 * NSA_KV_DIM].reshape(bsz, seq, 6, g, dh)
    kv6 = kv6.transpose(2, 0, 3, 1, 4)
    cmp_in = kv6[0:2].reshape(2, bsz * g, seq // CMP_STRIDE, CMP_STRIDE * dh)
    w1 = jnp.stack([p['cmp_k_w1'][l], p['cmp_v_w1'][l]]).astype(BF16)
    pe = jnp.stack([p['cmp_pe_k'][l].reshape(1, -1), p['cmp_pe_v'][l].reshape(1, -1)])
    w2 = jnp.stack([p['cmp_k_w2'][l], p['cmp_v_w2'][l]]).astype(BF16)
    kvc = _compress(cmp_in, w1, pe, w2)
    kv_b = kv6[2:6].astype(BF16)
    n_blk = seq // SEL_BLOCK
    onehot = (jnp.arange(seq)[:, None] // SEL_BLOCK == jnp.arange(n_blk)[None, :])
    onehot = jnp.broadcast_to(jnp.where(onehot, MASKED, 0.0).astype(BF16), (bsz, g, seq, n_blk))
    k_sel = jnp.concatenate(
        [kv_b[0], onehot, jnp.zeros((bsz, g, seq, NSA_KW - dh - n_blk), BF16)], axis=-1)
    ones_col = jnp.concatenate([jnp.ones((bsz, g, seq, 1), BF16),
                                jnp.zeros((bsz, g, seq, NSA_KW - dh - 1), BF16)], axis=-1)
    v_sel = jnp.concatenate([kv_b[1], ones_col], axis=-1)
    v_win = jnp.concatenate([kv_b[3], ones_col], axis=-1)
    q_nsa = proj[:, COL_QNSA:COL_QNSA + NSA_DIM].reshape(bsz, nq, tq, g, hpg, dh)
    q_nsa = q_nsa.transpose(0, 3, 1, 4, 2, 5).reshape(bsz, g, nq, hpg * tq, dh)
    g_nsa = proj[:, COL_GNSA:COL_GNSA + 3 * NSA_HEADS].reshape(bsz, nq, tq, 3, g, hpg)
    g_nsa = g_nsa.transpose(0, 4, 1, 5, 2, 3).reshape(bsz, g, nq, hpg * tq, 3)
    o_nsa = _nsa_attention(q_nsa, kvc, k_sel, v_sel, kv_b[2], v_win, bias_c, bias_t, g_nsa)
    return o_nsa.reshape(bsz, g, nq, hpg, tq, dh).transpose(0, 2, 4, 1, 3, 5).reshape(
        bsz * seq, NSA_DIM)


def _mem_branch(proj, mem, p, l):
    bsz, m_tok, d = mem.shape
    w_kv = jnp.concatenate([p['mem_w_k'][l], p['mem_w_v'][l]], axis=1).astype(BF16)
    kv_mem = _norm_matmul(mem.reshape(bsz * m_tok, d), _row(p['mem_norm'][l]), w_kv,
                          tm=min(1024, bsz * m_tok), tn=512, name="mem_kv")
    return _mem_attention(proj, kv_mem.reshape(bsz, m_tok, 2 * MEM_DIM),
                          seq=proj.shape[0] // bsz)


def _layer(x, mem, l, bias_c, bias_t, p):
    bsz, seq, d = x.shape
    t = bsz * seq
    row = _row
    x = x.reshape(t, d)

    x = _ffn(x, row(p['ffn1_norm'][l]), p['ffn1_w_gate'][l].astype(BF16),
             p['ffn1_w_up'][l].astype(BF16), p['ffn1_w_down'][l].astype(BF16),
             row(p['final_norm']), final=False)

    w_in = p['w_in'][l]
    o = 0
    parts = {}
    for name, size in (('rwkv', RWKV_PROJ), ('q', NSA_DIM), ('kv', 6 * NSA_KV_DIM),
                       ('g_nsa', 3 * NSA_HEADS), ('q_mem', MEM_DIM), ('g_br', N_BRANCH * d)):
        parts[name] = w_in[:, o:o + size]
        o += size
    pad = jnp.zeros((d, PROJ_COLS - COL_GNSA - 3 * NSA_HEADS), w_in.dtype)
    w_perm = jnp.concatenate([parts['rwkv'], parts['q'], parts['kv'], parts['g_br'],
                              parts['q_mem'], parts['g_nsa'], pad], axis=1).astype(BF16)
    proj = _norm_matmul(x, row(p['mix_norm'][l]), w_perm, tm=1024, tn=PROJ_TN, name="in_proj")

    y_rwkv = _rwkv_branch(proj, p, l, bsz, seq)
    y_nsa = _nsa_branch(proj, bias_c, bias_t, p, l, bsz, seq)
    y_mem = _mem_branch(proj, mem, p, l)

    x = _merge(x, y_rwkv, y_nsa, y_mem, proj, p['w_br_rwkv'][l].astype(BF16),
               p['w_br_nsa'][l].astype(BF16), p['w_br_mem'][l].astype(BF16),
               p['w_out'][l].astype(BF16))

    last = l == p['ffn1_norm'].shape[0] - 1
    x = _ffn(x, row(p['ffn2_norm'][l]), p['ffn2_w_gate'][l].astype(BF16),
             p['ffn2_w_up'][l].astype(BF16), p['ffn2_w_down'][l].astype(BF16),
             row(p['final_norm']), final=last)
    return x.reshape(bsz, seq, d)


def kernel(x, mem, ffn1_norm, ffn1_w_gate, ffn1_w_up, ffn1_w_down, mix_norm, w_in, rwkv_mu, rwkv_w0, rwkv_w2, rwkv_a0, rwkv_a2, rwkv_g2, rwkv_k_k, rwkv_k_a, rwkv_r_k, rwkv_gn_gain, rwkv_gn_bias, cmp_pe_k, cmp_k_w1, cmp_k_w2, cmp_pe_v, cmp_v_w1, cmp_v_w2, rel_bias, mem_norm, mem_w_k, mem_w_v, w_br_rwkv, w_br_nsa, w_br_mem, w_out, ffn2_norm, ffn2_w_gate, ffn2_w_up, ffn2_w_down, final_norm):
    p = dict(ffn1_norm=ffn1_norm, ffn1_w_gate=ffn1_w_gate, ffn1_w_up=ffn1_w_up,
             ffn1_w_down=ffn1_w_down, mix_norm=mix_norm, w_in=w_in, rwkv_mu=rwkv_mu,
             rwkv_w0=rwkv_w0, rwkv_w2=rwkv_w2, rwkv_a0=rwkv_a0, rwkv_a2=rwkv_a2, rwkv_g2=rwkv_g2,
             rwkv_k_k=rwkv_k_k, rwkv_k_a=rwkv_k_a, rwkv_r_k=rwkv_r_k, rwkv_gn_gain=rwkv_gn_gain,
             rwkv_gn_bias=rwkv_gn_bias, cmp_pe_k=cmp_pe_k, cmp_k_w1=cmp_k_w1, cmp_k_w2=cmp_k_w2,
             cmp_pe_v=cmp_pe_v, cmp_v_w1=cmp_v_w1, cmp_v_w2=cmp_v_w2, mem_norm=mem_norm,
             mem_w_k=mem_w_k, mem_w_v=mem_w_v, w_br_rwkv=w_br_rwkv, w_br_nsa=w_br_nsa,
             w_br_mem=w_br_mem, w_out=w_out, ffn2_norm=ffn2_norm, ffn2_w_gate=ffn2_w_gate,
             ffn2_w_up=ffn2_w_up, ffn2_w_down=ffn2_w_down, final_norm=final_norm)
    bias_c, bias_t = _bias_tables(rel_bias, x.shape[1])
    for l in range(ffn1_norm.shape[0]):
        x = _layer(x, mem, l, bias_c, bias_t, p)
    return x
```

```python
import functools
import math

import jax
import jax.numpy as jnp
from jax import lax
from jax.experimental import pallas as pl
from jax.experimental.pallas import tpu as pltpu

F32 = jnp.float32
BF16 = jnp.bfloat16
HI = lax.Precision.HIGHEST

D_MODEL = 1024
NORM_EPS = 1e-6
D_FF = 2816
RWKV_HEADS = 8
RWKV_HEAD_DIM = 64
RWKV_DIM = RWKV_HEADS * RWKV_HEAD_DIM
DECAY_LORA = 64
AAA_LORA = 64
GATE_LORA = 128
LORA_DIM = DECAY_LORA + AAA_LORA + GATE_LORA
RWKV_GN_EPS = 64e-5
RWKV_PROJ = 3 * RWKV_DIM + LORA_DIM
NSA_HEADS = 8
NSA_KV_GROUPS = 2
NSA_HPG = NSA_HEADS // NSA_KV_GROUPS
NSA_HEAD_DIM = 64
NSA_DIM = NSA_HEADS * NSA_HEAD_DIM
NSA_KV_DIM = NSA_KV_GROUPS * NSA_HEAD_DIM
CMP_LEN = 32
CMP_STRIDE = 16
CMP_HIDDEN = 256
SEL_BLOCK = 64
SEL_TOP_N = 16
WINDOW = 512
REL_BUCKETS = 32
REL_MAX_DIST = 128
MEM_HEADS = 4
MEM_HEAD_DIM = 128
MEM_DIM = MEM_HEADS * MEM_HEAD_DIM
N_BRANCH = 3

LANES = 128
GATE_PAD = LANES
COL_RWKV = 0
COL_QNSA = COL_RWKV + RWKV_PROJ
COL_GNSA = COL_QNSA + NSA_DIM
COL_QMEM = COL_GNSA + NSA_KV_GROUPS * GATE_PAD
COL_GBR = COL_QMEM + MEM_DIM
PROJ_COLS = COL_GBR + N_BRANCH * D_MODEL
PROJ_TN = 768
assert PROJ_COLS % PROJ_TN == 0 and COL_GBR % D_MODEL == 0 and COL_QMEM % MEM_DIM == 0
KV_KINDS = 6
KV_COLS = KV_KINDS * NSA_KV_GROUPS * LANES
KIND_K_SEL, KIND_V_SEL, KIND_K_WIN, KIND_V_WIN = 2, 3, 4, 5

RWKV_CHUNK = 64
NSA_TQ = 256
NSA_KW = 128
MASKED = -1e30
BT_DIAG, BT_PREV, BT_FAR, BT_WIN_EDGE, BT_NONE, BT_COUNT = 0, 1, 2, 3, 4, 5
VMEM_LIMIT = 56 * 1024 * 1024


def _dot(a, b, precision=None):
    return jnp.dot(a, b, preferred_element_type=F32, precision=precision)


def _dot_nt(a, b, precision=None):
    return lax.dot_general(a, b, (((1,), (1,)), ((), ())), preferred_element_type=F32,
                           precision=precision)


def _params(semantics):
    return pltpu.CompilerParams(dimension_semantics=semantics, vmem_limit_bytes=VMEM_LIMIT)


def _rms(x, g):
    return x * lax.rsqrt(jnp.mean(x * x, axis=-1, keepdims=True) + NORM_EPS) * g


def _ffn_body(x_ref, g_ref, wg_ref, wu_ref, wd_ref, fg_ref, o_ref, h_ref, acc_ref, *, nf, final):
    j = pl.program_id(1)

    @pl.when(j == 0)
    def _():
        h_ref[...] = _rms(x_ref[...], g_ref[...]).astype(BF16)
        acc_ref[...] = jnp.zeros_like(acc_ref)

    h = h_ref[...]
    gate = _dot(h, wg_ref[...])
    up = _dot(h, wu_ref[...])
    act = (jax.nn.silu(gate) * up).astype(BF16)
    acc_ref[...] += _dot(act, wd_ref[...])

    @pl.when(j == nf - 1)
    def _():
        y = x_ref[...] + 0.5 * acc_ref[...]
        if final:
            y = _rms(y, fg_ref[...])
        o_ref[...] = y


def _ffn(x, gain, wg, wu, wd, final_gain, *, final, tm=1024, tf=256):
    t, d = x.shape
    f = wg.shape[1]
    nf = f // tf
    return pl.pallas_call(
        functools.partial(_ffn_body, nf=nf, final=final),
        out_shape=jax.ShapeDtypeStruct((t, d), F32),
        grid=(t // tm, nf),
        in_specs=[
            pl.BlockSpec((tm, d), lambda i, j: (i, 0)),
            pl.BlockSpec((1, d), lambda i, j: (0, 0)),
            pl.BlockSpec((d, tf), lambda i, j: (0, j)),
            pl.BlockSpec((d, tf), lambda i, j: (0, j)),
            pl.BlockSpec((tf, d), lambda i, j: (j, 0)),
            pl.BlockSpec((1, d), lambda i, j: (0, 0)),
        ],
        out_specs=pl.BlockSpec((tm, d), lambda i, j: (i, 0)),
        scratch_shapes=[pltpu.VMEM((tm, d), BF16), pltpu.VMEM((tm, d), F32)],
        compiler_params=_params(("parallel", "arbitrary")),
        name="ffn_final" if final else "ffn",
    )(x, gain, wg, wu, wd, final_gain)


def _norm_matmul_body(x_ref, g_ref, w_ref, o_ref, h_ref):
    @pl.when(pl.program_id(1) == 0)
    def _():
        h_ref[...] = _rms(x_ref[...], g_ref[...]).astype(BF16)

    o_ref[...] = _dot(h_ref[...], w_ref[...])


def _norm_matmul(x, gain, w, *, tm, tn, name):
    t, d = x.shape
    n = w.shape[1]
    return pl.pallas_call(
        _norm_matmul_body,
        out_shape=jax.ShapeDtypeStruct((t, n), F32),
        grid=(t // tm, n // tn),
        in_specs=[
            pl.BlockSpec((tm, d), lambda i, j: (i, 0)),
            pl.BlockSpec((1, d), lambda i, j: (0, 0)),
            pl.BlockSpec((d, tn), lambda i, j: (0, j)),
        ],
        out_specs=pl.BlockSpec((tm, tn), lambda i, j: (i, j)),
        scratch_shapes=[pltpu.VMEM((tm, d), BF16)],
        compiler_params=_params(("parallel", "arbitrary")),
        name=name,
    )(x, gain, w)


def _kv_proj_body(x_ref, g_ref, w_ref, o_ref, *, tm, seq):
    dh = NSA_HEAD_DIM
    h = _rms(x_ref[...], g_ref[...]).astype(BF16)
    y = _dot(h, w_ref[...])
    row = lax.broadcasted_iota(jnp.int32, (tm, LANES), 0)
    lane = lax.broadcasted_iota(jnp.int32, (tm, LANES), 1)
    pos = (pl.program_id(0) * tm) % seq + row
    block_mark = jnp.where(lane - dh == (pos >> (SEL_BLOCK.bit_length() - 1)), MASKED, 0.0)
    ones_col = jnp.where(lane == dh, 1.0, 0.0)
    for kind in range(KV_KINDS):
        for gi in range(NSA_KV_GROUPS):
            c = (kind * NSA_KV_GROUPS + gi) * LANES
            part = y[:, c:c + LANES]
            if kind == KIND_K_SEL:
                part = part + block_mark
            if kind in (KIND_V_SEL, KIND_V_WIN):
                part = part + ones_col
            o_ref[:, c:c + LANES] = part.astype(BF16)


def _kv_proj(x, gain, w, *, seq, tm=512):
    t, d = x.shape
    n = w.shape[1]
    return pl.pallas_call(
        functools.partial(_kv_proj_body, tm=tm, seq=seq),
        out_shape=jax.ShapeDtypeStruct((t, n), BF16),
        grid=(t // tm,),
        in_specs=[
            pl.BlockSpec((tm, d), lambda i: (i, 0)),
            pl.BlockSpec((1, d), lambda i: (0, 0)),
            pl.BlockSpec((d, n), lambda i: (0, 0)),
        ],
        out_specs=pl.BlockSpec((tm, n), lambda i: (i, 0)),
        compiler_params=_params(("parallel",)),
        name="nsa_kv_proj",
    )(x, gain, w)


def _rwkv_prep_body(p_ref, prev_ref, mu_ref, wl_ref, w0_ref, a0_ref, kk_ref, ka_ref,
                    r_o, k_o, v_o, kk_o, b_o, lw_o, g_o, *, tm, tiles_per_seq):
    i = pl.program_id(0)
    p = p_ref[...]
    keep = jnp.where(i % tiles_per_seq == 0, 0.0, 1.0)
    prev_last = prev_ref[7:8, :] * keep
    rows = lax.broadcasted_iota(jnp.int32, p.shape, 0)
    shifted = jnp.where(rows == 0, prev_last, pltpu.roll(p, 1, 0))
    x = p + (shifted - p) * mu_ref[...]

    r = x[:, 0:RWKV_DIM]
    k = x[:, RWKV_DIM:2 * RWKV_DIM]
    v = x[:, 2 * RWKV_DIM:3 * RWKV_DIM]
    s = x[:, 3 * RWKV_DIM:RWKV_PROJ]
    lane = lax.broadcasted_iota(jnp.int32, s.shape, 1)
    z = jnp.where(lane < DECAY_LORA, jnp.tanh(s),
                  jnp.where(lane < DECAY_LORA + AAA_LORA, s, jax.nn.sigmoid(s)))
    lo = _dot(z.astype(BF16), wl_ref[...])
    w = -jax.nn.softplus(-(w0_ref[...] + lo[:, 0:RWKV_DIM])) - 0.5
    a = jax.nn.sigmoid(a0_ref[...] + lo[:, RWKV_DIM:2 * RWKV_DIM])
    g = lo[:, 2 * RWKV_DIM:3 * RWKV_DIM]

    kkr = k * kk_ref[...]
    hi = lax.broadcasted_iota(jnp.int32, (RWKV_DIM, RWKV_DIM), 0) // RWKV_HEAD_DIM
    hj = lax.broadcasted_iota(jnp.int32, (RWKV_DIM, RWKV_DIM), 1) // RWKV_HEAD_DIM
    same_head = (hi == hj).astype(F32)
    ssq = _dot(kkr * kkr, same_head, HI)
    kk = kkr / jnp.maximum(jnp.sqrt(ssq), 1e-12)

    r_o[...] = r
    k_o[...] = k * (1.0 + (a - 1.0) * ka_ref[...])
    v_o[...] = v
    kk_o[...] = kk
    b_o[...] = kk * a
    lw_o[...] = -jnp.exp(w)
    g_o[...] = g


def _rwkv_prep(proj, seq, mu, w_lora, w0, a0, k_k, k_a, *, tm=512):
    t = proj.shape[0]
    tiles_per_seq = seq // tm
    row = lambda i: (i, 0)
    const = lambda i: (0, 0)
    out = jax.ShapeDtypeStruct((t, RWKV_DIM), F32)
    return pl.pallas_call(
        functools.partial(_rwkv_prep_body, tm=tm, tiles_per_seq=tiles_per_seq),
        out_shape=[out] * 7,
        grid=(t // tm,),
        in_specs=[
            pl.BlockSpec((tm, RWKV_PROJ), row),
            pl.BlockSpec((8, RWKV_PROJ), lambda i: (jnp.maximum(i * (tm // 8) - 1, 0), 0)),
            pl.BlockSpec((1, RWKV_PROJ), const),
            pl.BlockSpec((LORA_DIM, 3 * RWKV_DIM), const),
            pl.BlockSpec((1, RWKV_DIM), const),
            pl.BlockSpec((1, RWKV_DIM), const),
            pl.BlockSpec((1, RWKV_DIM), const),
            pl.BlockSpec((1, RWKV_DIM), const),
        ],
        out_specs=[pl.BlockSpec((tm, RWKV_DIM), row)] * 7,
        compiler_params=_params(("parallel",)),
        name="rwkv_prep",
    )(proj, proj, mu, w_lora, w0, a0, k_k, k_a)


def _rwkv_scan_body(r_ref, k_ref, v_ref, kk_ref, b_ref, lw_ref, g_ref, rk_ref, gg_ref, gb_ref,
                    o_ref, st_ref):
    c_sz, n, nh = RWKV_CHUNK, RWKV_HEAD_DIM, RWKV_HEADS

    @pl.when(pl.program_id(1) == 0)
    def _():
        st_ref[...] = jnp.zeros_like(st_ref)

    ri = lax.broadcasted_iota(jnp.int32, (c_sz, c_sz), 0)
    ci = lax.broadcasted_iota(jnp.int32, (c_sz, c_sz), 1)
    strict = ci < ri
    incl = ci <= ri
    eye_b = (ci == ri).astype(BF16)
    row2 = lax.broadcasted_iota(jnp.int32, (c_sz, 2 * c_sz), 0)
    lane2 = lax.broadcasted_iota(jnp.int32, (c_sz, 2 * c_sz), 1)
    right_half = lane2 >= c_sz
    zeros_b = jnp.zeros((c_sz, n), BF16)

    lw = lw_ref[...]
    cum = _dot(incl.astype(F32), lw, HI)
    cum_last = cum[c_sz - 1:c_sz, :]
    r = r_ref[...]
    k = k_ref[...]
    v = v_ref[...]
    b = b_ref[...]
    p_inv = jnp.exp(-cum)
    p_end = jnp.exp(cum_last - cum)
    left = jnp.concatenate([(-(kk_ref[...] * jnp.exp(cum - lw))).astype(BF16),
                            (r * jnp.exp(cum)).astype(BF16)], axis=0)
    bt_b = (b * p_inv).astype(BF16)
    kt_b = (k * p_inv).astype(BF16)
    bh_b = (b * p_end).astype(BF16)
    kh_b = (k * p_end).astype(BF16)
    v_b = v.astype(BF16)
    d_p = jnp.exp(cum_last)
    rk_all = r * k * rk_ref[...]

    heads = range(nh)
    sls = [slice(h * n, (h + 1) * n) for h in heads]
    a_l = [_dot_nt(left[:, sls[h]], jnp.concatenate([bt_b[:, sls[h]], zeros_b], axis=0))
           for h in heads]
    a_r = [_dot_nt(left[:, sls[h]], kt_b[:, sls[h]]) for h in heads]
    v_t = [_dot_nt(eye_b, v_b[:, sls[h]]).astype(BF16) for h in heads]

    x = [jnp.where(lane2 < row2, a_l[h][:c_sz], jnp.where(lane2 == row2 + c_sz, 1.0, 0.0))
         for h in heads]
    for _ in range(6):
        hi = [x[h].astype(BF16) for h in heads]
        lo = [(x[h] - hi[h].astype(F32)).astype(BF16) for h in heads]
        x = [_dot(hi[h][:, :c_sz], hi[h]) + _dot(hi[h][:, :c_sz], lo[h])
             + _dot(lo[h][:, :c_sz], hi[h]) + jnp.where(right_half, x[h], 0.0) for h in heads]
    x_b = [x[h].astype(BF16) for h in heads]

    s0 = [st_ref[h] for h in heads]
    s0_b = [s0[h].astype(BF16) for h in heads]
    a_ak = [jnp.where(strict, a_r[h][:c_sz], 0.0).astype(BF16) for h in heads]
    a_rk = [jnp.where(incl, a_r[h][c_sz:], 0.0).astype(BF16) for h in heads]
    a_rb = [jnp.where(incl, a_l[h][c_sz:, :c_sz], 0.0).astype(BF16) for h in heads]
    rhs = [_dot_nt(left[:c_sz, sls[h]], s0_b[h]) + _dot(a_ak[h], v_b[:, sls[h]]) for h in heads]
    u_b = [_dot(x_b[h], jnp.concatenate([zeros_b, rhs[h].astype(BF16)], axis=0)).astype(BF16)
           for h in heads]
    y = [_dot_nt(left[c_sz:, sls[h]], s0_b[h]) + _dot(a_rb[h], u_b[h])
         + _dot(a_rk[h], v_b[:, sls[h]]) for h in heads]
    u_t = [_dot_nt(eye_b, u_b[h]).astype(BF16) for h in heads]
    for h in heads:
        st_ref[h] = (s0[h] * d_p[:, sls[h]] + _dot(u_t[h], bh_b[:, sls[h]])
                     + _dot(v_t[h], kh_b[:, sls[h]]))

    outs = []
    for h in heads:
        mean = jnp.mean(y[h], axis=-1, keepdims=True)
        var = jnp.mean(jnp.square(y[h] - mean), axis=-1, keepdims=True)
        yn = (y[h] - mean) * lax.rsqrt(var + RWKV_GN_EPS)
        yn = yn * gg_ref[:, sls[h]] + gb_ref[:, sls[h]]
        bonus = jnp.sum(rk_all[:, sls[h]], axis=-1, keepdims=True) * v[:, sls[h]]
        outs.append((yn + bonus) * g_ref[:, sls[h]])
    o_ref[...] = jnp.concatenate(outs, axis=1)


def _rwkv_scan(r, k, v, kk, b, lw, g, r_k, gn_gain, gn_bias, *, batch, seq):
    t = r.shape[0]
    nc = seq // RWKV_CHUNK
    row = lambda bi, c: (bi * nc + c, 0)
    const = lambda bi, c: (0, 0)
    tok = pl.BlockSpec((RWKV_CHUNK, RWKV_DIM), row)
    par = pl.BlockSpec((1, RWKV_DIM), const)
    return pl.pallas_call(
        _rwkv_scan_body,
        out_shape=jax.ShapeDtypeStruct((t, RWKV_DIM), F32),
        grid=(batch, nc),
        in_specs=[tok] * 7 + [par] * 3,
        out_specs=tok,
        scratch_shapes=[pltpu.VMEM((RWKV_HEADS, RWKV_HEAD_DIM, RWKV_HEAD_DIM), F32)],
        compiler_params=_params(("parallel", "arbitrary")),
        name="rwkv_scan",
    )(r, k, v, kk, b, lw, g, r_k, gn_gain, gn_bias)


def _compress_body(x_ref, w1_ref, pe_ref, w2_ref, o_ref):
    half = CMP_STRIDE * NSA_HEAD_DIM
    x = x_ref[0, 0].astype(BF16)
    first = _dot(x, w1_ref[0, 0:half, :])
    second = _dot(x, w1_ref[0, half:2 * half, :])
    n_rows = x.shape[0]
    second_next = pltpu.roll(second, n_rows - 1, 0)
    pe = jnp.broadcast_to(pe_ref[0], (8, 2 * half)).astype(BF16)
    pe_term = _dot(pe, w1_ref[0])[0:1, :]
    hid = first + second_next + pe_term
    o_ref[0, 0] = _dot(jax.nn.gelu(hid).astype(BF16), w2_ref[0])


def _compress(x, w1, pe, w2):
    _, bg, rows, width = x.shape
    return pl.pallas_call(
        _compress_body,
        out_shape=jax.ShapeDtypeStruct((2, bg, rows, NSA_HEAD_DIM), F32),
        grid=(2, bg),
        in_specs=[
            pl.BlockSpec((1, 1, rows, width), lambda s, i: (s, i, 0, 0)),
            pl.BlockSpec((1, 2 * width, CMP_HIDDEN), lambda s, i: (s, 0, 0)),
            pl.BlockSpec((1, 1, 2 * width), lambda s, i: (s, 0, 0)),
            pl.BlockSpec((1, CMP_HIDDEN, NSA_HEAD_DIM), lambda s, i: (s, 0, 0)),
        ],
        out_specs=pl.BlockSpec((1, 1, rows, NSA_HEAD_DIM), lambda s, i: (s, i, 0, 0)),
        compiler_params=_params(("parallel", "parallel")),
        name="nsa_compress",
    )(x, w1, pe, w2)


def _t5_bucket(dist):
    n = jnp.maximum(dist, 0)
    exact = REL_BUCKETS // 2
    nf = jnp.maximum(n, 1).astype(F32)
    large = exact + (jnp.log(nf / exact) / math.log(REL_MAX_DIST / exact)
                     * (REL_BUCKETS - exact)).astype(jnp.int32)
    large = jnp.minimum(large, REL_BUCKETS - 1)
    return jnp.where(n < exact, n, large)


def _bias_body(tab_ref, bc_ref, bt_ref, *, seq, n_cmp_pad):
    h = pl.program_id(0)
    tq = NSA_TQ

    def lookup(dist):
        bucket = _t5_bucket(dist)
        out = jnp.zeros(dist.shape, F32)
        for bkt in range(REL_BUCKETS):
            out = jnp.where(bucket == bkt, tab_ref[bkt, h], out)
        return out

    rows = lax.broadcasted_iota(jnp.int32, (tq, tq), 0)
    cols = lax.broadcasted_iota(jnp.int32, (tq, tq), 1)
    bt_ref[0, BT_DIAG, 0] = jnp.where(rows >= cols, lookup(rows - cols), MASKED)
    bt_ref[0, BT_PREV, 0] = lookup(tq + rows - cols)
    bt_ref[0, BT_FAR, 0] = lookup(2 * tq + rows - cols)
    bt_ref[0, BT_WIN_EDGE, 0] = jnp.where(rows < cols, lookup(WINDOW + rows - cols), MASKED)
    bt_ref[0, BT_NONE, 0] = jnp.full((tq, tq), MASKED, F32)

    rows_c = lax.broadcasted_iota(jnp.int32, (tq, n_cmp_pad), 0)
    cmp_end = lax.broadcasted_iota(jnp.int32, (tq, n_cmp_pad), 1) * CMP_STRIDE + CMP_LEN - 1

    def cmp_tile(i, carry):
        bc_ref[0, pl.ds(pl.multiple_of(i * tq, tq), tq), :] = lookup(i * tq + rows_c - cmp_end)
        return carry

    lax.fori_loop(0, seq // tq, cmp_tile, 0)


def _bias_tables(rel_bias, seq):
    g, hpg, tq = NSA_KV_GROUPS, NSA_HPG, NSA_TQ
    n_cmp_pad = seq // CMP_STRIDE
    return pl.pallas_call(
        functools.partial(_bias_body, seq=seq, n_cmp_pad=n_cmp_pad),
        out_shape=[jax.ShapeDtypeStruct((NSA_HEADS, seq, n_cmp_pad), F32),
                   jax.ShapeDtypeStruct((g, BT_COUNT, hpg, tq, tq), F32)],
        grid=(NSA_HEADS,),
        in_specs=[pl.BlockSpec(memory_space=pltpu.SMEM)],
        out_specs=[pl.BlockSpec((1, seq, n_cmp_pad), lambda h: (h, 0, 0)),
                   pl.BlockSpec((1, BT_COUNT, 1, tq, tq),
                                lambda h: (h // hpg, 0, h % hpg, 0, 0))],
        compiler_params=_params(("parallel",)),
        name="nsa_bias",
    )(rel_bias)


def _nsa_body(q_ref, kc_ref, vc_ref, ks_ref, vs_ref, kw_ref, vw_ref, bc_ref, bt_ref, gate_ref,
              o_ref):
    tq, hpg, dh = NSA_TQ, NSA_HPG, NSA_HEAD_DIM
    rws = hpg * tq
    n_blk_log2 = SEL_BLOCK.bit_length() - 1
    n_blk = ks_ref.shape[1] // SEL_BLOCK
    n_cmp_pad = kc_ref.shape[2]
    i = pl.program_id(2)

    def head_rows(x):
        w = x.shape[1] // hpg
        return jnp.concatenate([x[:, hh * w:(hh + 1) * w] for hh in range(hpg)], axis=0)

    q = (head_rows(q_ref[0]) * (dh ** -0.5)).astype(BF16)

    rowi = lax.broadcasted_iota(jnp.int32, (rws, n_cmp_pad), 0)
    lane = lax.broadcasted_iota(jnp.int32, (rws, n_cmp_pad), 1)
    t_row = i * tq + (rowi & (tq - 1))
    kc = kc_ref[0, 0].astype(BF16)
    vc = vc_ref[0, 0].astype(BF16)
    s = _dot_nt(q, kc) + bc_ref[...].reshape(rws, n_cmp_pad)
    valid = (t_row - (lane * CMP_STRIDE + CMP_LEN - 1) >= 0) & (lane < n_cmp_pad - 1)
    s = jnp.where(valid, s, MASKED)
    e = jnp.where(valid, jnp.exp(s - jnp.max(s, axis=-1, keepdims=True)), 0.0)
    den = jnp.sum(e, axis=-1, keepdims=True)
    p_c = e / jnp.where(den > 0.0, den, 1.0)
    o_cmp = _dot(p_c.astype(BF16), vc)

    p_sum = p_c[0:tq]
    for hh in range(1, hpg):
        p_sum = p_sum + p_c[hh * tq:(hh + 1) * tq]
    blk_o = lax.broadcasted_iota(jnp.int32, (n_blk, n_cmp_pad), 0)
    cmp_o = lax.broadcasted_iota(jnp.int32, (n_blk, n_cmp_pad), 1)
    overlap_t = ((cmp_o * CMP_STRIDE <= blk_o * SEL_BLOCK + SEL_BLOCK - 1)
                 & (cmp_o * CMP_STRIDE + CMP_LEN - 1 >= blk_o * SEL_BLOCK)).astype(F32)
    imp = _dot_nt(overlap_t, p_sum, HI)
    jj = lax.broadcasted_iota(jnp.int32, (n_blk, tq), 0)
    cur = (i * tq + lax.broadcasted_iota(jnp.int32, (n_blk, tq), 1)) >> n_blk_log2
    forced = (jj == 0) | (jj == cur) | (jj == cur - 1)
    imp = jnp.where(jj > cur, -1e6, jnp.where(forced, 1e6, imp))
    rank = jnp.zeros((n_blk, tq), jnp.int32)
    for a in range(n_blk):
        row = imp[a:a + 1, :]
        beats = (row > imp) | ((row == imp) & (a < jj))
        rank = rank + beats.astype(jnp.int32)
    not_sel = jnp.where(rank < SEL_TOP_N, 0.0, 1.0).astype(BF16)
    eye_b = (lax.broadcasted_iota(jnp.int32, (tq, tq), 0)
             == lax.broadcasted_iota(jnp.int32, (tq, tq), 1)).astype(BF16)
    not_sel_t = _dot_nt(eye_b, not_sel).astype(BF16)
    not_sel_rows = jnp.concatenate([not_sel_t] * hpg, axis=0)
    kw = ks_ref.shape[2]
    place_q = (lax.broadcasted_iota(jnp.int32, (dh, kw), 0)
               == lax.broadcasted_iota(jnp.int32, (dh, kw), 1)).astype(BF16)
    place_m = (lax.broadcasted_iota(jnp.int32, (n_blk, kw), 0) + dh
               == lax.broadcasted_iota(jnp.int32, (n_blk, kw), 1)).astype(BF16)
    q_aug = (_dot(q, place_q) + _dot(not_sel_rows, place_m)).astype(BF16)

    def attend(scores, v_tile, carry):
        m_prev, acc = carry
        m_new = jnp.maximum(m_prev, jnp.max(scores, axis=-1, keepdims=True))
        pr = jnp.exp(scores - m_new)
        return m_new, jnp.exp(m_prev - m_new) * acc + _dot(pr.astype(BF16), v_tile)

    init = (jnp.full((rws, 1), MASKED, F32), jnp.zeros((rws, vs_ref.shape[2]), F32))

    def sel_step(kt, carry):
        off = pl.multiple_of(kt * tq, tq)
        bias = bt_ref[0, jnp.minimum(i - kt, BT_FAR)].reshape(rws, tq)
        scores = _dot_nt(q_aug, ks_ref[0, pl.ds(off, tq), :]) + bias
        return attend(scores, vs_ref[0, pl.ds(off, tq), :], carry)

    _, acc_s = lax.fori_loop(0, i + 1, sel_step, init)
    o_sel = acc_s[:, 0:dh] / acc_s[:, dh:dh + 1]

    n_win = WINDOW // tq
    q_pad = _dot(q, place_q).astype(BF16)
    offs, scores = [], []
    for delta in range(n_win + 1):
        entry = {0: BT_DIAG, 1: BT_PREV, n_win: BT_WIN_EDGE}.get(delta, BT_FAR)
        if delta > 0:
            entry = jnp.where(i - delta >= 0, entry, BT_NONE)
        offs.append(pl.multiple_of(jnp.maximum(i - delta, 0) * tq, tq))
        scores.append(_dot_nt(q_pad, kw_ref[0, pl.ds(offs[-1], tq), :])
                      + bt_ref[0, entry].reshape(rws, tq))
    m_all = scores[0]
    for sc in scores[1:]:
        m_all = jnp.maximum(m_all, sc)
    m_w = jnp.max(m_all, axis=-1, keepdims=True)
    acc_w = jnp.zeros((rws, vw_ref.shape[2]), F32)
    for off, sc in zip(offs, scores):
        acc_w = acc_w + _dot(jnp.exp(sc - m_w).astype(BF16), vw_ref[0, pl.ds(off, tq), :])
    o_win = acc_w[:, 0:dh] / acc_w[:, dh:dh + 1]

    gates = jax.nn.sigmoid(gate_ref[0])
    gate = [head_rows(gates[:, br * hpg:(br + 1) * hpg]) for br in range(3)]
    out = gate[0] * o_cmp + gate[1] * o_sel + gate[2] * o_win
    o_ref[0] = jnp.concatenate([out[hh * tq:(hh + 1) * tq] for hh in range(hpg)], axis=1)


def _nsa_attention(proj, kv, kvc, bias_c, bias_t, *, bsz, seq):
    g, tq, hpg, dh = NSA_KV_GROUPS, NSA_TQ, NSA_HPG, NSA_HEAD_DIM
    n_rows = kvc.shape[2]
    group_w = hpg * dh
    kv_spec = lambda kind: pl.BlockSpec((1, seq, LANES), lambda b, gi, i: (b, 0, kind * g + gi))
    kv3 = kv.reshape(bsz, seq, KV_COLS)
    proj3 = proj.reshape(bsz, seq, PROJ_COLS)
    out = pl.pallas_call(
        _nsa_body,
        out_shape=jax.ShapeDtypeStruct((bsz, seq, NSA_DIM), F32),
        grid=(bsz, g, seq // tq),
        in_specs=[
            pl.BlockSpec((1, tq, group_w), lambda b, gi, i: (b, i, COL_QNSA // group_w + gi)),
            pl.BlockSpec((1, 1, n_rows, dh), lambda b, gi, i: (0, b * g + gi, 0, 0)),
            pl.BlockSpec((1, 1, n_rows, dh), lambda b, gi, i: (1, b * g + gi, 0, 0)),
            kv_spec(KIND_K_SEL), kv_spec(KIND_V_SEL), kv_spec(KIND_K_WIN), kv_spec(KIND_V_WIN),
            pl.BlockSpec((hpg, tq, n_rows), lambda b, gi, i: (gi, i, 0)),
            pl.BlockSpec((1, BT_COUNT, hpg, tq, tq), lambda b, gi, i: (gi, 0, 0, 0, 0)),
            pl.BlockSpec((1, tq, GATE_PAD), lambda b, gi, i: (b, i, COL_GNSA // GATE_PAD + gi)),
        ],
        out_specs=pl.BlockSpec((1, tq, group_w), lambda b, gi, i: (b, i, gi)),
        compiler_params=_params(("parallel", "parallel", "arbitrary")),
        name="nsa_attention",
    )(proj3, kvc, kvc, kv3, kv3, kv3, kv3, bias_c, bias_t, proj3)
    return out.reshape(bsz * seq, NSA_DIM)


def _mem_body(q_ref, kv_ref, o_ref):
    outs = []
    for h in range(MEM_HEADS):
        sl = slice(h * MEM_HEAD_DIM, (h + 1) * MEM_HEAD_DIM)
        qh = (q_ref[:, sl] * (MEM_HEAD_DIM ** -0.5)).astype(BF16)
        kh = kv_ref[0, :, sl].astype(BF16)
        vh = kv_ref[0, :, MEM_DIM + h * MEM_HEAD_DIM:MEM_DIM + (h + 1) * MEM_HEAD_DIM].astype(BF16)
        s = _dot_nt(qh, kh)
        e = jnp.exp(s - jnp.max(s, axis=-1, keepdims=True))
        p = e / jnp.sum(e, axis=-1, keepdims=True)
        outs.append(_dot(p.astype(BF16), vh))
    o_ref[...] = jnp.concatenate(outs, axis=1)


def _mem_attention(proj, kv, *, seq, tq=512):
    t = proj.shape[0]
    bsz, m, _ = kv.shape
    per_seq = seq // tq
    return pl.pallas_call(
        _mem_body,
        out_shape=jax.ShapeDtypeStruct((t, MEM_DIM), F32),
        grid=(t // tq,),
        in_specs=[
            pl.BlockSpec((tq, MEM_DIM), lambda i: (i, COL_QMEM // MEM_DIM)),
            pl.BlockSpec((1, m, 2 * MEM_DIM), lambda i: (i // per_seq, 0, 0)),
        ],
        out_specs=pl.BlockSpec((tq, MEM_DIM), lambda i: (i, 0)),
        compiler_params=_params(("parallel",)),
        name="mem_attention",
    )(proj, kv)


def _merge_body(x_ref, yr_ref, yn_ref, ym_ref, g0_ref, g1_ref, g2_ref, wr_ref, wn_ref, wm_ref,
                wo_ref, o_ref):
    merged = (jax.nn.sigmoid(g0_ref[...]) * _dot(yr_ref[...].astype(BF16), wr_ref[...])
              + jax.nn.sigmoid(g1_ref[...]) * _dot(yn_ref[...].astype(BF16), wn_ref[...])
              + jax.nn.sigmoid(g2_ref[...]) * _dot(ym_ref[...].astype(BF16), wm_ref[...]))
    o_ref[...] = x_ref[...] + _dot(merged.astype(BF16), wo_ref[...])


def _merge(x, y_rwkv, y_nsa, y_mem, proj, w_r, w_n, w_m, w_o, *, tm=512):
    t, d = x.shape
    row = lambda i: (i, 0)
    const = lambda i: (0, 0)
    gate0 = COL_GBR // d
    return pl.pallas_call(
        _merge_body,
        out_shape=jax.ShapeDtypeStruct((t, d), F32),
        grid=(t // tm,),
        in_specs=[
            pl.BlockSpec((tm, d), row),
            pl.BlockSpec((tm, RWKV_DIM), row),
            pl.BlockSpec((tm, NSA_DIM), row),
            pl.BlockSpec((tm, MEM_DIM), row),
            pl.BlockSpec((tm, d), lambda i: (i, gate0)),
            pl.BlockSpec((tm, d), lambda i: (i, gate0 + 1)),
            pl.BlockSpec((tm, d), lambda i: (i, gate0 + 2)),
            pl.BlockSpec((RWKV_DIM, d), const),
            pl.BlockSpec((NSA_DIM, d), const),
            pl.BlockSpec((MEM_DIM, d), const),
            pl.BlockSpec((d, d), const),
        ],
        out_specs=pl.BlockSpec((tm, d), row),
        compiler_params=_params(("parallel",)),
        name="merge",
    )(x, y_rwkv, y_nsa, y_mem, proj, proj, proj, w_r, w_n, w_m, w_o)


def _row(a):
    return a.reshape(1, -1)


def _in_proj(x, p, l, seq):
    d = x.shape[1]
    g, hpg, dh = NSA_KV_GROUPS, NSA_HPG, NSA_HEAD_DIM
    w_in = p['w_in'][l]
    o = 0
    parts = {}
    for name, size in (('rwkv', RWKV_PROJ), ('q', NSA_DIM), ('kv', KV_KINDS * NSA_KV_DIM),
                       ('g_nsa', 3 * NSA_HEADS), ('q_mem', MEM_DIM), ('g_br', N_BRANCH * d)):
        parts[name] = w_in[:, o:o + size]
        o += size
    gates = parts['g_nsa'].reshape(d, 3, g, hpg).transpose(0, 2, 1, 3).reshape(d, g, 3 * hpg)
    gates = jnp.pad(gates, ((0, 0), (0, 0), (0, GATE_PAD - 3 * hpg))).reshape(d, g * GATE_PAD)
    w_main = jnp.concatenate([parts['rwkv'], parts['q'], gates, parts['q_mem'], parts['g_br']],
                             axis=1).astype(BF16)
    w_kv = jnp.pad(parts['kv'].reshape(d, KV_KINDS * g, dh), ((0, 0), (0, 0), (0, LANES - dh)))
    w_kv = w_kv.reshape(d, KV_COLS).astype(BF16)
    gain = _row(p['mix_norm'][l])
    proj = _norm_matmul(x, gain, w_main, tm=1024, tn=PROJ_TN, name="in_proj")
    kv = _kv_proj(x, gain, w_kv, seq=seq)
    return proj, kv


def _rwkv_branch(proj, p, l, bsz, seq):
    w_lora = jnp.zeros((LORA_DIM, 3 * RWKV_DIM), F32)
    w_lora = w_lora.at[0:DECAY_LORA, 0:RWKV_DIM].set(p['rwkv_w2'][l])
    w_lora = w_lora.at[DECAY_LORA:DECAY_LORA + AAA_LORA, RWKV_DIM:2 * RWKV_DIM].set(p['rwkv_a2'][l])
    w_lora = w_lora.at[DECAY_LORA + AAA_LORA:, 2 * RWKV_DIM:].set(p['rwkv_g2'][l])
    r, k, v, kk, b, lw, gate = _rwkv_prep(
        proj, seq, _row(p['rwkv_mu'][l]), w_lora.astype(BF16), _row(p['rwkv_w0'][l]),
        _row(p['rwkv_a0'][l]), _row(p['rwkv_k_k'][l]), _row(p['rwkv_k_a'][l]))
    return _rwkv_scan(r, k, v, kk, b, lw, gate, _row(p['rwkv_r_k'][l]),
                      _row(p['rwkv_gn_gain'][l]), _row(p['rwkv_gn_bias'][l]), batch=bsz, seq=seq)


def _nsa_branch(proj, kv, bias_c, bias_t, p, l, bsz, seq):
    g, dh = NSA_KV_GROUPS, NSA_HEAD_DIM
    cmp_in = kv[:, 0:2 * g * LANES].reshape(bsz, seq, 2, g, LANES)[..., :dh]
    cmp_in = cmp_in.transpose(2, 0, 3, 1, 4).reshape(2, bsz * g, seq // CMP_STRIDE, CMP_STRIDE * dh)
    w1 = jnp.stack([p['cmp_k_w1'][l], p['cmp_v_w1'][l]]).astype(BF16)
    pe = jnp.stack([p['cmp_pe_k'][l].reshape(1, -1), p['cmp_pe_v'][l].reshape(1, -1)])
    w2 = jnp.stack([p['cmp_k_w2'][l], p['cmp_v_w2'][l]]).astype(BF16)
    kvc = _compress(cmp_in, w1, pe, w2)
    return _nsa_attention(proj, kv, kvc, bias_c, bias_t, bsz=bsz, seq=seq)


def _mem_branch(proj, mem, p, l):
    bsz, m_tok, d = mem.shape
    w_kv = jnp.concatenate([p['mem_w_k'][l], p['mem_w_v'][l]], axis=1).astype(BF16)
    kv_mem = _norm_matmul(mem.reshape(bsz * m_tok, d), _row(p['mem_norm'][l]), w_kv,
                          tm=min(1024, bsz * m_tok), tn=512, name="mem_kv")
    return _mem_attention(proj, kv_mem.reshape(bsz, m_tok, 2 * MEM_DIM),
                          seq=proj.shape[0] // bsz)


def _layer(x, mem, l, bias_c, bias_t, p):
    bsz, seq, d = x.shape
    t = bsz * seq
    row = _row
    x = x.reshape(t, d)

    x = _ffn(x, row(p['ffn1_norm'][l]), p['ffn1_w_gate'][l].astype(BF16),
             p['ffn1_w_up'][l].astype(BF16), p['ffn1_w_down'][l].astype(BF16),
             row(p['final_norm']), final=False)

    proj, kv = _in_proj(x, p, l, seq)
    y_rwkv = _rwkv_branch(proj, p, l, bsz, seq)
    y_nsa = _nsa_branch(proj, kv, bias_c, bias_t, p, l, bsz, seq)
    y_mem = _mem_branch(proj, mem, p, l)

    x = _merge(x, y_rwkv, y_nsa, y_mem, proj, p['w_br_rwkv'][l].astype(BF16),
               p['w_br_nsa'][l].astype(BF16), p['w_br_mem'][l].astype(BF16),
               p['w_out'][l].astype(BF16))

    last = l == p['ffn1_norm'].shape[0] - 1
    x = _ffn(x, row(p['ffn2_norm'][l]), p['ffn2_w_gate'][l].astype(BF16),
             p['ffn2_w_up'][l].astype(BF16), p['ffn2_w_down'][l].astype(BF16),
             row(p['final_norm']), final=last)
    return x.reshape(bsz, seq, d)


def kernel(x, mem, ffn1_norm, ffn1_w_gate, ffn1_w_up, ffn1_w_down, mix_norm, w_in, rwkv_mu, rwkv_w0, rwkv_w2, rwkv_a0, rwkv_a2, rwkv_g2, rwkv_k_k, rwkv_k_a, rwkv_r_k, rwkv_gn_gain, rwkv_gn_bias, cmp_pe_k, cmp_k_w1, cmp_k_w2, cmp_pe_v, cmp_v_w1, cmp_v_w2, rel_bias, mem_norm, mem_w_k, mem_w_v, w_br_rwkv, w_br_nsa, w_br_mem, w_out, ffn2_norm, ffn2_w_gate, ffn2_w_up, ffn2_w_down, final_norm):
    p = dict(ffn1_norm=ffn1_norm, ffn1_w_gate=ffn1_w_gate, ffn1_w_up=ffn1_w_up,
             ffn1_w_down=ffn1_w_down, mix_norm=mix_norm, w_in=w_in, rwkv_mu=rwkv_mu,
             rwkv_w0=rwkv_w0, rwkv_w2=rwkv_w2, rwkv_a0=rwkv_a0, rwkv_a2=rwkv_a2, rwkv_g2=rwkv_g2,
             rwkv_k_k=rwkv_k_k, rwkv_k_a=rwkv_k_a, rwkv_r_k=rwkv_r_k, rwkv_gn_gain=rwkv_gn_gain,
             rwkv_gn_bias=rwkv_gn_bias, cmp_pe_k=cmp_pe_k, cmp_k_w1=cmp_k_w1, cmp_k_w2=cmp_k_w2,
             cmp_pe_v=cmp_pe_v, cmp_v_w1=cmp_v_w1, cmp_v_w2=cmp_v_w2, mem_norm=mem_norm,
             mem_w_k=mem_w_k, mem_w_v=mem_w_v, w_br_rwkv=w_br_rwkv, w_br_nsa=w_br_nsa,
             w_br_mem=w_br_mem, w_out=w_out, ffn2_norm=ffn2_norm, ffn2_w_gate=ffn2_w_gate,
             ffn2_w_up=ffn2_w_up, ffn2_w_down=ffn2_w_down, final_norm=final_norm)
    bias_c, bias_t = _bias_tables(rel_bias, x.shape[1])
    for l in range(ffn1_norm.shape[0]):
        x = _layer(x, mem, l, bias_c, bias_t, p)
    return x
```

```python
import functools
import math

import jax
import jax.numpy as jnp
from jax import lax
from jax.experimental import pallas as pl
from jax.experimental.pallas import tpu as pltpu

F32 = jnp.float32
BF16 = jnp.bfloat16
HI = lax.Precision.HIGHEST

D_MODEL = 1024
NORM_EPS = 1e-6
D_FF = 2816
RWKV_HEADS = 8
RWKV_HEAD_DIM = 64
RWKV_DIM = RWKV_HEADS * RWKV_HEAD_DIM
DECAY_LORA = 64
AAA_LORA = 64
GATE_LORA = 128
LORA_DIM = DECAY_LORA + AAA_LORA + GATE_LORA
RWKV_GN_EPS = 64e-5
RWKV_PROJ = 3 * RWKV_DIM + LORA_DIM
NSA_HEADS = 8
NSA_KV_GROUPS = 2
NSA_HPG = NSA_HEADS // NSA_KV_GROUPS
NSA_HEAD_DIM = 64
NSA_DIM = NSA_HEADS * NSA_HEAD_DIM
NSA_KV_DIM = NSA_KV_GROUPS * NSA_HEAD_DIM
CMP_LEN = 32
CMP_STRIDE = 16
CMP_HIDDEN = 256
SEL_BLOCK = 64
SEL_TOP_N = 16
WINDOW = 512
REL_BUCKETS = 32
REL_MAX_DIST = 128
MEM_HEADS = 4
MEM_HEAD_DIM = 128
MEM_DIM = MEM_HEADS * MEM_HEAD_DIM
N_BRANCH = 3

LANES = 128
GATE_PAD = LANES
COL_RWKV = 0
COL_QNSA = COL_RWKV + RWKV_PROJ
COL_GNSA = COL_QNSA + NSA_DIM
COL_QMEM = COL_GNSA + NSA_KV_GROUPS * GATE_PAD
COL_GBR = COL_QMEM + MEM_DIM
PROJ_COLS = COL_GBR + N_BRANCH * D_MODEL
PROJ_TN = 768
assert PROJ_COLS % PROJ_TN == 0 and COL_GBR % D_MODEL == 0 and COL_QMEM % MEM_DIM == 0
KV_KINDS = 6
KV_ROW_KINDS = (0, 1, 2, 4)
KV_T_KINDS = (3, 5)
KV_COLS = len(KV_ROW_KINDS) * NSA_KV_GROUPS * LANES
KV_T_ROWS = len(KV_T_KINDS) * NSA_KV_GROUPS * LANES
ROW_K_SEL, ROW_K_WIN = KV_ROW_KINDS.index(2), KV_ROW_KINDS.index(4)
T_V_SEL, T_V_WIN = KV_T_KINDS.index(3), KV_T_KINDS.index(5)
LOG2E = 1.4426950408889634

RWKV_CHUNK = 64
NSA_TQ = 256
MASKED = -1e30
BT_DIAG, BT_PREV, BT_FAR, BT_WIN_EDGE, BT_NONE, BT_COUNT = 0, 1, 2, 3, 4, 5
VMEM_LIMIT = 56 * 1024 * 1024


def _dot(a, b, precision=None):
    return jnp.dot(a, b, preferred_element_type=F32, precision=precision)


def _dot_nt(a, b, precision=None):
    return lax.dot_general(a, b, (((1,), (1,)), ((), ())), preferred_element_type=F32,
                           precision=precision)


def _params(semantics):
    return pltpu.CompilerParams(dimension_semantics=semantics, vmem_limit_bytes=VMEM_LIMIT)


def _rms(x, g):
    return x * lax.rsqrt(jnp.mean(x * x, axis=-1, keepdims=True) + NORM_EPS) * g


def _ffn_body(x_ref, g_ref, wg_ref, wu_ref, wd_ref, fg_ref, o_ref, h_ref, acc_ref, *, nf, final):
    j = pl.program_id(1)

    @pl.when(j == 0)
    def _():
        h_ref[...] = _rms(x_ref[...], g_ref[...]).astype(BF16)
        acc_ref[...] = jnp.zeros_like(acc_ref)

    h = h_ref[...]
    gate = _dot(h, wg_ref[...])
    up = _dot(h, wu_ref[...])
    act = (jax.nn.silu(gate) * up).astype(BF16)
    acc_ref[...] += _dot(act, wd_ref[...])

    @pl.when(j == nf - 1)
    def _():
        y = x_ref[...] + 0.5 * acc_ref[...]
        if final:
            y = _rms(y, fg_ref[...])
        o_ref[...] = y


def _ffn(x, gain, wg, wu, wd, final_gain, *, final, tm=1024, tf=256):
    t, d = x.shape
    f = wg.shape[1]
    nf = f // tf
    return pl.pallas_call(
        functools.partial(_ffn_body, nf=nf, final=final),
        out_shape=jax.ShapeDtypeStruct((t, d), F32),
        grid=(t // tm, nf),
        in_specs=[
            pl.BlockSpec((tm, d), lambda i, j: (i, 0)),
            pl.BlockSpec((1, d), lambda i, j: (0, 0)),
            pl.BlockSpec((d, tf), lambda i, j: (0, j)),
            pl.BlockSpec((d, tf), lambda i, j: (0, j)),
            pl.BlockSpec((tf, d), lambda i, j: (j, 0)),
            pl.BlockSpec((1, d), lambda i, j: (0, 0)),
        ],
        out_specs=pl.BlockSpec((tm, d), lambda i, j: (i, 0)),
        scratch_shapes=[pltpu.VMEM((tm, d), BF16), pltpu.VMEM((tm, d), F32)],
        compiler_params=_params(("parallel", "arbitrary")),
        name="ffn_final" if final else "ffn",
    )(x, gain, wg, wu, wd, final_gain)


def _norm_matmul_body(x_ref, g_ref, w_ref, o_ref, h_ref):
    @pl.when(pl.program_id(1) == 0)
    def _():
        h_ref[...] = _rms(x_ref[...], g_ref[...]).astype(BF16)

    o_ref[...] = _dot(h_ref[...], w_ref[...])


def _norm_matmul(x, gain, w, *, tm, tn, name):
    t, d = x.shape
    n = w.shape[1]
    return pl.pallas_call(
        _norm_matmul_body,
        out_shape=jax.ShapeDtypeStruct((t, n), F32),
        grid=(t // tm, n // tn),
        in_specs=[
            pl.BlockSpec((tm, d), lambda i, j: (i, 0)),
            pl.BlockSpec((1, d), lambda i, j: (0, 0)),
            pl.BlockSpec((d, tn), lambda i, j: (0, j)),
        ],
        out_specs=pl.BlockSpec((tm, tn), lambda i, j: (i, j)),
        scratch_shapes=[pltpu.VMEM((tm, d), BF16)],
        compiler_params=_params(("parallel", "arbitrary")),
        name=name,
    )(x, gain, w)


def _kv_proj_body(x_ref, g_ref, w_ref, wt_ref, o_ref, ot_ref, *, tm, seq):
    dh, tk = NSA_HEAD_DIM, NSA_TQ
    h = _rms(x_ref[...], g_ref[...]).astype(BF16)
    y = _dot(h, w_ref[...])
    row = lax.broadcasted_iota(jnp.int32, (tm, LANES), 0)
    lane = lax.broadcasted_iota(jnp.int32, (tm, LANES), 1)
    pos = (pl.program_id(0) * tm) % seq + row
    block_mark = jnp.where(lane - dh == (pos >> (SEL_BLOCK.bit_length() - 1)), MASKED, 0.0)
    for tile in range(KV_COLS // LANES):
        part = y[:, tile * LANES:(tile + 1) * LANES]
        if tile // NSA_KV_GROUPS == ROW_K_SEL:
            part = part + block_mark
        o_ref[:, tile * LANES:(tile + 1) * LANES] = part.astype(BF16)

    y_t = _dot_nt(wt_ref[...], h)
    row_t = lax.broadcasted_iota(jnp.int32, y_t.shape, 0)
    y_t = (y_t + jnp.where((row_t & (LANES - 1)) == dh, 1.0, 0.0)).astype(BF16)
    for c in range(tm // tk):
        ot_ref[0, c] = y_t[:, c * tk:(c + 1) * tk]


def _kv_proj(x, gain, w, w_t, *, seq, tm=512):
    t, d = x.shape
    tk = NSA_TQ
    per_seq = seq // tm
    return pl.pallas_call(
        functools.partial(_kv_proj_body, tm=tm, seq=seq),
        out_shape=[jax.ShapeDtypeStruct((t, KV_COLS), BF16),
                   jax.ShapeDtypeStruct((t // seq, seq // tk, KV_T_ROWS, tk), BF16)],
        grid=(t // tm,),
        in_specs=[
            pl.BlockSpec((tm, d), lambda i: (i, 0)),
            pl.BlockSpec((1, d), lambda i: (0, 0)),
            pl.BlockSpec((d, KV_COLS), lambda i: (0, 0)),
            pl.BlockSpec((KV_T_ROWS, d), lambda i: (0, 0)),
        ],
        out_specs=[pl.BlockSpec((tm, KV_COLS), lambda i: (i, 0)),
                   pl.BlockSpec((1, tm // tk, KV_T_ROWS, tk),
                                lambda i: (i // per_seq, i % per_seq, 0, 0))],
        compiler_params=_params(("parallel",)),
        name="nsa_kv_proj",
    )(x, gain, w, w_t)


def _rwkv_prep_body(p_ref, prev_ref, mu_ref, wl_ref, w0_ref, a0_ref, kk_ref, ka_ref,
                    r_o, k_o, v_o, kk_o, b_o, lw_o, g_o, *, tm, tiles_per_seq):
    i = pl.program_id(0)
    p = p_ref[...]
    keep = jnp.where(i % tiles_per_seq == 0, 0.0, 1.0)
    prev_last = prev_ref[7:8, :] * keep
    rows = lax.broadcasted_iota(jnp.int32, p.shape, 0)
    shifted = jnp.where(rows == 0, prev_last, pltpu.roll(p, 1, 0))
    x = p + (shifted - p) * mu_ref[...]

    r = x[:, 0:RWKV_DIM]
    k = x[:, RWKV_DIM:2 * RWKV_DIM]
    v = x[:, 2 * RWKV_DIM:3 * RWKV_DIM]
    s = x[:, 3 * RWKV_DIM:RWKV_PROJ]
    lane = lax.broadcasted_iota(jnp.int32, s.shape, 1)
    z = jnp.where(lane < DECAY_LORA, jnp.tanh(s),
                  jnp.where(lane < DECAY_LORA + AAA_LORA, s, jax.nn.sigmoid(s)))
    lo = _dot(z.astype(BF16), wl_ref[...])
    w = -jax.nn.softplus(-(w0_ref[...] + lo[:, 0:RWKV_DIM])) - 0.5
    a = jax.nn.sigmoid(a0_ref[...] + lo[:, RWKV_DIM:2 * RWKV_DIM])
    g = lo[:, 2 * RWKV_DIM:3 * RWKV_DIM]

    kkr = k * kk_ref[...]
    hi = lax.broadcasted_iota(jnp.int32, (RWKV_DIM, RWKV_DIM), 0) // RWKV_HEAD_DIM
    hj = lax.broadcasted_iota(jnp.int32, (RWKV_DIM, RWKV_DIM), 1) // RWKV_HEAD_DIM
    same_head = (hi == hj).astype(F32)
    ssq = _dot(kkr * kkr, same_head, HI)
    kk = kkr / jnp.maximum(jnp.sqrt(ssq), 1e-12)

    r_o[...] = r
    k_o[...] = k * (1.0 + (a - 1.0) * ka_ref[...])
    v_o[...] = v
    kk_o[...] = kk
    b_o[...] = kk * a
    lw_o[...] = -jnp.exp(w)
    g_o[...] = g


def _rwkv_prep(proj, seq, mu, w_lora, w0, a0, k_k, k_a, *, tm=512):
    t = proj.shape[0]
    tiles_per_seq = seq // tm
    row = lambda i: (i, 0)
    const = lambda i: (0, 0)
    out = jax.ShapeDtypeStruct((t, RWKV_DIM), F32)
    return pl.pallas_call(
        functools.partial(_rwkv_prep_body, tm=tm, tiles_per_seq=tiles_per_seq),
        out_shape=[out] * 7,
        grid=(t // tm,),
        in_specs=[
            pl.BlockSpec((tm, RWKV_PROJ), row),
            pl.BlockSpec((8, RWKV_PROJ), lambda i: (jnp.maximum(i * (tm // 8) - 1, 0), 0)),
            pl.BlockSpec((1, RWKV_PROJ), const),
            pl.BlockSpec((LORA_DIM, 3 * RWKV_DIM), const),
            pl.BlockSpec((1, RWKV_DIM), const),
            pl.BlockSpec((1, RWKV_DIM), const),
            pl.BlockSpec((1, RWKV_DIM), const),
            pl.BlockSpec((1, RWKV_DIM), const),
        ],
        out_specs=[pl.BlockSpec((tm, RWKV_DIM), row)] * 7,
        compiler_params=_params(("parallel",)),
        name="rwkv_prep",
    )(proj, proj, mu, w_lora, w0, a0, k_k, k_a)


def _rwkv_scan_body(r_ref, k_ref, v_ref, kk_ref, b_ref, lw_ref, g_ref, rk_ref, gg_ref, gb_ref,
                    o_ref, st_ref):
    c_sz, n, nh = RWKV_CHUNK, RWKV_HEAD_DIM, RWKV_HEADS

    @pl.when(pl.program_id(1) == 0)
    def _():
        st_ref[...] = jnp.zeros_like(st_ref)

    ri = lax.broadcasted_iota(jnp.int32, (c_sz, c_sz), 0)
    ci = lax.broadcasted_iota(jnp.int32, (c_sz, c_sz), 1)
    strict = ci < ri
    incl = ci <= ri
    eye_b = (ci == ri).astype(BF16)
    row2 = lax.broadcasted_iota(jnp.int32, (c_sz, 2 * c_sz), 0)
    lane2 = lax.broadcasted_iota(jnp.int32, (c_sz, 2 * c_sz), 1)
    right_half = lane2 >= c_sz
    zeros_b = jnp.zeros((c_sz, n), BF16)

    lw = lw_ref[...]
    cum = _dot(incl.astype(F32), lw, HI)
    cum_last = cum[c_sz - 1:c_sz, :]
    r = r_ref[...]
    k = k_ref[...]
    v = v_ref[...]
    b = b_ref[...]
    p_inv = jnp.exp(-cum)
    p_end = jnp.exp(cum_last - cum)
    left = jnp.concatenate([(-(kk_ref[...] * jnp.exp(cum - lw))).astype(BF16),
                            (r * jnp.exp(cum)).astype(BF16)], axis=0)
    bt_b = (b * p_inv).astype(BF16)
    kt_b = (k * p_inv).astype(BF16)
    bh_b = (b * p_end).astype(BF16)
    kh_b = (k * p_end).astype(BF16)
    v_b = v.astype(BF16)
    d_p = jnp.exp(cum_last)
    rk_all = r * k * rk_ref[...]

    heads = range(nh)
    sls = [slice(h * n, (h + 1) * n) for h in heads]
    a_l = [_dot_nt(left[:, sls[h]], jnp.concatenate([bt_b[:, sls[h]], zeros_b], axis=0))
           for h in heads]
    a_r = [_dot_nt(left[:, sls[h]], kt_b[:, sls[h]]) for h in heads]
    v_t = [_dot_nt(eye_b, v_b[:, sls[h]]).astype(BF16) for h in heads]

    x = [jnp.where(lane2 < row2, a_l[h][:c_sz], jnp.where(lane2 == row2 + c_sz, 1.0, 0.0))
         for h in heads]
    for _ in range(6):
        hi = [x[h].astype(BF16) for h in heads]
        lo = [(x[h] - hi[h].astype(F32)).astype(BF16) for h in heads]
        x = [_dot(hi[h][:, :c_sz], hi[h]) + _dot(hi[h][:, :c_sz], lo[h])
             + _dot(lo[h][:, :c_sz], hi[h]) + jnp.where(right_half, x[h], 0.0) for h in heads]
    x_b = [x[h].astype(BF16) for h in heads]

    s0 = [st_ref[h] for h in heads]
    s0_b = [s0[h].astype(BF16) for h in heads]
    a_ak = [jnp.where(strict, a_r[h][:c_sz], 0.0).astype(BF16) for h in heads]
    a_rk = [jnp.where(incl, a_r[h][c_sz:], 0.0).astype(BF16) for h in heads]
    a_rb = [jnp.where(incl, a_l[h][c_sz:, :c_sz], 0.0).astype(BF16) for h in heads]
    rhs = [_dot_nt(left[:c_sz, sls[h]], s0_b[h]) + _dot(a_ak[h], v_b[:, sls[h]]) for h in heads]
    u_b = [_dot(x_b[h], jnp.concatenate([zeros_b, rhs[h].astype(BF16)], axis=0)).astype(BF16)
           for h in heads]
    y = [_dot_nt(left[c_sz:, sls[h]], s0_b[h]) + _dot(a_rb[h], u_b[h])
         + _dot(a_rk[h], v_b[:, sls[h]]) for h in heads]
    u_t = [_dot_nt(eye_b, u_b[h]).astype(BF16) for h in heads]
    for h in heads:
        st_ref[h] = (s0[h] * d_p[:, sls[h]] + _dot(u_t[h], bh_b[:, sls[h]])
                     + _dot(v_t[h], kh_b[:, sls[h]]))

    outs = []
    for h in heads:
        mean = jnp.mean(y[h], axis=-1, keepdims=True)
        var = jnp.mean(jnp.square(y[h] - mean), axis=-1, keepdims=True)
        yn = (y[h] - mean) * lax.rsqrt(var + RWKV_GN_EPS)
        yn = yn * gg_ref[:, sls[h]] + gb_ref[:, sls[h]]
        bonus = jnp.sum(rk_all[:, sls[h]], axis=-1, keepdims=True) * v[:, sls[h]]
        outs.append((yn + bonus) * g_ref[:, sls[h]])
    o_ref[...] = jnp.concatenate(outs, axis=1)


def _rwkv_scan(r, k, v, kk, b, lw, g, r_k, gn_gain, gn_bias, *, batch, seq):
    t = r.shape[0]
    nc = seq // RWKV_CHUNK
    row = lambda bi, c: (bi * nc + c, 0)
    const = lambda bi, c: (0, 0)
    tok = pl.BlockSpec((RWKV_CHUNK, RWKV_DIM), row)
    par = pl.BlockSpec((1, RWKV_DIM), const)
    return pl.pallas_call(
        _rwkv_scan_body,
        out_shape=jax.ShapeDtypeStruct((t, RWKV_DIM), F32),
        grid=(batch, nc),
        in_specs=[tok] * 7 + [par] * 3,
        out_specs=tok,
        scratch_shapes=[pltpu.VMEM((RWKV_HEADS, RWKV_HEAD_DIM, RWKV_HEAD_DIM), F32)],
        compiler_params=_params(("parallel", "arbitrary")),
        name="rwkv_scan",
    )(r, k, v, kk, b, lw, g, r_k, gn_gain, gn_bias)


def _compress_body(x_ref, w1_ref, pe_ref, w2_ref, w2t_ref, o_ref, ot_ref):
    half = CMP_STRIDE * NSA_HEAD_DIM
    x = x_ref[0, 0].astype(BF16)
    first = _dot(x, w1_ref[0, 0:half, :])
    second = _dot(x, w1_ref[0, half:2 * half, :])
    n_rows = x.shape[0]
    second_next = pltpu.roll(second, n_rows - 1, 0)
    pe = jnp.broadcast_to(pe_ref[0], (8, 2 * half)).astype(BF16)
    pe_term = _dot(pe, w1_ref[0])[0:1, :]
    hid = first + second_next + pe_term
    act = jax.nn.gelu(hid).astype(BF16)
    o_ref[0, 0] = _dot(act, w2_ref[0])
    ot_ref[0, 0] = _dot_nt(w2t_ref[0], act)


def _compress(x, w1, pe, w2):
    _, bg, rows, width = x.shape
    dh = NSA_HEAD_DIM
    return pl.pallas_call(
        _compress_body,
        out_shape=[jax.ShapeDtypeStruct((2, bg, rows, dh), F32),
                   jax.ShapeDtypeStruct((2, bg, dh, rows), F32)],
        grid=(2, bg),
        in_specs=[
            pl.BlockSpec((1, 1, rows, width), lambda s, i: (s, i, 0, 0)),
            pl.BlockSpec((1, 2 * width, CMP_HIDDEN), lambda s, i: (s, 0, 0)),
            pl.BlockSpec((1, 1, 2 * width), lambda s, i: (s, 0, 0)),
            pl.BlockSpec((1, CMP_HIDDEN, dh), lambda s, i: (s, 0, 0)),
            pl.BlockSpec((1, dh, CMP_HIDDEN), lambda s, i: (s, 0, 0)),
        ],
        out_specs=[pl.BlockSpec((1, 1, rows, dh), lambda s, i: (s, i, 0, 0)),
                   pl.BlockSpec((1, 1, dh, rows), lambda s, i: (s, i, 0, 0))],
        compiler_params=_params(("parallel", "parallel")),
        name="nsa_compress",
    )(x, w1, pe, w2, w2.transpose(0, 2, 1))


def _t5_bucket(dist):
    n = jnp.maximum(dist, 0)
    exact = REL_BUCKETS // 2
    nf = jnp.maximum(n, 1).astype(F32)
    large = exact + (jnp.log(nf / exact) / math.log(REL_MAX_DIST / exact)
                     * (REL_BUCKETS - exact)).astype(jnp.int32)
    large = jnp.minimum(large, REL_BUCKETS - 1)
    return jnp.where(n < exact, n, large)


def _bias_body(tab_ref, bc_ref, bt_ref, *, seq, n_cmp_pad):
    h = pl.program_id(0)
    tq = NSA_TQ

    def lookup(dist):
        bucket = _t5_bucket(dist)
        out = jnp.zeros(dist.shape, F32)
        for bkt in range(REL_BUCKETS):
            out = jnp.where(bucket == bkt, tab_ref[bkt, h] * LOG2E, out)
        return out

    key = lax.broadcasted_iota(jnp.int32, (tq, tq), 0)
    qry = lax.broadcasted_iota(jnp.int32, (tq, tq), 1)
    bt_ref[0, BT_DIAG] = jnp.where(qry >= key, lookup(qry - key), MASKED)
    bt_ref[0, BT_PREV] = lookup(tq + qry - key)
    bt_ref[0, BT_FAR] = lookup(2 * tq + qry - key)
    bt_ref[0, BT_WIN_EDGE] = jnp.where(qry < key, lookup(WINDOW + qry - key), MASKED)
    bt_ref[0, BT_NONE] = jnp.full((tq, tq), MASKED, F32)

    cmp_end = lax.broadcasted_iota(jnp.int32, (n_cmp_pad, tq), 0) * CMP_STRIDE + CMP_LEN - 1
    qry_c = lax.broadcasted_iota(jnp.int32, (n_cmp_pad, tq), 1)

    def cmp_tile(i, carry):
        bc_ref[0, i] = lookup(i * tq + qry_c - cmp_end)
        return carry

    lax.fori_loop(0, seq // tq, cmp_tile, 0)


def _bias_tables(rel_bias, seq):
    g, hpg, tq = NSA_KV_GROUPS, NSA_HPG, NSA_TQ
    n_cmp_pad = seq // CMP_STRIDE
    nq = seq // tq
    return pl.pallas_call(
        functools.partial(_bias_body, seq=seq, n_cmp_pad=n_cmp_pad),
        out_shape=[jax.ShapeDtypeStruct((g, nq, n_cmp_pad, hpg * tq), F32),
                   jax.ShapeDtypeStruct((g, BT_COUNT, tq, hpg * tq), F32)],
        grid=(NSA_HEADS,),
        in_specs=[pl.BlockSpec(memory_space=pltpu.SMEM)],
        out_specs=[pl.BlockSpec((1, nq, n_cmp_pad, tq), lambda h: (h // hpg, 0, 0, h % hpg)),
                   pl.BlockSpec((1, BT_COUNT, tq, tq), lambda h: (h // hpg, 0, 0, h % hpg))],
        compiler_params=_params(("parallel",)),
        name="nsa_bias",
    )(rel_bias)


def _eye(n, dtype):
    return (lax.broadcasted_iota(jnp.int32, (n, n), 0)
            == lax.broadcasted_iota(jnp.int32, (n, n), 1)).astype(dtype)


def _nsa_body(q_ref, kc_ref, vct_ref, ks_ref, vst_ref, kw_ref, vwt_ref, bc_ref, bt_ref, gate_ref,
              o_ref):
    tq, hpg, dh = NSA_TQ, NSA_HPG, NSA_HEAD_DIM
    rws = hpg * tq
    n_blk_log2 = SEL_BLOCK.bit_length() - 1
    n_blk = ks_ref.shape[1] // SEL_BLOCK
    n_cmp_pad = kc_ref.shape[2]
    kw = ks_ref.shape[2]
    i = pl.program_id(2)

    def per_head(x):
        return [x[:, hh * tq:(hh + 1) * tq] for hh in range(hpg)]

    xq = (q_ref[0] * (dh ** -0.5 * LOG2E)).astype(BF16)
    eye_d = _eye(dh, BF16)
    q_t = jnp.concatenate([_dot_nt(eye_d, xq[:, hh * dh:(hh + 1) * dh]) for hh in range(hpg)],
                          axis=1).astype(BF16)

    cmp_id = lax.broadcasted_iota(jnp.int32, (n_cmp_pad, rws), 0)
    t_pos = i * tq + (lax.broadcasted_iota(jnp.int32, (n_cmp_pad, rws), 1) & (tq - 1))
    valid = (t_pos - (cmp_id * CMP_STRIDE + CMP_LEN - 1) >= 0) & (cmp_id < n_cmp_pad - 1)
    s = jnp.where(valid, _dot(kc_ref[0, 0].astype(BF16), q_t) + bc_ref[0, 0], MASKED)
    e = jnp.where(valid, jnp.exp2(s - jnp.max(s, axis=0, keepdims=True)), 0.0)
    den = jnp.sum(e, axis=0, keepdims=True)
    p_c = e / jnp.where(den > 0.0, den, 1.0)
    o_cmp = _dot(vct_ref[0, 0].astype(BF16), p_c.astype(BF16))

    p_heads = per_head(p_c)
    p_sum = p_heads[0]
    for ph in p_heads[1:]:
        p_sum = p_sum + ph
    blk_o = lax.broadcasted_iota(jnp.int32, (n_blk, n_cmp_pad), 0)
    cmp_o = lax.broadcasted_iota(jnp.int32, (n_blk, n_cmp_pad), 1)
    overlap_t = ((cmp_o * CMP_STRIDE <= blk_o * SEL_BLOCK + SEL_BLOCK - 1)
                 & (cmp_o * CMP_STRIDE + CMP_LEN - 1 >= blk_o * SEL_BLOCK)).astype(F32)
    imp = _dot(overlap_t, p_sum, HI)
    jj = lax.broadcasted_iota(jnp.int32, (n_blk, tq), 0)
    cur = (i * tq + lax.broadcasted_iota(jnp.int32, (n_blk, tq), 1)) >> n_blk_log2
    forced = (jj == 0) | (jj == cur) | (jj == cur - 1)
    imp = jnp.where(jj > cur, -1e6, jnp.where(forced, 1e6, imp))
    rank = jnp.zeros((n_blk, tq), jnp.int32)
    for a in range(n_blk):
        row = imp[a:a + 1, :]
        beats = (row > imp) | ((row == imp) & (a < jj))
        rank = rank + beats.astype(jnp.int32)
    not_sel = jnp.where(rank < SEL_TOP_N, 0.0, 1.0).astype(BF16)

    q_aug = jnp.concatenate([q_t, jnp.concatenate([not_sel] * hpg, axis=1),
                             jnp.zeros((kw - dh - n_blk, rws), BF16)], axis=0)
    q_pad = jnp.concatenate([q_t, jnp.zeros((kw - dh, rws), BF16)], axis=0)

    def attend(scores, vt_tile, carry):
        m_prev, acc = carry
        m_new = jnp.maximum(m_prev, jnp.max(scores, axis=0, keepdims=True))
        pr = jnp.exp2(scores - m_new)
        return m_new, jnp.exp2(m_prev - m_new) * acc + _dot(vt_tile, pr.astype(BF16))

    init = (jnp.full((1, rws), MASKED, F32), jnp.zeros((vst_ref.shape[2], rws), F32))

    n_win = WINDOW // tq
    tiles, scores = [], []
    for delta in range(n_win + 1):
        entry = {0: BT_DIAG, 1: BT_PREV, n_win: BT_WIN_EDGE}.get(delta, BT_FAR)
        if delta > 0:
            entry = jnp.where(i - delta >= 0, entry, BT_NONE)
        tiles.append(jnp.maximum(i - delta, 0))
        off = pl.multiple_of(tiles[-1] * tq, tq)
        scores.append(_dot(kw_ref[0, pl.ds(off, tq), :], q_pad) + bt_ref[0, entry])
    m_all = scores[0]
    for sc in scores[1:]:
        m_all = jnp.maximum(m_all, sc)
    m_w = jnp.max(m_all, axis=0, keepdims=True)
    acc_w = jnp.zeros((vwt_ref.shape[2], rws), F32)
    for kt, sc in zip(tiles, scores):
        acc_w = acc_w + _dot(vwt_ref[0, kt], jnp.exp2(sc - m_w).astype(BF16))
    o_win = acc_w[0:dh] / acc_w[dh:dh + 1]

    g_t = _dot_nt(_eye(GATE_PAD, F32), gate_ref[0], HI)
    sig = jax.nn.sigmoid(g_t[0:4 * hpg])
    gate = [jnp.concatenate([sig[br * hpg + hh:br * hpg + hh + 1] for hh in range(hpg)], axis=1)
            for br in range(3)]
    def sel_step(kt, carry):
        off = pl.multiple_of(kt * tq, tq)
        scores = _dot(ks_ref[0, pl.ds(off, tq), :], q_aug) + bt_ref[0, jnp.minimum(i - kt, BT_FAR)]
        return attend(scores, vst_ref[0, kt], carry)

    _, acc_s = lax.fori_loop(0, i + 1, sel_step, init)
    o_sel = acc_s[0:dh] / acc_s[dh:dh + 1]

    y_t = (gate[0] * o_cmp + gate[1] * o_sel + gate[2] * o_win).astype(BF16)
    eye_q = _eye(tq, BF16)
    o_ref[0] = jnp.concatenate([_dot_nt(eye_q, yh) for yh in per_head(y_t)],
                               axis=1).astype(BF16)


def _nsa_attention(proj, kv, kv_t, kvc, kvc_t, bias_c, bias_t, *, bsz, seq):
    g, tq, hpg, dh = NSA_KV_GROUPS, NSA_TQ, NSA_HPG, NSA_HEAD_DIM
    n_rows = kvc.shape[2]
    nq = seq // tq
    group_w = hpg * dh
    kv_spec = lambda pos: pl.BlockSpec((1, seq, LANES), lambda b, gi, i: (b, 0, pos * g + gi))
    kvt_spec = lambda pos: pl.BlockSpec((1, nq, LANES, tq), lambda b, gi, i: (b, 0, pos * g + gi, 0))
    kv3 = kv.reshape(bsz, seq, KV_COLS)
    proj3 = proj.reshape(bsz, seq, PROJ_COLS)
    out = pl.pallas_call(
        _nsa_body,
        out_shape=jax.ShapeDtypeStruct((bsz, seq, NSA_DIM), BF16),
        grid=(bsz, g, nq),
        in_specs=[
            pl.BlockSpec((1, tq, group_w), lambda b, gi, i: (b, i, COL_QNSA // group_w + gi)),
            pl.BlockSpec((1, 1, n_rows, dh), lambda b, gi, i: (0, b * g + gi, 0, 0)),
            pl.BlockSpec((1, 1, dh, n_rows), lambda b, gi, i: (1, b * g + gi, 0, 0)),
            kv_spec(ROW_K_SEL), kvt_spec(T_V_SEL), kv_spec(ROW_K_WIN), kvt_spec(T_V_WIN),
            pl.BlockSpec((1, 1, n_rows, hpg * tq), lambda b, gi, i: (gi, i, 0, 0)),
            pl.BlockSpec((1, BT_COUNT, tq, hpg * tq), lambda b, gi, i: (gi, 0, 0, 0)),
            pl.BlockSpec((1, tq, GATE_PAD), lambda b, gi, i: (b, i, COL_GNSA // GATE_PAD + gi)),
        ],
        out_specs=pl.BlockSpec((1, tq, group_w), lambda b, gi, i: (b, i, gi)),
        compiler_params=_params(("parallel", "parallel", "arbitrary")),
        name="nsa_attention",
    )(proj3, kvc, kvc_t, kv3, kv_t, kv3, kv_t, bias_c, bias_t, proj3)
    return out.reshape(bsz * seq, NSA_DIM)


def _mem_body(q_ref, kv_ref, o_ref):
    outs = []
    for h in range(MEM_HEADS):
        sl = slice(h * MEM_HEAD_DIM, (h + 1) * MEM_HEAD_DIM)
        qh = (q_ref[:, sl] * (MEM_HEAD_DIM ** -0.5)).astype(BF16)
        kh = kv_ref[0, :, sl].astype(BF16)
        vh = kv_ref[0, :, MEM_DIM + h * MEM_HEAD_DIM:MEM_DIM + (h + 1) * MEM_HEAD_DIM].astype(BF16)
        s = _dot_nt(qh, kh)
        e = jnp.exp(s - jnp.max(s, axis=-1, keepdims=True))
        p = e / jnp.sum(e, axis=-1, keepdims=True)
        outs.append(_dot(p.astype(BF16), vh))
    o_ref[...] = jnp.concatenate(outs, axis=1)


def _mem_attention(proj, kv, *, seq, tq=512):
    t = proj.shape[0]
    bsz, m, _ = kv.shape
    per_seq = seq // tq
    return pl.pallas_call(
        _mem_body,
        out_shape=jax.ShapeDtypeStruct((t, MEM_DIM), F32),
        grid=(t // tq,),
        in_specs=[
            pl.BlockSpec((tq, MEM_DIM), lambda i: (i, COL_QMEM // MEM_DIM)),
            pl.BlockSpec((1, m, 2 * MEM_DIM), lambda i: (i // per_seq, 0, 0)),
        ],
        out_specs=pl.BlockSpec((tq, MEM_DIM), lambda i: (i, 0)),
        compiler_params=_params(("parallel",)),
        name="mem_attention",
    )(proj, kv)


def _merge_body(x_ref, yr_ref, yn_ref, ym_ref, g0_ref, g1_ref, g2_ref, wr_ref, wn_ref, wm_ref,
                wo_ref, o_ref):
    merged = (jax.nn.sigmoid(g0_ref[...]) * _dot(yr_ref[...].astype(BF16), wr_ref[...])
              + jax.nn.sigmoid(g1_ref[...]) * _dot(yn_ref[...], wn_ref[...])
              + jax.nn.sigmoid(g2_ref[...]) * _dot(ym_ref[...].astype(BF16), wm_ref[...]))
    o_ref[...] = x_ref[...] + _dot(merged.astype(BF16), wo_ref[...])


def _merge(x, y_rwkv, y_nsa, y_mem, proj, w_r, w_n, w_m, w_o, *, tm=512):
    t, d = x.shape
    row = lambda i: (i, 0)
    const = lambda i: (0, 0)
    gate0 = COL_GBR // d
    return pl.pallas_call(
        _merge_body,
        out_shape=jax.ShapeDtypeStruct((t, d), F32),
        grid=(t // tm,),
        in_specs=[
            pl.BlockSpec((tm, d), row),
            pl.BlockSpec((tm, RWKV_DIM), row),
            pl.BlockSpec((tm, NSA_DIM), row),
            pl.BlockSpec((tm, MEM_DIM), row),
            pl.BlockSpec((tm, d), lambda i: (i, gate0)),
            pl.BlockSpec((tm, d), lambda i: (i, gate0 + 1)),
            pl.BlockSpec((tm, d), lambda i: (i, gate0 + 2)),
            pl.BlockSpec((RWKV_DIM, d), const),
            pl.BlockSpec((NSA_DIM, d), const),
            pl.BlockSpec((MEM_DIM, d), const),
            pl.BlockSpec((d, d), const),
        ],
        out_specs=pl.BlockSpec((tm, d), row),
        compiler_params=_params(("parallel",)),
        name="merge",
    )(x, y_rwkv, y_nsa, y_mem, proj, proj, proj, w_r, w_n, w_m, w_o)


def _row(a):
    return a.reshape(1, -1)


def _in_proj(x, p, l, seq):
    d = x.shape[1]
    g, hpg, dh = NSA_KV_GROUPS, NSA_HPG, NSA_HEAD_DIM
    w_in = p['w_in'][l]
    o = 0
    parts = {}
    for name, size in (('rwkv', RWKV_PROJ), ('q', NSA_DIM), ('kv', KV_KINDS * NSA_KV_DIM),
                       ('g_nsa', 3 * NSA_HEADS), ('q_mem', MEM_DIM), ('g_br', N_BRANCH * d)):
        parts[name] = w_in[:, o:o + size]
        o += size
    gates = parts['g_nsa'].reshape(d, 3, g, hpg).transpose(0, 2, 1, 3).reshape(d, g, 3 * hpg)
    gates = jnp.pad(gates, ((0, 0), (0, 0), (0, GATE_PAD - 3 * hpg))).reshape(d, g * GATE_PAD)
    w_main = jnp.concatenate([parts['rwkv'], parts['q'], gates, parts['q_mem'], parts['g_br']],
                             axis=1).astype(BF16)
    w_kv = jnp.pad(parts['kv'].reshape(d, KV_KINDS, g, dh), ((0, 0), (0, 0), (0, 0), (0, LANES - dh)))
    w_row = w_kv[:, KV_ROW_KINDS, :, :].reshape(d, KV_COLS).astype(BF16)
    w_t = w_kv[:, KV_T_KINDS, :, :].reshape(d, KV_T_ROWS).T.astype(BF16)
    gain = _row(p['mix_norm'][l])
    proj = _norm_matmul(x, gain, w_main, tm=1024, tn=PROJ_TN, name="in_proj")
    kv, kv_t = _kv_proj(x, gain, w_row, w_t, seq=seq)
    return proj, kv, kv_t


def _rwkv_branch(proj, p, l, bsz, seq):
    w_lora = jnp.zeros((LORA_DIM, 3 * RWKV_DIM), F32)
    w_lora = w_lora.at[0:DECAY_LORA, 0:RWKV_DIM].set(p['rwkv_w2'][l])
    w_lora = w_lora.at[DECAY_LORA:DECAY_LORA + AAA_LORA, RWKV_DIM:2 * RWKV_DIM].set(p['rwkv_a2'][l])
    w_lora = w_lora.at[DECAY_LORA + AAA_LORA:, 2 * RWKV_DIM:].set(p['rwkv_g2'][l])
    r, k, v, kk, b, lw, gate = _rwkv_prep(
        proj, seq, _row(p['rwkv_mu'][l]), w_lora.astype(BF16), _row(p['rwkv_w0'][l]),
        _row(p['rwkv_a0'][l]), _row(p['rwkv_k_k'][l]), _row(p['rwkv_k_a'][l]))
    return _rwkv_scan(r, k, v, kk, b, lw, gate, _row(p['rwkv_r_k'][l]),
                      _row(p['rwkv_gn_gain'][l]), _row(p['rwkv_gn_bias'][l]), batch=bsz, seq=seq)


def _nsa_branch(proj, kv, kv_t, bias_c, bias_t, p, l, bsz, seq):
    g, dh = NSA_KV_GROUPS, NSA_HEAD_DIM
    cmp_in = kv[:, 0:2 * g * LANES].reshape(bsz, seq, 2, g, LANES)[..., :dh]
    cmp_in = cmp_in.transpose(2, 0, 3, 1, 4).reshape(2, bsz * g, seq // CMP_STRIDE, CMP_STRIDE * dh)
    w1 = jnp.stack([p['cmp_k_w1'][l], p['cmp_v_w1'][l]]).astype(BF16)
    pe = jnp.stack([p['cmp_pe_k'][l].reshape(1, -1), p['cmp_pe_v'][l].reshape(1, -1)])
    w2 = jnp.stack([p['cmp_k_w2'][l], p['cmp_v_w2'][l]]).astype(BF16)
    kvc, kvc_t = _compress(cmp_in, w1, pe, w2)
    return _nsa_attention(proj, kv, kv_t, kvc, kvc_t, bias_c, bias_t, bsz=bsz, seq=seq)


def _mem_branch(proj, mem, p, l):
    bsz, m_tok, d = mem.shape
    w_kv = jnp.concatenate([p['mem_w_k'][l], p['mem_w_v'][l]], axis=1).astype(BF16)
    kv_mem = _norm_matmul(mem.reshape(bsz * m_tok, d), _row(p['mem_norm'][l]), w_kv,
                          tm=min(1024, bsz * m_tok), tn=512, name="mem_kv")
    return _mem_attention(proj, kv_mem.reshape(bsz, m_tok, 2 * MEM_DIM),
                          seq=proj.shape[0] // bsz)


def _layer(x, mem, l, bias_c, bias_t, p):
    bsz, seq, d = x.shape
    t = bsz * seq
    row = _row
    x = x.reshape(t, d)

    x = _ffn(x, row(p['ffn1_norm'][l]), p['ffn1_w_gate'][l].astype(BF16),
             p['ffn1_w_up'][l].astype(BF16), p['ffn1_w_down'][l].astype(BF16),
             row(p['final_norm']), final=False)

    proj, kv, kv_t = _in_proj(x, p, l, seq)
    y_rwkv = _rwkv_branch(proj, p, l, bsz, seq)
    y_nsa = _nsa_branch(proj, kv, kv_t, bias_c, bias_t, p, l, bsz, seq)
    y_mem = _mem_branch(proj, mem, p, l)

    x = _merge(x, y_rwkv, y_nsa, y_mem, proj, p['w_br_rwkv'][l].astype(BF16),
               p['w_br_nsa'][l].astype(BF16), p['w_br_mem'][l].astype(BF16),
               p['w_out'][l].astype(BF16))

    last = l == p['ffn1_norm'].shape[0] - 1
    x = _ffn(x, row(p['ffn2_norm'][l]), p['ffn2_w_gate'][l].astype(BF16),
             p['ffn2_w_up'][l].astype(BF16), p['ffn2_w_down'][l].astype(BF16),
             row(p['final_norm']), final=last)
    return x.reshape(bsz, seq, d)


def kernel(x, mem, ffn1_norm, ffn1_w_gate, ffn1_w_up, ffn1_w_down, mix_norm, w_in, rwkv_mu, rwkv_w0, rwkv_w2, rwkv_a0, rwkv_a2, rwkv_g2, rwkv_k_k, rwkv_k_a, rwkv_r_k, rwkv_gn_gain, rwkv_gn_bias, cmp_pe_k, cmp_k_w1, cmp_k_w2, cmp_pe_v, cmp_v_w1, cmp_v_w2, rel_bias, mem_norm, mem_w_k, mem_w_v, w_br_rwkv, w_br_nsa, w_br_mem, w_out, ffn2_norm, ffn2_w_gate, ffn2_w_up, ffn2_w_down, final_norm):
    p = dict(ffn1_norm=ffn1_norm, ffn1_w_gate=ffn1_w_gate, ffn1_w_up=ffn1_w_up,
             ffn1_w_down=ffn1_w_down, mix_norm=mix_norm, w_in=w_in, rwkv_mu=rwkv_mu,
             rwkv_w0=rwkv_w0, rwkv_w2=rwkv_w2, rwkv_a0=rwkv_a0, rwkv_a2=rwkv_a2, rwkv_g2=rwkv_g2,
             rwkv_k_k=rwkv_k_k, rwkv_k_a=rwkv_k_a, rwkv_r_k=rwkv_r_k, rwkv_gn_gain=rwkv_gn_gain,
             rwkv_gn_bias=rwkv_gn_bias, cmp_pe_k=cmp_pe_k, cmp_k_w1=cmp_k_w1, cmp_k_w2=cmp_k_w2,
             cmp_pe_v=cmp_pe_v, cmp_v_w1=cmp_v_w1, cmp_v_w2=cmp_v_w2, mem_norm=mem_norm,
             mem_w_k=mem_w_k, mem_w_v=mem_w_v, w_br_rwkv=w_br_rwkv, w_br_nsa=w_br_nsa,
             w_br_mem=w_br_mem, w_out=w_out, ffn2_norm=ffn2_norm, ffn2_w_gate=ffn2_w_gate,
             ffn2_w_up=ffn2_w_up, ffn2_w_down=ffn2_w_down, final_norm=final_norm)
    bias_c, bias_t = _bias_tables(rel_bias, x.shape[1])
    for l in range(ffn1_norm.shape[0]):
        x = _layer(x, mem, l, bias_c, bias_t, p)
    return x
```

```python
import functools
import math

import jax
import jax.numpy as jnp
from jax import lax
from jax.experimental import pallas as pl
from jax.experimental.pallas import tpu as pltpu

F32 = jnp.float32
BF16 = jnp.bfloat16
HI = lax.Precision.HIGHEST

D_MODEL = 1024
NORM_EPS = 1e-6
D_FF = 2816
RWKV_HEADS = 8
RWKV_HEAD_DIM = 64
RWKV_DIM = RWKV_HEADS * RWKV_HEAD_DIM
DECAY_LORA = 64
AAA_LORA = 64
GATE_LORA = 128
LORA_DIM = DECAY_LORA + AAA_LORA + GATE_LORA
RWKV_GN_EPS = 64e-5
RWKV_PROJ = 3 * RWKV_DIM + LORA_DIM
NSA_HEADS = 8
NSA_KV_GROUPS = 2
NSA_HPG = NSA_HEADS // NSA_KV_GROUPS
NSA_HEAD_DIM = 64
NSA_DIM = NSA_HEADS * NSA_HEAD_DIM
NSA_KV_DIM = NSA_KV_GROUPS * NSA_HEAD_DIM
CMP_LEN = 32
CMP_STRIDE = 16
CMP_HIDDEN = 256
SEL_BLOCK = 64
SEL_TOP_N = 16
WINDOW = 512
REL_BUCKETS = 32
REL_MAX_DIST = 128
MEM_HEADS = 4
MEM_HEAD_DIM = 128
MEM_DIM = MEM_HEADS * MEM_HEAD_DIM
N_BRANCH = 3

LANES = 128
GATE_PAD = LANES
COL_RWKV = 0
COL_QNSA = COL_RWKV + RWKV_PROJ
COL_GNSA = COL_QNSA + NSA_DIM
COL_QMEM = COL_GNSA + NSA_KV_GROUPS * GATE_PAD
PROJ_COLS = COL_QMEM + MEM_DIM
PROJ_TN = 768
assert PROJ_COLS % PROJ_TN == 0 and COL_QMEM % MEM_DIM == 0
KV_KINDS = 6
KV_ROW_KINDS = (0, 1, 2, 4)
KV_T_KINDS = (3, 5)
KV_COLS = len(KV_ROW_KINDS) * NSA_KV_GROUPS * LANES
KV_T_ROWS = len(KV_T_KINDS) * NSA_KV_GROUPS * LANES
ROW_K_SEL, ROW_K_WIN = KV_ROW_KINDS.index(2), KV_ROW_KINDS.index(4)
T_V_SEL, T_V_WIN = KV_T_KINDS.index(3), KV_T_KINDS.index(5)
LOG2E = 1.4426950408889634

RWKV_CHUNK = 64
NSA_TQ = 256
MASKED = -1e30
BT_DIAG, BT_PREV, BT_FAR, BT_WIN_EDGE, BT_NONE, BT_COUNT = 0, 1, 2, 3, 4, 5
VMEM_LIMIT = 56 * 1024 * 1024


def _dot(a, b, precision=None):
    return jnp.dot(a, b, preferred_element_type=F32, precision=precision)


def _dot_nt(a, b, precision=None):
    return lax.dot_general(a, b, (((1,), (1,)), ((), ())), preferred_element_type=F32,
                           precision=precision)


def _params(semantics):
    return pltpu.CompilerParams(dimension_semantics=semantics, vmem_limit_bytes=VMEM_LIMIT)


def _rms(x, g):
    return x * lax.rsqrt(jnp.mean(x * x, axis=-1, keepdims=True) + NORM_EPS) * g


def _ffn_body(x_ref, g_ref, wg_ref, wu_ref, wd_ref, fg_ref, o_ref, h_ref, acc_ref, *, nf, final):
    j = pl.program_id(1)

    @pl.when(j == 0)
    def _():
        h_ref[...] = _rms(x_ref[...], g_ref[...]).astype(BF16)
        acc_ref[...] = jnp.zeros_like(acc_ref)

    h = h_ref[...]
    gate = _dot(h, wg_ref[...])
    up = _dot(h, wu_ref[...])
    act = (jax.nn.silu(gate) * up).astype(BF16)
    acc_ref[...] += _dot(act, wd_ref[...])

    @pl.when(j == nf - 1)
    def _():
        y = x_ref[...] + 0.5 * acc_ref[...]
        if final:
            y = _rms(y, fg_ref[...])
        o_ref[...] = y


def _ffn(x, gain, wg, wu, wd, final_gain, *, final, tm=1024, tf=256):
    t, d = x.shape
    f = wg.shape[1]
    nf = f // tf
    return pl.pallas_call(
        functools.partial(_ffn_body, nf=nf, final=final),
        out_shape=jax.ShapeDtypeStruct((t, d), F32),
        grid=(t // tm, nf),
        in_specs=[
            pl.BlockSpec((tm, d), lambda i, j: (i, 0)),
            pl.BlockSpec((1, d), lambda i, j: (0, 0)),
            pl.BlockSpec((d, tf), lambda i, j: (0, j)),
            pl.BlockSpec((d, tf), lambda i, j: (0, j)),
            pl.BlockSpec((tf, d), lambda i, j: (j, 0)),
            pl.BlockSpec((1, d), lambda i, j: (0, 0)),
        ],
        out_specs=pl.BlockSpec((tm, d), lambda i, j: (i, 0)),
        scratch_shapes=[pltpu.VMEM((tm, d), BF16), pltpu.VMEM((tm, d), F32)],
        compiler_params=_params(("parallel", "arbitrary")),
        name="ffn_final" if final else "ffn",
    )(x, gain, wg, wu, wd, final_gain)


def _norm_matmul_body(x_ref, g_ref, w_ref, o_ref, h_ref):
    @pl.when(pl.program_id(1) == 0)
    def _():
        h_ref[...] = _rms(x_ref[...], g_ref[...]).astype(BF16)

    o_ref[...] = _dot(h_ref[...], w_ref[...])


def _norm_matmul(x, gain, w, *, tm, tn, name):
    t, d = x.shape
    n = w.shape[1]
    return pl.pallas_call(
        _norm_matmul_body,
        out_shape=jax.ShapeDtypeStruct((t, n), F32),
        grid=(t // tm, n // tn),
        in_specs=[
            pl.BlockSpec((tm, d), lambda i, j: (i, 0)),
            pl.BlockSpec((1, d), lambda i, j: (0, 0)),
            pl.BlockSpec((d, tn), lambda i, j: (0, j)),
        ],
        out_specs=pl.BlockSpec((tm, tn), lambda i, j: (i, j)),
        scratch_shapes=[pltpu.VMEM((tm, d), BF16)],
        compiler_params=_params(("parallel", "arbitrary")),
        name=name,
    )(x, gain, w)


def _kv_proj_body(x_ref, g_ref, w_ref, wt_ref, o_ref, ot_ref, *, tm, seq):
    dh, tk = NSA_HEAD_DIM, NSA_TQ
    h = _rms(x_ref[...], g_ref[...]).astype(BF16)
    y = _dot(h, w_ref[...])
    row = lax.broadcasted_iota(jnp.int32, (tm, LANES), 0)
    lane = lax.broadcasted_iota(jnp.int32, (tm, LANES), 1)
    pos = (pl.program_id(0) * tm) % seq + row
    block_mark = jnp.where(lane - dh == (pos >> (SEL_BLOCK.bit_length() - 1)), MASKED, 0.0)
    for tile in range(KV_COLS // LANES):
        part = y[:, tile * LANES:(tile + 1) * LANES]
        if tile // NSA_KV_GROUPS == ROW_K_SEL:
            part = part + block_mark
        o_ref[:, tile * LANES:(tile + 1) * LANES] = part.astype(BF16)

    y_t = _dot_nt(wt_ref[...], h)
    row_t = lax.broadcasted_iota(jnp.int32, y_t.shape, 0)
    y_t = (y_t + jnp.where((row_t & (LANES - 1)) == dh, 1.0, 0.0)).astype(BF16)
    for c in range(tm // tk):
        ot_ref[0, c] = y_t[:, c * tk:(c + 1) * tk]


def _kv_proj(x, gain, w, w_t, *, seq, tm=512):
    t, d = x.shape
    tk = NSA_TQ
    per_seq = seq // tm
    return pl.pallas_call(
        functools.partial(_kv_proj_body, tm=tm, seq=seq),
        out_shape=[jax.ShapeDtypeStruct((t, KV_COLS), BF16),
                   jax.ShapeDtypeStruct((t // seq, seq // tk, KV_T_ROWS, tk), BF16)],
        grid=(t // tm,),
        in_specs=[
            pl.BlockSpec((tm, d), lambda i: (i, 0)),
            pl.BlockSpec((1, d), lambda i: (0, 0)),
            pl.BlockSpec((d, KV_COLS), lambda i: (0, 0)),
            pl.BlockSpec((KV_T_ROWS, d), lambda i: (0, 0)),
        ],
        out_specs=[pl.BlockSpec((tm, KV_COLS), lambda i: (i, 0)),
                   pl.BlockSpec((1, tm // tk, KV_T_ROWS, tk),
                                lambda i: (i // per_seq, i % per_seq, 0, 0))],
        compiler_params=_params(("parallel",)),
        name="nsa_kv_proj",
    )(x, gain, w, w_t)


def _rwkv_prep_body(p_ref, prev_ref, mu_ref, wl_ref, w0_ref, a0_ref, kk_ref, ka_ref,
                    r_o, k_o, v_o, kk_o, b_o, lw_o, g_o, *, tm, tiles_per_seq):
    i = pl.program_id(0)
    p = p_ref[...]
    keep = jnp.where(i % tiles_per_seq == 0, 0.0, 1.0)
    prev_last = prev_ref[7:8, :] * keep
    rows = lax.broadcasted_iota(jnp.int32, p.shape, 0)
    shifted = jnp.where(rows == 0, prev_last, pltpu.roll(p, 1, 0))
    x = p + (shifted - p) * mu_ref[...]

    r = x[:, 0:RWKV_DIM]
    k = x[:, RWKV_DIM:2 * RWKV_DIM]
    v = x[:, 2 * RWKV_DIM:3 * RWKV_DIM]
    s = x[:, 3 * RWKV_DIM:RWKV_PROJ]
    lane = lax.broadcasted_iota(jnp.int32, s.shape, 1)
    z = jnp.where(lane < DECAY_LORA, jnp.tanh(s),
                  jnp.where(lane < DECAY_LORA + AAA_LORA, s, jax.nn.sigmoid(s)))
    lo = _dot(z.astype(BF16), wl_ref[...])
    w = -jax.nn.softplus(-(w0_ref[...] + lo[:, 0:RWKV_DIM])) - 0.5
    a = jax.nn.sigmoid(a0_ref[...] + lo[:, RWKV_DIM:2 * RWKV_DIM])
    g = lo[:, 2 * RWKV_DIM:3 * RWKV_DIM]

    kkr = k * kk_ref[...]
    hi = lax.broadcasted_iota(jnp.int32, (RWKV_DIM, RWKV_DIM), 0) // RWKV_HEAD_DIM
    hj = lax.broadcasted_iota(jnp.int32, (RWKV_DIM, RWKV_DIM), 1) // RWKV_HEAD_DIM
    same_head = (hi == hj).astype(F32)
    ssq = _dot(kkr * kkr, same_head, HI)
    kk = kkr / jnp.maximum(jnp.sqrt(ssq), 1e-12)

    r_o[...] = r
    k_o[...] = k * (1.0 + (a - 1.0) * ka_ref[...])
    v_o[...] = v
    kk_o[...] = kk
    b_o[...] = kk * a
    lw_o[...] = -jnp.exp(w)
    g_o[...] = g


def _rwkv_prep(proj, seq, mu, w_lora, w0, a0, k_k, k_a, *, tm=512):
    t = proj.shape[0]
    tiles_per_seq = seq // tm
    row = lambda i: (i, 0)
    const = lambda i: (0, 0)
    out = jax.ShapeDtypeStruct((t, RWKV_DIM), F32)
    return pl.pallas_call(
        functools.partial(_rwkv_prep_body, tm=tm, tiles_per_seq=tiles_per_seq),
        out_shape=[out] * 7,
        grid=(t // tm,),
        in_specs=[
            pl.BlockSpec((tm, RWKV_PROJ), row),
            pl.BlockSpec((8, RWKV_PROJ), lambda i: (jnp.maximum(i * (tm // 8) - 1, 0), 0)),
            pl.BlockSpec((1, RWKV_PROJ), const),
            pl.BlockSpec((LORA_DIM, 3 * RWKV_DIM), const),
            pl.BlockSpec((1, RWKV_DIM), const),
            pl.BlockSpec((1, RWKV_DIM), const),
            pl.BlockSpec((1, RWKV_DIM), const),
            pl.BlockSpec((1, RWKV_DIM), const),
        ],
        out_specs=[pl.BlockSpec((tm, RWKV_DIM), row)] * 7,
        compiler_params=_params(("parallel",)),
        name="rwkv_prep",
    )(proj, proj, mu, w_lora, w0, a0, k_k, k_a)


def _rwkv_scan_body(r_ref, k_ref, v_ref, kk_ref, b_ref, lw_ref, g_ref, rk_ref, gg_ref, gb_ref,
                    o_ref, st_ref):
    c_sz, n, nh = RWKV_CHUNK, RWKV_HEAD_DIM, RWKV_HEADS

    @pl.when(pl.program_id(1) == 0)
    def _():
        st_ref[...] = jnp.zeros_like(st_ref)

    ri = lax.broadcasted_iota(jnp.int32, (c_sz, c_sz), 0)
    ci = lax.broadcasted_iota(jnp.int32, (c_sz, c_sz), 1)
    strict = ci < ri
    incl = ci <= ri
    eye_b = (ci == ri).astype(BF16)
    row2 = lax.broadcasted_iota(jnp.int32, (c_sz, 2 * c_sz), 0)
    lane2 = lax.broadcasted_iota(jnp.int32, (c_sz, 2 * c_sz), 1)
    right_half = lane2 >= c_sz
    zeros_b = jnp.zeros((c_sz, n), BF16)

    rows = []
    for bb in range(st_ref.shape[0]):
        lw = lw_ref[bb]
        cum = _dot(incl.astype(F32), lw, HI)
        cum_last = cum[c_sz - 1:c_sz, :]
        r, k, v, b = r_ref[bb], k_ref[bb], v_ref[bb], b_ref[bb]
        p_inv = jnp.exp(-cum)
        p_end = jnp.exp(cum_last - cum)
        rows.append(dict(
            left=jnp.concatenate([(-(kk_ref[bb] * jnp.exp(cum - lw))).astype(BF16),
                                  (r * jnp.exp(cum)).astype(BF16)], axis=0),
            bt=(b * p_inv).astype(BF16), kt=(k * p_inv).astype(BF16),
            bh=(b * p_end).astype(BF16), kh=(k * p_end).astype(BF16),
            v=v, v_b=v.astype(BF16), d_p=jnp.exp(cum_last), rk=r * k * rk_ref[...]))

    units = [(bb, h) for bb in range(len(rows)) for h in range(nh)]
    col = lambda name, u: rows[u[0]][name][:, u[1] * n:(u[1] + 1) * n]
    a_l = [_dot_nt(col('left', u), jnp.concatenate([col('bt', u), zeros_b], axis=0)) for u in units]
    a_r = [_dot_nt(col('left', u), col('kt', u)) for u in units]
    v_t = [_dot_nt(eye_b, col('v_b', u)).astype(BF16) for u in units]

    x = [jnp.where(lane2 < row2, al[:c_sz], jnp.where(lane2 == row2 + c_sz, 1.0, 0.0)) for al in a_l]
    for _ in range(6):
        hi = [xu.astype(BF16) for xu in x]
        lo = [(xu - h_.astype(F32)).astype(BF16) for xu, h_ in zip(x, hi)]
        x = [_dot(h_[:, :c_sz], h_) + _dot(h_[:, :c_sz], l_) + _dot(l_[:, :c_sz], h_)
             + jnp.where(right_half, xu, 0.0) for xu, h_, l_ in zip(x, hi, lo)]
    x_b = [xu.astype(BF16) for xu in x]

    s0 = [st_ref[bb, h] for bb, h in units]
    s0_b = [s.astype(BF16) for s in s0]
    a_ak = [jnp.where(strict, ar[:c_sz], 0.0).astype(BF16) for ar in a_r]
    a_rk = [jnp.where(incl, ar[c_sz:], 0.0).astype(BF16) for ar in a_r]
    a_rb = [jnp.where(incl, al[c_sz:, :c_sz], 0.0).astype(BF16) for al in a_l]
    rhs = [_dot_nt(col('left', u)[:c_sz], s0_b[j]) + _dot(a_ak[j], col('v_b', u))
           for j, u in enumerate(units)]
    u_b = [_dot(x_b[j], jnp.concatenate([zeros_b, rhs[j].astype(BF16)], axis=0)).astype(BF16)
           for j in range(len(units))]
    y = [_dot_nt(col('left', u)[c_sz:], s0_b[j]) + _dot(a_rb[j], u_b[j])
         + _dot(a_rk[j], col('v_b', u)) for j, u in enumerate(units)]
    u_t = [_dot_nt(eye_b, ub).astype(BF16) for ub in u_b]
    for j, u in enumerate(units):
        st_ref[u[0], u[1]] = (s0[j] * col('d_p', u) + _dot(u_t[j], col('bh', u))
                              + _dot(v_t[j], col('kh', u)))

    for bb in range(len(rows)):
        outs = []
        for h in range(nh):
            sl = slice(h * n, (h + 1) * n)
            yh = y[bb * nh + h]
            mean = jnp.mean(yh, axis=-1, keepdims=True)
            var = jnp.mean(jnp.square(yh - mean), axis=-1, keepdims=True)
            yn = (yh - mean) * lax.rsqrt(var + RWKV_GN_EPS)
            yn = yn * gg_ref[:, sl] + gb_ref[:, sl]
            bonus = jnp.sum(rows[bb]['rk'][:, sl], axis=-1, keepdims=True) * rows[bb]['v'][:, sl]
            outs.append((yn + bonus) * g_ref[bb, :, sl])
        o_ref[bb] = jnp.concatenate(outs, axis=1)


def _rwkv_scan(r, k, v, kk, b, lw, g, r_k, gn_gain, gn_bias, *, batch, seq, nb=4):
    t = r.shape[0]
    nc = seq // RWKV_CHUNK
    tok = pl.BlockSpec((nb, RWKV_CHUNK, RWKV_DIM), lambda bi, c: (bi, c, 0))
    par = pl.BlockSpec((1, RWKV_DIM), lambda bi, c: (0, 0))
    per_batch = lambda a: a.reshape(batch, seq, RWKV_DIM)
    out = pl.pallas_call(
        _rwkv_scan_body,
        out_shape=jax.ShapeDtypeStruct((batch, seq, RWKV_DIM), F32),
        grid=(batch // nb, nc),
        in_specs=[tok] * 7 + [par] * 3,
        out_specs=tok,
        scratch_shapes=[pltpu.VMEM((nb, RWKV_HEADS, RWKV_HEAD_DIM, RWKV_HEAD_DIM), F32)],
        compiler_params=_params(("parallel", "arbitrary")),
        name="rwkv_scan",
    )(*(per_batch(a) for a in (r, k, v, kk, b, lw, g)), r_k, gn_gain, gn_bias)
    return out.reshape(t, RWKV_DIM)


def _compress_body(x_ref, w1_ref, pe_ref, w2_ref, w2t_ref, o_ref, ot_ref):
    half = CMP_STRIDE * NSA_HEAD_DIM
    x = x_ref[0, 0].astype(BF16)
    first = _dot(x, w1_ref[0, 0:half, :])
    second = _dot(x, w1_ref[0, half:2 * half, :])
    n_rows = x.shape[0]
    second_next = pltpu.roll(second, n_rows - 1, 0)
    pe = jnp.broadcast_to(pe_ref[0], (8, 2 * half)).astype(BF16)
    pe_term = _dot(pe, w1_ref[0])[0:1, :]
    hid = first + second_next + pe_term
    act = jax.nn.gelu(hid).astype(BF16)
    o_ref[0, 0] = _dot(act, w2_ref[0])
    ot_ref[0, 0] = _dot_nt(w2t_ref[0], act)


def _compress(x, w1, pe, w2):
    _, bg, rows, width = x.shape
    dh = NSA_HEAD_DIM
    return pl.pallas_call(
        _compress_body,
        out_shape=[jax.ShapeDtypeStruct((2, bg, rows, dh), F32),
                   jax.ShapeDtypeStruct((2, bg, dh, rows), F32)],
        grid=(2, bg),
        in_specs=[
            pl.BlockSpec((1, 1, rows, width), lambda s, i: (s, i, 0, 0)),
            pl.BlockSpec((1, 2 * width, CMP_HIDDEN), lambda s, i: (s, 0, 0)),
            pl.BlockSpec((1, 1, 2 * width), lambda s, i: (s, 0, 0)),
            pl.BlockSpec((1, CMP_HIDDEN, dh), lambda s, i: (s, 0, 0)),
            pl.BlockSpec((1, dh, CMP_HIDDEN), lambda s, i: (s, 0, 0)),
        ],
        out_specs=[pl.BlockSpec((1, 1, rows, dh), lambda s, i: (s, i, 0, 0)),
                   pl.BlockSpec((1, 1, dh, rows), lambda s, i: (s, i, 0, 0))],
        compiler_params=_params(("parallel", "parallel")),
        name="nsa_compress",
    )(x, w1, pe, w2, w2.transpose(0, 2, 1))


def _t5_bucket(dist):
    n = jnp.maximum(dist, 0)
    exact = REL_BUCKETS // 2
    nf = jnp.maximum(n, 1).astype(F32)
    large = exact + (jnp.log(nf / exact) / math.log(REL_MAX_DIST / exact)
                     * (REL_BUCKETS - exact)).astype(jnp.int32)
    large = jnp.minimum(large, REL_BUCKETS - 1)
    return jnp.where(n < exact, n, large)


def _bias_body(tab_ref, bc_ref, bt_ref, *, seq, n_cmp_pad):
    h = pl.program_id(0)
    tq = NSA_TQ

    def lookup(dist):
        bucket = _t5_bucket(dist)
        out = jnp.zeros(dist.shape, F32)
        for bkt in range(REL_BUCKETS):
            out = jnp.where(bucket == bkt, tab_ref[bkt, h] * LOG2E, out)
        return out

    key = lax.broadcasted_iota(jnp.int32, (tq, tq), 0)
    qry = lax.broadcasted_iota(jnp.int32, (tq, tq), 1)
    bt_ref[0, BT_DIAG] = jnp.where(qry >= key, lookup(qry - key), MASKED)
    bt_ref[0, BT_PREV] = lookup(tq + qry - key)
    bt_ref[0, BT_FAR] = lookup(2 * tq + qry - key)
    bt_ref[0, BT_WIN_EDGE] = jnp.where(qry < key, lookup(WINDOW + qry - key), MASKED)
    bt_ref[0, BT_NONE] = jnp.full((tq, tq), MASKED, F32)

    cmp_end = lax.broadcasted_iota(jnp.int32, (n_cmp_pad, tq), 0) * CMP_STRIDE + CMP_LEN - 1
    qry_c = lax.broadcasted_iota(jnp.int32, (n_cmp_pad, tq), 1)

    def cmp_tile(i, carry):
        bc_ref[0, i] = lookup(i * tq + qry_c - cmp_end)
        return carry

    lax.fori_loop(0, seq // tq, cmp_tile, 0)


def _bias_tables(rel_bias, seq):
    g, hpg, tq = NSA_KV_GROUPS, NSA_HPG, NSA_TQ
    n_cmp_pad = seq // CMP_STRIDE
    nq = seq // tq
    return pl.pallas_call(
        functools.partial(_bias_body, seq=seq, n_cmp_pad=n_cmp_pad),
        out_shape=[jax.ShapeDtypeStruct((g, nq, n_cmp_pad, hpg * tq), F32),
                   jax.ShapeDtypeStruct((g, BT_COUNT, tq, hpg * tq), F32)],
        grid=(NSA_HEADS,),
        in_specs=[pl.BlockSpec(memory_space=pltpu.SMEM)],
        out_specs=[pl.BlockSpec((1, nq, n_cmp_pad, tq), lambda h: (h // hpg, 0, 0, h % hpg)),
                   pl.BlockSpec((1, BT_COUNT, tq, tq), lambda h: (h // hpg, 0, 0, h % hpg))],
        compiler_params=_params(("parallel",)),
        name="nsa_bias",
    )(rel_bias)


def _eye(n, dtype):
    return (lax.broadcasted_iota(jnp.int32, (n, n), 0)
            == lax.broadcasted_iota(jnp.int32, (n, n), 1)).astype(dtype)


def _nsa_body(q_ref, kc_ref, vct_ref, ks_ref, vst_ref, kw_ref, vwt_ref, bc_ref, bt_ref, gate_ref,
              o_ref):
    tq, hpg, dh = NSA_TQ, NSA_HPG, NSA_HEAD_DIM
    rws = hpg * tq
    n_blk_log2 = SEL_BLOCK.bit_length() - 1
    n_blk = ks_ref.shape[1] // SEL_BLOCK
    n_cmp_pad = kc_ref.shape[2]
    kw = ks_ref.shape[2]
    i = pl.program_id(2)

    def per_head(x):
        return [x[:, hh * tq:(hh + 1) * tq] for hh in range(hpg)]

    xq = (q_ref[0] * (dh ** -0.5 * LOG2E)).astype(BF16)
    eye_d = _eye(dh, BF16)
    q_t = jnp.concatenate([_dot_nt(eye_d, xq[:, hh * dh:(hh + 1) * dh]) for hh in range(hpg)],
                          axis=1).astype(BF16)

    cmp_id = lax.broadcasted_iota(jnp.int32, (n_cmp_pad, rws), 0)
    t_pos = i * tq + (lax.broadcasted_iota(jnp.int32, (n_cmp_pad, rws), 1) & (tq - 1))
    valid = (t_pos - (cmp_id * CMP_STRIDE + CMP_LEN - 1) >= 0) & (cmp_id < n_cmp_pad - 1)
    s = jnp.where(valid, _dot(kc_ref[0, 0].astype(BF16), q_t) + bc_ref[0, 0], MASKED)
    e = jnp.where(valid, jnp.exp2(s - jnp.max(s, axis=0, keepdims=True)), 0.0)
    den = jnp.sum(e, axis=0, keepdims=True)
    p_c = e / jnp.where(den > 0.0, den, 1.0)
    o_cmp = _dot(vct_ref[0, 0].astype(BF16), p_c.astype(BF16))

    p_heads = per_head(p_c)
    p_sum = p_heads[0]
    for ph in p_heads[1:]:
        p_sum = p_sum + ph
    blk_o = lax.broadcasted_iota(jnp.int32, (n_blk, n_cmp_pad), 0)
    cmp_o = lax.broadcasted_iota(jnp.int32, (n_blk, n_cmp_pad), 1)
    overlap_t = ((cmp_o * CMP_STRIDE <= blk_o * SEL_BLOCK + SEL_BLOCK - 1)
                 & (cmp_o * CMP_STRIDE + CMP_LEN - 1 >= blk_o * SEL_BLOCK)).astype(F32)
    imp = _dot(overlap_t, p_sum, HI)
    jj = lax.broadcasted_iota(jnp.int32, (n_blk, tq), 0)
    cur = (i * tq + lax.broadcasted_iota(jnp.int32, (n_blk, tq), 1)) >> n_blk_log2
    forced = (jj == 0) | (jj == cur) | (jj == cur - 1)
    imp = jnp.where(jj > cur, -1e6, jnp.where(forced, 1e6, imp))
    rank = jnp.zeros((n_blk, tq), jnp.int32)
    for a in range(n_blk):
        row = imp[a:a + 1, :]
        beats = (row > imp) | ((row == imp) & (a < jj))
        rank = rank + beats.astype(jnp.int32)
    not_sel = jnp.where(rank < SEL_TOP_N, 0.0, 1.0).astype(BF16)

    q_aug = jnp.concatenate([q_t, jnp.concatenate([not_sel] * hpg, axis=1),
                             jnp.zeros((kw - dh - n_blk, rws), BF16)], axis=0)
    q_pad = jnp.concatenate([q_t, jnp.zeros((kw - dh, rws), BF16)], axis=0)

    def attend(scores, vt_tile, carry):
        m_prev, acc = carry
        m_new = jnp.maximum(m_prev, jnp.max(scores, axis=0, keepdims=True))
        pr = jnp.exp2(scores - m_new)
        return m_new, jnp.exp2(m_prev - m_new) * acc + _dot(vt_tile, pr.astype(BF16))

    init = (jnp.full((1, rws), MASKED, F32), jnp.zeros((vst_ref.shape[2], rws), F32))

    n_win = WINDOW // tq
    tiles, scores = [], []
    for delta in range(n_win + 1):
        entry = {0: BT_DIAG, 1: BT_PREV, n_win: BT_WIN_EDGE}.get(delta, BT_FAR)
        if delta > 0:
            entry = jnp.where(i - delta >= 0, entry, BT_NONE)
        tiles.append(jnp.maximum(i - delta, 0))
        off = pl.multiple_of(tiles[-1] * tq, tq)
        scores.append(_dot(kw_ref[0, pl.ds(off, tq), :], q_pad) + bt_ref[0, entry])
    m_all = scores[0]
    for sc in scores[1:]:
        m_all = jnp.maximum(m_all, sc)
    m_w = jnp.max(m_all, axis=0, keepdims=True)
    acc_w = jnp.zeros((vwt_ref.shape[2], rws), F32)
    for kt, sc in zip(tiles, scores):
        acc_w = acc_w + _dot(vwt_ref[0, kt], jnp.exp2(sc - m_w).astype(BF16))
    o_win = acc_w[0:dh] / acc_w[dh:dh + 1]

    g_t = _dot_nt(_eye(GATE_PAD, F32), gate_ref[0], HI)
    sig = jax.nn.sigmoid(g_t[0:4 * hpg])
    gate = [jnp.concatenate([sig[br * hpg + hh:br * hpg + hh + 1] for hh in range(hpg)], axis=1)
            for br in range(3)]
    def sel_step(kt, carry):
        off = pl.multiple_of(kt * tq, tq)
        scores = _dot(ks_ref[0, pl.ds(off, tq), :], q_aug) + bt_ref[0, jnp.minimum(i - kt, BT_FAR)]
        return attend(scores, vst_ref[0, kt], carry)

    _, acc_s = lax.fori_loop(0, i + 1, sel_step, init)
    o_sel = acc_s[0:dh] / acc_s[dh:dh + 1]

    y_t = (gate[0] * o_cmp + gate[1] * o_sel + gate[2] * o_win).astype(BF16)
    eye_q = _eye(tq, BF16)
    o_ref[0] = jnp.concatenate([_dot_nt(eye_q, yh) for yh in per_head(y_t)],
                               axis=1).astype(BF16)


def _nsa_attention(proj, kv, kv_t, kvc, kvc_t, bias_c, bias_t, *, bsz, seq):
    g, tq, hpg, dh = NSA_KV_GROUPS, NSA_TQ, NSA_HPG, NSA_HEAD_DIM
    n_rows = kvc.shape[2]
    nq = seq // tq
    group_w = hpg * dh
    kv_spec = lambda pos: pl.BlockSpec((1, seq, LANES), lambda b, gi, i: (b, 0, pos * g + gi))
    kvt_spec = lambda pos: pl.BlockSpec((1, nq, LANES, tq), lambda b, gi, i: (b, 0, pos * g + gi, 0))
    kv3 = kv.reshape(bsz, seq, KV_COLS)
    proj3 = proj.reshape(bsz, seq, PROJ_COLS)
    out = pl.pallas_call(
        _nsa_body,
        out_shape=jax.ShapeDtypeStruct((bsz, seq, NSA_DIM), BF16),
        grid=(bsz, g, nq),
        in_specs=[
            pl.BlockSpec((1, tq, group_w), lambda b, gi, i: (b, i, COL_QNSA // group_w + gi)),
            pl.BlockSpec((1, 1, n_rows, dh), lambda b, gi, i: (0, b * g + gi, 0, 0)),
            pl.BlockSpec((1, 1, dh, n_rows), lambda b, gi, i: (1, b * g + gi, 0, 0)),
            kv_spec(ROW_K_SEL), kvt_spec(T_V_SEL), kv_spec(ROW_K_WIN), kvt_spec(T_V_WIN),
            pl.BlockSpec((1, 1, n_rows, hpg * tq), lambda b, gi, i: (gi, i, 0, 0)),
            pl.BlockSpec((1, BT_COUNT, tq, hpg * tq), lambda b, gi, i: (gi, 0, 0, 0)),
            pl.BlockSpec((1, tq, GATE_PAD), lambda b, gi, i: (b, i, COL_GNSA // GATE_PAD + gi)),
        ],
        out_specs=pl.BlockSpec((1, tq, group_w), lambda b, gi, i: (b, i, gi)),
        compiler_params=_params(("parallel", "parallel", "arbitrary")),
        name="nsa_attention",
    )(proj3, kvc, kvc_t, kv3, kv_t, kv3, kv_t, bias_c, bias_t, proj3)
    return out.reshape(bsz * seq, NSA_DIM)


def _mem_body(q_ref, kv_ref, o_ref):
    outs = []
    for h in range(MEM_HEADS):
        sl = slice(h * MEM_HEAD_DIM, (h + 1) * MEM_HEAD_DIM)
        qh = (q_ref[:, sl] * (MEM_HEAD_DIM ** -0.5)).astype(BF16)
        kh = kv_ref[0, :, sl].astype(BF16)
        vh = kv_ref[0, :, MEM_DIM + h * MEM_HEAD_DIM:MEM_DIM + (h + 1) * MEM_HEAD_DIM].astype(BF16)
        s = _dot_nt(qh, kh)
        e = jnp.exp(s - jnp.max(s, axis=-1, keepdims=True))
        p = e / jnp.sum(e, axis=-1, keepdims=True)
        outs.append(_dot(p.astype(BF16), vh))
    o_ref[...] = jnp.concatenate(outs, axis=1)


def _mem_attention(proj, kv, *, seq, tq=512):
    t = proj.shape[0]
    bsz, m, _ = kv.shape
    per_seq = seq // tq
    return pl.pallas_call(
        _mem_body,
        out_shape=jax.ShapeDtypeStruct((t, MEM_DIM), F32),
        grid=(t // tq,),
        in_specs=[
            pl.BlockSpec((tq, MEM_DIM), lambda i: (i, COL_QMEM // MEM_DIM)),
            pl.BlockSpec((1, m, 2 * MEM_DIM), lambda i: (i // per_seq, 0, 0)),
        ],
        out_specs=pl.BlockSpec((tq, MEM_DIM), lambda i: (i, 0)),
        compiler_params=_params(("parallel",)),
        name="mem_attention",
    )(proj, kv)


def _merge_body(x_ref, gain_ref, yr_ref, yn_ref, ym_ref, wg_ref, wr_ref, wn_ref, wm_ref, wo_ref,
                o_ref):
    d = x_ref.shape[1]
    x = x_ref[...]
    gates = jax.nn.sigmoid(_dot(_rms(x, gain_ref[...]).astype(BF16), wg_ref[...]))
    merged = (gates[:, 0:d] * _dot(yr_ref[...].astype(BF16), wr_ref[...])
              + gates[:, d:2 * d] * _dot(yn_ref[...], wn_ref[...])
              + gates[:, 2 * d:3 * d] * _dot(ym_ref[...].astype(BF16), wm_ref[...]))
    o_ref[...] = x + _dot(merged.astype(BF16), wo_ref[...])


def _merge(x, gain, y_rwkv, y_nsa, y_mem, w_g, w_r, w_n, w_m, w_o, *, tm=512):
    t, d = x.shape
    row = lambda i: (i, 0)
    const = lambda i: (0, 0)
    return pl.pallas_call(
        _merge_body,
        out_shape=jax.ShapeDtypeStruct((t, d), F32),
        grid=(t // tm,),
        in_specs=[
            pl.BlockSpec((tm, d), row),
            pl.BlockSpec((1, d), const),
            pl.BlockSpec((tm, RWKV_DIM), row),
            pl.BlockSpec((tm, NSA_DIM), row),
            pl.BlockSpec((tm, MEM_DIM), row),
            pl.BlockSpec((d, N_BRANCH * d), const),
            pl.BlockSpec((RWKV_DIM, d), const),
            pl.BlockSpec((NSA_DIM, d), const),
            pl.BlockSpec((MEM_DIM, d), const),
            pl.BlockSpec((d, d), const),
        ],
        out_specs=pl.BlockSpec((tm, d), row),
        compiler_params=_params(("parallel",)),
        name="merge",
    )(x, gain, y_rwkv, y_nsa, y_mem, w_g, w_r, w_n, w_m, w_o)


def _row(a):
    return a.reshape(1, -1)


def _in_proj(x, p, l, seq):
    d = x.shape[1]
    g, hpg, dh = NSA_KV_GROUPS, NSA_HPG, NSA_HEAD_DIM
    w_in = p['w_in'][l]
    o = 0
    parts = {}
    for name, size in (('rwkv', RWKV_PROJ), ('q', NSA_DIM), ('kv', KV_KINDS * NSA_KV_DIM),
                       ('g_nsa', 3 * NSA_HEADS), ('q_mem', MEM_DIM), ('g_br', N_BRANCH * d)):
        parts[name] = w_in[:, o:o + size]
        o += size
    gates = parts['g_nsa'].reshape(d, 3, g, hpg).transpose(0, 2, 1, 3).reshape(d, g, 3 * hpg)
    gates = jnp.pad(gates, ((0, 0), (0, 0), (0, GATE_PAD - 3 * hpg))).reshape(d, g * GATE_PAD)
    w_main = jnp.concatenate([parts['rwkv'], parts['q'], gates, parts['q_mem']], axis=1).astype(BF16)
    w_kv = jnp.pad(parts['kv'].reshape(d, KV_KINDS, g, dh), ((0, 0), (0, 0), (0, 0), (0, LANES - dh)))
    w_row = w_kv[:, KV_ROW_KINDS, :, :].reshape(d, KV_COLS).astype(BF16)
    w_t = w_kv[:, KV_T_KINDS, :, :].reshape(d, KV_T_ROWS).T.astype(BF16)
    gain = _row(p['mix_norm'][l])
    proj = _norm_matmul(x, gain, w_main, tm=1024, tn=PROJ_TN, name="in_proj")
    kv, kv_t = _kv_proj(x, gain, w_row, w_t, seq=seq)
    return proj, kv, kv_t


def _rwkv_branch(proj, p, l, bsz, seq):
    w_lora = jnp.zeros((LORA_DIM, 3 * RWKV_DIM), F32)
    w_lora = w_lora.at[0:DECAY_LORA, 0:RWKV_DIM].set(p['rwkv_w2'][l])
    w_lora = w_lora.at[DECAY_LORA:DECAY_LORA + AAA_LORA, RWKV_DIM:2 * RWKV_DIM].set(p['rwkv_a2'][l])
    w_lora = w_lora.at[DECAY_LORA + AAA_LORA:, 2 * RWKV_DIM:].set(p['rwkv_g2'][l])
    r, k, v, kk, b, lw, gate = _rwkv_prep(
        proj, seq, _row(p['rwkv_mu'][l]), w_lora.astype(BF16), _row(p['rwkv_w0'][l]),
        _row(p['rwkv_a0'][l]), _row(p['rwkv_k_k'][l]), _row(p['rwkv_k_a'][l]))
    return _rwkv_scan(r, k, v, kk, b, lw, gate, _row(p['rwkv_r_k'][l]),
                      _row(p['rwkv_gn_gain'][l]), _row(p['rwkv_gn_bias'][l]), batch=bsz, seq=seq)


def _nsa_branch(proj, kv, kv_t, bias_c, bias_t, p, l, bsz, seq):
    g, dh = NSA_KV_GROUPS, NSA_HEAD_DIM
    cmp_in = kv[:, 0:2 * g * LANES].reshape(bsz, seq, 2, g, LANES)[..., :dh]
    cmp_in = cmp_in.transpose(2, 0, 3, 1, 4).reshape(2, bsz * g, seq // CMP_STRIDE, CMP_STRIDE * dh)
    w1 = jnp.stack([p['cmp_k_w1'][l], p['cmp_v_w1'][l]]).astype(BF16)
    pe = jnp.stack([p['cmp_pe_k'][l].reshape(1, -1), p['cmp_pe_v'][l].reshape(1, -1)])
    w2 = jnp.stack([p['cmp_k_w2'][l], p['cmp_v_w2'][l]]).astype(BF16)
    kvc, kvc_t = _compress(cmp_in, w1, pe, w2)
    return _nsa_attention(proj, kv, kv_t, kvc, kvc_t, bias_c, bias_t, bsz=bsz, seq=seq)


def _mem_branch(proj, mem, p, l):
    bsz, m_tok, d = mem.shape
    w_kv = jnp.concatenate([p['mem_w_k'][l], p['mem_w_v'][l]], axis=1).astype(BF16)
    kv_mem = _norm_matmul(mem.reshape(bsz * m_tok, d), _row(p['mem_norm'][l]), w_kv,
                          tm=min(1024, bsz * m_tok), tn=512, name="mem_kv")
    return _mem_attention(proj, kv_mem.reshape(bsz, m_tok, 2 * MEM_DIM),
                          seq=proj.shape[0] // bsz)


def _layer(x, mem, l, bias_c, bias_t, p):
    bsz, seq, d = x.shape
    t = bsz * seq
    row = _row
    x = x.reshape(t, d)

    x = _ffn(x, row(p['ffn1_norm'][l]), p['ffn1_w_gate'][l].astype(BF16),
             p['ffn1_w_up'][l].astype(BF16), p['ffn1_w_down'][l].astype(BF16),
             row(p['final_norm']), final=False)

    proj, kv, kv_t = _in_proj(x, p, l, seq)
    y_rwkv = _rwkv_branch(proj, p, l, bsz, seq)
    y_nsa = _nsa_branch(proj, kv, kv_t, bias_c, bias_t, p, l, bsz, seq)
    y_mem = _mem_branch(proj, mem, p, l)

    w_gate = p['w_in'][l][:, -N_BRANCH * d:].astype(BF16)
    x = _merge(x, row(p['mix_norm'][l]), y_rwkv, y_nsa, y_mem, w_gate,
               p['w_br_rwkv'][l].astype(BF16), p['w_br_nsa'][l].astype(BF16),
               p['w_br_mem'][l].astype(BF16), p['w_out'][l].astype(BF16))

    last = l == p['ffn1_norm'].shape[0] - 1
    x = _ffn(x, row(p['ffn2_norm'][l]), p['ffn2_w_gate'][l].astype(BF16),
             p['ffn2_w_up'][l].astype(BF16), p['ffn2_w_down'][l].astype(BF16),
             row(p['final_norm']), final=last)
    return x.reshape(bsz, seq, d)


def kernel(x, mem, ffn1_norm, ffn1_w_gate, ffn1_w_up, ffn1_w_down, mix_norm, w_in, rwkv_mu, rwkv_w0, rwkv_w2, rwkv_a0, rwkv_a2, rwkv_g2, rwkv_k_k, rwkv_k_a, rwkv_r_k, rwkv_gn_gain, rwkv_gn_bias, cmp_pe_k, cmp_k_w1, cmp_k_w2, cmp_pe_v, cmp_v_w1, cmp_v_w2, rel_bias, mem_norm, mem_w_k, mem_w_v, w_br_rwkv, w_br_nsa, w_br_mem, w_out, ffn2_norm, ffn2_w_gate, ffn2_w_up, ffn2_w_down, final_norm):
    p = dict(ffn1_norm=ffn1_norm, ffn1_w_gate=ffn1_w_gate, ffn1_w_up=ffn1_w_up,
             ffn1_w_down=ffn1_w_down, mix_norm=mix_norm, w_in=w_in, rwkv_mu=rwkv_mu,
             rwkv_w0=rwkv_w0, rwkv_w2=rwkv_w2, rwkv_a0=rwkv_a0, rwkv_a2=rwkv_a2, rwkv_g2=rwkv_g2,
             rwkv_k_k=rwkv_k_k, rwkv_k_a=rwkv_k_a, rwkv_r_k=rwkv_r_k, rwkv_gn_gain=rwkv_gn_gain,
             rwkv_gn_bias=rwkv_gn_bias, cmp_pe_k=cmp_pe_k, cmp_k_w1=cmp_k_w1, cmp_k_w2=cmp_k_w2,
             cmp_pe_v=cmp_pe_v, cmp_v_w1=cmp_v_w1, cmp_v_w2=cmp_v_w2, mem_norm=mem_norm,
             mem_w_k=mem_w_k, mem_w_v=mem_w_v, w_br_rwkv=w_br_rwkv, w_br_nsa=w_br_nsa,
             w_br_mem=w_br_mem, w_out=w_out, ffn2_norm=ffn2_norm, ffn2_w_gate=ffn2_w_gate,
             ffn2_w_up=ffn2_w_up, ffn2_w_down=ffn2_w_down, final_norm=final_norm)
    bias_c, bias_t = _bias_tables(rel_bias, x.shape[1])
    for l in range(ffn1_norm.shape[0]):
        x = _layer(x, mem, l, bias_c, bias_t, p)
    return x
```

```python
import functools
import math

import jax
import jax.numpy as jnp
from jax import lax
from jax.experimental import pallas as pl
from jax.experimental.pallas import tpu as pltpu

F32 = jnp.float32
BF16 = jnp.bfloat16
HI = lax.Precision.HIGHEST

D_MODEL = 1024
NORM_EPS = 1e-6
D_FF = 2816
RWKV_HEADS = 8
RWKV_HEAD_DIM = 64
RWKV_DIM = RWKV_HEADS * RWKV_HEAD_DIM
DECAY_LORA = 64
AAA_LORA = 64
GATE_LORA = 128
LORA_DIM = DECAY_LORA + AAA_LORA + GATE_LORA
RWKV_GN_EPS = 64e-5
RWKV_PROJ = 3 * RWKV_DIM + LORA_DIM
NSA_HEADS = 8
NSA_KV_GROUPS = 2
NSA_HPG = NSA_HEADS // NSA_KV_GROUPS
NSA_HEAD_DIM = 64
NSA_DIM = NSA_HEADS * NSA_HEAD_DIM
NSA_KV_DIM = NSA_KV_GROUPS * NSA_HEAD_DIM
CMP_LEN = 32
CMP_STRIDE = 16
CMP_HIDDEN = 256
SEL_BLOCK = 64
SEL_TOP_N = 16
WINDOW = 512
REL_BUCKETS = 32
REL_MAX_DIST = 128
MEM_HEADS = 4
MEM_HEAD_DIM = 128
MEM_DIM = MEM_HEADS * MEM_HEAD_DIM
N_BRANCH = 3

LANES = 128
GATE_PAD = LANES
COL_RWKV = 0
COL_QNSA = COL_RWKV + RWKV_PROJ
COL_GNSA = COL_QNSA + NSA_DIM
COL_QMEM = COL_GNSA + NSA_KV_GROUPS * GATE_PAD
PROJ_COLS = COL_QMEM + MEM_DIM
PROJ_TN = 768
assert PROJ_COLS % PROJ_TN == 0 and COL_QMEM % MEM_DIM == 0
KV_KINDS = 6
KV_ROW_KINDS = (0, 1, 2, 4)
KV_T_KINDS = (3, 5)
KV_COLS = len(KV_ROW_KINDS) * NSA_KV_GROUPS * LANES
KV_T_ROWS = len(KV_T_KINDS) * NSA_KV_GROUPS * LANES
ROW_K_SEL, ROW_K_WIN = KV_ROW_KINDS.index(2), KV_ROW_KINDS.index(4)
T_V_SEL, T_V_WIN = KV_T_KINDS.index(3), KV_T_KINDS.index(5)
LOG2E = 1.4426950408889634

RWKV_CHUNK = 64
NSA_TQ = 256
MASKED = -1e30
BT_DIAG, BT_PREV, BT_FAR, BT_WIN_EDGE, BT_NONE, BT_COUNT = 0, 1, 2, 3, 4, 5
VMEM_LIMIT = 56 * 1024 * 1024


def _dot(a, b, precision=None):
    return jnp.dot(a, b, preferred_element_type=F32, precision=precision)


def _dot_nt(a, b, precision=None):
    return lax.dot_general(a, b, (((1,), (1,)), ((), ())), preferred_element_type=F32,
                           precision=precision)


def _params(semantics):
    return pltpu.CompilerParams(dimension_semantics=semantics, vmem_limit_bytes=VMEM_LIMIT)


def _rms(x, g):
    return x * lax.rsqrt(jnp.mean(x * x, axis=-1, keepdims=True) + NORM_EPS) * g


def _ffn_body(x_ref, g_ref, wg_ref, wu_ref, wd_ref, fg_ref, o_ref, h_ref, acc_ref, *, nf, final):
    j = pl.program_id(1)

    @pl.when(j == 0)
    def _():
        h_ref[...] = _rms(x_ref[...], g_ref[...]).astype(BF16)
        acc_ref[...] = jnp.zeros_like(acc_ref)

    h = h_ref[...]
    gate = _dot(h, wg_ref[...])
    up = _dot(h, wu_ref[...])
    act = (jax.nn.silu(gate) * up).astype(BF16)
    acc_ref[...] += _dot(act, wd_ref[...])

    @pl.when(j == nf - 1)
    def _():
        y = x_ref[...] + 0.5 * acc_ref[...]
        if final:
            y = _rms(y, fg_ref[...])
        o_ref[...] = y


def _ffn(x, gain, wg, wu, wd, final_gain, *, final, tm=1024, tf=256):
    t, d = x.shape
    f = wg.shape[1]
    nf = f // tf
    return pl.pallas_call(
        functools.partial(_ffn_body, nf=nf, final=final),
        out_shape=jax.ShapeDtypeStruct((t, d), F32),
        grid=(t // tm, nf),
        in_specs=[
            pl.BlockSpec((tm, d), lambda i, j: (i, 0)),
            pl.BlockSpec((1, d), lambda i, j: (0, 0)),
            pl.BlockSpec((d, tf), lambda i, j: (0, j)),
            pl.BlockSpec((d, tf), lambda i, j: (0, j)),
            pl.BlockSpec((tf, d), lambda i, j: (j, 0)),
            pl.BlockSpec((1, d), lambda i, j: (0, 0)),
        ],
        out_specs=pl.BlockSpec((tm, d), lambda i, j: (i, 0)),
        scratch_shapes=[pltpu.VMEM((tm, d), BF16), pltpu.VMEM((tm, d), F32)],
        compiler_params=_params(("parallel", "arbitrary")),
        name="ffn_final" if final else "ffn",
    )(x, gain, wg, wu, wd, final_gain)


def _norm_matmul_body(x_ref, g_ref, w_ref, o_ref, h_ref):
    @pl.when(pl.program_id(1) == 0)
    def _():
        h_ref[...] = _rms(x_ref[...], g_ref[...]).astype(BF16)

    o_ref[...] = _dot(h_ref[...], w_ref[...])


def _norm_matmul(x, gain, w, *, tm, tn, name):
    t, d = x.shape
    n = w.shape[1]
    return pl.pallas_call(
        _norm_matmul_body,
        out_shape=jax.ShapeDtypeStruct((t, n), F32),
        grid=(t // tm, n // tn),
        in_specs=[
            pl.BlockSpec((tm, d), lambda i, j: (i, 0)),
            pl.BlockSpec((1, d), lambda i, j: (0, 0)),
            pl.BlockSpec((d, tn), lambda i, j: (0, j)),
        ],
        out_specs=pl.BlockSpec((tm, tn), lambda i, j: (i, j)),
        scratch_shapes=[pltpu.VMEM((tm, d), BF16)],
        compiler_params=_params(("parallel", "arbitrary")),
        name=name,
    )(x, gain, w)


def _kv_proj_body(x_ref, g_ref, w_ref, wt_ref, o_ref, ot_ref, *, tm, seq):
    dh, tk = NSA_HEAD_DIM, NSA_TQ
    h = _rms(x_ref[...], g_ref[...]).astype(BF16)
    y = _dot(h, w_ref[...])
    row = lax.broadcasted_iota(jnp.int32, (tm, LANES), 0)
    lane = lax.broadcasted_iota(jnp.int32, (tm, LANES), 1)
    pos = (pl.program_id(0) * tm) % seq + row
    block_mark = jnp.where(lane - dh == (pos >> (SEL_BLOCK.bit_length() - 1)), MASKED, 0.0)
    for tile in range(KV_COLS // LANES):
        part = y[:, tile * LANES:(tile + 1) * LANES]
        if tile // NSA_KV_GROUPS == ROW_K_SEL:
            part = part + block_mark
        o_ref[:, tile * LANES:(tile + 1) * LANES] = part.astype(BF16)

    y_t = _dot_nt(wt_ref[...], h)
    row_t = lax.broadcasted_iota(jnp.int32, y_t.shape, 0)
    y_t = (y_t + jnp.where((row_t & (LANES - 1)) == dh, 1.0, 0.0)).astype(BF16)
    for c in range(tm // tk):
        ot_ref[0, c] = y_t[:, c * tk:(c + 1) * tk]


def _kv_proj(x, gain, w, w_t, *, seq, tm=512):
    t, d = x.shape
    tk = NSA_TQ
    per_seq = seq // tm
    return pl.pallas_call(
        functools.partial(_kv_proj_body, tm=tm, seq=seq),
        out_shape=[jax.ShapeDtypeStruct((t, KV_COLS), BF16),
                   jax.ShapeDtypeStruct((t // seq, seq // tk, KV_T_ROWS, tk), BF16)],
        grid=(t // tm,),
        in_specs=[
            pl.BlockSpec((tm, d), lambda i: (i, 0)),
            pl.BlockSpec((1, d), lambda i: (0, 0)),
            pl.BlockSpec((d, KV_COLS), lambda i: (0, 0)),
            pl.BlockSpec((KV_T_ROWS, d), lambda i: (0, 0)),
        ],
        out_specs=[pl.BlockSpec((tm, KV_COLS), lambda i: (i, 0)),
                   pl.BlockSpec((1, tm // tk, KV_T_ROWS, tk),
                                lambda i: (i // per_seq, i % per_seq, 0, 0))],
        compiler_params=_params(("parallel",)),
        name="nsa_kv_proj",
    )(x, gain, w, w_t)


def _rwkv_prep_body(p_ref, prev_ref, mu_ref, wl_ref, w0_ref, a0_ref, kk_ref, ka_ref,
                    r_o, k_o, v_o, kk_o, b_o, lw_o, g_o, *, tm, tiles_per_seq):
    i = pl.program_id(0)
    p = p_ref[...]
    keep = jnp.where(i % tiles_per_seq == 0, 0.0, 1.0)
    prev_last = prev_ref[7:8, :] * keep
    rows = lax.broadcasted_iota(jnp.int32, p.shape, 0)
    shifted = jnp.where(rows == 0, prev_last, pltpu.roll(p, 1, 0))
    x = p + (shifted - p) * mu_ref[...]

    r = x[:, 0:RWKV_DIM]
    k = x[:, RWKV_DIM:2 * RWKV_DIM]
    v = x[:, 2 * RWKV_DIM:3 * RWKV_DIM]
    s = x[:, 3 * RWKV_DIM:RWKV_PROJ]
    lane = lax.broadcasted_iota(jnp.int32, s.shape, 1)
    z = jnp.where(lane < DECAY_LORA, jnp.tanh(s),
                  jnp.where(lane < DECAY_LORA + AAA_LORA, s, jax.nn.sigmoid(s)))
    lo = _dot(z.astype(BF16), wl_ref[...])
    w = -jax.nn.softplus(-(w0_ref[...] + lo[:, 0:RWKV_DIM])) - 0.5
    a = jax.nn.sigmoid(a0_ref[...] + lo[:, RWKV_DIM:2 * RWKV_DIM])
    g = lo[:, 2 * RWKV_DIM:3 * RWKV_DIM]

    kkr = k * kk_ref[...]
    hi = lax.broadcasted_iota(jnp.int32, (RWKV_DIM, RWKV_DIM), 0) // RWKV_HEAD_DIM
    hj = lax.broadcasted_iota(jnp.int32, (RWKV_DIM, RWKV_DIM), 1) // RWKV_HEAD_DIM
    same_head = (hi == hj).astype(F32)
    ssq = _dot(kkr * kkr, same_head, HI)
    kk = kkr / jnp.maximum(jnp.sqrt(ssq), 1e-12)

    r_o[...] = r
    k_o[...] = k * (1.0 + (a - 1.0) * ka_ref[...])
    v_o[...] = v
    kk_o[...] = kk
    b_o[...] = kk * a
    lw_o[...] = -jnp.exp(w)
    g_o[...] = g


def _rwkv_prep(proj, seq, mu, w_lora, w0, a0, k_k, k_a, *, tm=512):
    t = proj.shape[0]
    tiles_per_seq = seq // tm
    row = lambda i: (i, 0)
    const = lambda i: (0, 0)
    out = jax.ShapeDtypeStruct((t, RWKV_DIM), F32)
    return pl.pallas_call(
        functools.partial(_rwkv_prep_body, tm=tm, tiles_per_seq=tiles_per_seq),
        out_shape=[out] * 7,
        grid=(t // tm,),
        in_specs=[
            pl.BlockSpec((tm, RWKV_PROJ), row),
            pl.BlockSpec((8, RWKV_PROJ), lambda i: (jnp.maximum(i * (tm // 8) - 1, 0), 0)),
            pl.BlockSpec((1, RWKV_PROJ), const),
            pl.BlockSpec((LORA_DIM, 3 * RWKV_DIM), const),
            pl.BlockSpec((1, RWKV_DIM), const),
            pl.BlockSpec((1, RWKV_DIM), const),
            pl.BlockSpec((1, RWKV_DIM), const),
            pl.BlockSpec((1, RWKV_DIM), const),
        ],
        out_specs=[pl.BlockSpec((tm, RWKV_DIM), row)] * 7,
        compiler_params=_params(("parallel",)),
        name="rwkv_prep",
    )(proj, proj, mu, w_lora, w0, a0, k_k, k_a)


def _rwkv_scan_body(r_ref, k_ref, v_ref, kk_ref, b_ref, lw_ref, g_ref, rk_ref, gg_ref, gb_ref,
                    o_ref, st_ref):
    c_sz, n, nh = RWKV_CHUNK, RWKV_HEAD_DIM, RWKV_HEADS

    @pl.when(pl.program_id(1) == 0)
    def _():
        st_ref[...] = jnp.zeros_like(st_ref)

    ri = lax.broadcasted_iota(jnp.int32, (c_sz, c_sz), 0)
    ci = lax.broadcasted_iota(jnp.int32, (c_sz, c_sz), 1)
    incl = ci <= ri
    eye_b = (ci == ri).astype(BF16)
    row2 = lax.broadcasted_iota(jnp.int32, (c_sz, 2 * c_sz), 0)
    lane2 = lax.broadcasted_iota(jnp.int32, (c_sz, 2 * c_sz), 1)
    right_half = lane2 >= c_sz
    zeros_b = jnp.zeros((c_sz, n), BF16)

    rows = []
    for bb in range(st_ref.shape[0]):
        lw = lw_ref[bb]
        cum = _dot(incl.astype(F32), lw, HI)
        cum_last = cum[c_sz - 1:c_sz, :]
        r, k, v, b = r_ref[bb], k_ref[bb], v_ref[bb], b_ref[bb]
        p_inv = jnp.exp(-cum)
        p_end = jnp.exp(cum_last - cum)
        rows.append(dict(
            left=jnp.concatenate([(-(kk_ref[bb] * jnp.exp(cum - lw))).astype(BF16),
                                  (r * jnp.exp(cum)).astype(BF16)], axis=0),
            bt=(b * p_inv).astype(BF16), kt=(k * p_inv).astype(BF16),
            bh=(b * p_end).astype(BF16), kh=(k * p_end).astype(BF16),
            v=v, v_b=v.astype(BF16), d_p=jnp.exp(cum_last), rk=r * k * rk_ref[...]))

    units = [(bb, h) for bb in range(len(rows)) for h in range(nh)]
    col = lambda name, u: rows[u[0]][name][:, u[1] * n:(u[1] + 1) * n]
    a_all = [_dot_nt(col('left', u), jnp.concatenate([col('bt', u), col('kt', u)], axis=0))
             for u in units]
    key2 = jnp.where(right_half, lane2 - c_sz, lane2)
    w_u = [jnp.where(right_half & (key2 < row2), a[:c_sz], 0.0).astype(BF16) for a in a_all]
    w_y = [jnp.where(key2 <= row2, a[c_sz:], 0.0).astype(BF16) for a in a_all]

    x = [jnp.where(lane2 < row2, a[:c_sz], jnp.where(lane2 == row2 + c_sz, 1.0, 0.0)) for a in a_all]
    for _ in range(6):
        hi = [xu.astype(BF16) for xu in x]
        lo = [(xu - h_.astype(F32)).astype(BF16) for xu, h_ in zip(x, hi)]
        x = [_dot(h_[:, :c_sz], h_) + _dot(h_[:, :c_sz], l_) + _dot(l_[:, :c_sz], h_)
             + jnp.where(right_half, xu, 0.0) for xu, h_, l_ in zip(x, hi, lo)]
    x_b = [xu.astype(BF16) for xu in x]

    s0 = [st_ref[bb, h] for bb, h in units]
    ls0 = [_dot_nt(col('left', u), s0[j].astype(BF16)) for j, u in enumerate(units)]
    rhs = [ls0[j][:c_sz] + _dot(w_u[j], jnp.concatenate([zeros_b, col('v_b', u)], axis=0))
           for j, u in enumerate(units)]
    u_b = [_dot(x_b[j], jnp.concatenate([zeros_b, rhs[j].astype(BF16)], axis=0)).astype(BF16)
           for j in range(len(units))]
    uv = [jnp.concatenate([u_b[j], col('v_b', u)], axis=0) for j, u in enumerate(units)]
    y = [ls0[j][c_sz:] + _dot(w_y[j], uv[j]) for j in range(len(units))]
    uv_t = [_dot_nt(eye_b, uv_j).astype(BF16) for uv_j in uv]
    for j, u in enumerate(units):
        st_ref[u[0], u[1]] = (s0[j] * col('d_p', u)
                              + _dot(uv_t[j], jnp.concatenate([col('bh', u), col('kh', u)], axis=0)))

    for bb in range(len(rows)):
        outs = []
        for h in range(nh):
            sl = slice(h * n, (h + 1) * n)
            yh = y[bb * nh + h]
            mean = jnp.mean(yh, axis=-1, keepdims=True)
            var = jnp.mean(jnp.square(yh - mean), axis=-1, keepdims=True)
            yn = (yh - mean) * lax.rsqrt(var + RWKV_GN_EPS)
            yn = yn * gg_ref[:, sl] + gb_ref[:, sl]
            bonus = jnp.sum(rows[bb]['rk'][:, sl], axis=-1, keepdims=True) * rows[bb]['v'][:, sl]
            outs.append((yn + bonus) * g_ref[bb, :, sl])
        o_ref[bb] = jnp.concatenate(outs, axis=1)


def _rwkv_scan(r, k, v, kk, b, lw, g, r_k, gn_gain, gn_bias, *, batch, seq, nb=4):
    t = r.shape[0]
    nc = seq // RWKV_CHUNK
    tok = pl.BlockSpec((nb, RWKV_CHUNK, RWKV_DIM), lambda bi, c: (bi, c, 0))
    par = pl.BlockSpec((1, RWKV_DIM), lambda bi, c: (0, 0))
    per_batch = lambda a: a.reshape(batch, seq, RWKV_DIM)
    out = pl.pallas_call(
        _rwkv_scan_body,
        out_shape=jax.ShapeDtypeStruct((batch, seq, RWKV_DIM), F32),
        grid=(batch // nb, nc),
        in_specs=[tok] * 7 + [par] * 3,
        out_specs=tok,
        scratch_shapes=[pltpu.VMEM((nb, RWKV_HEADS, RWKV_HEAD_DIM, RWKV_HEAD_DIM), F32)],
        compiler_params=_params(("parallel", "arbitrary")),
        name="rwkv_scan",
    )(*(per_batch(a) for a in (r, k, v, kk, b, lw, g)), r_k, gn_gain, gn_bias)
    return out.reshape(t, RWKV_DIM)


def _compress_body(x_ref, w1s_ref, w1_ref, pe_ref, w2_ref, w2t_ref, o_ref, ot_ref):
    n_rows = x_ref.shape[1]
    both = jnp.zeros((n_rows, 2 * CMP_HIDDEN), F32)
    for l in range(CMP_STRIDE):
        both = both + _dot(x_ref[0, :, l, :], w1s_ref[0, l])
    second_next = pltpu.roll(both[:, CMP_HIDDEN:], n_rows - 1, 0)
    pe = jnp.broadcast_to(pe_ref[0], (8, pe_ref.shape[2])).astype(BF16)
    pe_term = _dot(pe, w1_ref[0])[0:1, :]
    hid = both[:, :CMP_HIDDEN] + second_next + pe_term
    act = jax.nn.gelu(hid).astype(BF16)
    o_ref[0, 0] = _dot(act, w2_ref[0])
    ot_ref[0, 0] = _dot_nt(w2t_ref[0], act)


def _compress(kv, w1, pe, w2, *, bsz, seq):
    g, dh = NSA_KV_GROUPS, NSA_HEAD_DIM
    rows = seq // CMP_STRIDE
    w1r = jnp.pad(w1.reshape(2, CMP_LEN, dh, CMP_HIDDEN), ((0, 0), (0, 0), (0, LANES - dh), (0, 0)))
    w1s = jnp.concatenate([w1r[:, :CMP_STRIDE], w1r[:, CMP_STRIDE:]], axis=3)
    return pl.pallas_call(
        _compress_body,
        out_shape=[jax.ShapeDtypeStruct((2, bsz * g, rows, dh), F32),
                   jax.ShapeDtypeStruct((2, bsz * g, dh, rows), F32)],
        grid=(2, bsz * g),
        in_specs=[
            pl.BlockSpec((1, rows, CMP_STRIDE, LANES), lambda s, i: (i // g, 0, 0, s * g + i % g)),
            pl.BlockSpec((1, CMP_STRIDE, LANES, 2 * CMP_HIDDEN), lambda s, i: (s, 0, 0, 0)),
            pl.BlockSpec((1, CMP_LEN * dh, CMP_HIDDEN), lambda s, i: (s, 0, 0)),
            pl.BlockSpec((1, 1, CMP_LEN * dh), lambda s, i: (s, 0, 0)),
            pl.BlockSpec((1, CMP_HIDDEN, dh), lambda s, i: (s, 0, 0)),
            pl.BlockSpec((1, dh, CMP_HIDDEN), lambda s, i: (s, 0, 0)),
        ],
        out_specs=[pl.BlockSpec((1, 1, rows, dh), lambda s, i: (s, i, 0, 0)),
                   pl.BlockSpec((1, 1, dh, rows), lambda s, i: (s, i, 0, 0))],
        compiler_params=_params(("parallel", "parallel")),
        name="nsa_compress",
    )(kv.reshape(bsz, rows, CMP_STRIDE, KV_COLS), w1s, w1, pe, w2, w2.transpose(0, 2, 1))


def _t5_bucket(dist):
    n = jnp.maximum(dist, 0)
    exact = REL_BUCKETS // 2
    nf = jnp.maximum(n, 1).astype(F32)
    large = exact + (jnp.log(nf / exact) / math.log(REL_MAX_DIST / exact)
                     * (REL_BUCKETS - exact)).astype(jnp.int32)
    large = jnp.minimum(large, REL_BUCKETS - 1)
    return jnp.where(n < exact, n, large)


def _bias_body(tab_ref, bc_ref, bt_ref, *, seq, n_cmp_pad):
    h = pl.program_id(0)
    tq = NSA_TQ

    def lookup(dist):
        bucket = _t5_bucket(dist)
        out = jnp.zeros(dist.shape, F32)
        for bkt in range(REL_BUCKETS):
            out = jnp.where(bucket == bkt, tab_ref[bkt, h] * LOG2E, out)
        return out

    key = lax.broadcasted_iota(jnp.int32, (tq, tq), 0)
    qry = lax.broadcasted_iota(jnp.int32, (tq, tq), 1)
    bt_ref[0, BT_DIAG] = jnp.where(qry >= key, lookup(qry - key), MASKED)
    bt_ref[0, BT_PREV] = lookup(tq + qry - key)
    bt_ref[0, BT_FAR] = lookup(2 * tq + qry - key)
    bt_ref[0, BT_WIN_EDGE] = jnp.where(qry < key, lookup(WINDOW + qry - key), MASKED)
    bt_ref[0, BT_NONE] = jnp.full((tq, tq), MASKED, F32)

    cmp_end = lax.broadcasted_iota(jnp.int32, (n_cmp_pad, tq), 0) * CMP_STRIDE + CMP_LEN - 1
    qry_c = lax.broadcasted_iota(jnp.int32, (n_cmp_pad, tq), 1)

    def cmp_tile(i, carry):
        bc_ref[0, i] = lookup(i * tq + qry_c - cmp_end)
        return carry

    lax.fori_loop(0, seq // tq, cmp_tile, 0)


def _bias_tables(rel_bias, seq):
    g, hpg, tq = NSA_KV_GROUPS, NSA_HPG, NSA_TQ
    n_cmp_pad = seq // CMP_STRIDE
    nq = seq // tq
    return pl.pallas_call(
        functools.partial(_bias_body, seq=seq, n_cmp_pad=n_cmp_pad),
        out_shape=[jax.ShapeDtypeStruct((g, nq, n_cmp_pad, hpg * tq), F32),
                   jax.ShapeDtypeStruct((g, BT_COUNT, tq, hpg * tq), F32)],
        grid=(NSA_HEADS,),
        in_specs=[pl.BlockSpec(memory_space=pltpu.SMEM)],
        out_specs=[pl.BlockSpec((1, nq, n_cmp_pad, tq), lambda h: (h // hpg, 0, 0, h % hpg)),
                   pl.BlockSpec((1, BT_COUNT, tq, tq), lambda h: (h // hpg, 0, 0, h % hpg))],
        compiler_params=_params(("parallel",)),
        name="nsa_bias",
    )(rel_bias)


def _eye(n, dtype):
    return (lax.broadcasted_iota(jnp.int32, (n, n), 0)
            == lax.broadcasted_iota(jnp.int32, (n, n), 1)).astype(dtype)


def _nsa_body(q_ref, kc_ref, vct_ref, ks_ref, vst_ref, kw_ref, vwt_ref, bc_ref, bt_ref, gate_ref,
              o_ref):
    tq, hpg, dh = NSA_TQ, NSA_HPG, NSA_HEAD_DIM
    rws = hpg * tq
    n_blk_log2 = SEL_BLOCK.bit_length() - 1
    n_blk = ks_ref.shape[1] // SEL_BLOCK
    n_cmp_pad = kc_ref.shape[2]
    kw = ks_ref.shape[2]
    i = pl.program_id(2)

    def per_head(x):
        return [x[:, hh * tq:(hh + 1) * tq] for hh in range(hpg)]

    xq = (q_ref[0] * (dh ** -0.5 * LOG2E)).astype(BF16)
    eye_d = _eye(dh, BF16)
    q_t = jnp.concatenate([_dot_nt(eye_d, xq[:, hh * dh:(hh + 1) * dh]) for hh in range(hpg)],
                          axis=1).astype(BF16)

    cmp_id = lax.broadcasted_iota(jnp.int32, (n_cmp_pad, rws), 0)
    t_pos = i * tq + (lax.broadcasted_iota(jnp.int32, (n_cmp_pad, rws), 1) & (tq - 1))
    valid = (t_pos - (cmp_id * CMP_STRIDE + CMP_LEN - 1) >= 0) & (cmp_id < n_cmp_pad - 1)
    s = jnp.where(valid, _dot(kc_ref[0, 0].astype(BF16), q_t) + bc_ref[0, 0], MASKED)
    e = jnp.where(valid, jnp.exp2(s - jnp.max(s, axis=0, keepdims=True)), 0.0)
    den = jnp.sum(e, axis=0, keepdims=True)
    p_c = e / jnp.where(den > 0.0, den, 1.0)
    o_cmp = _dot(vct_ref[0, 0].astype(BF16), p_c.astype(BF16))

    p_heads = per_head(p_c)
    p_sum = p_heads[0]
    for ph in p_heads[1:]:
        p_sum = p_sum + ph
    blk_o = lax.broadcasted_iota(jnp.int32, (n_blk, n_cmp_pad), 0)
    cmp_o = lax.broadcasted_iota(jnp.int32, (n_blk, n_cmp_pad), 1)
    overlap_t = ((cmp_o * CMP_STRIDE <= blk_o * SEL_BLOCK + SEL_BLOCK - 1)
                 & (cmp_o * CMP_STRIDE + CMP_LEN - 1 >= blk_o * SEL_BLOCK)).astype(F32)
    imp = _dot(overlap_t, p_sum, HI)
    jj = lax.broadcasted_iota(jnp.int32, (n_blk, tq), 0)
    cur = (i * tq + lax.broadcasted_iota(jnp.int32, (n_blk, tq), 1)) >> n_blk_log2
    forced = (jj == 0) | (jj == cur) | (jj == cur - 1)
    imp = jnp.where(jj > cur, -1e6, jnp.where(forced, 1e6, imp))
    rank = jnp.zeros((n_blk, tq), jnp.int32)
    for a in range(n_blk):
        row = imp[a:a + 1, :]
        beats = (row > imp) | ((row == imp) & (a < jj))
        rank = rank + beats.astype(jnp.int32)
    not_sel = jnp.where(rank < SEL_TOP_N, 0.0, 1.0).astype(BF16)

    q_aug = jnp.concatenate([q_t, jnp.concatenate([not_sel] * hpg, axis=1),
                             jnp.zeros((kw - dh - n_blk, rws), BF16)], axis=0)
    q_pad = jnp.concatenate([q_t, jnp.zeros((kw - dh, rws), BF16)], axis=0)


    n_win = WINDOW // tq
    tiles, scores = [], []
    for delta in range(n_win + 1):
        entry = {0: BT_DIAG, 1: BT_PREV, n_win: BT_WIN_EDGE}.get(delta, BT_FAR)
        if delta > 0:
            entry = jnp.where(i - delta >= 0, entry, BT_NONE)
        tiles.append(jnp.maximum(i - delta, 0))
        off = pl.multiple_of(tiles[-1] * tq, tq)
        scores.append(_dot(kw_ref[0, pl.ds(off, tq), :], q_pad) + bt_ref[0, entry])
    m_all = scores[0]
    for sc in scores[1:]:
        m_all = jnp.maximum(m_all, sc)
    m_w = jnp.max(m_all, axis=0, keepdims=True)
    acc_w = jnp.zeros((vwt_ref.shape[2], rws), F32)
    for kt, sc in zip(tiles, scores):
        acc_w = acc_w + _dot(vwt_ref[0, kt], jnp.exp2(sc - m_w).astype(BF16))
    o_win = acc_w[0:dh] / acc_w[dh:dh + 1]

    g_t = _dot_nt(_eye(GATE_PAD, F32), gate_ref[0], HI)
    sig = jax.nn.sigmoid(g_t[0:4 * hpg])
    gate = [jnp.concatenate([sig[br * hpg + hh:br * hpg + hh + 1] for hh in range(hpg)], axis=1)
            for br in range(3)]
    def sel_step(kt, carry):
        m_prev, acc = carry
        off = pl.multiple_of(kt * tq, tq)
        scores = _dot(ks_ref[0, pl.ds(off, tq), :], q_aug) + bt_ref[0, jnp.minimum(i - kt, BT_FAR)]
        m_new = jnp.maximum(m_prev, jnp.max(scores, axis=0, keepdims=True))
        pr = jnp.exp2(scores - m_new).astype(BF16)
        return m_new, jnp.exp2(m_prev - m_new) * acc + _dot(vst_ref[0, kt], pr)

    init = (jnp.full((1, rws), MASKED, F32), jnp.zeros((vst_ref.shape[2], rws), F32))
    _, acc_s = lax.fori_loop(0, i + 1, sel_step, init)
    o_sel = acc_s[0:dh] / acc_s[dh:dh + 1]

    y_t = (gate[0] * o_cmp + gate[1] * o_sel + gate[2] * o_win).astype(BF16)
    eye_q = _eye(tq, BF16)
    o_ref[0] = jnp.concatenate([_dot_nt(eye_q, yh) for yh in per_head(y_t)],
                               axis=1).astype(BF16)


def _nsa_attention(proj, kv, kv_t, kvc, kvc_t, bias_c, bias_t, *, bsz, seq):
    g, tq, hpg, dh = NSA_KV_GROUPS, NSA_TQ, NSA_HPG, NSA_HEAD_DIM
    n_rows = kvc.shape[2]
    nq = seq // tq
    group_w = hpg * dh
    kv_spec = lambda pos: pl.BlockSpec((1, seq, LANES), lambda b, gi, i: (b, 0, pos * g + gi))
    kvt_spec = lambda pos: pl.BlockSpec((1, nq, LANES, tq), lambda b, gi, i: (b, 0, pos * g + gi, 0))
    kv3 = kv.reshape(bsz, seq, KV_COLS)
    proj3 = proj.reshape(bsz, seq, PROJ_COLS)
    out = pl.pallas_call(
        _nsa_body,
        out_shape=jax.ShapeDtypeStruct((bsz, seq, NSA_DIM), BF16),
        grid=(bsz, g, nq),
        in_specs=[
            pl.BlockSpec((1, tq, group_w), lambda b, gi, i: (b, i, COL_QNSA // group_w + gi)),
            pl.BlockSpec((1, 1, n_rows, dh), lambda b, gi, i: (0, b * g + gi, 0, 0)),
            pl.BlockSpec((1, 1, dh, n_rows), lambda b, gi, i: (1, b * g + gi, 0, 0)),
            kv_spec(ROW_K_SEL), kvt_spec(T_V_SEL), kv_spec(ROW_K_WIN), kvt_spec(T_V_WIN),
            pl.BlockSpec((1, 1, n_rows, hpg * tq), lambda b, gi, i: (gi, i, 0, 0)),
            pl.BlockSpec((1, BT_COUNT, tq, hpg * tq), lambda b, gi, i: (gi, 0, 0, 0)),
            pl.BlockSpec((1, tq, GATE_PAD), lambda b, gi, i: (b, i, COL_GNSA // GATE_PAD + gi)),
        ],
        out_specs=pl.BlockSpec((1, tq, group_w), lambda b, gi, i: (b, i, gi)),
        compiler_params=_params(("parallel", "parallel", "arbitrary")),
        name="nsa_attention",
    )(proj3, kvc, kvc_t, kv3, kv_t, kv3, kv_t, bias_c, bias_t, proj3)
    return out.reshape(bsz * seq, NSA_DIM)


def _mem_body(q_ref, kv_ref, o_ref):
    outs = []
    for h in range(MEM_HEADS):
        sl = slice(h * MEM_HEAD_DIM, (h + 1) * MEM_HEAD_DIM)
        qh = (q_ref[:, sl] * (MEM_HEAD_DIM ** -0.5)).astype(BF16)
        kh = kv_ref[0, :, sl].astype(BF16)
        vh = kv_ref[0, :, MEM_DIM + h * MEM_HEAD_DIM:MEM_DIM + (h + 1) * MEM_HEAD_DIM].astype(BF16)
        s = _dot_nt(qh, kh)
        e = jnp.exp(s - jnp.max(s, axis=-1, keepdims=True))
        p = e / jnp.sum(e, axis=-1, keepdims=True)
        outs.append(_dot(p.astype(BF16), vh))
    o_ref[...] = jnp.concatenate(outs, axis=1)


def _mem_attention(proj, kv, *, seq, tq=512):
    t = proj.shape[0]
    bsz, m, _ = kv.shape
    per_seq = seq // tq
    return pl.pallas_call(
        _mem_body,
        out_shape=jax.ShapeDtypeStruct((t, MEM_DIM), F32),
        grid=(t // tq,),
        in_specs=[
            pl.BlockSpec((tq, MEM_DIM), lambda i: (i, COL_QMEM // MEM_DIM)),
            pl.BlockSpec((1, m, 2 * MEM_DIM), lambda i: (i // per_seq, 0, 0)),
        ],
        out_specs=pl.BlockSpec((tq, MEM_DIM), lambda i: (i, 0)),
        compiler_params=_params(("parallel",)),
        name="mem_attention",
    )(proj, kv)


def _merge_body(x_ref, gain_ref, yr_ref, yn_ref, ym_ref, wg_ref, wr_ref, wn_ref, wm_ref, wo_ref,
                o_ref):
    d = x_ref.shape[1]
    x = x_ref[...]
    gates = jax.nn.sigmoid(_dot(_rms(x, gain_ref[...]).astype(BF16), wg_ref[...]))
    merged = (gates[:, 0:d] * _dot(yr_ref[...].astype(BF16), wr_ref[...])
              + gates[:, d:2 * d] * _dot(yn_ref[...], wn_ref[...])
              + gates[:, 2 * d:3 * d] * _dot(ym_ref[...].astype(BF16), wm_ref[...]))
    o_ref[...] = x + _dot(merged.astype(BF16), wo_ref[...])


def _merge(x, gain, y_rwkv, y_nsa, y_mem, w_g, w_r, w_n, w_m, w_o, *, tm=512):
    t, d = x.shape
    row = lambda i: (i, 0)
    const = lambda i: (0, 0)
    return pl.pallas_call(
        _merge_body,
        out_shape=jax.ShapeDtypeStruct((t, d), F32),
        grid=(t // tm,),
        in_specs=[
            pl.BlockSpec((tm, d), row),
            pl.BlockSpec((1, d), const),
            pl.BlockSpec((tm, RWKV_DIM), row),
            pl.BlockSpec((tm, NSA_DIM), row),
            pl.BlockSpec((tm, MEM_DIM), row),
            pl.BlockSpec((d, N_BRANCH * d), const),
            pl.BlockSpec((RWKV_DIM, d), const),
            pl.BlockSpec((NSA_DIM, d), const),
            pl.BlockSpec((MEM_DIM, d), const),
            pl.BlockSpec((d, d), const),
        ],
        out_specs=pl.BlockSpec((tm, d), row),
        compiler_params=_params(("parallel",)),
        name="merge",
    )(x, gain, y_rwkv, y_nsa, y_mem, w_g, w_r, w_n, w_m, w_o)


def _row(a):
    return a.reshape(1, -1)


def _in_proj(x, p, l, seq):
    d = x.shape[1]
    g, hpg, dh = NSA_KV_GROUPS, NSA_HPG, NSA_HEAD_DIM
    w_in = p['w_in'][l]
    o = 0
    parts = {}
    for name, size in (('rwkv', RWKV_PROJ), ('q', NSA_DIM), ('kv', KV_KINDS * NSA_KV_DIM),
                       ('g_nsa', 3 * NSA_HEADS), ('q_mem', MEM_DIM), ('g_br', N_BRANCH * d)):
        parts[name] = w_in[:, o:o + size]
        o += size
    gates = parts['g_nsa'].reshape(d, 3, g, hpg).transpose(0, 2, 1, 3).reshape(d, g, 3 * hpg)
    gates = jnp.pad(gates, ((0, 0), (0, 0), (0, GATE_PAD - 3 * hpg))).reshape(d, g * GATE_PAD)
    w_main = jnp.concatenate([parts['rwkv'], parts['q'], gates, parts['q_mem']], axis=1).astype(BF16)
    w_kv = jnp.pad(parts['kv'].reshape(d, KV_KINDS, g, dh), ((0, 0), (0, 0), (0, 0), (0, LANES - dh)))
    w_row = w_kv[:, KV_ROW_KINDS, :, :].reshape(d, KV_COLS).astype(BF16)
    w_t = w_kv[:, KV_T_KINDS, :, :].reshape(d, KV_T_ROWS).T.astype(BF16)
    gain = _row(p['mix_norm'][l])
    proj = _norm_matmul(x, gain, w_main, tm=1024, tn=PROJ_TN, name="in_proj")
    kv, kv_t = _kv_proj(x, gain, w_row, w_t, seq=seq)
    return proj, kv, kv_t


def _rwkv_branch(proj, p, l, bsz, seq):
    w_lora = jnp.zeros((LORA_DIM, 3 * RWKV_DIM), F32)
    w_lora = w_lora.at[0:DECAY_LORA, 0:RWKV_DIM].set(p['rwkv_w2'][l])
    w_lora = w_lora.at[DECAY_LORA:DECAY_LORA + AAA_LORA, RWKV_DIM:2 * RWKV_DIM].set(p['rwkv_a2'][l])
    w_lora = w_lora.at[DECAY_LORA + AAA_LORA:, 2 * RWKV_DIM:].set(p['rwkv_g2'][l])
    r, k, v, kk, b, lw, gate = _rwkv_prep(
        proj, seq, _row(p['rwkv_mu'][l]), w_lora.astype(BF16), _row(p['rwkv_w0'][l]),
        _row(p['rwkv_a0'][l]), _row(p['rwkv_k_k'][l]), _row(p['rwkv_k_a'][l]))
    return _rwkv_scan(r, k, v, kk, b, lw, gate, _row(p['rwkv_r_k'][l]),
                      _row(p['rwkv_gn_gain'][l]), _row(p['rwkv_gn_bias'][l]), batch=bsz, seq=seq)


def _nsa_branch(proj, kv, kv_t, bias_c, bias_t, p, l, bsz, seq):
    w1 = jnp.stack([p['cmp_k_w1'][l], p['cmp_v_w1'][l]]).astype(BF16)
    pe = jnp.stack([p['cmp_pe_k'][l].reshape(1, -1), p['cmp_pe_v'][l].reshape(1, -1)])
    w2 = jnp.stack([p['cmp_k_w2'][l], p['cmp_v_w2'][l]]).astype(BF16)
    kvc, kvc_t = _compress(kv, w1, pe, w2, bsz=bsz, seq=seq)
    return _nsa_attention(proj, kv, kv_t, kvc, kvc_t, bias_c, bias_t, bsz=bsz, seq=seq)


def _mem_branch(proj, mem, p, l):
    bsz, m_tok, d = mem.shape
    w_kv = jnp.concatenate([p['mem_w_k'][l], p['mem_w_v'][l]], axis=1).astype(BF16)
    kv_mem = _norm_matmul(mem.reshape(bsz * m_tok, d), _row(p['mem_norm'][l]), w_kv,
                          tm=min(1024, bsz * m_tok), tn=512, name="mem_kv")
    return _mem_attention(proj, kv_mem.reshape(bsz, m_tok, 2 * MEM_DIM),
                          seq=proj.shape[0] // bsz)


def _layer(x, mem, l, bias_c, bias_t, p):
    bsz, seq, d = x.shape
    t = bsz * seq
    row = _row
    x = x.reshape(t, d)

    x = _ffn(x, row(p['ffn1_norm'][l]), p['ffn1_w_gate'][l].astype(BF16),
             p['ffn1_w_up'][l].astype(BF16), p['ffn1_w_down'][l].astype(BF16),
             row(p['final_norm']), final=False)

    proj, kv, kv_t = _in_proj(x, p, l, seq)
    y_rwkv = _rwkv_branch(proj, p, l, bsz, seq)
    y_nsa = _nsa_branch(proj, kv, kv_t, bias_c, bias_t, p, l, bsz, seq)
    y_mem = _mem_branch(proj, mem, p, l)

    w_gate = p['w_in'][l][:, -N_BRANCH * d:].astype(BF16)
    x = _merge(x, row(p['mix_norm'][l]), y_rwkv, y_nsa, y_mem, w_gate,
               p['w_br_rwkv'][l].astype(BF16), p['w_br_nsa'][l].astype(BF16),
               p['w_br_mem'][l].astype(BF16), p['w_out'][l].astype(BF16))

    last = l == p['ffn1_norm'].shape[0] - 1
    x = _ffn(x, row(p['ffn2_norm'][l]), p['ffn2_w_gate'][l].astype(BF16),
             p['ffn2_w_up'][l].astype(BF16), p['ffn2_w_down'][l].astype(BF16),
             row(p['final_norm']), final=last)
    return x.reshape(bsz, seq, d)


def kernel(x, mem, ffn1_norm, ffn1_w_gate, ffn1_w_up, ffn1_w_down, mix_norm, w_in, rwkv_mu, rwkv_w0, rwkv_w2, rwkv_a0, rwkv_a2, rwkv_g2, rwkv_k_k, rwkv_k_a, rwkv_r_k, rwkv_gn_gain, rwkv_gn_bias, cmp_pe_k, cmp_k_w1, cmp_k_w2, cmp_pe_v, cmp_v_w1, cmp_v_w2, rel_bias, mem_norm, mem_w_k, mem_w_v, w_br_rwkv, w_br_nsa, w_br_mem, w_out, ffn2_norm, ffn2_w_gate, ffn2_w_up, ffn2_w_down, final_norm):
    p = dict(ffn1_norm=ffn1_norm, ffn1_w_gate=ffn1_w_gate, ffn1_w_up=ffn1_w_up,
             ffn1_w_down=ffn1_w_down, mix_norm=mix_norm, w_in=w_in, rwkv_mu=rwkv_mu,
             rwkv_w0=rwkv_w0, rwkv_w2=rwkv_w2, rwkv_a0=rwkv_a0, rwkv_a2=rwkv_a2, rwkv_g2=rwkv_g2,
             rwkv_k_k=rwkv_k_k, rwkv_k_a=rwkv_k_a, rwkv_r_k=rwkv_r_k, rwkv_gn_gain=rwkv_gn_gain,
             rwkv_gn_bias=rwkv_gn_bias, cmp_pe_k=cmp_pe_k, cmp_k_w1=cmp_k_w1, cmp_k_w2=cmp_k_w2,
             cmp_pe_v=cmp_pe_v, cmp_v_w1=cmp_v_w1, cmp_v_w2=cmp_v_w2, mem_norm=mem_norm,
             mem_w_k=mem_w_k, mem_w_v=mem_w_v, w_br_rwkv=w_br_rwkv, w_br_nsa=w_br_nsa,
             w_br_mem=w_br_mem, w_out=w_out, ffn2_norm=ffn2_norm, ffn2_w_gate=ffn2_w_gate,
             ffn2_w_up=ffn2_w_up, ffn2_w_down=ffn2_w_down, final_norm=final_norm)
    bias_c, bias_t = _bias_tables(rel_bias, x.shape[1])
    for l in range(ffn1_norm.shape[0]):
        x = _layer(x, mem, l, bias_c, bias_t, p)
    return x
```

```python
import functools
import math

import jax
import jax.numpy as jnp
from jax import lax
from jax.experimental import pallas as pl
from jax.experimental.pallas import tpu as pltpu

F32 = jnp.float32
BF16 = jnp.bfloat16
HI = lax.Precision.HIGHEST

D_MODEL = 1024
NORM_EPS = 1e-6
D_FF = 2816
RWKV_HEADS = 8
RWKV_HEAD_DIM = 64
RWKV_DIM = RWKV_HEADS * RWKV_HEAD_DIM
DECAY_LORA = 64
AAA_LORA = 64
GATE_LORA = 128
LORA_DIM = DECAY_LORA + AAA_LORA + GATE_LORA
RWKV_GN_EPS = 64e-5
RWKV_PROJ = 3 * RWKV_DIM + LORA_DIM
NSA_HEADS = 8
NSA_KV_GROUPS = 2
NSA_HPG = NSA_HEADS // NSA_KV_GROUPS
NSA_HEAD_DIM = 64
NSA_DIM = NSA_HEADS * NSA_HEAD_DIM
NSA_KV_DIM = NSA_KV_GROUPS * NSA_HEAD_DIM
CMP_LEN = 32
CMP_STRIDE = 16
CMP_HIDDEN = 256
SEL_BLOCK = 64
SEL_TOP_N = 16
WINDOW = 512
REL_BUCKETS = 32
REL_MAX_DIST = 128
MEM_HEADS = 4
MEM_HEAD_DIM = 128
MEM_DIM = MEM_HEADS * MEM_HEAD_DIM
N_BRANCH = 3

LANES = 128
GATE_PAD = LANES
COL_RWKV = 0
COL_QNSA = COL_RWKV + RWKV_PROJ
COL_GNSA = COL_QNSA + NSA_DIM
COL_QMEM = COL_GNSA + NSA_KV_GROUPS * GATE_PAD
PROJ_COLS = COL_QMEM + MEM_DIM
PROJ_TN = 768
assert PROJ_COLS % PROJ_TN == 0 and COL_QMEM % MEM_DIM == 0
KV_KINDS = 6
KV_ROW_KINDS = (0, 1, 2, 4)
KV_T_KINDS = (3, 5)
KV_COLS = len(KV_ROW_KINDS) * NSA_KV_GROUPS * LANES
KV_T_ROWS = len(KV_T_KINDS) * NSA_KV_GROUPS * LANES
ROW_K_SEL, ROW_K_WIN = KV_ROW_KINDS.index(2), KV_ROW_KINDS.index(4)
T_V_SEL, T_V_WIN = KV_T_KINDS.index(3), KV_T_KINDS.index(5)
LOG2E = 1.4426950408889634

RWKV_CHUNK = 64
NSA_TQ = 256
MASKED = -1e30
BT_DIAG, BT_PREV, BT_FAR, BT_WIN_EDGE, BT_NONE, BT_COUNT = 0, 1, 2, 3, 4, 5
VMEM_LIMIT = 56 * 1024 * 1024


def _dot(a, b, precision=None):
    return jnp.dot(a, b, preferred_element_type=F32, precision=precision)


def _dot_nt(a, b, precision=None):
    return lax.dot_general(a, b, (((1,), (1,)), ((), ())), preferred_element_type=F32,
                           precision=precision)


def _params(semantics):
    return pltpu.CompilerParams(dimension_semantics=semantics, vmem_limit_bytes=VMEM_LIMIT)


def _rms(x, g):
    return x * lax.rsqrt(jnp.mean(x * x, axis=-1, keepdims=True) + NORM_EPS) * g


def _ffn_body(x_ref, g_ref, wg_ref, wu_ref, wd_ref, fg_ref, o_ref, *, final):
    x = x_ref[...]
    h = _rms(x, g_ref[...]).astype(BF16)
    acc = jnp.zeros(x.shape, F32)
    for j in range(wg_ref.shape[0]):
        act = (jax.nn.silu(_dot(h, wg_ref[j])) * _dot(h, wu_ref[j])).astype(BF16)
        acc = acc + _dot(act, wd_ref[j])
    y = x + 0.5 * acc
    if final:
        y = _rms(y, fg_ref[...])
    o_ref[...] = y


def _ffn(x, gain, wg, wu, wd, final_gain, *, final, tm=1024, tf=256):
    t, d = x.shape
    f = wg.shape[1]
    nf = f // tf
    chunked = lambda w: w.reshape(d, nf, tf).transpose(1, 0, 2)
    resident = lambda shape: pl.BlockSpec(shape, lambda i: (0,) * len(shape),
                                          pipeline_mode=pl.Buffered(1))
    return pl.pallas_call(
        functools.partial(_ffn_body, final=final),
        out_shape=jax.ShapeDtypeStruct((t, d), F32),
        grid=(t // tm,),
        in_specs=[
            pl.BlockSpec((tm, d), lambda i: (i, 0)),
            pl.BlockSpec((1, d), lambda i: (0, 0)),
            resident((nf, d, tf)),
            resident((nf, d, tf)),
            resident((nf, tf, d)),
            pl.BlockSpec((1, d), lambda i: (0, 0)),
        ],
        out_specs=pl.BlockSpec((tm, d), lambda i: (i, 0)),
        compiler_params=_params(("parallel",)),
        name="ffn_final" if final else "ffn",
    )(x, gain, chunked(wg), chunked(wu), wd.reshape(nf, tf, d), final_gain)


def _norm_matmul_body(x_ref, g_ref, w_ref, o_ref, h_ref):
    @pl.when(pl.program_id(1) == 0)
    def _():
        h_ref[...] = _rms(x_ref[...], g_ref[...]).astype(BF16)

    o_ref[...] = _dot(h_ref[...], w_ref[...])


def _norm_matmul(x, gain, w, *, tm, tn, name):
    t, d = x.shape
    n = w.shape[1]
    return pl.pallas_call(
        _norm_matmul_body,
        out_shape=jax.ShapeDtypeStruct((t, n), F32),
        grid=(t // tm, n // tn),
        in_specs=[
            pl.BlockSpec((tm, d), lambda i, j: (i, 0)),
            pl.BlockSpec((1, d), lambda i, j: (0, 0)),
            pl.BlockSpec((d, tn), lambda i, j: (0, j)),
        ],
        out_specs=pl.BlockSpec((tm, tn), lambda i, j: (i, j)),
        scratch_shapes=[pltpu.VMEM((tm, d), BF16)],
        compiler_params=_params(("parallel", "arbitrary")),
        name=name,
    )(x, gain, w)


def _kv_proj_body(x_ref, g_ref, w_ref, wt_ref, o_ref, ot_ref, *, tm, seq):
    dh, tk = NSA_HEAD_DIM, NSA_TQ
    h = _rms(x_ref[...], g_ref[...]).astype(BF16)
    y = _dot(h, w_ref[...])
    row = lax.broadcasted_iota(jnp.int32, (tm, LANES), 0)
    lane = lax.broadcasted_iota(jnp.int32, (tm, LANES), 1)
    pos = (pl.program_id(0) * tm) % seq + row
    block_mark = jnp.where(lane - dh == (pos >> (SEL_BLOCK.bit_length() - 1)), MASKED, 0.0)
    for tile in range(KV_COLS // LANES):
        part = y[:, tile * LANES:(tile + 1) * LANES]
        if tile // NSA_KV_GROUPS == ROW_K_SEL:
            part = part + block_mark
        o_ref[:, tile * LANES:(tile + 1) * LANES] = part.astype(BF16)

    y_t = _dot_nt(wt_ref[...], h)
    row_t = lax.broadcasted_iota(jnp.int32, y_t.shape, 0)
    y_t = (y_t + jnp.where((row_t & (LANES - 1)) == dh, 1.0, 0.0)).astype(BF16)
    for c in range(tm // tk):
        ot_ref[0, c] = y_t[:, c * tk:(c + 1) * tk]


def _kv_proj(x, gain, w, w_t, *, seq, tm=512):
    t, d = x.shape
    tk = NSA_TQ
    per_seq = seq // tm
    return pl.pallas_call(
        functools.partial(_kv_proj_body, tm=tm, seq=seq),
        out_shape=[jax.ShapeDtypeStruct((t, KV_COLS), BF16),
                   jax.ShapeDtypeStruct((t // seq, seq // tk, KV_T_ROWS, tk), BF16)],
        grid=(t // tm,),
        in_specs=[
            pl.BlockSpec((tm, d), lambda i: (i, 0)),
            pl.BlockSpec((1, d), lambda i: (0, 0)),
            pl.BlockSpec((d, KV_COLS), lambda i: (0, 0)),
            pl.BlockSpec((KV_T_ROWS, d), lambda i: (0, 0)),
        ],
        out_specs=[pl.BlockSpec((tm, KV_COLS), lambda i: (i, 0)),
                   pl.BlockSpec((1, tm // tk, KV_T_ROWS, tk),
                                lambda i: (i // per_seq, i % per_seq, 0, 0))],
        compiler_params=_params(("parallel",)),
        name="nsa_kv_proj",
    )(x, gain, w, w_t)


def _rwkv_prep_body(p_ref, prev_ref, mu_ref, wl_ref, w0_ref, a0_ref, kk_ref, ka_ref,
                    r_o, k_o, v_o, kk_o, b_o, lw_o, g_o, *, tm, tiles_per_seq):
    i = pl.program_id(0)
    p = p_ref[...]
    keep = jnp.where(i % tiles_per_seq == 0, 0.0, 1.0)
    prev_last = prev_ref[7:8, :] * keep
    rows = lax.broadcasted_iota(jnp.int32, p.shape, 0)
    shifted = jnp.where(rows == 0, prev_last, pltpu.roll(p, 1, 0))
    x = p + (shifted - p) * mu_ref[...]

    r = x[:, 0:RWKV_DIM]
    k = x[:, RWKV_DIM:2 * RWKV_DIM]
    v = x[:, 2 * RWKV_DIM:3 * RWKV_DIM]
    s = x[:, 3 * RWKV_DIM:RWKV_PROJ]
    lane = lax.broadcasted_iota(jnp.int32, s.shape, 1)
    z = jnp.where(lane < DECAY_LORA, jnp.tanh(s),
                  jnp.where(lane < DECAY_LORA + AAA_LORA, s, jax.nn.sigmoid(s)))
    lo = _dot(z.astype(BF16), wl_ref[...])
    w = -jax.nn.softplus(-(w0_ref[...] + lo[:, 0:RWKV_DIM])) - 0.5
    a = jax.nn.sigmoid(a0_ref[...] + lo[:, RWKV_DIM:2 * RWKV_DIM])
    g = lo[:, 2 * RWKV_DIM:3 * RWKV_DIM]

    kkr = k * kk_ref[...]
    hi = lax.broadcasted_iota(jnp.int32, (RWKV_DIM, RWKV_DIM), 0) // RWKV_HEAD_DIM
    hj = lax.broadcasted_iota(jnp.int32, (RWKV_DIM, RWKV_DIM), 1) // RWKV_HEAD_DIM
    same_head = (hi == hj).astype(F32)
    ssq = _dot(kkr * kkr, same_head, HI)
    kk = kkr / jnp.maximum(jnp.sqrt(ssq), 1e-12)

    r_o[...] = r
    k_o[...] = k * (1.0 + (a - 1.0) * ka_ref[...])
    v_o[...] = v
    kk_o[...] = kk
    b_o[...] = kk * a
    lw_o[...] = -jnp.exp(w)
    g_o[...] = g


def _rwkv_prep(proj, seq, mu, w_lora, w0, a0, k_k, k_a, *, tm=512):
    t = proj.shape[0]
    tiles_per_seq = seq // tm
    row = lambda i: (i, 0)
    const = lambda i: (0, 0)
    out = jax.ShapeDtypeStruct((t, RWKV_DIM), F32)
    return pl.pallas_call(
        functools.partial(_rwkv_prep_body, tm=tm, tiles_per_seq=tiles_per_seq),
        out_shape=[out] * 7,
        grid=(t // tm,),
        in_specs=[
            pl.BlockSpec((tm, RWKV_PROJ), row),
            pl.BlockSpec((8, RWKV_PROJ), lambda i: (jnp.maximum(i * (tm // 8) - 1, 0), 0)),
            pl.BlockSpec((1, RWKV_PROJ), const),
            pl.BlockSpec((LORA_DIM, 3 * RWKV_DIM), const),
            pl.BlockSpec((1, RWKV_DIM), const),
            pl.BlockSpec((1, RWKV_DIM), const),
            pl.BlockSpec((1, RWKV_DIM), const),
            pl.BlockSpec((1, RWKV_DIM), const),
        ],
        out_specs=[pl.BlockSpec((tm, RWKV_DIM), row)] * 7,
        compiler_params=_params(("parallel",)),
        name="rwkv_prep",
    )(proj, proj, mu, w_lora, w0, a0, k_k, k_a)


def _rwkv_scan_body(r_ref, k_ref, v_ref, kk_ref, b_ref, lw_ref, g_ref, rk_ref, gg_ref, gb_ref,
                    o_ref, st_ref):
    c_sz, n, nh = RWKV_CHUNK, RWKV_HEAD_DIM, RWKV_HEADS

    @pl.when(pl.program_id(1) == 0)
    def _():
        st_ref[...] = jnp.zeros_like(st_ref)

    ri = lax.broadcasted_iota(jnp.int32, (c_sz, c_sz), 0)
    ci = lax.broadcasted_iota(jnp.int32, (c_sz, c_sz), 1)
    incl = ci <= ri
    eye_b = (ci == ri).astype(BF16)
    row2 = lax.broadcasted_iota(jnp.int32, (c_sz, 2 * c_sz), 0)
    lane2 = lax.broadcasted_iota(jnp.int32, (c_sz, 2 * c_sz), 1)
    right_half = lane2 >= c_sz
    zeros_b = jnp.zeros((c_sz, n), BF16)

    rows = []
    for bb in range(st_ref.shape[0]):
        lw = lw_ref[bb]
        cum = _dot(incl.astype(F32), lw, HI)
        cum_last = cum[c_sz - 1:c_sz, :]
        r, k, v, b = r_ref[bb], k_ref[bb], v_ref[bb], b_ref[bb]
        p_inv = jnp.exp(-cum)
        p_end = jnp.exp(cum_last - cum)
        rows.append(dict(
            left=jnp.concatenate([(-(kk_ref[bb] * jnp.exp(cum - lw))).astype(BF16),
                                  (r * jnp.exp(cum)).astype(BF16)], axis=0),
            bt=(b * p_inv).astype(BF16), kt=(k * p_inv).astype(BF16),
            bh=(b * p_end).astype(BF16), kh=(k * p_end).astype(BF16),
            v=v, v_b=v.astype(BF16), d_p=jnp.exp(cum_last), rk=r * k * rk_ref[...]))

    units = [(bb, h) for bb in range(len(rows)) for h in range(nh)]
    col = lambda name, u: rows[u[0]][name][:, u[1] * n:(u[1] + 1) * n]
    a_all = [_dot_nt(col('left', u), jnp.concatenate([col('bt', u), col('kt', u)], axis=0))
             for u in units]
    key2 = jnp.where(right_half, lane2 - c_sz, lane2)
    w_u = [jnp.where(right_half & (key2 < row2), a[:c_sz], 0.0).astype(BF16) for a in a_all]
    w_y = [jnp.where(key2 <= row2, a[c_sz:], 0.0).astype(BF16) for a in a_all]

    x = [jnp.where(lane2 < row2, a[:c_sz], jnp.where(lane2 == row2 + c_sz, 1.0, 0.0)) for a in a_all]
    for _ in range(6):
        hi = [xu.astype(BF16) for xu in x]
        lo = [(xu - h_.astype(F32)).astype(BF16) for xu, h_ in zip(x, hi)]
        x = [_dot(h_[:, :c_sz], h_) + _dot(h_[:, :c_sz], l_) + _dot(l_[:, :c_sz], h_)
             + jnp.where(right_half, xu, 0.0) for xu, h_, l_ in zip(x, hi, lo)]
    x_b = [xu.astype(BF16) for xu in x]

    s0 = [st_ref[bb, h] for bb, h in units]
    ls0 = [_dot_nt(col('left', u), s0[j].astype(BF16)) for j, u in enumerate(units)]
    rhs = [ls0[j][:c_sz] + _dot(w_u[j], jnp.concatenate([zeros_b, col('v_b', u)], axis=0))
           for j, u in enumerate(units)]
    u_b = [_dot(x_b[j], jnp.concatenate([zeros_b, rhs[j].astype(BF16)], axis=0)).astype(BF16)
           for j in range(len(units))]
    uv = [jnp.concatenate([u_b[j], col('v_b', u)], axis=0) for j, u in enumerate(units)]
    y = [ls0[j][c_sz:] + _dot(w_y[j], uv[j]) for j in range(len(units))]
    uv_t = [_dot_nt(eye_b, uv_j).astype(BF16) for uv_j in uv]
    for j, u in enumerate(units):
        st_ref[u[0], u[1]] = (s0[j] * col('d_p', u)
                              + _dot(uv_t[j], jnp.concatenate([col('bh', u), col('kh', u)], axis=0)))

    for bb in range(len(rows)):
        outs = []
        for h in range(nh):
            sl = slice(h * n, (h + 1) * n)
            yh = y[bb * nh + h]
            mean = jnp.mean(yh, axis=-1, keepdims=True)
            var = jnp.mean(jnp.square(yh - mean), axis=-1, keepdims=True)
            yn = (yh - mean) * lax.rsqrt(var + RWKV_GN_EPS)
            yn = yn * gg_ref[:, sl] + gb_ref[:, sl]
            bonus = jnp.sum(rows[bb]['rk'][:, sl], axis=-1, keepdims=True) * rows[bb]['v'][:, sl]
            outs.append((yn + bonus) * g_ref[bb, :, sl])
        o_ref[bb] = jnp.concatenate(outs, axis=1)


def _rwkv_scan(r, k, v, kk, b, lw, g, r_k, gn_gain, gn_bias, *, batch, seq, nb=4):
    t = r.shape[0]
    nc = seq // RWKV_CHUNK
    tok = pl.BlockSpec((nb, RWKV_CHUNK, RWKV_DIM), lambda bi, c: (bi, c, 0))
    par = pl.BlockSpec((1, RWKV_DIM), lambda bi, c: (0, 0))
    per_batch = lambda a: a.reshape(batch, seq, RWKV_DIM)
    out = pl.pallas_call(
        _rwkv_scan_body,
        out_shape=jax.ShapeDtypeStruct((batch, seq, RWKV_DIM), F32),
        grid=(batch // nb, nc),
        in_specs=[tok] * 7 + [par] * 3,
        out_specs=tok,
        scratch_shapes=[pltpu.VMEM((nb, RWKV_HEADS, RWKV_HEAD_DIM, RWKV_HEAD_DIM), F32)],
        compiler_params=_params(("parallel", "arbitrary")),
        name="rwkv_scan",
    )(*(per_batch(a) for a in (r, k, v, kk, b, lw, g)), r_k, gn_gain, gn_bias)
    return out.reshape(t, RWKV_DIM)


def _compress_body(x_ref, w1s_ref, w1_ref, pe_ref, w2_ref, w2t_ref, o_ref, ot_ref):
    n_rows = x_ref.shape[1]
    both = jnp.zeros((n_rows, 2 * CMP_HIDDEN), F32)
    for l in range(CMP_STRIDE):
        both = both + _dot(x_ref[0, :, l, :], w1s_ref[0, l])
    second_next = pltpu.roll(both[:, CMP_HIDDEN:], n_rows - 1, 0)
    pe = jnp.broadcast_to(pe_ref[0], (8, pe_ref.shape[2])).astype(BF16)
    pe_term = _dot(pe, w1_ref[0])[0:1, :]
    hid = both[:, :CMP_HIDDEN] + second_next + pe_term
    act = jax.nn.gelu(hid).astype(BF16)
    o_ref[0, 0] = _dot(act, w2_ref[0])
    ot_ref[0, 0] = _dot_nt(w2t_ref[0], act)


def _compress(kv, w1, pe, w2, *, bsz, seq):
    g, dh = NSA_KV_GROUPS, NSA_HEAD_DIM
    rows = seq // CMP_STRIDE
    w1r = jnp.pad(w1.reshape(2, CMP_LEN, dh, CMP_HIDDEN), ((0, 0), (0, 0), (0, LANES - dh), (0, 0)))
    w1s = jnp.concatenate([w1r[:, :CMP_STRIDE], w1r[:, CMP_STRIDE:]], axis=3)
    return pl.pallas_call(
        _compress_body,
        out_shape=[jax.ShapeDtypeStruct((2, bsz * g, rows, dh), F32),
                   jax.ShapeDtypeStruct((2, bsz * g, dh, rows), F32)],
        grid=(2, bsz * g),
        in_specs=[
            pl.BlockSpec((1, rows, CMP_STRIDE, LANES), lambda s, i: (i // g, 0, 0, s * g + i % g)),
            pl.BlockSpec((1, CMP_STRIDE, LANES, 2 * CMP_HIDDEN), lambda s, i: (s, 0, 0, 0)),
            pl.BlockSpec((1, CMP_LEN * dh, CMP_HIDDEN), lambda s, i: (s, 0, 0)),
            pl.BlockSpec((1, 1, CMP_LEN * dh), lambda s, i: (s, 0, 0)),
            pl.BlockSpec((1, CMP_HIDDEN, dh), lambda s, i: (s, 0, 0)),
            pl.BlockSpec((1, dh, CMP_HIDDEN), lambda s, i: (s, 0, 0)),
        ],
        out_specs=[pl.BlockSpec((1, 1, rows, dh), lambda s, i: (s, i, 0, 0)),
                   pl.BlockSpec((1, 1, dh, rows), lambda s, i: (s, i, 0, 0))],
        compiler_params=_params(("parallel", "parallel")),
        name="nsa_compress",
    )(kv.reshape(bsz, rows, CMP_STRIDE, KV_COLS), w1s, w1, pe, w2, w2.transpose(0, 2, 1))


def _t5_bucket(dist):
    n = jnp.maximum(dist, 0)
    exact = REL_BUCKETS // 2
    nf = jnp.maximum(n, 1).astype(F32)
    large = exact + (jnp.log(nf / exact) / math.log(REL_MAX_DIST / exact)
                     * (REL_BUCKETS - exact)).astype(jnp.int32)
    large = jnp.minimum(large, REL_BUCKETS - 1)
    return jnp.where(n < exact, n, large)


def _bias_body(tab_ref, bc_ref, bt_ref, *, seq, n_cmp_pad):
    h = pl.program_id(0)
    tq = NSA_TQ

    def lookup(dist):
        bucket = _t5_bucket(dist)
        out = jnp.zeros(dist.shape, F32)
        for bkt in range(REL_BUCKETS):
            out = jnp.where(bucket == bkt, tab_ref[bkt, h] * LOG2E, out)
        return out

    key = lax.broadcasted_iota(jnp.int32, (tq, tq), 0)
    qry = lax.broadcasted_iota(jnp.int32, (tq, tq), 1)
    bt_ref[0, BT_DIAG] = jnp.where(qry >= key, lookup(qry - key), MASKED)
    bt_ref[0, BT_PREV] = lookup(tq + qry - key)
    bt_ref[0, BT_FAR] = lookup(2 * tq + qry - key)
    bt_ref[0, BT_WIN_EDGE] = jnp.where(qry < key, lookup(WINDOW + qry - key), MASKED)
    bt_ref[0, BT_NONE] = jnp.full((tq, tq), MASKED, F32)

    cmp_end = lax.broadcasted_iota(jnp.int32, (n_cmp_pad, tq), 0) * CMP_STRIDE + CMP_LEN - 1
    qry_c = lax.broadcasted_iota(jnp.int32, (n_cmp_pad, tq), 1)

    def cmp_tile(i, carry):
        bc_ref[0, i] = lookup(i * tq + qry_c - cmp_end)
        return carry

    lax.fori_loop(0, seq // tq, cmp_tile, 0)


def _bias_tables(rel_bias, seq):
    g, hpg, tq = NSA_KV_GROUPS, NSA_HPG, NSA_TQ
    n_cmp_pad = seq // CMP_STRIDE
    nq = seq // tq
    return pl.pallas_call(
        functools.partial(_bias_body, seq=seq, n_cmp_pad=n_cmp_pad),
        out_shape=[jax.ShapeDtypeStruct((g, nq, n_cmp_pad, hpg * tq), F32),
                   jax.ShapeDtypeStruct((g, BT_COUNT, tq, hpg * tq), F32)],
        grid=(NSA_HEADS,),
        in_specs=[pl.BlockSpec(memory_space=pltpu.SMEM)],
        out_specs=[pl.BlockSpec((1, nq, n_cmp_pad, tq), lambda h: (h // hpg, 0, 0, h % hpg)),
                   pl.BlockSpec((1, BT_COUNT, tq, tq), lambda h: (h // hpg, 0, 0, h % hpg))],
        compiler_params=_params(("parallel",)),
        name="nsa_bias",
    )(rel_bias)


def _eye(n, dtype):
    return (lax.broadcasted_iota(jnp.int32, (n, n), 0)
            == lax.broadcasted_iota(jnp.int32, (n, n), 1)).astype(dtype)


def _nsa_body(q_ref, kc_ref, vct_ref, ks_ref, vst_ref, kw_ref, vwt_ref, bc_ref, bt_ref, gate_ref,
              o_ref):
    tq, hpg, dh = NSA_TQ, NSA_HPG, NSA_HEAD_DIM
    rws = hpg * tq
    n_blk_log2 = SEL_BLOCK.bit_length() - 1
    n_blk = ks_ref.shape[1] // SEL_BLOCK
    n_cmp_pad = kc_ref.shape[2]
    kw = ks_ref.shape[2]
    i = pl.program_id(2)

    def per_head(x):
        return [x[:, hh * tq:(hh + 1) * tq] for hh in range(hpg)]

    xq = (q_ref[0] * (dh ** -0.5 * LOG2E)).astype(BF16)
    eye_d = _eye(dh, BF16)
    q_t = jnp.concatenate([_dot_nt(eye_d, xq[:, hh * dh:(hh + 1) * dh]) for hh in range(hpg)],
                          axis=1).astype(BF16)

    cmp_id = lax.broadcasted_iota(jnp.int32, (n_cmp_pad, rws), 0)
    t_pos = i * tq + (lax.broadcasted_iota(jnp.int32, (n_cmp_pad, rws), 1) & (tq - 1))
    valid = (t_pos - (cmp_id * CMP_STRIDE + CMP_LEN - 1) >= 0) & (cmp_id < n_cmp_pad - 1)
    s = jnp.where(valid, _dot(kc_ref[0, 0].astype(BF16), q_t) + bc_ref[0, 0], MASKED)
    e = jnp.where(valid, jnp.exp2(s - jnp.max(s, axis=0, keepdims=True)), 0.0)
    den = jnp.sum(e, axis=0, keepdims=True)
    p_c = e / jnp.where(den > 0.0, den, 1.0)
    o_cmp = _dot(vct_ref[0, 0].astype(BF16), p_c.astype(BF16))

    p_heads = per_head(p_c)
    p_sum = p_heads[0]
    for ph in p_heads[1:]:
        p_sum = p_sum + ph
    blk_o = lax.broadcasted_iota(jnp.int32, (n_blk, n_cmp_pad), 0)
    cmp_o = lax.broadcasted_iota(jnp.int32, (n_blk, n_cmp_pad), 1)
    overlap_t = ((cmp_o * CMP_STRIDE <= blk_o * SEL_BLOCK + SEL_BLOCK - 1)
                 & (cmp_o * CMP_STRIDE + CMP_LEN - 1 >= blk_o * SEL_BLOCK)).astype(F32)
    imp = _dot(overlap_t, p_sum, HI)
    jj = lax.broadcasted_iota(jnp.int32, (n_blk, tq), 0)
    cur = (i * tq + lax.broadcasted_iota(jnp.int32, (n_blk, tq), 1)) >> n_blk_log2
    forced = (jj == 0) | (jj == cur) | (jj == cur - 1)
    imp = jnp.where(jj > cur, -1e6, jnp.where(forced, 1e6, imp))
    rank = jnp.zeros((n_blk, tq), jnp.int32)
    for a in range(n_blk):
        row = imp[a:a + 1, :]
        beats = (row > imp) | ((row == imp) & (a < jj))
        rank = rank + beats.astype(jnp.int32)
    not_sel = jnp.where(rank < SEL_TOP_N, 0.0, 1.0).astype(BF16)

    q_aug = jnp.concatenate([q_t, jnp.concatenate([not_sel] * hpg, axis=1),
                             jnp.zeros((kw - dh - n_blk, rws), BF16)], axis=0)
    q_pad = jnp.concatenate([q_t, jnp.zeros((kw - dh, rws), BF16)], axis=0)


    n_win = WINDOW // tq
    tiles, scores = [], []
    for delta in range(n_win + 1):
        entry = {0: BT_DIAG, 1: BT_PREV, n_win: BT_WIN_EDGE}.get(delta, BT_FAR)
        if delta > 0:
            entry = jnp.where(i - delta >= 0, entry, BT_NONE)
        tiles.append(jnp.maximum(i - delta, 0))
        off = pl.multiple_of(tiles[-1] * tq, tq)
        scores.append(_dot(kw_ref[0, pl.ds(off, tq), :], q_pad) + bt_ref[0, entry])
    m_all = scores[0]
    for sc in scores[1:]:
        m_all = jnp.maximum(m_all, sc)
    m_w = jnp.max(m_all, axis=0, keepdims=True)
    acc_w = jnp.zeros((vwt_ref.shape[2], rws), F32)
    for kt, sc in zip(tiles, scores):
        acc_w = acc_w + _dot(vwt_ref[0, kt], jnp.exp2(sc - m_w).astype(BF16))
    o_win = acc_w[0:dh] / acc_w[dh:dh + 1]

    g_t = _dot_nt(_eye(GATE_PAD, F32), gate_ref[0], HI)
    sig = jax.nn.sigmoid(g_t[0:4 * hpg])
    gate = [jnp.concatenate([sig[br * hpg + hh:br * hpg + hh + 1] for hh in range(hpg)], axis=1)
            for br in range(3)]
    def sel_step(kt, carry):
        m_prev, acc = carry
        off = pl.multiple_of(kt * tq, tq)
        scores = _dot(ks_ref[0, pl.ds(off, tq), :], q_aug) + bt_ref[0, jnp.minimum(i - kt, BT_FAR)]
        m_new = jnp.maximum(m_prev, jnp.max(scores, axis=0, keepdims=True))
        pr = jnp.exp2(scores - m_new).astype(BF16)
        return m_new, jnp.exp2(m_prev - m_new) * acc + _dot(vst_ref[0, kt], pr)

    init = (jnp.full((1, rws), MASKED, F32), jnp.zeros((vst_ref.shape[2], rws), F32))
    _, acc_s = lax.fori_loop(0, i + 1, sel_step, init)
    o_sel = acc_s[0:dh] / acc_s[dh:dh + 1]

    y_t = (gate[0] * o_cmp + gate[1] * o_sel + gate[2] * o_win).astype(BF16)
    eye_q = _eye(tq, BF16)
    o_ref[0] = jnp.concatenate([_dot_nt(eye_q, yh) for yh in per_head(y_t)],
                               axis=1).astype(BF16)


def _nsa_attention(proj, kv, kv_t, kvc, kvc_t, bias_c, bias_t, *, bsz, seq):
    g, tq, hpg, dh = NSA_KV_GROUPS, NSA_TQ, NSA_HPG, NSA_HEAD_DIM
    n_rows = kvc.shape[2]
    nq = seq // tq
    group_w = hpg * dh
    kv_spec = lambda pos: pl.BlockSpec((1, seq, LANES), lambda b, gi, i: (b, 0, pos * g + gi))
    kvt_spec = lambda pos: pl.BlockSpec((1, nq, LANES, tq), lambda b, gi, i: (b, 0, pos * g + gi, 0))
    kv3 = kv.reshape(bsz, seq, KV_COLS)
    proj3 = proj.reshape(bsz, seq, PROJ_COLS)
    out = pl.pallas_call(
        _nsa_body,
        out_shape=jax.ShapeDtypeStruct((bsz, seq, NSA_DIM), BF16),
        grid=(bsz, g, nq),
        in_specs=[
            pl.BlockSpec((1, tq, group_w), lambda b, gi, i: (b, i, COL_QNSA // group_w + gi)),
            pl.BlockSpec((1, 1, n_rows, dh), lambda b, gi, i: (0, b * g + gi, 0, 0)),
            pl.BlockSpec((1, 1, dh, n_rows), lambda b, gi, i: (1, b * g + gi, 0, 0)),
            kv_spec(ROW_K_SEL), kvt_spec(T_V_SEL), kv_spec(ROW_K_WIN), kvt_spec(T_V_WIN),
            pl.BlockSpec((1, 1, n_rows, hpg * tq), lambda b, gi, i: (gi, i, 0, 0)),
            pl.BlockSpec((1, BT_COUNT, tq, hpg * tq), lambda b, gi, i: (gi, 0, 0, 0)),
            pl.BlockSpec((1, tq, GATE_PAD), lambda b, gi, i: (b, i, COL_GNSA // GATE_PAD + gi)),
        ],
        out_specs=pl.BlockSpec((1, tq, group_w), lambda b, gi, i: (b, i, gi)),
        compiler_params=_params(("parallel", "parallel", "arbitrary")),
        name="nsa_attention",
    )(proj3, kvc, kvc_t, kv3, kv_t, kv3, kv_t, bias_c, bias_t, proj3)
    return out.reshape(bsz * seq, NSA_DIM)


def _mem_body(q_ref, kv_ref, o_ref):
    outs = []
    for h in range(MEM_HEADS):
        sl = slice(h * MEM_HEAD_DIM, (h + 1) * MEM_HEAD_DIM)
        qh = (q_ref[:, sl] * (MEM_HEAD_DIM ** -0.5)).astype(BF16)
        kh = kv_ref[0, :, sl].astype(BF16)
        vh = kv_ref[0, :, MEM_DIM + h * MEM_HEAD_DIM:MEM_DIM + (h + 1) * MEM_HEAD_DIM].astype(BF16)
        s = _dot_nt(qh, kh)
        e = jnp.exp(s - jnp.max(s, axis=-1, keepdims=True))
        p = e / jnp.sum(e, axis=-1, keepdims=True)
        outs.append(_dot(p.astype(BF16), vh))
    o_ref[...] = jnp.concatenate(outs, axis=1)


def _mem_attention(proj, kv, *, seq, tq=512):
    t = proj.shape[0]
    bsz, m, _ = kv.shape
    per_seq = seq // tq
    return pl.pallas_call(
        _mem_body,
        out_shape=jax.ShapeDtypeStruct((t, MEM_DIM), F32),
        grid=(t // tq,),
        in_specs=[
            pl.BlockSpec((tq, MEM_DIM), lambda i: (i, COL_QMEM // MEM_DIM)),
            pl.BlockSpec((1, m, 2 * MEM_DIM), lambda i: (i // per_seq, 0, 0)),
        ],
        out_specs=pl.BlockSpec((tq, MEM_DIM), lambda i: (i, 0)),
        compiler_params=_params(("parallel",)),
        name="mem_attention",
    )(proj, kv)


def _merge_body(x_ref, gain_ref, yr_ref, yn_ref, ym_ref, wg_ref, wr_ref, wn_ref, wm_ref, wo_ref,
                o_ref):
    d = x_ref.shape[1]
    x = x_ref[...]
    gates = jax.nn.sigmoid(_dot(_rms(x, gain_ref[...]).astype(BF16), wg_ref[...]))
    merged = (gates[:, 0:d] * _dot(yr_ref[...].astype(BF16), wr_ref[...])
              + gates[:, d:2 * d] * _dot(yn_ref[...], wn_ref[...])
              + gates[:, 2 * d:3 * d] * _dot(ym_ref[...].astype(BF16), wm_ref[...]))
    o_ref[...] = x + _dot(merged.astype(BF16), wo_ref[...])


def _merge(x, gain, y_rwkv, y_nsa, y_mem, w_g, w_r, w_n, w_m, w_o, *, tm=512):
    t, d = x.shape
    row = lambda i: (i, 0)
    const = lambda i: (0, 0)
    return pl.pallas_call(
        _merge_body,
        out_shape=jax.ShapeDtypeStruct((t, d), F32),
        grid=(t // tm,),
        in_specs=[
            pl.BlockSpec((tm, d), row),
            pl.BlockSpec((1, d), const),
            pl.BlockSpec((tm, RWKV_DIM), row),
            pl.BlockSpec((tm, NSA_DIM), row),
            pl.BlockSpec((tm, MEM_DIM), row),
            pl.BlockSpec((d, N_BRANCH * d), const),
            pl.BlockSpec((RWKV_DIM, d), const),
            pl.BlockSpec((NSA_DIM, d), const),
            pl.BlockSpec((MEM_DIM, d), const),
            pl.BlockSpec((d, d), const),
        ],
        out_specs=pl.BlockSpec((tm, d), row),
        compiler_params=_params(("parallel",)),
        name="merge",
    )(x, gain, y_rwkv, y_nsa, y_mem, w_g, w_r, w_n, w_m, w_o)


def _row(a):
    return a.reshape(1, -1)


def _in_proj(x, p, l, seq):
    d = x.shape[1]
    g, hpg, dh = NSA_KV_GROUPS, NSA_HPG, NSA_HEAD_DIM
    w_in = p['w_in'][l]
    o = 0
    parts = {}
    for name, size in (('rwkv', RWKV_PROJ), ('q', NSA_DIM), ('kv', KV_KINDS * NSA_KV_DIM),
                       ('g_nsa', 3 * NSA_HEADS), ('q_mem', MEM_DIM), ('g_br', N_BRANCH * d)):
        parts[name] = w_in[:, o:o + size]
        o += size
    gates = parts['g_nsa'].reshape(d, 3, g, hpg).transpose(0, 2, 1, 3).reshape(d, g, 3 * hpg)
    gates = jnp.pad(gates, ((0, 0), (0, 0), (0, GATE_PAD - 3 * hpg))).reshape(d, g * GATE_PAD)
    w_main = jnp.concatenate([parts['rwkv'], parts['q'], gates, parts['q_mem']], axis=1).astype(BF16)
    w_kv = jnp.pad(parts['kv'].reshape(d, KV_KINDS, g, dh), ((0, 0), (0, 0), (0, 0), (0, LANES - dh)))
    w_row = w_kv[:, KV_ROW_KINDS, :, :].reshape(d, KV_COLS).astype(BF16)
    w_t = w_kv[:, KV_T_KINDS, :, :].reshape(d, KV_T_ROWS).T.astype(BF16)
    gain = _row(p['mix_norm'][l])
    proj = _norm_matmul(x, gain, w_main, tm=1024, tn=PROJ_TN, name="in_proj")
    kv, kv_t = _kv_proj(x, gain, w_row, w_t, seq=seq)
    return proj, kv, kv_t


def _rwkv_branch(proj, p, l, bsz, seq):
    w_lora = jnp.zeros((LORA_DIM, 3 * RWKV_DIM), F32)
    w_lora = w_lora.at[0:DECAY_LORA, 0:RWKV_DIM].set(p['rwkv_w2'][l])
    w_lora = w_lora.at[DECAY_LORA:DECAY_LORA + AAA_LORA, RWKV_DIM:2 * RWKV_DIM].set(p['rwkv_a2'][l])
    w_lora = w_lora.at[DECAY_LORA + AAA_LORA:, 2 * RWKV_DIM:].set(p['rwkv_g2'][l])
    r, k, v, kk, b, lw, gate = _rwkv_prep(
        proj, seq, _row(p['rwkv_mu'][l]), w_lora.astype(BF16), _row(p['rwkv_w0'][l]),
        _row(p['rwkv_a0'][l]), _row(p['rwkv_k_k'][l]), _row(p['rwkv_k_a'][l]))
    return _rwkv_scan(r, k, v, kk, b, lw, gate, _row(p['rwkv_r_k'][l]),
                      _row(p['rwkv_gn_gain'][l]), _row(p['rwkv_gn_bias'][l]), batch=bsz, seq=seq)


def _nsa_branch(proj, kv, kv_t, bias_c, bias_t, p, l, bsz, seq):
    w1 = jnp.stack([p['cmp_k_w1'][l], p['cmp_v_w1'][l]]).astype(BF16)
    pe = jnp.stack([p['cmp_pe_k'][l].reshape(1, -1), p['cmp_pe_v'][l].reshape(1, -1)])
    w2 = jnp.stack([p['cmp_k_w2'][l], p['cmp_v_w2'][l]]).astype(BF16)
    kvc, kvc_t = _compress(kv, w1, pe, w2, bsz=bsz, seq=seq)
    return _nsa_attention(proj, kv, kv_t, kvc, kvc_t, bias_c, bias_t, bsz=bsz, seq=seq)


def _mem_branch(proj, mem, p, l):
    bsz, m_tok, d = mem.shape
    w_kv = jnp.concatenate([p['mem_w_k'][l], p['mem_w_v'][l]], axis=1).astype(BF16)
    kv_mem = _norm_matmul(mem.reshape(bsz * m_tok, d), _row(p['mem_norm'][l]), w_kv,
                          tm=min(1024, bsz * m_tok), tn=512, name="mem_kv")
    return _mem_attention(proj, kv_mem.reshape(bsz, m_tok, 2 * MEM_DIM),
                          seq=proj.shape[0] // bsz)


def _layer(x, mem, l, bias_c, bias_t, p):
    bsz, seq, d = x.shape
    t = bsz * seq
    row = _row
    x = x.reshape(t, d)

    x = _ffn(x, row(p['ffn1_norm'][l]), p['ffn1_w_gate'][l].astype(BF16),
             p['ffn1_w_up'][l].astype(BF16), p['ffn1_w_down'][l].astype(BF16),
             row(p['final_norm']), final=False)

    proj, kv, kv_t = _in_proj(x, p, l, seq)
    y_rwkv = _rwkv_branch(proj, p, l, bsz, seq)
    y_nsa = _nsa_branch(proj, kv, kv_t, bias_c, bias_t, p, l, bsz, seq)
    y_mem = _mem_branch(proj, mem, p, l)

    w_gate = p['w_in'][l][:, -N_BRANCH * d:].astype(BF16)
    x = _merge(x, row(p['mix_norm'][l]), y_rwkv, y_nsa, y_mem, w_gate,
               p['w_br_rwkv'][l].astype(BF16), p['w_br_nsa'][l].astype(BF16),
               p['w_br_mem'][l].astype(BF16), p['w_out'][l].astype(BF16))

    last = l == p['ffn1_norm'].shape[0] - 1
    x = _ffn(x, row(p['ffn2_norm'][l]), p['ffn2_w_gate'][l].astype(BF16),
             p['ffn2_w_up'][l].astype(BF16), p['ffn2_w_down'][l].astype(BF16),
             row(p['final_norm']), final=last)
    return x.reshape(bsz, seq, d)


def kernel(x, mem, ffn1_norm, ffn1_w_gate, ffn1_w_up, ffn1_w_down, mix_norm, w_in, rwkv_mu, rwkv_w0, rwkv_w2, rwkv_a0, rwkv_a2, rwkv_g2, rwkv_k_k, rwkv_k_a, rwkv_r_k, rwkv_gn_gain, rwkv_gn_bias, cmp_pe_k, cmp_k_w1, cmp_k_w2, cmp_pe_v, cmp_v_w1, cmp_v_w2, rel_bias, mem_norm, mem_w_k, mem_w_v, w_br_rwkv, w_br_nsa, w_br_mem, w_out, ffn2_norm, ffn2_w_gate, ffn2_w_up, ffn2_w_down, final_norm):
    p = dict(ffn1_norm=ffn1_norm, ffn1_w_gate=ffn1_w_gate, ffn1_w_up=ffn1_w_up,
             ffn1_w_down=ffn1_w_down, mix_norm=mix_norm, w_in=w_in, rwkv_mu=rwkv_mu,
             rwkv_w0=rwkv_w0, rwkv_w2=rwkv_w2, rwkv_a0=rwkv_a0, rwkv_a2=rwkv_a2, rwkv_g2=rwkv_g2,
             rwkv_k_k=rwkv_k_k, rwkv_k_a=rwkv_k_a, rwkv_r_k=rwkv_r_k, rwkv_gn_gain=rwkv_gn_gain,
             rwkv_gn_bias=rwkv_gn_bias, cmp_pe_k=cmp_pe_k, cmp_k_w1=cmp_k_w1, cmp_k_w2=cmp_k_w2,
             cmp_pe_v=cmp_pe_v, cmp_v_w1=cmp_v_w1, cmp_v_w2=cmp_v_w2, mem_norm=mem_norm,
             mem_w_k=mem_w_k, mem_w_v=mem_w_v, w_br_rwkv=w_br_rwkv, w_br_nsa=w_br_nsa,
             w_br_mem=w_br_mem, w_out=w_out, ffn2_norm=ffn2_norm, ffn2_w_gate=ffn2_w_gate,
             ffn2_w_up=ffn2_w_up, ffn2_w_down=ffn2_w_down, final_norm=final_norm)
    bias_c, bias_t = _bias_tables(rel_bias, x.shape[1])
    for l in range(ffn1_norm.shape[0]):
        x = _layer(x, mem, l, bias_c, bias_t, p)
    return x
```

```python
import functools
import math

import jax
import jax.numpy as jnp
from jax import lax
from jax.experimental import pallas as pl
from jax.experimental.pallas import tpu as pltpu

F32 = jnp.float32
BF16 = jnp.bfloat16
HI = lax.Precision.HIGHEST

D_MODEL = 1024
NORM_EPS = 1e-6
D_FF = 2816
RWKV_HEADS = 8
RWKV_HEAD_DIM = 64
RWKV_DIM = RWKV_HEADS * RWKV_HEAD_DIM
DECAY_LORA = 64
AAA_LORA = 64
GATE_LORA = 128
LORA_DIM = DECAY_LORA + AAA_LORA + GATE_LORA
RWKV_GN_EPS = 64e-5
RWKV_PROJ = 3 * RWKV_DIM + LORA_DIM
NSA_HEADS = 8
NSA_KV_GROUPS = 2
NSA_HPG = NSA_HEADS // NSA_KV_GROUPS
NSA_HEAD_DIM = 64
NSA_DIM = NSA_HEADS * NSA_HEAD_DIM
NSA_KV_DIM = NSA_KV_GROUPS * NSA_HEAD_DIM
CMP_LEN = 32
CMP_STRIDE = 16
CMP_HIDDEN = 256
SEL_BLOCK = 64
SEL_TOP_N = 16
WINDOW = 512
REL_BUCKETS = 32
REL_MAX_DIST = 128
MEM_HEADS = 4
MEM_HEAD_DIM = 128
MEM_DIM = MEM_HEADS * MEM_HEAD_DIM
N_BRANCH = 3

LANES = 128
GATE_PAD = LANES
COL_QNSA = 0
COL_QMEM = COL_QNSA + NSA_DIM
COL_GNSA = COL_QMEM + MEM_DIM
PROJ_COLS = COL_GNSA + NSA_KV_GROUPS * GATE_PAD
assert COL_QMEM % MEM_DIM == 0 and COL_GNSA % GATE_PAD == 0
KV_KINDS = 6
KV_ROW_KINDS = (0, 1, 2, 4)
KV_T_KINDS = (3, 5)
KV_COLS = len(KV_ROW_KINDS) * NSA_KV_GROUPS * LANES
KV_T_ROWS = len(KV_T_KINDS) * NSA_KV_GROUPS * LANES
ROW_K_SEL, ROW_K_WIN = KV_ROW_KINDS.index(2), KV_ROW_KINDS.index(4)
T_V_SEL, T_V_WIN = KV_T_KINDS.index(3), KV_T_KINDS.index(5)
LOG2E = 1.4426950408889634

RWKV_CHUNK = 64
NSA_TQ = 256
MASKED = -1e30
BT_DIAG, BT_PREV, BT_FAR, BT_WIN_EDGE, BT_NONE, BT_COUNT = 0, 1, 2, 3, 4, 5
VMEM_LIMIT = 56 * 1024 * 1024


def _dot(a, b, precision=None):
    return jnp.dot(a, b, preferred_element_type=F32, precision=precision)


def _dot_nt(a, b, precision=None):
    return lax.dot_general(a, b, (((1,), (1,)), ((), ())), preferred_element_type=F32,
                           precision=precision)


def _params(semantics):
    return pltpu.CompilerParams(dimension_semantics=semantics, vmem_limit_bytes=VMEM_LIMIT)


def _rms(x, g):
    return x * lax.rsqrt(jnp.mean(x * x, axis=-1, keepdims=True) + NORM_EPS) * g


def _ffn_body(x_ref, g_ref, wg_ref, wu_ref, wd_ref, fg_ref, o_ref, *, final):
    x = x_ref[...]
    h = _rms(x, g_ref[...]).astype(BF16)
    acc = jnp.zeros(x.shape, F32)
    for j in range(wg_ref.shape[0]):
        act = (jax.nn.silu(_dot(h, wg_ref[j])) * _dot(h, wu_ref[j])).astype(BF16)
        acc = acc + _dot(act, wd_ref[j])
    y = x + 0.5 * acc
    if final:
        y = _rms(y, fg_ref[...])
    o_ref[...] = y


def _ffn(x, gain, wg, wu, wd, final_gain, *, final, tm=1024, tf=256):
    t, d = x.shape
    f = wg.shape[1]
    nf = f // tf
    chunked = lambda w: w.reshape(d, nf, tf).transpose(1, 0, 2)
    resident = lambda shape: pl.BlockSpec(shape, lambda i: (0,) * len(shape),
                                          pipeline_mode=pl.Buffered(1))
    return pl.pallas_call(
        functools.partial(_ffn_body, final=final),
        out_shape=jax.ShapeDtypeStruct((t, d), F32),
        grid=(t // tm,),
        in_specs=[
            pl.BlockSpec((tm, d), lambda i: (i, 0)),
            pl.BlockSpec((1, d), lambda i: (0, 0)),
            resident((nf, d, tf)),
            resident((nf, d, tf)),
            resident((nf, tf, d)),
            pl.BlockSpec((1, d), lambda i: (0, 0)),
        ],
        out_specs=pl.BlockSpec((tm, d), lambda i: (i, 0)),
        compiler_params=_params(("parallel",)),
        name="ffn_final" if final else "ffn",
    )(x, gain, chunked(wg), chunked(wu), wd.reshape(nf, tf, d), final_gain)


def _norm_matmul_body(x_ref, g_ref, w_ref, o_ref, h_ref):
    @pl.when(pl.program_id(1) == 0)
    def _():
        h_ref[...] = _rms(x_ref[...], g_ref[...]).astype(BF16)

    o_ref[...] = _dot(h_ref[...], w_ref[...])


def _norm_matmul(x, gain, w, *, tm, tn, name):
    t, d = x.shape
    n = w.shape[1]
    return pl.pallas_call(
        _norm_matmul_body,
        out_shape=jax.ShapeDtypeStruct((t, n), F32),
        grid=(t // tm, n // tn),
        in_specs=[
            pl.BlockSpec((tm, d), lambda i, j: (i, 0)),
            pl.BlockSpec((1, d), lambda i, j: (0, 0)),
            pl.BlockSpec((d, tn), lambda i, j: (0, j)),
        ],
        out_specs=pl.BlockSpec((tm, tn), lambda i, j: (i, j)),
        scratch_shapes=[pltpu.VMEM((tm, d), BF16)],
        compiler_params=_params(("parallel", "arbitrary")),
        name=name,
    )(x, gain, w)


def _kv_proj_body(x_ref, g_ref, w_ref, wt_ref, o_ref, ot_ref, *, tm, seq):
    dh, tk = NSA_HEAD_DIM, NSA_TQ
    h = _rms(x_ref[...], g_ref[...]).astype(BF16)
    y = _dot(h, w_ref[...])
    row = lax.broadcasted_iota(jnp.int32, (tm, LANES), 0)
    lane = lax.broadcasted_iota(jnp.int32, (tm, LANES), 1)
    pos = (pl.program_id(0) * tm) % seq + row
    block_mark = jnp.where(lane - dh == (pos >> (SEL_BLOCK.bit_length() - 1)), MASKED, 0.0)
    for tile in range(KV_COLS // LANES):
        part = y[:, tile * LANES:(tile + 1) * LANES]
        if tile // NSA_KV_GROUPS == ROW_K_SEL:
            part = part + block_mark
        o_ref[:, tile * LANES:(tile + 1) * LANES] = part.astype(BF16)

    y_t = _dot_nt(wt_ref[...], h)
    row_t = lax.broadcasted_iota(jnp.int32, y_t.shape, 0)
    y_t = (y_t + jnp.where((row_t & (LANES - 1)) == dh, 1.0, 0.0)).astype(BF16)
    for c in range(tm // tk):
        ot_ref[0, c] = y_t[:, c * tk:(c + 1) * tk]


def _kv_proj(x, gain, w, w_t, *, seq, tm=512):
    t, d = x.shape
    tk = NSA_TQ
    per_seq = seq // tm
    return pl.pallas_call(
        functools.partial(_kv_proj_body, tm=tm, seq=seq),
        out_shape=[jax.ShapeDtypeStruct((t, KV_COLS), BF16),
                   jax.ShapeDtypeStruct((t // seq, seq // tk, KV_T_ROWS, tk), BF16)],
        grid=(t // tm,),
        in_specs=[
            pl.BlockSpec((tm, d), lambda i: (i, 0)),
            pl.BlockSpec((1, d), lambda i: (0, 0)),
            pl.BlockSpec((d, KV_COLS), lambda i: (0, 0)),
            pl.BlockSpec((KV_T_ROWS, d), lambda i: (0, 0)),
        ],
        out_specs=[pl.BlockSpec((tm, KV_COLS), lambda i: (i, 0)),
                   pl.BlockSpec((1, tm // tk, KV_T_ROWS, tk),
                                lambda i: (i // per_seq, i % per_seq, 0, 0))],
        compiler_params=_params(("parallel",)),
        name="nsa_kv_proj",
    )(x, gain, w, w_t)


def _in_proj_body(x_ref, xp_ref, g_ref, wr_ref, wo_ref, mu_ref, wl_ref, w0_ref, a0_ref, kk_ref,
                  ka_ref, proj_o, r_o, k_o, v_o, kk_o, b_o, lw_o, g_o, *, tiles_per_seq):
    i = pl.program_id(0)
    gain = g_ref[...]
    h = _rms(x_ref[...], gain).astype(BF16)
    proj_o[...] = _dot(h, wo_ref[...])
    p = _dot(h, wr_ref[...])
    keep = jnp.where(i % tiles_per_seq == 0, 0.0, 1.0)
    prev_last = _dot(_rms(xp_ref[...], gain).astype(BF16), wr_ref[...])[7:8, :] * keep
    rows = lax.broadcasted_iota(jnp.int32, p.shape, 0)
    shifted = jnp.where(rows == 0, prev_last, pltpu.roll(p, 1, 0))
    x = p + (shifted - p) * mu_ref[...]

    r = x[:, 0:RWKV_DIM]
    k = x[:, RWKV_DIM:2 * RWKV_DIM]
    v = x[:, 2 * RWKV_DIM:3 * RWKV_DIM]
    s = x[:, 3 * RWKV_DIM:RWKV_PROJ]
    lane = lax.broadcasted_iota(jnp.int32, s.shape, 1)
    z = jnp.where(lane < DECAY_LORA, jnp.tanh(s),
                  jnp.where(lane < DECAY_LORA + AAA_LORA, s, jax.nn.sigmoid(s)))
    lo = _dot(z.astype(BF16), wl_ref[...])
    a = jax.nn.sigmoid(a0_ref[...] + lo[:, RWKV_DIM:2 * RWKV_DIM])

    kkr = k * kk_ref[...]
    sq = kkr * kkr
    sq_hi = sq.astype(BF16)
    sq_lo = (sq - sq_hi.astype(F32)).astype(BF16)
    shift = RWKV_HEAD_DIM.bit_length() - 1
    same_head = ((lax.broadcasted_iota(jnp.int32, (RWKV_DIM, RWKV_DIM), 0) >> shift)
                 == (lax.broadcasted_iota(jnp.int32, (RWKV_DIM, RWKV_DIM), 1) >> shift)).astype(BF16)
    ssq = _dot(sq_hi, same_head) + _dot(sq_lo, same_head)
    kk = kkr / jnp.maximum(jnp.sqrt(ssq), 1e-12)

    r_o[...] = r
    k_o[...] = k * (1.0 + (a - 1.0) * ka_ref[...])
    v_o[...] = v
    kk_o[...] = kk
    b_o[...] = kk * a
    lw_o[...] = -math.exp(-0.5) * jax.nn.sigmoid(w0_ref[...] + lo[:, 0:RWKV_DIM])
    g_o[...] = lo[:, 2 * RWKV_DIM:3 * RWKV_DIM]


def _in_proj_call(x, seq, gain, w_rwkv, w_other, mu, w_lora, w0, a0, k_k, k_a, *, tm=512):
    t, d = x.shape
    row = lambda i: (i, 0)
    const = lambda i: (0, 0)
    resident = lambda shape: pl.BlockSpec(shape, const, pipeline_mode=pl.Buffered(1))
    vec = pl.BlockSpec((1, RWKV_DIM), const)
    tok = jax.ShapeDtypeStruct((t, RWKV_DIM), F32)
    return pl.pallas_call(
        functools.partial(_in_proj_body, tiles_per_seq=seq // tm),
        out_shape=[jax.ShapeDtypeStruct((t, PROJ_COLS), F32)] + [tok] * 7,
        grid=(t // tm,),
        in_specs=[
            pl.BlockSpec((tm, d), row),
            pl.BlockSpec((8, d), lambda i: (jnp.maximum(i * (tm // 8) - 1, 0), 0)),
            pl.BlockSpec((1, d), const),
            resident((d, RWKV_PROJ)),
            resident((d, PROJ_COLS)),
            pl.BlockSpec((1, RWKV_PROJ), const),
            resident((LORA_DIM, 3 * RWKV_DIM)),
            vec, vec, vec, vec,
        ],
        out_specs=[pl.BlockSpec((tm, PROJ_COLS), row)] + [pl.BlockSpec((tm, RWKV_DIM), row)] * 7,
        compiler_params=_params(("parallel",)),
        name="in_proj",
    )(x, x, gain, w_rwkv, w_other, mu, w_lora, w0, a0, k_k, k_a)


def _rwkv_scan_body(r_ref, k_ref, v_ref, kk_ref, b_ref, lw_ref, g_ref, rk_ref, gg_ref, gb_ref,
                    o_ref, st_ref):
    c_sz, n, nh = RWKV_CHUNK, RWKV_HEAD_DIM, RWKV_HEADS

    @pl.when(pl.program_id(1) == 0)
    def _():
        st_ref[...] = jnp.zeros_like(st_ref)

    ri = lax.broadcasted_iota(jnp.int32, (c_sz, c_sz), 0)
    ci = lax.broadcasted_iota(jnp.int32, (c_sz, c_sz), 1)
    incl = ci <= ri
    eye_b = (ci == ri).astype(BF16)
    row2 = lax.broadcasted_iota(jnp.int32, (c_sz, 2 * c_sz), 0)
    lane2 = lax.broadcasted_iota(jnp.int32, (c_sz, 2 * c_sz), 1)
    right_half = lane2 >= c_sz
    zeros_b = jnp.zeros((c_sz, n), BF16)

    rows = []
    for bb in range(st_ref.shape[0]):
        lw = lw_ref[bb]
        cum = _dot(incl.astype(F32), lw, HI)
        cum_last = cum[c_sz - 1:c_sz, :]
        r, k, v, b = r_ref[bb], k_ref[bb], v_ref[bb], b_ref[bb]
        p_inv = jnp.exp(-cum)
        p_end = jnp.exp(cum_last - cum)
        rows.append(dict(
            left=jnp.concatenate([(-(kk_ref[bb] * jnp.exp(cum - lw))).astype(BF16),
                                  (r * jnp.exp(cum)).astype(BF16)], axis=0),
            bt=(b * p_inv).astype(BF16), kt=(k * p_inv).astype(BF16),
            bh=(b * p_end).astype(BF16), kh=(k * p_end).astype(BF16),
            v=v, v_b=v.astype(BF16), d_p=jnp.exp(cum_last), rk=r * k * rk_ref[...]))

    units = [(bb, h) for bb in range(len(rows)) for h in range(nh)]
    col = lambda name, u: rows[u[0]][name][:, u[1] * n:(u[1] + 1) * n]
    a_all = [_dot_nt(col('left', u), jnp.concatenate([col('bt', u), col('kt', u)], axis=0))
             for u in units]
    key2 = jnp.where(right_half, lane2 - c_sz, lane2)
    w_u = [jnp.where(right_half & (key2 < row2), a[:c_sz], 0.0).astype(BF16) for a in a_all]
    w_y = [jnp.where(key2 <= row2, a[c_sz:], 0.0).astype(BF16) for a in a_all]

    x = [jnp.where(lane2 < row2, a[:c_sz], jnp.where(lane2 == row2 + c_sz, 1.0, 0.0)) for a in a_all]
    for _ in range(6):
        hi = [xu.astype(BF16) for xu in x]
        lo = [(xu - h_.astype(F32)).astype(BF16) for xu, h_ in zip(x, hi)]
        x = [_dot(h_[:, :c_sz], h_) + _dot(h_[:, :c_sz], l_) + _dot(l_[:, :c_sz], h_)
             + jnp.where(right_half, xu, 0.0) for xu, h_, l_ in zip(x, hi, lo)]
    x_b = [xu.astype(BF16) for xu in x]

    s0 = [st_ref[bb, h] for bb, h in units]
    ls0 = [_dot_nt(col('left', u), s0[j].astype(BF16)) for j, u in enumerate(units)]
    rhs = [ls0[j][:c_sz] + _dot(w_u[j], jnp.concatenate([zeros_b, col('v_b', u)], axis=0))
           for j, u in enumerate(units)]
    u_b = [_dot(x_b[j], jnp.concatenate([zeros_b, rhs[j].astype(BF16)], axis=0)).astype(BF16)
           for j in range(len(units))]
    uv = [jnp.concatenate([u_b[j], col('v_b', u)], axis=0) for j, u in enumerate(units)]
    y = [ls0[j][c_sz:] + _dot(w_y[j], uv[j]) for j in range(len(units))]
    uv_t = [_dot_nt(eye_b, uv_j).astype(BF16) for uv_j in uv]
    for j, u in enumerate(units):
        st_ref[u[0], u[1]] = (s0[j] * col('d_p', u)
                              + _dot(uv_t[j], jnp.concatenate([col('bh', u), col('kh', u)], axis=0)))

    for bb in range(len(rows)):
        outs = []
        for h in range(nh):
            sl = slice(h * n, (h + 1) * n)
            yh = y[bb * nh + h]
            mean = jnp.mean(yh, axis=-1, keepdims=True)
            var = jnp.mean(jnp.square(yh - mean), axis=-1, keepdims=True)
            yn = (yh - mean) * lax.rsqrt(var + RWKV_GN_EPS)
            yn = yn * gg_ref[:, sl] + gb_ref[:, sl]
            bonus = jnp.sum(rows[bb]['rk'][:, sl], axis=-1, keepdims=True) * rows[bb]['v'][:, sl]
            outs.append((yn + bonus) * g_ref[bb, :, sl])
        o_ref[bb] = jnp.concatenate(outs, axis=1)


def _rwkv_scan(r, k, v, kk, b, lw, g, r_k, gn_gain, gn_bias, *, batch, seq, nb=4):
    t = r.shape[0]
    nc = seq // RWKV_CHUNK
    tok = pl.BlockSpec((nb, RWKV_CHUNK, RWKV_DIM), lambda bi, c: (bi, c, 0))
    par = pl.BlockSpec((1, RWKV_DIM), lambda bi, c: (0, 0))
    per_batch = lambda a: a.reshape(batch, seq, RWKV_DIM)
    out = pl.pallas_call(
        _rwkv_scan_body,
        out_shape=jax.ShapeDtypeStruct((batch, seq, RWKV_DIM), F32),
        grid=(batch // nb, nc),
        in_specs=[tok] * 7 + [par] * 3,
        out_specs=tok,
        scratch_shapes=[pltpu.VMEM((nb, RWKV_HEADS, RWKV_HEAD_DIM, RWKV_HEAD_DIM), F32)],
        compiler_params=_params(("parallel", "arbitrary")),
        name="rwkv_scan",
    )(*(per_batch(a) for a in (r, k, v, kk, b, lw, g)), r_k, gn_gain, gn_bias)
    return out.reshape(t, RWKV_DIM)


def _compress_body(x_ref, w1s_ref, w1_ref, pe_ref, w2_ref, w2t_ref, o_ref, ot_ref):
    n_rows = x_ref.shape[1]
    both = jnp.zeros((n_rows, 2 * CMP_HIDDEN), F32)
    for l in range(CMP_STRIDE):
        both = both + _dot(x_ref[0, :, l, :], w1s_ref[0, l])
    second_next = pltpu.roll(both[:, CMP_HIDDEN:], n_rows - 1, 0)
    pe = jnp.broadcast_to(pe_ref[0], (8, pe_ref.shape[2])).astype(BF16)
    pe_term = _dot(pe, w1_ref[0])[0:1, :]
    hid = both[:, :CMP_HIDDEN] + second_next + pe_term
    act = jax.nn.gelu(hid).astype(BF16)
    o_ref[0, 0] = _dot(act, w2_ref[0])
    ot_ref[0, 0] = _dot_nt(w2t_ref[0], act)


def _compress(kv, w1, pe, w2, *, bsz, seq):
    g, dh = NSA_KV_GROUPS, NSA_HEAD_DIM
    rows = seq // CMP_STRIDE
    w1r = jnp.pad(w1.reshape(2, CMP_LEN, dh, CMP_HIDDEN), ((0, 0), (0, 0), (0, LANES - dh), (0, 0)))
    w1s = jnp.concatenate([w1r[:, :CMP_STRIDE], w1r[:, CMP_STRIDE:]], axis=3)
    return pl.pallas_call(
        _compress_body,
        out_shape=[jax.ShapeDtypeStruct((2, bsz * g, rows, dh), F32),
                   jax.ShapeDtypeStruct((2, bsz * g, dh, rows), F32)],
        grid=(2, bsz * g),
        in_specs=[
            pl.BlockSpec((1, rows, CMP_STRIDE, LANES), lambda s, i: (i // g, 0, 0, s * g + i % g)),
            pl.BlockSpec((1, CMP_STRIDE, LANES, 2 * CMP_HIDDEN), lambda s, i: (s, 0, 0, 0)),
            pl.BlockSpec((1, CMP_LEN * dh, CMP_HIDDEN), lambda s, i: (s, 0, 0)),
            pl.BlockSpec((1, 1, CMP_LEN * dh), lambda s, i: (s, 0, 0)),
            pl.BlockSpec((1, CMP_HIDDEN, dh), lambda s, i: (s, 0, 0)),
            pl.BlockSpec((1, dh, CMP_HIDDEN), lambda s, i: (s, 0, 0)),
        ],
        out_specs=[pl.BlockSpec((1, 1, rows, dh), lambda s, i: (s, i, 0, 0)),
                   pl.BlockSpec((1, 1, dh, rows), lambda s, i: (s, i, 0, 0))],
        compiler_params=_params(("parallel", "parallel")),
        name="nsa_compress",
    )(kv.reshape(bsz, rows, CMP_STRIDE, KV_COLS), w1s, w1, pe, w2, w2.transpose(0, 2, 1))


def _t5_bucket(dist):
    n = jnp.maximum(dist, 0)
    exact = REL_BUCKETS // 2
    nf = jnp.maximum(n, 1).astype(F32)
    large = exact + (jnp.log(nf / exact) / math.log(REL_MAX_DIST / exact)
                     * (REL_BUCKETS - exact)).astype(jnp.int32)
    large = jnp.minimum(large, REL_BUCKETS - 1)
    return jnp.where(n < exact, n, large)


def _bias_body(tab_ref, bc_ref, bt_ref, *, seq, n_cmp_pad):
    h = pl.program_id(0)
    tq = NSA_TQ

    def lookup(dist):
        bucket = _t5_bucket(dist)
        out = jnp.zeros(dist.shape, F32)
        for bkt in range(REL_BUCKETS):
            out = jnp.where(bucket == bkt, tab_ref[bkt, h] * LOG2E, out)
        return out

    key = lax.broadcasted_iota(jnp.int32, (tq, tq), 0)
    qry = lax.broadcasted_iota(jnp.int32, (tq, tq), 1)
    bt_ref[0, BT_DIAG] = jnp.where(qry >= key, lookup(qry - key), MASKED)
    bt_ref[0, BT_PREV] = lookup(tq + qry - key)
    bt_ref[0, BT_FAR] = lookup(2 * tq + qry - key)
    bt_ref[0, BT_WIN_EDGE] = jnp.where(qry < key, lookup(WINDOW + qry - key), MASKED)
    bt_ref[0, BT_NONE] = jnp.full((tq, tq), MASKED, F32)

    cmp_end = lax.broadcasted_iota(jnp.int32, (n_cmp_pad, tq), 0) * CMP_STRIDE + CMP_LEN - 1
    qry_c = lax.broadcasted_iota(jnp.int32, (n_cmp_pad, tq), 1)

    def cmp_tile(i, carry):
        bc_ref[0, i] = lookup(i * tq + qry_c - cmp_end)
        return carry

    lax.fori_loop(0, seq // tq, cmp_tile, 0)


def _bias_tables(rel_bias, seq):
    g, hpg, tq = NSA_KV_GROUPS, NSA_HPG, NSA_TQ
    n_cmp_pad = seq // CMP_STRIDE
    nq = seq // tq
    return pl.pallas_call(
        functools.partial(_bias_body, seq=seq, n_cmp_pad=n_cmp_pad),
        out_shape=[jax.ShapeDtypeStruct((g, nq, n_cmp_pad, hpg * tq), F32),
                   jax.ShapeDtypeStruct((g, BT_COUNT, tq, hpg * tq), F32)],
        grid=(NSA_HEADS,),
        in_specs=[pl.BlockSpec(memory_space=pltpu.SMEM)],
        out_specs=[pl.BlockSpec((1, nq, n_cmp_pad, tq), lambda h: (h // hpg, 0, 0, h % hpg)),
                   pl.BlockSpec((1, BT_COUNT, tq, tq), lambda h: (h // hpg, 0, 0, h % hpg))],
        compiler_params=_params(("parallel",)),
        name="nsa_bias",
    )(rel_bias)


def _eye(n, dtype):
    return (lax.broadcasted_iota(jnp.int32, (n, n), 0)
            == lax.broadcasted_iota(jnp.int32, (n, n), 1)).astype(dtype)


def _nsa_body(q_ref, kc_ref, vct_ref, ks_ref, vst_ref, kw_ref, vwt_ref, bc_ref, bt_ref, gate_ref,
              o_ref):
    tq, hpg, dh = NSA_TQ, NSA_HPG, NSA_HEAD_DIM
    rws = hpg * tq
    n_blk_log2 = SEL_BLOCK.bit_length() - 1
    n_blk = ks_ref.shape[1] // SEL_BLOCK
    n_cmp_pad = kc_ref.shape[2]
    kw = ks_ref.shape[2]
    i = pl.program_id(2)

    def per_head(x):
        return [x[:, hh * tq:(hh + 1) * tq] for hh in range(hpg)]

    xq = (q_ref[0] * (dh ** -0.5 * LOG2E)).astype(BF16)
    eye_d = _eye(dh, BF16)
    q_t = jnp.concatenate([_dot_nt(eye_d, xq[:, hh * dh:(hh + 1) * dh]) for hh in range(hpg)],
                          axis=1).astype(BF16)

    cmp_id = lax.broadcasted_iota(jnp.int32, (n_cmp_pad, rws), 0)
    t_pos = i * tq + (lax.broadcasted_iota(jnp.int32, (n_cmp_pad, rws), 1) & (tq - 1))
    valid = (t_pos - (cmp_id * CMP_STRIDE + CMP_LEN - 1) >= 0) & (cmp_id < n_cmp_pad - 1)
    s = jnp.where(valid, _dot(kc_ref[0, 0].astype(BF16), q_t) + bc_ref[0, 0], MASKED)
    e = jnp.where(valid, jnp.exp2(s - jnp.max(s, axis=0, keepdims=True)), 0.0)
    den = jnp.sum(e, axis=0, keepdims=True)
    p_c = e / jnp.where(den > 0.0, den, 1.0)
    o_cmp = _dot(vct_ref[0, 0].astype(BF16), p_c.astype(BF16))

    p_heads = per_head(p_c)
    p_sum = p_heads[0]
    for ph in p_heads[1:]:
        p_sum = p_sum + ph
    blk_o = lax.broadcasted_iota(jnp.int32, (n_blk, n_cmp_pad), 0)
    cmp_o = lax.broadcasted_iota(jnp.int32, (n_blk, n_cmp_pad), 1)
    overlap_t = ((cmp_o * CMP_STRIDE <= blk_o * SEL_BLOCK + SEL_BLOCK - 1)
                 & (cmp_o * CMP_STRIDE + CMP_LEN - 1 >= blk_o * SEL_BLOCK)).astype(F32)
    imp = _dot(overlap_t, p_sum, HI)
    jj = lax.broadcasted_iota(jnp.int32, (n_blk, tq), 0)
    cur = (i * tq + lax.broadcasted_iota(jnp.int32, (n_blk, tq), 1)) >> n_blk_log2
    forced = (jj == 0) | (jj == cur) | (jj == cur - 1)
    imp = jnp.where(jj > cur, -1e6, jnp.where(forced, 1e6, imp))
    rank = jnp.zeros((n_blk, tq), jnp.int32)
    for a in range(n_blk):
        row = imp[a:a + 1, :]
        beats = (row > imp) | ((row == imp) & (a < jj))
        rank = rank + beats.astype(jnp.int32)
    not_sel = jnp.where(rank < SEL_TOP_N, 0.0, 1.0).astype(BF16)

    q_aug = jnp.concatenate([q_t, jnp.concatenate([not_sel] * hpg, axis=1),
                             jnp.zeros((kw - dh - n_blk, rws), BF16)], axis=0)
    q_pad = jnp.concatenate([q_t, jnp.zeros((kw - dh, rws), BF16)], axis=0)


    n_win = WINDOW // tq
    tiles, scores = [], []
    for delta in range(n_win + 1):
        entry = {0: BT_DIAG, 1: BT_PREV, n_win: BT_WIN_EDGE}.get(delta, BT_FAR)
        if delta > 0:
            entry = jnp.where(i - delta >= 0, entry, BT_NONE)
        tiles.append(jnp.maximum(i - delta, 0))
        off = pl.multiple_of(tiles[-1] * tq, tq)
        scores.append(_dot(kw_ref[0, pl.ds(off, tq), :], q_pad) + bt_ref[0, entry])
    m_all = scores[0]
    for sc in scores[1:]:
        m_all = jnp.maximum(m_all, sc)
    m_w = jnp.max(m_all, axis=0, keepdims=True)
    acc_w = jnp.zeros((vwt_ref.shape[2], rws), F32)
    for kt, sc in zip(tiles, scores):
        acc_w = acc_w + _dot(vwt_ref[0, kt], jnp.exp2(sc - m_w).astype(BF16))
    o_win = acc_w[0:dh] / acc_w[dh:dh + 1]

    g_t = _dot_nt(_eye(GATE_PAD, F32), gate_ref[0], HI)
    sig = jax.nn.sigmoid(g_t[0:4 * hpg])
    gate = [jnp.concatenate([sig[br * hpg + hh:br * hpg + hh + 1] for hh in range(hpg)], axis=1)
            for br in range(3)]
    def sel_step(kt, carry):
        m_prev, acc = carry
        off = pl.multiple_of(kt * tq, tq)
        scores = _dot(ks_ref[0, pl.ds(off, tq), :], q_aug) + bt_ref[0, jnp.minimum(i - kt, BT_FAR)]
        m_new = jnp.maximum(m_prev, jnp.max(scores, axis=0, keepdims=True))
        pr = jnp.exp2(scores - m_new).astype(BF16)
        return m_new, jnp.exp2(m_prev - m_new) * acc + _dot(vst_ref[0, kt], pr)

    init = (jnp.full((1, rws), MASKED, F32), jnp.zeros((vst_ref.shape[2], rws), F32))
    _, acc_s = lax.fori_loop(0, i + 1, sel_step, init)
    o_sel = acc_s[0:dh] / acc_s[dh:dh + 1]

    y_t = (gate[0] * o_cmp + gate[1] * o_sel + gate[2] * o_win).astype(BF16)
    eye_q = _eye(tq, BF16)
    o_ref[0] = jnp.concatenate([_dot_nt(eye_q, yh) for yh in per_head(y_t)],
                               axis=1).astype(BF16)


def _nsa_attention(proj, kv, kv_t, kvc, kvc_t, bias_c, bias_t, *, bsz, seq):
    g, tq, hpg, dh = NSA_KV_GROUPS, NSA_TQ, NSA_HPG, NSA_HEAD_DIM
    n_rows = kvc.shape[2]
    nq = seq // tq
    group_w = hpg * dh
    kv_spec = lambda pos: pl.BlockSpec((1, seq, LANES), lambda b, gi, i: (b, 0, pos * g + gi))
    kvt_spec = lambda pos: pl.BlockSpec((1, nq, LANES, tq), lambda b, gi, i: (b, 0, pos * g + gi, 0))
    kv3 = kv.reshape(bsz, seq, KV_COLS)
    proj3 = proj.reshape(bsz, seq, PROJ_COLS)
    out = pl.pallas_call(
        _nsa_body,
        out_shape=jax.ShapeDtypeStruct((bsz, seq, NSA_DIM), BF16),
        grid=(bsz, g, nq),
        in_specs=[
            pl.BlockSpec((1, tq, group_w), lambda b, gi, i: (b, i, COL_QNSA // group_w + gi)),
            pl.BlockSpec((1, 1, n_rows, dh), lambda b, gi, i: (0, b * g + gi, 0, 0)),
            pl.BlockSpec((1, 1, dh, n_rows), lambda b, gi, i: (1, b * g + gi, 0, 0)),
            kv_spec(ROW_K_SEL), kvt_spec(T_V_SEL), kv_spec(ROW_K_WIN), kvt_spec(T_V_WIN),
            pl.BlockSpec((1, 1, n_rows, hpg * tq), lambda b, gi, i: (gi, i, 0, 0)),
            pl.BlockSpec((1, BT_COUNT, tq, hpg * tq), lambda b, gi, i: (gi, 0, 0, 0)),
            pl.BlockSpec((1, tq, GATE_PAD), lambda b, gi, i: (b, i, COL_GNSA // GATE_PAD + gi)),
        ],
        out_specs=pl.BlockSpec((1, tq, group_w), lambda b, gi, i: (b, i, gi)),
        compiler_params=_params(("parallel", "parallel", "arbitrary")),
        name="nsa_attention",
    )(proj3, kvc, kvc_t, kv3, kv_t, kv3, kv_t, bias_c, bias_t, proj3)
    return out.reshape(bsz * seq, NSA_DIM)


def _mem_body(q_ref, kv_ref, o_ref):
    outs = []
    for h in range(MEM_HEADS):
        sl = slice(h * MEM_HEAD_DIM, (h + 1) * MEM_HEAD_DIM)
        qh = (q_ref[:, sl] * (MEM_HEAD_DIM ** -0.5)).astype(BF16)
        kh = kv_ref[0, :, sl].astype(BF16)
        vh = kv_ref[0, :, MEM_DIM + h * MEM_HEAD_DIM:MEM_DIM + (h + 1) * MEM_HEAD_DIM].astype(BF16)
        s = _dot_nt(qh, kh)
        e = jnp.exp(s - jnp.max(s, axis=-1, keepdims=True))
        p = e / jnp.sum(e, axis=-1, keepdims=True)
        outs.append(_dot(p.astype(BF16), vh))
    o_ref[...] = jnp.concatenate(outs, axis=1)


def _mem_attention(proj, kv, *, seq, tq=512):
    t = proj.shape[0]
    bsz, m, _ = kv.shape
    per_seq = seq // tq
    return pl.pallas_call(
        _mem_body,
        out_shape=jax.ShapeDtypeStruct((t, MEM_DIM), F32),
        grid=(t // tq,),
        in_specs=[
            pl.BlockSpec((tq, MEM_DIM), lambda i: (i, COL_QMEM // MEM_DIM)),
            pl.BlockSpec((1, m, 2 * MEM_DIM), lambda i: (i // per_seq, 0, 0)),
        ],
        out_specs=pl.BlockSpec((tq, MEM_DIM), lambda i: (i, 0)),
        compiler_params=_params(("parallel",)),
        name="mem_attention",
    )(proj, kv)


def _merge_body(x_ref, gain_ref, yr_ref, yn_ref, ym_ref, wg_ref, wr_ref, wn_ref, wm_ref, wo_ref,
                o_ref):
    d = x_ref.shape[1]
    x = x_ref[...]
    gates = jax.nn.sigmoid(_dot(_rms(x, gain_ref[...]).astype(BF16), wg_ref[...]))
    merged = (gates[:, 0:d] * _dot(yr_ref[...].astype(BF16), wr_ref[...])
              + gates[:, d:2 * d] * _dot(yn_ref[...], wn_ref[...])
              + gates[:, 2 * d:3 * d] * _dot(ym_ref[...].astype(BF16), wm_ref[...]))
    o_ref[...] = x + _dot(merged.astype(BF16), wo_ref[...])


def _merge(x, gain, y_rwkv, y_nsa, y_mem, w_g, w_r, w_n, w_m, w_o, *, tm=512):
    t, d = x.shape
    row = lambda i: (i, 0)
    const = lambda i: (0, 0)
    return pl.pallas_call(
        _merge_body,
        out_shape=jax.ShapeDtypeStruct((t, d), F32),
        grid=(t // tm,),
        in_specs=[
            pl.BlockSpec((tm, d), row),
            pl.BlockSpec((1, d), const),
            pl.BlockSpec((tm, RWKV_DIM), row),
            pl.BlockSpec((tm, NSA_DIM), row),
            pl.BlockSpec((tm, MEM_DIM), row),
            pl.BlockSpec((d, N_BRANCH * d), const),
            pl.BlockSpec((RWKV_DIM, d), const),
            pl.BlockSpec((NSA_DIM, d), const),
            pl.BlockSpec((MEM_DIM, d), const),
            pl.BlockSpec((d, d), const),
        ],
        out_specs=pl.BlockSpec((tm, d), row),
        compiler_params=_params(("parallel",)),
        name="merge",
    )(x, gain, y_rwkv, y_nsa, y_mem, w_g, w_r, w_n, w_m, w_o)


def _row(a):
    return a.reshape(1, -1)


def _in_proj(x, p, l, seq):
    d = x.shape[1]
    g, hpg, dh = NSA_KV_GROUPS, NSA_HPG, NSA_HEAD_DIM
    w_in = p['w_in'][l]
    o = 0
    parts = {}
    for name, size in (('rwkv', RWKV_PROJ), ('q', NSA_DIM), ('kv', KV_KINDS * NSA_KV_DIM),
                       ('g_nsa', 3 * NSA_HEADS), ('q_mem', MEM_DIM)):
        parts[name] = w_in[:, o:o + size]
        o += size
    gates = parts['g_nsa'].reshape(d, 3, g, hpg).transpose(0, 2, 1, 3).reshape(d, g, 3 * hpg)
    gates = jnp.pad(gates, ((0, 0), (0, 0), (0, GATE_PAD - 3 * hpg))).reshape(d, g * GATE_PAD)
    w_other = jnp.concatenate([parts['q'], parts['q_mem'], gates], axis=1).astype(BF16)
    w_kv = jnp.pad(parts['kv'].reshape(d, KV_KINDS, g, dh), ((0, 0), (0, 0), (0, 0), (0, LANES - dh)))
    w_row = w_kv[:, KV_ROW_KINDS, :, :].reshape(d, KV_COLS).astype(BF16)
    w_t = w_kv[:, KV_T_KINDS, :, :].reshape(d, KV_T_ROWS).T.astype(BF16)
    w_lora = jnp.zeros((LORA_DIM, 3 * RWKV_DIM), F32)
    w_lora = w_lora.at[0:DECAY_LORA, 0:RWKV_DIM].set(p['rwkv_w2'][l])
    w_lora = w_lora.at[DECAY_LORA:DECAY_LORA + AAA_LORA, RWKV_DIM:2 * RWKV_DIM].set(p['rwkv_a2'][l])
    w_lora = w_lora.at[DECAY_LORA + AAA_LORA:, 2 * RWKV_DIM:].set(p['rwkv_g2'][l])
    gain = _row(p['mix_norm'][l])
    proj, *rwkv_in = _in_proj_call(
        x, seq, gain, parts['rwkv'].astype(BF16), w_other, _row(p['rwkv_mu'][l]),
        w_lora.astype(BF16), _row(p['rwkv_w0'][l]), _row(p['rwkv_a0'][l]),
        _row(p['rwkv_k_k'][l]), _row(p['rwkv_k_a'][l]))
    kv, kv_t = _kv_proj(x, gain, w_row, w_t, seq=seq)
    return proj, kv, kv_t, rwkv_in


def _rwkv_branch(rwkv_in, p, l, bsz, seq):
    return _rwkv_scan(*rwkv_in, _row(p['rwkv_r_k'][l]), _row(p['rwkv_gn_gain'][l]),
                      _row(p['rwkv_gn_bias'][l]), batch=bsz, seq=seq)


def _nsa_branch(proj, kv, kv_t, bias_c, bias_t, p, l, bsz, seq):
    w1 = jnp.stack([p['cmp_k_w1'][l], p['cmp_v_w1'][l]]).astype(BF16)
    pe = jnp.stack([p['cmp_pe_k'][l].reshape(1, -1), p['cmp_pe_v'][l].reshape(1, -1)])
    w2 = jnp.stack([p['cmp_k_w2'][l], p['cmp_v_w2'][l]]).astype(BF16)
    kvc, kvc_t = _compress(kv, w1, pe, w2, bsz=bsz, seq=seq)
    return _nsa_attention(proj, kv, kv_t, kvc, kvc_t, bias_c, bias_t, bsz=bsz, seq=seq)


def _mem_branch(proj, mem, p, l):
    bsz, m_tok, d = mem.shape
    w_kv = jnp.concatenate([p['mem_w_k'][l], p['mem_w_v'][l]], axis=1).astype(BF16)
    kv_mem = _norm_matmul(mem.reshape(bsz * m_tok, d), _row(p['mem_norm'][l]), w_kv,
                          tm=min(1024, bsz * m_tok), tn=512, name="mem_kv")
    return _mem_attention(proj, kv_mem.reshape(bsz, m_tok, 2 * MEM_DIM),
                          seq=proj.shape[0] // bsz)


def _layer(x, mem, l, bias_c, bias_t, p):
    bsz, seq, d = x.shape
    t = bsz * seq
    row = _row
    x = x.reshape(t, d)

    x = _ffn(x, row(p['ffn1_norm'][l]), p['ffn1_w_gate'][l].astype(BF16),
             p['ffn1_w_up'][l].astype(BF16), p['ffn1_w_down'][l].astype(BF16),
             row(p['final_norm']), final=False)

    proj, kv, kv_t, rwkv_in = _in_proj(x, p, l, seq)
    y_rwkv = _rwkv_branch(rwkv_in, p, l, bsz, seq)
    y_nsa = _nsa_branch(proj, kv, kv_t, bias_c, bias_t, p, l, bsz, seq)
    y_mem = _mem_branch(proj, mem, p, l)

    w_gate = p['w_in'][l][:, -N_BRANCH * d:].astype(BF16)
    x = _merge(x, row(p['mix_norm'][l]), y_rwkv, y_nsa, y_mem, w_gate,
               p['w_br_rwkv'][l].astype(BF16), p['w_br_nsa'][l].astype(BF16),
               p['w_br_mem'][l].astype(BF16), p['w_out'][l].astype(BF16))

    last = l == p['ffn1_norm'].shape[0] - 1
    x = _ffn(x, row(p['ffn2_norm'][l]), p['ffn2_w_gate'][l].astype(BF16),
             p['ffn2_w_up'][l].astype(BF16), p['ffn2_w_down'][l].astype(BF16),
             row(p['final_norm']), final=last)
    return x.reshape(bsz, seq, d)


def kernel(x, mem, ffn1_norm, ffn1_w_gate, ffn1_w_up, ffn1_w_down, mix_norm, w_in, rwkv_mu, rwkv_w0, rwkv_w2, rwkv_a0, rwkv_a2, rwkv_g2, rwkv_k_k, rwkv_k_a, rwkv_r_k, rwkv_gn_gain, rwkv_gn_bias, cmp_pe_k, cmp_k_w1, cmp_k_w2, cmp_pe_v, cmp_v_w1, cmp_v_w2, rel_bias, mem_norm, mem_w_k, mem_w_v, w_br_rwkv, w_br_nsa, w_br_mem, w_out, ffn2_norm, ffn2_w_gate, ffn2_w_up, ffn2_w_down, final_norm):
    p = dict(ffn1_norm=ffn1_norm, ffn1_w_gate=ffn1_w_gate, ffn1_w_up=ffn1_w_up,
             ffn1_w_down=ffn1_w_down, mix_norm=mix_norm, w_in=w_in, rwkv_mu=rwkv_mu,
             rwkv_w0=rwkv_w0, rwkv_w2=rwkv_w2, rwkv_a0=rwkv_a0, rwkv_a2=rwkv_a2, rwkv_g2=rwkv_g2,
             rwkv_k_k=rwkv_k_k, rwkv_k_a=rwkv_k_a, rwkv_r_k=rwkv_r_k, rwkv_gn_gain=rwkv_gn_gain,
             rwkv_gn_bias=rwkv_gn_bias, cmp_pe_k=cmp_pe_k, cmp_k_w1=cmp_k_w1, cmp_k_w2=cmp_k_w2,
             cmp_pe_v=cmp_pe_v, cmp_v_w1=cmp_v_w1, cmp_v_w2=cmp_v_w2, mem_norm=mem_norm,
             mem_w_k=mem_w_k, mem_w_v=mem_w_v, w_br_rwkv=w_br_rwkv, w_br_nsa=w_br_nsa,
             w_br_mem=w_br_mem, w_out=w_out, ffn2_norm=ffn2_norm, ffn2_w_gate=ffn2_w_gate,
             ffn2_w_up=ffn2_w_up, ffn2_w_down=ffn2_w_down, final_norm=final_norm)
    bias_c, bias_t = _bias_tables(rel_bias, x.shape[1])
    for l in range(ffn1_norm.shape[0]):
        x = _layer(x, mem, l, bias_c, bias_t, p)
    return x
```

```python
import functools
import math

import jax
import jax.numpy as jnp
from jax import lax
from jax.experimental import pallas as pl
from jax.experimental.pallas import tpu as pltpu

F32 = jnp.float32
BF16 = jnp.bfloat16
HI = lax.Precision.HIGHEST

D_MODEL = 1024
NORM_EPS = 1e-6
D_FF = 2816
RWKV_HEADS = 8
RWKV_HEAD_DIM = 64
RWKV_DIM = RWKV_HEADS * RWKV_HEAD_DIM
DECAY_LORA = 64
AAA_LORA = 64
GATE_LORA = 128
LORA_DIM = DECAY_LORA + AAA_LORA + GATE_LORA
RWKV_GN_EPS = 64e-5
RWKV_PROJ = 3 * RWKV_DIM + LORA_DIM
NSA_HEADS = 8
NSA_KV_GROUPS = 2
NSA_HPG = NSA_HEADS // NSA_KV_GROUPS
NSA_HEAD_DIM = 64
NSA_DIM = NSA_HEADS * NSA_HEAD_DIM
NSA_KV_DIM = NSA_KV_GROUPS * NSA_HEAD_DIM
CMP_LEN = 32
CMP_STRIDE = 16
CMP_HIDDEN = 256
SEL_BLOCK = 64
SEL_TOP_N = 16
WINDOW = 512
REL_BUCKETS = 32
REL_MAX_DIST = 128
MEM_HEADS = 4
MEM_HEAD_DIM = 128
MEM_DIM = MEM_HEADS * MEM_HEAD_DIM
N_BRANCH = 3

LANES = 128
GATE_PAD = LANES
COL_QNSA = 0
COL_QMEM = COL_QNSA + NSA_DIM
COL_GNSA = COL_QMEM + MEM_DIM
PROJ_COLS = COL_GNSA + NSA_KV_GROUPS * GATE_PAD
assert COL_QMEM % MEM_DIM == 0 and COL_GNSA % GATE_PAD == 0
KV_KINDS = 6
KV_ROW_KINDS = (0, 1, 2, 4)
KV_T_KINDS = (3, 5)
KV_COLS = len(KV_ROW_KINDS) * NSA_KV_GROUPS * LANES
KV_T_ROWS = len(KV_T_KINDS) * NSA_KV_GROUPS * LANES
ROW_K_SEL, ROW_K_WIN = KV_ROW_KINDS.index(2), KV_ROW_KINDS.index(4)
T_V_SEL, T_V_WIN = KV_T_KINDS.index(3), KV_T_KINDS.index(5)
LOG2E = 1.4426950408889634

RWKV_CHUNK = 64
NSA_TQ = 256
MASKED = -1e30
BT_DIAG, BT_PREV, BT_FAR, BT_WIN_EDGE, BT_NONE, BT_COUNT = 0, 1, 2, 3, 4, 5
VMEM_LIMIT = 56 * 1024 * 1024


def _dot(a, b, precision=None):
    return jnp.dot(a, b, preferred_element_type=F32, precision=precision)


def _dot_nt(a, b, precision=None):
    return lax.dot_general(a, b, (((1,), (1,)), ((), ())), preferred_element_type=F32,
                           precision=precision)


def _params(semantics):
    return pltpu.CompilerParams(dimension_semantics=semantics, vmem_limit_bytes=VMEM_LIMIT)


def _rms(x, g):
    return x * lax.rsqrt(jnp.mean(x * x, axis=-1, keepdims=True) + NORM_EPS) * g


def _ffn_body(x_ref, g_ref, wg_ref, wu_ref, wd_ref, fg_ref, o_ref, *, tf, final):
    x = x_ref[...]
    h = _rms(x, g_ref[...]).astype(BF16)
    acc = jnp.zeros(x.shape, F32)
    for j in range(wg_ref.shape[1] // tf):
        cols = slice(j * tf, (j + 1) * tf)
        act = (jax.nn.silu(_dot(h, wg_ref[:, cols])) * _dot(h, wu_ref[:, cols])).astype(BF16)
        acc = acc + _dot(act, wd_ref[cols, :])
    y = x + 0.5 * acc
    if final:
        y = _rms(y, fg_ref[...])
    o_ref[...] = y


def _ffn(x, gain, wg, wu, wd, final_gain, *, final, tm=1024, tf=256):
    t, d = x.shape
    f = wg.shape[1]
    resident = lambda shape: pl.BlockSpec(shape, lambda i: (0, 0), pipeline_mode=pl.Buffered(1))
    return pl.pallas_call(
        functools.partial(_ffn_body, tf=tf, final=final),
        out_shape=jax.ShapeDtypeStruct((t, d), F32),
        grid=(t // tm,),
        in_specs=[
            pl.BlockSpec((tm, d), lambda i: (i, 0)),
            pl.BlockSpec((1, d), lambda i: (0, 0)),
            resident((d, f)),
            resident((d, f)),
            resident((f, d)),
            pl.BlockSpec((1, d), lambda i: (0, 0)),
        ],
        out_specs=pl.BlockSpec((tm, d), lambda i: (i, 0)),
        compiler_params=_params(("parallel",)),
        name="ffn_final" if final else "ffn",
    )(x, gain, wg, wu, wd, final_gain)


def _norm_matmul_body(x_ref, g_ref, w_ref, o_ref, h_ref):
    @pl.when(pl.program_id(1) == 0)
    def _():
        h_ref[...] = _rms(x_ref[...], g_ref[...]).astype(BF16)

    o_ref[...] = _dot(h_ref[...], w_ref[...])


def _norm_matmul(x, gain, w, *, tm, tn, name):
    t, d = x.shape
    n = w.shape[1]
    return pl.pallas_call(
        _norm_matmul_body,
        out_shape=jax.ShapeDtypeStruct((t, n), F32),
        grid=(t // tm, n // tn),
        in_specs=[
            pl.BlockSpec((tm, d), lambda i, j: (i, 0)),
            pl.BlockSpec((1, d), lambda i, j: (0, 0)),
            pl.BlockSpec((d, tn), lambda i, j: (0, j)),
        ],
        out_specs=pl.BlockSpec((tm, tn), lambda i, j: (i, j)),
        scratch_shapes=[pltpu.VMEM((tm, d), BF16)],
        compiler_params=_params(("parallel", "arbitrary")),
        name=name,
    )(x, gain, w)


def _kv_proj_body(x_ref, g_ref, w_ref, wt_ref, o_ref, ot_ref, *, tm, seq):
    dh, tk = NSA_HEAD_DIM, NSA_TQ
    h = _rms(x_ref[...], g_ref[...]).astype(BF16)
    y = _dot(h, w_ref[...])
    row = lax.broadcasted_iota(jnp.int32, (tm, LANES), 0)
    lane = lax.broadcasted_iota(jnp.int32, (tm, LANES), 1)
    pos = (pl.program_id(0) * tm) % seq + row
    block_mark = jnp.where(lane - dh == (pos >> (SEL_BLOCK.bit_length() - 1)), MASKED, 0.0)
    for tile in range(KV_COLS // LANES):
        part = y[:, tile * LANES:(tile + 1) * LANES]
        if tile // NSA_KV_GROUPS == ROW_K_SEL:
            part = part + block_mark
        o_ref[:, tile * LANES:(tile + 1) * LANES] = part.astype(BF16)

    y_t = _dot_nt(wt_ref[...], h)
    row_t = lax.broadcasted_iota(jnp.int32, y_t.shape, 0)
    y_t = (y_t + jnp.where((row_t & (LANES - 1)) == dh, 1.0, 0.0)).astype(BF16)
    for c in range(tm // tk):
        ot_ref[0, c] = y_t[:, c * tk:(c + 1) * tk]


def _kv_proj(x, gain, w, w_t, *, seq, tm=512):
    t, d = x.shape
    tk = NSA_TQ
    per_seq = seq // tm
    return pl.pallas_call(
        functools.partial(_kv_proj_body, tm=tm, seq=seq),
        out_shape=[jax.ShapeDtypeStruct((t, KV_COLS), BF16),
                   jax.ShapeDtypeStruct((t // seq, seq // tk, KV_T_ROWS, tk), BF16)],
        grid=(t // tm,),
        in_specs=[
            pl.BlockSpec((tm, d), lambda i: (i, 0)),
            pl.BlockSpec((1, d), lambda i: (0, 0)),
            pl.BlockSpec((d, KV_COLS), lambda i: (0, 0)),
            pl.BlockSpec((KV_T_ROWS, d), lambda i: (0, 0)),
        ],
        out_specs=[pl.BlockSpec((tm, KV_COLS), lambda i: (i, 0)),
                   pl.BlockSpec((1, tm // tk, KV_T_ROWS, tk),
                                lambda i: (i // per_seq, i % per_seq, 0, 0))],
        compiler_params=_params(("parallel",)),
        name="nsa_kv_proj",
    )(x, gain, w, w_t)


def _in_proj_body(x_ref, xp_ref, g_ref, wr_ref, wo_ref, mu_ref, wl_ref, w0_ref, a0_ref, kk_ref,
                  ka_ref, proj_o, r_o, k_o, v_o, kk_o, b_o, lw_o, g_o, *, tiles_per_seq):
    i = pl.program_id(0)
    gain = g_ref[...]
    h = _rms(x_ref[...], gain).astype(BF16)
    proj_o[...] = _dot(h, wo_ref[...])
    p = _dot(h, wr_ref[...])
    keep = jnp.where(i % tiles_per_seq == 0, 0.0, 1.0)
    prev_last = _dot(_rms(xp_ref[...], gain).astype(BF16), wr_ref[...])[7:8, :] * keep
    rows = lax.broadcasted_iota(jnp.int32, p.shape, 0)
    shifted = jnp.where(rows == 0, prev_last, pltpu.roll(p, 1, 0))
    x = p + (shifted - p) * mu_ref[...]

    r = x[:, 0:RWKV_DIM]
    k = x[:, RWKV_DIM:2 * RWKV_DIM]
    v = x[:, 2 * RWKV_DIM:3 * RWKV_DIM]
    s = x[:, 3 * RWKV_DIM:RWKV_PROJ]
    lane = lax.broadcasted_iota(jnp.int32, s.shape, 1)
    z = jnp.where(lane < DECAY_LORA, jnp.tanh(s),
                  jnp.where(lane < DECAY_LORA + AAA_LORA, s, jax.nn.sigmoid(s)))
    lo = _dot(z.astype(BF16), wl_ref[...])
    a = jax.nn.sigmoid(a0_ref[...] + lo[:, RWKV_DIM:2 * RWKV_DIM])

    kkr = k * kk_ref[...]
    sq = kkr * kkr
    sq_hi = sq.astype(BF16)
    sq_lo = (sq - sq_hi.astype(F32)).astype(BF16)
    shift = RWKV_HEAD_DIM.bit_length() - 1
    same_head = ((lax.broadcasted_iota(jnp.int32, (RWKV_DIM, RWKV_DIM), 0) >> shift)
                 == (lax.broadcasted_iota(jnp.int32, (RWKV_DIM, RWKV_DIM), 1) >> shift)).astype(BF16)
    ssq = _dot(sq_hi, same_head) + _dot(sq_lo, same_head)
    kk = kkr / jnp.maximum(jnp.sqrt(ssq), 1e-12)

    r_o[...] = r
    k_o[...] = k * (1.0 + (a - 1.0) * ka_ref[...])
    v_o[...] = v
    kk_o[...] = kk
    b_o[...] = kk * a
    lw_o[...] = -math.exp(-0.5) * jax.nn.sigmoid(w0_ref[...] + lo[:, 0:RWKV_DIM])
    g_o[...] = lo[:, 2 * RWKV_DIM:3 * RWKV_DIM]


def _in_proj_call(x, seq, gain, w_rwkv, w_other, mu, w_lora, w0, a0, k_k, k_a, *, tm=512):
    t, d = x.shape
    row = lambda i: (i, 0)
    const = lambda i: (0, 0)
    resident = lambda shape: pl.BlockSpec(shape, const, pipeline_mode=pl.Buffered(1))
    vec = pl.BlockSpec((1, RWKV_DIM), const)
    tok = jax.ShapeDtypeStruct((t, RWKV_DIM), F32)
    return pl.pallas_call(
        functools.partial(_in_proj_body, tiles_per_seq=seq // tm),
        out_shape=[jax.ShapeDtypeStruct((t, PROJ_COLS), F32)] + [tok] * 7,
        grid=(t // tm,),
        in_specs=[
            pl.BlockSpec((tm, d), row),
            pl.BlockSpec((8, d), lambda i: (jnp.maximum(i * (tm // 8) - 1, 0), 0)),
            pl.BlockSpec((1, d), const),
            resident((d, RWKV_PROJ)),
            resident((d, PROJ_COLS)),
            pl.BlockSpec((1, RWKV_PROJ), const),
            resident((LORA_DIM, 3 * RWKV_DIM)),
            vec, vec, vec, vec,
        ],
        out_specs=[pl.BlockSpec((tm, PROJ_COLS), row)] + [pl.BlockSpec((tm, RWKV_DIM), row)] * 7,
        compiler_params=_params(("parallel",)),
        name="in_proj",
    )(x, x, gain, w_rwkv, w_other, mu, w_lora, w0, a0, k_k, k_a)


def _rwkv_scan_body(r_ref, k_ref, v_ref, kk_ref, b_ref, lw_ref, g_ref, rk_ref, gg_ref, gb_ref,
                    o_ref, st_ref):
    c_sz, n, nh = RWKV_CHUNK, RWKV_HEAD_DIM, RWKV_HEADS

    @pl.when(pl.program_id(1) == 0)
    def _():
        st_ref[...] = jnp.zeros_like(st_ref)

    ri = lax.broadcasted_iota(jnp.int32, (c_sz, c_sz), 0)
    ci = lax.broadcasted_iota(jnp.int32, (c_sz, c_sz), 1)
    incl = ci <= ri
    eye_b = (ci == ri).astype(BF16)
    row2 = lax.broadcasted_iota(jnp.int32, (c_sz, 2 * c_sz), 0)
    lane2 = lax.broadcasted_iota(jnp.int32, (c_sz, 2 * c_sz), 1)
    right_half = lane2 >= c_sz
    zeros_b = jnp.zeros((c_sz, n), BF16)

    rows = []
    for bb in range(st_ref.shape[0]):
        lw = lw_ref[bb]
        cum = _dot(incl.astype(F32), lw, HI)
        cum_last = cum[c_sz - 1:c_sz, :]
        r, k, v, b = r_ref[bb], k_ref[bb], v_ref[bb], b_ref[bb]
        p_inv = jnp.exp(-cum)
        p_end = jnp.exp(cum_last - cum)
        rows.append(dict(
            left=jnp.concatenate([(-(kk_ref[bb] * jnp.exp(cum - lw))).astype(BF16),
                                  (r * jnp.exp(cum)).astype(BF16)], axis=0),
            bt=(b * p_inv).astype(BF16), kt=(k * p_inv).astype(BF16),
            bh=(b * p_end).astype(BF16), kh=(k * p_end).astype(BF16),
            v=v, v_b=v.astype(BF16), d_p=jnp.exp(cum_last), rk=r * k * rk_ref[...]))

    units = [(bb, h) for bb in range(len(rows)) for h in range(nh)]
    col = lambda name, u: rows[u[0]][name][:, u[1] * n:(u[1] + 1) * n]
    a_all = [_dot_nt(col('left', u), jnp.concatenate([col('bt', u), col('kt', u)], axis=0))
             for u in units]
    key2 = jnp.where(right_half, lane2 - c_sz, lane2)
    w_u = [jnp.where(right_half & (key2 < row2), a[:c_sz], 0.0).astype(BF16) for a in a_all]
    w_y = [jnp.where(key2 <= row2, a[c_sz:], 0.0).astype(BF16) for a in a_all]

    x = [jnp.where(lane2 < row2, a[:c_sz], jnp.where(lane2 == row2 + c_sz, 1.0, 0.0)) for a in a_all]
    for _ in range(6):
        hi = [xu.astype(BF16) for xu in x]
        lo = [(xu - h_.astype(F32)).astype(BF16) for xu, h_ in zip(x, hi)]
        x = [_dot(h_[:, :c_sz], h_) + _dot(h_[:, :c_sz], l_) + _dot(l_[:, :c_sz], h_)
             + jnp.where(right_half, xu, 0.0) for xu, h_, l_ in zip(x, hi, lo)]
    x_b = [xu.astype(BF16) for xu in x]

    s0 = [st_ref[bb, h] for bb, h in units]
    ls0 = [_dot_nt(col('left', u), s0[j].astype(BF16)) for j, u in enumerate(units)]
    rhs = [ls0[j][:c_sz] + _dot(w_u[j], jnp.concatenate([zeros_b, col('v_b', u)], axis=0))
           for j, u in enumerate(units)]
    u_b = [_dot(x_b[j], jnp.concatenate([zeros_b, rhs[j].astype(BF16)], axis=0)).astype(BF16)
           for j in range(len(units))]
    uv = [jnp.concatenate([u_b[j], col('v_b', u)], axis=0) for j, u in enumerate(units)]
    y = [ls0[j][c_sz:] + _dot(w_y[j], uv[j]) for j in range(len(units))]
    uv_t = [_dot_nt(eye_b, uv_j).astype(BF16) for uv_j in uv]
    for j, u in enumerate(units):
        st_ref[u[0], u[1]] = (s0[j] * col('d_p', u)
                              + _dot(uv_t[j], jnp.concatenate([col('bh', u), col('kh', u)], axis=0)))

    for bb in range(len(rows)):
        outs = []
        for h in range(nh):
            sl = slice(h * n, (h + 1) * n)
            yh = y[bb * nh + h]
            mean = jnp.mean(yh, axis=-1, keepdims=True)
            var = jnp.mean(jnp.square(yh - mean), axis=-1, keepdims=True)
            yn = (yh - mean) * lax.rsqrt(var + RWKV_GN_EPS)
            yn = yn * gg_ref[:, sl] + gb_ref[:, sl]
            bonus = jnp.sum(rows[bb]['rk'][:, sl], axis=-1, keepdims=True) * rows[bb]['v'][:, sl]
            outs.append((yn + bonus) * g_ref[bb, :, sl])
        o_ref[bb] = jnp.concatenate(outs, axis=1)


def _rwkv_scan(r, k, v, kk, b, lw, g, r_k, gn_gain, gn_bias, *, batch, seq, nb=4):
    t = r.shape[0]
    nc = seq // RWKV_CHUNK
    tok = pl.BlockSpec((nb, RWKV_CHUNK, RWKV_DIM), lambda bi, c: (bi, c, 0))
    par = pl.BlockSpec((1, RWKV_DIM), lambda bi, c: (0, 0))
    per_batch = lambda a: a.reshape(batch, seq, RWKV_DIM)
    out = pl.pallas_call(
        _rwkv_scan_body,
        out_shape=jax.ShapeDtypeStruct((batch, seq, RWKV_DIM), F32),
        grid=(batch // nb, nc),
        in_specs=[tok] * 7 + [par] * 3,
        out_specs=tok,
        scratch_shapes=[pltpu.VMEM((nb, RWKV_HEADS, RWKV_HEAD_DIM, RWKV_HEAD_DIM), F32)],
        compiler_params=_params(("parallel", "arbitrary")),
        name="rwkv_scan",
    )(*(per_batch(a) for a in (r, k, v, kk, b, lw, g)), r_k, gn_gain, gn_bias)
    return out.reshape(t, RWKV_DIM)


def _compress_body(x_ref, w1s_ref, w1_ref, pe_ref, w2_ref, w2t_ref, o_ref, ot_ref):
    n_rows = x_ref.shape[1]
    both = jnp.zeros((n_rows, 2 * CMP_HIDDEN), F32)
    for l in range(CMP_STRIDE):
        both = both + _dot(x_ref[0, :, l, :], w1s_ref[0, l])
    second_next = pltpu.roll(both[:, CMP_HIDDEN:], n_rows - 1, 0)
    pe = jnp.broadcast_to(pe_ref[0], (8, pe_ref.shape[2])).astype(BF16)
    pe_term = _dot(pe, w1_ref[0])[0:1, :]
    hid = both[:, :CMP_HIDDEN] + second_next + pe_term
    act = jax.nn.gelu(hid).astype(BF16)
    o_ref[0, 0] = _dot(act, w2_ref[0])
    ot_ref[0, 0] = _dot_nt(w2t_ref[0], act)


def _compress(kv, w1, pe, w2, *, bsz, seq):
    g, dh = NSA_KV_GROUPS, NSA_HEAD_DIM
    rows = seq // CMP_STRIDE
    w1r = jnp.pad(w1.reshape(2, CMP_LEN, dh, CMP_HIDDEN), ((0, 0), (0, 0), (0, LANES - dh), (0, 0)))
    w1s = jnp.concatenate([w1r[:, :CMP_STRIDE], w1r[:, CMP_STRIDE:]], axis=3)
    return pl.pallas_call(
        _compress_body,
        out_shape=[jax.ShapeDtypeStruct((2, bsz * g, rows, dh), F32),
                   jax.ShapeDtypeStruct((2, bsz * g, dh, rows), F32)],
        grid=(2, bsz * g),
        in_specs=[
            pl.BlockSpec((1, rows, CMP_STRIDE, LANES), lambda s, i: (i // g, 0, 0, s * g + i % g)),
            pl.BlockSpec((1, CMP_STRIDE, LANES, 2 * CMP_HIDDEN), lambda s, i: (s, 0, 0, 0)),
            pl.BlockSpec((1, CMP_LEN * dh, CMP_HIDDEN), lambda s, i: (s, 0, 0)),
            pl.BlockSpec((1, 1, CMP_LEN * dh), lambda s, i: (s, 0, 0)),
            pl.BlockSpec((1, CMP_HIDDEN, dh), lambda s, i: (s, 0, 0)),
            pl.BlockSpec((1, dh, CMP_HIDDEN), lambda s, i: (s, 0, 0)),
        ],
        out_specs=[pl.BlockSpec((1, 1, rows, dh), lambda s, i: (s, i, 0, 0)),
                   pl.BlockSpec((1, 1, dh, rows), lambda s, i: (s, i, 0, 0))],
        compiler_params=_params(("parallel", "parallel")),
        name="nsa_compress",
    )(kv.reshape(bsz, rows, CMP_STRIDE, KV_COLS), w1s, w1, pe, w2, w2.transpose(0, 2, 1))


def _t5_bucket(dist):
    n = jnp.maximum(dist, 0)
    exact = REL_BUCKETS // 2
    nf = jnp.maximum(n, 1).astype(F32)
    large = exact + (jnp.log(nf / exact) / math.log(REL_MAX_DIST / exact)
                     * (REL_BUCKETS - exact)).astype(jnp.int32)
    large = jnp.minimum(large, REL_BUCKETS - 1)
    return jnp.where(n < exact, n, large)


def _bias_body(tab_ref, bc_ref, bt_ref, *, seq, n_cmp_pad):
    h = pl.program_id(0)
    tq = NSA_TQ

    def lookup(dist):
        bucket = _t5_bucket(dist)
        out = jnp.zeros(dist.shape, F32)
        for bkt in range(REL_BUCKETS):
            out = jnp.where(bucket == bkt, tab_ref[bkt, h] * LOG2E, out)
        return out

    key = lax.broadcasted_iota(jnp.int32, (tq, tq), 0)
    qry = lax.broadcasted_iota(jnp.int32, (tq, tq), 1)
    bt_ref[0, BT_DIAG] = jnp.where(qry >= key, lookup(qry - key), MASKED)
    bt_ref[0, BT_PREV] = lookup(tq + qry - key)
    bt_ref[0, BT_FAR] = lookup(2 * tq + qry - key)
    bt_ref[0, BT_WIN_EDGE] = jnp.where(qry < key, lookup(WINDOW + qry - key), MASKED)
    bt_ref[0, BT_NONE] = jnp.full((tq, tq), MASKED, F32)

    cmp_end = lax.broadcasted_iota(jnp.int32, (n_cmp_pad, tq), 0) * CMP_STRIDE + CMP_LEN - 1
    qry_c = lax.broadcasted_iota(jnp.int32, (n_cmp_pad, tq), 1)

    def cmp_tile(i, carry):
        bc_ref[0, i] = lookup(i * tq + qry_c - cmp_end)
        return carry

    lax.fori_loop(0, seq // tq, cmp_tile, 0)


def _bias_tables(rel_bias, seq):
    g, hpg, tq = NSA_KV_GROUPS, NSA_HPG, NSA_TQ
    n_cmp_pad = seq // CMP_STRIDE
    nq = seq // tq
    return pl.pallas_call(
        functools.partial(_bias_body, seq=seq, n_cmp_pad=n_cmp_pad),
        out_shape=[jax.ShapeDtypeStruct((g, nq, n_cmp_pad, hpg * tq), F32),
                   jax.ShapeDtypeStruct((g, BT_COUNT, tq, hpg * tq), F32)],
        grid=(NSA_HEADS,),
        in_specs=[pl.BlockSpec(memory_space=pltpu.SMEM)],
        out_specs=[pl.BlockSpec((1, nq, n_cmp_pad, tq), lambda h: (h // hpg, 0, 0, h % hpg)),
                   pl.BlockSpec((1, BT_COUNT, tq, tq), lambda h: (h // hpg, 0, 0, h % hpg))],
        compiler_params=_params(("parallel",)),
        name="nsa_bias",
    )(rel_bias)


def _eye(n, dtype):
    return (lax.broadcasted_iota(jnp.int32, (n, n), 0)
            == lax.broadcasted_iota(jnp.int32, (n, n), 1)).astype(dtype)


def _nsa_body(q_ref, kc_ref, vct_ref, ks_ref, vst_ref, kw_ref, vwt_ref, bc_ref, bt_ref, gate_ref,
              o_ref, m_ref, acc_ref):
    tq, hpg, dh = NSA_TQ, NSA_HPG, NSA_HEAD_DIM
    rws = hpg * tq
    n_blk_log2 = SEL_BLOCK.bit_length() - 1
    n_blk = ks_ref.shape[1] // SEL_BLOCK
    n_cmp_pad = kc_ref.shape[2]
    kw = ks_ref.shape[2]
    i = pl.program_id(2)

    def per_head(x):
        return [x[:, hh * tq:(hh + 1) * tq] for hh in range(hpg)]

    xq = (q_ref[0] * (dh ** -0.5 * LOG2E)).astype(BF16)
    eye_d = _eye(dh, BF16)
    q_t = jnp.concatenate([_dot_nt(eye_d, xq[:, hh * dh:(hh + 1) * dh]) for hh in range(hpg)],
                          axis=1).astype(BF16)

    cmp_id = lax.broadcasted_iota(jnp.int32, (n_cmp_pad, rws), 0)
    t_pos = i * tq + (lax.broadcasted_iota(jnp.int32, (n_cmp_pad, rws), 1) & (tq - 1))
    valid = (t_pos - (cmp_id * CMP_STRIDE + CMP_LEN - 1) >= 0) & (cmp_id < n_cmp_pad - 1)
    s = jnp.where(valid, _dot(kc_ref[0, 0].astype(BF16), q_t) + bc_ref[0, 0], MASKED)
    e = jnp.where(valid, jnp.exp2(s - jnp.max(s, axis=0, keepdims=True)), 0.0)
    den = jnp.sum(e, axis=0, keepdims=True)
    p_c = e / jnp.where(den > 0.0, den, 1.0)
    o_cmp = _dot(vct_ref[0, 0].astype(BF16), p_c.astype(BF16))

    p_heads = per_head(p_c)
    p_sum = p_heads[0]
    for ph in p_heads[1:]:
        p_sum = p_sum + ph
    blk_o = lax.broadcasted_iota(jnp.int32, (n_blk, n_cmp_pad), 0)
    cmp_o = lax.broadcasted_iota(jnp.int32, (n_blk, n_cmp_pad), 1)
    overlap_t = ((cmp_o * CMP_STRIDE <= blk_o * SEL_BLOCK + SEL_BLOCK - 1)
                 & (cmp_o * CMP_STRIDE + CMP_LEN - 1 >= blk_o * SEL_BLOCK)).astype(F32)
    imp = _dot(overlap_t, p_sum, HI)
    jj = lax.broadcasted_iota(jnp.int32, (n_blk, tq), 0)
    cur = (i * tq + lax.broadcasted_iota(jnp.int32, (n_blk, tq), 1)) >> n_blk_log2
    forced = (jj == 0) | (jj == cur) | (jj == cur - 1)
    imp = jnp.where(jj > cur, -1e6, jnp.where(forced, 1e6, imp))
    rank = jnp.zeros((n_blk, tq), jnp.int32)
    for a in range(n_blk):
        row = imp[a:a + 1, :]
        beats = (row > imp) | ((row == imp) & (a < jj))
        rank = rank + beats.astype(jnp.int32)
    not_sel = jnp.where(rank < SEL_TOP_N, 0.0, 1.0).astype(BF16)

    q_aug = jnp.concatenate([q_t, jnp.concatenate([not_sel] * hpg, axis=1),
                             jnp.zeros((kw - dh - n_blk, rws), BF16)], axis=0)
    q_pad = jnp.concatenate([q_t, jnp.zeros((kw - dh, rws), BF16)], axis=0)


    n_win = WINDOW // tq
    tiles, scores = [], []
    for delta in range(n_win + 1):
        entry = {0: BT_DIAG, 1: BT_PREV, n_win: BT_WIN_EDGE}.get(delta, BT_FAR)
        if delta > 0:
            entry = jnp.where(i - delta >= 0, entry, BT_NONE)
        tiles.append(jnp.maximum(i - delta, 0))
        off = pl.multiple_of(tiles[-1] * tq, tq)
        scores.append(_dot(kw_ref[0, pl.ds(off, tq), :], q_pad) + bt_ref[0, entry])
    m_all = scores[0]
    for sc in scores[1:]:
        m_all = jnp.maximum(m_all, sc)
    m_w = jnp.max(m_all, axis=0, keepdims=True)
    acc_w = jnp.zeros((vwt_ref.shape[2], rws), F32)
    for kt, sc in zip(tiles, scores):
        acc_w = acc_w + _dot(vwt_ref[0, kt], jnp.exp2(sc - m_w).astype(BF16))
    o_win = acc_w[0:dh] / acc_w[dh:dh + 1]

    g_t = _dot_nt(_eye(GATE_PAD, F32), gate_ref[0], HI)
    sig = jax.nn.sigmoid(g_t[0:4 * hpg])
    gate = [jnp.concatenate([sig[br * hpg + hh:br * hpg + hh + 1] for hh in range(hpg)], axis=1)
            for br in range(3)]
    m_ref[...] = jnp.full(m_ref.shape, MASKED, F32)
    acc_ref[...] = jnp.zeros(acc_ref.shape, F32)

    def sel_update(tiles):
        scores = []
        for kt in tiles:
            off = pl.multiple_of(kt * tq, tq)
            scores.append(_dot(ks_ref[0, pl.ds(off, tq), :], q_aug)
                          + bt_ref[0, jnp.minimum(i - kt, BT_FAR)])
        s_max = scores[0]
        for sc in scores[1:]:
            s_max = jnp.maximum(s_max, sc)
        m_prev = m_ref[...]
        m_new = jnp.maximum(m_prev, jnp.max(s_max, axis=0, keepdims=True))
        acc = jnp.exp2(m_prev - m_new) * acc_ref[...]
        for kt, sc in zip(tiles, scores):
            acc = acc + _dot(vst_ref[0, kt], jnp.exp2(sc - m_new).astype(BF16))
        m_ref[...] = m_new
        acc_ref[...] = acc

    def sel_pair(j, carry):
        sel_update([2 * j, 2 * j + 1])
        return carry

    lax.fori_loop(0, (i + 1) // 2, sel_pair, 0)

    @pl.when(i % 2 == 0)
    def _():
        sel_update([i])

    o_sel = acc_ref[0:dh, :] / acc_ref[dh:dh + 1, :]

    y_t = (gate[0] * o_cmp + gate[1] * o_sel + gate[2] * o_win).astype(BF16)
    eye_q = _eye(tq, BF16)
    o_ref[0] = jnp.concatenate([_dot_nt(eye_q, yh) for yh in per_head(y_t)],
                               axis=1).astype(BF16)


def _nsa_attention(proj, kv, kv_t, kvc, kvc_t, bias_c, bias_t, *, bsz, seq):
    g, tq, hpg, dh = NSA_KV_GROUPS, NSA_TQ, NSA_HPG, NSA_HEAD_DIM
    n_rows = kvc.shape[2]
    nq = seq // tq
    group_w = hpg * dh
    kv_spec = lambda pos: pl.BlockSpec((1, seq, LANES), lambda b, gi, i: (b, 0, pos * g + gi))
    kvt_spec = lambda pos: pl.BlockSpec((1, nq, LANES, tq), lambda b, gi, i: (b, 0, pos * g + gi, 0))
    kv3 = kv.reshape(bsz, seq, KV_COLS)
    proj3 = proj.reshape(bsz, seq, PROJ_COLS)
    out = pl.pallas_call(
        _nsa_body,
        out_shape=jax.ShapeDtypeStruct((bsz, seq, NSA_DIM), BF16),
        grid=(bsz, g, nq),
        in_specs=[
            pl.BlockSpec((1, tq, group_w), lambda b, gi, i: (b, i, COL_QNSA // group_w + gi)),
            pl.BlockSpec((1, 1, n_rows, dh), lambda b, gi, i: (0, b * g + gi, 0, 0)),
            pl.BlockSpec((1, 1, dh, n_rows), lambda b, gi, i: (1, b * g + gi, 0, 0)),
            kv_spec(ROW_K_SEL), kvt_spec(T_V_SEL), kv_spec(ROW_K_WIN), kvt_spec(T_V_WIN),
            pl.BlockSpec((1, 1, n_rows, hpg * tq), lambda b, gi, i: (gi, i, 0, 0)),
            pl.BlockSpec((1, BT_COUNT, tq, hpg * tq), lambda b, gi, i: (gi, 0, 0, 0)),
            pl.BlockSpec((1, tq, GATE_PAD), lambda b, gi, i: (b, i, COL_GNSA // GATE_PAD + gi)),
        ],
        out_specs=pl.BlockSpec((1, tq, group_w), lambda b, gi, i: (b, i, gi)),
        scratch_shapes=[pltpu.VMEM((1, hpg * tq), F32),
                        pltpu.VMEM((LANES, hpg * tq), F32)],
        compiler_params=_params(("parallel", "parallel", "arbitrary")),
        name="nsa_attention",
    )(proj3, kvc, kvc_t, kv3, kv_t, kv3, kv_t, bias_c, bias_t, proj3)
    return out.reshape(bsz * seq, NSA_DIM)


def _mem_body(q_ref, kv_ref, o_ref):
    outs = []
    for h in range(MEM_HEADS):
        sl = slice(h * MEM_HEAD_DIM, (h + 1) * MEM_HEAD_DIM)
        qh = (q_ref[:, sl] * (MEM_HEAD_DIM ** -0.5)).astype(BF16)
        kh = kv_ref[0, :, sl].astype(BF16)
        vh = kv_ref[0, :, MEM_DIM + h * MEM_HEAD_DIM:MEM_DIM + (h + 1) * MEM_HEAD_DIM].astype(BF16)
        s = _dot_nt(qh, kh)
        e = jnp.exp(s - jnp.max(s, axis=-1, keepdims=True))
        p = e / jnp.sum(e, axis=-1, keepdims=True)
        outs.append(_dot(p.astype(BF16), vh))
    o_ref[...] = jnp.concatenate(outs, axis=1)


def _mem_attention(proj, kv, *, seq, tq=512):
    t = proj.shape[0]
    bsz, m, _ = kv.shape
    per_seq = seq // tq
    return pl.pallas_call(
        _mem_body,
        out_shape=jax.ShapeDtypeStruct((t, MEM_DIM), F32),
        grid=(t // tq,),
        in_specs=[
            pl.BlockSpec((tq, MEM_DIM), lambda i: (i, COL_QMEM // MEM_DIM)),
            pl.BlockSpec((1, m, 2 * MEM_DIM), lambda i: (i // per_seq, 0, 0)),
        ],
        out_specs=pl.BlockSpec((tq, MEM_DIM), lambda i: (i, 0)),
        compiler_params=_params(("parallel",)),
        name="mem_attention",
    )(proj, kv)


def _merge_body(x_ref, gain_ref, yr_ref, yn_ref, ym_ref, wg_ref, wr_ref, wn_ref, wm_ref, wo_ref,
                o_ref):
    d = x_ref.shape[1]
    x = x_ref[...]
    gates = jax.nn.sigmoid(_dot(_rms(x, gain_ref[...]).astype(BF16), wg_ref[...]))
    merged = (gates[:, 0:d] * _dot(yr_ref[...].astype(BF16), wr_ref[...])
              + gates[:, d:2 * d] * _dot(yn_ref[...], wn_ref[...])
              + gates[:, 2 * d:3 * d] * _dot(ym_ref[...].astype(BF16), wm_ref[...]))
    o_ref[...] = x + _dot(merged.astype(BF16), wo_ref[...])


def _merge(x, gain, y_rwkv, y_nsa, y_mem, w_g, w_r, w_n, w_m, w_o, *, tm=512):
    t, d = x.shape
    row = lambda i: (i, 0)
    const = lambda i: (0, 0)
    return pl.pallas_call(
        _merge_body,
        out_shape=jax.ShapeDtypeStruct((t, d), F32),
        grid=(t // tm,),
        in_specs=[
            pl.BlockSpec((tm, d), row),
            pl.BlockSpec((1, d), const),
            pl.BlockSpec((tm, RWKV_DIM), row),
            pl.BlockSpec((tm, NSA_DIM), row),
            pl.BlockSpec((tm, MEM_DIM), row),
            pl.BlockSpec((d, N_BRANCH * d), const),
            pl.BlockSpec((RWKV_DIM, d), const),
            pl.BlockSpec((NSA_DIM, d), const),
            pl.BlockSpec((MEM_DIM, d), const),
            pl.BlockSpec((d, d), const),
        ],
        out_specs=pl.BlockSpec((tm, d), row),
        compiler_params=_params(("parallel",)),
        name="merge",
    )(x, gain, y_rwkv, y_nsa, y_mem, w_g, w_r, w_n, w_m, w_o)


def _row(a):
    return a.reshape(1, -1)


def _in_proj(x, p, l, seq):
    d = x.shape[1]
    g, hpg, dh = NSA_KV_GROUPS, NSA_HPG, NSA_HEAD_DIM
    w_in = p['w_in'][l]
    o = 0
    parts = {}
    for name, size in (('rwkv', RWKV_PROJ), ('q', NSA_DIM), ('kv', KV_KINDS * NSA_KV_DIM),
                       ('g_nsa', 3 * NSA_HEADS), ('q_mem', MEM_DIM)):
        parts[name] = w_in[:, o:o + size]
        o += size
    gates = parts['g_nsa'].reshape(d, 3, g, hpg).transpose(0, 2, 1, 3).reshape(d, g, 3 * hpg)
    gates = jnp.pad(gates, ((0, 0), (0, 0), (0, GATE_PAD - 3 * hpg))).reshape(d, g * GATE_PAD)
    w_other = jnp.concatenate([parts['q'], parts['q_mem'], gates], axis=1).astype(BF16)
    w_kv = jnp.pad(parts['kv'].reshape(d, KV_KINDS, g, dh), ((0, 0), (0, 0), (0, 0), (0, LANES - dh)))
    w_row = w_kv[:, KV_ROW_KINDS, :, :].reshape(d, KV_COLS).astype(BF16)
    w_t = w_kv[:, KV_T_KINDS, :, :].reshape(d, KV_T_ROWS).T.astype(BF16)
    w_lora = jnp.zeros((LORA_DIM, 3 * RWKV_DIM), F32)
    w_lora = w_lora.at[0:DECAY_LORA, 0:RWKV_DIM].set(p['rwkv_w2'][l])
    w_lora = w_lora.at[DECAY_LORA:DECAY_LORA + AAA_LORA, RWKV_DIM:2 * RWKV_DIM].set(p['rwkv_a2'][l])
    w_lora = w_lora.at[DECAY_LORA + AAA_LORA:, 2 * RWKV_DIM:].set(p['rwkv_g2'][l])
    gain = _row(p['mix_norm'][l])
    proj, *rwkv_in = _in_proj_call(
        x, seq, gain, parts['rwkv'].astype(BF16), w_other, _row(p['rwkv_mu'][l]),
        w_lora.astype(BF16), _row(p['rwkv_w0'][l]), _row(p['rwkv_a0'][l]),
        _row(p['rwkv_k_k'][l]), _row(p['rwkv_k_a'][l]))
    kv, kv_t = _kv_proj(x, gain, w_row, w_t, seq=seq)
    return proj, kv, kv_t, rwkv_in


def _rwkv_branch(rwkv_in, p, l, bsz, seq):
    return _rwkv_scan(*rwkv_in, _row(p['rwkv_r_k'][l]), _row(p['rwkv_gn_gain'][l]),
                      _row(p['rwkv_gn_bias'][l]), batch=bsz, seq=seq)


def _nsa_branch(proj, kv, kv_t, bias_c, bias_t, p, l, bsz, seq):
    w1 = jnp.stack([p['cmp_k_w1'][l], p['cmp_v_w1'][l]]).astype(BF16)
    pe = jnp.stack([p['cmp_pe_k'][l].reshape(1, -1), p['cmp_pe_v'][l].reshape(1, -1)])
    w2 = jnp.stack([p['cmp_k_w2'][l], p['cmp_v_w2'][l]]).astype(BF16)
    kvc, kvc_t = _compress(kv, w1, pe, w2, bsz=bsz, seq=seq)
    return _nsa_attention(proj, kv, kv_t, kvc, kvc_t, bias_c, bias_t, bsz=bsz, seq=seq)


def _mem_branch(proj, mem, p, l):
    bsz, m_tok, d = mem.shape
    w_kv = jnp.concatenate([p['mem_w_k'][l], p['mem_w_v'][l]], axis=1).astype(BF16)
    kv_mem = _norm_matmul(mem.reshape(bsz * m_tok, d), _row(p['mem_norm'][l]), w_kv,
                          tm=min(1024, bsz * m_tok), tn=512, name="mem_kv")
    return _mem_attention(proj, kv_mem.reshape(bsz, m_tok, 2 * MEM_DIM),
                          seq=proj.shape[0] // bsz)


def _layer(x, mem, l, bias_c, bias_t, p):
    bsz, seq, d = x.shape
    t = bsz * seq
    row = _row
    x = x.reshape(t, d)

    x = _ffn(x, row(p['ffn1_norm'][l]), p['ffn1_w_gate'][l].astype(BF16),
             p['ffn1_w_up'][l].astype(BF16), p['ffn1_w_down'][l].astype(BF16),
             row(p['final_norm']), final=False)

    proj, kv, kv_t, rwkv_in = _in_proj(x, p, l, seq)
    y_rwkv = _rwkv_branch(rwkv_in, p, l, bsz, seq)
    y_nsa = _nsa_branch(proj, kv, kv_t, bias_c, bias_t, p, l, bsz, seq)
    y_mem = _mem_branch(proj, mem, p, l)

    w_gate = p['w_in'][l][:, -N_BRANCH * d:].astype(BF16)
    x = _merge(x, row(p['mix_norm'][l]), y_rwkv, y_nsa, y_mem, w_gate,
               p['w_br_rwkv'][l].astype(BF16), p['w_br_nsa'][l].astype(BF16),
               p['w_br_mem'][l].astype(BF16), p['w_out'][l].astype(BF16))

    last = l == p['ffn1_norm'].shape[0] - 1
    x = _ffn(x, row(p['ffn2_norm'][l]), p['ffn2_w_gate'][l].astype(BF16),
             p['ffn2_w_up'][l].astype(BF16), p['ffn2_w_down'][l].astype(BF16),
             row(p['final_norm']), final=last)
    return x.reshape(bsz, seq, d)


def kernel(x, mem, ffn1_norm, ffn1_w_gate, ffn1_w_up, ffn1_w_down, mix_norm, w_in, rwkv_mu, rwkv_w0, rwkv_w2, rwkv_a0, rwkv_a2, rwkv_g2, rwkv_k_k, rwkv_k_a, rwkv_r_k, rwkv_gn_gain, rwkv_gn_bias, cmp_pe_k, cmp_k_w1, cmp_k_w2, cmp_pe_v, cmp_v_w1, cmp_v_w2, rel_bias, mem_norm, mem_w_k, mem_w_v, w_br_rwkv, w_br_nsa, w_br_mem, w_out, ffn2_norm, ffn2_w_gate, ffn2_w_up, ffn2_w_down, final_norm):
    p = dict(ffn1_norm=ffn1_norm, ffn1_w_gate=ffn1_w_gate, ffn1_w_up=ffn1_w_up,
             ffn1_w_down=ffn1_w_down, mix_norm=mix_norm, w_in=w_in, rwkv_mu=rwkv_mu,
             rwkv_w0=rwkv_w0, rwkv_w2=rwkv_w2, rwkv_a0=rwkv_a0, rwkv_a2=rwkv_a2, rwkv_g2=rwkv_g2,
             rwkv_k_k=rwkv_k_k, rwkv_k_a=rwkv_k_a, rwkv_r_k=rwkv_r_k, rwkv_gn_gain=rwkv_gn_gain,
             rwkv_gn_bias=rwkv_gn_bias, cmp_pe_k=cmp_pe_k, cmp_k_w1=cmp_k_w1, cmp_k_w2=cmp_k_w2,
             cmp_pe_v=cmp_pe_v, cmp_v_w1=cmp_v_w1, cmp_v_w2=cmp_v_w2, mem_norm=mem_norm,
             mem_w_k=mem_w_k, mem_w_v=mem_w_v, w_br_rwkv=w_br_rwkv, w_br_nsa=w_br_nsa,
             w_br_mem=w_br_mem, w_out=w_out, ffn2_norm=ffn2_norm, ffn2_w_gate=ffn2_w_gate,
             ffn2_w_up=ffn2_w_up, ffn2_w_down=ffn2_w_down, final_norm=final_norm)
    bias_c, bias_t = _bias_tables(rel_bias, x.shape[1])
    for l in range(ffn1_norm.shape[0]):
        x = _layer(x, mem, l, bias_c, bias_t, p)
    return x
```

```python
import functools
import math

import jax
import jax.numpy as jnp
from jax import lax
from jax.experimental import pallas as pl
from jax.experimental.pallas import tpu as pltpu

F32 = jnp.float32
BF16 = jnp.bfloat16
HI = lax.Precision.HIGHEST

D_MODEL = 1024
NORM_EPS = 1e-6
D_FF = 2816
RWKV_HEADS = 8
RWKV_HEAD_DIM = 64
RWKV_DIM = RWKV_HEADS * RWKV_HEAD_DIM
DECAY_LORA = 64
AAA_LORA = 64
GATE_LORA = 128
LORA_DIM = DECAY_LORA + AAA_LORA + GATE_LORA
RWKV_GN_EPS = 64e-5
RWKV_PROJ = 3 * RWKV_DIM + LORA_DIM
NSA_HEADS = 8
NSA_KV_GROUPS = 2
NSA_HPG = NSA_HEADS // NSA_KV_GROUPS
NSA_HEAD_DIM = 64
NSA_DIM = NSA_HEADS * NSA_HEAD_DIM
NSA_KV_DIM = NSA_KV_GROUPS * NSA_HEAD_DIM
CMP_LEN = 32
CMP_STRIDE = 16
CMP_HIDDEN = 256
SEL_BLOCK = 64
SEL_TOP_N = 16
WINDOW = 512
REL_BUCKETS = 32
REL_MAX_DIST = 128
MEM_HEADS = 4
MEM_HEAD_DIM = 128
MEM_DIM = MEM_HEADS * MEM_HEAD_DIM
N_BRANCH = 3

LANES = 128
GATE_PAD = LANES
COL_QNSA = 0
COL_QMEM = COL_QNSA + NSA_DIM
COL_GNSA = COL_QMEM + MEM_DIM
PROJ_COLS = COL_GNSA + NSA_KV_GROUPS * GATE_PAD
assert COL_QMEM % MEM_DIM == 0 and COL_GNSA % GATE_PAD == 0
KV_KINDS = 6
KV_ROW_KINDS = (0, 1, 2, 4)
KV_T_KINDS = (3, 5)
KV_COLS = len(KV_ROW_KINDS) * NSA_KV_GROUPS * LANES
KV_T_ROWS = len(KV_T_KINDS) * NSA_KV_GROUPS * LANES
ROW_K_SEL, ROW_K_WIN = KV_ROW_KINDS.index(2), KV_ROW_KINDS.index(4)
T_V_SEL, T_V_WIN = KV_T_KINDS.index(3), KV_T_KINDS.index(5)
LOG2E = 1.4426950408889634

RWKV_CHUNK = 64
RWKV_INV_BLOCK = 16
NSA_TQ = 256
MASKED = -1e30
BT_DIAG, BT_PREV, BT_FAR, BT_WIN_EDGE, BT_NONE, BT_COUNT = 0, 1, 2, 3, 4, 5
VMEM_LIMIT = 56 * 1024 * 1024


def _dot(a, b, precision=None):
    return jnp.dot(a, b, preferred_element_type=F32, precision=precision)


def _dot_nt(a, b, precision=None):
    return lax.dot_general(a, b, (((1,), (1,)), ((), ())), preferred_element_type=F32,
                           precision=precision)


def _params(semantics):
    return pltpu.CompilerParams(dimension_semantics=semantics, vmem_limit_bytes=VMEM_LIMIT)


def _rms(x, g):
    return x * lax.rsqrt(jnp.mean(x * x, axis=-1, keepdims=True) + NORM_EPS) * g


def _ffn_body(x_ref, g_ref, wg_ref, wu_ref, wd_ref, fg_ref, o_ref, *, tf, final):
    x = x_ref[...]
    h = _rms(x, g_ref[...]).astype(BF16)
    acc = jnp.zeros(x.shape, F32)
    for j in range(wg_ref.shape[1] // tf):
        cols = slice(j * tf, (j + 1) * tf)
        act = (jax.nn.silu(_dot(h, wg_ref[:, cols])) * _dot(h, wu_ref[:, cols])).astype(BF16)
        acc = acc + _dot(act, wd_ref[cols, :])
    y = x + 0.5 * acc
    if final:
        y = _rms(y, fg_ref[...])
    o_ref[...] = y


def _ffn(x, gain, wg, wu, wd, final_gain, *, final, tm=1024, tf=256):
    t, d = x.shape
    f = wg.shape[1]
    resident = lambda shape: pl.BlockSpec(shape, lambda i: (0, 0), pipeline_mode=pl.Buffered(1))
    return pl.pallas_call(
        functools.partial(_ffn_body, tf=tf, final=final),
        out_shape=jax.ShapeDtypeStruct((t, d), F32),
        grid=(t // tm,),
        in_specs=[
            pl.BlockSpec((tm, d), lambda i: (i, 0)),
            pl.BlockSpec((1, d), lambda i: (0, 0)),
            resident((d, f)),
            resident((d, f)),
            resident((f, d)),
            pl.BlockSpec((1, d), lambda i: (0, 0)),
        ],
        out_specs=pl.BlockSpec((tm, d), lambda i: (i, 0)),
        compiler_params=_params(("parallel",)),
        name="ffn_final" if final else "ffn",
    )(x, gain, wg, wu, wd, final_gain)


def _norm_matmul_body(x_ref, g_ref, w_ref, o_ref, h_ref):
    @pl.when(pl.program_id(1) == 0)
    def _():
        h_ref[...] = _rms(x_ref[...], g_ref[...]).astype(BF16)

    o_ref[...] = _dot(h_ref[...], w_ref[...])


def _norm_matmul(x, gain, w, *, tm, tn, name):
    t, d = x.shape
    n = w.shape[1]
    return pl.pallas_call(
        _norm_matmul_body,
        out_shape=jax.ShapeDtypeStruct((t, n), F32),
        grid=(t // tm, n // tn),
        in_specs=[
            pl.BlockSpec((tm, d), lambda i, j: (i, 0)),
            pl.BlockSpec((1, d), lambda i, j: (0, 0)),
            pl.BlockSpec((d, tn), lambda i, j: (0, j)),
        ],
        out_specs=pl.BlockSpec((tm, tn), lambda i, j: (i, j)),
        scratch_shapes=[pltpu.VMEM((tm, d), BF16)],
        compiler_params=_params(("parallel", "arbitrary")),
        name=name,
    )(x, gain, w)


def _kv_proj_body(x_ref, g_ref, w_ref, wt_ref, o_ref, ot_ref, *, tm, seq):
    dh, tk = NSA_HEAD_DIM, NSA_TQ
    h = _rms(x_ref[...], g_ref[...]).astype(BF16)
    y = _dot(h, w_ref[...])
    row = lax.broadcasted_iota(jnp.int32, (tm, LANES), 0)
    lane = lax.broadcasted_iota(jnp.int32, (tm, LANES), 1)
    pos = (pl.program_id(0) * tm) % seq + row
    block_mark = jnp.where(lane - dh == (pos >> (SEL_BLOCK.bit_length() - 1)), MASKED, 0.0)
    for tile in range(KV_COLS // LANES):
        part = y[:, tile * LANES:(tile + 1) * LANES]
        if tile // NSA_KV_GROUPS == ROW_K_SEL:
            part = part + block_mark
        o_ref[:, tile * LANES:(tile + 1) * LANES] = part.astype(BF16)

    y_t = _dot_nt(wt_ref[...], h)
    row_t = lax.broadcasted_iota(jnp.int32, y_t.shape, 0)
    y_t = (y_t + jnp.where((row_t & (LANES - 1)) == dh, 1.0, 0.0)).astype(BF16)
    for c in range(tm // tk):
        ot_ref[0, c] = y_t[:, c * tk:(c + 1) * tk]


def _kv_proj(x, gain, w, w_t, *, seq, tm=512):
    t, d = x.shape
    tk = NSA_TQ
    per_seq = seq // tm
    return pl.pallas_call(
        functools.partial(_kv_proj_body, tm=tm, seq=seq),
        out_shape=[jax.ShapeDtypeStruct((t, KV_COLS), BF16),
                   jax.ShapeDtypeStruct((t // seq, seq // tk, KV_T_ROWS, tk), BF16)],
        grid=(t // tm,),
        in_specs=[
            pl.BlockSpec((tm, d), lambda i: (i, 0)),
            pl.BlockSpec((1, d), lambda i: (0, 0)),
            pl.BlockSpec((d, KV_COLS), lambda i: (0, 0)),
            pl.BlockSpec((KV_T_ROWS, d), lambda i: (0, 0)),
        ],
        out_specs=[pl.BlockSpec((tm, KV_COLS), lambda i: (i, 0)),
                   pl.BlockSpec((1, tm // tk, KV_T_ROWS, tk),
                                lambda i: (i // per_seq, i % per_seq, 0, 0))],
        compiler_params=_params(("parallel",)),
        name="nsa_kv_proj",
    )(x, gain, w, w_t)


def _in_proj_body(x_ref, xp_ref, g_ref, wr_ref, wo_ref, mu_ref, wl_ref, w0_ref, a0_ref, kk_ref,
                  ka_ref, proj_o, r_o, k_o, v_o, kk_o, b_o, lw_o, g_o, *, tiles_per_seq):
    i = pl.program_id(0)
    gain = g_ref[...]
    h = _rms(x_ref[...], gain).astype(BF16)
    proj_o[...] = _dot(h, wo_ref[...])
    n_up = xp_ref.shape[0]
    h_up = _rms(xp_ref[...], gain).astype(BF16)
    p_all = _dot(jnp.concatenate([h_up, h], axis=0), wr_ref[...])
    p = p_all[n_up:]
    keep = jnp.where(i % tiles_per_seq == 0, 0.0, 1.0)
    prev_last = p_all[n_up - 1:n_up, :] * keep
    rows = lax.broadcasted_iota(jnp.int32, p.shape, 0)
    shifted = jnp.where(rows == 0, prev_last, pltpu.roll(p, 1, 0))
    x = p + (shifted - p) * mu_ref[...]

    r = x[:, 0:RWKV_DIM]
    k = x[:, RWKV_DIM:2 * RWKV_DIM]
    v = x[:, 2 * RWKV_DIM:3 * RWKV_DIM]
    s = x[:, 3 * RWKV_DIM:RWKV_PROJ]
    lane = lax.broadcasted_iota(jnp.int32, s.shape, 1)
    z = jnp.where(lane < DECAY_LORA, jnp.tanh(s),
                  jnp.where(lane < DECAY_LORA + AAA_LORA, s, jax.nn.sigmoid(s)))
    lo = _dot(z.astype(BF16), wl_ref[...])
    a = jax.nn.sigmoid(a0_ref[...] + lo[:, RWKV_DIM:2 * RWKV_DIM])

    kkr = k * kk_ref[...]
    sq = kkr * kkr
    sq_hi = sq.astype(BF16)
    sq_lo = (sq - sq_hi.astype(F32)).astype(BF16)
    shift = RWKV_HEAD_DIM.bit_length() - 1
    same_head = ((lax.broadcasted_iota(jnp.int32, (RWKV_DIM, RWKV_DIM), 0) >> shift)
                 == (lax.broadcasted_iota(jnp.int32, (RWKV_DIM, RWKV_DIM), 1) >> shift)).astype(BF16)
    ssq = _dot(sq_hi, same_head) + _dot(sq_lo, same_head)
    kk = kkr / jnp.maximum(jnp.sqrt(ssq), 1e-12)

    r_o[...] = r
    k_o[...] = k * (1.0 + (a - 1.0) * ka_ref[...])
    v_o[...] = v
    kk_o[...] = kk
    b_o[...] = kk * a
    lw_o[...] = -math.exp(-0.5) * jax.nn.sigmoid(w0_ref[...] + lo[:, 0:RWKV_DIM])
    g_o[...] = lo[:, 2 * RWKV_DIM:3 * RWKV_DIM]


def _in_proj_call(x, seq, gain, w_rwkv, w_other, mu, w_lora, w0, a0, k_k, k_a, *, tm=512):
    t, d = x.shape
    row = lambda i: (i, 0)
    const = lambda i: (0, 0)
    resident = lambda shape: pl.BlockSpec(shape, const, pipeline_mode=pl.Buffered(1))
    vec = pl.BlockSpec((1, RWKV_DIM), const)
    tok = jax.ShapeDtypeStruct((t, RWKV_DIM), F32)
    return pl.pallas_call(
        functools.partial(_in_proj_body, tiles_per_seq=seq // tm),
        out_shape=[jax.ShapeDtypeStruct((t, PROJ_COLS), F32)] + [tok] * 7,
        grid=(t // tm,),
        in_specs=[
            pl.BlockSpec((tm, d), row),
            pl.BlockSpec((16, d), lambda i: (jnp.maximum(i * (tm // 16) - 1, 0), 0)),
            pl.BlockSpec((1, d), const),
            resident((d, RWKV_PROJ)),
            resident((d, PROJ_COLS)),
            pl.BlockSpec((1, RWKV_PROJ), const),
            resident((LORA_DIM, 3 * RWKV_DIM)),
            vec, vec, vec, vec,
        ],
        out_specs=[pl.BlockSpec((tm, PROJ_COLS), row)] + [pl.BlockSpec((tm, RWKV_DIM), row)] * 7,
        compiler_params=_params(("parallel",)),
        name="in_proj",
    )(x, x, gain, w_rwkv, w_other, mu, w_lora, w0, a0, k_k, k_a)


def _rwkv_scan_body(r_ref, k_ref, v_ref, kk_ref, b_ref, lw_ref, g_ref, rk_ref, gg_ref, gb_ref,
                    o_ref, st_ref):
    c_sz, n, nh = RWKV_CHUNK, RWKV_HEAD_DIM, RWKV_HEADS

    @pl.when(pl.program_id(1) == 0)
    def _():
        st_ref[...] = jnp.zeros_like(st_ref)

    ri = lax.broadcasted_iota(jnp.int32, (c_sz, c_sz), 0)
    ci = lax.broadcasted_iota(jnp.int32, (c_sz, c_sz), 1)
    incl = ci <= ri
    eye_b = (ci == ri).astype(BF16)
    row2 = lax.broadcasted_iota(jnp.int32, (c_sz, 2 * c_sz), 0)
    lane2 = lax.broadcasted_iota(jnp.int32, (c_sz, 2 * c_sz), 1)
    right_half = lane2 >= c_sz
    zeros_b = jnp.zeros((c_sz, n), BF16)
    inv_shift = RWKV_INV_BLOCK.bit_length() - 1

    rows = []
    for bb in range(st_ref.shape[0]):
        lw = lw_ref[bb]
        cum = _dot(incl.astype(F32), lw, HI)
        cum_last = cum[c_sz - 1:c_sz, :]
        r, k, v, b = r_ref[bb], k_ref[bb], v_ref[bb], b_ref[bb]
        p_inv = jnp.exp(-cum)
        p_end = jnp.exp(cum_last - cum)
        rows.append(dict(
            left=jnp.concatenate([(-(kk_ref[bb] * jnp.exp(cum - lw))).astype(BF16),
                                  (r * jnp.exp(cum)).astype(BF16)], axis=0),
            bt=(b * p_inv).astype(BF16), kt=(k * p_inv).astype(BF16),
            bh=(b * p_end).astype(BF16), kh=(k * p_end).astype(BF16),
            v=v, v_b=v.astype(BF16), d_p=jnp.exp(cum_last), rk=r * k * rk_ref[...]))

    units = [(bb, h) for bb in range(len(rows)) for h in range(nh)]
    col = lambda name, u: rows[u[0]][name][:, u[1] * n:(u[1] + 1) * n]
    a_all = [_dot_nt(col('left', u), jnp.concatenate([col('bt', u), col('kt', u)], axis=0))
             for u in units]
    key2 = jnp.where(right_half, lane2 - c_sz, lane2)
    w_u = [jnp.where(right_half & (key2 < row2), a[:c_sz], 0.0).astype(BF16) for a in a_all]
    w_y = [jnp.where(key2 <= row2, a[c_sz:], 0.0).astype(BF16) for a in a_all]

    same_block = (row2 >> inv_shift) == (key2 >> inv_shift)
    x = [jnp.where((lane2 < row2) & same_block, a[:c_sz], jnp.where(lane2 == row2 + c_sz, 1.0, 0.0))
         for a in a_all]
    for _ in range(inv_shift):
        hi = [xu.astype(BF16) for xu in x]
        lo = [(xu - h_.astype(F32)).astype(BF16) for xu, h_ in zip(x, hi)]
        x = [_dot(h_[:, :c_sz], h_) + _dot(h_[:, :c_sz], l_) + _dot(l_[:, :c_sz], h_)
             + jnp.where(right_half, xu, 0.0) for xu, h_, l_ in zip(x, hi, lo)]
    x_b = [xu.astype(BF16) for xu in x]
    solve_diag = lambda j, z: _dot(x_b[j], jnp.concatenate([zeros_b, z.astype(BF16)], axis=0))
    q_b = [solve_diag(j, jnp.where((ci < ri) & ((ri >> inv_shift) != (ci >> inv_shift)),
                                   a[:c_sz, :c_sz], 0.0)).astype(BF16)
           for j, a in enumerate(a_all)]

    s0 = [st_ref[bb, h] for bb, h in units]
    ls0 = [_dot_nt(col('left', u), s0[j].astype(BF16)) for j, u in enumerate(units)]
    rhs = [ls0[j][:c_sz] + _dot(w_u[j], jnp.concatenate([zeros_b, col('v_b', u)], axis=0))
           for j, u in enumerate(units)]
    g0 = [solve_diag(j, rhs[j]) for j in range(len(units))]
    u_f = g0
    for _ in range(c_sz // RWKV_INV_BLOCK - 1):
        u_f = [g0[j] + _dot(q_b[j], u_f[j].astype(BF16)) for j in range(len(units))]
    u_b = [uj.astype(BF16) for uj in u_f]
    uv = [jnp.concatenate([u_b[j], col('v_b', u)], axis=0) for j, u in enumerate(units)]
    y = [ls0[j][c_sz:] + _dot(w_y[j], uv[j]) for j in range(len(units))]
    uv_t = [_dot_nt(eye_b, uv_j).astype(BF16) for uv_j in uv]
    for j, u in enumerate(units):
        st_ref[u[0], u[1]] = (s0[j] * col('d_p', u)
                              + _dot(uv_t[j], jnp.concatenate([col('bh', u), col('kh', u)], axis=0)))

    for bb in range(len(rows)):
        outs = []
        for h in range(nh):
            sl = slice(h * n, (h + 1) * n)
            yh = y[bb * nh + h]
            mean = jnp.mean(yh, axis=-1, keepdims=True)
            var = jnp.mean(jnp.square(yh - mean), axis=-1, keepdims=True)
            yn = (yh - mean) * lax.rsqrt(var + RWKV_GN_EPS)
            yn = yn * gg_ref[:, sl] + gb_ref[:, sl]
            bonus = jnp.sum(rows[bb]['rk'][:, sl], axis=-1, keepdims=True) * rows[bb]['v'][:, sl]
            outs.append((yn + bonus) * g_ref[bb, :, sl])
        o_ref[bb] = jnp.concatenate(outs, axis=1)


def _rwkv_scan(r, k, v, kk, b, lw, g, r_k, gn_gain, gn_bias, *, batch, seq, nb=4):
    t = r.shape[0]
    nc = seq // RWKV_CHUNK
    tok = pl.BlockSpec((nb, RWKV_CHUNK, RWKV_DIM), lambda bi, c: (bi, c, 0))
    par = pl.BlockSpec((1, RWKV_DIM), lambda bi, c: (0, 0))
    per_batch = lambda a: a.reshape(batch, seq, RWKV_DIM)
    out = pl.pallas_call(
        _rwkv_scan_body,
        out_shape=jax.ShapeDtypeStruct((batch, seq, RWKV_DIM), F32),
        grid=(batch // nb, nc),
        in_specs=[tok] * 7 + [par] * 3,
        out_specs=tok,
        scratch_shapes=[pltpu.VMEM((nb, RWKV_HEADS, RWKV_HEAD_DIM, RWKV_HEAD_DIM), F32)],
        compiler_params=_params(("parallel", "arbitrary")),
        name="rwkv_scan",
    )(*(per_batch(a) for a in (r, k, v, kk, b, lw, g)), r_k, gn_gain, gn_bias)
    return out.reshape(t, RWKV_DIM)


def _compress_body(x_ref, w1s_ref, w1_ref, pe_ref, w2_ref, w2t_ref, o_ref, ot_ref):
    n_rows = x_ref.shape[1]
    both = jnp.zeros((n_rows, 2 * CMP_HIDDEN), F32)
    for l in range(CMP_STRIDE):
        both = both + _dot(x_ref[0, :, l, :], w1s_ref[0, l])
    second_next = pltpu.roll(both[:, CMP_HIDDEN:], n_rows - 1, 0)
    pe = jnp.broadcast_to(pe_ref[0], (8, pe_ref.shape[2])).astype(BF16)
    pe_term = _dot(pe, w1_ref[0])[0:1, :]
    hid = both[:, :CMP_HIDDEN] + second_next + pe_term
    act = jax.nn.gelu(hid).astype(BF16)
    o_ref[0, 0] = _dot(act, w2_ref[0])
    ot_ref[0, 0] = _dot_nt(w2t_ref[0], act)


def _compress(kv, w1, pe, w2, *, bsz, seq):
    g, dh = NSA_KV_GROUPS, NSA_HEAD_DIM
    rows = seq // CMP_STRIDE
    w1r = jnp.pad(w1.reshape(2, CMP_LEN, dh, CMP_HIDDEN), ((0, 0), (0, 0), (0, LANES - dh), (0, 0)))
    w1s = jnp.concatenate([w1r[:, :CMP_STRIDE], w1r[:, CMP_STRIDE:]], axis=3)
    return pl.pallas_call(
        _compress_body,
        out_shape=[jax.ShapeDtypeStruct((2, bsz * g, rows, dh), F32),
                   jax.ShapeDtypeStruct((2, bsz * g, dh, rows), F32)],
        grid=(2, bsz * g),
        in_specs=[
            pl.BlockSpec((1, rows, CMP_STRIDE, LANES), lambda s, i: (i // g, 0, 0, s * g + i % g)),
            pl.BlockSpec((1, CMP_STRIDE, LANES, 2 * CMP_HIDDEN), lambda s, i: (s, 0, 0, 0)),
            pl.BlockSpec((1, CMP_LEN * dh, CMP_HIDDEN), lambda s, i: (s, 0, 0)),
            pl.BlockSpec((1, 1, CMP_LEN * dh), lambda s, i: (s, 0, 0)),
            pl.BlockSpec((1, CMP_HIDDEN, dh), lambda s, i: (s, 0, 0)),
            pl.BlockSpec((1, dh, CMP_HIDDEN), lambda s, i: (s, 0, 0)),
        ],
        out_specs=[pl.BlockSpec((1, 1, rows, dh), lambda s, i: (s, i, 0, 0)),
                   pl.BlockSpec((1, 1, dh, rows), lambda s, i: (s, i, 0, 0))],
        compiler_params=_params(("parallel", "parallel")),
        name="nsa_compress",
    )(kv.reshape(bsz, rows, CMP_STRIDE, KV_COLS), w1s, w1, pe, w2, w2.transpose(0, 2, 1))


def _t5_bucket(dist):
    n = jnp.maximum(dist, 0)
    exact = REL_BUCKETS // 2
    nf = jnp.maximum(n, 1).astype(F32)
    scaled = jnp.log(nf / exact) / math.log(REL_MAX_DIST / exact) * (REL_BUCKETS - exact)
    large = exact + jnp.floor(scaled).astype(jnp.int32)
    large = jnp.minimum(large, REL_BUCKETS - 1)
    return jnp.where(n < exact, n, large)


def _bias_body(tab_ref, bc_ref, bt_ref, *, seq, n_cmp_pad):
    h = pl.program_id(0)
    tq = NSA_TQ

    def lookup(dist):
        bucket = _t5_bucket(dist)
        out = jnp.zeros(dist.shape, F32)
        for bkt in range(REL_BUCKETS):
            out = jnp.where(bucket == bkt, tab_ref[bkt, h] * LOG2E, out)
        return out

    key = lax.broadcasted_iota(jnp.int32, (tq, tq), 0)
    qry = lax.broadcasted_iota(jnp.int32, (tq, tq), 1)
    bt_ref[0, BT_DIAG] = jnp.where(qry >= key, lookup(qry - key), MASKED)
    bt_ref[0, BT_PREV] = lookup(tq + qry - key)
    bt_ref[0, BT_FAR] = lookup(2 * tq + qry - key)
    bt_ref[0, BT_WIN_EDGE] = jnp.where(qry < key, lookup(WINDOW + qry - key), MASKED)
    bt_ref[0, BT_NONE] = jnp.full((tq, tq), MASKED, F32)

    cmp_end = lax.broadcasted_iota(jnp.int32, (n_cmp_pad, tq), 0) * CMP_STRIDE + CMP_LEN - 1
    qry_c = lax.broadcasted_iota(jnp.int32, (n_cmp_pad, tq), 1)

    def cmp_tile(i, carry):
        bc_ref[0, i] = lookup(i * tq + qry_c - cmp_end)
        return carry

    lax.fori_loop(0, seq // tq, cmp_tile, 0)


def _bias_tables(rel_bias, seq):
    g, hpg, tq = NSA_KV_GROUPS, NSA_HPG, NSA_TQ
    n_cmp_pad = seq // CMP_STRIDE
    nq = seq // tq
    return pl.pallas_call(
        functools.partial(_bias_body, seq=seq, n_cmp_pad=n_cmp_pad),
        out_shape=[jax.ShapeDtypeStruct((g, nq, n_cmp_pad, hpg * tq), F32),
                   jax.ShapeDtypeStruct((g, BT_COUNT, tq, hpg * tq), F32)],
        grid=(NSA_HEADS,),
        in_specs=[pl.BlockSpec(memory_space=pltpu.SMEM)],
        out_specs=[pl.BlockSpec((1, nq, n_cmp_pad, tq), lambda h: (h // hpg, 0, 0, h % hpg)),
                   pl.BlockSpec((1, BT_COUNT, tq, tq), lambda h: (h // hpg, 0, 0, h % hpg))],
        compiler_params=_params(("parallel",)),
        name="nsa_bias",
    )(rel_bias)


def _eye(n, dtype):
    return (lax.broadcasted_iota(jnp.int32, (n, n), 0)
            == lax.broadcasted_iota(jnp.int32, (n, n), 1)).astype(dtype)


def _nsa_body(q_ref, kc_ref, vct_ref, ks_ref, vst_ref, kw_ref, vwt_ref, bc_ref, bt_ref, gate_ref,
              o_ref, m_ref, acc_ref):
    tq, hpg, dh = NSA_TQ, NSA_HPG, NSA_HEAD_DIM
    rws = hpg * tq
    n_blk_log2 = SEL_BLOCK.bit_length() - 1
    n_blk = ks_ref.shape[1] // SEL_BLOCK
    n_cmp_pad = kc_ref.shape[2]
    kw = ks_ref.shape[2]
    i = pl.program_id(2)

    def per_head(x):
        return [x[:, hh * tq:(hh + 1) * tq] for hh in range(hpg)]

    xq = (q_ref[0] * (dh ** -0.5 * LOG2E)).astype(BF16)
    eye_d = _eye(dh, BF16)
    q_t = jnp.concatenate([_dot_nt(eye_d, xq[:, hh * dh:(hh + 1) * dh]) for hh in range(hpg)],
                          axis=1).astype(BF16)

    cmp_id = lax.broadcasted_iota(jnp.int32, (n_cmp_pad, rws), 0)
    t_pos = i * tq + (lax.broadcasted_iota(jnp.int32, (n_cmp_pad, rws), 1) & (tq - 1))
    valid = (t_pos - (cmp_id * CMP_STRIDE + CMP_LEN - 1) >= 0) & (cmp_id < n_cmp_pad - 1)
    s = jnp.where(valid, _dot(kc_ref[0, 0].astype(BF16), q_t) + bc_ref[0, 0], MASKED)
    e = jnp.where(valid, jnp.exp2(s - jnp.max(s, axis=0, keepdims=True)), 0.0)
    den = jnp.sum(e, axis=0, keepdims=True)
    p_c = e / jnp.where(den > 0.0, den, 1.0)
    o_cmp = _dot(vct_ref[0, 0].astype(BF16), p_c.astype(BF16))

    p_heads = per_head(p_c)
    p_sum = p_heads[0]
    for ph in p_heads[1:]:
        p_sum = p_sum + ph
    blk_o = lax.broadcasted_iota(jnp.int32, (n_blk, n_cmp_pad), 0)
    cmp_o = lax.broadcasted_iota(jnp.int32, (n_blk, n_cmp_pad), 1)
    overlap_t = ((cmp_o * CMP_STRIDE <= blk_o * SEL_BLOCK + SEL_BLOCK - 1)
                 & (cmp_o * CMP_STRIDE + CMP_LEN - 1 >= blk_o * SEL_BLOCK)).astype(F32)
    imp = _dot(overlap_t, p_sum, HI)
    jj = lax.broadcasted_iota(jnp.int32, (n_blk, tq), 0)
    cur = (i * tq + lax.broadcasted_iota(jnp.int32, (n_blk, tq), 1)) >> n_blk_log2
    forced = (jj == 0) | (jj == cur) | (jj == cur - 1)
    imp = jnp.where(jj > cur, -1e6, jnp.where(forced, 1e6, imp))
    rank = jnp.zeros((n_blk, tq), jnp.int32)
    for a in range(n_blk):
        row = imp[a:a + 1, :]
        beats = (row > imp) | ((row == imp) & (a < jj))
        rank = rank + beats.astype(jnp.int32)
    not_sel = jnp.where(rank < SEL_TOP_N, 0.0, 1.0).astype(BF16)

    q_aug = jnp.concatenate([q_t, jnp.concatenate([not_sel] * hpg, axis=1),
                             jnp.zeros((kw - dh - n_blk, rws), BF16)], axis=0)
    q_pad = jnp.concatenate([q_t, jnp.zeros((kw - dh, rws), BF16)], axis=0)


    n_win = WINDOW // tq
    tiles, scores = [], []
    for delta in range(n_win + 1):
        entry = {0: BT_DIAG, 1: BT_PREV, n_win: BT_WIN_EDGE}.get(delta, BT_FAR)
        if delta > 0:
            entry = jnp.where(i - delta >= 0, entry, BT_NONE)
        tiles.append(jnp.maximum(i - delta, 0))
        off = pl.multiple_of(tiles[-1] * tq, tq)
        scores.append(_dot(kw_ref[0, pl.ds(off, tq), :], q_pad) + bt_ref[0, entry])
    m_all = scores[0]
    for sc in scores[1:]:
        m_all = jnp.maximum(m_all, sc)
    m_w = jnp.max(m_all, axis=0, keepdims=True)
    acc_w = jnp.zeros((vwt_ref.shape[2], rws), F32)
    for kt, sc in zip(tiles, scores):
        acc_w = acc_w + _dot(vwt_ref[0, kt], jnp.exp2(sc - m_w).astype(BF16))
    o_win = acc_w[0:dh] / acc_w[dh:dh + 1]

    g_t = _dot_nt(_eye(GATE_PAD, F32), gate_ref[0], HI)
    sig = jax.nn.sigmoid(g_t[0:4 * hpg])
    gate = [jnp.concatenate([sig[br * hpg + hh:br * hpg + hh + 1] for hh in range(hpg)], axis=1)
            for br in range(3)]
    m_ref[...] = jnp.full(m_ref.shape, MASKED, F32)
    acc_ref[...] = jnp.zeros(acc_ref.shape, F32)

    def sel_update(tiles):
        scores = []
        for kt in tiles:
            off = pl.multiple_of(kt * tq, tq)
            scores.append(_dot(ks_ref[0, pl.ds(off, tq), :], q_aug)
                          + bt_ref[0, jnp.minimum(i - kt, BT_FAR)])
        s_max = scores[0]
        for sc in scores[1:]:
            s_max = jnp.maximum(s_max, sc)
        m_prev = m_ref[...]
        m_new = jnp.maximum(m_prev, jnp.max(s_max, axis=0, keepdims=True))
        acc = jnp.exp2(m_prev - m_new) * acc_ref[...]
        for kt, sc in zip(tiles, scores):
            acc = acc + _dot(vst_ref[0, kt], jnp.exp2(sc - m_new).astype(BF16))
        m_ref[...] = m_new
        acc_ref[...] = acc

    def sel_pair(j, carry):
        sel_update([2 * j, 2 * j + 1])
        return carry

    lax.fori_loop(0, (i + 1) // 2, sel_pair, 0)

    @pl.when(i % 2 == 0)
    def _():
        sel_update([i])

    o_sel = acc_ref[0:dh, :] / acc_ref[dh:dh + 1, :]

    y_t = (gate[0] * o_cmp + gate[1] * o_sel + gate[2] * o_win).astype(BF16)
    eye_q = _eye(tq, BF16)
    o_ref[0] = jnp.concatenate([_dot_nt(eye_q, yh) for yh in per_head(y_t)],
                               axis=1).astype(BF16)


def _nsa_attention(proj, kv, kv_t, kvc, kvc_t, bias_c, bias_t, *, bsz, seq):
    g, tq, hpg, dh = NSA_KV_GROUPS, NSA_TQ, NSA_HPG, NSA_HEAD_DIM
    n_rows = kvc.shape[2]
    nq = seq // tq
    group_w = hpg * dh
    kv_spec = lambda pos: pl.BlockSpec((1, seq, LANES), lambda b, gi, i: (b, 0, pos * g + gi))
    kvt_spec = lambda pos: pl.BlockSpec((1, nq, LANES, tq), lambda b, gi, i: (b, 0, pos * g + gi, 0))
    kv3 = kv.reshape(bsz, seq, KV_COLS)
    proj3 = proj.reshape(bsz, seq, PROJ_COLS)
    out = pl.pallas_call(
        _nsa_body,
        out_shape=jax.ShapeDtypeStruct((bsz, seq, NSA_DIM), BF16),
        grid=(bsz, g, nq),
        in_specs=[
            pl.BlockSpec((1, tq, group_w), lambda b, gi, i: (b, i, COL_QNSA // group_w + gi)),
            pl.BlockSpec((1, 1, n_rows, dh), lambda b, gi, i: (0, b * g + gi, 0, 0)),
            pl.BlockSpec((1, 1, dh, n_rows), lambda b, gi, i: (1, b * g + gi, 0, 0)),
            kv_spec(ROW_K_SEL), kvt_spec(T_V_SEL), kv_spec(ROW_K_WIN), kvt_spec(T_V_WIN),
            pl.BlockSpec((1, 1, n_rows, hpg * tq), lambda b, gi, i: (gi, i, 0, 0)),
            pl.BlockSpec((1, BT_COUNT, tq, hpg * tq), lambda b, gi, i: (gi, 0, 0, 0)),
            pl.BlockSpec((1, tq, GATE_PAD), lambda b, gi, i: (b, i, COL_GNSA // GATE_PAD + gi)),
        ],
        out_specs=pl.BlockSpec((1, tq, group_w), lambda b, gi, i: (b, i, gi)),
        scratch_shapes=[pltpu.VMEM((1, hpg * tq), F32),
                        pltpu.VMEM((LANES, hpg * tq), F32)],
        compiler_params=_params(("parallel", "parallel", "arbitrary")),
        name="nsa_attention",
    )(proj3, kvc, kvc_t, kv3, kv_t, kv3, kv_t, bias_c, bias_t, proj3)
    return out.reshape(bsz * seq, NSA_DIM)


def _mem_body(q_ref, kv_ref, o_ref):
    outs = []
    for h in range(MEM_HEADS):
        sl = slice(h * MEM_HEAD_DIM, (h + 1) * MEM_HEAD_DIM)
        qh = (q_ref[:, sl] * (MEM_HEAD_DIM ** -0.5)).astype(BF16)
        kh = kv_ref[0, :, sl].astype(BF16)
        vh = kv_ref[0, :, MEM_DIM + h * MEM_HEAD_DIM:MEM_DIM + (h + 1) * MEM_HEAD_DIM].astype(BF16)
        s = _dot_nt(qh, kh)
        e = jnp.exp(s - jnp.max(s, axis=-1, keepdims=True))
        p = e / jnp.sum(e, axis=-1, keepdims=True)
        outs.append(_dot(p.astype(BF16), vh))
    o_ref[...] = jnp.concatenate(outs, axis=1)


def _mem_attention(proj, kv, *, seq, tq=512):
    t = proj.shape[0]
    bsz, m, _ = kv.shape
    per_seq = seq // tq
    return pl.pallas_call(
        _mem_body,
        out_shape=jax.ShapeDtypeStruct((t, MEM_DIM), F32),
        grid=(t // tq,),
        in_specs=[
            pl.BlockSpec((tq, MEM_DIM), lambda i: (i, COL_QMEM // MEM_DIM)),
            pl.BlockSpec((1, m, 2 * MEM_DIM), lambda i: (i // per_seq, 0, 0)),
        ],
        out_specs=pl.BlockSpec((tq, MEM_DIM), lambda i: (i, 0)),
        compiler_params=_params(("parallel",)),
        name="mem_attention",
    )(proj, kv)


def _merge_body(x_ref, gain_ref, yr_ref, yn_ref, ym_ref, wg_ref, wr_ref, wn_ref, wm_ref, wo_ref,
                o_ref):
    d = x_ref.shape[1]
    x = x_ref[...]
    gates = jax.nn.sigmoid(_dot(_rms(x, gain_ref[...]).astype(BF16), wg_ref[...]))
    merged = (gates[:, 0:d] * _dot(yr_ref[...].astype(BF16), wr_ref[...])
              + gates[:, d:2 * d] * _dot(yn_ref[...], wn_ref[...])
              + gates[:, 2 * d:3 * d] * _dot(ym_ref[...].astype(BF16), wm_ref[...]))
    o_ref[...] = x + _dot(merged.astype(BF16), wo_ref[...])


def _merge(x, gain, y_rwkv, y_nsa, y_mem, w_g, w_r, w_n, w_m, w_o, *, tm=512):
    t, d = x.shape
    row = lambda i: (i, 0)
    const = lambda i: (0, 0)
    return pl.pallas_call(
        _merge_body,
        out_shape=jax.ShapeDtypeStruct((t, d), F32),
        grid=(t // tm,),
        in_specs=[
            pl.BlockSpec((tm, d), row),
            pl.BlockSpec((1, d), const),
            pl.BlockSpec((tm, RWKV_DIM), row),
            pl.BlockSpec((tm, NSA_DIM), row),
            pl.BlockSpec((tm, MEM_DIM), row),
            pl.BlockSpec((d, N_BRANCH * d), const),
            pl.BlockSpec((RWKV_DIM, d), const),
            pl.BlockSpec((NSA_DIM, d), const),
            pl.BlockSpec((MEM_DIM, d), const),
            pl.BlockSpec((d, d), const),
        ],
        out_specs=pl.BlockSpec((tm, d), row),
        compiler_params=_params(("parallel",)),
        name="merge",
    )(x, gain, y_rwkv, y_nsa, y_mem, w_g, w_r, w_n, w_m, w_o)


def _row(a):
    return a.reshape(1, -1)


def _in_proj(x, p, l, seq):
    d = x.shape[1]
    g, hpg, dh = NSA_KV_GROUPS, NSA_HPG, NSA_HEAD_DIM
    w_in = p['w_in'][l]
    o = 0
    parts = {}
    for name, size in (('rwkv', RWKV_PROJ), ('q', NSA_DIM), ('kv', KV_KINDS * NSA_KV_DIM),
                       ('g_nsa', 3 * NSA_HEADS), ('q_mem', MEM_DIM)):
        parts[name] = w_in[:, o:o + size]
        o += size
    gates = parts['g_nsa'].reshape(d, 3, g, hpg).transpose(0, 2, 1, 3).reshape(d, g, 3 * hpg)
    gates = jnp.pad(gates, ((0, 0), (0, 0), (0, GATE_PAD - 3 * hpg))).reshape(d, g * GATE_PAD)
    w_other = jnp.concatenate([parts['q'], parts['q_mem'], gates], axis=1).astype(BF16)
    w_kv = jnp.pad(parts['kv'].reshape(d, KV_KINDS, g, dh), ((0, 0), (0, 0), (0, 0), (0, LANES - dh)))
    w_row = w_kv[:, KV_ROW_KINDS, :, :].reshape(d, KV_COLS).astype(BF16)
    w_t = w_kv[:, KV_T_KINDS, :, :].reshape(d, KV_T_ROWS).T.astype(BF16)
    w_lora = jnp.zeros((LORA_DIM, 3 * RWKV_DIM), F32)
    w_lora = w_lora.at[0:DECAY_LORA, 0:RWKV_DIM].set(p['rwkv_w2'][l])
    w_lora = w_lora.at[DECAY_LORA:DECAY_LORA + AAA_LORA, RWKV_DIM:2 * RWKV_DIM].set(p['rwkv_a2'][l])
    w_lora = w_lora.at[DECAY_LORA + AAA_LORA:, 2 * RWKV_DIM:].set(p['rwkv_g2'][l])
    gain = _row(p['mix_norm'][l])
    proj, *rwkv_in = _in_proj_call(
        x, seq, gain, parts['rwkv'].astype(BF16), w_other, _row(p['rwkv_mu'][l]),
        w_lora.astype(BF16), _row(p['rwkv_w0'][l]), _row(p['rwkv_a0'][l]),
        _row(p['rwkv_k_k'][l]), _row(p['rwkv_k_a'][l]))
    kv, kv_t = _kv_proj(x, gain, w_row, w_t, seq=seq)
    return proj, kv, kv_t, rwkv_in


def _rwkv_branch(rwkv_in, p, l, bsz, seq):
    return _rwkv_scan(*rwkv_in, _row(p['rwkv_r_k'][l]), _row(p['rwkv_gn_gain'][l]),
                      _row(p['rwkv_gn_bias'][l]), batch=bsz, seq=seq)


def _nsa_branch(proj, kv, kv_t, bias_c, bias_t, p, l, bsz, seq):
    w1 = jnp.stack([p['cmp_k_w1'][l], p['cmp_v_w1'][l]]).astype(BF16)
    pe = jnp.stack([p['cmp_pe_k'][l].reshape(1, -1), p['cmp_pe_v'][l].reshape(1, -1)])
    w2 = jnp.stack([p['cmp_k_w2'][l], p['cmp_v_w2'][l]]).astype(BF16)
    kvc, kvc_t = _compress(kv, w1, pe, w2, bsz=bsz, seq=seq)
    return _nsa_attention(proj, kv, kv_t, kvc, kvc_t, bias_c, bias_t, bsz=bsz, seq=seq)


def _mem_branch(proj, mem, p, l):
    bsz, m_tok, d = mem.shape
    w_kv = jnp.concatenate([p['mem_w_k'][l], p['mem_w_v'][l]], axis=1).astype(BF16)
    kv_mem = _norm_matmul(mem.reshape(bsz * m_tok, d), _row(p['mem_norm'][l]), w_kv,
                          tm=min(1024, bsz * m_tok), tn=512, name="mem_kv")
    return _mem_attention(proj, kv_mem.reshape(bsz, m_tok, 2 * MEM_DIM),
                          seq=proj.shape[0] // bsz)


def _layer(x, mem, l, bias_c, bias_t, p):
    bsz, seq, d = x.shape
    t = bsz * seq
    row = _row
    x = x.reshape(t, d)

    x = _ffn(x, row(p['ffn1_norm'][l]), p['ffn1_w_gate'][l].astype(BF16),
             p['ffn1_w_up'][l].astype(BF16), p['ffn1_w_down'][l].astype(BF16),
             row(p['final_norm']), final=False)

    proj, kv, kv_t, rwkv_in = _in_proj(x, p, l, seq)
    y_rwkv = _rwkv_branch(rwkv_in, p, l, bsz, seq)
    y_nsa = _nsa_branch(proj, kv, kv_t, bias_c, bias_t, p, l, bsz, seq)
    y_mem = _mem_branch(proj, mem, p, l)

    w_gate = p['w_in'][l][:, -N_BRANCH * d:].astype(BF16)
    x = _merge(x, row(p['mix_norm'][l]), y_rwkv, y_nsa, y_mem, w_gate,
               p['w_br_rwkv'][l].astype(BF16), p['w_br_nsa'][l].astype(BF16),
               p['w_br_mem'][l].astype(BF16), p['w_out'][l].astype(BF16))

    last = l == p['ffn1_norm'].shape[0] - 1
    x = _ffn(x, row(p['ffn2_norm'][l]), p['ffn2_w_gate'][l].astype(BF16),
             p['ffn2_w_up'][l].astype(BF16), p['ffn2_w_down'][l].astype(BF16),
             row(p['final_norm']), final=last)
    return x.reshape(bsz, seq, d)


def kernel(x, mem, ffn1_norm, ffn1_w_gate, ffn1_w_up, ffn1_w_down, mix_norm, w_in, rwkv_mu, rwkv_w0, rwkv_w2, rwkv_a0, rwkv_a2, rwkv_g2, rwkv_k_k, rwkv_k_a, rwkv_r_k, rwkv_gn_gain, rwkv_gn_bias, cmp_pe_k, cmp_k_w1, cmp_k_w2, cmp_pe_v, cmp_v_w1, cmp_v_w2, rel_bias, mem_norm, mem_w_k, mem_w_v, w_br_rwkv, w_br_nsa, w_br_mem, w_out, ffn2_norm, ffn2_w_gate, ffn2_w_up, ffn2_w_down, final_norm):
    p = dict(ffn1_norm=ffn1_norm, ffn1_w_gate=ffn1_w_gate, ffn1_w_up=ffn1_w_up,
             ffn1_w_down=ffn1_w_down, mix_norm=mix_norm, w_in=w_in, rwkv_mu=rwkv_mu,
             rwkv_w0=rwkv_w0, rwkv_w2=rwkv_w2, rwkv_a0=rwkv_a0, rwkv_a2=rwkv_a2, rwkv_g2=rwkv_g2,
             rwkv_k_k=rwkv_k_k, rwkv_k_a=rwkv_k_a, rwkv_r_k=rwkv_r_k, rwkv_gn_gain=rwkv_gn_gain,
             rwkv_gn_bias=rwkv_gn_bias, cmp_pe_k=cmp_pe_k, cmp_k_w1=cmp_k_w1, cmp_k_w2=cmp_k_w2,
             cmp_pe_v=cmp_pe_v, cmp_v_w1=cmp_v_w1, cmp_v_w2=cmp_v_w2, mem_norm=mem_norm,
             mem_w_k=mem_w_k, mem_w_v=mem_w_v, w_br_rwkv=w_br_rwkv, w_br_nsa=w_br_nsa,
             w_br_mem=w_br_mem, w_out=w_out, ffn2_norm=ffn2_norm, ffn2_w_gate=ffn2_w_gate,
             ffn2_w_up=ffn2_w_up, ffn2_w_down=ffn2_w_down, final_norm=final_norm)
    bias_c, bias_t = _bias_tables(rel_bias, x.shape[1])
    for l in range(ffn1_norm.shape[0]):
        x = _layer(x, mem, l, bias_c, bias_t, p)
    return x
```

```python
import functools
import math

import jax
import jax.numpy as jnp
from jax import lax
from jax.experimental import pallas as pl
from jax.experimental.pallas import tpu as pltpu

F32 = jnp.float32
BF16 = jnp.bfloat16
HI = lax.Precision.HIGHEST

D_MODEL = 1024
NORM_EPS = 1e-6
D_FF = 2816
RWKV_HEADS = 8
RWKV_HEAD_DIM = 64
RWKV_DIM = RWKV_HEADS * RWKV_HEAD_DIM
DECAY_LORA = 64
AAA_LORA = 64
GATE_LORA = 128
LORA_DIM = DECAY_LORA + AAA_LORA + GATE_LORA
RWKV_GN_EPS = 64e-5
RWKV_PROJ = 3 * RWKV_DIM + LORA_DIM
NSA_HEADS = 8
NSA_KV_GROUPS = 2
NSA_HPG = NSA_HEADS // NSA_KV_GROUPS
NSA_HEAD_DIM = 64
NSA_DIM = NSA_HEADS * NSA_HEAD_DIM
NSA_KV_DIM = NSA_KV_GROUPS * NSA_HEAD_DIM
CMP_LEN = 32
CMP_STRIDE = 16
CMP_HIDDEN = 256
SEL_BLOCK = 64
SEL_TOP_N = 16
WINDOW = 512
REL_BUCKETS = 32
REL_MAX_DIST = 128
MEM_HEADS = 4
MEM_HEAD_DIM = 128
MEM_DIM = MEM_HEADS * MEM_HEAD_DIM
N_BRANCH = 3

LANES = 128
GATE_PAD = LANES
COL_QNSA = 0
COL_QMEM = COL_QNSA + NSA_DIM
COL_GNSA = COL_QMEM + MEM_DIM
PROJ_COLS = COL_GNSA + NSA_KV_GROUPS * GATE_PAD
assert COL_QMEM % MEM_DIM == 0 and COL_GNSA % GATE_PAD == 0
KV_KINDS = 6
KV_ROW_KINDS = (0, 1, 2, 4)
KV_T_KINDS = (3, 5)
KV_COLS = len(KV_ROW_KINDS) * NSA_KV_GROUPS * LANES
KV_T_ROWS = len(KV_T_KINDS) * NSA_KV_GROUPS * LANES
ROW_K_SEL, ROW_K_WIN = KV_ROW_KINDS.index(2), KV_ROW_KINDS.index(4)
T_V_SEL, T_V_WIN = KV_T_KINDS.index(3), KV_T_KINDS.index(5)
LOG2E = 1.4426950408889634

RWKV_CHUNK = 64
RWKV_INV_BLOCK = 16
NSA_TQ = 256
MASKED = -1e30
BT_DIAG, BT_PREV, BT_FAR, BT_WIN_EDGE, BT_NONE, BT_COUNT = 0, 1, 2, 3, 4, 5
VMEM_LIMIT = 56 * 1024 * 1024


def _dot(a, b, precision=None):
    return jnp.dot(a, b, preferred_element_type=F32, precision=precision)


def _dot_nt(a, b, precision=None):
    return lax.dot_general(a, b, (((1,), (1,)), ((), ())), preferred_element_type=F32,
                           precision=precision)


def _params(semantics):
    return pltpu.CompilerParams(dimension_semantics=semantics, vmem_limit_bytes=VMEM_LIMIT)


def _rms(x, g):
    return x * lax.rsqrt(jnp.mean(x * x, axis=-1, keepdims=True) + NORM_EPS) * g


def _ffn_body(x_ref, g_ref, wg_ref, wu_ref, wd_ref, fg_ref, o_ref, *, tf, final):
    x = x_ref[...]
    h = _rms(x, g_ref[...]).astype(BF16)
    acc = jnp.zeros(x.shape, F32)
    for j in range(wg_ref.shape[1] // tf):
        cols = slice(j * tf, (j + 1) * tf)
        act = (jax.nn.silu(_dot(h, wg_ref[:, cols])) * _dot(h, wu_ref[:, cols])).astype(BF16)
        acc = acc + _dot(act, wd_ref[cols, :])
    y = x + 0.5 * acc
    if final:
        y = _rms(y, fg_ref[...])
    o_ref[...] = y


def _ffn(x, gain, wg, wu, wd, final_gain, *, final, tm=1024, tf=256):
    t, d = x.shape
    f = wg.shape[1]
    resident = lambda shape: pl.BlockSpec(shape, lambda i: (0, 0), pipeline_mode=pl.Buffered(1))
    return pl.pallas_call(
        functools.partial(_ffn_body, tf=tf, final=final),
        out_shape=jax.ShapeDtypeStruct((t, d), F32),
        grid=(t // tm,),
        in_specs=[
            pl.BlockSpec((tm, d), lambda i: (i, 0)),
            pl.BlockSpec((1, d), lambda i: (0, 0)),
            resident((d, f)),
            resident((d, f)),
            resident((f, d)),
            pl.BlockSpec((1, d), lambda i: (0, 0)),
        ],
        out_specs=pl.BlockSpec((tm, d), lambda i: (i, 0)),
        compiler_params=_params(("parallel",)),
        name="ffn_final" if final else "ffn",
    )(x, gain, wg, wu, wd, final_gain)


def _norm_matmul_body(x_ref, g_ref, w_ref, o_ref, h_ref):
    @pl.when(pl.program_id(1) == 0)
    def _():
        h_ref[...] = _rms(x_ref[...], g_ref[...]).astype(BF16)

    o_ref[...] = _dot(h_ref[...], w_ref[...])


def _norm_matmul(x, gain, w, *, tm, tn, name):
    t, d = x.shape
    n = w.shape[1]
    return pl.pallas_call(
        _norm_matmul_body,
        out_shape=jax.ShapeDtypeStruct((t, n), F32),
        grid=(t // tm, n // tn),
        in_specs=[
            pl.BlockSpec((tm, d), lambda i, j: (i, 0)),
            pl.BlockSpec((1, d), lambda i, j: (0, 0)),
            pl.BlockSpec((d, tn), lambda i, j: (0, j)),
        ],
        out_specs=pl.BlockSpec((tm, tn), lambda i, j: (i, j)),
        scratch_shapes=[pltpu.VMEM((tm, d), BF16)],
        compiler_params=_params(("parallel", "arbitrary")),
        name=name,
    )(x, gain, w)


def _write_kv(h, w_ref, wt_ref, o_ref, ot_ref, *, seq):
    dh, tk, tm = NSA_HEAD_DIM, NSA_TQ, h.shape[0]
    y = _dot(h, w_ref[...])
    row = lax.broadcasted_iota(jnp.int32, (tm, LANES), 0)
    lane = lax.broadcasted_iota(jnp.int32, (tm, LANES), 1)
    pos = (pl.program_id(0) * tm) % seq + row
    block_mark = jnp.where(lane - dh == (pos >> (SEL_BLOCK.bit_length() - 1)), MASKED, 0.0)
    for tile in range(KV_COLS // LANES):
        part = y[:, tile * LANES:(tile + 1) * LANES]
        if tile // NSA_KV_GROUPS == ROW_K_SEL:
            part = part + block_mark
        o_ref[:, tile * LANES:(tile + 1) * LANES] = part.astype(BF16)

    y_t = _dot_nt(wt_ref[...], h)
    row_t = lax.broadcasted_iota(jnp.int32, y_t.shape, 0)
    y_t = (y_t + jnp.where((row_t & (LANES - 1)) == dh, 1.0, 0.0)).astype(BF16)
    for c in range(tm // tk):
        ot_ref[0, c] = y_t[:, c * tk:(c + 1) * tk]


def _in_proj_body(x_ref, xp_ref, g_ref, wr_ref, wo_ref, wkv_ref, wkvt_ref, mu_ref, wl_ref, w0_ref,
                  a0_ref, kk_ref, ka_ref, proj_o, kv_o, kvt_o, r_o, k_o, v_o, kk_o, b_o, lw_o, g_o,
                  *, tiles_per_seq, seq):
    i = pl.program_id(0)
    gain = g_ref[...]
    h = _rms(x_ref[...], gain).astype(BF16)
    proj_o[...] = _dot(h, wo_ref[...])
    _write_kv(h, wkv_ref, wkvt_ref, kv_o, kvt_o, seq=seq)
    n_up = xp_ref.shape[0]
    h_up = _rms(xp_ref[...], gain).astype(BF16)
    p_all = _dot(jnp.concatenate([h_up, h], axis=0), wr_ref[...])
    p = p_all[n_up:]
    keep = jnp.where(i % tiles_per_seq == 0, 0.0, 1.0)
    prev_last = p_all[n_up - 1:n_up, :] * keep
    rows = lax.broadcasted_iota(jnp.int32, p.shape, 0)
    shifted = jnp.where(rows == 0, prev_last, pltpu.roll(p, 1, 0))
    x = p + (shifted - p) * mu_ref[...]

    r = x[:, 0:RWKV_DIM]
    k = x[:, RWKV_DIM:2 * RWKV_DIM]
    v = x[:, 2 * RWKV_DIM:3 * RWKV_DIM]
    s = x[:, 3 * RWKV_DIM:RWKV_PROJ]
    lane = lax.broadcasted_iota(jnp.int32, s.shape, 1)
    z = jnp.where(lane < DECAY_LORA, jnp.tanh(s),
                  jnp.where(lane < DECAY_LORA + AAA_LORA, s, jax.nn.sigmoid(s)))
    lo = _dot(z.astype(BF16), wl_ref[...])
    a = jax.nn.sigmoid(a0_ref[...] + lo[:, RWKV_DIM:2 * RWKV_DIM])

    kkr = k * kk_ref[...]
    sq = kkr * kkr
    sq_hi = sq.astype(BF16)
    sq_lo = (sq - sq_hi.astype(F32)).astype(BF16)
    shift = RWKV_HEAD_DIM.bit_length() - 1
    same_head = ((lax.broadcasted_iota(jnp.int32, (RWKV_DIM, RWKV_DIM), 0) >> shift)
                 == (lax.broadcasted_iota(jnp.int32, (RWKV_DIM, RWKV_DIM), 1) >> shift)).astype(BF16)
    ssq = _dot(sq_hi, same_head) + _dot(sq_lo, same_head)
    kk = kkr / jnp.maximum(jnp.sqrt(ssq), 1e-12)

    r_o[...] = r
    k_o[...] = k * (1.0 + (a - 1.0) * ka_ref[...])
    v_o[...] = v
    kk_o[...] = kk
    b_o[...] = kk * a
    lw_o[...] = -math.exp(-0.5) * jax.nn.sigmoid(w0_ref[...] + lo[:, 0:RWKV_DIM])
    g_o[...] = lo[:, 2 * RWKV_DIM:3 * RWKV_DIM]


def _in_proj_call(x, seq, gain, w_rwkv, w_other, w_kv, w_kv_t, mu, w_lora, w0, a0, k_k, k_a, *,
                  tm=512):
    t, d = x.shape
    tk = NSA_TQ
    per_seq = seq // tm
    row = lambda i: (i, 0)
    const = lambda i: (0, 0)
    resident = lambda shape: pl.BlockSpec(shape, const, pipeline_mode=pl.Buffered(1))
    vec = pl.BlockSpec((1, RWKV_DIM), const)
    tok = jax.ShapeDtypeStruct((t, RWKV_DIM), F32)
    return pl.pallas_call(
        functools.partial(_in_proj_body, tiles_per_seq=per_seq, seq=seq),
        out_shape=[jax.ShapeDtypeStruct((t, PROJ_COLS), F32),
                   jax.ShapeDtypeStruct((t, KV_COLS), BF16),
                   jax.ShapeDtypeStruct((t // seq, seq // tk, KV_T_ROWS, tk), BF16)] + [tok] * 7,
        grid=(t // tm,),
        in_specs=[
            pl.BlockSpec((tm, d), row),
            pl.BlockSpec((16, d), lambda i: (jnp.maximum(i * (tm // 16) - 1, 0), 0)),
            pl.BlockSpec((1, d), const),
            resident((d, RWKV_PROJ)),
            resident((d, PROJ_COLS)),
            resident((d, KV_COLS)),
            resident((KV_T_ROWS, d)),
            pl.BlockSpec((1, RWKV_PROJ), const),
            resident((LORA_DIM, 3 * RWKV_DIM)),
            vec, vec, vec, vec,
        ],
        out_specs=[pl.BlockSpec((tm, PROJ_COLS), row),
                   pl.BlockSpec((tm, KV_COLS), row),
                   pl.BlockSpec((1, tm // tk, KV_T_ROWS, tk),
                                lambda i: (i // per_seq, i % per_seq, 0, 0))]
        + [pl.BlockSpec((tm, RWKV_DIM), row)] * 7,
        compiler_params=_params(("parallel",)),
        name="in_proj",
    )(x, x, gain, w_rwkv, w_other, w_kv, w_kv_t, mu, w_lora, w0, a0, k_k, k_a)


def _rwkv_scan_body(r_ref, k_ref, v_ref, kk_ref, b_ref, lw_ref, g_ref, rk_ref, gg_ref, gb_ref,
                    o_ref, st_ref):
    c_sz, n, nh = RWKV_CHUNK, RWKV_HEAD_DIM, RWKV_HEADS

    @pl.when(pl.program_id(1) == 0)
    def _():
        st_ref[...] = jnp.zeros_like(st_ref)

    ri = lax.broadcasted_iota(jnp.int32, (c_sz, c_sz), 0)
    ci = lax.broadcasted_iota(jnp.int32, (c_sz, c_sz), 1)
    incl = ci <= ri
    eye_b = (ci == ri).astype(BF16)
    row2 = lax.broadcasted_iota(jnp.int32, (c_sz, 2 * c_sz), 0)
    lane2 = lax.broadcasted_iota(jnp.int32, (c_sz, 2 * c_sz), 1)
    right_half = lane2 >= c_sz
    zeros_b = jnp.zeros((c_sz, n), BF16)
    inv_shift = RWKV_INV_BLOCK.bit_length() - 1

    rows = []
    for bb in range(st_ref.shape[0]):
        lw = lw_ref[bb]
        cum = _dot(incl.astype(F32), lw, HI)
        cum_last = cum[c_sz - 1:c_sz, :]
        r, k, v, b = r_ref[bb], k_ref[bb], v_ref[bb], b_ref[bb]
        p_inv = jnp.exp(-cum)
        p_end = jnp.exp(cum_last - cum)
        rows.append(dict(
            left=jnp.concatenate([(-(kk_ref[bb] * jnp.exp(cum - lw))).astype(BF16),
                                  (r * jnp.exp(cum)).astype(BF16)], axis=0),
            bt=(b * p_inv).astype(BF16), kt=(k * p_inv).astype(BF16),
            bh=(b * p_end).astype(BF16), kh=(k * p_end).astype(BF16),
            v=v, v_b=v.astype(BF16), d_p=jnp.exp(cum_last), rk=r * k * rk_ref[...]))

    units = [(bb, h) for bb in range(len(rows)) for h in range(nh)]
    col = lambda name, u: rows[u[0]][name][:, u[1] * n:(u[1] + 1) * n]
    a_all = [_dot_nt(col('left', u), jnp.concatenate([col('bt', u), col('kt', u)], axis=0))
             for u in units]
    key2 = jnp.where(right_half, lane2 - c_sz, lane2)
    w_u = [jnp.where(right_half & (key2 < row2), a[:c_sz], 0.0).astype(BF16) for a in a_all]
    w_y = [jnp.where(key2 <= row2, a[c_sz:], 0.0).astype(BF16) for a in a_all]

    same_block = (row2 >> inv_shift) == (key2 >> inv_shift)
    x = [jnp.where((lane2 < row2) & same_block, a[:c_sz], jnp.where(lane2 == row2 + c_sz, 1.0, 0.0))
         for a in a_all]
    for _ in range(inv_shift):
        hi = [xu.astype(BF16) for xu in x]
        lo = [(xu - h_.astype(F32)).astype(BF16) for xu, h_ in zip(x, hi)]
        x = [_dot(h_[:, :c_sz], h_) + _dot(h_[:, :c_sz], l_) + _dot(l_[:, :c_sz], h_)
             + jnp.where(right_half, xu, 0.0) for xu, h_, l_ in zip(x, hi, lo)]
    x_b = [xu.astype(BF16) for xu in x]
    solve_diag = lambda j, z: _dot(x_b[j], jnp.concatenate([zeros_b, z.astype(BF16)], axis=0))
    q_b = [solve_diag(j, jnp.where((ci < ri) & ((ri >> inv_shift) != (ci >> inv_shift)),
                                   a[:c_sz, :c_sz], 0.0)).astype(BF16)
           for j, a in enumerate(a_all)]

    s0 = [st_ref[bb, h] for bb, h in units]
    ls0 = [_dot_nt(col('left', u), s0[j].astype(BF16)) for j, u in enumerate(units)]
    rhs = [ls0[j][:c_sz] + _dot(w_u[j], jnp.concatenate([zeros_b, col('v_b', u)], axis=0))
           for j, u in enumerate(units)]
    g0 = [solve_diag(j, rhs[j]) for j in range(len(units))]
    u_f = g0
    for _ in range(c_sz // RWKV_INV_BLOCK - 1):
        u_f = [g0[j] + _dot(q_b[j], u_f[j].astype(BF16)) for j in range(len(units))]
    u_b = [uj.astype(BF16) for uj in u_f]
    uv = [jnp.concatenate([u_b[j], col('v_b', u)], axis=0) for j, u in enumerate(units)]
    y = [ls0[j][c_sz:] + _dot(w_y[j], uv[j]) for j in range(len(units))]
    uv_t = [_dot_nt(eye_b, uv_j).astype(BF16) for uv_j in uv]
    for j, u in enumerate(units):
        st_ref[u[0], u[1]] = (s0[j] * col('d_p', u)
                              + _dot(uv_t[j], jnp.concatenate([col('bh', u), col('kh', u)], axis=0)))

    for bb in range(len(rows)):
        outs = []
        for h in range(nh):
            sl = slice(h * n, (h + 1) * n)
            yh = y[bb * nh + h]
            mean = jnp.mean(yh, axis=-1, keepdims=True)
            var = jnp.mean(jnp.square(yh - mean), axis=-1, keepdims=True)
            yn = (yh - mean) * lax.rsqrt(var + RWKV_GN_EPS)
            yn = yn * gg_ref[:, sl] + gb_ref[:, sl]
            bonus = jnp.sum(rows[bb]['rk'][:, sl], axis=-1, keepdims=True) * rows[bb]['v'][:, sl]
            outs.append((yn + bonus) * g_ref[bb, :, sl])
        o_ref[bb] = jnp.concatenate(outs, axis=1)


def _rwkv_scan(r, k, v, kk, b, lw, g, r_k, gn_gain, gn_bias, *, batch, seq, nb=4):
    t = r.shape[0]
    nc = seq // RWKV_CHUNK
    tok = pl.BlockSpec((nb, RWKV_CHUNK, RWKV_DIM), lambda bi, c: (bi, c, 0))
    par = pl.BlockSpec((1, RWKV_DIM), lambda bi, c: (0, 0))
    per_batch = lambda a: a.reshape(batch, seq, RWKV_DIM)
    out = pl.pallas_call(
        _rwkv_scan_body,
        out_shape=jax.ShapeDtypeStruct((batch, seq, RWKV_DIM), F32),
        grid=(batch // nb, nc),
        in_specs=[tok] * 7 + [par] * 3,
        out_specs=tok,
        scratch_shapes=[pltpu.VMEM((nb, RWKV_HEADS, RWKV_HEAD_DIM, RWKV_HEAD_DIM), F32)],
        compiler_params=_params(("parallel", "arbitrary")),
        name="rwkv_scan",
    )(*(per_batch(a) for a in (r, k, v, kk, b, lw, g)), r_k, gn_gain, gn_bias)
    return out.reshape(t, RWKV_DIM)


def _compress_body(x_ref, w1s_ref, w1_ref, pe_ref, w2_ref, w2t_ref, o_ref, ot_ref):
    n_rows = x_ref.shape[1]
    both = jnp.zeros((n_rows, 2 * CMP_HIDDEN), F32)
    for l in range(CMP_STRIDE):
        both = both + _dot(x_ref[0, :, l, :], w1s_ref[0, l])
    second_next = pltpu.roll(both[:, CMP_HIDDEN:], n_rows - 1, 0)
    pe = jnp.broadcast_to(pe_ref[0], (8, pe_ref.shape[2])).astype(BF16)
    pe_term = _dot(pe, w1_ref[0])[0:1, :]
    hid = both[:, :CMP_HIDDEN] + second_next + pe_term
    act = jax.nn.gelu(hid).astype(BF16)
    o_ref[0, 0] = _dot(act, w2_ref[0])
    ot_ref[0, 0] = _dot_nt(w2t_ref[0], act)


def _compress(kv, w1, pe, w2, *, bsz, seq):
    g, dh = NSA_KV_GROUPS, NSA_HEAD_DIM
    rows = seq // CMP_STRIDE
    w1r = jnp.pad(w1.reshape(2, CMP_LEN, dh, CMP_HIDDEN), ((0, 0), (0, 0), (0, LANES - dh), (0, 0)))
    w1s = jnp.concatenate([w1r[:, :CMP_STRIDE], w1r[:, CMP_STRIDE:]], axis=3)
    return pl.pallas_call(
        _compress_body,
        out_shape=[jax.ShapeDtypeStruct((2, bsz * g, rows, dh), F32),
                   jax.ShapeDtypeStruct((2, bsz * g, dh, rows), F32)],
        grid=(2, bsz * g),
        in_specs=[
            pl.BlockSpec((1, rows, CMP_STRIDE, LANES), lambda s, i: (i // g, 0, 0, s * g + i % g)),
            pl.BlockSpec((1, CMP_STRIDE, LANES, 2 * CMP_HIDDEN), lambda s, i: (s, 0, 0, 0)),
            pl.BlockSpec((1, CMP_LEN * dh, CMP_HIDDEN), lambda s, i: (s, 0, 0)),
            pl.BlockSpec((1, 1, CMP_LEN * dh), lambda s, i: (s, 0, 0)),
            pl.BlockSpec((1, CMP_HIDDEN, dh), lambda s, i: (s, 0, 0)),
            pl.BlockSpec((1, dh, CMP_HIDDEN), lambda s, i: (s, 0, 0)),
        ],
        out_specs=[pl.BlockSpec((1, 1, rows, dh), lambda s, i: (s, i, 0, 0)),
                   pl.BlockSpec((1, 1, dh, rows), lambda s, i: (s, i, 0, 0))],
        compiler_params=_params(("parallel", "parallel")),
        name="nsa_compress",
    )(kv.reshape(bsz, rows, CMP_STRIDE, KV_COLS), w1s, w1, pe, w2, w2.transpose(0, 2, 1))


def _t5_bucket(dist):
    n = jnp.maximum(dist, 0)
    exact = REL_BUCKETS // 2
    nf = jnp.maximum(n, 1).astype(F32)
    scaled = jnp.log(nf / exact) / math.log(REL_MAX_DIST / exact) * (REL_BUCKETS - exact)
    large = exact + jnp.floor(scaled).astype(jnp.int32)
    large = jnp.minimum(large, REL_BUCKETS - 1)
    return jnp.where(n < exact, n, large)


def _bias_body(tab_ref, bc_ref, bt_ref, *, seq, n_cmp_pad):
    h = pl.program_id(0)
    tq = NSA_TQ

    def lookup(dist):
        bucket = _t5_bucket(dist)
        out = jnp.zeros(dist.shape, F32)
        for bkt in range(REL_BUCKETS):
            out = jnp.where(bucket == bkt, tab_ref[bkt, h] * LOG2E, out)
        return out

    far = tab_ref[REL_BUCKETS - 1, h] * LOG2E
    assert tq + 1 >= REL_MAX_DIST and WINDOW - tq + 1 >= REL_MAX_DIST

    key = lax.broadcasted_iota(jnp.int32, (tq, tq), 0)
    qry = lax.broadcasted_iota(jnp.int32, (tq, tq), 1)
    bt_ref[0, BT_DIAG] = jnp.where(qry >= key, lookup(qry - key), MASKED)
    bt_ref[0, BT_PREV] = lookup(tq + qry - key)
    bt_ref[0, BT_FAR] = jnp.full((tq, tq), far, F32)
    bt_ref[0, BT_WIN_EDGE] = jnp.where(qry < key, far, MASKED)
    bt_ref[0, BT_NONE] = jnp.full((tq, tq), MASKED, F32)

    per_tile = tq // CMP_STRIDE
    pad = 16
    assert pad * CMP_STRIDE >= REL_MAX_DIST + CMP_LEN - 1 and pad % 8 == 0
    band = per_tile + pad
    cmp_end = lax.broadcasted_iota(jnp.int32, (band, tq), 0) * CMP_STRIDE + CMP_LEN - 1
    qry_c = lax.broadcasted_iota(jnp.int32, (band, tq), 1)

    def cmp_tile(i, carry):
        start = pl.multiple_of(jnp.maximum(i * per_tile - pad, 0), 8)
        bc_ref[0, i] = jnp.full((n_cmp_pad, tq), far, F32)
        bc_ref[0, i, pl.ds(start, band), :] = lookup(i * tq + qry_c - (start * CMP_STRIDE + cmp_end))
        return carry

    lax.fori_loop(0, seq // tq, cmp_tile, 0)


def _bias_tables(rel_bias, seq):
    g, hpg, tq = NSA_KV_GROUPS, NSA_HPG, NSA_TQ
    n_cmp_pad = seq // CMP_STRIDE
    nq = seq // tq
    return pl.pallas_call(
        functools.partial(_bias_body, seq=seq, n_cmp_pad=n_cmp_pad),
        out_shape=[jax.ShapeDtypeStruct((g, nq, n_cmp_pad, hpg * tq), F32),
                   jax.ShapeDtypeStruct((g, BT_COUNT, tq, hpg * tq), F32)],
        grid=(NSA_HEADS,),
        in_specs=[pl.BlockSpec(memory_space=pltpu.SMEM)],
        out_specs=[pl.BlockSpec((1, nq, n_cmp_pad, tq), lambda h: (h // hpg, 0, 0, h % hpg)),
                   pl.BlockSpec((1, BT_COUNT, tq, tq), lambda h: (h // hpg, 0, 0, h % hpg))],
        compiler_params=_params(("parallel",)),
        name="nsa_bias",
    )(rel_bias)


def _eye(n, dtype):
    return (lax.broadcasted_iota(jnp.int32, (n, n), 0)
            == lax.broadcasted_iota(jnp.int32, (n, n), 1)).astype(dtype)


def _nsa_body(q_ref, kc_ref, vct_ref, ks_ref, vst_ref, kw_ref, vwt_ref, bc_ref, bt_ref, gate_ref,
              o_ref, m_ref, acc_ref):
    tq, hpg, dh = NSA_TQ, NSA_HPG, NSA_HEAD_DIM
    rws = hpg * tq
    n_blk_log2 = SEL_BLOCK.bit_length() - 1
    n_blk = ks_ref.shape[1] // SEL_BLOCK
    n_cmp_pad = kc_ref.shape[2]
    kw = ks_ref.shape[2]
    i = pl.program_id(2)

    def per_head(x):
        return [x[:, hh * tq:(hh + 1) * tq] for hh in range(hpg)]

    xq = (q_ref[0] * (dh ** -0.5 * LOG2E)).astype(BF16)
    eye_d = _eye(dh, BF16)
    q_t = jnp.concatenate([_dot_nt(eye_d, xq[:, hh * dh:(hh + 1) * dh]) for hh in range(hpg)],
                          axis=1).astype(BF16)

    cmp_id = lax.broadcasted_iota(jnp.int32, (n_cmp_pad, rws), 0)
    t_pos = i * tq + (lax.broadcasted_iota(jnp.int32, (n_cmp_pad, rws), 1) & (tq - 1))
    valid = (t_pos - (cmp_id * CMP_STRIDE + CMP_LEN - 1) >= 0) & (cmp_id < n_cmp_pad - 1)
    s = jnp.where(valid, _dot(kc_ref[0, 0].astype(BF16), q_t) + bc_ref[0, 0], MASKED)
    e = jnp.where(valid, jnp.exp2(s - jnp.max(s, axis=0, keepdims=True)), 0.0)
    den = jnp.sum(e, axis=0, keepdims=True)
    p_c = e / jnp.where(den > 0.0, den, 1.0)
    o_cmp = _dot(vct_ref[0, 0].astype(BF16), p_c.astype(BF16))

    p_heads = per_head(p_c)
    p_sum = p_heads[0]
    for ph in p_heads[1:]:
        p_sum = p_sum + ph
    blk_o = lax.broadcasted_iota(jnp.int32, (n_blk, n_cmp_pad), 0)
    cmp_o = lax.broadcasted_iota(jnp.int32, (n_blk, n_cmp_pad), 1)
    overlap_t = ((cmp_o * CMP_STRIDE <= blk_o * SEL_BLOCK + SEL_BLOCK - 1)
                 & (cmp_o * CMP_STRIDE + CMP_LEN - 1 >= blk_o * SEL_BLOCK)).astype(F32)
    imp = _dot(overlap_t, p_sum, HI)
    jj = lax.broadcasted_iota(jnp.int32, (n_blk, tq), 0)
    cur = (i * tq + lax.broadcasted_iota(jnp.int32, (n_blk, tq), 1)) >> n_blk_log2
    forced = (jj == 0) | (jj == cur) | (jj == cur - 1)
    imp = jnp.where(jj > cur, -1e6, jnp.where(forced, 1e6, imp))
    rank = jnp.zeros((n_blk, tq), jnp.int32)
    for a in range(n_blk):
        row = imp[a:a + 1, :]
        beats = (row > imp) | ((row == imp) & (a < jj))
        rank = rank + beats.astype(jnp.int32)
    not_sel = jnp.where(rank < SEL_TOP_N, 0.0, 1.0).astype(BF16)

    q_aug = jnp.concatenate([q_t, jnp.concatenate([not_sel] * hpg, axis=1),
                             jnp.zeros((kw - dh - n_blk, rws), BF16)], axis=0)
    q_pad = jnp.concatenate([q_t, jnp.zeros((kw - dh, rws), BF16)], axis=0)


    n_win = WINDOW // tq
    tiles, scores = [], []
    for delta in range(n_win + 1):
        entry = {0: BT_DIAG, 1: BT_PREV, n_win: BT_WIN_EDGE}.get(delta, BT_FAR)
        if delta > 0:
            entry = jnp.where(i - delta >= 0, entry, BT_NONE)
        tiles.append(jnp.maximum(i - delta, 0))
        off = pl.multiple_of(tiles[-1] * tq, tq)
        scores.append(_dot(kw_ref[0, pl.ds(off, tq), :], q_pad) + bt_ref[0, entry])
    m_all = scores[0]
    for sc in scores[1:]:
        m_all = jnp.maximum(m_all, sc)
    m_w = jnp.max(m_all, axis=0, keepdims=True)
    acc_w = jnp.zeros((vwt_ref.shape[2], rws), F32)
    for kt, sc in zip(tiles, scores):
        acc_w = acc_w + _dot(vwt_ref[0, kt], jnp.exp2(sc - m_w).astype(BF16))
    o_win = acc_w[0:dh] / acc_w[dh:dh + 1]

    g_t = _dot_nt(_eye(GATE_PAD, F32), gate_ref[0], HI)
    sig = jax.nn.sigmoid(g_t[0:4 * hpg])
    gate = [jnp.concatenate([sig[br * hpg + hh:br * hpg + hh + 1] for hh in range(hpg)], axis=1)
            for br in range(3)]
    m_ref[...] = jnp.full(m_ref.shape, MASKED, F32)
    acc_ref[...] = jnp.zeros(acc_ref.shape, F32)

    def sel_update(tiles):
        scores = []
        for kt in tiles:
            off = pl.multiple_of(kt * tq, tq)
            scores.append(_dot(ks_ref[0, pl.ds(off, tq), :], q_aug)
                          + bt_ref[0, jnp.minimum(i - kt, BT_FAR)])
        s_max = scores[0]
        for sc in scores[1:]:
            s_max = jnp.maximum(s_max, sc)
        m_prev = m_ref[...]
        m_new = jnp.maximum(m_prev, jnp.max(s_max, axis=0, keepdims=True))
        acc = jnp.exp2(m_prev - m_new) * acc_ref[...]
        for kt, sc in zip(tiles, scores):
            acc = acc + _dot(vst_ref[0, kt], jnp.exp2(sc - m_new).astype(BF16))
        m_ref[...] = m_new
        acc_ref[...] = acc

    def sel_pair(j, carry):
        sel_update([2 * j, 2 * j + 1])
        return carry

    lax.fori_loop(0, (i + 1) // 2, sel_pair, 0)

    @pl.when(i % 2 == 0)
    def _():
        sel_update([i])

    o_sel = acc_ref[0:dh, :] / acc_ref[dh:dh + 1, :]

    y_t = (gate[0] * o_cmp + gate[1] * o_sel + gate[2] * o_win).astype(BF16)
    eye_q = _eye(tq, BF16)
    o_ref[0] = jnp.concatenate([_dot_nt(eye_q, yh) for yh in per_head(y_t)],
                               axis=1).astype(BF16)


def _nsa_attention(proj, kv, kv_t, kvc, kvc_t, bias_c, bias_t, *, bsz, seq):
    g, tq, hpg, dh = NSA_KV_GROUPS, NSA_TQ, NSA_HPG, NSA_HEAD_DIM
    n_rows = kvc.shape[2]
    nq = seq // tq
    group_w = hpg * dh
    kv_spec = lambda pos: pl.BlockSpec((1, seq, LANES), lambda b, gi, i: (b, 0, pos * g + gi))
    kvt_spec = lambda pos: pl.BlockSpec((1, nq, LANES, tq), lambda b, gi, i: (b, 0, pos * g + gi, 0))
    kv3 = kv.reshape(bsz, seq, KV_COLS)
    proj3 = proj.reshape(bsz, seq, PROJ_COLS)
    out = pl.pallas_call(
        _nsa_body,
        out_shape=jax.ShapeDtypeStruct((bsz, seq, NSA_DIM), BF16),
        grid=(bsz, g, nq),
        in_specs=[
            pl.BlockSpec((1, tq, group_w), lambda b, gi, i: (b, i, COL_QNSA // group_w + gi)),
            pl.BlockSpec((1, 1, n_rows, dh), lambda b, gi, i: (0, b * g + gi, 0, 0)),
            pl.BlockSpec((1, 1, dh, n_rows), lambda b, gi, i: (1, b * g + gi, 0, 0)),
            kv_spec(ROW_K_SEL), kvt_spec(T_V_SEL), kv_spec(ROW_K_WIN), kvt_spec(T_V_WIN),
            pl.BlockSpec((1, 1, n_rows, hpg * tq), lambda b, gi, i: (gi, i, 0, 0)),
            pl.BlockSpec((1, BT_COUNT, tq, hpg * tq), lambda b, gi, i: (gi, 0, 0, 0)),
            pl.BlockSpec((1, tq, GATE_PAD), lambda b, gi, i: (b, i, COL_GNSA // GATE_PAD + gi)),
        ],
        out_specs=pl.BlockSpec((1, tq, group_w), lambda b, gi, i: (b, i, gi)),
        scratch_shapes=[pltpu.VMEM((1, hpg * tq), F32),
                        pltpu.VMEM((LANES, hpg * tq), F32)],
        compiler_params=_params(("parallel", "parallel", "arbitrary")),
        name="nsa_attention",
    )(proj3, kvc, kvc_t, kv3, kv_t, kv3, kv_t, bias_c, bias_t, proj3)
    return out.reshape(bsz * seq, NSA_DIM)


def _mem_body(q_ref, kv_ref, o_ref):
    outs = []
    for h in range(MEM_HEADS):
        sl = slice(h * MEM_HEAD_DIM, (h + 1) * MEM_HEAD_DIM)
        qh = (q_ref[:, sl] * (MEM_HEAD_DIM ** -0.5)).astype(BF16)
        kh = kv_ref[0, :, sl].astype(BF16)
        vh = kv_ref[0, :, MEM_DIM + h * MEM_HEAD_DIM:MEM_DIM + (h + 1) * MEM_HEAD_DIM].astype(BF16)
        s = _dot_nt(qh, kh)
        e = jnp.exp(s - jnp.max(s, axis=-1, keepdims=True))
        p = e / jnp.sum(e, axis=-1, keepdims=True)
        outs.append(_dot(p.astype(BF16), vh))
    o_ref[...] = jnp.concatenate(outs, axis=1)


def _mem_attention(proj, kv, *, seq, tq=512):
    t = proj.shape[0]
    bsz, m, _ = kv.shape
    per_seq = seq // tq
    return pl.pallas_call(
        _mem_body,
        out_shape=jax.ShapeDtypeStruct((t, MEM_DIM), F32),
        grid=(t // tq,),
        in_specs=[
            pl.BlockSpec((tq, MEM_DIM), lambda i: (i, COL_QMEM // MEM_DIM)),
            pl.BlockSpec((1, m, 2 * MEM_DIM), lambda i: (i // per_seq, 0, 0)),
        ],
        out_specs=pl.BlockSpec((tq, MEM_DIM), lambda i: (i, 0)),
        compiler_params=_params(("parallel",)),
        name="mem_attention",
    )(proj, kv)


def _merge_body(x_ref, gain_ref, yr_ref, yn_ref, ym_ref, wg_ref, wr_ref, wn_ref, wm_ref, wo_ref,
                o_ref):
    d = x_ref.shape[1]
    x = x_ref[...]
    gates = jax.nn.sigmoid(_dot(_rms(x, gain_ref[...]).astype(BF16), wg_ref[...]))
    merged = (gates[:, 0:d] * _dot(yr_ref[...].astype(BF16), wr_ref[...])
              + gates[:, d:2 * d] * _dot(yn_ref[...], wn_ref[...])
              + gates[:, 2 * d:3 * d] * _dot(ym_ref[...].astype(BF16), wm_ref[...]))
    o_ref[...] = x + _dot(merged.astype(BF16), wo_ref[...])


def _merge(x, gain, y_rwkv, y_nsa, y_mem, w_g, w_r, w_n, w_m, w_o, *, tm=512):
    t, d = x.shape
    row = lambda i: (i, 0)
    const = lambda i: (0, 0)
    return pl.pallas_call(
        _merge_body,
        out_shape=jax.ShapeDtypeStruct((t, d), F32),
        grid=(t // tm,),
        in_specs=[
            pl.BlockSpec((tm, d), row),
            pl.BlockSpec((1, d), const),
            pl.BlockSpec((tm, RWKV_DIM), row),
            pl.BlockSpec((tm, NSA_DIM), row),
            pl.BlockSpec((tm, MEM_DIM), row),
            pl.BlockSpec((d, N_BRANCH * d), const),
            pl.BlockSpec((RWKV_DIM, d), const),
            pl.BlockSpec((NSA_DIM, d), const),
            pl.BlockSpec((MEM_DIM, d), const),
            pl.BlockSpec((d, d), const),
        ],
        out_specs=pl.BlockSpec((tm, d), row),
        compiler_params=_params(("parallel",)),
        name="merge",
    )(x, gain, y_rwkv, y_nsa, y_mem, w_g, w_r, w_n, w_m, w_o)


def _row(a):
    return a.reshape(1, -1)


def _in_proj(x, p, l, seq):
    d = x.shape[1]
    g, hpg, dh = NSA_KV_GROUPS, NSA_HPG, NSA_HEAD_DIM
    w_in = p['w_in'][l]
    o = 0
    parts = {}
    for name, size in (('rwkv', RWKV_PROJ), ('q', NSA_DIM), ('kv', KV_KINDS * NSA_KV_DIM),
                       ('g_nsa', 3 * NSA_HEADS), ('q_mem', MEM_DIM)):
        parts[name] = w_in[:, o:o + size]
        o += size
    gates = parts['g_nsa'].reshape(d, 3, g, hpg).transpose(0, 2, 1, 3).reshape(d, g, 3 * hpg)
    gates = jnp.pad(gates, ((0, 0), (0, 0), (0, GATE_PAD - 3 * hpg))).reshape(d, g * GATE_PAD)
    w_other = jnp.concatenate([parts['q'], parts['q_mem'], gates], axis=1).astype(BF16)
    w_kv = jnp.pad(parts['kv'].reshape(d, KV_KINDS, g, dh), ((0, 0), (0, 0), (0, 0), (0, LANES - dh)))
    w_row = w_kv[:, KV_ROW_KINDS, :, :].reshape(d, KV_COLS).astype(BF16)
    w_t = w_kv[:, KV_T_KINDS, :, :].reshape(d, KV_T_ROWS).T.astype(BF16)
    w_lora = jnp.zeros((LORA_DIM, 3 * RWKV_DIM), F32)
    w_lora = w_lora.at[0:DECAY_LORA, 0:RWKV_DIM].set(p['rwkv_w2'][l])
    w_lora = w_lora.at[DECAY_LORA:DECAY_LORA + AAA_LORA, RWKV_DIM:2 * RWKV_DIM].set(p['rwkv_a2'][l])
    w_lora = w_lora.at[DECAY_LORA + AAA_LORA:, 2 * RWKV_DIM:].set(p['rwkv_g2'][l])
    proj, kv, kv_t, *rwkv_in = _in_proj_call(
        x, seq, _row(p['mix_norm'][l]), parts['rwkv'].astype(BF16), w_other, w_row, w_t,
        _row(p['rwkv_mu'][l]), w_lora.astype(BF16), _row(p['rwkv_w0'][l]), _row(p['rwkv_a0'][l]),
        _row(p['rwkv_k_k'][l]), _row(p['rwkv_k_a'][l]))
    return proj, kv, kv_t, rwkv_in


def _rwkv_branch(rwkv_in, p, l, bsz, seq):
    return _rwkv_scan(*rwkv_in, _row(p['rwkv_r_k'][l]), _row(p['rwkv_gn_gain'][l]),
                      _row(p['rwkv_gn_bias'][l]), batch=bsz, seq=seq)


def _nsa_branch(proj, kv, kv_t, bias_c, bias_t, p, l, bsz, seq):
    w1 = jnp.stack([p['cmp_k_w1'][l], p['cmp_v_w1'][l]]).astype(BF16)
    pe = jnp.stack([p['cmp_pe_k'][l].reshape(1, -1), p['cmp_pe_v'][l].reshape(1, -1)])
    w2 = jnp.stack([p['cmp_k_w2'][l], p['cmp_v_w2'][l]]).astype(BF16)
    kvc, kvc_t = _compress(kv, w1, pe, w2, bsz=bsz, seq=seq)
    return _nsa_attention(proj, kv, kv_t, kvc, kvc_t, bias_c, bias_t, bsz=bsz, seq=seq)


def _mem_branch(proj, mem, p, l):
    bsz, m_tok, d = mem.shape
    w_kv = jnp.concatenate([p['mem_w_k'][l], p['mem_w_v'][l]], axis=1).astype(BF16)
    kv_mem = _norm_matmul(mem.reshape(bsz * m_tok, d), _row(p['mem_norm'][l]), w_kv,
                          tm=min(1024, bsz * m_tok), tn=512, name="mem_kv")
    return _mem_attention(proj, kv_mem.reshape(bsz, m_tok, 2 * MEM_DIM),
                          seq=proj.shape[0] // bsz)


def _layer(x, mem, l, bias_c, bias_t, p):
    bsz, seq, d = x.shape
    t = bsz * seq
    row = _row
    x = x.reshape(t, d)

    x = _ffn(x, row(p['ffn1_norm'][l]), p['ffn1_w_gate'][l].astype(BF16),
             p['ffn1_w_up'][l].astype(BF16), p['ffn1_w_down'][l].astype(BF16),
             row(p['final_norm']), final=False)

    proj, kv, kv_t, rwkv_in = _in_proj(x, p, l, seq)
    y_rwkv = _rwkv_branch(rwkv_in, p, l, bsz, seq)
    y_nsa = _nsa_branch(proj, kv, kv_t, bias_c, bias_t, p, l, bsz, seq)
    y_mem = _mem_branch(proj, mem, p, l)

    w_gate = p['w_in'][l][:, -N_BRANCH * d:].astype(BF16)
    x = _merge(x, row(p['mix_norm'][l]), y_rwkv, y_nsa, y_mem, w_gate,
               p['w_br_rwkv'][l].astype(BF16), p['w_br_nsa'][l].astype(BF16),
               p['w_br_mem'][l].astype(BF16), p['w_out'][l].astype(BF16))

    last = l == p['ffn1_norm'].shape[0] - 1
    x = _ffn(x, row(p['ffn2_norm'][l]), p['ffn2_w_gate'][l].astype(BF16),
             p['ffn2_w_up'][l].astype(BF16), p['ffn2_w_down'][l].astype(BF16),
             row(p['final_norm']), final=last)
    return x.reshape(bsz, seq, d)


def kernel(x, mem, ffn1_norm, ffn1_w_gate, ffn1_w_up, ffn1_w_down, mix_norm, w_in, rwkv_mu, rwkv_w0, rwkv_w2, rwkv_a0, rwkv_a2, rwkv_g2, rwkv_k_k, rwkv_k_a, rwkv_r_k, rwkv_gn_gain, rwkv_gn_bias, cmp_pe_k, cmp_k_w1, cmp_k_w2, cmp_pe_v, cmp_v_w1, cmp_v_w2, rel_bias, mem_norm, mem_w_k, mem_w_v, w_br_rwkv, w_br_nsa, w_br_mem, w_out, ffn2_norm, ffn2_w_gate, ffn2_w_up, ffn2_w_down, final_norm):
    p = dict(ffn1_norm=ffn1_norm, ffn1_w_gate=ffn1_w_gate, ffn1_w_up=ffn1_w_up,
             ffn1_w_down=ffn1_w_down, mix_norm=mix_norm, w_in=w_in, rwkv_mu=rwkv_mu,
             rwkv_w0=rwkv_w0, rwkv_w2=rwkv_w2, rwkv_a0=rwkv_a0, rwkv_a2=rwkv_a2, rwkv_g2=rwkv_g2,
             rwkv_k_k=rwkv_k_k, rwkv_k_a=rwkv_k_a, rwkv_r_k=rwkv_r_k, rwkv_gn_gain=rwkv_gn_gain,
             rwkv_gn_bias=rwkv_gn_bias, cmp_pe_k=cmp_pe_k, cmp_k_w1=cmp_k_w1, cmp_k_w2=cmp_k_w2,
             cmp_pe_v=cmp_pe_v, cmp_v_w1=cmp_v_w1, cmp_v_w2=cmp_v_w2, mem_norm=mem_norm,
             mem_w_k=mem_w_k, mem_w_v=mem_w_v, w_br_rwkv=w_br_rwkv, w_br_nsa=w_br_nsa,
             w_br_mem=w_br_mem, w_out=w_out, ffn2_norm=ffn2_norm, ffn2_w_gate=ffn2_w_gate,
             ffn2_w_up=ffn2_w_up, ffn2_w_down=ffn2_w_down, final_norm=final_norm)
    bias_c, bias_t = _bias_tables(rel_bias, x.shape[1])
    for l in range(ffn1_norm.shape[0]):
        x = _layer(x, mem, l, bias_c, bias_t, p)
    return x
```

```python
import functools
import math

import jax
import jax.numpy as jnp
from jax import lax
from jax.experimental import pallas as pl
from jax.experimental.pallas import tpu as pltpu

F32 = jnp.float32
BF16 = jnp.bfloat16
HI = lax.Precision.HIGHEST

D_MODEL = 1024
NORM_EPS = 1e-6
D_FF = 2816
RWKV_HEADS = 8
RWKV_HEAD_DIM = 64
RWKV_DIM = RWKV_HEADS * RWKV_HEAD_DIM
DECAY_LORA = 64
AAA_LORA = 64
GATE_LORA = 128
LORA_DIM = DECAY_LORA + AAA_LORA + GATE_LORA
RWKV_GN_EPS = 64e-5
RWKV_PROJ = 3 * RWKV_DIM + LORA_DIM
NSA_HEADS = 8
NSA_KV_GROUPS = 2
NSA_HPG = NSA_HEADS // NSA_KV_GROUPS
NSA_HEAD_DIM = 64
NSA_DIM = NSA_HEADS * NSA_HEAD_DIM
NSA_KV_DIM = NSA_KV_GROUPS * NSA_HEAD_DIM
CMP_LEN = 32
CMP_STRIDE = 16
CMP_HIDDEN = 256
SEL_BLOCK = 64
SEL_TOP_N = 16
WINDOW = 512
REL_BUCKETS = 32
REL_MAX_DIST = 128
MEM_HEADS = 4
MEM_HEAD_DIM = 128
MEM_DIM = MEM_HEADS * MEM_HEAD_DIM
N_BRANCH = 3

LANES = 128
GATE_PAD = LANES
COL_QNSA = 0
COL_QMEM = COL_QNSA + NSA_DIM
COL_GNSA = COL_QMEM + MEM_DIM
PROJ_COLS = COL_GNSA + NSA_KV_GROUPS * GATE_PAD
assert COL_QMEM % MEM_DIM == 0 and COL_GNSA % GATE_PAD == 0
KV_KINDS = 6
KV_ROW_KINDS = (0, 1, 2, 4)
KV_T_KINDS = (3, 5)
KV_COLS = len(KV_ROW_KINDS) * NSA_KV_GROUPS * LANES
KV_T_ROWS = len(KV_T_KINDS) * NSA_KV_GROUPS * LANES
ROW_K_SEL, ROW_K_WIN = KV_ROW_KINDS.index(2), KV_ROW_KINDS.index(4)
T_V_SEL, T_V_WIN = KV_T_KINDS.index(3), KV_T_KINDS.index(5)
LOG2E = 1.4426950408889634

RWKV_CHUNK = 64
RWKV_INV_BLOCK = 16
NSA_TQ = 256
MASKED = -1e30
BT_DIAG, BT_PREV, BT_FAR, BT_WIN_EDGE, BT_NONE, BT_COUNT = 0, 1, 2, 3, 4, 5
VMEM_LIMIT = 56 * 1024 * 1024


def _dot(a, b, precision=None):
    return jnp.dot(a, b, preferred_element_type=F32, precision=precision)


def _dot_nt(a, b, precision=None):
    return lax.dot_general(a, b, (((1,), (1,)), ((), ())), preferred_element_type=F32,
                           precision=precision)


def _params(semantics):
    return pltpu.CompilerParams(dimension_semantics=semantics, vmem_limit_bytes=VMEM_LIMIT)


def _rms(x, g):
    return x * lax.rsqrt(jnp.mean(x * x, axis=-1, keepdims=True) + NORM_EPS) * g


def _ffn_body(x_ref, g_ref, wg_ref, wu_ref, wd_ref, fg_ref, o_ref, *, tf, final):
    x = x_ref[...]
    h = _rms(x, g_ref[...]).astype(BF16)
    acc = jnp.zeros(x.shape, F32)
    for j in range(wg_ref.shape[1] // tf):
        cols = slice(j * tf, (j + 1) * tf)
        act = (jax.nn.silu(_dot(h, wg_ref[:, cols])) * _dot(h, wu_ref[:, cols])).astype(BF16)
        acc = acc + _dot(act, wd_ref[cols, :])
    y = x + 0.5 * acc
    if final:
        y = _rms(y, fg_ref[...])
    o_ref[...] = y


def _ffn(x, gain, wg, wu, wd, final_gain, *, final, tm=1024, tf=256):
    t, d = x.shape
    f = wg.shape[1]
    resident = lambda shape: pl.BlockSpec(shape, lambda i: (0, 0), pipeline_mode=pl.Buffered(1))
    return pl.pallas_call(
        functools.partial(_ffn_body, tf=tf, final=final),
        out_shape=jax.ShapeDtypeStruct((t, d), F32),
        grid=(t // tm,),
        in_specs=[
            pl.BlockSpec((tm, d), lambda i: (i, 0)),
            pl.BlockSpec((1, d), lambda i: (0, 0)),
            resident((d, f)),
            resident((d, f)),
            resident((f, d)),
            pl.BlockSpec((1, d), lambda i: (0, 0)),
        ],
        out_specs=pl.BlockSpec((tm, d), lambda i: (i, 0)),
        compiler_params=_params(("parallel",)),
        name="ffn_final" if final else "ffn",
    )(x, gain, wg, wu, wd, final_gain)


def _write_kv(h, w_ref, wt_ref, o_ref, ot_ref, *, seq):
    dh, tk, tm = NSA_HEAD_DIM, NSA_TQ, h.shape[0]
    y = _dot(h, w_ref[...])
    row = lax.broadcasted_iota(jnp.int32, (tm, LANES), 0)
    lane = lax.broadcasted_iota(jnp.int32, (tm, LANES), 1)
    pos = (pl.program_id(0) * tm) % seq + row
    block_mark = jnp.where(lane - dh == (pos >> (SEL_BLOCK.bit_length() - 1)), MASKED, 0.0)
    for tile in range(KV_COLS // LANES):
        part = y[:, tile * LANES:(tile + 1) * LANES]
        if tile // NSA_KV_GROUPS == ROW_K_SEL:
            part = part + block_mark
        o_ref[:, tile * LANES:(tile + 1) * LANES] = part.astype(BF16)

    y_t = _dot_nt(wt_ref[...], h)
    row_t = lax.broadcasted_iota(jnp.int32, y_t.shape, 0)
    y_t = (y_t + jnp.where((row_t & (LANES - 1)) == dh, 1.0, 0.0)).astype(BF16)
    for c in range(tm // tk):
        ot_ref[0, c] = y_t[:, c * tk:(c + 1) * tk]


def _in_proj_body(x_ref, xp_ref, g_ref, wr_ref, wo_ref, wkv_ref, wkvt_ref, mu_ref, wl_ref, w0_ref,
                  a0_ref, kk_ref, ka_ref, proj_o, kv_o, kvt_o, r_o, k_o, v_o, kk_o, b_o, lw_o, g_o,
                  *, tiles_per_seq, seq):
    i = pl.program_id(0)
    gain = g_ref[...]
    h = _rms(x_ref[...], gain).astype(BF16)
    proj_o[...] = _dot(h, wo_ref[...])
    _write_kv(h, wkv_ref, wkvt_ref, kv_o, kvt_o, seq=seq)
    n_up = xp_ref.shape[0]
    h_up = _rms(xp_ref[...], gain).astype(BF16)
    p_all = _dot(jnp.concatenate([h_up, h], axis=0), wr_ref[...])
    p = p_all[n_up:]
    keep = jnp.where(i % tiles_per_seq == 0, 0.0, 1.0)
    prev_last = p_all[n_up - 1:n_up, :] * keep
    rows = lax.broadcasted_iota(jnp.int32, p.shape, 0)
    shifted = jnp.where(rows == 0, prev_last, pltpu.roll(p, 1, 0))
    x = p + (shifted - p) * mu_ref[...]

    r = x[:, 0:RWKV_DIM]
    k = x[:, RWKV_DIM:2 * RWKV_DIM]
    v = x[:, 2 * RWKV_DIM:3 * RWKV_DIM]
    s = x[:, 3 * RWKV_DIM:RWKV_PROJ]
    lane = lax.broadcasted_iota(jnp.int32, s.shape, 1)
    z = jnp.where(lane < DECAY_LORA, jnp.tanh(s),
                  jnp.where(lane < DECAY_LORA + AAA_LORA, s, jax.nn.sigmoid(s)))
    lo = _dot(z.astype(BF16), wl_ref[...])
    a = jax.nn.sigmoid(a0_ref[...] + lo[:, RWKV_DIM:2 * RWKV_DIM])

    kkr = k * kk_ref[...]
    sq = kkr * kkr
    sq_hi = sq.astype(BF16)
    sq_lo = (sq - sq_hi.astype(F32)).astype(BF16)
    shift = RWKV_HEAD_DIM.bit_length() - 1
    same_head = ((lax.broadcasted_iota(jnp.int32, (RWKV_DIM, RWKV_DIM), 0) >> shift)
                 == (lax.broadcasted_iota(jnp.int32, (RWKV_DIM, RWKV_DIM), 1) >> shift)).astype(BF16)
    ssq = _dot(sq_hi, same_head) + _dot(sq_lo, same_head)
    kk = kkr / jnp.maximum(jnp.sqrt(ssq), 1e-12)

    r_o[...] = r
    k_o[...] = k * (1.0 + (a - 1.0) * ka_ref[...])
    v_o[...] = v
    kk_o[...] = kk
    b_o[...] = kk * a
    lw_o[...] = -math.exp(-0.5) * jax.nn.sigmoid(w0_ref[...] + lo[:, 0:RWKV_DIM])
    g_o[...] = lo[:, 2 * RWKV_DIM:3 * RWKV_DIM]


def _in_proj_call(x, seq, gain, w_rwkv, w_other, w_kv, w_kv_t, mu, w_lora, w0, a0, k_k, k_a, *,
                  tm=512):
    t, d = x.shape
    tk = NSA_TQ
    per_seq = seq // tm
    row = lambda i: (i, 0)
    const = lambda i: (0, 0)
    resident = lambda shape: pl.BlockSpec(shape, const, pipeline_mode=pl.Buffered(1))
    vec = pl.BlockSpec((1, RWKV_DIM), const)
    tok = jax.ShapeDtypeStruct((t, RWKV_DIM), F32)
    return pl.pallas_call(
        functools.partial(_in_proj_body, tiles_per_seq=per_seq, seq=seq),
        out_shape=[jax.ShapeDtypeStruct((t, PROJ_COLS), F32),
                   jax.ShapeDtypeStruct((t, KV_COLS), BF16),
                   jax.ShapeDtypeStruct((t // seq, seq // tk, KV_T_ROWS, tk), BF16)] + [tok] * 7,
        grid=(t // tm,),
        in_specs=[
            pl.BlockSpec((tm, d), row),
            pl.BlockSpec((16, d), lambda i: (jnp.maximum(i * (tm // 16) - 1, 0), 0)),
            pl.BlockSpec((1, d), const),
            resident((d, RWKV_PROJ)),
            resident((d, PROJ_COLS)),
            resident((d, KV_COLS)),
            resident((KV_T_ROWS, d)),
            pl.BlockSpec((1, RWKV_PROJ), const),
            resident((LORA_DIM, 3 * RWKV_DIM)),
            vec, vec, vec, vec,
        ],
        out_specs=[pl.BlockSpec((tm, PROJ_COLS), row),
                   pl.BlockSpec((tm, KV_COLS), row),
                   pl.BlockSpec((1, tm // tk, KV_T_ROWS, tk),
                                lambda i: (i // per_seq, i % per_seq, 0, 0))]
        + [pl.BlockSpec((tm, RWKV_DIM), row)] * 7,
        compiler_params=_params(("parallel",)),
        name="in_proj",
    )(x, x, gain, w_rwkv, w_other, w_kv, w_kv_t, mu, w_lora, w0, a0, k_k, k_a)


def _rwkv_scan_body(r_ref, k_ref, v_ref, kk_ref, b_ref, lw_ref, g_ref, rk_ref, gg_ref, gb_ref,
                    o_ref, st_ref):
    c_sz, n, nh = RWKV_CHUNK, RWKV_HEAD_DIM, RWKV_HEADS

    @pl.when(pl.program_id(1) == 0)
    def _():
        st_ref[...] = jnp.zeros_like(st_ref)

    ri = lax.broadcasted_iota(jnp.int32, (c_sz, c_sz), 0)
    ci = lax.broadcasted_iota(jnp.int32, (c_sz, c_sz), 1)
    incl = ci <= ri
    eye_b = (ci == ri).astype(BF16)
    row2 = lax.broadcasted_iota(jnp.int32, (c_sz, 2 * c_sz), 0)
    lane2 = lax.broadcasted_iota(jnp.int32, (c_sz, 2 * c_sz), 1)
    right_half = lane2 >= c_sz
    zeros_b = jnp.zeros((c_sz, n), BF16)
    inv_shift = RWKV_INV_BLOCK.bit_length() - 1

    rows = []
    for bb in range(st_ref.shape[0]):
        lw = lw_ref[bb]
        cum = _dot(incl.astype(F32), lw, HI)
        cum_last = cum[c_sz - 1:c_sz, :]
        r, k, v, b = r_ref[bb], k_ref[bb], v_ref[bb], b_ref[bb]
        p_inv = jnp.exp(-cum)
        p_end = jnp.exp(cum_last - cum)
        rows.append(dict(
            left=jnp.concatenate([(-(kk_ref[bb] * jnp.exp(cum - lw))).astype(BF16),
                                  (r * jnp.exp(cum)).astype(BF16)], axis=0),
            bt=(b * p_inv).astype(BF16), kt=(k * p_inv).astype(BF16),
            bh=(b * p_end).astype(BF16), kh=(k * p_end).astype(BF16),
            v=v, v_b=v.astype(BF16), d_p=jnp.exp(cum_last), rk=r * k * rk_ref[...]))

    units = [(bb, h) for bb in range(len(rows)) for h in range(nh)]
    col = lambda name, u: rows[u[0]][name][:, u[1] * n:(u[1] + 1) * n]
    a_all = [_dot_nt(col('left', u), jnp.concatenate([col('bt', u), col('kt', u)], axis=0))
             for u in units]
    key2 = jnp.where(right_half, lane2 - c_sz, lane2)
    w_u = [jnp.where(right_half & (key2 < row2), a[:c_sz], 0.0).astype(BF16) for a in a_all]
    w_y = [jnp.where(key2 <= row2, a[c_sz:], 0.0).astype(BF16) for a in a_all]

    same_block = (row2 >> inv_shift) == (key2 >> inv_shift)
    x = [jnp.where((lane2 < row2) & same_block, a[:c_sz], jnp.where(lane2 == row2 + c_sz, 1.0, 0.0))
         for a in a_all]
    for _ in range(inv_shift):
        hi = [xu.astype(BF16) for xu in x]
        lo = [(xu - h_.astype(F32)).astype(BF16) for xu, h_ in zip(x, hi)]
        x = [_dot(h_[:, :c_sz], h_) + _dot(h_[:, :c_sz], l_) + _dot(l_[:, :c_sz], h_)
             + jnp.where(right_half, xu, 0.0) for xu, h_, l_ in zip(x, hi, lo)]
    x_b = [xu.astype(BF16) for xu in x]
    solve_diag = lambda j, z: _dot(x_b[j], jnp.concatenate([zeros_b, z.astype(BF16)], axis=0))
    q_b = [solve_diag(j, jnp.where((ci < ri) & ((ri >> inv_shift) != (ci >> inv_shift)),
                                   a[:c_sz, :c_sz], 0.0)).astype(BF16)
           for j, a in enumerate(a_all)]

    s0 = [st_ref[bb, h] for bb, h in units]
    ls0 = [_dot_nt(col('left', u), s0[j].astype(BF16)) for j, u in enumerate(units)]
    rhs = [ls0[j][:c_sz] + _dot(w_u[j], jnp.concatenate([zeros_b, col('v_b', u)], axis=0))
           for j, u in enumerate(units)]
    g0 = [solve_diag(j, rhs[j]) for j in range(len(units))]
    u_f = g0
    for _ in range(c_sz // RWKV_INV_BLOCK - 1):
        u_f = [g0[j] + _dot(q_b[j], u_f[j].astype(BF16)) for j in range(len(units))]
    u_b = [uj.astype(BF16) for uj in u_f]
    uv = [jnp.concatenate([u_b[j], col('v_b', u)], axis=0) for j, u in enumerate(units)]
    y = [ls0[j][c_sz:] + _dot(w_y[j], uv[j]) for j in range(len(units))]
    uv_t = [_dot_nt(eye_b, uv_j).astype(BF16) for uv_j in uv]
    for j, u in enumerate(units):
        st_ref[u[0], u[1]] = (s0[j] * col('d_p', u)
                              + _dot(uv_t[j], jnp.concatenate([col('bh', u), col('kh', u)], axis=0)))

    for bb in range(len(rows)):
        outs = []
        for h in range(nh):
            sl = slice(h * n, (h + 1) * n)
            yh = y[bb * nh + h]
            mean = jnp.mean(yh, axis=-1, keepdims=True)
            var = jnp.mean(jnp.square(yh - mean), axis=-1, keepdims=True)
            yn = (yh - mean) * lax.rsqrt(var + RWKV_GN_EPS)
            yn = yn * gg_ref[:, sl] + gb_ref[:, sl]
            bonus = jnp.sum(rows[bb]['rk'][:, sl], axis=-1, keepdims=True) * rows[bb]['v'][:, sl]
            outs.append((yn + bonus) * g_ref[bb, :, sl])
        o_ref[bb] = jnp.concatenate(outs, axis=1)


def _rwkv_scan(r, k, v, kk, b, lw, g, r_k, gn_gain, gn_bias, *, batch, seq, nb=4):
    t = r.shape[0]
    nc = seq // RWKV_CHUNK
    tok = pl.BlockSpec((nb, RWKV_CHUNK, RWKV_DIM), lambda bi, c: (bi, c, 0))
    par = pl.BlockSpec((1, RWKV_DIM), lambda bi, c: (0, 0))
    per_batch = lambda a: a.reshape(batch, seq, RWKV_DIM)
    out = pl.pallas_call(
        _rwkv_scan_body,
        out_shape=jax.ShapeDtypeStruct((batch, seq, RWKV_DIM), F32),
        grid=(batch // nb, nc),
        in_specs=[tok] * 7 + [par] * 3,
        out_specs=tok,
        scratch_shapes=[pltpu.VMEM((nb, RWKV_HEADS, RWKV_HEAD_DIM, RWKV_HEAD_DIM), F32)],
        compiler_params=_params(("parallel", "arbitrary")),
        name="rwkv_scan",
    )(*(per_batch(a) for a in (r, k, v, kk, b, lw, g)), r_k, gn_gain, gn_bias)
    return out.reshape(t, RWKV_DIM)


def _compress_body(x_ref, w1s_ref, w1_ref, pe_ref, w2_ref, w2t_ref, o_ref, ot_ref):
    n_rows = x_ref.shape[1]
    for kind in range(2):
        pe = jnp.broadcast_to(pe_ref[kind], (8, pe_ref.shape[2])).astype(BF16)
        pe_term = _dot(pe, w1_ref[kind])[0:1, :]
        for gi in range(NSA_KV_GROUPS):
            tile = kind * NSA_KV_GROUPS + gi
            both = jnp.zeros((n_rows, 2 * CMP_HIDDEN), F32)
            for l in range(CMP_STRIDE):
                both = both + _dot(x_ref[0, :, l, tile * LANES:(tile + 1) * LANES], w1s_ref[kind, l])
            second_next = pltpu.roll(both[:, CMP_HIDDEN:], n_rows - 1, 0)
            hid = both[:, :CMP_HIDDEN] + second_next + pe_term
            act = jax.nn.gelu(hid).astype(BF16)
            o_ref[kind, gi] = _dot(act, w2_ref[kind])
            ot_ref[kind, gi] = _dot_nt(w2t_ref[kind], act)


def _compress(kv, w1, pe, w2, *, bsz, seq):
    g, dh = NSA_KV_GROUPS, NSA_HEAD_DIM
    rows = seq // CMP_STRIDE
    w1r = jnp.pad(w1.reshape(2, CMP_LEN, dh, CMP_HIDDEN), ((0, 0), (0, 0), (0, LANES - dh), (0, 0)))
    w1s = jnp.concatenate([w1r[:, :CMP_STRIDE], w1r[:, CMP_STRIDE:]], axis=3)
    whole = lambda a: pl.BlockSpec(a.shape, lambda b: (0,) * a.ndim)
    w2t = w2.transpose(0, 2, 1)
    return pl.pallas_call(
        _compress_body,
        out_shape=[jax.ShapeDtypeStruct((2, bsz * g, rows, dh), F32),
                   jax.ShapeDtypeStruct((2, bsz * g, dh, rows), F32)],
        grid=(bsz,),
        in_specs=[pl.BlockSpec((1, rows, CMP_STRIDE, 2 * g * LANES), lambda b: (b, 0, 0, 0)),
                  whole(w1s), whole(w1), whole(pe), whole(w2), whole(w2t)],
        out_specs=[pl.BlockSpec((2, g, rows, dh), lambda b: (0, b, 0, 0)),
                   pl.BlockSpec((2, g, dh, rows), lambda b: (0, b, 0, 0))],
        compiler_params=_params(("parallel",)),
        name="nsa_compress",
    )(kv.reshape(bsz, rows, CMP_STRIDE, KV_COLS), w1s, w1, pe, w2, w2t)


def _t5_bucket(dist):
    n = jnp.maximum(dist, 0)
    exact = REL_BUCKETS // 2
    nf = jnp.maximum(n, 1).astype(F32)
    scaled = jnp.log(nf / exact) / math.log(REL_MAX_DIST / exact) * (REL_BUCKETS - exact)
    large = exact + jnp.floor(scaled).astype(jnp.int32)
    large = jnp.minimum(large, REL_BUCKETS - 1)
    return jnp.where(n < exact, n, large)


def _bias_body(tab_ref, bc_ref, bt_ref, *, seq, n_cmp_pad):
    h = pl.program_id(0)
    tq = NSA_TQ

    def lookup(dist):
        bucket = _t5_bucket(dist)
        out = jnp.zeros(dist.shape, F32)
        for bkt in range(REL_BUCKETS):
            out = jnp.where(bucket == bkt, tab_ref[bkt, h] * LOG2E, out)
        return out

    far = tab_ref[REL_BUCKETS - 1, h] * LOG2E
    assert tq + 1 >= REL_MAX_DIST and WINDOW - tq + 1 >= REL_MAX_DIST

    key = lax.broadcasted_iota(jnp.int32, (tq, tq), 0)
    qry = lax.broadcasted_iota(jnp.int32, (tq, tq), 1)
    bt_ref[0, BT_DIAG] = jnp.where(qry >= key, lookup(qry - key), MASKED)
    bt_ref[0, BT_PREV] = lookup(tq + qry - key)
    bt_ref[0, BT_FAR] = jnp.full((tq, tq), far, F32)
    bt_ref[0, BT_WIN_EDGE] = jnp.where(qry < key, far, MASKED)
    bt_ref[0, BT_NONE] = jnp.full((tq, tq), MASKED, F32)

    per_tile = tq // CMP_STRIDE
    pad = 16
    assert pad * CMP_STRIDE >= REL_MAX_DIST + CMP_LEN - 1 and pad % 8 == 0
    band = per_tile + pad
    cmp_end = lax.broadcasted_iota(jnp.int32, (band, tq), 0) * CMP_STRIDE + CMP_LEN - 1
    qry_c = lax.broadcasted_iota(jnp.int32, (band, tq), 1)

    def cmp_tile(i, carry):
        start = pl.multiple_of(jnp.maximum(i * per_tile - pad, 0), 8)
        bc_ref[0, i] = jnp.full((n_cmp_pad, tq), far, F32)
        bc_ref[0, i, pl.ds(start, band), :] = lookup(i * tq + qry_c - (start * CMP_STRIDE + cmp_end))
        return carry

    lax.fori_loop(0, seq // tq, cmp_tile, 0)


def _bias_tables(rel_bias, seq):
    g, hpg, tq = NSA_KV_GROUPS, NSA_HPG, NSA_TQ
    n_cmp_pad = seq // CMP_STRIDE
    nq = seq // tq
    return pl.pallas_call(
        functools.partial(_bias_body, seq=seq, n_cmp_pad=n_cmp_pad),
        out_shape=[jax.ShapeDtypeStruct((g, nq, n_cmp_pad, hpg * tq), F32),
                   jax.ShapeDtypeStruct((g, BT_COUNT, tq, hpg * tq), F32)],
        grid=(NSA_HEADS,),
        in_specs=[pl.BlockSpec(memory_space=pltpu.SMEM)],
        out_specs=[pl.BlockSpec((1, nq, n_cmp_pad, tq), lambda h: (h // hpg, 0, 0, h % hpg)),
                   pl.BlockSpec((1, BT_COUNT, tq, tq), lambda h: (h // hpg, 0, 0, h % hpg))],
        compiler_params=_params(("parallel",)),
        name="nsa_bias",
    )(rel_bias)


def _eye(n, dtype):
    return (lax.broadcasted_iota(jnp.int32, (n, n), 0)
            == lax.broadcasted_iota(jnp.int32, (n, n), 1)).astype(dtype)


def _nsa_body(q_ref, kc_ref, vct_ref, ks_ref, vst_ref, kw_ref, vwt_ref, bc_ref, bt_ref, gate_ref,
              o_ref, m_ref, acc_ref):
    tq, hpg, dh = NSA_TQ, NSA_HPG, NSA_HEAD_DIM
    rws = hpg * tq
    n_blk_log2 = SEL_BLOCK.bit_length() - 1
    n_blk = ks_ref.shape[1] // SEL_BLOCK
    n_cmp_pad = kc_ref.shape[2]
    kw = ks_ref.shape[2]
    i = pl.program_id(2)

    def per_head(x):
        return [x[:, hh * tq:(hh + 1) * tq] for hh in range(hpg)]

    xq = (q_ref[0] * (dh ** -0.5 * LOG2E)).astype(BF16)
    eye_d = _eye(dh, BF16)
    q_t = jnp.concatenate([_dot_nt(eye_d, xq[:, hh * dh:(hh + 1) * dh]) for hh in range(hpg)],
                          axis=1).astype(BF16)

    cmp_id = lax.broadcasted_iota(jnp.int32, (n_cmp_pad, rws), 0)
    t_pos = i * tq + (lax.broadcasted_iota(jnp.int32, (n_cmp_pad, rws), 1) & (tq - 1))
    valid = (t_pos - (cmp_id * CMP_STRIDE + CMP_LEN - 1) >= 0) & (cmp_id < n_cmp_pad - 1)
    s = jnp.where(valid, _dot(kc_ref[0, 0].astype(BF16), q_t) + bc_ref[0, 0], MASKED)
    e = jnp.where(valid, jnp.exp2(s - jnp.max(s, axis=0, keepdims=True)), 0.0)
    den = jnp.sum(e, axis=0, keepdims=True)
    p_c = e / jnp.where(den > 0.0, den, 1.0)
    o_cmp = _dot(vct_ref[0, 0].astype(BF16), p_c.astype(BF16))

    p_heads = per_head(p_c)
    p_sum = p_heads[0]
    for ph in p_heads[1:]:
        p_sum = p_sum + ph
    blk_o = lax.broadcasted_iota(jnp.int32, (n_blk, n_cmp_pad), 0)
    cmp_o = lax.broadcasted_iota(jnp.int32, (n_blk, n_cmp_pad), 1)
    overlap_t = ((cmp_o * CMP_STRIDE <= blk_o * SEL_BLOCK + SEL_BLOCK - 1)
                 & (cmp_o * CMP_STRIDE + CMP_LEN - 1 >= blk_o * SEL_BLOCK)).astype(F32)
    imp = _dot(overlap_t, p_sum, HI)
    jj = lax.broadcasted_iota(jnp.int32, (n_blk, tq), 0)
    cur = (i * tq + lax.broadcasted_iota(jnp.int32, (n_blk, tq), 1)) >> n_blk_log2
    forced = (jj == 0) | (jj == cur) | (jj == cur - 1)
    imp = jnp.where(jj > cur, -1e6, jnp.where(forced, 1e6, imp))
    rank = jnp.zeros((n_blk, tq), jnp.int32)
    for a in range(n_blk):
        row = imp[a:a + 1, :]
        beats = (row > imp) | ((row == imp) & (a < jj))
        rank = rank + beats.astype(jnp.int32)
    not_sel = jnp.where(rank < SEL_TOP_N, 0.0, 1.0).astype(BF16)

    q_aug = jnp.concatenate([q_t, jnp.concatenate([not_sel] * hpg, axis=1),
                             jnp.zeros((kw - dh - n_blk, rws), BF16)], axis=0)
    q_pad = jnp.concatenate([q_t, jnp.zeros((kw - dh, rws), BF16)], axis=0)


    n_win = WINDOW // tq
    tiles, scores = [], []
    for delta in range(n_win + 1):
        entry = {0: BT_DIAG, 1: BT_PREV, n_win: BT_WIN_EDGE}.get(delta, BT_FAR)
        if delta > 0:
            entry = jnp.where(i - delta >= 0, entry, BT_NONE)
        tiles.append(jnp.maximum(i - delta, 0))
        off = pl.multiple_of(tiles[-1] * tq, tq)
        scores.append(_dot(kw_ref[0, pl.ds(off, tq), :], q_pad) + bt_ref[0, entry])
    m_all = scores[0]
    for sc in scores[1:]:
        m_all = jnp.maximum(m_all, sc)
    m_w = jnp.max(m_all, axis=0, keepdims=True)
    acc_w = jnp.zeros((vwt_ref.shape[2], rws), F32)
    for kt, sc in zip(tiles, scores):
        acc_w = acc_w + _dot(vwt_ref[0, kt], jnp.exp2(sc - m_w).astype(BF16))
    o_win = acc_w[0:dh] / acc_w[dh:dh + 1]

    g_t = _dot_nt(_eye(GATE_PAD, F32), gate_ref[0], HI)
    sig = jax.nn.sigmoid(g_t[0:4 * hpg])
    gate = [jnp.concatenate([sig[br * hpg + hh:br * hpg + hh + 1] for hh in range(hpg)], axis=1)
            for br in range(3)]
    m_ref[...] = jnp.full(m_ref.shape, MASKED, F32)
    acc_ref[...] = jnp.zeros(acc_ref.shape, F32)

    def sel_update(tiles):
        scores = []
        for kt in tiles:
            off = pl.multiple_of(kt * tq, tq)
            scores.append(_dot(ks_ref[0, pl.ds(off, tq), :], q_aug)
                          + bt_ref[0, jnp.minimum(i - kt, BT_FAR)])
        s_max = scores[0]
        for sc in scores[1:]:
            s_max = jnp.maximum(s_max, sc)
        m_prev = m_ref[...]
        m_new = jnp.maximum(m_prev, jnp.max(s_max, axis=0, keepdims=True))
        acc = jnp.exp2(m_prev - m_new) * acc_ref[...]
        for kt, sc in zip(tiles, scores):
            acc = acc + _dot(vst_ref[0, kt], jnp.exp2(sc - m_new).astype(BF16))
        m_ref[...] = m_new
        acc_ref[...] = acc

    def sel_pair(j, carry):
        sel_update([2 * j, 2 * j + 1])
        return carry

    lax.fori_loop(0, (i + 1) // 2, sel_pair, 0)

    @pl.when(i % 2 == 0)
    def _():
        sel_update([i])

    o_sel = acc_ref[0:dh, :] / acc_ref[dh:dh + 1, :]

    y_t = (gate[0] * o_cmp + gate[1] * o_sel + gate[2] * o_win).astype(BF16)
    eye_q = _eye(tq, BF16)
    o_ref[0] = jnp.concatenate([_dot_nt(eye_q, yh) for yh in per_head(y_t)],
                               axis=1).astype(BF16)


def _nsa_attention(proj, kv, kv_t, kvc, kvc_t, bias_c, bias_t, *, bsz, seq):
    g, tq, hpg, dh = NSA_KV_GROUPS, NSA_TQ, NSA_HPG, NSA_HEAD_DIM
    n_rows = kvc.shape[2]
    nq = seq // tq
    group_w = hpg * dh
    kv_spec = lambda pos: pl.BlockSpec((1, seq, LANES), lambda b, gi, i: (b, 0, pos * g + gi))
    kvt_spec = lambda pos: pl.BlockSpec((1, nq, LANES, tq), lambda b, gi, i: (b, 0, pos * g + gi, 0))
    kv3 = kv.reshape(bsz, seq, KV_COLS)
    proj3 = proj.reshape(bsz, seq, PROJ_COLS)
    out = pl.pallas_call(
        _nsa_body,
        out_shape=jax.ShapeDtypeStruct((bsz, seq, NSA_DIM), BF16),
        grid=(bsz, g, nq),
        in_specs=[
            pl.BlockSpec((1, tq, group_w), lambda b, gi, i: (b, i, COL_QNSA // group_w + gi)),
            pl.BlockSpec((1, 1, n_rows, dh), lambda b, gi, i: (0, b * g + gi, 0, 0)),
            pl.BlockSpec((1, 1, dh, n_rows), lambda b, gi, i: (1, b * g + gi, 0, 0)),
            kv_spec(ROW_K_SEL), kvt_spec(T_V_SEL), kv_spec(ROW_K_WIN), kvt_spec(T_V_WIN),
            pl.BlockSpec((1, 1, n_rows, hpg * tq), lambda b, gi, i: (gi, i, 0, 0)),
            pl.BlockSpec((1, BT_COUNT, tq, hpg * tq), lambda b, gi, i: (gi, 0, 0, 0)),
            pl.BlockSpec((1, tq, GATE_PAD), lambda b, gi, i: (b, i, COL_GNSA // GATE_PAD + gi)),
        ],
        out_specs=pl.BlockSpec((1, tq, group_w), lambda b, gi, i: (b, i, gi)),
        scratch_shapes=[pltpu.VMEM((1, hpg * tq), F32),
                        pltpu.VMEM((LANES, hpg * tq), F32)],
        compiler_params=_params(("parallel", "parallel", "arbitrary")),
        name="nsa_attention",
    )(proj3, kvc, kvc_t, kv3, kv_t, kv3, kv_t, bias_c, bias_t, proj3)
    return out.reshape(bsz * seq, NSA_DIM)


def _mem_body(q_ref, mem_ref, g_ref, w_ref, o_ref, kv_ref):
    @pl.when(pl.program_id(1) == 0)
    def _():
        kv_ref[...] = _dot(_rms(mem_ref[0], g_ref[...]).astype(BF16), w_ref[...]).astype(BF16)

    outs = []
    for h in range(MEM_HEADS):
        sl = slice(h * MEM_HEAD_DIM, (h + 1) * MEM_HEAD_DIM)
        qh = (q_ref[:, sl] * (MEM_HEAD_DIM ** -0.5)).astype(BF16)
        s = _dot_nt(qh, kv_ref[:, sl])
        e = jnp.exp(s - jnp.max(s, axis=-1, keepdims=True))
        p = e / jnp.sum(e, axis=-1, keepdims=True)
        outs.append(_dot(p.astype(BF16), kv_ref[:, MEM_DIM + h * MEM_HEAD_DIM:
                                                MEM_DIM + (h + 1) * MEM_HEAD_DIM]))
    o_ref[...] = jnp.concatenate(outs, axis=1)


def _mem_attention(proj, mem, gain, w_kv, *, seq, tq=1024):
    t = proj.shape[0]
    bsz, m, d = mem.shape
    per_seq = seq // tq
    return pl.pallas_call(
        _mem_body,
        out_shape=jax.ShapeDtypeStruct((t, MEM_DIM), F32),
        grid=(bsz, per_seq),
        in_specs=[
            pl.BlockSpec((tq, MEM_DIM), lambda b, j: (b * per_seq + j, COL_QMEM // MEM_DIM)),
            pl.BlockSpec((1, m, d), lambda b, j: (b, 0, 0)),
            pl.BlockSpec((1, d), lambda b, j: (0, 0)),
            pl.BlockSpec((d, 2 * MEM_DIM), lambda b, j: (0, 0), pipeline_mode=pl.Buffered(1)),
        ],
        out_specs=pl.BlockSpec((tq, MEM_DIM), lambda b, j: (b * per_seq + j, 0)),
        scratch_shapes=[pltpu.VMEM((m, 2 * MEM_DIM), BF16)],
        compiler_params=_params(("parallel", "arbitrary")),
        name="mem_attention",
    )(proj, mem, gain, w_kv)


def _merge_body(x_ref, gain_ref, yr_ref, yn_ref, ym_ref, wg_ref, wr_ref, wn_ref, wm_ref, wo_ref,
                o_ref):
    d = x_ref.shape[1]
    x = x_ref[...]
    gates = jax.nn.sigmoid(_dot(_rms(x, gain_ref[...]).astype(BF16), wg_ref[...]))
    merged = (gates[:, 0:d] * _dot(yr_ref[...].astype(BF16), wr_ref[...])
              + gates[:, d:2 * d] * _dot(yn_ref[...], wn_ref[...])
              + gates[:, 2 * d:3 * d] * _dot(ym_ref[...].astype(BF16), wm_ref[...]))
    o_ref[...] = x + _dot(merged.astype(BF16), wo_ref[...])


def _merge(x, gain, y_rwkv, y_nsa, y_mem, w_g, w_r, w_n, w_m, w_o, *, tm=512):
    t, d = x.shape
    row = lambda i: (i, 0)
    const = lambda i: (0, 0)
    return pl.pallas_call(
        _merge_body,
        out_shape=jax.ShapeDtypeStruct((t, d), F32),
        grid=(t // tm,),
        in_specs=[
            pl.BlockSpec((tm, d), row),
            pl.BlockSpec((1, d), const),
            pl.BlockSpec((tm, RWKV_DIM), row),
            pl.BlockSpec((tm, NSA_DIM), row),
            pl.BlockSpec((tm, MEM_DIM), row),
            pl.BlockSpec((d, N_BRANCH * d), const),
            pl.BlockSpec((RWKV_DIM, d), const),
            pl.BlockSpec((NSA_DIM, d), const),
            pl.BlockSpec((MEM_DIM, d), const),
            pl.BlockSpec((d, d), const),
        ],
        out_specs=pl.BlockSpec((tm, d), row),
        compiler_params=_params(("parallel",)),
        name="merge",
    )(x, gain, y_rwkv, y_nsa, y_mem, w_g, w_r, w_n, w_m, w_o)


def _row(a):
    return a.reshape(1, -1)


def _in_proj(x, p, l, seq):
    d = x.shape[1]
    g, hpg, dh = NSA_KV_GROUPS, NSA_HPG, NSA_HEAD_DIM
    w_in = p['w_in'][l]
    o = 0
    parts = {}
    for name, size in (('rwkv', RWKV_PROJ), ('q', NSA_DIM), ('kv', KV_KINDS * NSA_KV_DIM),
                       ('g_nsa', 3 * NSA_HEADS), ('q_mem', MEM_DIM)):
        parts[name] = w_in[:, o:o + size]
        o += size
    gates = parts['g_nsa'].reshape(d, 3, g, hpg).transpose(0, 2, 1, 3).reshape(d, g, 3 * hpg)
    gates = jnp.pad(gates, ((0, 0), (0, 0), (0, GATE_PAD - 3 * hpg))).reshape(d, g * GATE_PAD)
    w_other = jnp.concatenate([parts['q'], parts['q_mem'], gates], axis=1).astype(BF16)
    w_kv = jnp.pad(parts['kv'].reshape(d, KV_KINDS, g, dh), ((0, 0), (0, 0), (0, 0), (0, LANES - dh)))
    w_row = w_kv[:, KV_ROW_KINDS, :, :].reshape(d, KV_COLS).astype(BF16)
    w_t = w_kv[:, KV_T_KINDS, :, :].reshape(d, KV_T_ROWS).T.astype(BF16)
    w_lora = jnp.zeros((LORA_DIM, 3 * RWKV_DIM), F32)
    w_lora = w_lora.at[0:DECAY_LORA, 0:RWKV_DIM].set(p['rwkv_w2'][l])
    w_lora = w_lora.at[DECAY_LORA:DECAY_LORA + AAA_LORA, RWKV_DIM:2 * RWKV_DIM].set(p['rwkv_a2'][l])
    w_lora = w_lora.at[DECAY_LORA + AAA_LORA:, 2 * RWKV_DIM:].set(p['rwkv_g2'][l])
    proj, kv, kv_t, *rwkv_in = _in_proj_call(
        x, seq, _row(p['mix_norm'][l]), parts['rwkv'].astype(BF16), w_other, w_row, w_t,
        _row(p['rwkv_mu'][l]), w_lora.astype(BF16), _row(p['rwkv_w0'][l]), _row(p['rwkv_a0'][l]),
        _row(p['rwkv_k_k'][l]), _row(p['rwkv_k_a'][l]))
    return proj, kv, kv_t, rwkv_in


def _rwkv_branch(rwkv_in, p, l, bsz, seq):
    return _rwkv_scan(*rwkv_in, _row(p['rwkv_r_k'][l]), _row(p['rwkv_gn_gain'][l]),
                      _row(p['rwkv_gn_bias'][l]), batch=bsz, seq=seq)


def _nsa_branch(proj, kv, kv_t, bias_c, bias_t, p, l, bsz, seq):
    w1 = jnp.stack([p['cmp_k_w1'][l], p['cmp_v_w1'][l]]).astype(BF16)
    pe = jnp.stack([p['cmp_pe_k'][l].reshape(1, -1), p['cmp_pe_v'][l].reshape(1, -1)])
    w2 = jnp.stack([p['cmp_k_w2'][l], p['cmp_v_w2'][l]]).astype(BF16)
    kvc, kvc_t = _compress(kv, w1, pe, w2, bsz=bsz, seq=seq)
    return _nsa_attention(proj, kv, kv_t, kvc, kvc_t, bias_c, bias_t, bsz=bsz, seq=seq)


def _mem_branch(proj, mem, p, l):
    w_kv = jnp.concatenate([p['mem_w_k'][l], p['mem_w_v'][l]], axis=1).astype(BF16)
    return _mem_attention(proj, mem, _row(p['mem_norm'][l]), w_kv, seq=proj.shape[0] // mem.shape[0])


def _layer(x, mem, l, bias_c, bias_t, p):
    bsz, seq, d = x.shape
    t = bsz * seq
    row = _row
    x = x.reshape(t, d)

    x = _ffn(x, row(p['ffn1_norm'][l]), p['ffn1_w_gate'][l].astype(BF16),
             p['ffn1_w_up'][l].astype(BF16), p['ffn1_w_down'][l].astype(BF16),
             row(p['final_norm']), final=False)

    proj, kv, kv_t, rwkv_in = _in_proj(x, p, l, seq)
    y_rwkv = _rwkv_branch(rwkv_in, p, l, bsz, seq)
    y_nsa = _nsa_branch(proj, kv, kv_t, bias_c, bias_t, p, l, bsz, seq)
    y_mem = _mem_branch(proj, mem, p, l)

    w_gate = p['w_in'][l][:, -N_BRANCH * d:].astype(BF16)
    x = _merge(x, row(p['mix_norm'][l]), y_rwkv, y_nsa, y_mem, w_gate,
               p['w_br_rwkv'][l].astype(BF16), p['w_br_nsa'][l].astype(BF16),
               p['w_br_mem'][l].astype(BF16), p['w_out'][l].astype(BF16))

    last = l == p['ffn1_norm'].shape[0] - 1
    x = _ffn(x, row(p['ffn2_norm'][l]), p['ffn2_w_gate'][l].astype(BF16),
             p['ffn2_w_up'][l].astype(BF16), p['ffn2_w_down'][l].astype(BF16),
             row(p['final_norm']), final=last)
    return x.reshape(bsz, seq, d)


def kernel(x, mem, ffn1_norm, ffn1_w_gate, ffn1_w_up, ffn1_w_down, mix_norm, w_in, rwkv_mu, rwkv_w0, rwkv_w2, rwkv_a0, rwkv_a2, rwkv_g2, rwkv_k_k, rwkv_k_a, rwkv_r_k, rwkv_gn_gain, rwkv_gn_bias, cmp_pe_k, cmp_k_w1, cmp_k_w2, cmp_pe_v, cmp_v_w1, cmp_v_w2, rel_bias, mem_norm, mem_w_k, mem_w_v, w_br_rwkv, w_br_nsa, w_br_mem, w_out, ffn2_norm, ffn2_w_gate, ffn2_w_up, ffn2_w_down, final_norm):
    p = dict(ffn1_norm=ffn1_norm, ffn1_w_gate=ffn1_w_gate, ffn1_w_up=ffn1_w_up,
             ffn1_w_down=ffn1_w_down, mix_norm=mix_norm, w_in=w_in, rwkv_mu=rwkv_mu,
             rwkv_w0=rwkv_w0, rwkv_w2=rwkv_w2, rwkv_a0=rwkv_a0, rwkv_a2=rwkv_a2, rwkv_g2=rwkv_g2,
             rwkv_k_k=rwkv_k_k, rwkv_k_a=rwkv_k_a, rwkv_r_k=rwkv_r_k, rwkv_gn_gain=rwkv_gn_gain,
             rwkv_gn_bias=rwkv_gn_bias, cmp_pe_k=cmp_pe_k, cmp_k_w1=cmp_k_w1, cmp_k_w2=cmp_k_w2,
             cmp_pe_v=cmp_pe_v, cmp_v_w1=cmp_v_w1, cmp_v_w2=cmp_v_w2, mem_norm=mem_norm,
             mem_w_k=mem_w_k, mem_w_v=mem_w_v, w_br_rwkv=w_br_rwkv, w_br_nsa=w_br_nsa,
             w_br_mem=w_br_mem, w_out=w_out, ffn2_norm=ffn2_norm, ffn2_w_gate=ffn2_w_gate,
             ffn2_w_up=ffn2_w_up, ffn2_w_down=ffn2_w_down, final_norm=final_norm)
    bias_c, bias_t = _bias_tables(rel_bias, x.shape[1])
    for l in range(ffn1_norm.shape[0]):
        x = _layer(x, mem, l, bias_c, bias_t, p)
    return x
```

```python
import functools
import math

import jax
import jax.numpy as jnp
from jax import lax
from jax.experimental import pallas as pl
from jax.experimental.pallas import tpu as pltpu

F32 = jnp.float32
BF16 = jnp.bfloat16
HI = lax.Precision.HIGHEST

D_MODEL = 1024
NORM_EPS = 1e-6
D_FF = 2816
RWKV_HEADS = 8
RWKV_HEAD_DIM = 64
RWKV_DIM = RWKV_HEADS * RWKV_HEAD_DIM
DECAY_LORA = 64
AAA_LORA = 64
GATE_LORA = 128
LORA_DIM = DECAY_LORA + AAA_LORA + GATE_LORA
RWKV_GN_EPS = 64e-5
RWKV_PROJ = 3 * RWKV_DIM + LORA_DIM
NSA_HEADS = 8
NSA_KV_GROUPS = 2
NSA_HPG = NSA_HEADS // NSA_KV_GROUPS
NSA_HEAD_DIM = 64
NSA_DIM = NSA_HEADS * NSA_HEAD_DIM
NSA_KV_DIM = NSA_KV_GROUPS * NSA_HEAD_DIM
CMP_LEN = 32
CMP_STRIDE = 16
CMP_HIDDEN = 256
SEL_BLOCK = 64
SEL_TOP_N = 16
WINDOW = 512
REL_BUCKETS = 32
REL_MAX_DIST = 128
MEM_HEADS = 4
MEM_HEAD_DIM = 128
MEM_DIM = MEM_HEADS * MEM_HEAD_DIM
N_BRANCH = 3

LANES = 128
GATE_PAD = LANES
COL_QNSA = 0
COL_QMEM = COL_QNSA + NSA_DIM
COL_GNSA = COL_QMEM + MEM_DIM
PROJ_COLS = COL_GNSA + NSA_KV_GROUPS * GATE_PAD
assert COL_QMEM % MEM_DIM == 0 and COL_GNSA % GATE_PAD == 0
KV_KINDS = 6
KV_ROW_KINDS = (0, 1, 2, 4)
KV_T_KINDS = (3, 5)
KV_COLS = len(KV_ROW_KINDS) * NSA_KV_GROUPS * LANES
KV_T_ROWS = len(KV_T_KINDS) * NSA_KV_GROUPS * LANES
ROW_K_SEL, ROW_K_WIN = KV_ROW_KINDS.index(2), KV_ROW_KINDS.index(4)
T_V_SEL, T_V_WIN = KV_T_KINDS.index(3), KV_T_KINDS.index(5)
LOG2E = 1.4426950408889634

RWKV_CHUNK = 64
RWKV_INV_BLOCK = 16
NSA_TQ = 256
MASKED = -1e30
BT_DIAG, BT_PREV, BT_FAR, BT_WIN_EDGE, BT_NONE, BT_COUNT = 0, 1, 2, 3, 4, 5
V7X_VMEM_BYTES = 64 * 1024 * 1024
VMEM_LIMIT = V7X_VMEM_BYTES - 8 * 1024 * 1024


def _dot(a, b, precision=None):
    return jnp.dot(a, b, preferred_element_type=F32, precision=precision)


def _dot_nt(a, b, precision=None):
    return lax.dot_general(a, b, (((1,), (1,)), ((), ())), preferred_element_type=F32,
                           precision=precision)


def _params(semantics):
    return pltpu.CompilerParams(dimension_semantics=semantics, vmem_limit_bytes=VMEM_LIMIT)


def _rms(x, g):
    return x * lax.rsqrt(jnp.mean(x * x, axis=-1, keepdims=True) + NORM_EPS) * g


def _ffn_body(x_ref, g_ref, wg_ref, wu_ref, wd_ref, fg_ref, o_ref, *, tf, final):
    x = x_ref[...]
    h = _rms(x, g_ref[...]).astype(BF16)
    acc = jnp.zeros(x.shape, F32)
    for j in range(wg_ref.shape[1] // tf):
        cols = slice(j * tf, (j + 1) * tf)
        act = (jax.nn.silu(_dot(h, wg_ref[:, cols])) * _dot(h, wu_ref[:, cols])).astype(BF16)
        acc = acc + _dot(act, wd_ref[cols, :])
    y = x + 0.5 * acc
    if final:
        y = _rms(y, fg_ref[...])
    o_ref[...] = y


def _ffn(x, gain, wg, wu, wd, final_gain, *, final, tm=1024, tf=256):
    t, d = x.shape
    f = wg.shape[1]
    resident = lambda shape: pl.BlockSpec(shape, lambda i: (0, 0), pipeline_mode=pl.Buffered(1))
    return pl.pallas_call(
        functools.partial(_ffn_body, tf=tf, final=final),
        out_shape=jax.ShapeDtypeStruct((t, d), F32),
        grid=(t // tm,),
        in_specs=[
            pl.BlockSpec((tm, d), lambda i: (i, 0)),
            pl.BlockSpec((1, d), lambda i: (0, 0)),
            resident((d, f)),
            resident((d, f)),
            resident((f, d)),
            pl.BlockSpec((1, d), lambda i: (0, 0)),
        ],
        out_specs=pl.BlockSpec((tm, d), lambda i: (i, 0)),
        compiler_params=_params(("parallel",)),
        name="ffn_final" if final else "ffn",
    )(x, gain, wg, wu, wd, final_gain)


def _write_kv(h, w_ref, wt_ref, o_ref, ot_ref, *, seq):
    dh, tk, tm = NSA_HEAD_DIM, NSA_TQ, h.shape[0]
    y = _dot(h, w_ref[...])
    row = lax.broadcasted_iota(jnp.int32, (tm, LANES), 0)
    lane = lax.broadcasted_iota(jnp.int32, (tm, LANES), 1)
    pos = (pl.program_id(0) * tm) % seq + row
    block_mark = jnp.where(lane - dh == (pos >> (SEL_BLOCK.bit_length() - 1)), MASKED, 0.0)
    for tile in range(KV_COLS // LANES):
        part = y[:, tile * LANES:(tile + 1) * LANES]
        if tile // NSA_KV_GROUPS == ROW_K_SEL:
            part = part + block_mark
        o_ref[:, tile * LANES:(tile + 1) * LANES] = part.astype(BF16)

    y_t = _dot_nt(wt_ref[...], h)
    row_t = lax.broadcasted_iota(jnp.int32, y_t.shape, 0)
    y_t = (y_t + jnp.where((row_t & (LANES - 1)) == dh, 1.0, 0.0)).astype(BF16)
    for c in range(tm // tk):
        ot_ref[0, c] = y_t[:, c * tk:(c + 1) * tk]


def _in_proj_body(x_ref, xp_ref, g_ref, wr_ref, wo_ref, wkv_ref, wkvt_ref, mu_ref, wl_ref, w0_ref,
                  a0_ref, kk_ref, ka_ref, proj_o, kv_o, kvt_o, r_o, k_o, v_o, kk_o, b_o, lw_o, g_o,
                  *, tiles_per_seq, seq):
    i = pl.program_id(0)
    gain = g_ref[...]
    h = _rms(x_ref[...], gain).astype(BF16)
    proj_o[...] = _dot(h, wo_ref[...])
    _write_kv(h, wkv_ref, wkvt_ref, kv_o, kvt_o, seq=seq)
    n_up = xp_ref.shape[0]
    h_up = _rms(xp_ref[...], gain).astype(BF16)
    p_all = _dot(jnp.concatenate([h_up, h], axis=0), wr_ref[...])
    p = p_all[n_up:]
    keep = jnp.where(i % tiles_per_seq == 0, 0.0, 1.0)
    prev_last = p_all[n_up - 1:n_up, :] * keep
    rows = lax.broadcasted_iota(jnp.int32, p.shape, 0)
    shifted = jnp.where(rows == 0, prev_last, pltpu.roll(p, 1, 0))
    x = p + (shifted - p) * mu_ref[...]

    r = x[:, 0:RWKV_DIM]
    k = x[:, RWKV_DIM:2 * RWKV_DIM]
    v = x[:, 2 * RWKV_DIM:3 * RWKV_DIM]
    s = x[:, 3 * RWKV_DIM:RWKV_PROJ]
    lane = lax.broadcasted_iota(jnp.int32, s.shape, 1)
    z = jnp.where(lane < DECAY_LORA, jnp.tanh(s),
                  jnp.where(lane < DECAY_LORA + AAA_LORA, s, jax.nn.sigmoid(s)))
    lo = _dot(z.astype(BF16), wl_ref[...])
    a = jax.nn.sigmoid(a0_ref[...] + lo[:, RWKV_DIM:2 * RWKV_DIM])

    kkr = k * kk_ref[...]
    sq = kkr * kkr
    sq_hi = sq.astype(BF16)
    sq_lo = (sq - sq_hi.astype(F32)).astype(BF16)
    shift = RWKV_HEAD_DIM.bit_length() - 1
    same_head = ((lax.broadcasted_iota(jnp.int32, (RWKV_DIM, RWKV_DIM), 0) >> shift)
                 == (lax.broadcasted_iota(jnp.int32, (RWKV_DIM, RWKV_DIM), 1) >> shift)).astype(BF16)
    ssq = _dot(sq_hi, same_head) + _dot(sq_lo, same_head)
    kk = kkr / jnp.maximum(jnp.sqrt(ssq), 1e-12)

    r_o[...] = r
    k_o[...] = k * (1.0 + (a - 1.0) * ka_ref[...])
    v_o[...] = v
    kk_o[...] = kk
    b_o[...] = kk * a
    lw_o[...] = -math.exp(-0.5) * jax.nn.sigmoid(w0_ref[...] + lo[:, 0:RWKV_DIM])
    g_o[...] = lo[:, 2 * RWKV_DIM:3 * RWKV_DIM]


def _in_proj_call(x, seq, gain, w_rwkv, w_other, w_kv, w_kv_t, mu, w_lora, w0, a0, k_k, k_a, *,
                  tm=512):
    t, d = x.shape
    tk = NSA_TQ
    per_seq = seq // tm
    row = lambda i: (i, 0)
    const = lambda i: (0, 0)
    resident = lambda shape: pl.BlockSpec(shape, const, pipeline_mode=pl.Buffered(1))
    vec = pl.BlockSpec((1, RWKV_DIM), const)
    tok = jax.ShapeDtypeStruct((t, RWKV_DIM), F32)
    return pl.pallas_call(
        functools.partial(_in_proj_body, tiles_per_seq=per_seq, seq=seq),
        out_shape=[jax.ShapeDtypeStruct((t, PROJ_COLS), F32),
                   jax.ShapeDtypeStruct((t, KV_COLS), BF16),
                   jax.ShapeDtypeStruct((t // seq, seq // tk, KV_T_ROWS, tk), BF16)] + [tok] * 7,
        grid=(t // tm,),
        in_specs=[
            pl.BlockSpec((tm, d), row),
            pl.BlockSpec((16, d), lambda i: (jnp.maximum(i * (tm // 16) - 1, 0), 0)),
            pl.BlockSpec((1, d), const),
            resident((d, RWKV_PROJ)),
            resident((d, PROJ_COLS)),
            resident((d, KV_COLS)),
            resident((KV_T_ROWS, d)),
            pl.BlockSpec((1, RWKV_PROJ), const),
            resident((LORA_DIM, 3 * RWKV_DIM)),
            vec, vec, vec, vec,
        ],
        out_specs=[pl.BlockSpec((tm, PROJ_COLS), row),
                   pl.BlockSpec((tm, KV_COLS), row),
                   pl.BlockSpec((1, tm // tk, KV_T_ROWS, tk),
                                lambda i: (i // per_seq, i % per_seq, 0, 0))]
        + [pl.BlockSpec((tm, RWKV_DIM), row)] * 7,
        compiler_params=_params(("parallel",)),
        name="in_proj",
    )(x, x, gain, w_rwkv, w_other, w_kv, w_kv_t, mu, w_lora, w0, a0, k_k, k_a)


def _rwkv_scan_body(r_ref, k_ref, v_ref, kk_ref, b_ref, lw_ref, g_ref, rk_ref, gg_ref, gb_ref,
                    o_ref, st_ref):
    c_sz, n, nh = RWKV_CHUNK, RWKV_HEAD_DIM, RWKV_HEADS

    @pl.when(pl.program_id(1) == 0)
    def _():
        st_ref[...] = jnp.zeros_like(st_ref)

    ri = lax.broadcasted_iota(jnp.int32, (c_sz, c_sz), 0)
    ci = lax.broadcasted_iota(jnp.int32, (c_sz, c_sz), 1)
    incl = ci <= ri
    eye_b = (ci == ri).astype(BF16)
    row2 = lax.broadcasted_iota(jnp.int32, (c_sz, 2 * c_sz), 0)
    lane2 = lax.broadcasted_iota(jnp.int32, (c_sz, 2 * c_sz), 1)
    right_half = lane2 >= c_sz
    zeros_b = jnp.zeros((c_sz, n), BF16)
    inv_shift = RWKV_INV_BLOCK.bit_length() - 1

    rows = []
    for bb in range(st_ref.shape[0]):
        lw = lw_ref[bb]
        cum = _dot(incl.astype(F32), lw, HI)
        cum_last = cum[c_sz - 1:c_sz, :]
        r, k, v, b = r_ref[bb], k_ref[bb], v_ref[bb], b_ref[bb]
        p_inv = jnp.exp(-cum)
        p_end = jnp.exp(cum_last - cum)
        rows.append(dict(
            left=jnp.concatenate([(-(kk_ref[bb] * jnp.exp(cum - lw))).astype(BF16),
                                  (r * jnp.exp(cum)).astype(BF16)], axis=0),
            bt=(b * p_inv).astype(BF16), kt=(k * p_inv).astype(BF16),
            bh=(b * p_end).astype(BF16), kh=(k * p_end).astype(BF16),
            v=v, v_b=v.astype(BF16), d_p=jnp.exp(cum_last), rk=r * k * rk_ref[...]))

    units = [(bb, h) for bb in range(len(rows)) for h in range(nh)]
    col = lambda name, u: rows[u[0]][name][:, u[1] * n:(u[1] + 1) * n]
    a_all = [_dot_nt(col('left', u), jnp.concatenate([col('bt', u), col('kt', u)], axis=0))
             for u in units]
    key2 = jnp.where(right_half, lane2 - c_sz, lane2)
    w_u = [jnp.where(right_half & (key2 < row2), a[:c_sz], 0.0).astype(BF16) for a in a_all]
    w_y = [jnp.where(key2 <= row2, a[c_sz:], 0.0).astype(BF16) for a in a_all]

    same_block = (row2 >> inv_shift) == (key2 >> inv_shift)
    x = [jnp.where((lane2 < row2) & same_block, a[:c_sz], jnp.where(lane2 == row2 + c_sz, 1.0, 0.0))
         for a in a_all]
    for _ in range(inv_shift):
        hi = [xu.astype(BF16) for xu in x]
        lo = [(xu - h_.astype(F32)).astype(BF16) for xu, h_ in zip(x, hi)]
        x = [_dot(h_[:, :c_sz], h_) + _dot(h_[:, :c_sz], l_) + _dot(l_[:, :c_sz], h_)
             + jnp.where(right_half, xu, 0.0) for xu, h_, l_ in zip(x, hi, lo)]
    x_b = [xu.astype(BF16) for xu in x]
    solve_diag = lambda j, z: _dot(x_b[j], jnp.concatenate([zeros_b, z.astype(BF16)], axis=0))
    q_b = [solve_diag(j, jnp.where((ci < ri) & ((ri >> inv_shift) != (ci >> inv_shift)),
                                   a[:c_sz, :c_sz], 0.0)).astype(BF16)
           for j, a in enumerate(a_all)]

    s0 = [st_ref[bb, h] for bb, h in units]
    ls0 = [_dot_nt(col('left', u), s0[j].astype(BF16)) for j, u in enumerate(units)]
    rhs = [ls0[j][:c_sz] + _dot(w_u[j], jnp.concatenate([zeros_b, col('v_b', u)], axis=0))
           for j, u in enumerate(units)]
    g0 = [solve_diag(j, rhs[j]) for j in range(len(units))]
    u_f = g0
    for _ in range(c_sz // RWKV_INV_BLOCK - 1):
        u_f = [g0[j] + _dot(q_b[j], u_f[j].astype(BF16)) for j in range(len(units))]
    u_b = [uj.astype(BF16) for uj in u_f]
    uv = [jnp.concatenate([u_b[j], col('v_b', u)], axis=0) for j, u in enumerate(units)]
    y = [ls0[j][c_sz:] + _dot(w_y[j], uv[j]) for j in range(len(units))]
    uv_t = [_dot_nt(eye_b, uv_j).astype(BF16) for uv_j in uv]
    for j, u in enumerate(units):
        st_ref[u[0], u[1]] = (s0[j] * col('d_p', u)
                              + _dot(uv_t[j], jnp.concatenate([col('bh', u), col('kh', u)], axis=0)))

    for bb in range(len(rows)):
        outs = []
        for h in range(nh):
            sl = slice(h * n, (h + 1) * n)
            yh = y[bb * nh + h]
            mean = jnp.mean(yh, axis=-1, keepdims=True)
            var = jnp.mean(jnp.square(yh - mean), axis=-1, keepdims=True)
            yn = (yh - mean) * lax.rsqrt(var + RWKV_GN_EPS)
            yn = yn * gg_ref[:, sl] + gb_ref[:, sl]
            bonus = jnp.sum(rows[bb]['rk'][:, sl], axis=-1, keepdims=True) * rows[bb]['v'][:, sl]
            outs.append((yn + bonus) * g_ref[bb, :, sl])
        o_ref[bb] = jnp.concatenate(outs, axis=1)


def _rwkv_scan(r, k, v, kk, b, lw, g, r_k, gn_gain, gn_bias, *, batch, seq, nb=4):
    t = r.shape[0]
    nc = seq // RWKV_CHUNK
    tok = pl.BlockSpec((nb, RWKV_CHUNK, RWKV_DIM), lambda bi, c: (bi, c, 0))
    par = pl.BlockSpec((1, RWKV_DIM), lambda bi, c: (0, 0))
    per_batch = lambda a: a.reshape(batch, seq, RWKV_DIM)
    out = pl.pallas_call(
        _rwkv_scan_body,
        out_shape=jax.ShapeDtypeStruct((batch, seq, RWKV_DIM), F32),
        grid=(batch // nb, nc),
        in_specs=[tok] * 7 + [par] * 3,
        out_specs=tok,
        scratch_shapes=[pltpu.VMEM((nb, RWKV_HEADS, RWKV_HEAD_DIM, RWKV_HEAD_DIM), F32)],
        compiler_params=_params(("parallel", "arbitrary")),
        name="rwkv_scan",
    )(*(per_batch(a) for a in (r, k, v, kk, b, lw, g)), r_k, gn_gain, gn_bias)
    return out.reshape(t, RWKV_DIM)


def _compress_body(x_ref, w1s_ref, w1_ref, pe_ref, w2_ref, w2t_ref, o_ref, ot_ref):
    n_rows = x_ref.shape[1]
    for kind in range(2):
        pe = jnp.broadcast_to(pe_ref[kind], (8, pe_ref.shape[2])).astype(BF16)
        pe_term = _dot(pe, w1_ref[kind])[0:1, :]
        for gi in range(NSA_KV_GROUPS):
            tile = kind * NSA_KV_GROUPS + gi
            both = jnp.zeros((n_rows, 2 * CMP_HIDDEN), F32)
            for l in range(CMP_STRIDE):
                both = both + _dot(x_ref[0, :, l, tile * LANES:(tile + 1) * LANES], w1s_ref[kind, l])
            second_next = pltpu.roll(both[:, CMP_HIDDEN:], n_rows - 1, 0)
            hid = both[:, :CMP_HIDDEN] + second_next + pe_term
            act = jax.nn.gelu(hid).astype(BF16)
            o_ref[kind, gi] = _dot(act, w2_ref[kind])
            ot_ref[kind, gi] = _dot_nt(w2t_ref[kind], act)


def _compress(kv, w1, pe, w2, *, bsz, seq):
    g, dh = NSA_KV_GROUPS, NSA_HEAD_DIM
    rows = seq // CMP_STRIDE
    w1r = jnp.pad(w1.reshape(2, CMP_LEN, dh, CMP_HIDDEN), ((0, 0), (0, 0), (0, LANES - dh), (0, 0)))
    w1s = jnp.concatenate([w1r[:, :CMP_STRIDE], w1r[:, CMP_STRIDE:]], axis=3)
    whole = lambda a: pl.BlockSpec(a.shape, lambda b: (0,) * a.ndim)
    w2t = w2.transpose(0, 2, 1)
    return pl.pallas_call(
        _compress_body,
        out_shape=[jax.ShapeDtypeStruct((2, bsz * g, rows, dh), F32),
                   jax.ShapeDtypeStruct((2, bsz * g, dh, rows), F32)],
        grid=(bsz,),
        in_specs=[pl.BlockSpec((1, rows, CMP_STRIDE, 2 * g * LANES), lambda b: (b, 0, 0, 0)),
                  whole(w1s), whole(w1), whole(pe), whole(w2), whole(w2t)],
        out_specs=[pl.BlockSpec((2, g, rows, dh), lambda b: (0, b, 0, 0)),
                   pl.BlockSpec((2, g, dh, rows), lambda b: (0, b, 0, 0))],
        compiler_params=_params(("parallel",)),
        name="nsa_compress",
    )(kv.reshape(bsz, rows, CMP_STRIDE, KV_COLS), w1s, w1, pe, w2, w2t)


def _t5_bucket(dist):
    n = jnp.maximum(dist, 0)
    exact = REL_BUCKETS // 2
    nf = jnp.maximum(n, 1).astype(F32)
    scaled = jnp.log(nf / exact) / math.log(REL_MAX_DIST / exact) * (REL_BUCKETS - exact)
    large = exact + jnp.floor(scaled).astype(jnp.int32)
    large = jnp.minimum(large, REL_BUCKETS - 1)
    return jnp.where(n < exact, n, large)


def _bias_body(tab_ref, bc_ref, bt_ref, *, seq, n_cmp_pad):
    h = pl.program_id(0)
    tq = NSA_TQ

    def lookup(dist):
        bucket = _t5_bucket(dist)
        out = jnp.zeros(dist.shape, F32)
        for bkt in range(REL_BUCKETS):
            out = jnp.where(bucket == bkt, tab_ref[bkt, h] * LOG2E, out)
        return out

    far = tab_ref[REL_BUCKETS - 1, h] * LOG2E
    assert tq + 1 >= REL_MAX_DIST and WINDOW - tq + 1 >= REL_MAX_DIST

    key = lax.broadcasted_iota(jnp.int32, (tq, tq), 0)
    qry = lax.broadcasted_iota(jnp.int32, (tq, tq), 1)
    bt_ref[0, BT_DIAG] = jnp.where(qry >= key, lookup(qry - key), MASKED)
    bt_ref[0, BT_PREV] = lookup(tq + qry - key)
    bt_ref[0, BT_FAR] = jnp.full((tq, tq), far, F32)
    bt_ref[0, BT_WIN_EDGE] = jnp.where(qry < key, far, MASKED)
    bt_ref[0, BT_NONE] = jnp.full((tq, tq), MASKED, F32)

    per_tile = tq // CMP_STRIDE
    pad = 16
    assert pad * CMP_STRIDE >= REL_MAX_DIST + CMP_LEN - 1 and pad % 8 == 0
    band = per_tile + pad
    cmp_end = lax.broadcasted_iota(jnp.int32, (band, tq), 0) * CMP_STRIDE + CMP_LEN - 1
    qry_c = lax.broadcasted_iota(jnp.int32, (band, tq), 1)

    def cmp_tile(i, carry):
        start = pl.multiple_of(jnp.maximum(i * per_tile - pad, 0), 8)
        bc_ref[0, i] = jnp.full((n_cmp_pad, tq), far, F32)
        bc_ref[0, i, pl.ds(start, band), :] = lookup(i * tq + qry_c - (start * CMP_STRIDE + cmp_end))
        return carry

    lax.fori_loop(0, seq // tq, cmp_tile, 0)


def _bias_tables(rel_bias, seq):
    g, hpg, tq = NSA_KV_GROUPS, NSA_HPG, NSA_TQ
    n_cmp_pad = seq // CMP_STRIDE
    nq = seq // tq
    return pl.pallas_call(
        functools.partial(_bias_body, seq=seq, n_cmp_pad=n_cmp_pad),
        out_shape=[jax.ShapeDtypeStruct((g, nq, n_cmp_pad, hpg * tq), F32),
                   jax.ShapeDtypeStruct((g, BT_COUNT, tq, hpg * tq), F32)],
        grid=(NSA_HEADS,),
        in_specs=[pl.BlockSpec(memory_space=pltpu.SMEM)],
        out_specs=[pl.BlockSpec((1, nq, n_cmp_pad, tq), lambda h: (h // hpg, 0, 0, h % hpg)),
                   pl.BlockSpec((1, BT_COUNT, tq, tq), lambda h: (h // hpg, 0, 0, h % hpg))],
        compiler_params=_params(("parallel",)),
        name="nsa_bias",
    )(rel_bias)


def _eye(n, dtype):
    return (lax.broadcasted_iota(jnp.int32, (n, n), 0)
            == lax.broadcasted_iota(jnp.int32, (n, n), 1)).astype(dtype)


def _nsa_body(q_ref, kc_ref, vct_ref, ks_ref, vst_ref, kw_ref, vwt_ref, bc_ref, bt_ref, gate_ref,
              o_ref, m_ref, acc_ref):
    tq, hpg, dh = NSA_TQ, NSA_HPG, NSA_HEAD_DIM
    rws = hpg * tq
    n_blk_log2 = SEL_BLOCK.bit_length() - 1
    n_blk = ks_ref.shape[1] // SEL_BLOCK
    n_cmp_pad = kc_ref.shape[2]
    kw = ks_ref.shape[2]
    i = pl.program_id(2)

    def per_head(x):
        return [x[:, hh * tq:(hh + 1) * tq] for hh in range(hpg)]

    xq = (q_ref[0] * (dh ** -0.5 * LOG2E)).astype(BF16)
    eye_d = _eye(dh, BF16)
    q_t = jnp.concatenate([_dot_nt(eye_d, xq[:, hh * dh:(hh + 1) * dh]) for hh in range(hpg)],
                          axis=1).astype(BF16)

    cmp_id = lax.broadcasted_iota(jnp.int32, (n_cmp_pad, rws), 0)
    t_pos = i * tq + (lax.broadcasted_iota(jnp.int32, (n_cmp_pad, rws), 1) & (tq - 1))
    valid = (t_pos - (cmp_id * CMP_STRIDE + CMP_LEN - 1) >= 0) & (cmp_id < n_cmp_pad - 1)
    s = jnp.where(valid, _dot(kc_ref[0, 0].astype(BF16), q_t) + bc_ref[0, 0], MASKED)
    e = jnp.where(valid, jnp.exp2(s - jnp.max(s, axis=0, keepdims=True)), 0.0)
    den = jnp.sum(e, axis=0, keepdims=True)
    p_c = e / jnp.where(den > 0.0, den, 1.0)
    o_cmp = _dot(vct_ref[0, 0].astype(BF16), p_c.astype(BF16))

    p_heads = per_head(p_c)
    p_sum = p_heads[0]
    for ph in p_heads[1:]:
        p_sum = p_sum + ph
    blk_o = lax.broadcasted_iota(jnp.int32, (n_blk, n_cmp_pad), 0)
    cmp_o = lax.broadcasted_iota(jnp.int32, (n_blk, n_cmp_pad), 1)
    overlap_t = ((cmp_o * CMP_STRIDE <= blk_o * SEL_BLOCK + SEL_BLOCK - 1)
                 & (cmp_o * CMP_STRIDE + CMP_LEN - 1 >= blk_o * SEL_BLOCK)).astype(F32)
    imp = _dot(overlap_t, p_sum, HI)
    jj = lax.broadcasted_iota(jnp.int32, (n_blk, tq), 0)
    cur = (i * tq + lax.broadcasted_iota(jnp.int32, (n_blk, tq), 1)) >> n_blk_log2
    forced = (jj == 0) | (jj == cur) | (jj == cur - 1)
    imp = jnp.where(jj > cur, -1e6, jnp.where(forced, 1e6, imp))
    rank = jnp.zeros((n_blk, tq), jnp.int32)
    for a in range(n_blk):
        row = imp[a:a + 1, :]
        beats = (row > imp) | ((row == imp) & (a < jj))
        rank = rank + beats.astype(jnp.int32)
    not_sel = jnp.where(rank < SEL_TOP_N, 0.0, 1.0).astype(BF16)

    q_aug = jnp.concatenate([q_t, jnp.concatenate([not_sel] * hpg, axis=1),
                             jnp.zeros((kw - dh - n_blk, rws), BF16)], axis=0)
    q_pad = jnp.concatenate([q_t, jnp.zeros((kw - dh, rws), BF16)], axis=0)


    n_win = WINDOW // tq
    tiles, scores = [], []
    for delta in range(n_win + 1):
        entry = {0: BT_DIAG, 1: BT_PREV, n_win: BT_WIN_EDGE}.get(delta, BT_FAR)
        if delta > 0:
            entry = jnp.where(i - delta >= 0, entry, BT_NONE)
        tiles.append(jnp.maximum(i - delta, 0))
        off = pl.multiple_of(tiles[-1] * tq, tq)
        scores.append(_dot(kw_ref[0, pl.ds(off, tq), :], q_pad) + bt_ref[0, entry])
    m_all = scores[0]
    for sc in scores[1:]:
        m_all = jnp.maximum(m_all, sc)
    m_w = jnp.max(m_all, axis=0, keepdims=True)
    acc_w = jnp.zeros((vwt_ref.shape[2], rws), F32)
    for kt, sc in zip(tiles, scores):
        acc_w = acc_w + _dot(vwt_ref[0, kt], jnp.exp2(sc - m_w).astype(BF16))
    o_win = acc_w[0:dh] / acc_w[dh:dh + 1]

    g_t = _dot_nt(_eye(GATE_PAD, F32), gate_ref[0], HI)
    sig = jax.nn.sigmoid(g_t[0:4 * hpg])
    gate = [jnp.concatenate([sig[br * hpg + hh:br * hpg + hh + 1] for hh in range(hpg)], axis=1)
            for br in range(3)]
    m_ref[...] = jnp.full(m_ref.shape, MASKED, F32)
    acc_ref[...] = jnp.zeros(acc_ref.shape, F32)

    def sel_update(tiles):
        scores = []
        for kt in tiles:
            off = pl.multiple_of(kt * tq, tq)
            scores.append(_dot(ks_ref[0, pl.ds(off, tq), :], q_aug)
                          + bt_ref[0, jnp.minimum(i - kt, BT_FAR)])
        s_max = scores[0]
        for sc in scores[1:]:
            s_max = jnp.maximum(s_max, sc)
        m_prev = m_ref[...]
        m_new = jnp.maximum(m_prev, jnp.max(s_max, axis=0, keepdims=True))
        acc = jnp.exp2(m_prev - m_new) * acc_ref[...]
        for kt, sc in zip(tiles, scores):
            acc = acc + _dot(vst_ref[0, kt], jnp.exp2(sc - m_new).astype(BF16))
        m_ref[...] = m_new
        acc_ref[...] = acc

    def sel_pair(j, carry):
        sel_update([2 * j, 2 * j + 1])
        return carry

    lax.fori_loop(0, (i + 1) // 2, sel_pair, 0)

    @pl.when(i % 2 == 0)
    def _():
        sel_update([i])

    o_sel = acc_ref[0:dh, :] / acc_ref[dh:dh + 1, :]

    y_t = (gate[0] * o_cmp + gate[1] * o_sel + gate[2] * o_win).astype(BF16)
    eye_q = _eye(tq, BF16)
    o_ref[0] = jnp.concatenate([_dot_nt(eye_q, yh) for yh in per_head(y_t)],
                               axis=1).astype(BF16)


def _nsa_attention(proj, kv, kv_t, kvc, kvc_t, bias_c, bias_t, *, bsz, seq):
    g, tq, hpg, dh = NSA_KV_GROUPS, NSA_TQ, NSA_HPG, NSA_HEAD_DIM
    n_rows = kvc.shape[2]
    nq = seq // tq
    group_w = hpg * dh
    kv_spec = lambda pos: pl.BlockSpec((1, seq, LANES), lambda gi, b, i: (b, 0, pos * g + gi))
    kvt_spec = lambda pos: pl.BlockSpec((1, nq, LANES, tq), lambda gi, b, i: (b, 0, pos * g + gi, 0))
    kv3 = kv.reshape(bsz, seq, KV_COLS)
    proj3 = proj.reshape(bsz, seq, PROJ_COLS)
    out = pl.pallas_call(
        _nsa_body,
        out_shape=jax.ShapeDtypeStruct((bsz, seq, NSA_DIM), BF16),
        grid=(g, bsz, nq),
        in_specs=[
            pl.BlockSpec((1, tq, group_w), lambda gi, b, i: (b, i, COL_QNSA // group_w + gi)),
            pl.BlockSpec((1, 1, n_rows, dh), lambda gi, b, i: (0, b * g + gi, 0, 0)),
            pl.BlockSpec((1, 1, dh, n_rows), lambda gi, b, i: (1, b * g + gi, 0, 0)),
            kv_spec(ROW_K_SEL), kvt_spec(T_V_SEL), kv_spec(ROW_K_WIN), kvt_spec(T_V_WIN),
            pl.BlockSpec((1, 1, n_rows, hpg * tq), lambda gi, b, i: (gi, i, 0, 0)),
            pl.BlockSpec((1, BT_COUNT, tq, hpg * tq), lambda gi, b, i: (gi, 0, 0, 0)),
            pl.BlockSpec((1, tq, GATE_PAD), lambda gi, b, i: (b, i, COL_GNSA // GATE_PAD + gi)),
        ],
        out_specs=pl.BlockSpec((1, tq, group_w), lambda gi, b, i: (b, i, gi)),
        scratch_shapes=[pltpu.VMEM((1, hpg * tq), F32),
                        pltpu.VMEM((LANES, hpg * tq), F32)],
        compiler_params=_params(("parallel", "parallel", "arbitrary")),
        name="nsa_attention",
    )(proj3, kvc, kvc_t, kv3, kv_t, kv3, kv_t, bias_c, bias_t, proj3)
    return out.reshape(bsz * seq, NSA_DIM)


def _mem_body(q_ref, mem_ref, g_ref, w_ref, o_ref, kv_ref):
    @pl.when(pl.program_id(1) == 0)
    def _():
        kv_ref[...] = _dot(_rms(mem_ref[0], g_ref[...]).astype(BF16), w_ref[...]).astype(BF16)

    outs = []
    for h in range(MEM_HEADS):
        sl = slice(h * MEM_HEAD_DIM, (h + 1) * MEM_HEAD_DIM)
        qh = (q_ref[:, sl] * (MEM_HEAD_DIM ** -0.5)).astype(BF16)
        s = _dot_nt(qh, kv_ref[:, sl])
        e = jnp.exp(s - jnp.max(s, axis=-1, keepdims=True))
        p = e / jnp.sum(e, axis=-1, keepdims=True)
        outs.append(_dot(p.astype(BF16), kv_ref[:, MEM_DIM + h * MEM_HEAD_DIM:
                                                MEM_DIM + (h + 1) * MEM_HEAD_DIM]))
    o_ref[...] = jnp.concatenate(outs, axis=1)


def _mem_attention(proj, mem, gain, w_kv, *, seq, tq=1024):
    t = proj.shape[0]
    bsz, m, d = mem.shape
    per_seq = seq // tq
    return pl.pallas_call(
        _mem_body,
        out_shape=jax.ShapeDtypeStruct((t, MEM_DIM), F32),
        grid=(bsz, per_seq),
        in_specs=[
            pl.BlockSpec((tq, MEM_DIM), lambda b, j: (b * per_seq + j, COL_QMEM // MEM_DIM)),
            pl.BlockSpec((1, m, d), lambda b, j: (b, 0, 0)),
            pl.BlockSpec((1, d), lambda b, j: (0, 0)),
            pl.BlockSpec((d, 2 * MEM_DIM), lambda b, j: (0, 0), pipeline_mode=pl.Buffered(1)),
        ],
        out_specs=pl.BlockSpec((tq, MEM_DIM), lambda b, j: (b * per_seq + j, 0)),
        scratch_shapes=[pltpu.VMEM((m, 2 * MEM_DIM), BF16)],
        compiler_params=_params(("parallel", "arbitrary")),
        name="mem_attention",
    )(proj, mem, gain, w_kv)


def _merge_body(x_ref, gain_ref, yr_ref, yn_ref, ym_ref, wg_ref, wr_ref, wn_ref, wm_ref, wo_ref,
                o_ref):
    d = x_ref.shape[1]
    x = x_ref[...]
    gates = jax.nn.sigmoid(_dot(_rms(x, gain_ref[...]).astype(BF16), wg_ref[...]))
    merged = (gates[:, 0:d] * _dot(yr_ref[...].astype(BF16), wr_ref[...])
              + gates[:, d:2 * d] * _dot(yn_ref[...], wn_ref[...])
              + gates[:, 2 * d:3 * d] * _dot(ym_ref[...].astype(BF16), wm_ref[...]))
    o_ref[...] = x + _dot(merged.astype(BF16), wo_ref[...])


def _merge(x, gain, y_rwkv, y_nsa, y_mem, w_g, w_r, w_n, w_m, w_o, *, tm=512):
    t, d = x.shape
    row = lambda i: (i, 0)
    const = lambda i: (0, 0)
    return pl.pallas_call(
        _merge_body,
        out_shape=jax.ShapeDtypeStruct((t, d), F32),
        grid=(t // tm,),
        in_specs=[
            pl.BlockSpec((tm, d), row),
            pl.BlockSpec((1, d), const),
            pl.BlockSpec((tm, RWKV_DIM), row),
            pl.BlockSpec((tm, NSA_DIM), row),
            pl.BlockSpec((tm, MEM_DIM), row),
            pl.BlockSpec((d, N_BRANCH * d), const),
            pl.BlockSpec((RWKV_DIM, d), const),
            pl.BlockSpec((NSA_DIM, d), const),
            pl.BlockSpec((MEM_DIM, d), const),
            pl.BlockSpec((d, d), const),
        ],
        out_specs=pl.BlockSpec((tm, d), row),
        compiler_params=_params(("parallel",)),
        name="merge",
    )(x, gain, y_rwkv, y_nsa, y_mem, w_g, w_r, w_n, w_m, w_o)


def _row(a):
    return a.reshape(1, -1)


def _in_proj(x, p, l, seq):
    d = x.shape[1]
    g, hpg, dh = NSA_KV_GROUPS, NSA_HPG, NSA_HEAD_DIM
    w_in = p['w_in'][l]
    o = 0
    parts = {}
    for name, size in (('rwkv', RWKV_PROJ), ('q', NSA_DIM), ('kv', KV_KINDS * NSA_KV_DIM),
                       ('g_nsa', 3 * NSA_HEADS), ('q_mem', MEM_DIM)):
        parts[name] = w_in[:, o:o + size]
        o += size
    gates = parts['g_nsa'].reshape(d, 3, g, hpg).transpose(0, 2, 1, 3).reshape(d, g, 3 * hpg)
    gates = jnp.pad(gates, ((0, 0), (0, 0), (0, GATE_PAD - 3 * hpg))).reshape(d, g * GATE_PAD)
    w_other = jnp.concatenate([parts['q'], parts['q_mem'], gates], axis=1).astype(BF16)
    w_kv = jnp.pad(parts['kv'].reshape(d, KV_KINDS, g, dh), ((0, 0), (0, 0), (0, 0), (0, LANES - dh)))
    w_row = w_kv[:, KV_ROW_KINDS, :, :].reshape(d, KV_COLS).astype(BF16)
    w_t = w_kv[:, KV_T_KINDS, :, :].reshape(d, KV_T_ROWS).T.astype(BF16)
    w_lora = jnp.zeros((LORA_DIM, 3 * RWKV_DIM), F32)
    w_lora = w_lora.at[0:DECAY_LORA, 0:RWKV_DIM].set(p['rwkv_w2'][l])
    w_lora = w_lora.at[DECAY_LORA:DECAY_LORA + AAA_LORA, RWKV_DIM:2 * RWKV_DIM].set(p['rwkv_a2'][l])
    w_lora = w_lora.at[DECAY_LORA + AAA_LORA:, 2 * RWKV_DIM:].set(p['rwkv_g2'][l])
    proj, kv, kv_t, *rwkv_in = _in_proj_call(
        x, seq, _row(p['mix_norm'][l]), parts['rwkv'].astype(BF16), w_other, w_row, w_t,
        _row(p['rwkv_mu'][l]), w_lora.astype(BF16), _row(p['rwkv_w0'][l]), _row(p['rwkv_a0'][l]),
        _row(p['rwkv_k_k'][l]), _row(p['rwkv_k_a'][l]))
    return proj, kv, kv_t, rwkv_in


def _rwkv_branch(rwkv_in, p, l, bsz, seq):
    return _rwkv_scan(*rwkv_in, _row(p['rwkv_r_k'][l]), _row(p['rwkv_gn_gain'][l]),
                      _row(p['rwkv_gn_bias'][l]), batch=bsz, seq=seq)


def _nsa_branch(proj, kv, kv_t, bias_c, bias_t, p, l, bsz, seq):
    w1 = jnp.stack([p['cmp_k_w1'][l], p['cmp_v_w1'][l]]).astype(BF16)
    pe = jnp.stack([p['cmp_pe_k'][l].reshape(1, -1), p['cmp_pe_v'][l].reshape(1, -1)])
    w2 = jnp.stack([p['cmp_k_w2'][l], p['cmp_v_w2'][l]]).astype(BF16)
    kvc, kvc_t = _compress(kv, w1, pe, w2, bsz=bsz, seq=seq)
    return _nsa_attention(proj, kv, kv_t, kvc, kvc_t, bias_c, bias_t, bsz=bsz, seq=seq)


def _mem_branch(proj, mem, p, l):
    w_kv = jnp.concatenate([p['mem_w_k'][l], p['mem_w_v'][l]], axis=1).astype(BF16)
    return _mem_attention(proj, mem, _row(p['mem_norm'][l]), w_kv, seq=proj.shape[0] // mem.shape[0])


def _layer(x, mem, l, bias_c, bias_t, p):
    bsz, seq, d = x.shape
    t = bsz * seq
    row = _row
    x = x.reshape(t, d)

    x = _ffn(x, row(p['ffn1_norm'][l]), p['ffn1_w_gate'][l].astype(BF16),
             p['ffn1_w_up'][l].astype(BF16), p['ffn1_w_down'][l].astype(BF16),
             row(p['final_norm']), final=False)

    proj, kv, kv_t, rwkv_in = _in_proj(x, p, l, seq)
    y_rwkv = _rwkv_branch(rwkv_in, p, l, bsz, seq)
    y_nsa = _nsa_branch(proj, kv, kv_t, bias_c, bias_t, p, l, bsz, seq)
    y_mem = _mem_branch(proj, mem, p, l)

    w_gate = p['w_in'][l][:, -N_BRANCH * d:].astype(BF16)
    x = _merge(x, row(p['mix_norm'][l]), y_rwkv, y_nsa, y_mem, w_gate,
               p['w_br_rwkv'][l].astype(BF16), p['w_br_nsa'][l].astype(BF16),
               p['w_br_mem'][l].astype(BF16), p['w_out'][l].astype(BF16))

    last = l == p['ffn1_norm'].shape[0] - 1
    x = _ffn(x, row(p['ffn2_norm'][l]), p['ffn2_w_gate'][l].astype(BF16),
             p['ffn2_w_up'][l].astype(BF16), p['ffn2_w_down'][l].astype(BF16),
             row(p['final_norm']), final=last)
    return x.reshape(bsz, seq, d)


def kernel(x, mem, ffn1_norm, ffn1_w_gate, ffn1_w_up, ffn1_w_down, mix_norm, w_in, rwkv_mu, rwkv_w0, rwkv_w2, rwkv_a0, rwkv_a2, rwkv_g2, rwkv_k_k, rwkv_k_a, rwkv_r_k, rwkv_gn_gain, rwkv_gn_bias, cmp_pe_k, cmp_k_w1, cmp_k_w2, cmp_pe_v, cmp_v_w1, cmp_v_w2, rel_bias, mem_norm, mem_w_k, mem_w_v, w_br_rwkv, w_br_nsa, w_br_mem, w_out, ffn2_norm, ffn2_w_gate, ffn2_w_up, ffn2_w_down, final_norm):
    p = dict(ffn1_norm=ffn1_norm, ffn1_w_gate=ffn1_w_gate, ffn1_w_up=ffn1_w_up,
             ffn1_w_down=ffn1_w_down, mix_norm=mix_norm, w_in=w_in, rwkv_mu=rwkv_mu,
             rwkv_w0=rwkv_w0, rwkv_w2=rwkv_w2, rwkv_a0=rwkv_a0, rwkv_a2=rwkv_a2, rwkv_g2=rwkv_g2,
             rwkv_k_k=rwkv_k_k, rwkv_k_a=rwkv_k_a, rwkv_r_k=rwkv_r_k, rwkv_gn_gain=rwkv_gn_gain,
             rwkv_gn_bias=rwkv_gn_bias, cmp_pe_k=cmp_pe_k, cmp_k_w1=cmp_k_w1, cmp_k_w2=cmp_k_w2,
             cmp_pe_v=cmp_pe_v, cmp_v_w1=cmp_v_w1, cmp_v_w2=cmp_v_w2, mem_norm=mem_norm,
             mem_w_k=mem_w_k, mem_w_v=mem_w_v, w_br_rwkv=w_br_rwkv, w_br_nsa=w_br_nsa,
             w_br_mem=w_br_mem, w_out=w_out, ffn2_norm=ffn2_norm, ffn2_w_gate=ffn2_w_gate,
             ffn2_w_up=ffn2_w_up, ffn2_w_down=ffn2_w_down, final_norm=final_norm)
    bias_c, bias_t = _bias_tables(rel_bias, x.shape[1])
    for l in range(ffn1_norm.shape[0]):
        x = _layer(x, mem, l, bias_c, bias_t, p)
    return x
```

```python
import functools
import math

import jax
import jax.numpy as jnp
from jax import lax
from jax.experimental import pallas as pl
from jax.experimental.pallas import tpu as pltpu

F32 = jnp.float32
BF16 = jnp.bfloat16
HI = lax.Precision.HIGHEST

NORM_EPS = 1e-6
RWKV_HEADS = 8
RWKV_HEAD_DIM = 64
RWKV_DIM = RWKV_HEADS * RWKV_HEAD_DIM
DECAY_LORA = 64
AAA_LORA = 64
GATE_LORA = 128
LORA_DIM = DECAY_LORA + AAA_LORA + GATE_LORA
RWKV_GN_EPS = 64e-5
RWKV_PROJ = 3 * RWKV_DIM + LORA_DIM
NSA_HEADS = 8
NSA_KV_GROUPS = 2
NSA_HPG = NSA_HEADS // NSA_KV_GROUPS
NSA_HEAD_DIM = 64
NSA_DIM = NSA_HEADS * NSA_HEAD_DIM
NSA_KV_DIM = NSA_KV_GROUPS * NSA_HEAD_DIM
CMP_LEN = 32
CMP_STRIDE = 16
CMP_HIDDEN = 256
SEL_BLOCK = 64
SEL_TOP_N = 16
WINDOW = 512
REL_BUCKETS = 32
REL_MAX_DIST = 128
MEM_HEADS = 4
MEM_HEAD_DIM = 128
MEM_DIM = MEM_HEADS * MEM_HEAD_DIM
N_BRANCH = 3

LANES = 128
GATE_PAD = LANES
COL_QNSA = 0
COL_QMEM = COL_QNSA + NSA_DIM
COL_GNSA = COL_QMEM + MEM_DIM
PROJ_COLS = COL_GNSA + NSA_KV_GROUPS * GATE_PAD
assert COL_QMEM % MEM_DIM == 0 and COL_GNSA % GATE_PAD == 0
KV_KINDS = 6
KV_ROW_KINDS = (0, 1, 2, 4)
KV_T_KINDS = (3, 5)
KV_COLS = len(KV_ROW_KINDS) * NSA_KV_GROUPS * LANES
KV_T_ROWS = len(KV_T_KINDS) * NSA_KV_GROUPS * LANES
ROW_K_SEL, ROW_K_WIN = KV_ROW_KINDS.index(2), KV_ROW_KINDS.index(4)
T_V_SEL, T_V_WIN = KV_T_KINDS.index(3), KV_T_KINDS.index(5)
LOG2E = 1.4426950408889634

RWKV_CHUNK = 64
RWKV_INV_BLOCK = 16
NSA_TQ = 256
MASKED = -1e30
BT_DIAG, BT_PREV, BT_FAR, BT_WIN_EDGE, BT_NONE, BT_COUNT = 0, 1, 2, 3, 4, 5
V7X_VMEM_BYTES = 64 * 1024 * 1024
VMEM_LIMIT = V7X_VMEM_BYTES - 8 * 1024 * 1024


def _dot(a, b, precision=None):
    return jnp.dot(a, b, preferred_element_type=F32, precision=precision)


def _dot_nt(a, b, precision=None):
    return lax.dot_general(a, b, (((1,), (1,)), ((), ())), preferred_element_type=F32,
                           precision=precision)


def _params(semantics):
    return pltpu.CompilerParams(dimension_semantics=semantics, vmem_limit_bytes=VMEM_LIMIT)


def _rms(x, g):
    return x * lax.rsqrt(jnp.mean(x * x, axis=-1, keepdims=True) + NORM_EPS) * g


def _ffn_body(x_ref, g_ref, wg_ref, wu_ref, wd_ref, fg_ref, o_ref, *, tf, final):
    x = x_ref[...]
    h = _rms(x, g_ref[...]).astype(BF16)
    acc = jnp.zeros(x.shape, F32)
    for j in range(wg_ref.shape[1] // tf):
        cols = slice(j * tf, (j + 1) * tf)
        act = (jax.nn.silu(_dot(h, wg_ref[:, cols])) * _dot(h, wu_ref[:, cols])).astype(BF16)
        acc = acc + _dot(act, wd_ref[cols, :])
    y = x + 0.5 * acc
    if final:
        y = _rms(y, fg_ref[...])
    o_ref[...] = y


def _ffn(x, gain, wg, wu, wd, final_gain, *, final, tm=1024, tf=256):
    t, d = x.shape
    f = wg.shape[1]
    resident = lambda shape: pl.BlockSpec(shape, lambda i: (0, 0), pipeline_mode=pl.Buffered(1))
    return pl.pallas_call(
        functools.partial(_ffn_body, tf=tf, final=final),
        out_shape=jax.ShapeDtypeStruct((t, d), F32),
        grid=(t // tm,),
        in_specs=[
            pl.BlockSpec((tm, d), lambda i: (i, 0)),
            pl.BlockSpec((1, d), lambda i: (0, 0)),
            resident((d, f)),
            resident((d, f)),
            resident((f, d)),
            pl.BlockSpec((1, d), lambda i: (0, 0)),
        ],
        out_specs=pl.BlockSpec((tm, d), lambda i: (i, 0)),
        compiler_params=_params(("parallel",)),
        name="ffn_final" if final else "ffn",
    )(x, gain, wg, wu, wd, final_gain)


def _write_kv(h, w_ref, wt_ref, o_ref, ot_ref, *, seq):
    dh, tk, tm = NSA_HEAD_DIM, NSA_TQ, h.shape[0]
    y = _dot(h, w_ref[...])
    row = lax.broadcasted_iota(jnp.int32, (tm, LANES), 0)
    lane = lax.broadcasted_iota(jnp.int32, (tm, LANES), 1)
    pos = (pl.program_id(0) * tm) % seq + row
    block_mark = jnp.where(lane - dh == (pos >> (SEL_BLOCK.bit_length() - 1)), MASKED, 0.0)
    for tile in range(KV_COLS // LANES):
        part = y[:, tile * LANES:(tile + 1) * LANES]
        if tile // NSA_KV_GROUPS == ROW_K_SEL:
            part = part + block_mark
        o_ref[:, tile * LANES:(tile + 1) * LANES] = part.astype(BF16)

    y_t = _dot_nt(wt_ref[...], h)
    row_t = lax.broadcasted_iota(jnp.int32, y_t.shape, 0)
    y_t = (y_t + jnp.where((row_t & (LANES - 1)) == dh, 1.0, 0.0)).astype(BF16)
    for c in range(tm // tk):
        ot_ref[0, c] = y_t[:, c * tk:(c + 1) * tk]


def _in_proj_body(x_ref, xp_ref, g_ref, wr_ref, wo_ref, wkv_ref, wkvt_ref, mu_ref, wl_ref, w0_ref,
                  a0_ref, kk_ref, ka_ref, proj_o, kv_o, kvt_o, r_o, k_o, v_o, kk_o, b_o, lw_o, g_o,
                  *, tiles_per_seq, seq):
    i = pl.program_id(0)
    gain = g_ref[...]
    h = _rms(x_ref[...], gain).astype(BF16)
    proj_o[...] = _dot(h, wo_ref[...])
    _write_kv(h, wkv_ref, wkvt_ref, kv_o, kvt_o, seq=seq)
    n_up = xp_ref.shape[0]
    h_up = _rms(xp_ref[...], gain).astype(BF16)
    p_all = _dot(jnp.concatenate([h_up, h], axis=0), wr_ref[...])
    p = p_all[n_up:]
    keep = jnp.where(i % tiles_per_seq == 0, 0.0, 1.0)
    prev_last = p_all[n_up - 1:n_up, :] * keep
    rows = lax.broadcasted_iota(jnp.int32, p.shape, 0)
    shifted = jnp.where(rows == 0, prev_last, pltpu.roll(p, 1, 0))
    x = p + (shifted - p) * mu_ref[...]

    r = x[:, 0:RWKV_DIM]
    k = x[:, RWKV_DIM:2 * RWKV_DIM]
    v = x[:, 2 * RWKV_DIM:3 * RWKV_DIM]
    s = x[:, 3 * RWKV_DIM:RWKV_PROJ]
    lane = lax.broadcasted_iota(jnp.int32, s.shape, 1)
    z = jnp.where(lane < DECAY_LORA, jnp.tanh(s),
                  jnp.where(lane < DECAY_LORA + AAA_LORA, s, jax.nn.sigmoid(s)))
    lo = _dot(z.astype(BF16), wl_ref[...])
    a = jax.nn.sigmoid(a0_ref[...] + lo[:, RWKV_DIM:2 * RWKV_DIM])

    kkr = k * kk_ref[...]
    sq = kkr * kkr
    sq_hi = sq.astype(BF16)
    sq_lo = (sq - sq_hi.astype(F32)).astype(BF16)
    shift = RWKV_HEAD_DIM.bit_length() - 1
    same_head = ((lax.broadcasted_iota(jnp.int32, (RWKV_DIM, RWKV_DIM), 0) >> shift)
                 == (lax.broadcasted_iota(jnp.int32, (RWKV_DIM, RWKV_DIM), 1) >> shift)).astype(BF16)
    ssq = _dot(sq_hi, same_head) + _dot(sq_lo, same_head)
    kk = kkr / jnp.maximum(jnp.sqrt(ssq), 1e-12)

    r_o[...] = r
    k_o[...] = k * (1.0 + (a - 1.0) * ka_ref[...])
    v_o[...] = v
    kk_o[...] = kk
    b_o[...] = kk * a
    lw_o[...] = -math.exp(-0.5) * jax.nn.sigmoid(w0_ref[...] + lo[:, 0:RWKV_DIM])
    g_o[...] = lo[:, 2 * RWKV_DIM:3 * RWKV_DIM]


def _in_proj_call(x, seq, gain, w_rwkv, w_other, w_kv, w_kv_t, mu, w_lora, w0, a0, k_k, k_a, *,
                  tm=512):
    t, d = x.shape
    tk = NSA_TQ
    per_seq = seq // tm
    row = lambda i: (i, 0)
    const = lambda i: (0, 0)
    resident = lambda shape: pl.BlockSpec(shape, const, pipeline_mode=pl.Buffered(1))
    vec = pl.BlockSpec((1, RWKV_DIM), const)
    tok = jax.ShapeDtypeStruct((t, RWKV_DIM), F32)
    return pl.pallas_call(
        functools.partial(_in_proj_body, tiles_per_seq=per_seq, seq=seq),
        out_shape=[jax.ShapeDtypeStruct((t, PROJ_COLS), F32),
                   jax.ShapeDtypeStruct((t, KV_COLS), BF16),
                   jax.ShapeDtypeStruct((t // seq, seq // tk, KV_T_ROWS, tk), BF16)] + [tok] * 7,
        grid=(t // tm,),
        in_specs=[
            pl.BlockSpec((tm, d), row),
            pl.BlockSpec((16, d), lambda i: (jnp.maximum(i * (tm // 16) - 1, 0), 0)),
            pl.BlockSpec((1, d), const),
            resident((d, RWKV_PROJ)),
            resident((d, PROJ_COLS)),
            resident((d, KV_COLS)),
            resident((KV_T_ROWS, d)),
            pl.BlockSpec((1, RWKV_PROJ), const),
            resident((LORA_DIM, 3 * RWKV_DIM)),
            vec, vec, vec, vec,
        ],
        out_specs=[pl.BlockSpec((tm, PROJ_COLS), row),
                   pl.BlockSpec((tm, KV_COLS), row),
                   pl.BlockSpec((1, tm // tk, KV_T_ROWS, tk),
                                lambda i: (i // per_seq, i % per_seq, 0, 0))]
        + [pl.BlockSpec((tm, RWKV_DIM), row)] * 7,
        compiler_params=_params(("parallel",)),
        name="in_proj",
    )(x, x, gain, w_rwkv, w_other, w_kv, w_kv_t, mu, w_lora, w0, a0, k_k, k_a)


def _rwkv_scan_body(r_ref, k_ref, v_ref, kk_ref, b_ref, lw_ref, g_ref, rk_ref, gg_ref, gb_ref,
                    o_ref, st_ref):
    c_sz, n, nh = RWKV_CHUNK, RWKV_HEAD_DIM, RWKV_HEADS

    @pl.when(pl.program_id(1) == 0)
    def _():
        st_ref[...] = jnp.zeros_like(st_ref)

    ri = lax.broadcasted_iota(jnp.int32, (c_sz, c_sz), 0)
    ci = lax.broadcasted_iota(jnp.int32, (c_sz, c_sz), 1)
    incl = ci <= ri
    eye_b = (ci == ri).astype(BF16)
    row2 = lax.broadcasted_iota(jnp.int32, (c_sz, 2 * c_sz), 0)
    lane2 = lax.broadcasted_iota(jnp.int32, (c_sz, 2 * c_sz), 1)
    right_half = lane2 >= c_sz
    zeros_b = jnp.zeros((c_sz, n), BF16)
    inv_shift = RWKV_INV_BLOCK.bit_length() - 1

    rows = []
    for bb in range(st_ref.shape[0]):
        lw = lw_ref[bb]
        cum = _dot(incl.astype(F32), lw, HI)
        cum_last = cum[c_sz - 1:c_sz, :]
        r, k, v, b = r_ref[bb], k_ref[bb], v_ref[bb], b_ref[bb]
        p_inv = jnp.exp(-cum)
        p_end = jnp.exp(cum_last - cum)
        rows.append(dict(
            left=jnp.concatenate([(-(kk_ref[bb] * jnp.exp(cum - lw))).astype(BF16),
                                  (r * jnp.exp(cum)).astype(BF16)], axis=0),
            bt=(b * p_inv).astype(BF16), kt=(k * p_inv).astype(BF16),
            bh=(b * p_end).astype(BF16), kh=(k * p_end).astype(BF16),
            v=v, v_b=v.astype(BF16), d_p=jnp.exp(cum_last), rk=r * k * rk_ref[...]))

    units = [(bb, h) for bb in range(len(rows)) for h in range(nh)]
    col = lambda name, u: rows[u[0]][name][:, u[1] * n:(u[1] + 1) * n]
    a_all = [_dot_nt(col('left', u), jnp.concatenate([col('bt', u), col('kt', u)], axis=0))
             for u in units]
    key2 = jnp.where(right_half, lane2 - c_sz, lane2)
    w_u = [jnp.where(right_half & (key2 < row2), a[:c_sz], 0.0).astype(BF16) for a in a_all]
    w_y = [jnp.where(key2 <= row2, a[c_sz:], 0.0).astype(BF16) for a in a_all]

    same_block = (row2 >> inv_shift) == (key2 >> inv_shift)
    x = [jnp.where((lane2 < row2) & same_block, a[:c_sz], jnp.where(lane2 == row2 + c_sz, 1.0, 0.0))
         for a in a_all]
    for _ in range(inv_shift):
        hi = [xu.astype(BF16) for xu in x]
        lo = [(xu - h_.astype(F32)).astype(BF16) for xu, h_ in zip(x, hi)]
        x = [_dot(h_[:, :c_sz], h_) + _dot(h_[:, :c_sz], l_) + _dot(l_[:, :c_sz], h_)
             + jnp.where(right_half, xu, 0.0) for xu, h_, l_ in zip(x, hi, lo)]
    x_b = [xu.astype(BF16) for xu in x]
    solve_diag = lambda j, z: _dot(x_b[j], jnp.concatenate([zeros_b, z.astype(BF16)], axis=0))
    q_b = [solve_diag(j, jnp.where((ci < ri) & ((ri >> inv_shift) != (ci >> inv_shift)),
                                   a[:c_sz, :c_sz], 0.0)).astype(BF16)
           for j, a in enumerate(a_all)]

    s0 = [st_ref[bb, h] for bb, h in units]
    ls0 = [_dot_nt(col('left', u), s0[j].astype(BF16)) for j, u in enumerate(units)]
    rhs = [ls0[j][:c_sz] + _dot(w_u[j], jnp.concatenate([zeros_b, col('v_b', u)], axis=0))
           for j, u in enumerate(units)]
    g0 = [solve_diag(j, rhs[j]) for j in range(len(units))]
    u_f = g0
    for _ in range(c_sz // RWKV_INV_BLOCK - 1):
        u_f = [g0[j] + _dot(q_b[j], u_f[j].astype(BF16)) for j in range(len(units))]
    u_b = [uj.astype(BF16) for uj in u_f]
    uv = [jnp.concatenate([u_b[j], col('v_b', u)], axis=0) for j, u in enumerate(units)]
    y = [ls0[j][c_sz:] + _dot(w_y[j], uv[j]) for j in range(len(units))]
    uv_t = [_dot_nt(eye_b, uv_j).astype(BF16) for uv_j in uv]
    for j, u in enumerate(units):
        st_ref[u[0], u[1]] = (s0[j] * col('d_p', u)
                              + _dot(uv_t[j], jnp.concatenate([col('bh', u), col('kh', u)], axis=0)))

    for bb in range(len(rows)):
        outs = []
        for h in range(nh):
            sl = slice(h * n, (h + 1) * n)
            yh = y[bb * nh + h]
            mean = jnp.mean(yh, axis=-1, keepdims=True)
            var = jnp.mean(jnp.square(yh - mean), axis=-1, keepdims=True)
            yn = (yh - mean) * lax.rsqrt(var + RWKV_GN_EPS)
            yn = yn * gg_ref[:, sl] + gb_ref[:, sl]
            bonus = jnp.sum(rows[bb]['rk'][:, sl], axis=-1, keepdims=True) * rows[bb]['v'][:, sl]
            outs.append((yn + bonus) * g_ref[bb, :, sl])
        o_ref[bb] = jnp.concatenate(outs, axis=1)


def _rwkv_scan(r, k, v, kk, b, lw, g, r_k, gn_gain, gn_bias, *, batch, seq, nb=4):
    t = r.shape[0]
    nc = seq // RWKV_CHUNK
    tok = pl.BlockSpec((nb, RWKV_CHUNK, RWKV_DIM), lambda bi, c: (bi, c, 0))
    par = pl.BlockSpec((1, RWKV_DIM), lambda bi, c: (0, 0))
    per_batch = lambda a: a.reshape(batch, seq, RWKV_DIM)
    out = pl.pallas_call(
        _rwkv_scan_body,
        out_shape=jax.ShapeDtypeStruct((batch, seq, RWKV_DIM), F32),
        grid=(batch // nb, nc),
        in_specs=[tok] * 7 + [par] * 3,
        out_specs=tok,
        scratch_shapes=[pltpu.VMEM((nb, RWKV_HEADS, RWKV_HEAD_DIM, RWKV_HEAD_DIM), F32)],
        compiler_params=_params(("parallel", "arbitrary")),
        name="rwkv_scan",
    )(*(per_batch(a) for a in (r, k, v, kk, b, lw, g)), r_k, gn_gain, gn_bias)
    return out.reshape(t, RWKV_DIM)


def _compress_body(x_ref, w1s_ref, w1_ref, pe_ref, w2_ref, w2t_ref, o_ref, ot_ref):
    n_rows = x_ref.shape[1]
    for kind in range(2):
        pe = jnp.broadcast_to(pe_ref[kind], (8, pe_ref.shape[2])).astype(BF16)
        pe_term = _dot(pe, w1_ref[kind])[0:1, :]
        for gi in range(NSA_KV_GROUPS):
            tile = kind * NSA_KV_GROUPS + gi
            both = jnp.zeros((n_rows, 2 * CMP_HIDDEN), F32)
            for l in range(CMP_STRIDE):
                both = both + _dot(x_ref[0, :, l, tile * LANES:(tile + 1) * LANES], w1s_ref[kind, l])
            second_next = pltpu.roll(both[:, CMP_HIDDEN:], n_rows - 1, 0)
            hid = both[:, :CMP_HIDDEN] + second_next + pe_term
            act = jax.nn.gelu(hid).astype(BF16)
            o_ref[kind, gi] = _dot(act, w2_ref[kind])
            ot_ref[kind, gi] = _dot_nt(w2t_ref[kind], act)


def _compress(kv, w1, pe, w2, *, bsz, seq):
    g, dh = NSA_KV_GROUPS, NSA_HEAD_DIM
    rows = seq // CMP_STRIDE
    w1r = jnp.pad(w1.reshape(2, CMP_LEN, dh, CMP_HIDDEN), ((0, 0), (0, 0), (0, LANES - dh), (0, 0)))
    w1s = jnp.concatenate([w1r[:, :CMP_STRIDE], w1r[:, CMP_STRIDE:]], axis=3)
    whole = lambda a: pl.BlockSpec(a.shape, lambda b: (0,) * a.ndim)
    w2t = w2.transpose(0, 2, 1)
    return pl.pallas_call(
        _compress_body,
        out_shape=[jax.ShapeDtypeStruct((2, bsz * g, rows, dh), F32),
                   jax.ShapeDtypeStruct((2, bsz * g, dh, rows), F32)],
        grid=(bsz,),
        in_specs=[pl.BlockSpec((1, rows, CMP_STRIDE, 2 * g * LANES), lambda b: (b, 0, 0, 0)),
                  whole(w1s), whole(w1), whole(pe), whole(w2), whole(w2t)],
        out_specs=[pl.BlockSpec((2, g, rows, dh), lambda b: (0, b, 0, 0)),
                   pl.BlockSpec((2, g, dh, rows), lambda b: (0, b, 0, 0))],
        compiler_params=_params(("parallel",)),
        name="nsa_compress",
    )(kv.reshape(bsz, rows, CMP_STRIDE, KV_COLS), w1s, w1, pe, w2, w2t)


def _t5_bucket(dist):
    n = jnp.maximum(dist, 0)
    exact = REL_BUCKETS // 2
    nf = jnp.maximum(n, 1).astype(F32)
    scaled = jnp.log(nf / exact) / math.log(REL_MAX_DIST / exact) * (REL_BUCKETS - exact)
    large = exact + jnp.floor(scaled).astype(jnp.int32)
    large = jnp.minimum(large, REL_BUCKETS - 1)
    return jnp.where(n < exact, n, large)


def _bias_body(tab_ref, bc_ref, bt_ref, *, seq, n_cmp_pad):
    h = pl.program_id(0)
    tq = NSA_TQ

    def lookup(dist):
        bucket = _t5_bucket(dist)
        out = jnp.zeros(dist.shape, F32)
        for bkt in range(REL_BUCKETS):
            out = jnp.where(bucket == bkt, tab_ref[bkt, h] * LOG2E, out)
        return out

    far = tab_ref[REL_BUCKETS - 1, h] * LOG2E
    assert tq + 1 >= REL_MAX_DIST and WINDOW - tq + 1 >= REL_MAX_DIST

    key = lax.broadcasted_iota(jnp.int32, (tq, tq), 0)
    qry = lax.broadcasted_iota(jnp.int32, (tq, tq), 1)
    bt_ref[0, BT_DIAG] = jnp.where(qry >= key, lookup(qry - key), MASKED)
    bt_ref[0, BT_PREV] = lookup(tq + qry - key)
    bt_ref[0, BT_FAR] = jnp.full((tq, tq), far, F32)
    bt_ref[0, BT_WIN_EDGE] = jnp.where(qry < key, far, MASKED)
    bt_ref[0, BT_NONE] = jnp.full((tq, tq), MASKED, F32)

    per_tile = tq // CMP_STRIDE
    pad = 16
    assert pad * CMP_STRIDE >= REL_MAX_DIST + CMP_LEN - 1 and pad % 8 == 0
    band = per_tile + pad
    cmp_end = lax.broadcasted_iota(jnp.int32, (band, tq), 0) * CMP_STRIDE + CMP_LEN - 1
    qry_c = lax.broadcasted_iota(jnp.int32, (band, tq), 1)

    def cmp_tile(i, carry):
        start = pl.multiple_of(jnp.maximum(i * per_tile - pad, 0), 8)
        bc_ref[0, i] = jnp.full((n_cmp_pad, tq), far, F32)
        bc_ref[0, i, pl.ds(start, band), :] = lookup(i * tq + qry_c - (start * CMP_STRIDE + cmp_end))
        return carry

    lax.fori_loop(0, seq // tq, cmp_tile, 0)


def _bias_tables(rel_bias, seq):
    g, hpg, tq = NSA_KV_GROUPS, NSA_HPG, NSA_TQ
    n_cmp_pad = seq // CMP_STRIDE
    nq = seq // tq
    return pl.pallas_call(
        functools.partial(_bias_body, seq=seq, n_cmp_pad=n_cmp_pad),
        out_shape=[jax.ShapeDtypeStruct((g, nq, n_cmp_pad, hpg * tq), F32),
                   jax.ShapeDtypeStruct((g, BT_COUNT, tq, hpg * tq), F32)],
        grid=(NSA_HEADS,),
        in_specs=[pl.BlockSpec(memory_space=pltpu.SMEM)],
        out_specs=[pl.BlockSpec((1, nq, n_cmp_pad, tq), lambda h: (h // hpg, 0, 0, h % hpg)),
                   pl.BlockSpec((1, BT_COUNT, tq, tq), lambda h: (h // hpg, 0, 0, h % hpg))],
        compiler_params=_params(("parallel",)),
        name="nsa_bias",
    )(rel_bias)


def _eye(n, dtype):
    return (lax.broadcasted_iota(jnp.int32, (n, n), 0)
            == lax.broadcasted_iota(jnp.int32, (n, n), 1)).astype(dtype)


def _nsa_body(q_ref, kc_ref, vct_ref, ks_ref, vst_ref, kw_ref, vwt_ref, bc_ref, bt_ref, gate_ref,
              o_ref, m_ref, acc_ref):
    tq, hpg, dh = NSA_TQ, NSA_HPG, NSA_HEAD_DIM
    rws = hpg * tq
    n_blk_log2 = SEL_BLOCK.bit_length() - 1
    n_blk = ks_ref.shape[1] // SEL_BLOCK
    n_cmp_pad = kc_ref.shape[2]
    kw = ks_ref.shape[2]
    i = pl.program_id(2)

    def per_head(x):
        return [x[:, hh * tq:(hh + 1) * tq] for hh in range(hpg)]

    xq = (q_ref[0] * (dh ** -0.5 * LOG2E)).astype(BF16)
    eye_d = _eye(dh, BF16)
    q_t = jnp.concatenate([_dot_nt(eye_d, xq[:, hh * dh:(hh + 1) * dh]) for hh in range(hpg)],
                          axis=1).astype(BF16)

    cmp_id = lax.broadcasted_iota(jnp.int32, (n_cmp_pad, rws), 0)
    t_pos = i * tq + (lax.broadcasted_iota(jnp.int32, (n_cmp_pad, rws), 1) & (tq - 1))
    valid = (t_pos - (cmp_id * CMP_STRIDE + CMP_LEN - 1) >= 0) & (cmp_id < n_cmp_pad - 1)
    s = jnp.where(valid, _dot(kc_ref[0, 0].astype(BF16), q_t) + bc_ref[0, 0], MASKED)
    e = jnp.where(valid, jnp.exp2(s - jnp.max(s, axis=0, keepdims=True)), 0.0)
    den = jnp.sum(e, axis=0, keepdims=True)
    p_c = e / jnp.where(den > 0.0, den, 1.0)
    o_cmp = _dot(vct_ref[0, 0].astype(BF16), p_c.astype(BF16))

    p_heads = per_head(p_c)
    p_sum = p_heads[0]
    for ph in p_heads[1:]:
        p_sum = p_sum + ph
    blk_o = lax.broadcasted_iota(jnp.int32, (n_blk, n_cmp_pad), 0)
    cmp_o = lax.broadcasted_iota(jnp.int32, (n_blk, n_cmp_pad), 1)
    overlap_t = ((cmp_o * CMP_STRIDE <= blk_o * SEL_BLOCK + SEL_BLOCK - 1)
                 & (cmp_o * CMP_STRIDE + CMP_LEN - 1 >= blk_o * SEL_BLOCK)).astype(F32)
    imp = _dot(overlap_t, p_sum, HI)
    jj = lax.broadcasted_iota(jnp.int32, (n_blk, tq), 0)
    cur = (i * tq + lax.broadcasted_iota(jnp.int32, (n_blk, tq), 1)) >> n_blk_log2
    forced = (jj == 0) | (jj == cur) | (jj == cur - 1)
    imp = jnp.where(jj > cur, -1e6, jnp.where(forced, 1e6, imp))
    rank = jnp.zeros((n_blk, tq), jnp.int32)
    for a in range(n_blk):
        row = imp[a:a + 1, :]
        beats = (row > imp) | ((row == imp) & (a < jj))
        rank = rank + beats.astype(jnp.int32)
    not_sel = jnp.where(rank < SEL_TOP_N, 0.0, 1.0).astype(BF16)

    q_aug = jnp.concatenate([q_t, jnp.concatenate([not_sel] * hpg, axis=1),
                             jnp.zeros((kw - dh - n_blk, rws), BF16)], axis=0)
    q_pad = jnp.concatenate([q_t, jnp.zeros((kw - dh, rws), BF16)], axis=0)


    n_win = WINDOW // tq
    tiles, scores = [], []
    for delta in range(n_win + 1):
        entry = {0: BT_DIAG, 1: BT_PREV, n_win: BT_WIN_EDGE}.get(delta, BT_FAR)
        if delta > 0:
            entry = jnp.where(i - delta >= 0, entry, BT_NONE)
        tiles.append(jnp.maximum(i - delta, 0))
        off = pl.multiple_of(tiles[-1] * tq, tq)
        scores.append(_dot(kw_ref[0, pl.ds(off, tq), :], q_pad) + bt_ref[0, entry])
    m_all = scores[0]
    for sc in scores[1:]:
        m_all = jnp.maximum(m_all, sc)
    m_w = jnp.max(m_all, axis=0, keepdims=True)
    acc_w = jnp.zeros((vwt_ref.shape[2], rws), F32)
    for kt, sc in zip(tiles, scores):
        acc_w = acc_w + _dot(vwt_ref[0, kt], jnp.exp2(sc - m_w).astype(BF16))
    o_win = acc_w[0:dh] / acc_w[dh:dh + 1]

    g_t = _dot_nt(_eye(GATE_PAD, F32), gate_ref[0], HI)
    sig = jax.nn.sigmoid(g_t[0:4 * hpg])
    gate = [jnp.concatenate([sig[br * hpg + hh:br * hpg + hh + 1] for hh in range(hpg)], axis=1)
            for br in range(3)]
    m_ref[...] = jnp.full(m_ref.shape, MASKED, F32)
    acc_ref[...] = jnp.zeros(acc_ref.shape, F32)

    def sel_update(tiles):
        scores = []
        for kt in tiles:
            off = pl.multiple_of(kt * tq, tq)
            scores.append(_dot(ks_ref[0, pl.ds(off, tq), :], q_aug)
                          + bt_ref[0, jnp.minimum(i - kt, BT_FAR)])
        s_max = scores[0]
        for sc in scores[1:]:
            s_max = jnp.maximum(s_max, sc)
        m_prev = m_ref[...]
        m_new = jnp.maximum(m_prev, jnp.max(s_max, axis=0, keepdims=True))
        acc = jnp.exp2(m_prev - m_new) * acc_ref[...]
        for kt, sc in zip(tiles, scores):
            acc = acc + _dot(vst_ref[0, kt], jnp.exp2(sc - m_new).astype(BF16))
        m_ref[...] = m_new
        acc_ref[...] = acc

    def sel_pair(j, carry):
        sel_update([2 * j, 2 * j + 1])
        return carry

    lax.fori_loop(0, (i + 1) // 2, sel_pair, 0)

    @pl.when(i % 2 == 0)
    def _():
        sel_update([i])

    o_sel = acc_ref[0:dh, :] / acc_ref[dh:dh + 1, :]

    y_t = (gate[0] * o_cmp + gate[1] * o_sel + gate[2] * o_win).astype(BF16)
    eye_q = _eye(tq, BF16)
    o_ref[0] = jnp.concatenate([_dot_nt(eye_q, yh) for yh in per_head(y_t)],
                               axis=1).astype(BF16)


def _nsa_attention(proj, kv, kv_t, kvc, kvc_t, bias_c, bias_t, *, bsz, seq):
    g, tq, hpg, dh = NSA_KV_GROUPS, NSA_TQ, NSA_HPG, NSA_HEAD_DIM
    n_rows = kvc.shape[2]
    nq = seq // tq
    group_w = hpg * dh
    kv_spec = lambda pos: pl.BlockSpec((1, seq, LANES), lambda gi, b, i: (b, 0, pos * g + gi))
    kvt_spec = lambda pos: pl.BlockSpec((1, nq, LANES, tq), lambda gi, b, i: (b, 0, pos * g + gi, 0))
    kv3 = kv.reshape(bsz, seq, KV_COLS)
    proj3 = proj.reshape(bsz, seq, PROJ_COLS)
    out = pl.pallas_call(
        _nsa_body,
        out_shape=jax.ShapeDtypeStruct((bsz, seq, NSA_DIM), BF16),
        grid=(g, bsz, nq),
        in_specs=[
            pl.BlockSpec((1, tq, group_w), lambda gi, b, i: (b, i, COL_QNSA // group_w + gi)),
            pl.BlockSpec((1, 1, n_rows, dh), lambda gi, b, i: (0, b * g + gi, 0, 0)),
            pl.BlockSpec((1, 1, dh, n_rows), lambda gi, b, i: (1, b * g + gi, 0, 0)),
            kv_spec(ROW_K_SEL), kvt_spec(T_V_SEL), kv_spec(ROW_K_WIN), kvt_spec(T_V_WIN),
            pl.BlockSpec((1, 1, n_rows, hpg * tq), lambda gi, b, i: (gi, i, 0, 0)),
            pl.BlockSpec((1, BT_COUNT, tq, hpg * tq), lambda gi, b, i: (gi, 0, 0, 0)),
            pl.BlockSpec((1, tq, GATE_PAD), lambda gi, b, i: (b, i, COL_GNSA // GATE_PAD + gi)),
        ],
        out_specs=pl.BlockSpec((1, tq, group_w), lambda gi, b, i: (b, i, gi)),
        scratch_shapes=[pltpu.VMEM((1, hpg * tq), F32),
                        pltpu.VMEM((LANES, hpg * tq), F32)],
        compiler_params=_params(("parallel", "parallel", "arbitrary")),
        name="nsa_attention",
    )(proj3, kvc, kvc_t, kv3, kv_t, kv3, kv_t, bias_c, bias_t, proj3)
    return out.reshape(bsz * seq, NSA_DIM)


def _mem_body(q_ref, mem_ref, g_ref, w_ref, o_ref, kv_ref):
    @pl.when(pl.program_id(1) == 0)
    def _():
        kv_ref[...] = _dot(_rms(mem_ref[0], g_ref[...]).astype(BF16), w_ref[...]).astype(BF16)

    outs = []
    for h in range(MEM_HEADS):
        sl = slice(h * MEM_HEAD_DIM, (h + 1) * MEM_HEAD_DIM)
        qh = (q_ref[:, sl] * (MEM_HEAD_DIM ** -0.5)).astype(BF16)
        s = _dot_nt(qh, kv_ref[:, sl])
        e = jnp.exp(s - jnp.max(s, axis=-1, keepdims=True))
        p = e / jnp.sum(e, axis=-1, keepdims=True)
        outs.append(_dot(p.astype(BF16), kv_ref[:, MEM_DIM + h * MEM_HEAD_DIM:
                                                MEM_DIM + (h + 1) * MEM_HEAD_DIM]))
    o_ref[...] = jnp.concatenate(outs, axis=1)


def _mem_attention(proj, mem, gain, w_kv, *, seq, tq=1024):
    t = proj.shape[0]
    bsz, m, d = mem.shape
    per_seq = seq // tq
    return pl.pallas_call(
        _mem_body,
        out_shape=jax.ShapeDtypeStruct((t, MEM_DIM), F32),
        grid=(bsz, per_seq),
        in_specs=[
            pl.BlockSpec((tq, MEM_DIM), lambda b, j: (b * per_seq + j, COL_QMEM // MEM_DIM)),
            pl.BlockSpec((1, m, d), lambda b, j: (b, 0, 0)),
            pl.BlockSpec((1, d), lambda b, j: (0, 0)),
            pl.BlockSpec((d, 2 * MEM_DIM), lambda b, j: (0, 0), pipeline_mode=pl.Buffered(1)),
        ],
        out_specs=pl.BlockSpec((tq, MEM_DIM), lambda b, j: (b * per_seq + j, 0)),
        scratch_shapes=[pltpu.VMEM((m, 2 * MEM_DIM), BF16)],
        compiler_params=_params(("parallel", "arbitrary")),
        name="mem_attention",
    )(proj, mem, gain, w_kv)


def _merge_body(x_ref, gain_ref, yr_ref, yn_ref, ym_ref, wg_ref, wr_ref, wn_ref, wm_ref, wo_ref,
                o_ref):
    d = x_ref.shape[1]
    x = x_ref[...]
    h = _rms(x, gain_ref[...]).astype(BF16)
    merged = jnp.zeros(x.shape, F32)
    for b, (y_ref, w_ref) in enumerate(((yr_ref, wr_ref), (yn_ref, wn_ref), (ym_ref, wm_ref))):
        gate = jax.nn.sigmoid(_dot(h, wg_ref[:, b * d:(b + 1) * d]))
        merged = merged + gate * _dot(y_ref[...].astype(BF16), w_ref[...])
    o_ref[...] = x + _dot(merged.astype(BF16), wo_ref[...])


def _merge(x, gain, y_rwkv, y_nsa, y_mem, w_g, w_r, w_n, w_m, w_o, *, tm=1024):
    t, d = x.shape
    row = lambda i: (i, 0)
    const = lambda i: (0, 0)
    resident = lambda shape: pl.BlockSpec(shape, const, pipeline_mode=pl.Buffered(1))
    return pl.pallas_call(
        _merge_body,
        out_shape=jax.ShapeDtypeStruct((t, d), F32),
        grid=(t // tm,),
        in_specs=[
            pl.BlockSpec((tm, d), row),
            pl.BlockSpec((1, d), const),
            pl.BlockSpec((tm, RWKV_DIM), row),
            pl.BlockSpec((tm, NSA_DIM), row),
            pl.BlockSpec((tm, MEM_DIM), row),
            resident((d, N_BRANCH * d)),
            resident((RWKV_DIM, d)),
            resident((NSA_DIM, d)),
            resident((MEM_DIM, d)),
            resident((d, d)),
        ],
        out_specs=pl.BlockSpec((tm, d), row),
        compiler_params=_params(("parallel",)),
        name="merge",
    )(x, gain, y_rwkv, y_nsa, y_mem, w_g, w_r, w_n, w_m, w_o)


def _row(a):
    return a.reshape(1, -1)


def _in_proj(x, p, l, seq):
    d = x.shape[1]
    g, hpg, dh = NSA_KV_GROUPS, NSA_HPG, NSA_HEAD_DIM
    w_in = p['w_in'][l]
    o = 0
    parts = {}
    for name, size in (('rwkv', RWKV_PROJ), ('q', NSA_DIM), ('kv', KV_KINDS * NSA_KV_DIM),
                       ('g_nsa', 3 * NSA_HEADS), ('q_mem', MEM_DIM)):
        parts[name] = w_in[:, o:o + size]
        o += size
    gates = parts['g_nsa'].reshape(d, 3, g, hpg).transpose(0, 2, 1, 3).reshape(d, g, 3 * hpg)
    gates = jnp.pad(gates, ((0, 0), (0, 0), (0, GATE_PAD - 3 * hpg))).reshape(d, g * GATE_PAD)
    w_other = jnp.concatenate([parts['q'], parts['q_mem'], gates], axis=1).astype(BF16)
    w_kv = jnp.pad(parts['kv'].reshape(d, KV_KINDS, g, dh), ((0, 0), (0, 0), (0, 0), (0, LANES - dh)))
    w_row = w_kv[:, KV_ROW_KINDS, :, :].reshape(d, KV_COLS).astype(BF16)
    w_t = w_kv[:, KV_T_KINDS, :, :].reshape(d, KV_T_ROWS).T.astype(BF16)
    w_lora = jnp.zeros((LORA_DIM, 3 * RWKV_DIM), F32)
    w_lora = w_lora.at[0:DECAY_LORA, 0:RWKV_DIM].set(p['rwkv_w2'][l])
    w_lora = w_lora.at[DECAY_LORA:DECAY_LORA + AAA_LORA, RWKV_DIM:2 * RWKV_DIM].set(p['rwkv_a2'][l])
    w_lora = w_lora.at[DECAY_LORA + AAA_LORA:, 2 * RWKV_DIM:].set(p['rwkv_g2'][l])
    proj, kv, kv_t, *rwkv_in = _in_proj_call(
        x, seq, _row(p['mix_norm'][l]), parts['rwkv'].astype(BF16), w_other, w_row, w_t,
        _row(p['rwkv_mu'][l]), w_lora.astype(BF16), _row(p['rwkv_w0'][l]), _row(p['rwkv_a0'][l]),
        _row(p['rwkv_k_k'][l]), _row(p['rwkv_k_a'][l]))
    return proj, kv, kv_t, rwkv_in


def _rwkv_branch(rwkv_in, p, l, bsz, seq):
    return _rwkv_scan(*rwkv_in, _row(p['rwkv_r_k'][l]), _row(p['rwkv_gn_gain'][l]),
                      _row(p['rwkv_gn_bias'][l]), batch=bsz, seq=seq)


def _nsa_branch(proj, kv, kv_t, bias_c, bias_t, p, l, bsz, seq):
    w1 = jnp.stack([p['cmp_k_w1'][l], p['cmp_v_w1'][l]]).astype(BF16)
    pe = jnp.stack([p['cmp_pe_k'][l].reshape(1, -1), p['cmp_pe_v'][l].reshape(1, -1)])
    w2 = jnp.stack([p['cmp_k_w2'][l], p['cmp_v_w2'][l]]).astype(BF16)
    kvc, kvc_t = _compress(kv, w1, pe, w2, bsz=bsz, seq=seq)
    return _nsa_attention(proj, kv, kv_t, kvc, kvc_t, bias_c, bias_t, bsz=bsz, seq=seq)


def _mem_branch(proj, mem, p, l):
    w_kv = jnp.concatenate([p['mem_w_k'][l], p['mem_w_v'][l]], axis=1).astype(BF16)
    return _mem_attention(proj, mem, _row(p['mem_norm'][l]), w_kv, seq=proj.shape[0] // mem.shape[0])


def _layer(x, mem, l, bias_c, bias_t, p):
    bsz, seq, d = x.shape
    t = bsz * seq
    row = _row
    x = x.reshape(t, d)

    x = _ffn(x, row(p['ffn1_norm'][l]), p['ffn1_w_gate'][l].astype(BF16),
             p['ffn1_w_up'][l].astype(BF16), p['ffn1_w_down'][l].astype(BF16),
             row(p['final_norm']), final=False)

    proj, kv, kv_t, rwkv_in = _in_proj(x, p, l, seq)
    y_rwkv = _rwkv_branch(rwkv_in, p, l, bsz, seq)
    y_nsa = _nsa_branch(proj, kv, kv_t, bias_c, bias_t, p, l, bsz, seq)
    y_mem = _mem_branch(proj, mem, p, l)

    w_gate = p['w_in'][l][:, -N_BRANCH * d:].astype(BF16)
    x = _merge(x, row(p['mix_norm'][l]), y_rwkv, y_nsa, y_mem, w_gate,
               p['w_br_rwkv'][l].astype(BF16), p['w_br_nsa'][l].astype(BF16),
               p['w_br_mem'][l].astype(BF16), p['w_out'][l].astype(BF16))

    last = l == p['ffn1_norm'].shape[0] - 1
    x = _ffn(x, row(p['ffn2_norm'][l]), p['ffn2_w_gate'][l].astype(BF16),
             p['ffn2_w_up'][l].astype(BF16), p['ffn2_w_down'][l].astype(BF16),
             row(p['final_norm']), final=last)
    return x.reshape(bsz, seq, d)


def kernel(x, mem, ffn1_norm, ffn1_w_gate, ffn1_w_up, ffn1_w_down, mix_norm, w_in, rwkv_mu, rwkv_w0, rwkv_w2, rwkv_a0, rwkv_a2, rwkv_g2, rwkv_k_k, rwkv_k_a, rwkv_r_k, rwkv_gn_gain, rwkv_gn_bias, cmp_pe_k, cmp_k_w1, cmp_k_w2, cmp_pe_v, cmp_v_w1, cmp_v_w2, rel_bias, mem_norm, mem_w_k, mem_w_v, w_br_rwkv, w_br_nsa, w_br_mem, w_out, ffn2_norm, ffn2_w_gate, ffn2_w_up, ffn2_w_down, final_norm):
    p = dict(ffn1_norm=ffn1_norm, ffn1_w_gate=ffn1_w_gate, ffn1_w_up=ffn1_w_up,
             ffn1_w_down=ffn1_w_down, mix_norm=mix_norm, w_in=w_in, rwkv_mu=rwkv_mu,
             rwkv_w0=rwkv_w0, rwkv_w2=rwkv_w2, rwkv_a0=rwkv_a0, rwkv_a2=rwkv_a2, rwkv_g2=rwkv_g2,
             rwkv_k_k=rwkv_k_k, rwkv_k_a=rwkv_k_a, rwkv_r_k=rwkv_r_k, rwkv_gn_gain=rwkv_gn_gain,
             rwkv_gn_bias=rwkv_gn_bias, cmp_pe_k=cmp_pe_k, cmp_k_w1=cmp_k_w1, cmp_k_w2=cmp_k_w2,
             cmp_pe_v=cmp_pe_v, cmp_v_w1=cmp_v_w1, cmp_v_w2=cmp_v_w2, mem_norm=mem_norm,
             mem_w_k=mem_w_k, mem_w_v=mem_w_v, w_br_rwkv=w_br_rwkv, w_br_nsa=w_br_nsa,
             w_br_mem=w_br_mem, w_out=w_out, ffn2_norm=ffn2_norm, ffn2_w_gate=ffn2_w_gate,
             ffn2_w_up=ffn2_w_up, ffn2_w_down=ffn2_w_down, final_norm=final_norm)
    bias_c, bias_t = _bias_tables(rel_bias, x.shape[1])
    for l in range(ffn1_norm.shape[0]):
        x = _layer(x, mem, l, bias_c, bias_t, p)
    return x
```

```python
import functools
import math

import jax
import jax.numpy as jnp
from jax import lax
from jax.experimental import pallas as pl
from jax.experimental.pallas import tpu as pltpu

F32 = jnp.float32
BF16 = jnp.bfloat16
HI = lax.Precision.HIGHEST

NORM_EPS = 1e-6
RWKV_HEADS = 8
RWKV_HEAD_DIM = 64
RWKV_DIM = RWKV_HEADS * RWKV_HEAD_DIM
DECAY_LORA = 64
AAA_LORA = 64
GATE_LORA = 128
LORA_DIM = DECAY_LORA + AAA_LORA + GATE_LORA
RWKV_GN_EPS = 64e-5
RWKV_PROJ = 3 * RWKV_DIM + LORA_DIM
NSA_HEADS = 8
NSA_KV_GROUPS = 2
NSA_HPG = NSA_HEADS // NSA_KV_GROUPS
NSA_HEAD_DIM = 64
NSA_DIM = NSA_HEADS * NSA_HEAD_DIM
NSA_KV_DIM = NSA_KV_GROUPS * NSA_HEAD_DIM
CMP_LEN = 32
CMP_STRIDE = 16
CMP_HIDDEN = 256
SEL_BLOCK = 64
SEL_TOP_N = 16
WINDOW = 512
REL_BUCKETS = 32
REL_MAX_DIST = 128
MEM_HEADS = 4
MEM_HEAD_DIM = 128
MEM_DIM = MEM_HEADS * MEM_HEAD_DIM
N_BRANCH = 3

LANES = 128
GATE_PAD = LANES
COL_QNSA = 0
COL_QMEM = COL_QNSA + NSA_DIM
COL_GNSA = COL_QMEM + MEM_DIM
PROJ_COLS = COL_GNSA + NSA_KV_GROUPS * GATE_PAD
assert COL_QMEM % MEM_DIM == 0 and COL_GNSA % GATE_PAD == 0
KV_KINDS = 6
KV_ROW_KINDS = (0, 1, 2, 4)
KV_T_KINDS = (3, 5)
KV_COLS = len(KV_ROW_KINDS) * NSA_KV_GROUPS * LANES
KV_T_ROWS = len(KV_T_KINDS) * NSA_KV_GROUPS * LANES
ROW_K_SEL, ROW_K_WIN = KV_ROW_KINDS.index(2), KV_ROW_KINDS.index(4)
T_V_SEL, T_V_WIN = KV_T_KINDS.index(3), KV_T_KINDS.index(5)
LOG2E = 1.4426950408889634

RWKV_CHUNK = 64
RWKV_INV_BLOCK = 16
NSA_TQ = 256
MASKED = -1e30
BT_DIAG, BT_PREV, BT_FAR, BT_WIN_EDGE, BT_NONE, BT_COUNT = 0, 1, 2, 3, 4, 5
V7X_VMEM_BYTES = 64 * 1024 * 1024
VMEM_LIMIT = V7X_VMEM_BYTES - 8 * 1024 * 1024


def _dot(a, b, precision=None):
    return jnp.dot(a, b, preferred_element_type=F32, precision=precision)


def _dot_nt(a, b, precision=None):
    return lax.dot_general(a, b, (((1,), (1,)), ((), ())), preferred_element_type=F32,
                           precision=precision)


def _params(semantics):
    return pltpu.CompilerParams(dimension_semantics=semantics, vmem_limit_bytes=VMEM_LIMIT)


def _rms(x, g):
    return x * lax.rsqrt(jnp.mean(x * x, axis=-1, keepdims=True) + NORM_EPS) * g


def _ffn_body(x_ref, g_ref, wg_ref, wu_ref, wd_ref, fg_ref, o_ref, *, tf, final):
    x = x_ref[...]
    h = _rms(x, g_ref[...]).astype(BF16)
    acc = jnp.zeros(x.shape, F32)
    for j in range(wg_ref.shape[1] // tf):
        cols = slice(j * tf, (j + 1) * tf)
        act = (jax.nn.silu(_dot(h, wg_ref[:, cols])) * _dot(h, wu_ref[:, cols])).astype(BF16)
        acc = acc + _dot(act, wd_ref[cols, :])
    y = x + 0.5 * acc
    if final:
        y = _rms(y, fg_ref[...])
    o_ref[...] = y


def _ffn(x, gain, wg, wu, wd, final_gain, *, final, tm=1024, tf=256):
    t, d = x.shape
    f = wg.shape[1]
    resident = lambda shape: pl.BlockSpec(shape, lambda i: (0, 0), pipeline_mode=pl.Buffered(1))
    return pl.pallas_call(
        functools.partial(_ffn_body, tf=tf, final=final),
        out_shape=jax.ShapeDtypeStruct((t, d), F32),
        grid=(t // tm,),
        in_specs=[
            pl.BlockSpec((tm, d), lambda i: (i, 0)),
            pl.BlockSpec((1, d), lambda i: (0, 0)),
            resident((d, f)),
            resident((d, f)),
            resident((f, d)),
            pl.BlockSpec((1, d), lambda i: (0, 0)),
        ],
        out_specs=pl.BlockSpec((tm, d), lambda i: (i, 0)),
        compiler_params=_params(("parallel",)),
        name="ffn_final" if final else "ffn",
    )(x, gain, wg, wu, wd, final_gain)


def _write_kv(h, w_ref, wt_ref, o_ref, ot_ref, *, seq):
    dh, tk, tm = NSA_HEAD_DIM, NSA_TQ, h.shape[0]
    y = _dot(h, w_ref[...])
    row = lax.broadcasted_iota(jnp.int32, (tm, LANES), 0)
    lane = lax.broadcasted_iota(jnp.int32, (tm, LANES), 1)
    pos = (pl.program_id(0) * tm) % seq + row
    block_mark = jnp.where(lane - dh == (pos >> (SEL_BLOCK.bit_length() - 1)), MASKED, 0.0)
    for tile in range(KV_COLS // LANES):
        part = y[:, tile * LANES:(tile + 1) * LANES]
        if tile // NSA_KV_GROUPS == ROW_K_SEL:
            part = part + block_mark
        o_ref[:, tile * LANES:(tile + 1) * LANES] = part.astype(BF16)

    y_t = _dot_nt(wt_ref[...], h)
    row_t = lax.broadcasted_iota(jnp.int32, y_t.shape, 0)
    y_t = (y_t + jnp.where((row_t & (LANES - 1)) == dh, 1.0, 0.0)).astype(BF16)
    for c in range(tm // tk):
        ot_ref[0, c] = y_t[:, c * tk:(c + 1) * tk]


def _in_proj_body(x_ref, xp_ref, g_ref, wr_ref, wo_ref, wkv_ref, wkvt_ref, mu_ref, wl_ref, w0_ref,
                  a0_ref, kk_ref, ka_ref, proj_o, kv_o, kvt_o, r_o, k_o, v_o, kk_o, b_o, lw_o, g_o,
                  *, tiles_per_seq, seq):
    i = pl.program_id(0)
    gain = g_ref[...]
    h = _rms(x_ref[...], gain).astype(BF16)
    proj_o[...] = _dot(h, wo_ref[...])
    _write_kv(h, wkv_ref, wkvt_ref, kv_o, kvt_o, seq=seq)
    n_up = xp_ref.shape[0]
    h_up = _rms(xp_ref[...], gain).astype(BF16)
    p_all = _dot(jnp.concatenate([h_up, h], axis=0), wr_ref[...])
    p = p_all[n_up:]
    keep = jnp.where(i % tiles_per_seq == 0, 0.0, 1.0)
    prev_last = p_all[n_up - 1:n_up, :] * keep
    rows = lax.broadcasted_iota(jnp.int32, p.shape, 0)
    shifted = jnp.where(rows == 0, prev_last, pltpu.roll(p, 1, 0))
    x = p + (shifted - p) * mu_ref[...]

    r = x[:, 0:RWKV_DIM]
    k = x[:, RWKV_DIM:2 * RWKV_DIM]
    v = x[:, 2 * RWKV_DIM:3 * RWKV_DIM]
    s = x[:, 3 * RWKV_DIM:RWKV_PROJ]
    lane = lax.broadcasted_iota(jnp.int32, s.shape, 1)
    z = jnp.where(lane < DECAY_LORA, jnp.tanh(s),
                  jnp.where(lane < DECAY_LORA + AAA_LORA, s, jax.nn.sigmoid(s)))
    lo = _dot(z.astype(BF16), wl_ref[...])
    a = jax.nn.sigmoid(a0_ref[...] + lo[:, RWKV_DIM:2 * RWKV_DIM])

    kkr = k * kk_ref[...]
    sq = kkr * kkr
    sq_hi = sq.astype(BF16)
    sq_lo = (sq - sq_hi.astype(F32)).astype(BF16)
    shift = RWKV_HEAD_DIM.bit_length() - 1
    same_head = ((lax.broadcasted_iota(jnp.int32, (RWKV_DIM, RWKV_DIM), 0) >> shift)
                 == (lax.broadcasted_iota(jnp.int32, (RWKV_DIM, RWKV_DIM), 1) >> shift)).astype(BF16)
    ssq = _dot(sq_hi, same_head) + _dot(sq_lo, same_head)
    kk = kkr / jnp.maximum(jnp.sqrt(ssq), 1e-12)

    r_o[...] = r
    k_o[...] = k * (1.0 + (a - 1.0) * ka_ref[...])
    v_o[...] = v
    kk_o[...] = kk
    b_o[...] = kk * a
    lw_o[...] = -math.exp(-0.5) * jax.nn.sigmoid(w0_ref[...] + lo[:, 0:RWKV_DIM])
    g_o[...] = lo[:, 2 * RWKV_DIM:3 * RWKV_DIM]


def _in_proj_call(x, seq, gain, w_rwkv, w_other, w_kv, w_kv_t, mu, w_lora, w0, a0, k_k, k_a, *,
                  tm=512):
    t, d = x.shape
    tk = NSA_TQ
    per_seq = seq // tm
    row = lambda i: (i, 0)
    const = lambda i: (0, 0)
    resident = lambda shape: pl.BlockSpec(shape, const, pipeline_mode=pl.Buffered(1))
    vec = pl.BlockSpec((1, RWKV_DIM), const)
    tok = jax.ShapeDtypeStruct((t, RWKV_DIM), F32)
    return pl.pallas_call(
        functools.partial(_in_proj_body, tiles_per_seq=per_seq, seq=seq),
        out_shape=[jax.ShapeDtypeStruct((t, PROJ_COLS), F32),
                   jax.ShapeDtypeStruct((t, KV_COLS), BF16),
                   jax.ShapeDtypeStruct((t // seq, seq // tk, KV_T_ROWS, tk), BF16)] + [tok] * 7,
        grid=(t // tm,),
        in_specs=[
            pl.BlockSpec((tm, d), row),
            pl.BlockSpec((16, d), lambda i: (jnp.maximum(i * (tm // 16) - 1, 0), 0)),
            pl.BlockSpec((1, d), const),
            resident((d, RWKV_PROJ)),
            resident((d, PROJ_COLS)),
            resident((d, KV_COLS)),
            resident((KV_T_ROWS, d)),
            pl.BlockSpec((1, RWKV_PROJ), const),
            resident((LORA_DIM, 3 * RWKV_DIM)),
            vec, vec, vec, vec,
        ],
        out_specs=[pl.BlockSpec((tm, PROJ_COLS), row),
                   pl.BlockSpec((tm, KV_COLS), row),
                   pl.BlockSpec((1, tm // tk, KV_T_ROWS, tk),
                                lambda i: (i // per_seq, i % per_seq, 0, 0))]
        + [pl.BlockSpec((tm, RWKV_DIM), row)] * 7,
        compiler_params=_params(("parallel",)),
        name="in_proj",
    )(x, x, gain, w_rwkv, w_other, w_kv, w_kv_t, mu, w_lora, w0, a0, k_k, k_a)


def _rwkv_scan_body(r_ref, k_ref, v_ref, kk_ref, b_ref, lw_ref, g_ref, rk_ref, gg_ref, gb_ref,
                    o_ref, st_ref):
    c_sz, n, nh = RWKV_CHUNK, RWKV_HEAD_DIM, RWKV_HEADS

    @pl.when(pl.program_id(1) == 0)
    def _():
        st_ref[...] = jnp.zeros_like(st_ref)

    ri = lax.broadcasted_iota(jnp.int32, (c_sz, c_sz), 0)
    ci = lax.broadcasted_iota(jnp.int32, (c_sz, c_sz), 1)
    incl = ci <= ri
    eye_b = (ci == ri).astype(BF16)
    row2 = lax.broadcasted_iota(jnp.int32, (c_sz, 2 * c_sz), 0)
    lane2 = lax.broadcasted_iota(jnp.int32, (c_sz, 2 * c_sz), 1)
    right_half = lane2 >= c_sz
    zeros_b = jnp.zeros((c_sz, n), BF16)
    inv_shift = RWKV_INV_BLOCK.bit_length() - 1

    rows = []
    for bb in range(st_ref.shape[0]):
        lw = lw_ref[bb]
        cum = _dot(incl.astype(F32), lw, HI)
        cum_last = cum[c_sz - 1:c_sz, :]
        r, k, v, b = r_ref[bb], k_ref[bb], v_ref[bb], b_ref[bb]
        p_inv = jnp.exp(-cum)
        p_end = jnp.exp(cum_last - cum)
        rows.append(dict(
            left=jnp.concatenate([(-(kk_ref[bb] * jnp.exp(cum - lw))).astype(BF16),
                                  (r * jnp.exp(cum)).astype(BF16)], axis=0),
            bt=(b * p_inv).astype(BF16), kt=(k * p_inv).astype(BF16),
            bh=(b * p_end).astype(BF16), kh=(k * p_end).astype(BF16),
            v=v, v_b=v.astype(BF16), d_p=jnp.exp(cum_last), rk=r * k * rk_ref[...]))

    units = [(bb, h) for bb in range(len(rows)) for h in range(nh)]
    col = lambda name, u: rows[u[0]][name][:, u[1] * n:(u[1] + 1) * n]
    a_all = [_dot_nt(col('left', u), jnp.concatenate([col('bt', u), col('kt', u)], axis=0))
             for u in units]
    key2 = jnp.where(right_half, lane2 - c_sz, lane2)
    w_u = [jnp.where(right_half & (key2 < row2), a[:c_sz], 0.0).astype(BF16) for a in a_all]
    w_y = [jnp.where(key2 <= row2, a[c_sz:], 0.0).astype(BF16) for a in a_all]

    same_block = (row2 >> inv_shift) == (key2 >> inv_shift)
    x = [jnp.where((lane2 < row2) & same_block, a[:c_sz], jnp.where(lane2 == row2 + c_sz, 1.0, 0.0))
         for a in a_all]
    for _ in range(inv_shift):
        hi = [xu.astype(BF16) for xu in x]
        lo = [(xu - h_.astype(F32)).astype(BF16) for xu, h_ in zip(x, hi)]
        x = [_dot(h_[:, :c_sz], h_) + _dot(h_[:, :c_sz], l_) + _dot(l_[:, :c_sz], h_)
             + jnp.where(right_half, xu, 0.0) for xu, h_, l_ in zip(x, hi, lo)]
    x_b = [xu.astype(BF16) for xu in x]
    solve_diag = lambda j, z: _dot(x_b[j], jnp.concatenate([zeros_b, z.astype(BF16)], axis=0))
    q_b = [solve_diag(j, jnp.where((ci < ri) & ((ri >> inv_shift) != (ci >> inv_shift)),
                                   a[:c_sz, :c_sz], 0.0)).astype(BF16)
           for j, a in enumerate(a_all)]

    s0 = [st_ref[bb, h] for bb, h in units]
    ls0 = [_dot_nt(col('left', u), s0[j].astype(BF16)) for j, u in enumerate(units)]
    rhs = [ls0[j][:c_sz] + _dot(w_u[j], jnp.concatenate([zeros_b, col('v_b', u)], axis=0))
           for j, u in enumerate(units)]
    g0 = [solve_diag(j, rhs[j]) for j in range(len(units))]
    u_f = g0
    for _ in range(c_sz // RWKV_INV_BLOCK - 1):
        u_f = [g0[j] + _dot(q_b[j], u_f[j].astype(BF16)) for j in range(len(units))]
    u_b = [uj.astype(BF16) for uj in u_f]
    uv = [jnp.concatenate([u_b[j], col('v_b', u)], axis=0) for j, u in enumerate(units)]
    y = [ls0[j][c_sz:] + _dot(w_y[j], uv[j]) for j in range(len(units))]
    uv_t = [_dot_nt(eye_b, uv_j).astype(BF16) for uv_j in uv]
    for j, u in enumerate(units):
        st_ref[u[0], u[1]] = (s0[j] * col('d_p', u)
                              + _dot(uv_t[j], jnp.concatenate([col('bh', u), col('kh', u)], axis=0)))

    for bb in range(len(rows)):
        outs = []
        for h in range(nh):
            sl = slice(h * n, (h + 1) * n)
            yh = y[bb * nh + h]
            mean = jnp.mean(yh, axis=-1, keepdims=True)
            var = jnp.mean(jnp.square(yh - mean), axis=-1, keepdims=True)
            yn = (yh - mean) * lax.rsqrt(var + RWKV_GN_EPS)
            yn = yn * gg_ref[:, sl] + gb_ref[:, sl]
            bonus = jnp.sum(rows[bb]['rk'][:, sl], axis=-1, keepdims=True) * rows[bb]['v'][:, sl]
            outs.append((yn + bonus) * g_ref[bb, :, sl])
        o_ref[bb] = jnp.concatenate(outs, axis=1)


def _rwkv_scan(r, k, v, kk, b, lw, g, r_k, gn_gain, gn_bias, *, batch, seq, nb=4):
    t = r.shape[0]
    nc = seq // RWKV_CHUNK
    tok = pl.BlockSpec((nb, RWKV_CHUNK, RWKV_DIM), lambda bi, c: (bi, c, 0))
    par = pl.BlockSpec((1, RWKV_DIM), lambda bi, c: (0, 0))
    per_batch = lambda a: a.reshape(batch, seq, RWKV_DIM)
    out = pl.pallas_call(
        _rwkv_scan_body,
        out_shape=jax.ShapeDtypeStruct((batch, seq, RWKV_DIM), F32),
        grid=(batch // nb, nc),
        in_specs=[tok] * 7 + [par] * 3,
        out_specs=tok,
        scratch_shapes=[pltpu.VMEM((nb, RWKV_HEADS, RWKV_HEAD_DIM, RWKV_HEAD_DIM), F32)],
        compiler_params=_params(("parallel", "arbitrary")),
        name="rwkv_scan",
    )(*(per_batch(a) for a in (r, k, v, kk, b, lw, g)), r_k, gn_gain, gn_bias)
    return out.reshape(t, RWKV_DIM)


def _compress_body(x_ref, w1s_ref, w1_ref, pe_ref, w2_ref, w2t_ref, o_ref, ot_ref):
    n_rows = x_ref.shape[1]
    for kind in range(2):
        pe = jnp.broadcast_to(pe_ref[kind], (8, pe_ref.shape[2])).astype(BF16)
        pe_term = _dot(pe, w1_ref[kind])[0:1, :]
        for gi in range(NSA_KV_GROUPS):
            tile = kind * NSA_KV_GROUPS + gi
            both = jnp.zeros((n_rows, 2 * CMP_HIDDEN), F32)
            for l in range(CMP_STRIDE):
                both = both + _dot(x_ref[0, :, l, tile * LANES:(tile + 1) * LANES], w1s_ref[kind, l])
            second_next = pltpu.roll(both[:, CMP_HIDDEN:], n_rows - 1, 0)
            hid = both[:, :CMP_HIDDEN] + second_next + pe_term
            act = jax.nn.gelu(hid).astype(BF16)
            o_ref[kind, gi] = _dot(act, w2_ref[kind])
            ot_ref[kind, gi] = _dot_nt(w2t_ref[kind], act)


def _compress(kv, w1, pe, w2, *, bsz, seq):
    g, dh = NSA_KV_GROUPS, NSA_HEAD_DIM
    rows = seq // CMP_STRIDE
    w1r = jnp.pad(w1.reshape(2, CMP_LEN, dh, CMP_HIDDEN), ((0, 0), (0, 0), (0, LANES - dh), (0, 0)))
    w1s = jnp.concatenate([w1r[:, :CMP_STRIDE], w1r[:, CMP_STRIDE:]], axis=3)
    whole = lambda a: pl.BlockSpec(a.shape, lambda b: (0,) * a.ndim)
    w2t = w2.transpose(0, 2, 1)
    return pl.pallas_call(
        _compress_body,
        out_shape=[jax.ShapeDtypeStruct((2, bsz * g, rows, dh), F32),
                   jax.ShapeDtypeStruct((2, bsz * g, dh, rows), F32)],
        grid=(bsz,),
        in_specs=[pl.BlockSpec((1, rows, CMP_STRIDE, 2 * g * LANES), lambda b: (b, 0, 0, 0)),
                  whole(w1s), whole(w1), whole(pe), whole(w2), whole(w2t)],
        out_specs=[pl.BlockSpec((2, g, rows, dh), lambda b: (0, b, 0, 0)),
                   pl.BlockSpec((2, g, dh, rows), lambda b: (0, b, 0, 0))],
        compiler_params=_params(("parallel",)),
        name="nsa_compress",
    )(kv.reshape(bsz, rows, CMP_STRIDE, KV_COLS), w1s, w1, pe, w2, w2t)


def _t5_bucket(dist):
    n = jnp.maximum(dist, 0)
    exact = REL_BUCKETS // 2
    nf = jnp.maximum(n, 1).astype(F32)
    scaled = jnp.log(nf / exact) / math.log(REL_MAX_DIST / exact) * (REL_BUCKETS - exact)
    large = exact + jnp.floor(scaled).astype(jnp.int32)
    large = jnp.minimum(large, REL_BUCKETS - 1)
    return jnp.where(n < exact, n, large)


def _bias_body(tab_ref, bc_ref, bt_ref, *, seq, n_cmp_pad):
    h = pl.program_id(0)
    tq = NSA_TQ

    def lookup(dist):
        bucket = _t5_bucket(dist)
        out = jnp.zeros(dist.shape, F32)
        for bkt in range(REL_BUCKETS):
            out = jnp.where(bucket == bkt, tab_ref[bkt, h] * LOG2E, out)
        return out

    far = tab_ref[REL_BUCKETS - 1, h] * LOG2E
    assert tq + 1 >= REL_MAX_DIST and WINDOW - tq + 1 >= REL_MAX_DIST

    key = lax.broadcasted_iota(jnp.int32, (tq, tq), 0)
    qry = lax.broadcasted_iota(jnp.int32, (tq, tq), 1)
    bt_ref[0, BT_DIAG] = jnp.where(qry >= key, lookup(qry - key), MASKED)
    bt_ref[0, BT_PREV] = lookup(tq + qry - key)
    bt_ref[0, BT_FAR] = jnp.full((tq, tq), far, F32)
    bt_ref[0, BT_WIN_EDGE] = jnp.where(qry < key, far, MASKED)
    bt_ref[0, BT_NONE] = jnp.full((tq, tq), MASKED, F32)

    per_tile = tq // CMP_STRIDE
    pad = 16
    assert pad * CMP_STRIDE >= REL_MAX_DIST + CMP_LEN - 1 and pad % 8 == 0
    band = per_tile + pad
    cmp_end = lax.broadcasted_iota(jnp.int32, (band, tq), 0) * CMP_STRIDE + CMP_LEN - 1
    qry_c = lax.broadcasted_iota(jnp.int32, (band, tq), 1)

    def cmp_tile(i, carry):
        start = pl.multiple_of(jnp.maximum(i * per_tile - pad, 0), 8)
        bc_ref[0, i] = jnp.full((n_cmp_pad, tq), far, F32)
        bc_ref[0, i, pl.ds(start, band), :] = lookup(i * tq + qry_c - (start * CMP_STRIDE + cmp_end))
        return carry

    lax.fori_loop(0, seq // tq, cmp_tile, 0)


def _bias_tables(rel_bias, seq):
    g, hpg, tq = NSA_KV_GROUPS, NSA_HPG, NSA_TQ
    n_cmp_pad = seq // CMP_STRIDE
    nq = seq // tq
    return pl.pallas_call(
        functools.partial(_bias_body, seq=seq, n_cmp_pad=n_cmp_pad),
        out_shape=[jax.ShapeDtypeStruct((g, nq, n_cmp_pad, hpg * tq), F32),
                   jax.ShapeDtypeStruct((g, BT_COUNT, tq, hpg * tq), F32)],
        grid=(NSA_HEADS,),
        in_specs=[pl.BlockSpec(memory_space=pltpu.SMEM)],
        out_specs=[pl.BlockSpec((1, nq, n_cmp_pad, tq), lambda h: (h // hpg, 0, 0, h % hpg)),
                   pl.BlockSpec((1, BT_COUNT, tq, tq), lambda h: (h // hpg, 0, 0, h % hpg))],
        compiler_params=_params(("parallel",)),
        name="nsa_bias",
    )(rel_bias)


def _eye(n, dtype):
    return (lax.broadcasted_iota(jnp.int32, (n, n), 0)
            == lax.broadcasted_iota(jnp.int32, (n, n), 1)).astype(dtype)


def _nsa_body(q_ref, kc_ref, vct_ref, ks_ref, vst_ref, kw_ref, vwt_ref, bc_ref, bt_ref, gate_ref,
              o_ref, m_ref, acc_ref):
    tq, hpg, dh = NSA_TQ, NSA_HPG, NSA_HEAD_DIM
    rws = hpg * tq
    n_blk_log2 = SEL_BLOCK.bit_length() - 1
    n_blk = ks_ref.shape[1] // SEL_BLOCK
    n_cmp_pad = kc_ref.shape[2]
    kw = ks_ref.shape[2]
    i = pl.program_id(2)

    def per_head(x):
        return [x[:, hh * tq:(hh + 1) * tq] for hh in range(hpg)]

    xq = (q_ref[0] * (dh ** -0.5 * LOG2E)).astype(BF16)
    eye_d = _eye(dh, BF16)
    q_t = jnp.concatenate([_dot_nt(eye_d, xq[:, hh * dh:(hh + 1) * dh]) for hh in range(hpg)],
                          axis=1).astype(BF16)

    cmp_id = lax.broadcasted_iota(jnp.int32, (n_cmp_pad, rws), 0)
    t_pos = i * tq + (lax.broadcasted_iota(jnp.int32, (n_cmp_pad, rws), 1) & (tq - 1))
    valid = (t_pos - (cmp_id * CMP_STRIDE + CMP_LEN - 1) >= 0) & (cmp_id < n_cmp_pad - 1)
    s = jnp.where(valid, _dot(kc_ref[0, 0].astype(BF16), q_t) + bc_ref[0, 0], MASKED)
    e = jnp.where(valid, jnp.exp2(s - jnp.max(s, axis=0, keepdims=True)), 0.0)
    den = jnp.sum(e, axis=0, keepdims=True)
    p_c = e / jnp.where(den > 0.0, den, 1.0)
    o_cmp = _dot(vct_ref[0, 0].astype(BF16), p_c.astype(BF16))

    p_heads = per_head(p_c)
    p_sum = p_heads[0]
    for ph in p_heads[1:]:
        p_sum = p_sum + ph
    blk_o = lax.broadcasted_iota(jnp.int32, (n_blk, n_cmp_pad), 0)
    cmp_o = lax.broadcasted_iota(jnp.int32, (n_blk, n_cmp_pad), 1)
    overlap_t = ((cmp_o * CMP_STRIDE <= blk_o * SEL_BLOCK + SEL_BLOCK - 1)
                 & (cmp_o * CMP_STRIDE + CMP_LEN - 1 >= blk_o * SEL_BLOCK)).astype(F32)
    imp = _dot(overlap_t, p_sum, HI)
    jj = lax.broadcasted_iota(jnp.int32, (n_blk, tq), 0)
    cur = (i * tq + lax.broadcasted_iota(jnp.int32, (n_blk, tq), 1)) >> n_blk_log2
    forced = (jj == 0) | (jj == cur) | (jj == cur - 1)
    imp = jnp.where(jj > cur, -1e6, jnp.where(forced, 1e6, imp))
    rank = jnp.zeros((n_blk, tq), jnp.int32)
    for a in range(n_blk):
        row = imp[a:a + 1, :]
        beats = (row > imp) | ((row == imp) & (a < jj))
        rank = rank + beats.astype(jnp.int32)
    not_sel = jnp.where(rank < SEL_TOP_N, 0.0, 1.0).astype(BF16)

    q_aug = jnp.concatenate([q_t, jnp.concatenate([not_sel] * hpg, axis=1),
                             jnp.zeros((kw - dh - n_blk, rws), BF16)], axis=0)
    q_pad = jnp.concatenate([q_t, jnp.zeros((kw - dh, rws), BF16)], axis=0)


    n_win = WINDOW // tq
    tiles, scores = [], []
    for delta in range(n_win + 1):
        entry = {0: BT_DIAG, 1: BT_PREV, n_win: BT_WIN_EDGE}.get(delta, BT_FAR)
        if delta > 0:
            entry = jnp.where(i - delta >= 0, entry, BT_NONE)
        tiles.append(jnp.maximum(i - delta, 0))
        off = pl.multiple_of(tiles[-1] * tq, tq)
        scores.append(_dot(kw_ref[0, pl.ds(off, tq), :], q_pad) + bt_ref[0, entry])
    m_all = scores[0]
    for sc in scores[1:]:
        m_all = jnp.maximum(m_all, sc)
    m_w = jnp.max(m_all, axis=0, keepdims=True)
    acc_w = _dot(jnp.concatenate([vwt_ref[0, kt] for kt in tiles], axis=1),
                 jnp.concatenate([jnp.exp2(sc - m_w).astype(BF16) for sc in scores], axis=0))
    o_win = acc_w[0:dh] / acc_w[dh:dh + 1]

    g_t = _dot_nt(_eye(GATE_PAD, F32), gate_ref[0], HI)
    sig = jax.nn.sigmoid(g_t[0:4 * hpg])
    gate = [jnp.concatenate([sig[br * hpg + hh:br * hpg + hh + 1] for hh in range(hpg)], axis=1)
            for br in range(3)]
    m_ref[...] = jnp.full(m_ref.shape, MASKED, F32)
    acc_ref[...] = jnp.zeros(acc_ref.shape, F32)

    def sel_update(tiles):
        scores = []
        for kt in tiles:
            off = pl.multiple_of(kt * tq, tq)
            scores.append(_dot(ks_ref[0, pl.ds(off, tq), :], q_aug)
                          + bt_ref[0, jnp.minimum(i - kt, BT_FAR)])
        s_max = scores[0]
        for sc in scores[1:]:
            s_max = jnp.maximum(s_max, sc)
        m_prev = m_ref[...]
        m_new = jnp.maximum(m_prev, jnp.max(s_max, axis=0, keepdims=True))
        pv = _dot(jnp.concatenate([vst_ref[0, kt] for kt in tiles], axis=1),
                  jnp.concatenate([jnp.exp2(sc - m_new).astype(BF16) for sc in scores], axis=0))
        m_ref[...] = m_new
        acc_ref[...] = jnp.exp2(m_prev - m_new) * acc_ref[...] + pv

    def sel_pair(j, carry):
        sel_update([2 * j, 2 * j + 1])
        return carry

    lax.fori_loop(0, (i + 1) // 2, sel_pair, 0)

    @pl.when(i % 2 == 0)
    def _():
        sel_update([i])

    o_sel = acc_ref[0:dh, :] / acc_ref[dh:dh + 1, :]

    y_t = (gate[0] * o_cmp + gate[1] * o_sel + gate[2] * o_win).astype(BF16)
    eye_q = _eye(tq, BF16)
    o_ref[0] = jnp.concatenate([_dot_nt(eye_q, yh) for yh in per_head(y_t)],
                               axis=1).astype(BF16)


def _nsa_attention(proj, kv, kv_t, kvc, kvc_t, bias_c, bias_t, *, bsz, seq):
    g, tq, hpg, dh = NSA_KV_GROUPS, NSA_TQ, NSA_HPG, NSA_HEAD_DIM
    n_rows = kvc.shape[2]
    nq = seq // tq
    group_w = hpg * dh
    kv_spec = lambda pos: pl.BlockSpec((1, seq, LANES), lambda gi, b, i: (b, 0, pos * g + gi))
    kvt_spec = lambda pos: pl.BlockSpec((1, nq, LANES, tq), lambda gi, b, i: (b, 0, pos * g + gi, 0))
    kv3 = kv.reshape(bsz, seq, KV_COLS)
    proj3 = proj.reshape(bsz, seq, PROJ_COLS)
    out = pl.pallas_call(
        _nsa_body,
        out_shape=jax.ShapeDtypeStruct((bsz, seq, NSA_DIM), BF16),
        grid=(g, bsz, nq),
        in_specs=[
            pl.BlockSpec((1, tq, group_w), lambda gi, b, i: (b, i, COL_QNSA // group_w + gi)),
            pl.BlockSpec((1, 1, n_rows, dh), lambda gi, b, i: (0, b * g + gi, 0, 0)),
            pl.BlockSpec((1, 1, dh, n_rows), lambda gi, b, i: (1, b * g + gi, 0, 0)),
            kv_spec(ROW_K_SEL), kvt_spec(T_V_SEL), kv_spec(ROW_K_WIN), kvt_spec(T_V_WIN),
            pl.BlockSpec((1, 1, n_rows, hpg * tq), lambda gi, b, i: (gi, i, 0, 0)),
            pl.BlockSpec((1, BT_COUNT, tq, hpg * tq), lambda gi, b, i: (gi, 0, 0, 0)),
            pl.BlockSpec((1, tq, GATE_PAD), lambda gi, b, i: (b, i, COL_GNSA // GATE_PAD + gi)),
        ],
        out_specs=pl.BlockSpec((1, tq, group_w), lambda gi, b, i: (b, i, gi)),
        scratch_shapes=[pltpu.VMEM((1, hpg * tq), F32),
                        pltpu.VMEM((LANES, hpg * tq), F32)],
        compiler_params=_params(("parallel", "parallel", "arbitrary")),
        name="nsa_attention",
    )(proj3, kvc, kvc_t, kv3, kv_t, kv3, kv_t, bias_c, bias_t, proj3)
    return out.reshape(bsz * seq, NSA_DIM)


def _mem_body(q_ref, mem_ref, g_ref, w_ref, o_ref, kv_ref):
    @pl.when(pl.program_id(1) == 0)
    def _():
        kv_ref[...] = _dot(_rms(mem_ref[0], g_ref[...]).astype(BF16), w_ref[...]).astype(BF16)

    outs = []
    for h in range(MEM_HEADS):
        sl = slice(h * MEM_HEAD_DIM, (h + 1) * MEM_HEAD_DIM)
        qh = (q_ref[:, sl] * (MEM_HEAD_DIM ** -0.5)).astype(BF16)
        s = _dot_nt(qh, kv_ref[:, sl])
        e = jnp.exp(s - jnp.max(s, axis=-1, keepdims=True))
        p = e / jnp.sum(e, axis=-1, keepdims=True)
        outs.append(_dot(p.astype(BF16), kv_ref[:, MEM_DIM + h * MEM_HEAD_DIM:
                                                MEM_DIM + (h + 1) * MEM_HEAD_DIM]))
    o_ref[...] = jnp.concatenate(outs, axis=1)


def _mem_attention(proj, mem, gain, w_kv, *, seq, tq=1024):
    t = proj.shape[0]
    bsz, m, d = mem.shape
    per_seq = seq // tq
    return pl.pallas_call(
        _mem_body,
        out_shape=jax.ShapeDtypeStruct((t, MEM_DIM), F32),
        grid=(bsz, per_seq),
        in_specs=[
            pl.BlockSpec((tq, MEM_DIM), lambda b, j: (b * per_seq + j, COL_QMEM // MEM_DIM)),
            pl.BlockSpec((1, m, d), lambda b, j: (b, 0, 0)),
            pl.BlockSpec((1, d), lambda b, j: (0, 0)),
            pl.BlockSpec((d, 2 * MEM_DIM), lambda b, j: (0, 0), pipeline_mode=pl.Buffered(1)),
        ],
        out_specs=pl.BlockSpec((tq, MEM_DIM), lambda b, j: (b * per_seq + j, 0)),
        scratch_shapes=[pltpu.VMEM((m, 2 * MEM_DIM), BF16)],
        compiler_params=_params(("parallel", "arbitrary")),
        name="mem_attention",
    )(proj, mem, gain, w_kv)


def _merge_body(x_ref, gain_ref, yr_ref, yn_ref, ym_ref, wg_ref, wr_ref, wn_ref, wm_ref, wo_ref,
                o_ref):
    d = x_ref.shape[1]
    x = x_ref[...]
    h = _rms(x, gain_ref[...]).astype(BF16)
    merged = jnp.zeros(x.shape, F32)
    for b, (y_ref, w_ref) in enumerate(((yr_ref, wr_ref), (yn_ref, wn_ref), (ym_ref, wm_ref))):
        gate = jax.nn.sigmoid(_dot(h, wg_ref[:, b * d:(b + 1) * d]))
        merged = merged + gate * _dot(y_ref[...].astype(BF16), w_ref[...])
    o_ref[...] = x + _dot(merged.astype(BF16), wo_ref[...])


def _merge(x, gain, y_rwkv, y_nsa, y_mem, w_g, w_r, w_n, w_m, w_o, *, tm=1024):
    t, d = x.shape
    row = lambda i: (i, 0)
    const = lambda i: (0, 0)
    resident = lambda shape: pl.BlockSpec(shape, const, pipeline_mode=pl.Buffered(1))
    return pl.pallas_call(
        _merge_body,
        out_shape=jax.ShapeDtypeStruct((t, d), F32),
        grid=(t // tm,),
        in_specs=[
            pl.BlockSpec((tm, d), row),
            pl.BlockSpec((1, d), const),
            pl.BlockSpec((tm, RWKV_DIM), row),
            pl.BlockSpec((tm, NSA_DIM), row),
            pl.BlockSpec((tm, MEM_DIM), row),
            resident((d, N_BRANCH * d)),
            resident((RWKV_DIM, d)),
            resident((NSA_DIM, d)),
            resident((MEM_DIM, d)),
            resident((d, d)),
        ],
        out_specs=pl.BlockSpec((tm, d), row),
        compiler_params=_params(("parallel",)),
        name="merge",
    )(x, gain, y_rwkv, y_nsa, y_mem, w_g, w_r, w_n, w_m, w_o)


def _row(a):
    return a.reshape(1, -1)


def _in_proj(x, p, l, seq):
    d = x.shape[1]
    g, hpg, dh = NSA_KV_GROUPS, NSA_HPG, NSA_HEAD_DIM
    w_in = p['w_in'][l]
    o = 0
    parts = {}
    for name, size in (('rwkv', RWKV_PROJ), ('q', NSA_DIM), ('kv', KV_KINDS * NSA_KV_DIM),
                       ('g_nsa', 3 * NSA_HEADS), ('q_mem', MEM_DIM)):
        parts[name] = w_in[:, o:o + size]
        o += size
    gates = parts['g_nsa'].reshape(d, 3, g, hpg).transpose(0, 2, 1, 3).reshape(d, g, 3 * hpg)
    gates = jnp.pad(gates, ((0, 0), (0, 0), (0, GATE_PAD - 3 * hpg))).reshape(d, g * GATE_PAD)
    w_other = jnp.concatenate([parts['q'], parts['q_mem'], gates], axis=1).astype(BF16)
    w_kv = jnp.pad(parts['kv'].reshape(d, KV_KINDS, g, dh), ((0, 0), (0, 0), (0, 0), (0, LANES - dh)))
    w_row = w_kv[:, KV_ROW_KINDS, :, :].reshape(d, KV_COLS).astype(BF16)
    w_t = w_kv[:, KV_T_KINDS, :, :].reshape(d, KV_T_ROWS).T.astype(BF16)
    w_lora = jnp.zeros((LORA_DIM, 3 * RWKV_DIM), F32)
    w_lora = w_lora.at[0:DECAY_LORA, 0:RWKV_DIM].set(p['rwkv_w2'][l])
    w_lora = w_lora.at[DECAY_LORA:DECAY_LORA + AAA_LORA, RWKV_DIM:2 * RWKV_DIM].set(p['rwkv_a2'][l])
    w_lora = w_lora.at[DECAY_LORA + AAA_LORA:, 2 * RWKV_DIM:].set(p['rwkv_g2'][l])
    proj, kv, kv_t, *rwkv_in = _in_proj_call(
        x, seq, _row(p['mix_norm'][l]), parts['rwkv'].astype(BF16), w_other, w_row, w_t,
        _row(p['rwkv_mu'][l]), w_lora.astype(BF16), _row(p['rwkv_w0'][l]), _row(p['rwkv_a0'][l]),
        _row(p['rwkv_k_k'][l]), _row(p['rwkv_k_a'][l]))
    return proj, kv, kv_t, rwkv_in


def _rwkv_branch(rwkv_in, p, l, bsz, seq):
    return _rwkv_scan(*rwkv_in, _row(p['rwkv_r_k'][l]), _row(p['rwkv_gn_gain'][l]),
                      _row(p['rwkv_gn_bias'][l]), batch=bsz, seq=seq)


def _nsa_branch(proj, kv, kv_t, bias_c, bias_t, p, l, bsz, seq):
    w1 = jnp.stack([p['cmp_k_w1'][l], p['cmp_v_w1'][l]]).astype(BF16)
    pe = jnp.stack([p['cmp_pe_k'][l].reshape(1, -1), p['cmp_pe_v'][l].reshape(1, -1)])
    w2 = jnp.stack([p['cmp_k_w2'][l], p['cmp_v_w2'][l]]).astype(BF16)
    kvc, kvc_t = _compress(kv, w1, pe, w2, bsz=bsz, seq=seq)
    return _nsa_attention(proj, kv, kv_t, kvc, kvc_t, bias_c, bias_t, bsz=bsz, seq=seq)


def _mem_branch(proj, mem, p, l):
    w_kv = jnp.concatenate([p['mem_w_k'][l], p['mem_w_v'][l]], axis=1).astype(BF16)
    return _mem_attention(proj, mem, _row(p['mem_norm'][l]), w_kv, seq=proj.shape[0] // mem.shape[0])


def _layer(x, mem, l, bias_c, bias_t, p):
    bsz, seq, d = x.shape
    t = bsz * seq
    row = _row
    x = x.reshape(t, d)

    x = _ffn(x, row(p['ffn1_norm'][l]), p['ffn1_w_gate'][l].astype(BF16),
             p['ffn1_w_up'][l].astype(BF16), p['ffn1_w_down'][l].astype(BF16),
             row(p['final_norm']), final=False)

    proj, kv, kv_t, rwkv_in = _in_proj(x, p, l, seq)
    y_rwkv = _rwkv_branch(rwkv_in, p, l, bsz, seq)
    y_nsa = _nsa_branch(proj, kv, kv_t, bias_c, bias_t, p, l, bsz, seq)
    y_mem = _mem_branch(proj, mem, p, l)

    w_gate = p['w_in'][l][:, -N_BRANCH * d:].astype(BF16)
    x = _merge(x, row(p['mix_norm'][l]), y_rwkv, y_nsa, y_mem, w_gate,
               p['w_br_rwkv'][l].astype(BF16), p['w_br_nsa'][l].astype(BF16),
               p['w_br_mem'][l].astype(BF16), p['w_out'][l].astype(BF16))

    last = l == p['ffn1_norm'].shape[0] - 1
    x = _ffn(x, row(p['ffn2_norm'][l]), p['ffn2_w_gate'][l].astype(BF16),
             p['ffn2_w_up'][l].astype(BF16), p['ffn2_w_down'][l].astype(BF16),
             row(p['final_norm']), final=last)
    return x.reshape(bsz, seq, d)


def kernel(x, mem, ffn1_norm, ffn1_w_gate, ffn1_w_up, ffn1_w_down, mix_norm, w_in, rwkv_mu, rwkv_w0, rwkv_w2, rwkv_a0, rwkv_a2, rwkv_g2, rwkv_k_k, rwkv_k_a, rwkv_r_k, rwkv_gn_gain, rwkv_gn_bias, cmp_pe_k, cmp_k_w1, cmp_k_w2, cmp_pe_v, cmp_v_w1, cmp_v_w2, rel_bias, mem_norm, mem_w_k, mem_w_v, w_br_rwkv, w_br_nsa, w_br_mem, w_out, ffn2_norm, ffn2_w_gate, ffn2_w_up, ffn2_w_down, final_norm):
    p = dict(ffn1_norm=ffn1_norm, ffn1_w_gate=ffn1_w_gate, ffn1_w_up=ffn1_w_up,
             ffn1_w_down=ffn1_w_down, mix_norm=mix_norm, w_in=w_in, rwkv_mu=rwkv_mu,
             rwkv_w0=rwkv_w0, rwkv_w2=rwkv_w2, rwkv_a0=rwkv_a0, rwkv_a2=rwkv_a2, rwkv_g2=rwkv_g2,
             rwkv_k_k=rwkv_k_k, rwkv_k_a=rwkv_k_a, rwkv_r_k=rwkv_r_k, rwkv_gn_gain=rwkv_gn_gain,
             rwkv_gn_bias=rwkv_gn_bias, cmp_pe_k=cmp_pe_k, cmp_k_w1=cmp_k_w1, cmp_k_w2=cmp_k_w2,
             cmp_pe_v=cmp_pe_v, cmp_v_w1=cmp_v_w1, cmp_v_w2=cmp_v_w2, mem_norm=mem_norm,
             mem_w_k=mem_w_k, mem_w_v=mem_w_v, w_br_rwkv=w_br_rwkv, w_br_nsa=w_br_nsa,
             w_br_mem=w_br_mem, w_out=w_out, ffn2_norm=ffn2_norm, ffn2_w_gate=ffn2_w_gate,
             ffn2_w_up=ffn2_w_up, ffn2_w_down=ffn2_w_down, final_norm=final_norm)
    bias_c, bias_t = _bias_tables(rel_bias, x.shape[1])
    for l in range(ffn1_norm.shape[0]):
        x = _layer(x, mem, l, bias_c, bias_t, p)
    return x
```

```python
import functools
import math

import jax
import jax.numpy as jnp
from jax import lax
from jax.experimental import pallas as pl
from jax.experimental.pallas import tpu as pltpu

F32 = jnp.float32
BF16 = jnp.bfloat16
HI = lax.Precision.HIGHEST

NORM_EPS = 1e-6
RWKV_HEADS = 8
RWKV_HEAD_DIM = 64
RWKV_DIM = RWKV_HEADS * RWKV_HEAD_DIM
DECAY_LORA = 64
AAA_LORA = 64
GATE_LORA = 128
LORA_DIM = DECAY_LORA + AAA_LORA + GATE_LORA
RWKV_GN_EPS = 64e-5
RWKV_PROJ = 3 * RWKV_DIM + LORA_DIM
NSA_HEADS = 8
NSA_KV_GROUPS = 2
NSA_HPG = NSA_HEADS // NSA_KV_GROUPS
NSA_HEAD_DIM = 64
NSA_DIM = NSA_HEADS * NSA_HEAD_DIM
NSA_KV_DIM = NSA_KV_GROUPS * NSA_HEAD_DIM
CMP_LEN = 32
CMP_STRIDE = 16
CMP_HIDDEN = 256
SEL_BLOCK = 64
SEL_TOP_N = 16
WINDOW = 512
REL_BUCKETS = 32
REL_MAX_DIST = 128
MEM_HEADS = 4
MEM_HEAD_DIM = 128
MEM_DIM = MEM_HEADS * MEM_HEAD_DIM
N_BRANCH = 3

LANES = 128
GATE_PAD = LANES
COL_QNSA = 0
COL_QMEM = COL_QNSA + NSA_DIM
COL_GNSA = COL_QMEM + MEM_DIM
PROJ_COLS = COL_GNSA + NSA_KV_GROUPS * GATE_PAD
assert COL_QMEM % MEM_DIM == 0 and COL_GNSA % (NSA_KV_GROUPS * GATE_PAD) == 0 and COL_QNSA % NSA_DIM == 0
KV_KINDS = 6
KV_ROW_KINDS = (0, 1, 2, 4)
KV_T_KINDS = (3, 5)
KV_COLS = len(KV_ROW_KINDS) * NSA_KV_GROUPS * LANES
KV_T_ROWS = len(KV_T_KINDS) * NSA_KV_GROUPS * LANES
ROW_K_SEL, ROW_K_WIN = KV_ROW_KINDS.index(2), KV_ROW_KINDS.index(4)
T_V_SEL, T_V_WIN = KV_T_KINDS.index(3), KV_T_KINDS.index(5)
LOG2E = 1.4426950408889634

RWKV_CHUNK = 64
RWKV_INV_BLOCK = 16
NSA_TQ = 256
MASKED = -1e30
BT_DIAG, BT_PREV, BT_FAR, BT_WIN_EDGE, BT_NONE, BT_COUNT = 0, 1, 2, 3, 4, 5
V7X_VMEM_BYTES = 64 * 1024 * 1024
VMEM_LIMIT = V7X_VMEM_BYTES - 8 * 1024 * 1024


def _dot(a, b, precision=None):
    return jnp.dot(a, b, preferred_element_type=F32, precision=precision)


def _dot_nt(a, b, precision=None):
    return lax.dot_general(a, b, (((1,), (1,)), ((), ())), preferred_element_type=F32,
                           precision=precision)


def _params(semantics):
    return pltpu.CompilerParams(dimension_semantics=semantics, vmem_limit_bytes=VMEM_LIMIT)


def _rms(x, g):
    return x * lax.rsqrt(jnp.mean(x * x, axis=-1, keepdims=True) + NORM_EPS) * g


def _ffn_body(x_ref, g_ref, wg_ref, wu_ref, wd_ref, fg_ref, o_ref, *, tf, final):
    x = x_ref[...]
    h = _rms(x, g_ref[...]).astype(BF16)
    acc = jnp.zeros(x.shape, F32)
    for j in range(wg_ref.shape[1] // tf):
        cols = slice(j * tf, (j + 1) * tf)
        act = (jax.nn.silu(_dot(h, wg_ref[:, cols])) * _dot(h, wu_ref[:, cols])).astype(BF16)
        acc = acc + _dot(act, wd_ref[cols, :])
    y = x + 0.5 * acc
    if final:
        y = _rms(y, fg_ref[...])
    o_ref[...] = y


def _ffn(x, gain, wg, wu, wd, final_gain, *, final, tm=1024, tf=256):
    t, d = x.shape
    f = wg.shape[1]
    resident = lambda shape: pl.BlockSpec(shape, lambda i: (0, 0), pipeline_mode=pl.Buffered(1))
    return pl.pallas_call(
        functools.partial(_ffn_body, tf=tf, final=final),
        out_shape=jax.ShapeDtypeStruct((t, d), F32),
        grid=(t // tm,),
        in_specs=[
            pl.BlockSpec((tm, d), lambda i: (i, 0)),
            pl.BlockSpec((1, d), lambda i: (0, 0)),
            resident((d, f)),
            resident((d, f)),
            resident((f, d)),
            pl.BlockSpec((1, d), lambda i: (0, 0)),
        ],
        out_specs=pl.BlockSpec((tm, d), lambda i: (i, 0)),
        compiler_params=_params(("parallel",)),
        name="ffn_final" if final else "ffn",
    )(x, gain, wg, wu, wd, final_gain)


def _write_kv(h, w_ref, wt_ref, o_ref, ot_ref, *, seq):
    dh, tk, tm = NSA_HEAD_DIM, NSA_TQ, h.shape[0]
    y = _dot(h, w_ref[...])
    row = lax.broadcasted_iota(jnp.int32, (tm, LANES), 0)
    lane = lax.broadcasted_iota(jnp.int32, (tm, LANES), 1)
    pos = (pl.program_id(0) * tm) % seq + row
    block_mark = jnp.where(lane - dh == (pos >> (SEL_BLOCK.bit_length() - 1)), MASKED, 0.0)
    for tile in range(KV_COLS // LANES):
        part = y[:, tile * LANES:(tile + 1) * LANES]
        if tile // NSA_KV_GROUPS == ROW_K_SEL:
            part = part + block_mark
        o_ref[:, tile * LANES:(tile + 1) * LANES] = part.astype(BF16)

    y_t = _dot_nt(wt_ref[...], h)
    row_t = lax.broadcasted_iota(jnp.int32, y_t.shape, 0)
    y_t = (y_t + jnp.where((row_t & (LANES - 1)) == dh, 1.0, 0.0)).astype(BF16)
    for c in range(tm // tk):
        ot_ref[0, c] = y_t[:, c * tk:(c + 1) * tk]


def _in_proj_body(x_ref, xp_ref, g_ref, wr_ref, wo_ref, wkv_ref, wkvt_ref, mu_ref, wl_ref, w0_ref,
                  a0_ref, kk_ref, ka_ref, proj_o, kv_o, kvt_o, r_o, k_o, v_o, kk_o, b_o, lw_o, g_o,
                  *, tiles_per_seq, seq):
    i = pl.program_id(0)
    gain = g_ref[...]
    h = _rms(x_ref[...], gain).astype(BF16)
    proj_o[...] = _dot(h, wo_ref[...])
    _write_kv(h, wkv_ref, wkvt_ref, kv_o, kvt_o, seq=seq)
    n_up = xp_ref.shape[0]
    h_up = _rms(xp_ref[...], gain).astype(BF16)
    p_all = _dot(jnp.concatenate([h_up, h], axis=0), wr_ref[...])
    p = p_all[n_up:]
    keep = jnp.where(i % tiles_per_seq == 0, 0.0, 1.0)
    prev_last = p_all[n_up - 1:n_up, :] * keep
    rows = lax.broadcasted_iota(jnp.int32, p.shape, 0)
    shifted = jnp.where(rows == 0, prev_last, pltpu.roll(p, 1, 0))
    x = p + (shifted - p) * mu_ref[...]

    r = x[:, 0:RWKV_DIM]
    k = x[:, RWKV_DIM:2 * RWKV_DIM]
    v = x[:, 2 * RWKV_DIM:3 * RWKV_DIM]
    s = x[:, 3 * RWKV_DIM:RWKV_PROJ]
    lane = lax.broadcasted_iota(jnp.int32, s.shape, 1)
    z = jnp.where(lane < DECAY_LORA, jnp.tanh(s),
                  jnp.where(lane < DECAY_LORA + AAA_LORA, s, jax.nn.sigmoid(s)))
    lo = _dot(z.astype(BF16), wl_ref[...])
    a = jax.nn.sigmoid(a0_ref[...] + lo[:, RWKV_DIM:2 * RWKV_DIM])

    kkr = k * kk_ref[...]
    sq = kkr * kkr
    sq_hi = sq.astype(BF16)
    sq_lo = (sq - sq_hi.astype(F32)).astype(BF16)
    shift = RWKV_HEAD_DIM.bit_length() - 1
    same_head = ((lax.broadcasted_iota(jnp.int32, (RWKV_DIM, RWKV_DIM), 0) >> shift)
                 == (lax.broadcasted_iota(jnp.int32, (RWKV_DIM, RWKV_DIM), 1) >> shift)).astype(BF16)
    ssq = _dot(sq_hi, same_head) + _dot(sq_lo, same_head)
    kk = kkr / jnp.maximum(jnp.sqrt(ssq), 1e-12)

    r_o[...] = r
    k_o[...] = k * (1.0 + (a - 1.0) * ka_ref[...])
    v_o[...] = v
    kk_o[...] = kk
    b_o[...] = kk * a
    lw_o[...] = -math.exp(-0.5) * jax.nn.sigmoid(w0_ref[...] + lo[:, 0:RWKV_DIM])
    g_o[...] = lo[:, 2 * RWKV_DIM:3 * RWKV_DIM]


def _in_proj_call(x, seq, gain, w_rwkv, w_other, w_kv, w_kv_t, mu, w_lora, w0, a0, k_k, k_a, *,
                  tm=512):
    t, d = x.shape
    tk = NSA_TQ
    per_seq = seq // tm
    row = lambda i: (i, 0)
    const = lambda i: (0, 0)
    resident = lambda shape: pl.BlockSpec(shape, const, pipeline_mode=pl.Buffered(1))
    vec = pl.BlockSpec((1, RWKV_DIM), const)
    tok = jax.ShapeDtypeStruct((t, RWKV_DIM), F32)
    return pl.pallas_call(
        functools.partial(_in_proj_body, tiles_per_seq=per_seq, seq=seq),
        out_shape=[jax.ShapeDtypeStruct((t, PROJ_COLS), F32),
                   jax.ShapeDtypeStruct((t, KV_COLS), BF16),
                   jax.ShapeDtypeStruct((t // seq, seq // tk, KV_T_ROWS, tk), BF16)] + [tok] * 7,
        grid=(t // tm,),
        in_specs=[
            pl.BlockSpec((tm, d), row),
            pl.BlockSpec((16, d), lambda i: (jnp.maximum(i * (tm // 16) - 1, 0), 0)),
            pl.BlockSpec((1, d), const),
            resident((d, RWKV_PROJ)),
            resident((d, PROJ_COLS)),
            resident((d, KV_COLS)),
            resident((KV_T_ROWS, d)),
            pl.BlockSpec((1, RWKV_PROJ), const),
            resident((LORA_DIM, 3 * RWKV_DIM)),
            vec, vec, vec, vec,
        ],
        out_specs=[pl.BlockSpec((tm, PROJ_COLS), row),
                   pl.BlockSpec((tm, KV_COLS), row),
                   pl.BlockSpec((1, tm // tk, KV_T_ROWS, tk),
                                lambda i: (i // per_seq, i % per_seq, 0, 0))]
        + [pl.BlockSpec((tm, RWKV_DIM), row)] * 7,
        compiler_params=_params(("parallel",)),
        name="in_proj",
    )(x, x, gain, w_rwkv, w_other, w_kv, w_kv_t, mu, w_lora, w0, a0, k_k, k_a)


def _rwkv_scan_body(r_ref, k_ref, v_ref, kk_ref, b_ref, lw_ref, g_ref, rk_ref, gg_ref, gb_ref,
                    o_ref, st_ref):
    c_sz, n, nh = RWKV_CHUNK, RWKV_HEAD_DIM, RWKV_HEADS

    @pl.when(pl.program_id(1) == 0)
    def _():
        st_ref[...] = jnp.zeros_like(st_ref)

    ri = lax.broadcasted_iota(jnp.int32, (c_sz, c_sz), 0)
    ci = lax.broadcasted_iota(jnp.int32, (c_sz, c_sz), 1)
    incl = ci <= ri
    eye_b = (ci == ri).astype(BF16)
    row2 = lax.broadcasted_iota(jnp.int32, (c_sz, 2 * c_sz), 0)
    lane2 = lax.broadcasted_iota(jnp.int32, (c_sz, 2 * c_sz), 1)
    right_half = lane2 >= c_sz
    zeros_b = jnp.zeros((c_sz, n), BF16)
    inv_shift = RWKV_INV_BLOCK.bit_length() - 1

    rows = []
    for bb in range(st_ref.shape[0]):
        lw = lw_ref[bb]
        cum = _dot(incl.astype(F32), lw, HI)
        cum_last = cum[c_sz - 1:c_sz, :]
        r, k, v, b = r_ref[bb], k_ref[bb], v_ref[bb], b_ref[bb]
        p_inv = jnp.exp(-cum)
        p_end = jnp.exp(cum_last - cum)
        rows.append(dict(
            left=jnp.concatenate([(-(kk_ref[bb] * jnp.exp(cum - lw))).astype(BF16),
                                  (r * jnp.exp(cum)).astype(BF16)], axis=0),
            bt=(b * p_inv).astype(BF16), kt=(k * p_inv).astype(BF16),
            bh=(b * p_end).astype(BF16), kh=(k * p_end).astype(BF16),
            v=v, v_b=v.astype(BF16), d_p=jnp.exp(cum_last), rk=r * k * rk_ref[...]))

    units = [(bb, h) for bb in range(len(rows)) for h in range(nh)]
    col = lambda name, u: rows[u[0]][name][:, u[1] * n:(u[1] + 1) * n]
    a_all = [_dot_nt(col('left', u), jnp.concatenate([col('bt', u), col('kt', u)], axis=0))
             for u in units]
    key2 = jnp.where(right_half, lane2 - c_sz, lane2)
    w_u = [jnp.where(right_half & (key2 < row2), a[:c_sz], 0.0).astype(BF16) for a in a_all]
    w_y = [jnp.where(key2 <= row2, a[c_sz:], 0.0).astype(BF16) for a in a_all]

    same_block = (row2 >> inv_shift) == (key2 >> inv_shift)
    x = [jnp.where((lane2 < row2) & same_block, a[:c_sz], jnp.where(lane2 == row2 + c_sz, 1.0, 0.0))
         for a in a_all]
    for _ in range(inv_shift):
        hi = [xu.astype(BF16) for xu in x]
        lo = [(xu - h_.astype(F32)).astype(BF16) for xu, h_ in zip(x, hi)]
        x = [_dot(h_[:, :c_sz], h_) + _dot(h_[:, :c_sz], l_) + _dot(l_[:, :c_sz], h_)
             + jnp.where(right_half, xu, 0.0) for xu, h_, l_ in zip(x, hi, lo)]
    x_b = [xu.astype(BF16) for xu in x]
    solve_diag = lambda j, z: _dot(x_b[j], jnp.concatenate([zeros_b, z.astype(BF16)], axis=0))
    q_b = [solve_diag(j, jnp.where((ci < ri) & ((ri >> inv_shift) != (ci >> inv_shift)),
                                   a[:c_sz, :c_sz], 0.0)).astype(BF16)
           for j, a in enumerate(a_all)]

    s0 = [st_ref[bb, h] for bb, h in units]
    ls0 = [_dot_nt(col('left', u), s0[j].astype(BF16)) for j, u in enumerate(units)]
    rhs = [ls0[j][:c_sz] + _dot(w_u[j], jnp.concatenate([zeros_b, col('v_b', u)], axis=0))
           for j, u in enumerate(units)]
    g0 = [solve_diag(j, rhs[j]) for j in range(len(units))]
    u_f = g0
    for _ in range(c_sz // RWKV_INV_BLOCK - 1):
        u_f = [g0[j] + _dot(q_b[j], u_f[j].astype(BF16)) for j in range(len(units))]
    u_b = [uj.astype(BF16) for uj in u_f]
    uv = [jnp.concatenate([u_b[j], col('v_b', u)], axis=0) for j, u in enumerate(units)]
    y = [ls0[j][c_sz:] + _dot(w_y[j], uv[j]) for j in range(len(units))]
    uv_t = [_dot_nt(eye_b, uv_j).astype(BF16) for uv_j in uv]
    for j, u in enumerate(units):
        st_ref[u[0], u[1]] = (s0[j] * col('d_p', u)
                              + _dot(uv_t[j], jnp.concatenate([col('bh', u), col('kh', u)], axis=0)))

    for bb in range(len(rows)):
        outs = []
        for h in range(nh):
            sl = slice(h * n, (h + 1) * n)
            yh = y[bb * nh + h]
            mean = jnp.mean(yh, axis=-1, keepdims=True)
            var = jnp.mean(jnp.square(yh - mean), axis=-1, keepdims=True)
            yn = (yh - mean) * lax.rsqrt(var + RWKV_GN_EPS)
            yn = yn * gg_ref[:, sl] + gb_ref[:, sl]
            bonus = jnp.sum(rows[bb]['rk'][:, sl], axis=-1, keepdims=True) * rows[bb]['v'][:, sl]
            outs.append((yn + bonus) * g_ref[bb, :, sl])
        o_ref[bb] = jnp.concatenate(outs, axis=1)


def _rwkv_scan(r, k, v, kk, b, lw, g, r_k, gn_gain, gn_bias, *, batch, seq, nb=4):
    t = r.shape[0]
    nc = seq // RWKV_CHUNK
    tok = pl.BlockSpec((nb, RWKV_CHUNK, RWKV_DIM), lambda bi, c: (bi, c, 0))
    par = pl.BlockSpec((1, RWKV_DIM), lambda bi, c: (0, 0))
    per_batch = lambda a: a.reshape(batch, seq, RWKV_DIM)
    out = pl.pallas_call(
        _rwkv_scan_body,
        out_shape=jax.ShapeDtypeStruct((batch, seq, RWKV_DIM), F32),
        grid=(batch // nb, nc),
        in_specs=[tok] * 7 + [par] * 3,
        out_specs=tok,
        scratch_shapes=[pltpu.VMEM((nb, RWKV_HEADS, RWKV_HEAD_DIM, RWKV_HEAD_DIM), F32)],
        compiler_params=_params(("parallel", "arbitrary")),
        name="rwkv_scan",
    )(*(per_batch(a) for a in (r, k, v, kk, b, lw, g)), r_k, gn_gain, gn_bias)
    return out.reshape(t, RWKV_DIM)


def _compress_body(x_ref, w1s_ref, w1_ref, pe_ref, w2_ref, w2t_ref, o_ref, ot_ref):
    n_rows = x_ref.shape[1]
    for kind in range(2):
        pe = jnp.broadcast_to(pe_ref[kind], (8, pe_ref.shape[2])).astype(BF16)
        pe_term = _dot(pe, w1_ref[kind])[0:1, :]
        for gi in range(NSA_KV_GROUPS):
            tile = kind * NSA_KV_GROUPS + gi
            both = jnp.zeros((n_rows, 2 * CMP_HIDDEN), F32)
            for l in range(CMP_STRIDE):
                both = both + _dot(x_ref[0, :, l, tile * LANES:(tile + 1) * LANES], w1s_ref[kind, l])
            second_next = pltpu.roll(both[:, CMP_HIDDEN:], n_rows - 1, 0)
            hid = both[:, :CMP_HIDDEN] + second_next + pe_term
            act = jax.nn.gelu(hid).astype(BF16)
            o_ref[kind, gi] = _dot(act, w2_ref[kind])
            ot_ref[kind, gi] = _dot_nt(w2t_ref[kind], act)


def _compress(kv, w1, pe, w2, *, bsz, seq):
    g, dh = NSA_KV_GROUPS, NSA_HEAD_DIM
    rows = seq // CMP_STRIDE
    w1r = jnp.pad(w1.reshape(2, CMP_LEN, dh, CMP_HIDDEN), ((0, 0), (0, 0), (0, LANES - dh), (0, 0)))
    w1s = jnp.concatenate([w1r[:, :CMP_STRIDE], w1r[:, CMP_STRIDE:]], axis=3)
    whole = lambda a: pl.BlockSpec(a.shape, lambda b: (0,) * a.ndim)
    w2t = w2.transpose(0, 2, 1)
    return pl.pallas_call(
        _compress_body,
        out_shape=[jax.ShapeDtypeStruct((2, bsz * g, rows, dh), F32),
                   jax.ShapeDtypeStruct((2, bsz * g, dh, rows), F32)],
        grid=(bsz,),
        in_specs=[pl.BlockSpec((1, rows, CMP_STRIDE, 2 * g * LANES), lambda b: (b, 0, 0, 0)),
                  whole(w1s), whole(w1), whole(pe), whole(w2), whole(w2t)],
        out_specs=[pl.BlockSpec((2, g, rows, dh), lambda b: (0, b, 0, 0)),
                   pl.BlockSpec((2, g, dh, rows), lambda b: (0, b, 0, 0))],
        compiler_params=_params(("parallel",)),
        name="nsa_compress",
    )(kv.reshape(bsz, rows, CMP_STRIDE, KV_COLS), w1s, w1, pe, w2, w2t)


def _t5_bucket(dist):
    n = jnp.maximum(dist, 0)
    exact = REL_BUCKETS // 2
    nf = jnp.maximum(n, 1).astype(F32)
    scaled = jnp.log(nf / exact) / math.log(REL_MAX_DIST / exact) * (REL_BUCKETS - exact)
    large = exact + jnp.floor(scaled).astype(jnp.int32)
    large = jnp.minimum(large, REL_BUCKETS - 1)
    return jnp.where(n < exact, n, large)


def _bias_body(tab_ref, bc_ref, bt_ref, *, seq, n_cmp_pad):
    h = pl.program_id(0)
    tq = NSA_TQ

    def lookup(dist):
        bucket = _t5_bucket(dist)
        out = jnp.zeros(dist.shape, F32)
        for bkt in range(REL_BUCKETS):
            out = jnp.where(bucket == bkt, tab_ref[bkt, h] * LOG2E, out)
        return out

    far = tab_ref[REL_BUCKETS - 1, h] * LOG2E
    assert tq + 1 >= REL_MAX_DIST and WINDOW - tq + 1 >= REL_MAX_DIST

    key = lax.broadcasted_iota(jnp.int32, (tq, tq), 0)
    qry = lax.broadcasted_iota(jnp.int32, (tq, tq), 1)
    bt_ref[0, BT_DIAG] = jnp.where(qry >= key, lookup(qry - key), MASKED)
    bt_ref[0, BT_PREV] = lookup(tq + qry - key)
    bt_ref[0, BT_FAR] = jnp.full((tq, tq), far, F32)
    bt_ref[0, BT_WIN_EDGE] = jnp.where(qry < key, far, MASKED)
    bt_ref[0, BT_NONE] = jnp.full((tq, tq), MASKED, F32)

    per_tile = tq // CMP_STRIDE
    pad = 16
    assert pad * CMP_STRIDE >= REL_MAX_DIST + CMP_LEN - 1 and pad % 8 == 0
    band = per_tile + pad
    cmp_end = lax.broadcasted_iota(jnp.int32, (band, tq), 0) * CMP_STRIDE + CMP_LEN - 1
    qry_c = lax.broadcasted_iota(jnp.int32, (band, tq), 1)

    def cmp_tile(i, carry):
        start = pl.multiple_of(jnp.maximum(i * per_tile - pad, 0), 8)
        bc_ref[0, i] = jnp.full((n_cmp_pad, tq), far, F32)
        bc_ref[0, i, pl.ds(start, band), :] = lookup(i * tq + qry_c - (start * CMP_STRIDE + cmp_end))
        return carry

    lax.fori_loop(0, seq // tq, cmp_tile, 0)


def _bias_tables(rel_bias, seq):
    g, hpg, tq = NSA_KV_GROUPS, NSA_HPG, NSA_TQ
    n_cmp_pad = seq // CMP_STRIDE
    nq = seq // tq
    return pl.pallas_call(
        functools.partial(_bias_body, seq=seq, n_cmp_pad=n_cmp_pad),
        out_shape=[jax.ShapeDtypeStruct((g, nq, n_cmp_pad, hpg * tq), F32),
                   jax.ShapeDtypeStruct((g, BT_COUNT, tq, hpg * tq), F32)],
        grid=(NSA_HEADS,),
        in_specs=[pl.BlockSpec(memory_space=pltpu.SMEM)],
        out_specs=[pl.BlockSpec((1, nq, n_cmp_pad, tq), lambda h: (h // hpg, 0, 0, h % hpg)),
                   pl.BlockSpec((1, BT_COUNT, tq, tq), lambda h: (h // hpg, 0, 0, h % hpg))],
        compiler_params=_params(("parallel",)),
        name="nsa_bias",
    )(rel_bias)


def _eye(n, dtype):
    return (lax.broadcasted_iota(jnp.int32, (n, n), 0)
            == lax.broadcasted_iota(jnp.int32, (n, n), 1)).astype(dtype)


def _nsa_body(q_ref, kc_ref, vct_ref, ks_ref, vst_ref, kw_ref, vwt_ref, bc_ref, bt_ref, gate_ref,
              o_ref, m_ref, acc_ref):
    tq, hpg, dh, n_grp = NSA_TQ, NSA_HPG, NSA_HEAD_DIM, NSA_KV_GROUPS
    rws = hpg * tq
    n_blk_log2 = SEL_BLOCK.bit_length() - 1
    n_blk = ks_ref.shape[1] // SEL_BLOCK
    n_cmp_pad = kc_ref.shape[2]
    i = pl.program_id(1)
    n_win = WINDOW // tq
    lanes_of = lambda gi: slice(gi * LANES, (gi + 1) * LANES)

    def per_head(x):
        return [x[:, hh * tq:(hh + 1) * tq] for hh in range(hpg)]

    def before_loop(gi):
        xq = (q_ref[0, :, gi * hpg * dh:(gi + 1) * hpg * dh] * (dh ** -0.5 * LOG2E)).astype(BF16)
        eye_d = _eye(dh, BF16)
        q_t = jnp.concatenate([_dot_nt(eye_d, xq[:, hh * dh:(hh + 1) * dh]) for hh in range(hpg)],
                              axis=1).astype(BF16)

        cmp_id = lax.broadcasted_iota(jnp.int32, (n_cmp_pad, rws), 0)
        t_pos = i * tq + (lax.broadcasted_iota(jnp.int32, (n_cmp_pad, rws), 1) & (tq - 1))
        valid = (t_pos - (cmp_id * CMP_STRIDE + CMP_LEN - 1) >= 0) & (cmp_id < n_cmp_pad - 1)
        s = jnp.where(valid, _dot(kc_ref[0, gi].astype(BF16), q_t) + bc_ref[gi, 0], MASKED)
        e = jnp.where(valid, jnp.exp2(s - jnp.max(s, axis=0, keepdims=True)), 0.0)
        den = jnp.sum(e, axis=0, keepdims=True)
        p_c = e / jnp.where(den > 0.0, den, 1.0)
        o_cmp = _dot(vct_ref[0, gi].astype(BF16), p_c.astype(BF16))

        p_heads = per_head(p_c)
        p_sum = p_heads[0]
        for ph in p_heads[1:]:
            p_sum = p_sum + ph
        blk_o = lax.broadcasted_iota(jnp.int32, (n_blk, n_cmp_pad), 0)
        cmp_o = lax.broadcasted_iota(jnp.int32, (n_blk, n_cmp_pad), 1)
        overlap_t = ((cmp_o * CMP_STRIDE <= blk_o * SEL_BLOCK + SEL_BLOCK - 1)
                     & (cmp_o * CMP_STRIDE + CMP_LEN - 1 >= blk_o * SEL_BLOCK)).astype(F32)
        imp = _dot(overlap_t, p_sum, HI)
        jj = lax.broadcasted_iota(jnp.int32, (n_blk, tq), 0)
        cur = (i * tq + lax.broadcasted_iota(jnp.int32, (n_blk, tq), 1)) >> n_blk_log2
        forced = (jj == 0) | (jj == cur) | (jj == cur - 1)
        imp = jnp.where(jj > cur, -1e6, jnp.where(forced, 1e6, imp))
        rank = jnp.zeros((n_blk, tq), jnp.int32)
        for a in range(n_blk):
            row = imp[a:a + 1, :]
            beats = (row > imp) | ((row == imp) & (a < jj))
            rank = rank + beats.astype(jnp.int32)
        not_sel = jnp.where(rank < SEL_TOP_N, 0.0, 1.0).astype(BF16)

        q_aug = jnp.concatenate([q_t, jnp.concatenate([not_sel] * hpg, axis=1),
                                 jnp.zeros((LANES - dh - n_blk, rws), BF16)], axis=0)
        q_pad = jnp.concatenate([q_t, jnp.zeros((LANES - dh, rws), BF16)], axis=0)

        tiles, scores = [], []
        for delta in range(n_win + 1):
            entry = {0: BT_DIAG, 1: BT_PREV, n_win: BT_WIN_EDGE}.get(delta, BT_FAR)
            if delta > 0:
                entry = jnp.where(i - delta >= 0, entry, BT_NONE)
            tiles.append(jnp.maximum(i - delta, 0))
            off = pl.multiple_of(tiles[-1] * tq, tq)
            scores.append(_dot(kw_ref[0, pl.ds(off, tq), lanes_of(gi)], q_pad) + bt_ref[gi, entry])
        m_all = scores[0]
        for sc in scores[1:]:
            m_all = jnp.maximum(m_all, sc)
        m_w = jnp.max(m_all, axis=0, keepdims=True)
        acc_w = jnp.zeros((LANES, rws), F32)
        for kt, sc in zip(tiles, scores):
            acc_w = acc_w + _dot(vwt_ref[0, kt, lanes_of(gi), :], jnp.exp2(sc - m_w).astype(BF16))
        o_win = acc_w[0:dh] / acc_w[dh:dh + 1]

        g_t = _dot_nt(_eye(GATE_PAD, F32), gate_ref[0, :, gi * GATE_PAD:(gi + 1) * GATE_PAD], HI)
        sig = jax.nn.sigmoid(g_t[0:4 * hpg])
        gate = [jnp.concatenate([sig[br * hpg + hh:br * hpg + hh + 1] for hh in range(hpg)], axis=1)
                for br in range(3)]
        return q_aug, gate[0] * o_cmp + gate[2] * o_win, gate[1]

    prepared = [before_loop(gi) for gi in range(n_grp)]

    m_ref[...] = jnp.full(m_ref.shape, MASKED, F32)
    acc_ref[...] = jnp.zeros(acc_ref.shape, F32)

    def sel_update(tiles):
        for gi in range(n_grp):
            q_aug = prepared[gi][0]
            scores = []
            for kt in tiles:
                off = pl.multiple_of(kt * tq, tq)
                scores.append(_dot(ks_ref[0, pl.ds(off, tq), lanes_of(gi)], q_aug)
                              + bt_ref[gi, jnp.minimum(i - kt, BT_FAR)])
            s_max = scores[0]
            for sc in scores[1:]:
                s_max = jnp.maximum(s_max, sc)
            m_prev = m_ref[gi]
            m_new = jnp.maximum(m_prev, jnp.max(s_max, axis=0, keepdims=True))
            acc = jnp.exp2(m_prev - m_new) * acc_ref[gi]
            for kt, sc in zip(tiles, scores):
                acc = acc + _dot(vst_ref[0, kt, lanes_of(gi), :], jnp.exp2(sc - m_new).astype(BF16))
            m_ref[gi] = m_new
            acc_ref[gi] = acc

    def sel_pair(j, carry):
        sel_update([2 * j, 2 * j + 1])
        return carry

    lax.fori_loop(0, (i + 1) // 2, sel_pair, 0)

    @pl.when(i % 2 == 0)
    def _():
        sel_update([i])

    eye_q = _eye(tq, BF16)
    outs = []
    for gi in range(n_grp):
        _, cmp_win, gate_sel = prepared[gi]
        o_sel = acc_ref[gi, 0:dh, :] / acc_ref[gi, dh:dh + 1, :]
        y_t = (cmp_win + gate_sel * o_sel).astype(BF16)
        outs += [_dot_nt(eye_q, yh) for yh in per_head(y_t)]
    o_ref[0] = jnp.concatenate(outs, axis=1).astype(BF16)


def _nsa_attention(proj, kv, kv_t, kvc, kvc_t, bias_c, bias_t, *, bsz, seq):
    g, tq, hpg, dh = NSA_KV_GROUPS, NSA_TQ, NSA_HPG, NSA_HEAD_DIM
    n_rows = kvc.shape[2]
    nq = seq // tq
    kv_spec = lambda pos: pl.BlockSpec((1, seq, g * LANES), lambda b, i: (b, 0, pos))
    kvt_spec = lambda pos: pl.BlockSpec((1, nq, g * LANES, tq), lambda b, i: (b, 0, pos, 0))
    kv3 = kv.reshape(bsz, seq, KV_COLS)
    proj3 = proj.reshape(bsz, seq, PROJ_COLS)
    out = pl.pallas_call(
        _nsa_body,
        out_shape=jax.ShapeDtypeStruct((bsz, seq, NSA_DIM), BF16),
        grid=(bsz, nq),
        in_specs=[
            pl.BlockSpec((1, tq, NSA_DIM), lambda b, i: (b, i, COL_QNSA // NSA_DIM)),
            pl.BlockSpec((1, g, n_rows, dh), lambda b, i: (0, b, 0, 0)),
            pl.BlockSpec((1, g, dh, n_rows), lambda b, i: (1, b, 0, 0)),
            kv_spec(ROW_K_SEL), kvt_spec(T_V_SEL), kv_spec(ROW_K_WIN), kvt_spec(T_V_WIN),
            pl.BlockSpec((g, 1, n_rows, hpg * tq), lambda b, i: (0, i, 0, 0)),
            pl.BlockSpec((g, BT_COUNT, tq, hpg * tq), lambda b, i: (0, 0, 0, 0),
                         pipeline_mode=pl.Buffered(1)),
            pl.BlockSpec((1, tq, g * GATE_PAD), lambda b, i: (b, i, COL_GNSA // (g * GATE_PAD))),
        ],
        out_specs=pl.BlockSpec((1, tq, NSA_DIM), lambda b, i: (b, i, 0)),
        scratch_shapes=[pltpu.VMEM((g, 1, hpg * tq), F32),
                        pltpu.VMEM((g, LANES, hpg * tq), F32)],
        compiler_params=_params(("parallel", "arbitrary")),
        name="nsa_attention",
    )(proj3, kvc, kvc_t, kv3, kv_t, kv3, kv_t, bias_c, bias_t, proj3)
    return out.reshape(bsz * seq, NSA_DIM)


def _mem_body(q_ref, mem_ref, g_ref, w_ref, o_ref, kv_ref):
    @pl.when(pl.program_id(1) == 0)
    def _():
        kv_ref[...] = _dot(_rms(mem_ref[0], g_ref[...]).astype(BF16), w_ref[...]).astype(BF16)

    outs = []
    for h in range(MEM_HEADS):
        sl = slice(h * MEM_HEAD_DIM, (h + 1) * MEM_HEAD_DIM)
        qh = (q_ref[:, sl] * (MEM_HEAD_DIM ** -0.5)).astype(BF16)
        s = _dot_nt(qh, kv_ref[:, sl])
        e = jnp.exp(s - jnp.max(s, axis=-1, keepdims=True))
        p = e / jnp.sum(e, axis=-1, keepdims=True)
        outs.append(_dot(p.astype(BF16), kv_ref[:, MEM_DIM + h * MEM_HEAD_DIM:
                                                MEM_DIM + (h + 1) * MEM_HEAD_DIM]))
    o_ref[...] = jnp.concatenate(outs, axis=1)


def _mem_attention(proj, mem, gain, w_kv, *, seq, tq=1024):
    t = proj.shape[0]
    bsz, m, d = mem.shape
    per_seq = seq // tq
    return pl.pallas_call(
        _mem_body,
        out_shape=jax.ShapeDtypeStruct((t, MEM_DIM), F32),
        grid=(bsz, per_seq),
        in_specs=[
            pl.BlockSpec((tq, MEM_DIM), lambda b, j: (b * per_seq + j, COL_QMEM // MEM_DIM)),
            pl.BlockSpec((1, m, d), lambda b, j: (b, 0, 0)),
            pl.BlockSpec((1, d), lambda b, j: (0, 0)),
            pl.BlockSpec((d, 2 * MEM_DIM), lambda b, j: (0, 0), pipeline_mode=pl.Buffered(1)),
        ],
        out_specs=pl.BlockSpec((tq, MEM_DIM), lambda b, j: (b * per_seq + j, 0)),
        scratch_shapes=[pltpu.VMEM((m, 2 * MEM_DIM), BF16)],
        compiler_params=_params(("parallel", "arbitrary")),
        name="mem_attention",
    )(proj, mem, gain, w_kv)


def _merge_body(x_ref, gain_ref, yr_ref, yn_ref, ym_ref, wg_ref, wr_ref, wn_ref, wm_ref, wo_ref,
                o_ref):
    d = x_ref.shape[1]
    x = x_ref[...]
    h = _rms(x, gain_ref[...]).astype(BF16)
    merged = jnp.zeros(x.shape, F32)
    for b, (y_ref, w_ref) in enumerate(((yr_ref, wr_ref), (yn_ref, wn_ref), (ym_ref, wm_ref))):
        gate = jax.nn.sigmoid(_dot(h, wg_ref[:, b * d:(b + 1) * d]))
        merged = merged + gate * _dot(y_ref[...].astype(BF16), w_ref[...])
    o_ref[...] = x + _dot(merged.astype(BF16), wo_ref[...])


def _merge(x, gain, y_rwkv, y_nsa, y_mem, w_g, w_r, w_n, w_m, w_o, *, tm=1024):
    t, d = x.shape
    row = lambda i: (i, 0)
    const = lambda i: (0, 0)
    resident = lambda shape: pl.BlockSpec(shape, const, pipeline_mode=pl.Buffered(1))
    return pl.pallas_call(
        _merge_body,
        out_shape=jax.ShapeDtypeStruct((t, d), F32),
        grid=(t // tm,),
        in_specs=[
            pl.BlockSpec((tm, d), row),
            pl.BlockSpec((1, d), const),
            pl.BlockSpec((tm, RWKV_DIM), row),
            pl.BlockSpec((tm, NSA_DIM), row),
            pl.BlockSpec((tm, MEM_DIM), row),
            resident((d, N_BRANCH * d)),
            resident((RWKV_DIM, d)),
            resident((NSA_DIM, d)),
            resident((MEM_DIM, d)),
            resident((d, d)),
        ],
        out_specs=pl.BlockSpec((tm, d), row),
        compiler_params=_params(("parallel",)),
        name="merge",
    )(x, gain, y_rwkv, y_nsa, y_mem, w_g, w_r, w_n, w_m, w_o)


def _row(a):
    return a.reshape(1, -1)


def _in_proj(x, p, l, seq):
    d = x.shape[1]
    g, hpg, dh = NSA_KV_GROUPS, NSA_HPG, NSA_HEAD_DIM
    w_in = p['w_in'][l]
    o = 0
    parts = {}
    for name, size in (('rwkv', RWKV_PROJ), ('q', NSA_DIM), ('kv', KV_KINDS * NSA_KV_DIM),
                       ('g_nsa', 3 * NSA_HEADS), ('q_mem', MEM_DIM)):
        parts[name] = w_in[:, o:o + size]
        o += size
    gates = parts['g_nsa'].reshape(d, 3, g, hpg).transpose(0, 2, 1, 3).reshape(d, g, 3 * hpg)
    gates = jnp.pad(gates, ((0, 0), (0, 0), (0, GATE_PAD - 3 * hpg))).reshape(d, g * GATE_PAD)
    w_other = jnp.concatenate([parts['q'], parts['q_mem'], gates], axis=1).astype(BF16)
    w_kv = jnp.pad(parts['kv'].reshape(d, KV_KINDS, g, dh), ((0, 0), (0, 0), (0, 0), (0, LANES - dh)))
    w_row = w_kv[:, KV_ROW_KINDS, :, :].reshape(d, KV_COLS).astype(BF16)
    w_t = w_kv[:, KV_T_KINDS, :, :].reshape(d, KV_T_ROWS).T.astype(BF16)
    w_lora = jnp.zeros((LORA_DIM, 3 * RWKV_DIM), F32)
    w_lora = w_lora.at[0:DECAY_LORA, 0:RWKV_DIM].set(p['rwkv_w2'][l])
    w_lora = w_lora.at[DECAY_LORA:DECAY_LORA + AAA_LORA, RWKV_DIM:2 * RWKV_DIM].set(p['rwkv_a2'][l])
    w_lora = w_lora.at[DECAY_LORA + AAA_LORA:, 2 * RWKV_DIM:].set(p['rwkv_g2'][l])
    proj, kv, kv_t, *rwkv_in = _in_proj_call(
        x, seq, _row(p['mix_norm'][l]), parts['rwkv'].astype(BF16), w_other, w_row, w_t,
        _row(p['rwkv_mu'][l]), w_lora.astype(BF16), _row(p['rwkv_w0'][l]), _row(p['rwkv_a0'][l]),
        _row(p['rwkv_k_k'][l]), _row(p['rwkv_k_a'][l]))
    return proj, kv, kv_t, rwkv_in


def _rwkv_branch(rwkv_in, p, l, bsz, seq):
    return _rwkv_scan(*rwkv_in, _row(p['rwkv_r_k'][l]), _row(p['rwkv_gn_gain'][l]),
                      _row(p['rwkv_gn_bias'][l]), batch=bsz, seq=seq)


def _nsa_branch(proj, kv, kv_t, bias_c, bias_t, p, l, bsz, seq):
    w1 = jnp.stack([p['cmp_k_w1'][l], p['cmp_v_w1'][l]]).astype(BF16)
    pe = jnp.stack([p['cmp_pe_k'][l].reshape(1, -1), p['cmp_pe_v'][l].reshape(1, -1)])
    w2 = jnp.stack([p['cmp_k_w2'][l], p['cmp_v_w2'][l]]).astype(BF16)
    kvc, kvc_t = _compress(kv, w1, pe, w2, bsz=bsz, seq=seq)
    return _nsa_attention(proj, kv, kv_t, kvc, kvc_t, bias_c, bias_t, bsz=bsz, seq=seq)


def _mem_branch(proj, mem, p, l):
    w_kv = jnp.concatenate([p['mem_w_k'][l], p['mem_w_v'][l]], axis=1).astype(BF16)
    return _mem_attention(proj, mem, _row(p['mem_norm'][l]), w_kv, seq=proj.shape[0] // mem.shape[0])


def _layer(x, mem, l, bias_c, bias_t, p):
    bsz, seq, d = x.shape
    t = bsz * seq
    row = _row
    x = x.reshape(t, d)

    x = _ffn(x, row(p['ffn1_norm'][l]), p['ffn1_w_gate'][l].astype(BF16),
             p['ffn1_w_up'][l].astype(BF16), p['ffn1_w_down'][l].astype(BF16),
             row(p['final_norm']), final=False)

    proj, kv, kv_t, rwkv_in = _in_proj(x, p, l, seq)
    y_rwkv = _rwkv_branch(rwkv_in, p, l, bsz, seq)
    y_nsa = _nsa_branch(proj, kv, kv_t, bias_c, bias_t, p, l, bsz, seq)
    y_mem = _mem_branch(proj, mem, p, l)

    w_gate = p['w_in'][l][:, -N_BRANCH * d:].astype(BF16)
    x = _merge(x, row(p['mix_norm'][l]), y_rwkv, y_nsa, y_mem, w_gate,
               p['w_br_rwkv'][l].astype(BF16), p['w_br_nsa'][l].astype(BF16),
               p['w_br_mem'][l].astype(BF16), p['w_out'][l].astype(BF16))

    last = l == p['ffn1_norm'].shape[0] - 1
    x = _ffn(x, row(p['ffn2_norm'][l]), p['ffn2_w_gate'][l].astype(BF16),
             p['ffn2_w_up'][l].astype(BF16), p['ffn2_w_down'][l].astype(BF16),
             row(p['final_norm']), final=last)
    return x.reshape(bsz, seq, d)


def kernel(x, mem, ffn1_norm, ffn1_w_gate, ffn1_w_up, ffn1_w_down, mix_norm, w_in, rwkv_mu, rwkv_w0, rwkv_w2, rwkv_a0, rwkv_a2, rwkv_g2, rwkv_k_k, rwkv_k_a, rwkv_r_k, rwkv_gn_gain, rwkv_gn_bias, cmp_pe_k, cmp_k_w1, cmp_k_w2, cmp_pe_v, cmp_v_w1, cmp_v_w2, rel_bias, mem_norm, mem_w_k, mem_w_v, w_br_rwkv, w_br_nsa, w_br_mem, w_out, ffn2_norm, ffn2_w_gate, ffn2_w_up, ffn2_w_down, final_norm):
    p = dict(ffn1_norm=ffn1_norm, ffn1_w_gate=ffn1_w_gate, ffn1_w_up=ffn1_w_up,
             ffn1_w_down=ffn1_w_down, mix_norm=mix_norm, w_in=w_in, rwkv_mu=rwkv_mu,
             rwkv_w0=rwkv_w0, rwkv_w2=rwkv_w2, rwkv_a0=rwkv_a0, rwkv_a2=rwkv_a2, rwkv_g2=rwkv_g2,
             rwkv_k_k=rwkv_k_k, rwkv_k_a=rwkv_k_a, rwkv_r_k=rwkv_r_k, rwkv_gn_gain=rwkv_gn_gain,
             rwkv_gn_bias=rwkv_gn_bias, cmp_pe_k=cmp_pe_k, cmp_k_w1=cmp_k_w1, cmp_k_w2=cmp_k_w2,
             cmp_pe_v=cmp_pe_v, cmp_v_w1=cmp_v_w1, cmp_v_w2=cmp_v_w2, mem_norm=mem_norm,
             mem_w_k=mem_w_k, mem_w_v=mem_w_v, w_br_rwkv=w_br_rwkv, w_br_nsa=w_br_nsa,
             w_br_mem=w_br_mem, w_out=w_out, ffn2_norm=ffn2_norm, ffn2_w_gate=ffn2_w_gate,
             ffn2_w_up=ffn2_w_up, ffn2_w_down=ffn2_w_down, final_norm=final_norm)
    bias_c, bias_t = _bias_tables(rel_bias, x.shape[1])
    for l in range(ffn1_norm.shape[0]):
        x = _layer(x, mem, l, bias_c, bias_t, p)
    return x
```

```python
import functools
import math

import jax
import jax.numpy as jnp
from jax import lax
from jax.experimental import pallas as pl
from jax.experimental.pallas import tpu as pltpu

F32 = jnp.float32
BF16 = jnp.bfloat16
HI = lax.Precision.HIGHEST

NORM_EPS = 1e-6
RWKV_HEADS = 8
RWKV_HEAD_DIM = 64
RWKV_DIM = RWKV_HEADS * RWKV_HEAD_DIM
DECAY_LORA = 64
AAA_LORA = 64
GATE_LORA = 128
LORA_DIM = DECAY_LORA + AAA_LORA + GATE_LORA
RWKV_GN_EPS = 64e-5
RWKV_PROJ = 3 * RWKV_DIM + LORA_DIM
NSA_HEADS = 8
NSA_KV_GROUPS = 2
NSA_HPG = NSA_HEADS // NSA_KV_GROUPS
NSA_HEAD_DIM = 64
NSA_DIM = NSA_HEADS * NSA_HEAD_DIM
NSA_KV_DIM = NSA_KV_GROUPS * NSA_HEAD_DIM
CMP_LEN = 32
CMP_STRIDE = 16
CMP_HIDDEN = 256
SEL_BLOCK = 64
SEL_TOP_N = 16
WINDOW = 512
REL_BUCKETS = 32
REL_MAX_DIST = 128
MEM_HEADS = 4
MEM_HEAD_DIM = 128
MEM_DIM = MEM_HEADS * MEM_HEAD_DIM
N_BRANCH = 3

LANES = 128
GATE_PAD = LANES
COL_QNSA = 0
COL_QMEM = COL_QNSA + NSA_DIM
COL_GNSA = COL_QMEM + MEM_DIM
PROJ_COLS = COL_GNSA + NSA_KV_GROUPS * GATE_PAD
assert COL_QMEM % MEM_DIM == 0 and COL_GNSA % (NSA_KV_GROUPS * GATE_PAD) == 0 and COL_QNSA % NSA_DIM == 0
KV_KINDS = 6
KV_ROW_KINDS = (0, 1, 2, 4)
KV_T_KINDS = (3, 5)
KV_COLS = len(KV_ROW_KINDS) * NSA_KV_GROUPS * LANES
KV_T_ROWS = len(KV_T_KINDS) * NSA_KV_GROUPS * LANES
ROW_K_SEL, ROW_K_WIN = KV_ROW_KINDS.index(2), KV_ROW_KINDS.index(4)
T_V_SEL, T_V_WIN = KV_T_KINDS.index(3), KV_T_KINDS.index(5)
LOG2E = 1.4426950408889634

RWKV_CHUNK = 64
RWKV_INV_BLOCK = 16
NSA_TQ = 256
NSA_QT = 2
MASKED = -1e30
BT_DIAG, BT_PREV, BT_FAR, BT_WIN_EDGE, BT_NONE, BT_COUNT = 0, 1, 2, 3, 4, 5
V7X_VMEM_BYTES = 64 * 1024 * 1024
VMEM_LIMIT = V7X_VMEM_BYTES - 8 * 1024 * 1024


def _dot(a, b, precision=None):
    return jnp.dot(a, b, preferred_element_type=F32, precision=precision)


def _dot_nt(a, b, precision=None):
    return lax.dot_general(a, b, (((1,), (1,)), ((), ())), preferred_element_type=F32,
                           precision=precision)


def _params(semantics):
    return pltpu.CompilerParams(dimension_semantics=semantics, vmem_limit_bytes=VMEM_LIMIT)


def _rms(x, g):
    return x * lax.rsqrt(jnp.mean(x * x, axis=-1, keepdims=True) + NORM_EPS) * g


def _ffn_body(x_ref, g_ref, wg_ref, wu_ref, wd_ref, fg_ref, o_ref, *, tf, final):
    x = x_ref[...]
    h = _rms(x, g_ref[...]).astype(BF16)
    acc = jnp.zeros(x.shape, F32)
    for j in range(wg_ref.shape[1] // tf):
        cols = slice(j * tf, (j + 1) * tf)
        act = (jax.nn.silu(_dot(h, wg_ref[:, cols])) * _dot(h, wu_ref[:, cols])).astype(BF16)
        acc = acc + _dot(act, wd_ref[cols, :])
    y = x + 0.5 * acc
    if final:
        y = _rms(y, fg_ref[...])
    o_ref[...] = y


def _ffn(x, gain, wg, wu, wd, final_gain, *, final, tm=1024, tf=256):
    t, d = x.shape
    f = wg.shape[1]
    resident = lambda shape: pl.BlockSpec(shape, lambda i: (0, 0), pipeline_mode=pl.Buffered(1))
    return pl.pallas_call(
        functools.partial(_ffn_body, tf=tf, final=final),
        out_shape=jax.ShapeDtypeStruct((t, d), F32),
        grid=(t // tm,),
        in_specs=[
            pl.BlockSpec((tm, d), lambda i: (i, 0)),
            pl.BlockSpec((1, d), lambda i: (0, 0)),
            resident((d, f)),
            resident((d, f)),
            resident((f, d)),
            pl.BlockSpec((1, d), lambda i: (0, 0)),
        ],
        out_specs=pl.BlockSpec((tm, d), lambda i: (i, 0)),
        compiler_params=_params(("parallel",)),
        name="ffn_final" if final else "ffn",
    )(x, gain, wg, wu, wd, final_gain)


def _write_kv(h, w_ref, wt_ref, o_ref, ot_ref, *, seq):
    dh, tk, tm = NSA_HEAD_DIM, NSA_TQ, h.shape[0]
    y = _dot(h, w_ref[...])
    row = lax.broadcasted_iota(jnp.int32, (tm, LANES), 0)
    lane = lax.broadcasted_iota(jnp.int32, (tm, LANES), 1)
    pos = (pl.program_id(0) * tm) % seq + row
    block_mark = jnp.where(lane - dh == (pos >> (SEL_BLOCK.bit_length() - 1)), MASKED, 0.0)
    for tile in range(KV_COLS // LANES):
        part = y[:, tile * LANES:(tile + 1) * LANES]
        if tile // NSA_KV_GROUPS == ROW_K_SEL:
            part = part + block_mark
        o_ref[:, tile * LANES:(tile + 1) * LANES] = part.astype(BF16)

    y_t = _dot_nt(wt_ref[...], h)
    row_t = lax.broadcasted_iota(jnp.int32, y_t.shape, 0)
    y_t = (y_t + jnp.where((row_t & (LANES - 1)) == dh, 1.0, 0.0)).astype(BF16)
    for c in range(tm // tk):
        ot_ref[0, c] = y_t[:, c * tk:(c + 1) * tk]


def _in_proj_body(x_ref, xp_ref, g_ref, wr_ref, wo_ref, wkv_ref, wkvt_ref, mu_ref, wl_ref, w0_ref,
                  a0_ref, kk_ref, ka_ref, proj_o, kv_o, kvt_o, r_o, k_o, v_o, kk_o, b_o, lw_o, g_o,
                  *, tiles_per_seq, seq):
    i = pl.program_id(0)
    gain = g_ref[...]
    h = _rms(x_ref[...], gain).astype(BF16)
    proj_o[...] = _dot(h, wo_ref[...])
    _write_kv(h, wkv_ref, wkvt_ref, kv_o, kvt_o, seq=seq)
    n_up = xp_ref.shape[0]
    h_up = _rms(xp_ref[...], gain).astype(BF16)
    p_all = _dot(jnp.concatenate([h_up, h], axis=0), wr_ref[...])
    p = p_all[n_up:]
    keep = jnp.where(i % tiles_per_seq == 0, 0.0, 1.0)
    prev_last = p_all[n_up - 1:n_up, :] * keep
    rows = lax.broadcasted_iota(jnp.int32, p.shape, 0)
    shifted = jnp.where(rows == 0, prev_last, pltpu.roll(p, 1, 0))
    x = p + (shifted - p) * mu_ref[...]

    r = x[:, 0:RWKV_DIM]
    k = x[:, RWKV_DIM:2 * RWKV_DIM]
    v = x[:, 2 * RWKV_DIM:3 * RWKV_DIM]
    s = x[:, 3 * RWKV_DIM:RWKV_PROJ]
    lane = lax.broadcasted_iota(jnp.int32, s.shape, 1)
    z = jnp.where(lane < DECAY_LORA, jnp.tanh(s),
                  jnp.where(lane < DECAY_LORA + AAA_LORA, s, jax.nn.sigmoid(s)))
    lo = _dot(z.astype(BF16), wl_ref[...])
    a = jax.nn.sigmoid(a0_ref[...] + lo[:, RWKV_DIM:2 * RWKV_DIM])

    kkr = k * kk_ref[...]
    sq = kkr * kkr
    sq_hi = sq.astype(BF16)
    sq_lo = (sq - sq_hi.astype(F32)).astype(BF16)
    shift = RWKV_HEAD_DIM.bit_length() - 1
    same_head = ((lax.broadcasted_iota(jnp.int32, (RWKV_DIM, RWKV_DIM), 0) >> shift)
                 == (lax.broadcasted_iota(jnp.int32, (RWKV_DIM, RWKV_DIM), 1) >> shift)).astype(BF16)
    ssq = _dot(sq_hi, same_head) + _dot(sq_lo, same_head)
    kk = kkr / jnp.maximum(jnp.sqrt(ssq), 1e-12)

    r_o[...] = r
    k_o[...] = k * (1.0 + (a - 1.0) * ka_ref[...])
    v_o[...] = v
    kk_o[...] = kk
    b_o[...] = kk * a
    lw_o[...] = -math.exp(-0.5) * jax.nn.sigmoid(w0_ref[...] + lo[:, 0:RWKV_DIM])
    g_o[...] = lo[:, 2 * RWKV_DIM:3 * RWKV_DIM]


def _in_proj_call(x, seq, gain, w_rwkv, w_other, w_kv, w_kv_t, mu, w_lora, w0, a0, k_k, k_a, *,
                  tm=512):
    t, d = x.shape
    tk = NSA_TQ
    per_seq = seq // tm
    row = lambda i: (i, 0)
    const = lambda i: (0, 0)
    resident = lambda shape: pl.BlockSpec(shape, const, pipeline_mode=pl.Buffered(1))
    vec = pl.BlockSpec((1, RWKV_DIM), const)
    tok = jax.ShapeDtypeStruct((t, RWKV_DIM), F32)
    return pl.pallas_call(
        functools.partial(_in_proj_body, tiles_per_seq=per_seq, seq=seq),
        out_shape=[jax.ShapeDtypeStruct((t, PROJ_COLS), F32),
                   jax.ShapeDtypeStruct((t, KV_COLS), BF16),
                   jax.ShapeDtypeStruct((t // seq, seq // tk, KV_T_ROWS, tk), BF16)] + [tok] * 7,
        grid=(t // tm,),
        in_specs=[
            pl.BlockSpec((tm, d), row),
            pl.BlockSpec((16, d), lambda i: (jnp.maximum(i * (tm // 16) - 1, 0), 0)),
            pl.BlockSpec((1, d), const),
            resident((d, RWKV_PROJ)),
            resident((d, PROJ_COLS)),
            resident((d, KV_COLS)),
            resident((KV_T_ROWS, d)),
            pl.BlockSpec((1, RWKV_PROJ), const),
            resident((LORA_DIM, 3 * RWKV_DIM)),
            vec, vec, vec, vec,
        ],
        out_specs=[pl.BlockSpec((tm, PROJ_COLS), row),
                   pl.BlockSpec((tm, KV_COLS), row),
                   pl.BlockSpec((1, tm // tk, KV_T_ROWS, tk),
                                lambda i: (i // per_seq, i % per_seq, 0, 0))]
        + [pl.BlockSpec((tm, RWKV_DIM), row)] * 7,
        compiler_params=_params(("parallel",)),
        name="in_proj",
    )(x, x, gain, w_rwkv, w_other, w_kv, w_kv_t, mu, w_lora, w0, a0, k_k, k_a)


def _rwkv_scan_body(r_ref, k_ref, v_ref, kk_ref, b_ref, lw_ref, g_ref, rk_ref, gg_ref, gb_ref,
                    o_ref, st_ref):
    c_sz, n, nh = RWKV_CHUNK, RWKV_HEAD_DIM, RWKV_HEADS

    @pl.when(pl.program_id(1) == 0)
    def _():
        st_ref[...] = jnp.zeros_like(st_ref)

    ri = lax.broadcasted_iota(jnp.int32, (c_sz, c_sz), 0)
    ci = lax.broadcasted_iota(jnp.int32, (c_sz, c_sz), 1)
    incl = ci <= ri
    eye_b = (ci == ri).astype(BF16)
    row2 = lax.broadcasted_iota(jnp.int32, (c_sz, 2 * c_sz), 0)
    lane2 = lax.broadcasted_iota(jnp.int32, (c_sz, 2 * c_sz), 1)
    right_half = lane2 >= c_sz
    zeros_b = jnp.zeros((c_sz, n), BF16)
    inv_shift = RWKV_INV_BLOCK.bit_length() - 1

    rows = []
    for bb in range(st_ref.shape[0]):
        lw = lw_ref[bb]
        cum = _dot(incl.astype(F32), lw, HI)
        cum_last = cum[c_sz - 1:c_sz, :]
        r, k, v, b = r_ref[bb], k_ref[bb], v_ref[bb], b_ref[bb]
        p_inv = jnp.exp(-cum)
        p_end = jnp.exp(cum_last - cum)
        rows.append(dict(
            left=jnp.concatenate([(-(kk_ref[bb] * jnp.exp(cum - lw))).astype(BF16),
                                  (r * jnp.exp(cum)).astype(BF16)], axis=0),
            bt=(b * p_inv).astype(BF16), kt=(k * p_inv).astype(BF16),
            bh=(b * p_end).astype(BF16), kh=(k * p_end).astype(BF16),
            v=v, v_b=v.astype(BF16), d_p=jnp.exp(cum_last), rk=r * k * rk_ref[...]))

    units = [(bb, h) for bb in range(len(rows)) for h in range(nh)]
    col = lambda name, u: rows[u[0]][name][:, u[1] * n:(u[1] + 1) * n]
    a_all = [_dot_nt(col('left', u), jnp.concatenate([col('bt', u), col('kt', u)], axis=0))
             for u in units]
    key2 = jnp.where(right_half, lane2 - c_sz, lane2)
    w_u = [jnp.where(right_half & (key2 < row2), a[:c_sz], 0.0).astype(BF16) for a in a_all]
    w_y = [jnp.where(key2 <= row2, a[c_sz:], 0.0).astype(BF16) for a in a_all]

    same_block = (row2 >> inv_shift) == (key2 >> inv_shift)
    x = [jnp.where((lane2 < row2) & same_block, a[:c_sz], jnp.where(lane2 == row2 + c_sz, 1.0, 0.0))
         for a in a_all]
    for _ in range(inv_shift):
        hi = [xu.astype(BF16) for xu in x]
        lo = [(xu - h_.astype(F32)).astype(BF16) for xu, h_ in zip(x, hi)]
        x = [_dot(h_[:, :c_sz], h_) + _dot(h_[:, :c_sz], l_) + _dot(l_[:, :c_sz], h_)
             + jnp.where(right_half, xu, 0.0) for xu, h_, l_ in zip(x, hi, lo)]
    x_b = [xu.astype(BF16) for xu in x]
    solve_diag = lambda j, z: _dot(x_b[j], jnp.concatenate([zeros_b, z.astype(BF16)], axis=0))
    q_b = [solve_diag(j, jnp.where((ci < ri) & ((ri >> inv_shift) != (ci >> inv_shift)),
                                   a[:c_sz, :c_sz], 0.0)).astype(BF16)
           for j, a in enumerate(a_all)]

    s0 = [st_ref[bb, h] for bb, h in units]
    ls0 = [_dot_nt(col('left', u), s0[j].astype(BF16)) for j, u in enumerate(units)]
    rhs = [ls0[j][:c_sz] + _dot(w_u[j], jnp.concatenate([zeros_b, col('v_b', u)], axis=0))
           for j, u in enumerate(units)]
    g0 = [solve_diag(j, rhs[j]) for j in range(len(units))]
    u_f = g0
    for _ in range(c_sz // RWKV_INV_BLOCK - 1):
        u_f = [g0[j] + _dot(q_b[j], u_f[j].astype(BF16)) for j in range(len(units))]
    u_b = [uj.astype(BF16) for uj in u_f]
    uv = [jnp.concatenate([u_b[j], col('v_b', u)], axis=0) for j, u in enumerate(units)]
    y = [ls0[j][c_sz:] + _dot(w_y[j], uv[j]) for j in range(len(units))]
    uv_t = [_dot_nt(eye_b, uv_j).astype(BF16) for uv_j in uv]
    for j, u in enumerate(units):
        st_ref[u[0], u[1]] = (s0[j] * col('d_p', u)
                              + _dot(uv_t[j], jnp.concatenate([col('bh', u), col('kh', u)], axis=0)))

    for bb in range(len(rows)):
        outs = []
        for h in range(nh):
            sl = slice(h * n, (h + 1) * n)
            yh = y[bb * nh + h]
            mean = jnp.mean(yh, axis=-1, keepdims=True)
            var = jnp.mean(jnp.square(yh - mean), axis=-1, keepdims=True)
            yn = (yh - mean) * lax.rsqrt(var + RWKV_GN_EPS)
            yn = yn * gg_ref[:, sl] + gb_ref[:, sl]
            bonus = jnp.sum(rows[bb]['rk'][:, sl], axis=-1, keepdims=True) * rows[bb]['v'][:, sl]
            outs.append((yn + bonus) * g_ref[bb, :, sl])
        o_ref[bb] = jnp.concatenate(outs, axis=1)


def _rwkv_scan(r, k, v, kk, b, lw, g, r_k, gn_gain, gn_bias, *, batch, seq, nb=4):
    t = r.shape[0]
    nc = seq // RWKV_CHUNK
    tok = pl.BlockSpec((nb, RWKV_CHUNK, RWKV_DIM), lambda bi, c: (bi, c, 0))
    par = pl.BlockSpec((1, RWKV_DIM), lambda bi, c: (0, 0))
    per_batch = lambda a: a.reshape(batch, seq, RWKV_DIM)
    out = pl.pallas_call(
        _rwkv_scan_body,
        out_shape=jax.ShapeDtypeStruct((batch, seq, RWKV_DIM), F32),
        grid=(batch // nb, nc),
        in_specs=[tok] * 7 + [par] * 3,
        out_specs=tok,
        scratch_shapes=[pltpu.VMEM((nb, RWKV_HEADS, RWKV_HEAD_DIM, RWKV_HEAD_DIM), F32)],
        compiler_params=_params(("parallel", "arbitrary")),
        name="rwkv_scan",
    )(*(per_batch(a) for a in (r, k, v, kk, b, lw, g)), r_k, gn_gain, gn_bias)
    return out.reshape(t, RWKV_DIM)


def _compress_body(x_ref, w1s_ref, w1_ref, pe_ref, w2_ref, w2t_ref, o_ref, ot_ref):
    n_rows = x_ref.shape[1]
    for kind in range(2):
        pe = jnp.broadcast_to(pe_ref[kind], (8, pe_ref.shape[2])).astype(BF16)
        pe_term = _dot(pe, w1_ref[kind])[0:1, :]
        for gi in range(NSA_KV_GROUPS):
            tile = kind * NSA_KV_GROUPS + gi
            both = jnp.zeros((n_rows, 2 * CMP_HIDDEN), F32)
            for l in range(CMP_STRIDE):
                both = both + _dot(x_ref[0, :, l, tile * LANES:(tile + 1) * LANES], w1s_ref[kind, l])
            second_next = pltpu.roll(both[:, CMP_HIDDEN:], n_rows - 1, 0)
            hid = both[:, :CMP_HIDDEN] + second_next + pe_term
            act = jax.nn.gelu(hid).astype(BF16)
            o_ref[kind, gi] = _dot(act, w2_ref[kind])
            ot_ref[kind, gi] = _dot_nt(w2t_ref[kind], act)


def _compress(kv, w1, pe, w2, *, bsz, seq):
    g, dh = NSA_KV_GROUPS, NSA_HEAD_DIM
    rows = seq // CMP_STRIDE
    w1r = jnp.pad(w1.reshape(2, CMP_LEN, dh, CMP_HIDDEN), ((0, 0), (0, 0), (0, LANES - dh), (0, 0)))
    w1s = jnp.concatenate([w1r[:, :CMP_STRIDE], w1r[:, CMP_STRIDE:]], axis=3)
    whole = lambda a: pl.BlockSpec(a.shape, lambda b: (0,) * a.ndim)
    w2t = w2.transpose(0, 2, 1)
    return pl.pallas_call(
        _compress_body,
        out_shape=[jax.ShapeDtypeStruct((2, bsz * g, rows, dh), F32),
                   jax.ShapeDtypeStruct((2, bsz * g, dh, rows), F32)],
        grid=(bsz,),
        in_specs=[pl.BlockSpec((1, rows, CMP_STRIDE, 2 * g * LANES), lambda b: (b, 0, 0, 0)),
                  whole(w1s), whole(w1), whole(pe), whole(w2), whole(w2t)],
        out_specs=[pl.BlockSpec((2, g, rows, dh), lambda b: (0, b, 0, 0)),
                   pl.BlockSpec((2, g, dh, rows), lambda b: (0, b, 0, 0))],
        compiler_params=_params(("parallel",)),
        name="nsa_compress",
    )(kv.reshape(bsz, rows, CMP_STRIDE, KV_COLS), w1s, w1, pe, w2, w2t)


def _t5_bucket(dist):
    n = jnp.maximum(dist, 0)
    exact = REL_BUCKETS // 2
    nf = jnp.maximum(n, 1).astype(F32)
    scaled = jnp.log(nf / exact) / math.log(REL_MAX_DIST / exact) * (REL_BUCKETS - exact)
    large = exact + jnp.floor(scaled).astype(jnp.int32)
    large = jnp.minimum(large, REL_BUCKETS - 1)
    return jnp.where(n < exact, n, large)


def _bias_body(tab_ref, bc_ref, bt_ref, *, seq, n_cmp_pad):
    h = pl.program_id(0)
    tq = NSA_TQ

    def lookup(dist):
        bucket = _t5_bucket(dist)
        out = jnp.zeros(dist.shape, F32)
        for bkt in range(REL_BUCKETS):
            out = jnp.where(bucket == bkt, tab_ref[bkt, h] * LOG2E, out)
        return out

    far = tab_ref[REL_BUCKETS - 1, h] * LOG2E
    assert tq + 1 >= REL_MAX_DIST and WINDOW - tq + 1 >= REL_MAX_DIST

    key = lax.broadcasted_iota(jnp.int32, (tq, tq), 0)
    qry = lax.broadcasted_iota(jnp.int32, (tq, tq), 1)
    bt_ref[0, BT_DIAG] = jnp.where(qry >= key, lookup(qry - key), MASKED)
    bt_ref[0, BT_PREV] = lookup(tq + qry - key)
    bt_ref[0, BT_FAR] = jnp.full((tq, tq), far, F32)
    bt_ref[0, BT_WIN_EDGE] = jnp.where(qry < key, far, MASKED)
    bt_ref[0, BT_NONE] = jnp.full((tq, tq), MASKED, F32)

    per_tile = tq // CMP_STRIDE
    pad = 16
    assert pad * CMP_STRIDE >= REL_MAX_DIST + CMP_LEN - 1 and pad % 8 == 0
    band = per_tile + pad
    cmp_end = lax.broadcasted_iota(jnp.int32, (band, tq), 0) * CMP_STRIDE + CMP_LEN - 1
    qry_c = lax.broadcasted_iota(jnp.int32, (band, tq), 1)

    def cmp_tile(i, carry):
        start = pl.multiple_of(jnp.maximum(i * per_tile - pad, 0), 8)
        bc_ref[0, i] = jnp.full((n_cmp_pad, tq), far, F32)
        bc_ref[0, i, pl.ds(start, band), :] = lookup(i * tq + qry_c - (start * CMP_STRIDE + cmp_end))
        return carry

    lax.fori_loop(0, seq // tq, cmp_tile, 0)


def _bias_tables(rel_bias, seq):
    g, hpg, tq = NSA_KV_GROUPS, NSA_HPG, NSA_TQ
    n_cmp_pad = seq // CMP_STRIDE
    nq = seq // tq
    return pl.pallas_call(
        functools.partial(_bias_body, seq=seq, n_cmp_pad=n_cmp_pad),
        out_shape=[jax.ShapeDtypeStruct((g, nq, n_cmp_pad, hpg * tq), F32),
                   jax.ShapeDtypeStruct((g, BT_COUNT, tq, hpg * tq), F32)],
        grid=(NSA_HEADS,),
        in_specs=[pl.BlockSpec(memory_space=pltpu.SMEM)],
        out_specs=[pl.BlockSpec((1, nq, n_cmp_pad, tq), lambda h: (h // hpg, 0, 0, h % hpg)),
                   pl.BlockSpec((1, BT_COUNT, tq, tq), lambda h: (h // hpg, 0, 0, h % hpg))],
        compiler_params=_params(("parallel",)),
        name="nsa_bias",
    )(rel_bias)


def _eye(n, dtype):
    return (lax.broadcasted_iota(jnp.int32, (n, n), 0)
            == lax.broadcasted_iota(jnp.int32, (n, n), 1)).astype(dtype)


def _nsa_body(q_ref, kc_ref, vct_ref, ks_ref, vst_ref, kw_ref, vwt_ref, bc_ref, bt_ref, gate_ref,
              o_ref, m_ref, acc_ref):
    tq, hpg, dh, n_grp = NSA_TQ, NSA_HPG, NSA_HEAD_DIM, NSA_KV_GROUPS
    rws = hpg * tq
    n_blk_log2 = SEL_BLOCK.bit_length() - 1
    n_blk = ks_ref.shape[1] // SEL_BLOCK
    n_cmp_pad = kc_ref.shape[2]
    step = pl.program_id(1)
    n_win = WINDOW // tq
    chains = [(gi, qi) for qi in range(NSA_QT) for gi in range(n_grp)]
    lanes_of = lambda gi: slice(gi * LANES, (gi + 1) * LANES)

    def per_head(x):
        return [x[:, hh * tq:(hh + 1) * tq] for hh in range(hpg)]

    def before_loop(gi, qi):
        i = NSA_QT * step + qi
        rows = slice(qi * tq, (qi + 1) * tq)
        xq = (q_ref[0, rows, gi * hpg * dh:(gi + 1) * hpg * dh] * (dh ** -0.5 * LOG2E)).astype(BF16)
        eye_d = _eye(dh, BF16)
        q_t = jnp.concatenate([_dot_nt(eye_d, xq[:, hh * dh:(hh + 1) * dh]) for hh in range(hpg)],
                              axis=1).astype(BF16)

        cmp_id = lax.broadcasted_iota(jnp.int32, (n_cmp_pad, rws), 0)
        t_pos = i * tq + (lax.broadcasted_iota(jnp.int32, (n_cmp_pad, rws), 1) & (tq - 1))
        valid = (t_pos - (cmp_id * CMP_STRIDE + CMP_LEN - 1) >= 0) & (cmp_id < n_cmp_pad - 1)
        s = jnp.where(valid, _dot(kc_ref[0, gi].astype(BF16), q_t) + bc_ref[gi, qi], MASKED)
        e = jnp.where(valid, jnp.exp2(s - jnp.max(s, axis=0, keepdims=True)), 0.0)
        den = jnp.sum(e, axis=0, keepdims=True)
        p_c = e / jnp.where(den > 0.0, den, 1.0)
        o_cmp = _dot(vct_ref[0, gi].astype(BF16), p_c.astype(BF16))

        p_heads = per_head(p_c)
        p_sum = p_heads[0]
        for ph in p_heads[1:]:
            p_sum = p_sum + ph
        blk_o = lax.broadcasted_iota(jnp.int32, (n_blk, n_cmp_pad), 0)
        cmp_o = lax.broadcasted_iota(jnp.int32, (n_blk, n_cmp_pad), 1)
        overlap_t = ((cmp_o * CMP_STRIDE <= blk_o * SEL_BLOCK + SEL_BLOCK - 1)
                     & (cmp_o * CMP_STRIDE + CMP_LEN - 1 >= blk_o * SEL_BLOCK)).astype(F32)
        imp = _dot(overlap_t, p_sum, HI)
        jj = lax.broadcasted_iota(jnp.int32, (n_blk, tq), 0)
        cur = (i * tq + lax.broadcasted_iota(jnp.int32, (n_blk, tq), 1)) >> n_blk_log2
        forced = (jj == 0) | (jj == cur) | (jj == cur - 1)
        imp = jnp.where(jj > cur, -1e6, jnp.where(forced, 1e6, imp))
        rank = jnp.zeros((n_blk, tq), jnp.int32)
        for a in range(n_blk):
            row = imp[a:a + 1, :]
            beats = (row > imp) | ((row == imp) & (a < jj))
            rank = rank + beats.astype(jnp.int32)
        not_sel = jnp.where(rank < SEL_TOP_N, 0.0, 1.0).astype(BF16)

        q_aug = jnp.concatenate([q_t, jnp.concatenate([not_sel] * hpg, axis=1),
                                 jnp.zeros((LANES - dh - n_blk, rws), BF16)], axis=0)
        q_pad = jnp.concatenate([q_t, jnp.zeros((LANES - dh, rws), BF16)], axis=0)

        tiles, scores = [], []
        for delta in range(n_win + 1):
            entry = {0: BT_DIAG, 1: BT_PREV, n_win: BT_WIN_EDGE}.get(delta, BT_FAR)
            if delta > 0:
                entry = jnp.where(i - delta >= 0, entry, BT_NONE)
            tiles.append(jnp.maximum(i - delta, 0))
            off = pl.multiple_of(tiles[-1] * tq, tq)
            scores.append(_dot(kw_ref[0, pl.ds(off, tq), lanes_of(gi)], q_pad) + bt_ref[gi, entry])
        m_all = scores[0]
        for sc in scores[1:]:
            m_all = jnp.maximum(m_all, sc)
        m_w = jnp.max(m_all, axis=0, keepdims=True)
        acc_w = jnp.zeros((LANES, rws), F32)
        for kt, sc in zip(tiles, scores):
            acc_w = acc_w + _dot(vwt_ref[0, kt, lanes_of(gi), :], jnp.exp2(sc - m_w).astype(BF16))
        o_win = acc_w[0:dh] / acc_w[dh:dh + 1]

        g_t = _dot_nt(_eye(GATE_PAD, F32), gate_ref[0, rows, gi * GATE_PAD:(gi + 1) * GATE_PAD], HI)
        sig = jax.nn.sigmoid(g_t[0:4 * hpg])
        gate = [jnp.concatenate([sig[br * hpg + hh:br * hpg + hh + 1] for hh in range(hpg)], axis=1)
                for br in range(3)]
        return q_aug, gate[0] * o_cmp + gate[2] * o_win, gate[1]

    prepared = {c: before_loop(*c) for c in chains}

    m_ref[...] = jnp.full(m_ref.shape, MASKED, F32)
    acc_ref[...] = jnp.zeros(acc_ref.shape, F32)

    def sel_update(tiles, which):
        for gi, qi in which:
            i = NSA_QT * step + qi
            q_aug = prepared[(gi, qi)][0]
            scores = []
            for kt in tiles:
                off = pl.multiple_of(kt * tq, tq)
                scores.append(_dot(ks_ref[0, pl.ds(off, tq), lanes_of(gi)], q_aug)
                              + bt_ref[gi, jnp.minimum(i - kt, BT_FAR)])
            s_max = scores[0]
            for sc in scores[1:]:
                s_max = jnp.maximum(s_max, sc)
            m_prev = m_ref[qi, gi]
            m_new = jnp.maximum(m_prev, jnp.max(s_max, axis=0, keepdims=True))
            acc = jnp.exp2(m_prev - m_new) * acc_ref[qi, gi]
            for kt, sc in zip(tiles, scores):
                acc = acc + _dot(vst_ref[0, kt, lanes_of(gi), :], jnp.exp2(sc - m_new).astype(BF16))
            m_ref[qi, gi] = m_new
            acc_ref[qi, gi] = acc

    def sel_pair(j, carry):
        sel_update([2 * j, 2 * j + 1], chains)
        return carry

    assert NSA_QT == 2
    lax.fori_loop(0, step, sel_pair, 0)
    last = NSA_QT * step
    sel_update([last], [c for c in chains if c[1] == 0])
    sel_update([last, last + 1], [c for c in chains if c[1] == 1])

    eye_q = _eye(tq, BF16)
    for qi in range(NSA_QT):
        outs = []
        for gi in range(n_grp):
            _, cmp_win, gate_sel = prepared[(gi, qi)]
            o_sel = acc_ref[qi, gi, 0:dh, :] / acc_ref[qi, gi, dh:dh + 1, :]
            y_t = (cmp_win + gate_sel * o_sel).astype(BF16)
            outs += [_dot_nt(eye_q, yh) for yh in per_head(y_t)]
        o_ref[0, qi * tq:(qi + 1) * tq, :] = jnp.concatenate(outs, axis=1).astype(BF16)


def _nsa_attention(proj, kv, kv_t, kvc, kvc_t, bias_c, bias_t, *, bsz, seq):
    g, tq, hpg, dh = NSA_KV_GROUPS, NSA_TQ, NSA_HPG, NSA_HEAD_DIM
    n_rows = kvc.shape[2]
    nq = seq // tq
    qt = NSA_QT
    kv_spec = lambda pos: pl.BlockSpec((1, seq, g * LANES), lambda b, i: (b, 0, pos))
    kvt_spec = lambda pos: pl.BlockSpec((1, nq, g * LANES, tq), lambda b, i: (b, 0, pos, 0))
    kv3 = kv.reshape(bsz, seq, KV_COLS)
    proj3 = proj.reshape(bsz, seq, PROJ_COLS)
    out = pl.pallas_call(
        _nsa_body,
        out_shape=jax.ShapeDtypeStruct((bsz, seq, NSA_DIM), BF16),
        grid=(bsz, nq // qt),
        in_specs=[
            pl.BlockSpec((1, qt * tq, NSA_DIM), lambda b, i: (b, i, COL_QNSA // NSA_DIM)),
            pl.BlockSpec((1, g, n_rows, dh), lambda b, i: (0, b, 0, 0)),
            pl.BlockSpec((1, g, dh, n_rows), lambda b, i: (1, b, 0, 0)),
            kv_spec(ROW_K_SEL), kvt_spec(T_V_SEL), kv_spec(ROW_K_WIN), kvt_spec(T_V_WIN),
            pl.BlockSpec((g, qt, n_rows, hpg * tq), lambda b, i: (0, i, 0, 0)),
            pl.BlockSpec((g, BT_COUNT, tq, hpg * tq), lambda b, i: (0, 0, 0, 0),
                         pipeline_mode=pl.Buffered(1)),
            pl.BlockSpec((1, qt * tq, g * GATE_PAD), lambda b, i: (b, i, COL_GNSA // (g * GATE_PAD))),
        ],
        out_specs=pl.BlockSpec((1, qt * tq, NSA_DIM), lambda b, i: (b, i, 0)),
        scratch_shapes=[pltpu.VMEM((qt, g, 1, hpg * tq), F32),
                        pltpu.VMEM((qt, g, LANES, hpg * tq), F32)],
        compiler_params=_params(("parallel", "arbitrary")),
        name="nsa_attention",
    )(proj3, kvc, kvc_t, kv3, kv_t, kv3, kv_t, bias_c, bias_t, proj3)
    return out.reshape(bsz * seq, NSA_DIM)


def _mem_body(q_ref, mem_ref, g_ref, w_ref, o_ref, kv_ref):
    @pl.when(pl.program_id(1) == 0)
    def _():
        kv_ref[...] = _dot(_rms(mem_ref[0], g_ref[...]).astype(BF16), w_ref[...]).astype(BF16)

    outs = []
    for h in range(MEM_HEADS):
        sl = slice(h * MEM_HEAD_DIM, (h + 1) * MEM_HEAD_DIM)
        qh = (q_ref[:, sl] * (MEM_HEAD_DIM ** -0.5)).astype(BF16)
        s = _dot_nt(qh, kv_ref[:, sl])
        e = jnp.exp(s - jnp.max(s, axis=-1, keepdims=True))
        p = e / jnp.sum(e, axis=-1, keepdims=True)
        outs.append(_dot(p.astype(BF16), kv_ref[:, MEM_DIM + h * MEM_HEAD_DIM:
                                                MEM_DIM + (h + 1) * MEM_HEAD_DIM]))
    o_ref[...] = jnp.concatenate(outs, axis=1)


def _mem_attention(proj, mem, gain, w_kv, *, seq, tq=1024):
    t = proj.shape[0]
    bsz, m, d = mem.shape
    per_seq = seq // tq
    return pl.pallas_call(
        _mem_body,
        out_shape=jax.ShapeDtypeStruct((t, MEM_DIM), F32),
        grid=(bsz, per_seq),
        in_specs=[
            pl.BlockSpec((tq, MEM_DIM), lambda b, j: (b * per_seq + j, COL_QMEM // MEM_DIM)),
            pl.BlockSpec((1, m, d), lambda b, j: (b, 0, 0)),
            pl.BlockSpec((1, d), lambda b, j: (0, 0)),
            pl.BlockSpec((d, 2 * MEM_DIM), lambda b, j: (0, 0), pipeline_mode=pl.Buffered(1)),
        ],
        out_specs=pl.BlockSpec((tq, MEM_DIM), lambda b, j: (b * per_seq + j, 0)),
        scratch_shapes=[pltpu.VMEM((m, 2 * MEM_DIM), BF16)],
        compiler_params=_params(("parallel", "arbitrary")),
        name="mem_attention",
    )(proj, mem, gain, w_kv)


def _merge_body(x_ref, gain_ref, yr_ref, yn_ref, ym_ref, wg_ref, wr_ref, wn_ref, wm_ref, wo_ref,
                o_ref):
    d = x_ref.shape[1]
    x = x_ref[...]
    h = _rms(x, gain_ref[...]).astype(BF16)
    merged = jnp.zeros(x.shape, F32)
    for b, (y_ref, w_ref) in enumerate(((yr_ref, wr_ref), (yn_ref, wn_ref), (ym_ref, wm_ref))):
        gate = jax.nn.sigmoid(_dot(h, wg_ref[:, b * d:(b + 1) * d]))
        merged = merged + gate * _dot(y_ref[...].astype(BF16), w_ref[...])
    o_ref[...] = x + _dot(merged.astype(BF16), wo_ref[...])


def _merge(x, gain, y_rwkv, y_nsa, y_mem, w_g, w_r, w_n, w_m, w_o, *, tm=1024):
    t, d = x.shape
    row = lambda i: (i, 0)
    const = lambda i: (0, 0)
    resident = lambda shape: pl.BlockSpec(shape, const, pipeline_mode=pl.Buffered(1))
    return pl.pallas_call(
        _merge_body,
        out_shape=jax.ShapeDtypeStruct((t, d), F32),
        grid=(t // tm,),
        in_specs=[
            pl.BlockSpec((tm, d), row),
            pl.BlockSpec((1, d), const),
            pl.BlockSpec((tm, RWKV_DIM), row),
            pl.BlockSpec((tm, NSA_DIM), row),
            pl.BlockSpec((tm, MEM_DIM), row),
            resident((d, N_BRANCH * d)),
            resident((RWKV_DIM, d)),
            resident((NSA_DIM, d)),
            resident((MEM_DIM, d)),
            resident((d, d)),
        ],
        out_specs=pl.BlockSpec((tm, d), row),
        compiler_params=_params(("parallel",)),
        name="merge",
    )(x, gain, y_rwkv, y_nsa, y_mem, w_g, w_r, w_n, w_m, w_o)


def _row(a):
    return a.reshape(1, -1)


def _in_proj(x, p, l, seq):
    d = x.shape[1]
    g, hpg, dh = NSA_KV_GROUPS, NSA_HPG, NSA_HEAD_DIM
    w_in = p['w_in'][l]
    o = 0
    parts = {}
    for name, size in (('rwkv', RWKV_PROJ), ('q', NSA_DIM), ('kv', KV_KINDS * NSA_KV_DIM),
                       ('g_nsa', 3 * NSA_HEADS), ('q_mem', MEM_DIM)):
        parts[name] = w_in[:, o:o + size]
        o += size
    gates = parts['g_nsa'].reshape(d, 3, g, hpg).transpose(0, 2, 1, 3).reshape(d, g, 3 * hpg)
    gates = jnp.pad(gates, ((0, 0), (0, 0), (0, GATE_PAD - 3 * hpg))).reshape(d, g * GATE_PAD)
    w_other = jnp.concatenate([parts['q'], parts['q_mem'], gates], axis=1).astype(BF16)
    w_kv = jnp.pad(parts['kv'].reshape(d, KV_KINDS, g, dh), ((0, 0), (0, 0), (0, 0), (0, LANES - dh)))
    w_row = w_kv[:, KV_ROW_KINDS, :, :].reshape(d, KV_COLS).astype(BF16)
    w_t = w_kv[:, KV_T_KINDS, :, :].reshape(d, KV_T_ROWS).T.astype(BF16)
    w_lora = jnp.zeros((LORA_DIM, 3 * RWKV_DIM), F32)
    w_lora = w_lora.at[0:DECAY_LORA, 0:RWKV_DIM].set(p['rwkv_w2'][l])
    w_lora = w_lora.at[DECAY_LORA:DECAY_LORA + AAA_LORA, RWKV_DIM:2 * RWKV_DIM].set(p['rwkv_a2'][l])
    w_lora = w_lora.at[DECAY_LORA + AAA_LORA:, 2 * RWKV_DIM:].set(p['rwkv_g2'][l])
    proj, kv, kv_t, *rwkv_in = _in_proj_call(
        x, seq, _row(p['mix_norm'][l]), parts['rwkv'].astype(BF16), w_other, w_row, w_t,
        _row(p['rwkv_mu'][l]), w_lora.astype(BF16), _row(p['rwkv_w0'][l]), _row(p['rwkv_a0'][l]),
        _row(p['rwkv_k_k'][l]), _row(p['rwkv_k_a'][l]))
    return proj, kv, kv_t, rwkv_in


def _rwkv_branch(rwkv_in, p, l, bsz, seq):
    return _rwkv_scan(*rwkv_in, _row(p['rwkv_r_k'][l]), _row(p['rwkv_gn_gain'][l]),
                      _row(p['rwkv_gn_bias'][l]), batch=bsz, seq=seq)


def _nsa_branch(proj, kv, kv_t, bias_c, bias_t, p, l, bsz, seq):
    w1 = jnp.stack([p['cmp_k_w1'][l], p['cmp_v_w1'][l]]).astype(BF16)
    pe = jnp.stack([p['cmp_pe_k'][l].reshape(1, -1), p['cmp_pe_v'][l].reshape(1, -1)])
    w2 = jnp.stack([p['cmp_k_w2'][l], p['cmp_v_w2'][l]]).astype(BF16)
    kvc, kvc_t = _compress(kv, w1, pe, w2, bsz=bsz, seq=seq)
    return _nsa_attention(proj, kv, kv_t, kvc, kvc_t, bias_c, bias_t, bsz=bsz, seq=seq)


def _mem_branch(proj, mem, p, l):
    w_kv = jnp.concatenate([p['mem_w_k'][l], p['mem_w_v'][l]], axis=1).astype(BF16)
    return _mem_attention(proj, mem, _row(p['mem_norm'][l]), w_kv, seq=proj.shape[0] // mem.shape[0])


def _layer(x, mem, l, bias_c, bias_t, p):
    bsz, seq, d = x.shape
    t = bsz * seq
    row = _row
    x = x.reshape(t, d)

    x = _ffn(x, row(p['ffn1_norm'][l]), p['ffn1_w_gate'][l].astype(BF16),
             p['ffn1_w_up'][l].astype(BF16), p['ffn1_w_down'][l].astype(BF16),
             row(p['final_norm']), final=False)

    proj, kv, kv_t, rwkv_in = _in_proj(x, p, l, seq)
    y_rwkv = _rwkv_branch(rwkv_in, p, l, bsz, seq)
    y_nsa = _nsa_branch(proj, kv, kv_t, bias_c, bias_t, p, l, bsz, seq)
    y_mem = _mem_branch(proj, mem, p, l)

    w_gate = p['w_in'][l][:, -N_BRANCH * d:].astype(BF16)
    x = _merge(x, row(p['mix_norm'][l]), y_rwkv, y_nsa, y_mem, w_gate,
               p['w_br_rwkv'][l].astype(BF16), p['w_br_nsa'][l].astype(BF16),
               p['w_br_mem'][l].astype(BF16), p['w_out'][l].astype(BF16))

    last = l == p['ffn1_norm'].shape[0] - 1
    x = _ffn(x, row(p['ffn2_norm'][l]), p['ffn2_w_gate'][l].astype(BF16),
             p['ffn2_w_up'][l].astype(BF16), p['ffn2_w_down'][l].astype(BF16),
             row(p['final_norm']), final=last)
    return x.reshape(bsz, seq, d)


def kernel(x, mem, ffn1_norm, ffn1_w_gate, ffn1_w_up, ffn1_w_down, mix_norm, w_in, rwkv_mu, rwkv_w0, rwkv_w2, rwkv_a0, rwkv_a2, rwkv_g2, rwkv_k_k, rwkv_k_a, rwkv_r_k, rwkv_gn_gain, rwkv_gn_bias, cmp_pe_k, cmp_k_w1, cmp_k_w2, cmp_pe_v, cmp_v_w1, cmp_v_w2, rel_bias, mem_norm, mem_w_k, mem_w_v, w_br_rwkv, w_br_nsa, w_br_mem, w_out, ffn2_norm, ffn2_w_gate, ffn2_w_up, ffn2_w_down, final_norm):
    p = dict(ffn1_norm=ffn1_norm, ffn1_w_gate=ffn1_w_gate, ffn1_w_up=ffn1_w_up,
             ffn1_w_down=ffn1_w_down, mix_norm=mix_norm, w_in=w_in, rwkv_mu=rwkv_mu,
             rwkv_w0=rwkv_w0, rwkv_w2=rwkv_w2, rwkv_a0=rwkv_a0, rwkv_a2=rwkv_a2, rwkv_g2=rwkv_g2,
             rwkv_k_k=rwkv_k_k, rwkv_k_a=rwkv_k_a, rwkv_r_k=rwkv_r_k, rwkv_gn_gain=rwkv_gn_gain,
             rwkv_gn_bias=rwkv_gn_bias, cmp_pe_k=cmp_pe_k, cmp_k_w1=cmp_k_w1, cmp_k_w2=cmp_k_w2,
             cmp_pe_v=cmp_pe_v, cmp_v_w1=cmp_v_w1, cmp_v_w2=cmp_v_w2, mem_norm=mem_norm,
             mem_w_k=mem_w_k, mem_w_v=mem_w_v, w_br_rwkv=w_br_rwkv, w_br_nsa=w_br_nsa,
             w_br_mem=w_br_mem, w_out=w_out, ffn2_norm=ffn2_norm, ffn2_w_gate=ffn2_w_gate,
             ffn2_w_up=ffn2_w_up, ffn2_w_down=ffn2_w_down, final_norm=final_norm)
    bias_c, bias_t = _bias_tables(rel_bias, x.shape[1])
    for l in range(ffn1_norm.shape[0]):
        x = _layer(x, mem, l, bias_c, bias_t, p)
    return x
```

```python
import functools
import math

import jax
import jax.numpy as jnp
from jax import lax
from jax.experimental import pallas as pl
from jax.experimental.pallas import tpu as pltpu

F32 = jnp.float32
BF16 = jnp.bfloat16
HI = lax.Precision.HIGHEST

NORM_EPS = 1e-6
RWKV_HEADS = 8
RWKV_HEAD_DIM = 64
RWKV_DIM = RWKV_HEADS * RWKV_HEAD_DIM
DECAY_LORA = 64
AAA_LORA = 64
GATE_LORA = 128
LORA_DIM = DECAY_LORA + AAA_LORA + GATE_LORA
RWKV_GN_EPS = 64e-5
RWKV_PROJ = 3 * RWKV_DIM + LORA_DIM
NSA_HEADS = 8
NSA_KV_GROUPS = 2
NSA_HPG = NSA_HEADS // NSA_KV_GROUPS
NSA_HEAD_DIM = 64
NSA_DIM = NSA_HEADS * NSA_HEAD_DIM
NSA_KV_DIM = NSA_KV_GROUPS * NSA_HEAD_DIM
CMP_LEN = 32
CMP_STRIDE = 16
CMP_HIDDEN = 256
SEL_BLOCK = 64
SEL_TOP_N = 16
WINDOW = 512
REL_BUCKETS = 32
REL_MAX_DIST = 128
MEM_HEADS = 4
MEM_HEAD_DIM = 128
MEM_DIM = MEM_HEADS * MEM_HEAD_DIM
N_BRANCH = 3

LANES = 128
GATE_PAD = LANES
COL_QNSA = 0
COL_QMEM = COL_QNSA + NSA_DIM
COL_GNSA = COL_QMEM + MEM_DIM
PROJ_COLS = COL_GNSA + NSA_KV_GROUPS * GATE_PAD
assert COL_QMEM % MEM_DIM == 0 and COL_GNSA % (NSA_KV_GROUPS * GATE_PAD) == 0 and COL_QNSA % NSA_DIM == 0
KV_KINDS = 6
KV_ROW_KINDS = (0, 1, 2, 4)
KV_T_KINDS = (3, 5)
KV_COLS = len(KV_ROW_KINDS) * NSA_KV_GROUPS * LANES
KV_T_ROWS = len(KV_T_KINDS) * NSA_KV_GROUPS * LANES
ROW_K_SEL, ROW_K_WIN = KV_ROW_KINDS.index(2), KV_ROW_KINDS.index(4)
T_V_SEL, T_V_WIN = KV_T_KINDS.index(3), KV_T_KINDS.index(5)
LOG2E = 1.4426950408889634

RWKV_CHUNK = 64
RWKV_INV_BLOCK = 16
NSA_TQ = 256
NSA_QT = 2
MASKED = -1e30
BT_DIAG, BT_PREV, BT_FAR, BT_WIN_EDGE, BT_NONE, BT_COUNT = 0, 1, 2, 3, 4, 5
V7X_VMEM_BYTES = 64 * 1024 * 1024
VMEM_LIMIT = V7X_VMEM_BYTES - 4 * 1024 * 1024


def _dot(a, b, precision=None):
    return jnp.dot(a, b, preferred_element_type=F32, precision=precision)


def _dot_nt(a, b, precision=None):
    return lax.dot_general(a, b, (((1,), (1,)), ((), ())), preferred_element_type=F32,
                           precision=precision)


def _params(semantics):
    return pltpu.CompilerParams(dimension_semantics=semantics, vmem_limit_bytes=VMEM_LIMIT)


def _rms(x, g):
    return x * lax.rsqrt(jnp.mean(x * x, axis=-1, keepdims=True) + NORM_EPS) * g


def _ffn_body(x_ref, g_ref, wg_ref, wu_ref, wd_ref, fg_ref, o_ref, *, tf, final):
    x = x_ref[...]
    h = _rms(x, g_ref[...]).astype(BF16)
    acc = jnp.zeros(x.shape, F32)
    for j in range(wg_ref.shape[1] // tf):
        cols = slice(j * tf, (j + 1) * tf)
        act = (jax.nn.silu(_dot(h, wg_ref[:, cols].astype(BF16)))
               * _dot(h, wu_ref[:, cols].astype(BF16))).astype(BF16)
        acc = acc + _dot(act, wd_ref[cols, :].astype(BF16))
    y = x + 0.5 * acc
    if final:
        y = _rms(y, fg_ref[...])
    o_ref[...] = y


def _ffn(x, gain, wg, wu, wd, final_gain, *, final, tm=1024, tf=256):
    t, d = x.shape
    f = wg.shape[1]
    resident = lambda shape: pl.BlockSpec(shape, lambda i: (0, 0), pipeline_mode=pl.Buffered(1))
    return pl.pallas_call(
        functools.partial(_ffn_body, tf=tf, final=final),
        out_shape=jax.ShapeDtypeStruct((t, d), F32),
        grid=(t // tm,),
        in_specs=[
            pl.BlockSpec((tm, d), lambda i: (i, 0)),
            pl.BlockSpec((1, d), lambda i: (0, 0)),
            resident((d, f)),
            resident((d, f)),
            resident((f, d)),
            pl.BlockSpec((1, d), lambda i: (0, 0)),
        ],
        out_specs=pl.BlockSpec((tm, d), lambda i: (i, 0)),
        compiler_params=_params(("parallel",)),
        name="ffn_final" if final else "ffn",
    )(x, gain, wg, wu, wd, final_gain)


def _write_kv(h, w_ref, wt_ref, o_ref, ot_ref, *, seq):
    dh, tk, tm = NSA_HEAD_DIM, NSA_TQ, h.shape[0]
    y = _dot(h, w_ref[...])
    row = lax.broadcasted_iota(jnp.int32, (tm, LANES), 0)
    lane = lax.broadcasted_iota(jnp.int32, (tm, LANES), 1)
    pos = (pl.program_id(0) * tm) % seq + row
    block_mark = jnp.where(lane - dh == (pos >> (SEL_BLOCK.bit_length() - 1)), MASKED, 0.0)
    for tile in range(KV_COLS // LANES):
        part = y[:, tile * LANES:(tile + 1) * LANES]
        if tile // NSA_KV_GROUPS == ROW_K_SEL:
            part = part + block_mark
        o_ref[:, tile * LANES:(tile + 1) * LANES] = part.astype(BF16)

    y_t = _dot_nt(wt_ref[...], h)
    row_t = lax.broadcasted_iota(jnp.int32, y_t.shape, 0)
    y_t = (y_t + jnp.where((row_t & (LANES - 1)) == dh, 1.0, 0.0)).astype(BF16)
    for c in range(tm // tk):
        ot_ref[0, c] = y_t[:, c * tk:(c + 1) * tk]


def _in_proj_body(x_ref, xp_ref, g_ref, wr_ref, wo_ref, wkv_ref, wkvt_ref, mu_ref, wl_ref, w0_ref,
                  a0_ref, kk_ref, ka_ref, proj_o, kv_o, kvt_o, r_o, k_o, v_o, kk_o, b_o, lw_o, g_o,
                  *, tiles_per_seq, seq):
    i = pl.program_id(0)
    gain = g_ref[...]
    h = _rms(x_ref[...], gain).astype(BF16)
    proj_o[...] = _dot(h, wo_ref[...])
    _write_kv(h, wkv_ref, wkvt_ref, kv_o, kvt_o, seq=seq)
    n_up = xp_ref.shape[0]
    h_up = _rms(xp_ref[...], gain).astype(BF16)
    p_all = _dot(jnp.concatenate([h_up, h], axis=0), wr_ref[...])
    p = p_all[n_up:]
    keep = jnp.where(i % tiles_per_seq == 0, 0.0, 1.0)
    prev_last = p_all[n_up - 1:n_up, :] * keep
    rows = lax.broadcasted_iota(jnp.int32, p.shape, 0)
    shifted = jnp.where(rows == 0, prev_last, pltpu.roll(p, 1, 0))
    x = p + (shifted - p) * mu_ref[...]

    r = x[:, 0:RWKV_DIM]
    k = x[:, RWKV_DIM:2 * RWKV_DIM]
    v = x[:, 2 * RWKV_DIM:3 * RWKV_DIM]
    s = x[:, 3 * RWKV_DIM:RWKV_PROJ]
    lane = lax.broadcasted_iota(jnp.int32, s.shape, 1)
    z = jnp.where(lane < DECAY_LORA, jnp.tanh(s),
                  jnp.where(lane < DECAY_LORA + AAA_LORA, s, jax.nn.sigmoid(s)))
    lo = _dot(z.astype(BF16), wl_ref[...])
    a = jax.nn.sigmoid(a0_ref[...] + lo[:, RWKV_DIM:2 * RWKV_DIM])

    kkr = k * kk_ref[...]
    sq = kkr * kkr
    sq_hi = sq.astype(BF16)
    sq_lo = (sq - sq_hi.astype(F32)).astype(BF16)
    shift = RWKV_HEAD_DIM.bit_length() - 1
    same_head = ((lax.broadcasted_iota(jnp.int32, (RWKV_DIM, RWKV_DIM), 0) >> shift)
                 == (lax.broadcasted_iota(jnp.int32, (RWKV_DIM, RWKV_DIM), 1) >> shift)).astype(BF16)
    ssq = _dot(sq_hi, same_head) + _dot(sq_lo, same_head)
    kk = kkr / jnp.maximum(jnp.sqrt(ssq), 1e-12)

    r_o[...] = r
    k_o[...] = k * (1.0 + (a - 1.0) * ka_ref[...])
    v_o[...] = v
    kk_o[...] = kk
    b_o[...] = kk * a
    lw_o[...] = -math.exp(-0.5) * jax.nn.sigmoid(w0_ref[...] + lo[:, 0:RWKV_DIM])
    g_o[...] = lo[:, 2 * RWKV_DIM:3 * RWKV_DIM]


def _in_proj_call(x, seq, gain, w_rwkv, w_other, w_kv, w_kv_t, mu, w_lora, w0, a0, k_k, k_a, *,
                  tm=512):
    t, d = x.shape
    tk = NSA_TQ
    per_seq = seq // tm
    row = lambda i: (i, 0)
    const = lambda i: (0, 0)
    resident = lambda shape: pl.BlockSpec(shape, const, pipeline_mode=pl.Buffered(1))
    vec = pl.BlockSpec((1, RWKV_DIM), const)
    tok = jax.ShapeDtypeStruct((t, RWKV_DIM), F32)
    return pl.pallas_call(
        functools.partial(_in_proj_body, tiles_per_seq=per_seq, seq=seq),
        out_shape=[jax.ShapeDtypeStruct((t, PROJ_COLS), F32),
                   jax.ShapeDtypeStruct((t, KV_COLS), BF16),
                   jax.ShapeDtypeStruct((t // seq, seq // tk, KV_T_ROWS, tk), BF16)] + [tok] * 7,
        grid=(t // tm,),
        in_specs=[
            pl.BlockSpec((tm, d), row),
            pl.BlockSpec((16, d), lambda i: (jnp.maximum(i * (tm // 16) - 1, 0), 0)),
            pl.BlockSpec((1, d), const),
            resident((d, RWKV_PROJ)),
            resident((d, PROJ_COLS)),
            resident((d, KV_COLS)),
            resident((KV_T_ROWS, d)),
            pl.BlockSpec((1, RWKV_PROJ), const),
            resident((LORA_DIM, 3 * RWKV_DIM)),
            vec, vec, vec, vec,
        ],
        out_specs=[pl.BlockSpec((tm, PROJ_COLS), row),
                   pl.BlockSpec((tm, KV_COLS), row),
                   pl.BlockSpec((1, tm // tk, KV_T_ROWS, tk),
                                lambda i: (i // per_seq, i % per_seq, 0, 0))]
        + [pl.BlockSpec((tm, RWKV_DIM), row)] * 7,
        compiler_params=_params(("parallel",)),
        name="in_proj",
    )(x, x, gain, w_rwkv, w_other, w_kv, w_kv_t, mu, w_lora, w0, a0, k_k, k_a)


def _rwkv_scan_body(r_ref, k_ref, v_ref, kk_ref, b_ref, lw_ref, g_ref, rk_ref, gg_ref, gb_ref,
                    o_ref, st_ref):
    c_sz, n, nh = RWKV_CHUNK, RWKV_HEAD_DIM, RWKV_HEADS

    @pl.when(pl.program_id(1) == 0)
    def _():
        st_ref[...] = jnp.zeros_like(st_ref)

    ri = lax.broadcasted_iota(jnp.int32, (c_sz, c_sz), 0)
    ci = lax.broadcasted_iota(jnp.int32, (c_sz, c_sz), 1)
    incl = ci <= ri
    eye_b = (ci == ri).astype(BF16)
    row2 = lax.broadcasted_iota(jnp.int32, (c_sz, 2 * c_sz), 0)
    lane2 = lax.broadcasted_iota(jnp.int32, (c_sz, 2 * c_sz), 1)
    right_half = lane2 >= c_sz
    zeros_b = jnp.zeros((c_sz, n), BF16)
    inv_shift = RWKV_INV_BLOCK.bit_length() - 1

    rows = []
    for bb in range(st_ref.shape[0]):
        lw = lw_ref[bb]
        cum = _dot(incl.astype(F32), lw, HI)
        cum_last = cum[c_sz - 1:c_sz, :]
        r, k, v, b = r_ref[bb], k_ref[bb], v_ref[bb], b_ref[bb]
        p_inv = jnp.exp(-cum)
        p_end = jnp.exp(cum_last - cum)
        rows.append(dict(
            left=jnp.concatenate([(-(kk_ref[bb] * jnp.exp(cum - lw))).astype(BF16),
                                  (r * jnp.exp(cum)).astype(BF16)], axis=0),
            bt=(b * p_inv).astype(BF16), kt=(k * p_inv).astype(BF16),
            bh=(b * p_end).astype(BF16), kh=(k * p_end).astype(BF16),
            v=v, v_b=v.astype(BF16), d_p=jnp.exp(cum_last), rk=r * k * rk_ref[...]))

    units = [(bb, h) for bb in range(len(rows)) for h in range(nh)]
    col = lambda name, u: rows[u[0]][name][:, u[1] * n:(u[1] + 1) * n]
    a_all = [_dot_nt(col('left', u), jnp.concatenate([col('bt', u), col('kt', u)], axis=0))
             for u in units]
    key2 = jnp.where(right_half, lane2 - c_sz, lane2)
    w_u = [jnp.where(right_half & (key2 < row2), a[:c_sz], 0.0).astype(BF16) for a in a_all]
    w_y = [jnp.where(key2 <= row2, a[c_sz:], 0.0).astype(BF16) for a in a_all]

    same_block = (row2 >> inv_shift) == (key2 >> inv_shift)
    x = [jnp.where((lane2 < row2) & same_block, a[:c_sz], jnp.where(lane2 == row2 + c_sz, 1.0, 0.0))
         for a in a_all]
    for _ in range(inv_shift):
        hi = [xu.astype(BF16) for xu in x]
        lo = [(xu - h_.astype(F32)).astype(BF16) for xu, h_ in zip(x, hi)]
        x = [_dot(h_[:, :c_sz], h_) + _dot(h_[:, :c_sz], l_) + _dot(l_[:, :c_sz], h_)
             + jnp.where(right_half, xu, 0.0) for xu, h_, l_ in zip(x, hi, lo)]
    x_b = [xu.astype(BF16) for xu in x]
    solve_diag = lambda j, z: _dot(x_b[j], jnp.concatenate([zeros_b, z.astype(BF16)], axis=0))
    q_b = [solve_diag(j, jnp.where((ci < ri) & ((ri >> inv_shift) != (ci >> inv_shift)),
                                   a[:c_sz, :c_sz], 0.0)).astype(BF16)
           for j, a in enumerate(a_all)]

    s0 = [st_ref[bb, h] for bb, h in units]
    ls0 = [_dot_nt(col('left', u), s0[j].astype(BF16)) for j, u in enumerate(units)]
    rhs = [ls0[j][:c_sz] + _dot(w_u[j], jnp.concatenate([zeros_b, col('v_b', u)], axis=0))
           for j, u in enumerate(units)]
    g0 = [solve_diag(j, rhs[j]) for j in range(len(units))]
    u_f = g0
    for _ in range(c_sz // RWKV_INV_BLOCK - 1):
        u_f = [g0[j] + _dot(q_b[j], u_f[j].astype(BF16)) for j in range(len(units))]
    u_b = [uj.astype(BF16) for uj in u_f]
    uv = [jnp.concatenate([u_b[j], col('v_b', u)], axis=0) for j, u in enumerate(units)]
    y = [ls0[j][c_sz:] + _dot(w_y[j], uv[j]) for j in range(len(units))]
    uv_t = [_dot_nt(eye_b, uv_j).astype(BF16) for uv_j in uv]
    for j, u in enumerate(units):
        st_ref[u[0], u[1]] = (s0[j] * col('d_p', u)
                              + _dot(uv_t[j], jnp.concatenate([col('bh', u), col('kh', u)], axis=0)))

    for bb in range(len(rows)):
        outs = []
        for h in range(nh):
            sl = slice(h * n, (h + 1) * n)
            yh = y[bb * nh + h]
            mean = jnp.mean(yh, axis=-1, keepdims=True)
            var = jnp.mean(jnp.square(yh - mean), axis=-1, keepdims=True)
            yn = (yh - mean) * lax.rsqrt(var + RWKV_GN_EPS)
            yn = yn * gg_ref[:, sl] + gb_ref[:, sl]
            bonus = jnp.sum(rows[bb]['rk'][:, sl], axis=-1, keepdims=True) * rows[bb]['v'][:, sl]
            outs.append((yn + bonus) * g_ref[bb, :, sl])
        o_ref[bb] = jnp.concatenate(outs, axis=1)


def _rwkv_scan(r, k, v, kk, b, lw, g, r_k, gn_gain, gn_bias, *, batch, seq, nb=4):
    t = r.shape[0]
    nc = seq // RWKV_CHUNK
    tok = pl.BlockSpec((nb, RWKV_CHUNK, RWKV_DIM), lambda bi, c: (bi, c, 0))
    par = pl.BlockSpec((1, RWKV_DIM), lambda bi, c: (0, 0))
    per_batch = lambda a: a.reshape(batch, seq, RWKV_DIM)
    out = pl.pallas_call(
        _rwkv_scan_body,
        out_shape=jax.ShapeDtypeStruct((batch, seq, RWKV_DIM), F32),
        grid=(batch // nb, nc),
        in_specs=[tok] * 7 + [par] * 3,
        out_specs=tok,
        scratch_shapes=[pltpu.VMEM((nb, RWKV_HEADS, RWKV_HEAD_DIM, RWKV_HEAD_DIM), F32)],
        compiler_params=_params(("parallel", "arbitrary")),
        name="rwkv_scan",
    )(*(per_batch(a) for a in (r, k, v, kk, b, lw, g)), r_k, gn_gain, gn_bias)
    return out.reshape(t, RWKV_DIM)


def _compress_body(x_ref, w1s_ref, w1_ref, pe_ref, w2_ref, w2t_ref, o_ref, ot_ref):
    n_rows = x_ref.shape[1]
    for kind in range(2):
        pe = jnp.broadcast_to(pe_ref[kind], (8, pe_ref.shape[2])).astype(BF16)
        pe_term = _dot(pe, w1_ref[kind])[0:1, :]
        for gi in range(NSA_KV_GROUPS):
            tile = kind * NSA_KV_GROUPS + gi
            both = jnp.zeros((n_rows, 2 * CMP_HIDDEN), F32)
            for l in range(CMP_STRIDE):
                both = both + _dot(x_ref[0, :, l, tile * LANES:(tile + 1) * LANES], w1s_ref[kind, l])
            second_next = pltpu.roll(both[:, CMP_HIDDEN:], n_rows - 1, 0)
            hid = both[:, :CMP_HIDDEN] + second_next + pe_term
            act = jax.nn.gelu(hid).astype(BF16)
            o_ref[kind, gi] = _dot(act, w2_ref[kind])
            ot_ref[kind, gi] = _dot_nt(w2t_ref[kind], act)


def _compress(kv, w1, pe, w2, *, bsz, seq):
    g, dh = NSA_KV_GROUPS, NSA_HEAD_DIM
    rows = seq // CMP_STRIDE
    w1r = jnp.pad(w1.reshape(2, CMP_LEN, dh, CMP_HIDDEN), ((0, 0), (0, 0), (0, LANES - dh), (0, 0)))
    w1s = jnp.concatenate([w1r[:, :CMP_STRIDE], w1r[:, CMP_STRIDE:]], axis=3)
    whole = lambda a: pl.BlockSpec(a.shape, lambda b: (0,) * a.ndim)
    w2t = w2.transpose(0, 2, 1)
    return pl.pallas_call(
        _compress_body,
        out_shape=[jax.ShapeDtypeStruct((2, bsz * g, rows, dh), F32),
                   jax.ShapeDtypeStruct((2, bsz * g, dh, rows), F32)],
        grid=(bsz,),
        in_specs=[pl.BlockSpec((1, rows, CMP_STRIDE, 2 * g * LANES), lambda b: (b, 0, 0, 0)),
                  whole(w1s), whole(w1), whole(pe), whole(w2), whole(w2t)],
        out_specs=[pl.BlockSpec((2, g, rows, dh), lambda b: (0, b, 0, 0)),
                   pl.BlockSpec((2, g, dh, rows), lambda b: (0, b, 0, 0))],
        compiler_params=_params(("parallel",)),
        name="nsa_compress",
    )(kv.reshape(bsz, rows, CMP_STRIDE, KV_COLS), w1s, w1, pe, w2, w2t)


def _t5_bucket(dist):
    n = jnp.maximum(dist, 0)
    exact = REL_BUCKETS // 2
    nf = jnp.maximum(n, 1).astype(F32)
    scaled = jnp.log(nf / exact) / math.log(REL_MAX_DIST / exact) * (REL_BUCKETS - exact)
    large = exact + jnp.floor(scaled).astype(jnp.int32)
    large = jnp.minimum(large, REL_BUCKETS - 1)
    return jnp.where(n < exact, n, large)


def _bias_body(tab_ref, bc_ref, bt_ref, *, seq, n_cmp_pad):
    h = pl.program_id(0)
    tq = NSA_TQ

    def lookup(dist):
        bucket = _t5_bucket(dist)
        out = jnp.zeros(dist.shape, F32)
        for bkt in range(REL_BUCKETS):
            out = jnp.where(bucket == bkt, tab_ref[bkt, h] * LOG2E, out)
        return out

    far = tab_ref[REL_BUCKETS - 1, h] * LOG2E
    assert tq + 1 >= REL_MAX_DIST and WINDOW - tq + 1 >= REL_MAX_DIST

    key = lax.broadcasted_iota(jnp.int32, (tq, tq), 0)
    qry = lax.broadcasted_iota(jnp.int32, (tq, tq), 1)
    bt_ref[0, BT_DIAG] = jnp.where(qry >= key, lookup(qry - key), MASKED)
    bt_ref[0, BT_PREV] = lookup(tq + qry - key)
    bt_ref[0, BT_FAR] = jnp.full((tq, tq), far, F32)
    bt_ref[0, BT_WIN_EDGE] = jnp.where(qry < key, far, MASKED)
    bt_ref[0, BT_NONE] = jnp.full((tq, tq), MASKED, F32)

    per_tile = tq // CMP_STRIDE
    pad = 16
    assert pad * CMP_STRIDE >= REL_MAX_DIST + CMP_LEN - 1 and pad % 8 == 0
    band = per_tile + pad
    cmp_end = lax.broadcasted_iota(jnp.int32, (band, tq), 0) * CMP_STRIDE + CMP_LEN - 1
    qry_c = lax.broadcasted_iota(jnp.int32, (band, tq), 1)

    def cmp_tile(i, carry):
        start = pl.multiple_of(jnp.maximum(i * per_tile - pad, 0), 8)
        bc_ref[0, i] = jnp.full((n_cmp_pad, tq), far, F32)
        bc_ref[0, i, pl.ds(start, band), :] = lookup(i * tq + qry_c - (start * CMP_STRIDE + cmp_end))
        return carry

    lax.fori_loop(0, seq // tq, cmp_tile, 0)


def _bias_tables(rel_bias, seq):
    g, hpg, tq = NSA_KV_GROUPS, NSA_HPG, NSA_TQ
    n_cmp_pad = seq // CMP_STRIDE
    nq = seq // tq
    return pl.pallas_call(
        functools.partial(_bias_body, seq=seq, n_cmp_pad=n_cmp_pad),
        out_shape=[jax.ShapeDtypeStruct((g, nq, n_cmp_pad, hpg * tq), F32),
                   jax.ShapeDtypeStruct((g, BT_COUNT, tq, hpg * tq), F32)],
        grid=(NSA_HEADS,),
        in_specs=[pl.BlockSpec(memory_space=pltpu.SMEM)],
        out_specs=[pl.BlockSpec((1, nq, n_cmp_pad, tq), lambda h: (h // hpg, 0, 0, h % hpg)),
                   pl.BlockSpec((1, BT_COUNT, tq, tq), lambda h: (h // hpg, 0, 0, h % hpg))],
        compiler_params=_params(("parallel",)),
        name="nsa_bias",
    )(rel_bias)


def _eye(n, dtype):
    return (lax.broadcasted_iota(jnp.int32, (n, n), 0)
            == lax.broadcasted_iota(jnp.int32, (n, n), 1)).astype(dtype)


def _nsa_body(q_ref, kc_ref, vct_ref, ks_ref, vst_ref, kw_ref, vwt_ref, bc_ref, bt_ref, gate_ref,
              o_ref, m_ref, acc_ref):
    tq, hpg, dh, n_grp = NSA_TQ, NSA_HPG, NSA_HEAD_DIM, NSA_KV_GROUPS
    rws = hpg * tq
    n_blk_log2 = SEL_BLOCK.bit_length() - 1
    n_blk = ks_ref.shape[1] // SEL_BLOCK
    n_cmp_pad = kc_ref.shape[2]
    step = pl.program_id(1)
    n_win = WINDOW // tq
    chains = [(gi, qi) for qi in range(NSA_QT) for gi in range(n_grp)]
    lanes_of = lambda gi: slice(gi * LANES, (gi + 1) * LANES)

    def per_head(x):
        return [x[:, hh * tq:(hh + 1) * tq] for hh in range(hpg)]

    def before_loop(gi, qi):
        i = NSA_QT * step + qi
        rows = slice(qi * tq, (qi + 1) * tq)
        xq = (q_ref[0, rows, gi * hpg * dh:(gi + 1) * hpg * dh] * (dh ** -0.5 * LOG2E)).astype(BF16)
        eye_d = _eye(dh, BF16)
        q_t = jnp.concatenate([_dot_nt(eye_d, xq[:, hh * dh:(hh + 1) * dh]) for hh in range(hpg)],
                              axis=1).astype(BF16)

        cmp_id = lax.broadcasted_iota(jnp.int32, (n_cmp_pad, rws), 0)
        t_pos = i * tq + (lax.broadcasted_iota(jnp.int32, (n_cmp_pad, rws), 1) & (tq - 1))
        valid = (t_pos - (cmp_id * CMP_STRIDE + CMP_LEN - 1) >= 0) & (cmp_id < n_cmp_pad - 1)
        s = jnp.where(valid, _dot(kc_ref[0, gi].astype(BF16), q_t) + bc_ref[gi, qi], MASKED)
        e = jnp.where(valid, jnp.exp2(s - jnp.max(s, axis=0, keepdims=True)), 0.0)
        den = jnp.sum(e, axis=0, keepdims=True)
        p_c = e / jnp.where(den > 0.0, den, 1.0)
        o_cmp = _dot(vct_ref[0, gi].astype(BF16), p_c.astype(BF16))

        p_heads = per_head(p_c)
        p_sum = p_heads[0]
        for ph in p_heads[1:]:
            p_sum = p_sum + ph
        blk_o = lax.broadcasted_iota(jnp.int32, (n_blk, n_cmp_pad), 0)
        cmp_o = lax.broadcasted_iota(jnp.int32, (n_blk, n_cmp_pad), 1)
        overlap_t = ((cmp_o * CMP_STRIDE <= blk_o * SEL_BLOCK + SEL_BLOCK - 1)
                     & (cmp_o * CMP_STRIDE + CMP_LEN - 1 >= blk_o * SEL_BLOCK)).astype(F32)
        imp = _dot(overlap_t, p_sum, HI)
        jj = lax.broadcasted_iota(jnp.int32, (n_blk, tq), 0)
        cur = (i * tq + lax.broadcasted_iota(jnp.int32, (n_blk, tq), 1)) >> n_blk_log2
        forced = (jj == 0) | (jj == cur) | (jj == cur - 1)
        imp = jnp.where(jj > cur, -1e6, jnp.where(forced, 1e6, imp))
        rank = jnp.zeros((n_blk, tq), jnp.int32)
        for a in range(n_blk):
            row = imp[a:a + 1, :]
            beats = (row > imp) | ((row == imp) & (a < jj))
            rank = rank + beats.astype(jnp.int32)
        not_sel = jnp.where(rank < SEL_TOP_N, 0.0, 1.0).astype(BF16)

        q_aug = jnp.concatenate([q_t, jnp.concatenate([not_sel] * hpg, axis=1),
                                 jnp.zeros((LANES - dh - n_blk, rws), BF16)], axis=0)
        q_pad = jnp.concatenate([q_t, jnp.zeros((LANES - dh, rws), BF16)], axis=0)

        tiles, scores = [], []
        for delta in range(n_win + 1):
            entry = {0: BT_DIAG, 1: BT_PREV, n_win: BT_WIN_EDGE}.get(delta, BT_FAR)
            if delta > 0:
                entry = jnp.where(i - delta >= 0, entry, BT_NONE)
            tiles.append(jnp.maximum(i - delta, 0))
            off = pl.multiple_of(tiles[-1] * tq, tq)
            scores.append(_dot(kw_ref[0, pl.ds(off, tq), lanes_of(gi)], q_pad) + bt_ref[gi, entry])
        m_all = scores[0]
        for sc in scores[1:]:
            m_all = jnp.maximum(m_all, sc)
        m_w = jnp.max(m_all, axis=0, keepdims=True)
        acc_w = jnp.zeros((LANES, rws), F32)
        for kt, sc in zip(tiles, scores):
            acc_w = acc_w + _dot(vwt_ref[0, kt, lanes_of(gi), :], jnp.exp2(sc - m_w).astype(BF16))
        o_win = acc_w[0:dh] / acc_w[dh:dh + 1]

        g_t = _dot_nt(_eye(GATE_PAD, F32), gate_ref[0, rows, gi * GATE_PAD:(gi + 1) * GATE_PAD], HI)
        sig = jax.nn.sigmoid(g_t[0:4 * hpg])
        gate = [jnp.concatenate([sig[br * hpg + hh:br * hpg + hh + 1] for hh in range(hpg)], axis=1)
                for br in range(3)]
        return q_aug, gate[0] * o_cmp + gate[2] * o_win, gate[1]

    prepared = {c: before_loop(*c) for c in chains}

    m_ref[...] = jnp.full(m_ref.shape, MASKED, F32)
    acc_ref[...] = jnp.zeros(acc_ref.shape, F32)

    def sel_update(tiles, which):
        for gi, qi in which:
            i = NSA_QT * step + qi
            q_aug = prepared[(gi, qi)][0]
            scores = []
            for kt in tiles:
                off = pl.multiple_of(kt * tq, tq)
                scores.append(_dot(ks_ref[0, pl.ds(off, tq), lanes_of(gi)], q_aug)
                              + bt_ref[gi, jnp.minimum(i - kt, BT_FAR)])
            s_max = scores[0]
            for sc in scores[1:]:
                s_max = jnp.maximum(s_max, sc)
            m_prev = m_ref[qi, gi]
            m_new = jnp.maximum(m_prev, jnp.max(s_max, axis=0, keepdims=True))
            acc = jnp.exp2(m_prev - m_new) * acc_ref[qi, gi]
            for kt, sc in zip(tiles, scores):
                acc = acc + _dot(vst_ref[0, kt, lanes_of(gi), :], jnp.exp2(sc - m_new).astype(BF16))
            m_ref[qi, gi] = m_new
            acc_ref[qi, gi] = acc

    def sel_pair(j, carry):
        sel_update([2 * j, 2 * j + 1], chains)
        return carry

    assert NSA_QT == 2
    lax.fori_loop(0, step, sel_pair, 0)
    last = NSA_QT * step
    sel_update([last], [c for c in chains if c[1] == 0])
    sel_update([last, last + 1], [c for c in chains if c[1] == 1])

    eye_q = _eye(tq, BF16)
    for qi in range(NSA_QT):
        outs = []
        for gi in range(n_grp):
            _, cmp_win, gate_sel = prepared[(gi, qi)]
            o_sel = acc_ref[qi, gi, 0:dh, :] / acc_ref[qi, gi, dh:dh + 1, :]
            y_t = (cmp_win + gate_sel * o_sel).astype(BF16)
            outs += [_dot_nt(eye_q, yh) for yh in per_head(y_t)]
        o_ref[0, qi * tq:(qi + 1) * tq, :] = jnp.concatenate(outs, axis=1).astype(BF16)


def _nsa_attention(proj, kv, kv_t, kvc, kvc_t, bias_c, bias_t, *, bsz, seq):
    g, tq, hpg, dh = NSA_KV_GROUPS, NSA_TQ, NSA_HPG, NSA_HEAD_DIM
    n_rows = kvc.shape[2]
    nq = seq // tq
    qt = NSA_QT
    kv_spec = lambda pos: pl.BlockSpec((1, seq, g * LANES), lambda b, i: (b, 0, pos))
    kvt_spec = lambda pos: pl.BlockSpec((1, nq, g * LANES, tq), lambda b, i: (b, 0, pos, 0))
    kv3 = kv.reshape(bsz, seq, KV_COLS)
    proj3 = proj.reshape(bsz, seq, PROJ_COLS)
    out = pl.pallas_call(
        _nsa_body,
        out_shape=jax.ShapeDtypeStruct((bsz, seq, NSA_DIM), BF16),
        grid=(bsz, nq // qt),
        in_specs=[
            pl.BlockSpec((1, qt * tq, NSA_DIM), lambda b, i: (b, i, COL_QNSA // NSA_DIM)),
            pl.BlockSpec((1, g, n_rows, dh), lambda b, i: (0, b, 0, 0)),
            pl.BlockSpec((1, g, dh, n_rows), lambda b, i: (1, b, 0, 0)),
            kv_spec(ROW_K_SEL), kvt_spec(T_V_SEL), kv_spec(ROW_K_WIN), kvt_spec(T_V_WIN),
            pl.BlockSpec((g, qt, n_rows, hpg * tq), lambda b, i: (0, i, 0, 0)),
            pl.BlockSpec((g, BT_COUNT, tq, hpg * tq), lambda b, i: (0, 0, 0, 0),
                         pipeline_mode=pl.Buffered(1)),
            pl.BlockSpec((1, qt * tq, g * GATE_PAD), lambda b, i: (b, i, COL_GNSA // (g * GATE_PAD))),
        ],
        out_specs=pl.BlockSpec((1, qt * tq, NSA_DIM), lambda b, i: (b, i, 0)),
        scratch_shapes=[pltpu.VMEM((qt, g, 1, hpg * tq), F32),
                        pltpu.VMEM((qt, g, LANES, hpg * tq), F32)],
        compiler_params=_params(("parallel", "arbitrary")),
        name="nsa_attention",
    )(proj3, kvc, kvc_t, kv3, kv_t, kv3, kv_t, bias_c, bias_t, proj3)
    return out.reshape(bsz * seq, NSA_DIM)


def _mem_body(q_ref, mem_ref, g_ref, w_ref, o_ref, kv_ref):
    @pl.when(pl.program_id(1) == 0)
    def _():
        kv_ref[...] = _dot(_rms(mem_ref[0], g_ref[...]).astype(BF16), w_ref[...]).astype(BF16)

    outs = []
    for h in range(MEM_HEADS):
        sl = slice(h * MEM_HEAD_DIM, (h + 1) * MEM_HEAD_DIM)
        qh = (q_ref[:, sl] * (MEM_HEAD_DIM ** -0.5)).astype(BF16)
        s = _dot_nt(qh, kv_ref[:, sl])
        e = jnp.exp(s - jnp.max(s, axis=-1, keepdims=True))
        p = e / jnp.sum(e, axis=-1, keepdims=True)
        outs.append(_dot(p.astype(BF16), kv_ref[:, MEM_DIM + h * MEM_HEAD_DIM:
                                                MEM_DIM + (h + 1) * MEM_HEAD_DIM]))
    o_ref[...] = jnp.concatenate(outs, axis=1)


def _mem_attention(proj, mem, gain, w_kv, *, seq, tq=1024):
    t = proj.shape[0]
    bsz, m, d = mem.shape
    per_seq = seq // tq
    return pl.pallas_call(
        _mem_body,
        out_shape=jax.ShapeDtypeStruct((t, MEM_DIM), F32),
        grid=(bsz, per_seq),
        in_specs=[
            pl.BlockSpec((tq, MEM_DIM), lambda b, j: (b * per_seq + j, COL_QMEM // MEM_DIM)),
            pl.BlockSpec((1, m, d), lambda b, j: (b, 0, 0)),
            pl.BlockSpec((1, d), lambda b, j: (0, 0)),
            pl.BlockSpec((d, 2 * MEM_DIM), lambda b, j: (0, 0), pipeline_mode=pl.Buffered(1)),
        ],
        out_specs=pl.BlockSpec((tq, MEM_DIM), lambda b, j: (b * per_seq + j, 0)),
        scratch_shapes=[pltpu.VMEM((m, 2 * MEM_DIM), BF16)],
        compiler_params=_params(("parallel", "arbitrary")),
        name="mem_attention",
    )(proj, mem, gain, w_kv)


def _merge_body(x_ref, gain_ref, yr_ref, yn_ref, ym_ref, wg_ref, wr_ref, wn_ref, wm_ref, wo_ref,
                o_ref):
    d = x_ref.shape[1]
    x = x_ref[...]
    h = _rms(x, gain_ref[...]).astype(BF16)
    merged = jnp.zeros(x.shape, F32)
    for b, (y_ref, w_ref) in enumerate(((yr_ref, wr_ref), (yn_ref, wn_ref), (ym_ref, wm_ref))):
        gate = jax.nn.sigmoid(_dot(h, wg_ref[:, b * d:(b + 1) * d]))
        merged = merged + gate * _dot(y_ref[...].astype(BF16), w_ref[...])
    o_ref[...] = x + _dot(merged.astype(BF16), wo_ref[...])


def _merge(x, gain, y_rwkv, y_nsa, y_mem, w_g, w_r, w_n, w_m, w_o, *, tm=1024):
    t, d = x.shape
    row = lambda i: (i, 0)
    const = lambda i: (0, 0)
    resident = lambda shape: pl.BlockSpec(shape, const, pipeline_mode=pl.Buffered(1))
    return pl.pallas_call(
        _merge_body,
        out_shape=jax.ShapeDtypeStruct((t, d), F32),
        grid=(t // tm,),
        in_specs=[
            pl.BlockSpec((tm, d), row),
            pl.BlockSpec((1, d), const),
            pl.BlockSpec((tm, RWKV_DIM), row),
            pl.BlockSpec((tm, NSA_DIM), row),
            pl.BlockSpec((tm, MEM_DIM), row),
            resident((d, N_BRANCH * d)),
            resident((RWKV_DIM, d)),
            resident((NSA_DIM, d)),
            resident((MEM_DIM, d)),
            resident((d, d)),
        ],
        out_specs=pl.BlockSpec((tm, d), row),
        compiler_params=_params(("parallel",)),
        name="merge",
    )(x, gain, y_rwkv, y_nsa, y_mem, w_g, w_r, w_n, w_m, w_o)


def _row(a):
    return a.reshape(1, -1)


def _in_proj(x, p, l, seq):
    d = x.shape[1]
    g, hpg, dh = NSA_KV_GROUPS, NSA_HPG, NSA_HEAD_DIM
    w_in = p['w_in'][l]
    o = 0
    parts = {}
    for name, size in (('rwkv', RWKV_PROJ), ('q', NSA_DIM), ('kv', KV_KINDS * NSA_KV_DIM),
                       ('g_nsa', 3 * NSA_HEADS), ('q_mem', MEM_DIM)):
        parts[name] = w_in[:, o:o + size]
        o += size
    gates = parts['g_nsa'].reshape(d, 3, g, hpg).transpose(0, 2, 1, 3).reshape(d, g, 3 * hpg)
    gates = jnp.pad(gates, ((0, 0), (0, 0), (0, GATE_PAD - 3 * hpg))).reshape(d, g * GATE_PAD)
    w_other = jnp.concatenate([parts['q'], parts['q_mem'], gates], axis=1).astype(BF16)
    w_kv = jnp.pad(parts['kv'].reshape(d, KV_KINDS, g, dh), ((0, 0), (0, 0), (0, 0), (0, LANES - dh)))
    w_row = w_kv[:, KV_ROW_KINDS, :, :].reshape(d, KV_COLS).astype(BF16)
    w_t = w_kv[:, KV_T_KINDS, :, :].reshape(d, KV_T_ROWS).T.astype(BF16)
    w_lora = jnp.zeros((LORA_DIM, 3 * RWKV_DIM), F32)
    w_lora = w_lora.at[0:DECAY_LORA, 0:RWKV_DIM].set(p['rwkv_w2'][l])
    w_lora = w_lora.at[DECAY_LORA:DECAY_LORA + AAA_LORA, RWKV_DIM:2 * RWKV_DIM].set(p['rwkv_a2'][l])
    w_lora = w_lora.at[DECAY_LORA + AAA_LORA:, 2 * RWKV_DIM:].set(p['rwkv_g2'][l])
    proj, kv, kv_t, *rwkv_in = _in_proj_call(
        x, seq, _row(p['mix_norm'][l]), parts['rwkv'].astype(BF16), w_other, w_row, w_t,
        _row(p['rwkv_mu'][l]), w_lora.astype(BF16), _row(p['rwkv_w0'][l]), _row(p['rwkv_a0'][l]),
        _row(p['rwkv_k_k'][l]), _row(p['rwkv_k_a'][l]))
    return proj, kv, kv_t, rwkv_in


def _rwkv_branch(rwkv_in, p, l, bsz, seq):
    return _rwkv_scan(*rwkv_in, _row(p['rwkv_r_k'][l]), _row(p['rwkv_gn_gain'][l]),
                      _row(p['rwkv_gn_bias'][l]), batch=bsz, seq=seq)


def _nsa_branch(proj, kv, kv_t, bias_c, bias_t, p, l, bsz, seq):
    w1 = jnp.stack([p['cmp_k_w1'][l], p['cmp_v_w1'][l]]).astype(BF16)
    pe = jnp.stack([p['cmp_pe_k'][l].reshape(1, -1), p['cmp_pe_v'][l].reshape(1, -1)])
    w2 = jnp.stack([p['cmp_k_w2'][l], p['cmp_v_w2'][l]]).astype(BF16)
    kvc, kvc_t = _compress(kv, w1, pe, w2, bsz=bsz, seq=seq)
    return _nsa_attention(proj, kv, kv_t, kvc, kvc_t, bias_c, bias_t, bsz=bsz, seq=seq)


def _mem_branch(proj, mem, p, l):
    w_kv = jnp.concatenate([p['mem_w_k'][l], p['mem_w_v'][l]], axis=1).astype(BF16)
    return _mem_attention(proj, mem, _row(p['mem_norm'][l]), w_kv, seq=proj.shape[0] // mem.shape[0])


def _layer(x, mem, l, bias_c, bias_t, p):
    bsz, seq, d = x.shape
    t = bsz * seq
    row = _row
    x = x.reshape(t, d)

    x = _ffn(x, row(p['ffn1_norm'][l]), p['ffn1_w_gate'][l], p['ffn1_w_up'][l], p['ffn1_w_down'][l],
             row(p['final_norm']), final=False)

    proj, kv, kv_t, rwkv_in = _in_proj(x, p, l, seq)
    y_rwkv = _rwkv_branch(rwkv_in, p, l, bsz, seq)
    y_nsa = _nsa_branch(proj, kv, kv_t, bias_c, bias_t, p, l, bsz, seq)
    y_mem = _mem_branch(proj, mem, p, l)

    w_gate = p['w_in'][l][:, -N_BRANCH * d:].astype(BF16)
    x = _merge(x, row(p['mix_norm'][l]), y_rwkv, y_nsa, y_mem, w_gate,
               p['w_br_rwkv'][l].astype(BF16), p['w_br_nsa'][l].astype(BF16),
               p['w_br_mem'][l].astype(BF16), p['w_out'][l].astype(BF16))

    last = l == p['ffn1_norm'].shape[0] - 1
    x = _ffn(x, row(p['ffn2_norm'][l]), p['ffn2_w_gate'][l], p['ffn2_w_up'][l], p['ffn2_w_down'][l],
             row(p['final_norm']), final=last)
    return x.reshape(bsz, seq, d)


def kernel(x, mem, ffn1_norm, ffn1_w_gate, ffn1_w_up, ffn1_w_down, mix_norm, w_in, rwkv_mu, rwkv_w0, rwkv_w2, rwkv_a0, rwkv_a2, rwkv_g2, rwkv_k_k, rwkv_k_a, rwkv_r_k, rwkv_gn_gain, rwkv_gn_bias, cmp_pe_k, cmp_k_w1, cmp_k_w2, cmp_pe_v, cmp_v_w1, cmp_v_w2, rel_bias, mem_norm, mem_w_k, mem_w_v, w_br_rwkv, w_br_nsa, w_br_mem, w_out, ffn2_norm, ffn2_w_gate, ffn2_w_up, ffn2_w_down, final_norm):
    p = dict(ffn1_norm=ffn1_norm, ffn1_w_gate=ffn1_w_gate, ffn1_w_up=ffn1_w_up,
             ffn1_w_down=ffn1_w_down, mix_norm=mix_norm, w_in=w_in, rwkv_mu=rwkv_mu,
             rwkv_w0=rwkv_w0, rwkv_w2=rwkv_w2, rwkv_a0=rwkv_a0, rwkv_a2=rwkv_a2, rwkv_g2=rwkv_g2,
             rwkv_k_k=rwkv_k_k, rwkv_k_a=rwkv_k_a, rwkv_r_k=rwkv_r_k, rwkv_gn_gain=rwkv_gn_gain,
             rwkv_gn_bias=rwkv_gn_bias, cmp_pe_k=cmp_pe_k, cmp_k_w1=cmp_k_w1, cmp_k_w2=cmp_k_w2,
             cmp_pe_v=cmp_pe_v, cmp_v_w1=cmp_v_w1, cmp_v_w2=cmp_v_w2, mem_norm=mem_norm,
             mem_w_k=mem_w_k, mem_w_v=mem_w_v, w_br_rwkv=w_br_rwkv, w_br_nsa=w_br_nsa,
             w_br_mem=w_br_mem, w_out=w_out, ffn2_norm=ffn2_norm, ffn2_w_gate=ffn2_w_gate,
             ffn2_w_up=ffn2_w_up, ffn2_w_down=ffn2_w_down, final_norm=final_norm)
    bias_c, bias_t = _bias_tables(rel_bias, x.shape[1])
    for l in range(ffn1_norm.shape[0]):
        x = _layer(x, mem, l, bias_c, bias_t, p)
    return x
```

```python
import functools
import math

import jax
import jax.numpy as jnp
from jax import lax
from jax.experimental import pallas as pl
from jax.experimental.pallas import tpu as pltpu

F32 = jnp.float32
BF16 = jnp.bfloat16
HI = lax.Precision.HIGHEST

NORM_EPS = 1e-6
RWKV_HEADS = 8
RWKV_HEAD_DIM = 64
RWKV_DIM = RWKV_HEADS * RWKV_HEAD_DIM
DECAY_LORA = 64
AAA_LORA = 64
GATE_LORA = 128
LORA_DIM = DECAY_LORA + AAA_LORA + GATE_LORA
RWKV_GN_EPS = 64e-5
RWKV_PROJ = 3 * RWKV_DIM + LORA_DIM
NSA_HEADS = 8
NSA_KV_GROUPS = 2
NSA_HPG = NSA_HEADS // NSA_KV_GROUPS
NSA_HEAD_DIM = 64
NSA_DIM = NSA_HEADS * NSA_HEAD_DIM
NSA_KV_DIM = NSA_KV_GROUPS * NSA_HEAD_DIM
CMP_LEN = 32
CMP_STRIDE = 16
CMP_HIDDEN = 256
SEL_BLOCK = 64
SEL_TOP_N = 16
WINDOW = 512
REL_BUCKETS = 32
REL_MAX_DIST = 128
MEM_HEADS = 4
MEM_HEAD_DIM = 128
MEM_DIM = MEM_HEADS * MEM_HEAD_DIM
N_BRANCH = 3

LANES = 128
GATE_PAD = LANES
COL_QNSA = 0
COL_QMEM = COL_QNSA + NSA_DIM
COL_GNSA = COL_QMEM + MEM_DIM
PROJ_COLS = COL_GNSA + NSA_KV_GROUPS * GATE_PAD
assert COL_QMEM % MEM_DIM == 0 and COL_GNSA % (NSA_KV_GROUPS * GATE_PAD) == 0 and COL_QNSA % NSA_DIM == 0
KV_KINDS = 6
KV_ROW_KINDS = (0, 1, 2, 4)
KV_T_KINDS = (3, 5)
KV_COLS = len(KV_ROW_KINDS) * NSA_KV_GROUPS * LANES
KV_T_ROWS = len(KV_T_KINDS) * NSA_KV_GROUPS * LANES
ROW_K_SEL, ROW_K_WIN = KV_ROW_KINDS.index(2), KV_ROW_KINDS.index(4)
T_V_SEL, T_V_WIN = KV_T_KINDS.index(3), KV_T_KINDS.index(5)
W_RWKV = 0
W_OTHER = W_RWKV + RWKV_PROJ
W_KV = W_OTHER + PROJ_COLS
W_WINDOW = W_KV + KV_COLS
assert W_OTHER % LANES == 0 and W_KV % LANES == 0
LOG2E = 1.4426950408889634

RWKV_CHUNK = 64
RWKV_INV_BLOCK = 16
NSA_TQ = 256
NSA_QT = 2
MASKED = -1e30
BT_DIAG, BT_PREV, BT_FAR, BT_WIN_EDGE, BT_NONE, BT_COUNT = 0, 1, 2, 3, 4, 5
V7X_VMEM_BYTES = 64 * 1024 * 1024
VMEM_LIMIT = V7X_VMEM_BYTES - 4 * 1024 * 1024


def _dot(a, b, precision=None):
    return jnp.dot(a, b, preferred_element_type=F32, precision=precision)


def _dot_nt(a, b, precision=None):
    return lax.dot_general(a, b, (((1,), (1,)), ((), ())), preferred_element_type=F32,
                           precision=precision)


def _params(semantics):
    return pltpu.CompilerParams(dimension_semantics=semantics, vmem_limit_bytes=VMEM_LIMIT)


def _rms(x, g):
    return x * lax.rsqrt(jnp.mean(x * x, axis=-1, keepdims=True) + NORM_EPS) * g


def _ffn_body(x_ref, g_ref, wg_ref, wu_ref, wd_ref, fg_ref, o_ref, *, tf, final):
    x = x_ref[...]
    h = _rms(x, g_ref[...]).astype(BF16)
    acc = jnp.zeros(x.shape, F32)
    for j in range(wg_ref.shape[1] // tf):
        cols = slice(j * tf, (j + 1) * tf)
        act = (jax.nn.silu(_dot(h, wg_ref[:, cols].astype(BF16)))
               * _dot(h, wu_ref[:, cols].astype(BF16))).astype(BF16)
        acc = acc + _dot(act, wd_ref[cols, :].astype(BF16))
    y = x + 0.5 * acc
    if final:
        y = _rms(y, fg_ref[...])
    o_ref[...] = y


def _ffn(x, gain, wg, wu, wd, final_gain, *, final, tm=1024, tf=256):
    t, d = x.shape
    f = wg.shape[1]
    resident = lambda shape: pl.BlockSpec(shape, lambda i: (0, 0), pipeline_mode=pl.Buffered(1))
    return pl.pallas_call(
        functools.partial(_ffn_body, tf=tf, final=final),
        out_shape=jax.ShapeDtypeStruct((t, d), F32),
        grid=(t // tm,),
        in_specs=[
            pl.BlockSpec((tm, d), lambda i: (i, 0)),
            pl.BlockSpec((1, d), lambda i: (0, 0)),
            resident((d, f)),
            resident((d, f)),
            resident((f, d)),
            pl.BlockSpec((1, d), lambda i: (0, 0)),
        ],
        out_specs=pl.BlockSpec((tm, d), lambda i: (i, 0)),
        compiler_params=_params(("parallel",)),
        name="ffn_final" if final else "ffn",
    )(x, gain, wg, wu, wd, final_gain)


def _write_kv(h, w_ref, wt_ref, o_ref, ot_ref, *, seq):
    dh, tk, tm = NSA_HEAD_DIM, NSA_TQ, h.shape[0]
    y = _dot(h, w_ref[...])
    row = lax.broadcasted_iota(jnp.int32, (tm, LANES), 0)
    lane = lax.broadcasted_iota(jnp.int32, (tm, LANES), 1)
    pos = (pl.program_id(0) * tm) % seq + row
    block_mark = jnp.where(lane - dh == (pos >> (SEL_BLOCK.bit_length() - 1)), MASKED, 0.0)
    for tile in range(KV_COLS // LANES):
        part = y[:, tile * LANES:(tile + 1) * LANES]
        if tile // NSA_KV_GROUPS == ROW_K_SEL:
            part = part + block_mark
        o_ref[:, tile * LANES:(tile + 1) * LANES] = part.astype(BF16)

    y_t = _dot_nt(wt_ref[...], h)
    row_t = lax.broadcasted_iota(jnp.int32, y_t.shape, 0)
    y_t = (y_t + jnp.where((row_t & (LANES - 1)) == dh, 1.0, 0.0)).astype(BF16)
    for c in range(tm // tk):
        ot_ref[0, c] = y_t[:, c * tk:(c + 1) * tk]


def _in_proj_body(x_ref, xp_ref, g_ref, w_ref, wkvt_ref, mu_ref, wl_ref, w0_ref,
                  a0_ref, kk_ref, ka_ref, proj_o, kv_o, kvt_o, r_o, k_o, v_o, kk_o, b_o, lw_o, g_o,
                  *, tiles_per_seq, seq):
    i = pl.program_id(0)
    gain = g_ref[...]
    h = _rms(x_ref[...], gain).astype(BF16)
    wr_ref = w_ref.at[:, pl.ds(W_RWKV, RWKV_PROJ)]
    proj_o[...] = _dot(h, w_ref[:, W_OTHER:W_OTHER + PROJ_COLS])
    _write_kv(h, w_ref.at[:, pl.ds(W_KV, KV_COLS)], wkvt_ref, kv_o, kvt_o, seq=seq)
    n_up = xp_ref.shape[0]
    h_up = _rms(xp_ref[...], gain).astype(BF16)
    p_all = _dot(jnp.concatenate([h_up, h], axis=0), wr_ref[...])
    p = p_all[n_up:]
    keep = jnp.where(i % tiles_per_seq == 0, 0.0, 1.0)
    prev_last = p_all[n_up - 1:n_up, :] * keep
    rows = lax.broadcasted_iota(jnp.int32, p.shape, 0)
    shifted = jnp.where(rows == 0, prev_last, pltpu.roll(p, 1, 0))
    x = p + (shifted - p) * mu_ref[...]

    r = x[:, 0:RWKV_DIM]
    k = x[:, RWKV_DIM:2 * RWKV_DIM]
    v = x[:, 2 * RWKV_DIM:3 * RWKV_DIM]
    s = x[:, 3 * RWKV_DIM:RWKV_PROJ]
    lane = lax.broadcasted_iota(jnp.int32, s.shape, 1)
    z = jnp.where(lane < DECAY_LORA, jnp.tanh(s),
                  jnp.where(lane < DECAY_LORA + AAA_LORA, s, jax.nn.sigmoid(s)))
    lo = _dot(z.astype(BF16), wl_ref[...])
    a = jax.nn.sigmoid(a0_ref[...] + lo[:, RWKV_DIM:2 * RWKV_DIM])

    kkr = k * kk_ref[...]
    sq = kkr * kkr
    sq_hi = sq.astype(BF16)
    sq_lo = (sq - sq_hi.astype(F32)).astype(BF16)
    shift = RWKV_HEAD_DIM.bit_length() - 1
    same_head = ((lax.broadcasted_iota(jnp.int32, (RWKV_DIM, RWKV_DIM), 0) >> shift)
                 == (lax.broadcasted_iota(jnp.int32, (RWKV_DIM, RWKV_DIM), 1) >> shift)).astype(BF16)
    ssq = _dot(sq_hi, same_head) + _dot(sq_lo, same_head)
    kk = kkr / jnp.maximum(jnp.sqrt(ssq), 1e-12)

    r_o[...] = r
    k_o[...] = k * (1.0 + (a - 1.0) * ka_ref[...])
    v_o[...] = v
    kk_o[...] = kk
    b_o[...] = kk * a
    lw_o[...] = -math.exp(-0.5) * jax.nn.sigmoid(w0_ref[...] + lo[:, 0:RWKV_DIM])
    g_o[...] = lo[:, 2 * RWKV_DIM:3 * RWKV_DIM]


def _in_proj_call(x, seq, gain, w_all, w_kv_t, mu, w_lora, w0, a0, k_k, k_a, *, tm=512):
    t, d = x.shape
    tk = NSA_TQ
    per_seq = seq // tm
    row = lambda i: (i, 0)
    const = lambda i: (0, 0)
    resident = lambda shape: pl.BlockSpec(shape, const, pipeline_mode=pl.Buffered(1))
    vec = pl.BlockSpec((1, RWKV_DIM), const)
    tok = jax.ShapeDtypeStruct((t, RWKV_DIM), F32)
    return pl.pallas_call(
        functools.partial(_in_proj_body, tiles_per_seq=per_seq, seq=seq),
        out_shape=[jax.ShapeDtypeStruct((t, PROJ_COLS), F32),
                   jax.ShapeDtypeStruct((t, KV_COLS), BF16),
                   jax.ShapeDtypeStruct((t // seq, seq // tk, KV_T_ROWS, tk), BF16)] + [tok] * 7,
        grid=(t // tm,),
        in_specs=[
            pl.BlockSpec((tm, d), row),
            pl.BlockSpec((16, d), lambda i: (jnp.maximum(i * (tm // 16) - 1, 0), 0)),
            pl.BlockSpec((1, d), const),
            pl.BlockSpec((d, W_WINDOW), lambda i: (0, w_all.shape[1] // W_WINDOW - 1),
                         pipeline_mode=pl.Buffered(1)),
            resident((KV_T_ROWS, d)),
            pl.BlockSpec((1, RWKV_PROJ), const),
            resident((LORA_DIM, 3 * RWKV_DIM)),
            vec, vec, vec, vec,
        ],
        out_specs=[pl.BlockSpec((tm, PROJ_COLS), row),
                   pl.BlockSpec((tm, KV_COLS), row),
                   pl.BlockSpec((1, tm // tk, KV_T_ROWS, tk),
                                lambda i: (i // per_seq, i % per_seq, 0, 0))]
        + [pl.BlockSpec((tm, RWKV_DIM), row)] * 7,
        compiler_params=_params(("parallel",)),
        name="in_proj",
    )(x, x, gain, w_all, w_kv_t, mu, w_lora, w0, a0, k_k, k_a)


def _rwkv_scan_body(r_ref, k_ref, v_ref, kk_ref, b_ref, lw_ref, g_ref, rk_ref, gg_ref, gb_ref,
                    o_ref, st_ref):
    c_sz, n, nh = RWKV_CHUNK, RWKV_HEAD_DIM, RWKV_HEADS

    @pl.when(pl.program_id(1) == 0)
    def _():
        st_ref[...] = jnp.zeros_like(st_ref)

    ri = lax.broadcasted_iota(jnp.int32, (c_sz, c_sz), 0)
    ci = lax.broadcasted_iota(jnp.int32, (c_sz, c_sz), 1)
    incl = ci <= ri
    eye_b = (ci == ri).astype(BF16)
    row2 = lax.broadcasted_iota(jnp.int32, (c_sz, 2 * c_sz), 0)
    lane2 = lax.broadcasted_iota(jnp.int32, (c_sz, 2 * c_sz), 1)
    right_half = lane2 >= c_sz
    zeros_b = jnp.zeros((c_sz, n), BF16)
    inv_shift = RWKV_INV_BLOCK.bit_length() - 1

    rows = []
    for bb in range(st_ref.shape[0]):
        lw = lw_ref[bb]
        cum = _dot(incl.astype(F32), lw, HI)
        cum_last = cum[c_sz - 1:c_sz, :]
        r, k, v, b = r_ref[bb], k_ref[bb], v_ref[bb], b_ref[bb]
        p_inv = jnp.exp(-cum)
        p_end = jnp.exp(cum_last - cum)
        rows.append(dict(
            left=jnp.concatenate([(-(kk_ref[bb] * jnp.exp(cum - lw))).astype(BF16),
                                  (r * jnp.exp(cum)).astype(BF16)], axis=0),
            bt=(b * p_inv).astype(BF16), kt=(k * p_inv).astype(BF16),
            bh=(b * p_end).astype(BF16), kh=(k * p_end).astype(BF16),
            v=v, v_b=v.astype(BF16), d_p=jnp.exp(cum_last), rk=r * k * rk_ref[...]))

    units = [(bb, h) for bb in range(len(rows)) for h in range(nh)]
    col = lambda name, u: rows[u[0]][name][:, u[1] * n:(u[1] + 1) * n]
    a_all = [_dot_nt(col('left', u), jnp.concatenate([col('bt', u), col('kt', u)], axis=0))
             for u in units]
    key2 = jnp.where(right_half, lane2 - c_sz, lane2)
    w_u = [jnp.where(right_half & (key2 < row2), a[:c_sz], 0.0).astype(BF16) for a in a_all]
    w_y = [jnp.where(key2 <= row2, a[c_sz:], 0.0).astype(BF16) for a in a_all]

    same_block = (row2 >> inv_shift) == (key2 >> inv_shift)
    x = [jnp.where((lane2 < row2) & same_block, a[:c_sz], jnp.where(lane2 == row2 + c_sz, 1.0, 0.0))
         for a in a_all]
    for _ in range(inv_shift):
        hi = [xu.astype(BF16) for xu in x]
        lo = [(xu - h_.astype(F32)).astype(BF16) for xu, h_ in zip(x, hi)]
        x = [_dot(h_[:, :c_sz], h_) + _dot(h_[:, :c_sz], l_) + _dot(l_[:, :c_sz], h_)
             + jnp.where(right_half, xu, 0.0) for xu, h_, l_ in zip(x, hi, lo)]
    x_b = [xu.astype(BF16) for xu in x]
    solve_diag = lambda j, z: _dot(x_b[j], jnp.concatenate([zeros_b, z.astype(BF16)], axis=0))
    q_b = [solve_diag(j, jnp.where((ci < ri) & ((ri >> inv_shift) != (ci >> inv_shift)),
                                   a[:c_sz, :c_sz], 0.0)).astype(BF16)
           for j, a in enumerate(a_all)]

    s0 = [st_ref[bb, h] for bb, h in units]
    ls0 = [_dot_nt(col('left', u), s0[j].astype(BF16)) for j, u in enumerate(units)]
    rhs = [ls0[j][:c_sz] + _dot(w_u[j], jnp.concatenate([zeros_b, col('v_b', u)], axis=0))
           for j, u in enumerate(units)]
    g0 = [solve_diag(j, rhs[j]) for j in range(len(units))]
    u_f = g0
    for _ in range(c_sz // RWKV_INV_BLOCK - 1):
        u_f = [g0[j] + _dot(q_b[j], u_f[j].astype(BF16)) for j in range(len(units))]
    u_b = [uj.astype(BF16) for uj in u_f]
    uv = [jnp.concatenate([u_b[j], col('v_b', u)], axis=0) for j, u in enumerate(units)]
    y = [ls0[j][c_sz:] + _dot(w_y[j], uv[j]) for j in range(len(units))]
    uv_t = [_dot_nt(eye_b, uv_j).astype(BF16) for uv_j in uv]
    for j, u in enumerate(units):
        st_ref[u[0], u[1]] = (s0[j] * col('d_p', u)
                              + _dot(uv_t[j], jnp.concatenate([col('bh', u), col('kh', u)], axis=0)))

    for bb in range(len(rows)):
        outs = []
        for h in range(nh):
            sl = slice(h * n, (h + 1) * n)
            yh = y[bb * nh + h]
            mean = jnp.mean(yh, axis=-1, keepdims=True)
            var = jnp.mean(jnp.square(yh - mean), axis=-1, keepdims=True)
            yn = (yh - mean) * lax.rsqrt(var + RWKV_GN_EPS)
            yn = yn * gg_ref[:, sl] + gb_ref[:, sl]
            bonus = jnp.sum(rows[bb]['rk'][:, sl], axis=-1, keepdims=True) * rows[bb]['v'][:, sl]
            outs.append((yn + bonus) * g_ref[bb, :, sl])
        o_ref[bb] = jnp.concatenate(outs, axis=1)


def _rwkv_scan(r, k, v, kk, b, lw, g, r_k, gn_gain, gn_bias, *, batch, seq, nb=4):
    t = r.shape[0]
    nc = seq // RWKV_CHUNK
    tok = pl.BlockSpec((nb, RWKV_CHUNK, RWKV_DIM), lambda bi, c: (bi, c, 0))
    par = pl.BlockSpec((1, RWKV_DIM), lambda bi, c: (0, 0))
    per_batch = lambda a: a.reshape(batch, seq, RWKV_DIM)
    out = pl.pallas_call(
        _rwkv_scan_body,
        out_shape=jax.ShapeDtypeStruct((batch, seq, RWKV_DIM), F32),
        grid=(batch // nb, nc),
        in_specs=[tok] * 7 + [par] * 3,
        out_specs=tok,
        scratch_shapes=[pltpu.VMEM((nb, RWKV_HEADS, RWKV_HEAD_DIM, RWKV_HEAD_DIM), F32)],
        compiler_params=_params(("parallel", "arbitrary")),
        name="rwkv_scan",
    )(*(per_batch(a) for a in (r, k, v, kk, b, lw, g)), r_k, gn_gain, gn_bias)
    return out.reshape(t, RWKV_DIM)


def _compress_body(x_ref, w1s_ref, w1_ref, pe_ref, w2_ref, w2t_ref, o_ref, ot_ref):
    n_rows = x_ref.shape[1]
    for kind in range(2):
        pe = jnp.broadcast_to(pe_ref[kind], (8, pe_ref.shape[2])).astype(BF16)
        pe_term = _dot(pe, w1_ref[kind])[0:1, :]
        for gi in range(NSA_KV_GROUPS):
            tile = kind * NSA_KV_GROUPS + gi
            both = jnp.zeros((n_rows, 2 * CMP_HIDDEN), F32)
            for l in range(CMP_STRIDE):
                both = both + _dot(x_ref[0, :, l, tile * LANES:(tile + 1) * LANES], w1s_ref[kind, l])
            second_next = pltpu.roll(both[:, CMP_HIDDEN:], n_rows - 1, 0)
            hid = both[:, :CMP_HIDDEN] + second_next + pe_term
            act = jax.nn.gelu(hid).astype(BF16)
            o_ref[kind, gi] = _dot(act, w2_ref[kind])
            ot_ref[kind, gi] = _dot_nt(w2t_ref[kind], act)


def _compress(kv, w1, pe, w2, *, bsz, seq):
    g, dh = NSA_KV_GROUPS, NSA_HEAD_DIM
    rows = seq // CMP_STRIDE
    w1r = jnp.pad(w1.reshape(2, CMP_LEN, dh, CMP_HIDDEN), ((0, 0), (0, 0), (0, LANES - dh), (0, 0)))
    w1s = jnp.concatenate([w1r[:, :CMP_STRIDE], w1r[:, CMP_STRIDE:]], axis=3)
    whole = lambda a: pl.BlockSpec(a.shape, lambda b: (0,) * a.ndim)
    w2t = w2.transpose(0, 2, 1)
    return pl.pallas_call(
        _compress_body,
        out_shape=[jax.ShapeDtypeStruct((2, bsz * g, rows, dh), F32),
                   jax.ShapeDtypeStruct((2, bsz * g, dh, rows), F32)],
        grid=(bsz,),
        in_specs=[pl.BlockSpec((1, rows, CMP_STRIDE, 2 * g * LANES), lambda b: (b, 0, 0, 0)),
                  whole(w1s), whole(w1), whole(pe), whole(w2), whole(w2t)],
        out_specs=[pl.BlockSpec((2, g, rows, dh), lambda b: (0, b, 0, 0)),
                   pl.BlockSpec((2, g, dh, rows), lambda b: (0, b, 0, 0))],
        compiler_params=_params(("parallel",)),
        name="nsa_compress",
    )(kv.reshape(bsz, rows, CMP_STRIDE, KV_COLS), w1s, w1, pe, w2, w2t)


def _t5_bucket(dist):
    n = jnp.maximum(dist, 0)
    exact = REL_BUCKETS // 2
    nf = jnp.maximum(n, 1).astype(F32)
    scaled = jnp.log(nf / exact) / math.log(REL_MAX_DIST / exact) * (REL_BUCKETS - exact)
    large = exact + jnp.floor(scaled).astype(jnp.int32)
    large = jnp.minimum(large, REL_BUCKETS - 1)
    return jnp.where(n < exact, n, large)


def _bias_body(tab_ref, bc_ref, bt_ref, *, seq, n_cmp_pad):
    h = pl.program_id(0)
    tq = NSA_TQ

    def lookup(dist):
        bucket = _t5_bucket(dist)
        out = jnp.zeros(dist.shape, F32)
        for bkt in range(REL_BUCKETS):
            out = jnp.where(bucket == bkt, tab_ref[bkt, h] * LOG2E, out)
        return out

    far = tab_ref[REL_BUCKETS - 1, h] * LOG2E
    assert tq + 1 >= REL_MAX_DIST and WINDOW - tq + 1 >= REL_MAX_DIST

    key = lax.broadcasted_iota(jnp.int32, (tq, tq), 0)
    qry = lax.broadcasted_iota(jnp.int32, (tq, tq), 1)
    bt_ref[0, BT_DIAG] = jnp.where(qry >= key, lookup(qry - key), MASKED)
    bt_ref[0, BT_PREV] = lookup(tq + qry - key)
    bt_ref[0, BT_FAR] = jnp.full((tq, tq), far, F32)
    bt_ref[0, BT_WIN_EDGE] = jnp.where(qry < key, far, MASKED)
    bt_ref[0, BT_NONE] = jnp.full((tq, tq), MASKED, F32)

    per_tile = tq // CMP_STRIDE
    pad = 16
    assert pad * CMP_STRIDE >= REL_MAX_DIST + CMP_LEN - 1 and pad % 8 == 0
    band = per_tile + pad
    cmp_end = lax.broadcasted_iota(jnp.int32, (band, tq), 0) * CMP_STRIDE + CMP_LEN - 1
    qry_c = lax.broadcasted_iota(jnp.int32, (band, tq), 1)

    def cmp_tile(i, carry):
        start = pl.multiple_of(jnp.maximum(i * per_tile - pad, 0), 8)
        bc_ref[0, i] = jnp.full((n_cmp_pad, tq), far, F32)
        bc_ref[0, i, pl.ds(start, band), :] = lookup(i * tq + qry_c - (start * CMP_STRIDE + cmp_end))
        return carry

    lax.fori_loop(0, seq // tq, cmp_tile, 0)


def _bias_tables(rel_bias, seq):
    g, hpg, tq = NSA_KV_GROUPS, NSA_HPG, NSA_TQ
    n_cmp_pad = seq // CMP_STRIDE
    nq = seq // tq
    return pl.pallas_call(
        functools.partial(_bias_body, seq=seq, n_cmp_pad=n_cmp_pad),
        out_shape=[jax.ShapeDtypeStruct((g, nq, n_cmp_pad, hpg * tq), F32),
                   jax.ShapeDtypeStruct((g, BT_COUNT, tq, hpg * tq), F32)],
        grid=(NSA_HEADS,),
        in_specs=[pl.BlockSpec(memory_space=pltpu.SMEM)],
        out_specs=[pl.BlockSpec((1, nq, n_cmp_pad, tq), lambda h: (h // hpg, 0, 0, h % hpg)),
                   pl.BlockSpec((1, BT_COUNT, tq, tq), lambda h: (h // hpg, 0, 0, h % hpg))],
        compiler_params=_params(("parallel",)),
        name="nsa_bias",
    )(rel_bias)


def _eye(n, dtype):
    return (lax.broadcasted_iota(jnp.int32, (n, n), 0)
            == lax.broadcasted_iota(jnp.int32, (n, n), 1)).astype(dtype)


def _nsa_body(q_ref, kc_ref, vct_ref, ks_ref, vst_ref, kw_ref, vwt_ref, bc_ref, bt_ref, gate_ref,
              o_ref, m_ref, acc_ref):
    tq, hpg, dh, n_grp = NSA_TQ, NSA_HPG, NSA_HEAD_DIM, NSA_KV_GROUPS
    rws = hpg * tq
    n_blk_log2 = SEL_BLOCK.bit_length() - 1
    n_blk = ks_ref.shape[1] // SEL_BLOCK
    n_cmp_pad = kc_ref.shape[2]
    step = pl.program_id(1)
    n_win = WINDOW // tq
    chains = [(gi, qi) for qi in range(NSA_QT) for gi in range(n_grp)]
    lanes_of = lambda gi: slice(gi * LANES, (gi + 1) * LANES)

    def per_head(x):
        return [x[:, hh * tq:(hh + 1) * tq] for hh in range(hpg)]

    def before_loop(gi, qi):
        i = NSA_QT * step + qi
        rows = slice(qi * tq, (qi + 1) * tq)
        xq = (q_ref[0, rows, gi * hpg * dh:(gi + 1) * hpg * dh] * (dh ** -0.5 * LOG2E)).astype(BF16)
        eye_d = _eye(dh, BF16)
        q_t = jnp.concatenate([_dot_nt(eye_d, xq[:, hh * dh:(hh + 1) * dh]) for hh in range(hpg)],
                              axis=1).astype(BF16)

        cmp_id = lax.broadcasted_iota(jnp.int32, (n_cmp_pad, rws), 0)
        t_pos = i * tq + (lax.broadcasted_iota(jnp.int32, (n_cmp_pad, rws), 1) & (tq - 1))
        valid = (t_pos - (cmp_id * CMP_STRIDE + CMP_LEN - 1) >= 0) & (cmp_id < n_cmp_pad - 1)
        s = jnp.where(valid, _dot(kc_ref[0, gi].astype(BF16), q_t) + bc_ref[gi, qi], MASKED)
        e = jnp.where(valid, jnp.exp2(s - jnp.max(s, axis=0, keepdims=True)), 0.0)
        den = jnp.sum(e, axis=0, keepdims=True)
        p_c = e / jnp.where(den > 0.0, den, 1.0)
        o_cmp = _dot(vct_ref[0, gi].astype(BF16), p_c.astype(BF16))

        p_heads = per_head(p_c)
        p_sum = p_heads[0]
        for ph in p_heads[1:]:
            p_sum = p_sum + ph
        blk_o = lax.broadcasted_iota(jnp.int32, (n_blk, n_cmp_pad), 0)
        cmp_o = lax.broadcasted_iota(jnp.int32, (n_blk, n_cmp_pad), 1)
        overlap_t = ((cmp_o * CMP_STRIDE <= blk_o * SEL_BLOCK + SEL_BLOCK - 1)
                     & (cmp_o * CMP_STRIDE + CMP_LEN - 1 >= blk_o * SEL_BLOCK)).astype(F32)
        imp = _dot(overlap_t, p_sum, HI)
        jj = lax.broadcasted_iota(jnp.int32, (n_blk, tq), 0)
        cur = (i * tq + lax.broadcasted_iota(jnp.int32, (n_blk, tq), 1)) >> n_blk_log2
        forced = (jj == 0) | (jj == cur) | (jj == cur - 1)
        imp = jnp.where(jj > cur, -1e6, jnp.where(forced, 1e6, imp))
        rank = jnp.zeros((n_blk, tq), jnp.int32)
        for a in range(n_blk):
            row = imp[a:a + 1, :]
            beats = (row > imp) | ((row == imp) & (a < jj))
            rank = rank + beats.astype(jnp.int32)
        not_sel = jnp.where(rank < SEL_TOP_N, 0.0, 1.0).astype(BF16)

        q_aug = jnp.concatenate([q_t, jnp.concatenate([not_sel] * hpg, axis=1),
                                 jnp.zeros((LANES - dh - n_blk, rws), BF16)], axis=0)
        q_pad = jnp.concatenate([q_t, jnp.zeros((LANES - dh, rws), BF16)], axis=0)

        tiles, scores = [], []
        for delta in range(n_win + 1):
            entry = {0: BT_DIAG, 1: BT_PREV, n_win: BT_WIN_EDGE}.get(delta, BT_FAR)
            if delta > 0:
                entry = jnp.where(i - delta >= 0, entry, BT_NONE)
            tiles.append(jnp.maximum(i - delta, 0))
            off = pl.multiple_of(tiles[-1] * tq, tq)
            scores.append(_dot(kw_ref[0, pl.ds(off, tq), lanes_of(gi)], q_pad) + bt_ref[gi, entry])
        m_all = scores[0]
        for sc in scores[1:]:
            m_all = jnp.maximum(m_all, sc)
        m_w = jnp.max(m_all, axis=0, keepdims=True)
        acc_w = jnp.zeros((LANES, rws), F32)
        for kt, sc in zip(tiles, scores):
            acc_w = acc_w + _dot(vwt_ref[0, kt, lanes_of(gi), :], jnp.exp2(sc - m_w).astype(BF16))
        o_win = acc_w[0:dh] / acc_w[dh:dh + 1]

        g_t = _dot_nt(_eye(GATE_PAD, F32), gate_ref[0, rows, gi * GATE_PAD:(gi + 1) * GATE_PAD], HI)
        sig = jax.nn.sigmoid(g_t[0:4 * hpg])
        gate = [jnp.concatenate([sig[br * hpg + hh:br * hpg + hh + 1] for hh in range(hpg)], axis=1)
                for br in range(3)]
        return q_aug, gate[0] * o_cmp + gate[2] * o_win, gate[1]

    prepared = {c: before_loop(*c) for c in chains}

    m_ref[...] = jnp.full(m_ref.shape, MASKED, F32)
    acc_ref[...] = jnp.zeros(acc_ref.shape, F32)

    def sel_update(tiles, which):
        for gi, qi in which:
            i = NSA_QT * step + qi
            q_aug = prepared[(gi, qi)][0]
            scores = []
            for kt in tiles:
                off = pl.multiple_of(kt * tq, tq)
                scores.append(_dot(ks_ref[0, pl.ds(off, tq), lanes_of(gi)], q_aug)
                              + bt_ref[gi, jnp.minimum(i - kt, BT_FAR)])
            s_max = scores[0]
            for sc in scores[1:]:
                s_max = jnp.maximum(s_max, sc)
            m_prev = m_ref[qi, gi]
            m_new = jnp.maximum(m_prev, jnp.max(s_max, axis=0, keepdims=True))
            acc = jnp.exp2(m_prev - m_new) * acc_ref[qi, gi]
            for kt, sc in zip(tiles, scores):
                acc = acc + _dot(vst_ref[0, kt, lanes_of(gi), :], jnp.exp2(sc - m_new).astype(BF16))
            m_ref[qi, gi] = m_new
            acc_ref[qi, gi] = acc

    def sel_pair(j, carry):
        sel_update([2 * j, 2 * j + 1], chains)
        return carry

    assert NSA_QT == 2
    lax.fori_loop(0, step, sel_pair, 0)
    last = NSA_QT * step
    sel_update([last], [c for c in chains if c[1] == 0])
    sel_update([last, last + 1], [c for c in chains if c[1] == 1])

    eye_q = _eye(tq, BF16)
    for qi in range(NSA_QT):
        outs = []
        for gi in range(n_grp):
            _, cmp_win, gate_sel = prepared[(gi, qi)]
            o_sel = acc_ref[qi, gi, 0:dh, :] / acc_ref[qi, gi, dh:dh + 1, :]
            y_t = (cmp_win + gate_sel * o_sel).astype(BF16)
            outs += [_dot_nt(eye_q, yh) for yh in per_head(y_t)]
        o_ref[0, qi * tq:(qi + 1) * tq, :] = jnp.concatenate(outs, axis=1).astype(BF16)


def _nsa_attention(proj, kv, kv_t, kvc, kvc_t, bias_c, bias_t, *, bsz, seq):
    g, tq, hpg, dh = NSA_KV_GROUPS, NSA_TQ, NSA_HPG, NSA_HEAD_DIM
    n_rows = kvc.shape[2]
    nq = seq // tq
    qt = NSA_QT
    kv_spec = lambda pos: pl.BlockSpec((1, seq, g * LANES), lambda b, i: (b, 0, pos))
    kvt_spec = lambda pos: pl.BlockSpec((1, nq, g * LANES, tq), lambda b, i: (b, 0, pos, 0))
    kv3 = kv.reshape(bsz, seq, KV_COLS)
    proj3 = proj.reshape(bsz, seq, PROJ_COLS)
    out = pl.pallas_call(
        _nsa_body,
        out_shape=jax.ShapeDtypeStruct((bsz, seq, NSA_DIM), BF16),
        grid=(bsz, nq // qt),
        in_specs=[
            pl.BlockSpec((1, qt * tq, NSA_DIM), lambda b, i: (b, i, COL_QNSA // NSA_DIM)),
            pl.BlockSpec((1, g, n_rows, dh), lambda b, i: (0, b, 0, 0)),
            pl.BlockSpec((1, g, dh, n_rows), lambda b, i: (1, b, 0, 0)),
            kv_spec(ROW_K_SEL), kvt_spec(T_V_SEL), kv_spec(ROW_K_WIN), kvt_spec(T_V_WIN),
            pl.BlockSpec((g, qt, n_rows, hpg * tq), lambda b, i: (0, i, 0, 0)),
            pl.BlockSpec((g, BT_COUNT, tq, hpg * tq), lambda b, i: (0, 0, 0, 0),
                         pipeline_mode=pl.Buffered(1)),
            pl.BlockSpec((1, qt * tq, g * GATE_PAD), lambda b, i: (b, i, COL_GNSA // (g * GATE_PAD))),
        ],
        out_specs=pl.BlockSpec((1, qt * tq, NSA_DIM), lambda b, i: (b, i, 0)),
        scratch_shapes=[pltpu.VMEM((qt, g, 1, hpg * tq), F32),
                        pltpu.VMEM((qt, g, LANES, hpg * tq), F32)],
        compiler_params=_params(("parallel", "arbitrary")),
        name="nsa_attention",
    )(proj3, kvc, kvc_t, kv3, kv_t, kv3, kv_t, bias_c, bias_t, proj3)
    return out.reshape(bsz * seq, NSA_DIM)


def _mem_body(q_ref, mem_ref, g_ref, w_ref, o_ref, kv_ref):
    @pl.when(pl.program_id(1) == 0)
    def _():
        kv_ref[...] = _dot(_rms(mem_ref[0], g_ref[...]).astype(BF16), w_ref[...]).astype(BF16)

    outs = []
    for h in range(MEM_HEADS):
        sl = slice(h * MEM_HEAD_DIM, (h + 1) * MEM_HEAD_DIM)
        qh = (q_ref[:, sl] * (MEM_HEAD_DIM ** -0.5)).astype(BF16)
        s = _dot_nt(qh, kv_ref[:, sl])
        e = jnp.exp(s - jnp.max(s, axis=-1, keepdims=True))
        p = e / jnp.sum(e, axis=-1, keepdims=True)
        outs.append(_dot(p.astype(BF16), kv_ref[:, MEM_DIM + h * MEM_HEAD_DIM:
                                                MEM_DIM + (h + 1) * MEM_HEAD_DIM]))
    o_ref[...] = jnp.concatenate(outs, axis=1)


def _mem_attention(proj, mem, gain, w_kv, *, seq, tq=1024):
    t = proj.shape[0]
    bsz, m, d = mem.shape
    per_seq = seq // tq
    return pl.pallas_call(
        _mem_body,
        out_shape=jax.ShapeDtypeStruct((t, MEM_DIM), F32),
        grid=(bsz, per_seq),
        in_specs=[
            pl.BlockSpec((tq, MEM_DIM), lambda b, j: (b * per_seq + j, COL_QMEM // MEM_DIM)),
            pl.BlockSpec((1, m, d), lambda b, j: (b, 0, 0)),
            pl.BlockSpec((1, d), lambda b, j: (0, 0)),
            pl.BlockSpec((d, 2 * MEM_DIM), lambda b, j: (0, 0), pipeline_mode=pl.Buffered(1)),
        ],
        out_specs=pl.BlockSpec((tq, MEM_DIM), lambda b, j: (b * per_seq + j, 0)),
        scratch_shapes=[pltpu.VMEM((m, 2 * MEM_DIM), BF16)],
        compiler_params=_params(("parallel", "arbitrary")),
        name="mem_attention",
    )(proj, mem, gain, w_kv)


def _merge_body(x_ref, gain_ref, yr_ref, yn_ref, ym_ref, wg_ref, wr_ref, wn_ref, wm_ref, wo_ref,
                o_ref):
    d = x_ref.shape[1]
    x = x_ref[...]
    h = _rms(x, gain_ref[...]).astype(BF16)
    merged = jnp.zeros(x.shape, F32)
    for b, (y_ref, w_ref) in enumerate(((yr_ref, wr_ref), (yn_ref, wn_ref), (ym_ref, wm_ref))):
        gate = jax.nn.sigmoid(_dot(h, wg_ref[:, b * d:(b + 1) * d]))
        merged = merged + gate * _dot(y_ref[...].astype(BF16), w_ref[...])
    o_ref[...] = x + _dot(merged.astype(BF16), wo_ref[...])


def _merge(x, gain, y_rwkv, y_nsa, y_mem, w_g, w_r, w_n, w_m, w_o, *, tm=1024):
    t, d = x.shape
    row = lambda i: (i, 0)
    const = lambda i: (0, 0)
    resident = lambda shape: pl.BlockSpec(shape, const, pipeline_mode=pl.Buffered(1))
    return pl.pallas_call(
        _merge_body,
        out_shape=jax.ShapeDtypeStruct((t, d), F32),
        grid=(t // tm,),
        in_specs=[
            pl.BlockSpec((tm, d), row),
            pl.BlockSpec((1, d), const),
            pl.BlockSpec((tm, RWKV_DIM), row),
            pl.BlockSpec((tm, NSA_DIM), row),
            pl.BlockSpec((tm, MEM_DIM), row),
            resident((d, N_BRANCH * d)),
            resident((RWKV_DIM, d)),
            resident((NSA_DIM, d)),
            resident((MEM_DIM, d)),
            resident((d, d)),
        ],
        out_specs=pl.BlockSpec((tm, d), row),
        compiler_params=_params(("parallel",)),
        name="merge",
    )(x, gain, y_rwkv, y_nsa, y_mem, w_g, w_r, w_n, w_m, w_o)


def _row(a):
    return a.reshape(1, -1)


def _regroup_w_in(w_in):
    d = w_in.shape[0]
    g, hpg, dh = NSA_KV_GROUPS, NSA_HPG, NSA_HEAD_DIM
    o = 0
    parts = {}
    for name, size in (('rwkv', RWKV_PROJ), ('q', NSA_DIM), ('kv', KV_KINDS * NSA_KV_DIM),
                       ('g_nsa', 3 * NSA_HEADS), ('q_mem', MEM_DIM), ('g_branch', N_BRANCH * d)):
        parts[name] = w_in[:, o:o + size]
        o += size
    assert o == w_in.shape[1]
    gates = parts['g_nsa'].reshape(d, 3, g, hpg).transpose(0, 2, 1, 3).reshape(d, g, 3 * hpg)
    gates = jnp.pad(gates, ((0, 0), (0, 0), (0, GATE_PAD - 3 * hpg))).reshape(d, g * GATE_PAD)
    zeros =lambda n: jnp.zeros((d, n), w_in.dtype)
    kv_tile = lambda kind, gi: parts['kv'][:, (kind * g + gi) * dh:(kind * g + gi + 1) * dh]
    row_tiles = [t for kind in KV_ROW_KINDS for gi in range(g) for t in (kv_tile(kind, gi), zeros(LANES - dh))]
    gap = -(N_BRANCH * d) % W_WINDOW
    w_all = jnp.concatenate([parts['g_branch'], zeros(gap), parts['rwkv'], parts['q'], parts['q_mem'],
                             gates] + row_tiles, axis=1).astype(BF16)
    t_tiles = [t for kind in KV_T_KINDS for gi in range(g) for t in (kv_tile(kind, gi), zeros(LANES - dh))]
    w_t = jnp.concatenate(t_tiles, axis=1).T.astype(BF16)
    return w_all, w_t


def _in_proj(x, w_all, w_t, p, l, seq):
    w_lora = jnp.zeros((LORA_DIM, 3 * RWKV_DIM), F32)
    w_lora = w_lora.at[0:DECAY_LORA, 0:RWKV_DIM].set(p['rwkv_w2'][l])
    w_lora = w_lora.at[DECAY_LORA:DECAY_LORA + AAA_LORA, RWKV_DIM:2 * RWKV_DIM].set(p['rwkv_a2'][l])
    w_lora = w_lora.at[DECAY_LORA + AAA_LORA:, 2 * RWKV_DIM:].set(p['rwkv_g2'][l])
    proj, kv, kv_t, *rwkv_in = _in_proj_call(
        x, seq, _row(p['mix_norm'][l]), w_all, w_t, _row(p['rwkv_mu'][l]), w_lora.astype(BF16), _row(p['rwkv_w0'][l]), _row(p['rwkv_a0'][l]),
        _row(p['rwkv_k_k'][l]), _row(p['rwkv_k_a'][l]))
    return proj, kv, kv_t, rwkv_in


def _rwkv_branch(rwkv_in, p, l, bsz, seq):
    return _rwkv_scan(*rwkv_in, _row(p['rwkv_r_k'][l]), _row(p['rwkv_gn_gain'][l]),
                      _row(p['rwkv_gn_bias'][l]), batch=bsz, seq=seq)


def _nsa_branch(proj, kv, kv_t, bias_c, bias_t, p, l, bsz, seq):
    w1 = jnp.stack([p['cmp_k_w1'][l], p['cmp_v_w1'][l]]).astype(BF16)
    pe = jnp.stack([p['cmp_pe_k'][l].reshape(1, -1), p['cmp_pe_v'][l].reshape(1, -1)])
    w2 = jnp.stack([p['cmp_k_w2'][l], p['cmp_v_w2'][l]]).astype(BF16)
    kvc, kvc_t = _compress(kv, w1, pe, w2, bsz=bsz, seq=seq)
    return _nsa_attention(proj, kv, kv_t, kvc, kvc_t, bias_c, bias_t, bsz=bsz, seq=seq)


def _mem_branch(proj, mem, p, l):
    w_kv = jnp.concatenate([p['mem_w_k'][l], p['mem_w_v'][l]], axis=1).astype(BF16)
    return _mem_attention(proj, mem, _row(p['mem_norm'][l]), w_kv, seq=proj.shape[0] // mem.shape[0])


def _layer(x, mem, l, bias_c, bias_t, p):
    bsz, seq, d = x.shape
    t = bsz * seq
    row = _row
    x = x.reshape(t, d)

    x = _ffn(x, row(p['ffn1_norm'][l]), p['ffn1_w_gate'][l], p['ffn1_w_up'][l], p['ffn1_w_down'][l],
             row(p['final_norm']), final=False)

    w_all, w_t = _regroup_w_in(p['w_in'][l])
    proj, kv, kv_t, rwkv_in = _in_proj(x, w_all, w_t, p, l, seq)
    y_rwkv = _rwkv_branch(rwkv_in, p, l, bsz, seq)
    y_nsa = _nsa_branch(proj, kv, kv_t, bias_c, bias_t, p, l, bsz, seq)
    y_mem = _mem_branch(proj, mem, p, l)

    x = _merge(x, row(p['mix_norm'][l]), y_rwkv, y_nsa, y_mem, w_all,
               p['w_br_rwkv'][l].astype(BF16), p['w_br_nsa'][l].astype(BF16),
               p['w_br_mem'][l].astype(BF16), p['w_out'][l].astype(BF16))

    last = l == p['ffn1_norm'].shape[0] - 1
    x = _ffn(x, row(p['ffn2_norm'][l]), p['ffn2_w_gate'][l], p['ffn2_w_up'][l], p['ffn2_w_down'][l],
             row(p['final_norm']), final=last)
    return x.reshape(bsz, seq, d)


def kernel(x, mem, ffn1_norm, ffn1_w_gate, ffn1_w_up, ffn1_w_down, mix_norm, w_in, rwkv_mu, rwkv_w0, rwkv_w2, rwkv_a0, rwkv_a2, rwkv_g2, rwkv_k_k, rwkv_k_a, rwkv_r_k, rwkv_gn_gain, rwkv_gn_bias, cmp_pe_k, cmp_k_w1, cmp_k_w2, cmp_pe_v, cmp_v_w1, cmp_v_w2, rel_bias, mem_norm, mem_w_k, mem_w_v, w_br_rwkv, w_br_nsa, w_br_mem, w_out, ffn2_norm, ffn2_w_gate, ffn2_w_up, ffn2_w_down, final_norm):
    p = dict(ffn1_norm=ffn1_norm, ffn1_w_gate=ffn1_w_gate, ffn1_w_up=ffn1_w_up,
             ffn1_w_down=ffn1_w_down, mix_norm=mix_norm, w_in=w_in, rwkv_mu=rwkv_mu,
             rwkv_w0=rwkv_w0, rwkv_w2=rwkv_w2, rwkv_a0=rwkv_a0, rwkv_a2=rwkv_a2, rwkv_g2=rwkv_g2,
             rwkv_k_k=rwkv_k_k, rwkv_k_a=rwkv_k_a, rwkv_r_k=rwkv_r_k, rwkv_gn_gain=rwkv_gn_gain,
             rwkv_gn_bias=rwkv_gn_bias, cmp_pe_k=cmp_pe_k, cmp_k_w1=cmp_k_w1, cmp_k_w2=cmp_k_w2,
             cmp_pe_v=cmp_pe_v, cmp_v_w1=cmp_v_w1, cmp_v_w2=cmp_v_w2, mem_norm=mem_norm,
             mem_w_k=mem_w_k, mem_w_v=mem_w_v, w_br_rwkv=w_br_rwkv, w_br_nsa=w_br_nsa,
             w_br_mem=w_br_mem, w_out=w_out, ffn2_norm=ffn2_norm, ffn2_w_gate=ffn2_w_gate,
             ffn2_w_up=ffn2_w_up, ffn2_w_down=ffn2_w_down, final_norm=final_norm)
    bias_c, bias_t = _bias_tables(rel_bias, x.shape[1])
    for l in range(ffn1_norm.shape[0]):
        x = _layer(x, mem, l, bias_c, bias_t, p)
    return x
```

```python
import functools
import math

import jax
import jax.numpy as jnp
from jax import lax
from jax.experimental import pallas as pl
from jax.experimental.pallas import tpu as pltpu

F32 = jnp.float32
BF16 = jnp.bfloat16
HI = lax.Precision.HIGHEST

NORM_EPS = 1e-6
RWKV_HEADS = 8
RWKV_HEAD_DIM = 64
RWKV_DIM = RWKV_HEADS * RWKV_HEAD_DIM
DECAY_LORA = 64
AAA_LORA = 64
GATE_LORA = 128
LORA_DIM = DECAY_LORA + AAA_LORA + GATE_LORA
RWKV_GN_EPS = 64e-5
RWKV_PROJ = 3 * RWKV_DIM + LORA_DIM
NSA_HEADS = 8
NSA_KV_GROUPS = 2
NSA_HPG = NSA_HEADS // NSA_KV_GROUPS
NSA_HEAD_DIM = 64
NSA_DIM = NSA_HEADS * NSA_HEAD_DIM
NSA_KV_DIM = NSA_KV_GROUPS * NSA_HEAD_DIM
CMP_LEN = 32
CMP_STRIDE = 16
CMP_HIDDEN = 256
SEL_BLOCK = 64
SEL_TOP_N = 16
WINDOW = 512
REL_BUCKETS = 32
REL_MAX_DIST = 128
MEM_HEADS = 4
MEM_HEAD_DIM = 128
MEM_DIM = MEM_HEADS * MEM_HEAD_DIM
N_BRANCH = 3

LANES = 128
GATE_PAD = LANES
COL_QNSA = 0
COL_QMEM = COL_QNSA + NSA_DIM
COL_GNSA = COL_QMEM + MEM_DIM
PROJ_COLS = COL_GNSA + NSA_KV_GROUPS * GATE_PAD
assert COL_QMEM % MEM_DIM == 0 and COL_GNSA % (NSA_KV_GROUPS * GATE_PAD) == 0 and COL_QNSA % NSA_DIM == 0
KV_KINDS = 6
KV_ROW_KINDS = (0, 1, 2, 4)
KV_T_KINDS = (3, 5)
KV_COLS = len(KV_ROW_KINDS) * NSA_KV_GROUPS * LANES
KV_T_ROWS = len(KV_T_KINDS) * NSA_KV_GROUPS * LANES
ROW_K_SEL, ROW_K_WIN = KV_ROW_KINDS.index(2), KV_ROW_KINDS.index(4)
T_V_SEL, T_V_WIN = KV_T_KINDS.index(3), KV_T_KINDS.index(5)
LOG2E = 1.4426950408889634

RWKV_CHUNK = 64
RWKV_INV_BLOCK = 16
NSA_TQ = 256
NSA_QT = 4
MASKED = -1e30
BT_DIAG, BT_PREV, BT_FAR, BT_WIN_EDGE, BT_NONE, BT_COUNT = 0, 1, 2, 3, 4, 5
V7X_VMEM_BYTES = 64 * 1024 * 1024
VMEM_LIMIT = V7X_VMEM_BYTES - 4 * 1024 * 1024


def _dot(a, b, precision=None):
    return jnp.dot(a, b, preferred_element_type=F32, precision=precision)


def _dot_nt(a, b, precision=None):
    return lax.dot_general(a, b, (((1,), (1,)), ((), ())), preferred_element_type=F32,
                           precision=precision)


def _params(semantics):
    return pltpu.CompilerParams(dimension_semantics=semantics, vmem_limit_bytes=VMEM_LIMIT)


def _rms(x, g):
    return x * lax.rsqrt(jnp.mean(x * x, axis=-1, keepdims=True) + NORM_EPS) * g


def _ffn_body(x_ref, g_ref, wg_ref, wu_ref, wd_ref, fg_ref, o_ref, *, tf, final):
    x = x_ref[...]
    h = _rms(x, g_ref[...]).astype(BF16)
    acc = jnp.zeros(x.shape, F32)
    for j in range(wg_ref.shape[1] // tf):
        cols = slice(j * tf, (j + 1) * tf)
        act = (jax.nn.silu(_dot(h, wg_ref[:, cols].astype(BF16)))
               * _dot(h, wu_ref[:, cols].astype(BF16))).astype(BF16)
        acc = acc + _dot(act, wd_ref[cols, :].astype(BF16))
    y = x + 0.5 * acc
    if final:
        y = _rms(y, fg_ref[...])
    o_ref[...] = y


def _ffn(x, gain, wg, wu, wd, final_gain, *, final, tm=1024, tf=256):
    t, d = x.shape
    f = wg.shape[1]
    resident = lambda shape: pl.BlockSpec(shape, lambda i: (0, 0), pipeline_mode=pl.Buffered(1))
    return pl.pallas_call(
        functools.partial(_ffn_body, tf=tf, final=final),
        out_shape=jax.ShapeDtypeStruct((t, d), F32),
        grid=(t // tm,),
        in_specs=[
            pl.BlockSpec((tm, d), lambda i: (i, 0)),
            pl.BlockSpec((1, d), lambda i: (0, 0)),
            resident((d, f)),
            resident((d, f)),
            resident((f, d)),
            pl.BlockSpec((1, d), lambda i: (0, 0)),
        ],
        out_specs=pl.BlockSpec((tm, d), lambda i: (i, 0)),
        compiler_params=_params(("parallel",)),
        name="ffn_final" if final else "ffn",
    )(x, gain, wg, wu, wd, final_gain)


def _write_kv(h, w_ref, wt_ref, o_ref, ot_ref, *, seq):
    dh, tk, tm = NSA_HEAD_DIM, NSA_TQ, h.shape[0]
    y = _dot(h, w_ref[...])
    row = lax.broadcasted_iota(jnp.int32, (tm, LANES), 0)
    lane = lax.broadcasted_iota(jnp.int32, (tm, LANES), 1)
    pos = (pl.program_id(0) * tm) % seq + row
    block_mark = jnp.where(lane - dh == (pos >> (SEL_BLOCK.bit_length() - 1)), MASKED, 0.0)
    for tile in range(KV_COLS // LANES):
        part = y[:, tile * LANES:(tile + 1) * LANES]
        if tile // NSA_KV_GROUPS == ROW_K_SEL:
            part = part + block_mark
        o_ref[:, tile * LANES:(tile + 1) * LANES] = part.astype(BF16)

    y_t = _dot_nt(wt_ref[...], h)
    row_t = lax.broadcasted_iota(jnp.int32, y_t.shape, 0)
    y_t = (y_t + jnp.where((row_t & (LANES - 1)) == dh, 1.0, 0.0)).astype(BF16)
    for c in range(tm // tk):
        ot_ref[0, c] = y_t[:, c * tk:(c + 1) * tk]


def _in_proj_body(x_ref, xp_ref, g_ref, wr_ref, wo_ref, wkv_ref, wkvt_ref, mu_ref, wl_ref, w0_ref,
                  a0_ref, kk_ref, ka_ref, proj_o, kv_o, kvt_o, r_o, k_o, v_o, kk_o, b_o, lw_o, g_o,
                  *, tiles_per_seq, seq):
    i = pl.program_id(0)
    gain = g_ref[...]
    h = _rms(x_ref[...], gain).astype(BF16)
    proj_o[...] = _dot(h, wo_ref[...])
    _write_kv(h, wkv_ref, wkvt_ref, kv_o, kvt_o, seq=seq)
    n_up = xp_ref.shape[0]
    h_up = _rms(xp_ref[...], gain).astype(BF16)
    p_all = _dot(jnp.concatenate([h_up, h], axis=0), wr_ref[...])
    p = p_all[n_up:]
    keep = jnp.where(i % tiles_per_seq == 0, 0.0, 1.0)
    prev_last = p_all[n_up - 1:n_up, :] * keep
    rows = lax.broadcasted_iota(jnp.int32, p.shape, 0)
    shifted = jnp.where(rows == 0, prev_last, pltpu.roll(p, 1, 0))
    x = p + (shifted - p) * mu_ref[...]

    r = x[:, 0:RWKV_DIM]
    k = x[:, RWKV_DIM:2 * RWKV_DIM]
    v = x[:, 2 * RWKV_DIM:3 * RWKV_DIM]
    s = x[:, 3 * RWKV_DIM:RWKV_PROJ]
    lane = lax.broadcasted_iota(jnp.int32, s.shape, 1)
    z = jnp.where(lane < DECAY_LORA, jnp.tanh(s),
                  jnp.where(lane < DECAY_LORA + AAA_LORA, s, jax.nn.sigmoid(s)))
    lo = _dot(z.astype(BF16), wl_ref[...])
    a = jax.nn.sigmoid(a0_ref[...] + lo[:, RWKV_DIM:2 * RWKV_DIM])

    kkr = k * kk_ref[...]
    sq = kkr * kkr
    sq_hi = sq.astype(BF16)
    sq_lo = (sq - sq_hi.astype(F32)).astype(BF16)
    shift = RWKV_HEAD_DIM.bit_length() - 1
    same_head = ((lax.broadcasted_iota(jnp.int32, (RWKV_DIM, RWKV_DIM), 0) >> shift)
                 == (lax.broadcasted_iota(jnp.int32, (RWKV_DIM, RWKV_DIM), 1) >> shift)).astype(BF16)
    ssq = _dot(sq_hi, same_head) + _dot(sq_lo, same_head)
    kk = kkr / jnp.maximum(jnp.sqrt(ssq), 1e-12)

    r_o[...] = r
    k_o[...] = k * (1.0 + (a - 1.0) * ka_ref[...])
    v_o[...] = v
    kk_o[...] = kk
    b_o[...] = kk * a
    lw_o[...] = -math.exp(-0.5) * jax.nn.sigmoid(w0_ref[...] + lo[:, 0:RWKV_DIM])
    g_o[...] = lo[:, 2 * RWKV_DIM:3 * RWKV_DIM]


def _in_proj_call(x, seq, gain, w_rwkv, w_other, w_kv, w_kv_t, mu, w_lora, w0, a0, k_k, k_a, *,
                  tm=512):
    t, d = x.shape
    tk = NSA_TQ
    per_seq = seq // tm
    row = lambda i: (i, 0)
    const = lambda i: (0, 0)
    resident = lambda shape: pl.BlockSpec(shape, const, pipeline_mode=pl.Buffered(1))
    vec = pl.BlockSpec((1, RWKV_DIM), const)
    tok = jax.ShapeDtypeStruct((t, RWKV_DIM), F32)
    return pl.pallas_call(
        functools.partial(_in_proj_body, tiles_per_seq=per_seq, seq=seq),
        out_shape=[jax.ShapeDtypeStruct((t, PROJ_COLS), F32),
                   jax.ShapeDtypeStruct((t, KV_COLS), BF16),
                   jax.ShapeDtypeStruct((t // seq, seq // tk, KV_T_ROWS, tk), BF16)] + [tok] * 7,
        grid=(t // tm,),
        in_specs=[
            pl.BlockSpec((tm, d), row),
            pl.BlockSpec((16, d), lambda i: (jnp.maximum(i * (tm // 16) - 1, 0), 0)),
            pl.BlockSpec((1, d), const),
            resident((d, RWKV_PROJ)),
            resident((d, PROJ_COLS)),
            resident((d, KV_COLS)),
            resident((KV_T_ROWS, d)),
            pl.BlockSpec((1, RWKV_PROJ), const),
            resident((LORA_DIM, 3 * RWKV_DIM)),
            vec, vec, vec, vec,
        ],
        out_specs=[pl.BlockSpec((tm, PROJ_COLS), row),
                   pl.BlockSpec((tm, KV_COLS), row),
                   pl.BlockSpec((1, tm // tk, KV_T_ROWS, tk),
                                lambda i: (i // per_seq, i % per_seq, 0, 0))]
        + [pl.BlockSpec((tm, RWKV_DIM), row)] * 7,
        compiler_params=_params(("parallel",)),
        name="in_proj",
    )(x, x, gain, w_rwkv, w_other, w_kv, w_kv_t, mu, w_lora, w0, a0, k_k, k_a)


def _rwkv_scan_body(r_ref, k_ref, v_ref, kk_ref, b_ref, lw_ref, g_ref, rk_ref, gg_ref, gb_ref,
                    o_ref, st_ref):
    c_sz, n, nh = RWKV_CHUNK, RWKV_HEAD_DIM, RWKV_HEADS

    @pl.when(pl.program_id(1) == 0)
    def _():
        st_ref[...] = jnp.zeros_like(st_ref)

    ri = lax.broadcasted_iota(jnp.int32, (c_sz, c_sz), 0)
    ci = lax.broadcasted_iota(jnp.int32, (c_sz, c_sz), 1)
    incl = ci <= ri
    eye_b = (ci == ri).astype(BF16)
    row2 = lax.broadcasted_iota(jnp.int32, (c_sz, 2 * c_sz), 0)
    lane2 = lax.broadcasted_iota(jnp.int32, (c_sz, 2 * c_sz), 1)
    right_half = lane2 >= c_sz
    zeros_b = jnp.zeros((c_sz, n), BF16)
    inv_shift = RWKV_INV_BLOCK.bit_length() - 1

    rows = []
    for bb in range(st_ref.shape[0]):
        lw = lw_ref[bb]
        cum = _dot(incl.astype(F32), lw, HI)
        cum_last = cum[c_sz - 1:c_sz, :]
        r, k, v, b = r_ref[bb], k_ref[bb], v_ref[bb], b_ref[bb]
        p_inv = jnp.exp(-cum)
        p_end = jnp.exp(cum_last - cum)
        rows.append(dict(
            left=jnp.concatenate([(-(kk_ref[bb] * jnp.exp(cum - lw))).astype(BF16),
                                  (r * jnp.exp(cum)).astype(BF16)], axis=0),
            bt=(b * p_inv).astype(BF16), kt=(k * p_inv).astype(BF16),
            bh=(b * p_end).astype(BF16), kh=(k * p_end).astype(BF16),
            v=v, v_b=v.astype(BF16), d_p=jnp.exp(cum_last), rk=r * k * rk_ref[...]))

    units = [(bb, h) for bb in range(len(rows)) for h in range(nh)]
    col = lambda name, u: rows[u[0]][name][:, u[1] * n:(u[1] + 1) * n]
    a_all = [_dot_nt(col('left', u), jnp.concatenate([col('bt', u), col('kt', u)], axis=0))
             for u in units]
    key2 = jnp.where(right_half, lane2 - c_sz, lane2)
    w_u = [jnp.where(right_half & (key2 < row2), a[:c_sz], 0.0).astype(BF16) for a in a_all]
    w_y = [jnp.where(key2 <= row2, a[c_sz:], 0.0).astype(BF16) for a in a_all]

    same_block = (row2 >> inv_shift) == (key2 >> inv_shift)
    x = [jnp.where((lane2 < row2) & same_block, a[:c_sz], jnp.where(lane2 == row2 + c_sz, 1.0, 0.0))
         for a in a_all]
    for _ in range(inv_shift):
        hi = [xu.astype(BF16) for xu in x]
        lo = [(xu - h_.astype(F32)).astype(BF16) for xu, h_ in zip(x, hi)]
        x = [_dot(h_[:, :c_sz], h_) + _dot(h_[:, :c_sz], l_) + _dot(l_[:, :c_sz], h_)
             + jnp.where(right_half, xu, 0.0) for xu, h_, l_ in zip(x, hi, lo)]
    x_b = [xu.astype(BF16) for xu in x]
    solve_diag = lambda j, z: _dot(x_b[j], jnp.concatenate([zeros_b, z.astype(BF16)], axis=0))
    q_b = [solve_diag(j, jnp.where((ci < ri) & ((ri >> inv_shift) != (ci >> inv_shift)),
                                   a[:c_sz, :c_sz], 0.0)).astype(BF16)
           for j, a in enumerate(a_all)]

    s0 = [st_ref[bb, h] for bb, h in units]
    ls0 = [_dot_nt(col('left', u), s0[j].astype(BF16)) for j, u in enumerate(units)]
    rhs = [ls0[j][:c_sz] + _dot(w_u[j], jnp.concatenate([zeros_b, col('v_b', u)], axis=0))
           for j, u in enumerate(units)]
    g0 = [solve_diag(j, rhs[j]) for j in range(len(units))]
    u_f = g0
    for _ in range(c_sz // RWKV_INV_BLOCK - 1):
        u_f = [g0[j] + _dot(q_b[j], u_f[j].astype(BF16)) for j in range(len(units))]
    u_b = [uj.astype(BF16) for uj in u_f]
    uv = [jnp.concatenate([u_b[j], col('v_b', u)], axis=0) for j, u in enumerate(units)]
    y = [ls0[j][c_sz:] + _dot(w_y[j], uv[j]) for j in range(len(units))]
    uv_t = [_dot_nt(eye_b, uv_j).astype(BF16) for uv_j in uv]
    for j, u in enumerate(units):
        st_ref[u[0], u[1]] = (s0[j] * col('d_p', u)
                              + _dot(uv_t[j], jnp.concatenate([col('bh', u), col('kh', u)], axis=0)))

    for bb in range(len(rows)):
        outs = []
        for h in range(nh):
            sl = slice(h * n, (h + 1) * n)
            yh = y[bb * nh + h]
            mean = jnp.mean(yh, axis=-1, keepdims=True)
            var = jnp.mean(jnp.square(yh - mean), axis=-1, keepdims=True)
            yn = (yh - mean) * lax.rsqrt(var + RWKV_GN_EPS)
            yn = yn * gg_ref[:, sl] + gb_ref[:, sl]
            bonus = jnp.sum(rows[bb]['rk'][:, sl], axis=-1, keepdims=True) * rows[bb]['v'][:, sl]
            outs.append((yn + bonus) * g_ref[bb, :, sl])
        o_ref[bb] = jnp.concatenate(outs, axis=1)


def _rwkv_scan(r, k, v, kk, b, lw, g, r_k, gn_gain, gn_bias, *, batch, seq, nb=4):
    t = r.shape[0]
    nc = seq // RWKV_CHUNK
    tok = pl.BlockSpec((nb, RWKV_CHUNK, RWKV_DIM), lambda bi, c: (bi, c, 0))
    par = pl.BlockSpec((1, RWKV_DIM), lambda bi, c: (0, 0))
    per_batch = lambda a: a.reshape(batch, seq, RWKV_DIM)
    out = pl.pallas_call(
        _rwkv_scan_body,
        out_shape=jax.ShapeDtypeStruct((batch, seq, RWKV_DIM), F32),
        grid=(batch // nb, nc),
        in_specs=[tok] * 7 + [par] * 3,
        out_specs=tok,
        scratch_shapes=[pltpu.VMEM((nb, RWKV_HEADS, RWKV_HEAD_DIM, RWKV_HEAD_DIM), F32)],
        compiler_params=_params(("parallel", "arbitrary")),
        name="rwkv_scan",
    )(*(per_batch(a) for a in (r, k, v, kk, b, lw, g)), r_k, gn_gain, gn_bias)
    return out.reshape(t, RWKV_DIM)


def _compress_body(x_ref, w1s_ref, w1_ref, pe_ref, w2_ref, w2t_ref, o_ref, ot_ref):
    n_rows = x_ref.shape[1]
    for kind in range(2):
        pe = jnp.broadcast_to(pe_ref[kind], (8, pe_ref.shape[2])).astype(BF16)
        pe_term = _dot(pe, w1_ref[kind])[0:1, :]
        for gi in range(NSA_KV_GROUPS):
            tile = kind * NSA_KV_GROUPS + gi
            both = jnp.zeros((n_rows, 2 * CMP_HIDDEN), F32)
            for l in range(CMP_STRIDE):
                both = both + _dot(x_ref[0, :, l, tile * LANES:(tile + 1) * LANES], w1s_ref[kind, l])
            second_next = pltpu.roll(both[:, CMP_HIDDEN:], n_rows - 1, 0)
            hid = both[:, :CMP_HIDDEN] + second_next + pe_term
            act = jax.nn.gelu(hid).astype(BF16)
            o_ref[kind, gi] = _dot(act, w2_ref[kind])
            ot_ref[kind, gi] = _dot_nt(w2t_ref[kind], act)


def _compress(kv, w1, pe, w2, *, bsz, seq):
    g, dh = NSA_KV_GROUPS, NSA_HEAD_DIM
    rows = seq // CMP_STRIDE
    w1r = jnp.pad(w1.reshape(2, CMP_LEN, dh, CMP_HIDDEN), ((0, 0), (0, 0), (0, LANES - dh), (0, 0)))
    w1s = jnp.concatenate([w1r[:, :CMP_STRIDE], w1r[:, CMP_STRIDE:]], axis=3)
    whole = lambda a: pl.BlockSpec(a.shape, lambda b: (0,) * a.ndim)
    w2t = w2.transpose(0, 2, 1)
    return pl.pallas_call(
        _compress_body,
        out_shape=[jax.ShapeDtypeStruct((2, bsz * g, rows, dh), F32),
                   jax.ShapeDtypeStruct((2, bsz * g, dh, rows), F32)],
        grid=(bsz,),
        in_specs=[pl.BlockSpec((1, rows, CMP_STRIDE, 2 * g * LANES), lambda b: (b, 0, 0, 0)),
                  whole(w1s), whole(w1), whole(pe), whole(w2), whole(w2t)],
        out_specs=[pl.BlockSpec((2, g, rows, dh), lambda b: (0, b, 0, 0)),
                   pl.BlockSpec((2, g, dh, rows), lambda b: (0, b, 0, 0))],
        compiler_params=_params(("parallel",)),
        name="nsa_compress",
    )(kv.reshape(bsz, rows, CMP_STRIDE, KV_COLS), w1s, w1, pe, w2, w2t)


def _t5_bucket(dist):
    n = jnp.maximum(dist, 0)
    exact = REL_BUCKETS // 2
    nf = jnp.maximum(n, 1).astype(F32)
    scaled = jnp.log(nf / exact) / math.log(REL_MAX_DIST / exact) * (REL_BUCKETS - exact)
    large = exact + jnp.floor(scaled).astype(jnp.int32)
    large = jnp.minimum(large, REL_BUCKETS - 1)
    return jnp.where(n < exact, n, large)


def _bias_body(tab_ref, bc_ref, bt_ref, *, seq, n_cmp_pad):
    h = pl.program_id(0)
    tq = NSA_TQ

    def lookup(dist):
        bucket = _t5_bucket(dist)
        out = jnp.zeros(dist.shape, F32)
        for bkt in range(REL_BUCKETS):
            out = jnp.where(bucket == bkt, tab_ref[bkt, h] * LOG2E, out)
        return out

    far = tab_ref[REL_BUCKETS - 1, h] * LOG2E
    assert tq + 1 >= REL_MAX_DIST and WINDOW - tq + 1 >= REL_MAX_DIST

    key = lax.broadcasted_iota(jnp.int32, (tq, tq), 0)
    qry = lax.broadcasted_iota(jnp.int32, (tq, tq), 1)
    bt_ref[0, BT_DIAG] = jnp.where(qry >= key, lookup(qry - key), MASKED)
    bt_ref[0, BT_PREV] = lookup(tq + qry - key)
    bt_ref[0, BT_FAR] = jnp.full((tq, tq), far, F32)
    bt_ref[0, BT_WIN_EDGE] = jnp.where(qry < key, far, MASKED)
    bt_ref[0, BT_NONE] = jnp.full((tq, tq), MASKED, F32)

    per_tile = tq // CMP_STRIDE
    pad = 16
    assert pad * CMP_STRIDE >= REL_MAX_DIST + CMP_LEN - 1 and pad % 8 == 0
    band = per_tile + pad
    cmp_end = lax.broadcasted_iota(jnp.int32, (band, tq), 0) * CMP_STRIDE + CMP_LEN - 1
    qry_c = lax.broadcasted_iota(jnp.int32, (band, tq), 1)

    def cmp_tile(i, carry):
        start = pl.multiple_of(jnp.maximum(i * per_tile - pad, 0), 8)
        bc_ref[0, i] = jnp.full((n_cmp_pad, tq), far, F32)
        bc_ref[0, i, pl.ds(start, band), :] = lookup(i * tq + qry_c - (start * CMP_STRIDE + cmp_end))
        return carry

    lax.fori_loop(0, seq // tq, cmp_tile, 0)


def _bias_tables(rel_bias, seq):
    g, hpg, tq = NSA_KV_GROUPS, NSA_HPG, NSA_TQ
    n_cmp_pad = seq // CMP_STRIDE
    nq = seq // tq
    return pl.pallas_call(
        functools.partial(_bias_body, seq=seq, n_cmp_pad=n_cmp_pad),
        out_shape=[jax.ShapeDtypeStruct((g, nq, n_cmp_pad, hpg * tq), F32),
                   jax.ShapeDtypeStruct((g, BT_COUNT, tq, hpg * tq), F32)],
        grid=(NSA_HEADS,),
        in_specs=[pl.BlockSpec(memory_space=pltpu.SMEM)],
        out_specs=[pl.BlockSpec((1, nq, n_cmp_pad, tq), lambda h: (h // hpg, 0, 0, h % hpg)),
                   pl.BlockSpec((1, BT_COUNT, tq, tq), lambda h: (h // hpg, 0, 0, h % hpg))],
        compiler_params=_params(("parallel",)),
        name="nsa_bias",
    )(rel_bias)


def _eye(n, dtype):
    return (lax.broadcasted_iota(jnp.int32, (n, n), 0)
            == lax.broadcasted_iota(jnp.int32, (n, n), 1)).astype(dtype)


def _nsa_body(q_ref, kc_ref, vct_ref, ks_ref, vst_ref, kw_ref, vwt_ref, bc_ref, bt_ref, gate_ref,
              o_ref, m_ref, acc_ref):
    tq, hpg, dh, n_grp = NSA_TQ, NSA_HPG, NSA_HEAD_DIM, NSA_KV_GROUPS
    rws = hpg * tq
    n_blk_log2 = SEL_BLOCK.bit_length() - 1
    n_blk = ks_ref.shape[1] // SEL_BLOCK
    n_cmp_pad = kc_ref.shape[2]
    step = pl.program_id(1)
    n_win = WINDOW // tq
    chains = [(gi, qi) for qi in range(NSA_QT) for gi in range(n_grp)]
    lanes_of = lambda gi: slice(gi * LANES, (gi + 1) * LANES)

    def per_head(x):
        return [x[:, hh * tq:(hh + 1) * tq] for hh in range(hpg)]

    def before_loop(gi, qi):
        i = NSA_QT * step + qi
        rows = slice(qi * tq, (qi + 1) * tq)
        xq = (q_ref[0, rows, gi * hpg * dh:(gi + 1) * hpg * dh] * (dh ** -0.5 * LOG2E)).astype(BF16)
        eye_d = _eye(dh, BF16)
        q_t = jnp.concatenate([_dot_nt(eye_d, xq[:, hh * dh:(hh + 1) * dh]) for hh in range(hpg)],
                              axis=1).astype(BF16)

        cmp_id = lax.broadcasted_iota(jnp.int32, (n_cmp_pad, rws), 0)
        t_pos = i * tq + (lax.broadcasted_iota(jnp.int32, (n_cmp_pad, rws), 1) & (tq - 1))
        valid = (t_pos - (cmp_id * CMP_STRIDE + CMP_LEN - 1) >= 0) & (cmp_id < n_cmp_pad - 1)
        s = jnp.where(valid, _dot(kc_ref[0, gi].astype(BF16), q_t) + bc_ref[gi, qi], MASKED)
        e = jnp.where(valid, jnp.exp2(s - jnp.max(s, axis=0, keepdims=True)), 0.0)
        den = jnp.sum(e, axis=0, keepdims=True)
        p_c = e / jnp.where(den > 0.0, den, 1.0)
        o_cmp = _dot(vct_ref[0, gi].astype(BF16), p_c.astype(BF16))

        p_heads = per_head(p_c)
        p_sum = p_heads[0]
        for ph in p_heads[1:]:
            p_sum = p_sum + ph
        blk_o = lax.broadcasted_iota(jnp.int32, (n_blk, n_cmp_pad), 0)
        cmp_o = lax.broadcasted_iota(jnp.int32, (n_blk, n_cmp_pad), 1)
        overlap_t = ((cmp_o * CMP_STRIDE <= blk_o * SEL_BLOCK + SEL_BLOCK - 1)
                     & (cmp_o * CMP_STRIDE + CMP_LEN - 1 >= blk_o * SEL_BLOCK)).astype(F32)
        imp = _dot(overlap_t, p_sum, HI)
        jj = lax.broadcasted_iota(jnp.int32, (n_blk, tq), 0)
        cur = (i * tq + lax.broadcasted_iota(jnp.int32, (n_blk, tq), 1)) >> n_blk_log2
        forced = (jj == 0) | (jj == cur) | (jj == cur - 1)
        imp = jnp.where(jj > cur, -1e6, jnp.where(forced, 1e6, imp))
        rank = jnp.zeros((n_blk, tq), jnp.int32)
        for a in range(n_blk):
            row = imp[a:a + 1, :]
            beats = (row > imp) | ((row == imp) & (a < jj))
            rank = rank + beats.astype(jnp.int32)
        not_sel = jnp.where(rank < SEL_TOP_N, 0.0, 1.0).astype(BF16)

        q_aug = jnp.concatenate([q_t, jnp.concatenate([not_sel] * hpg, axis=1),
                                 jnp.zeros((LANES - dh - n_blk, rws), BF16)], axis=0)
        q_pad = jnp.concatenate([q_t, jnp.zeros((LANES - dh, rws), BF16)], axis=0)

        tiles, scores = [], []
        for delta in range(n_win + 1):
            entry = {0: BT_DIAG, 1: BT_PREV, n_win: BT_WIN_EDGE}.get(delta, BT_FAR)
            if delta > 0:
                entry = jnp.where(i - delta >= 0, entry, BT_NONE)
            tiles.append(jnp.maximum(i - delta, 0))
            off = pl.multiple_of(tiles[-1] * tq, tq)
            scores.append(_dot(kw_ref[0, pl.ds(off, tq), lanes_of(gi)], q_pad) + bt_ref[gi, entry])
        m_all = scores[0]
        for sc in scores[1:]:
            m_all = jnp.maximum(m_all, sc)
        m_w = jnp.max(m_all, axis=0, keepdims=True)
        acc_w = jnp.zeros((LANES, rws), F32)
        for kt, sc in zip(tiles, scores):
            acc_w = acc_w + _dot(vwt_ref[0, kt, lanes_of(gi), :], jnp.exp2(sc - m_w).astype(BF16))
        o_win = acc_w[0:dh] / acc_w[dh:dh + 1]

        g_t = _dot_nt(_eye(GATE_PAD, F32), gate_ref[0, rows, gi * GATE_PAD:(gi + 1) * GATE_PAD], HI)
        sig = jax.nn.sigmoid(g_t[0:4 * hpg])
        gate = [jnp.concatenate([sig[br * hpg + hh:br * hpg + hh + 1] for hh in range(hpg)], axis=1)
                for br in range(3)]
        return q_aug, gate[0] * o_cmp + gate[2] * o_win, gate[1]

    prepared = {c: before_loop(*c) for c in chains}

    m_ref[...] = jnp.full(m_ref.shape, MASKED, F32)
    acc_ref[...] = jnp.zeros(acc_ref.shape, F32)

    def sel_update(tiles, which):
        for gi, qi in which:
            i = NSA_QT * step + qi
            q_aug = prepared[(gi, qi)][0]
            scores = []
            for kt in tiles:
                off = pl.multiple_of(kt * tq, tq)
                scores.append(_dot(ks_ref[0, pl.ds(off, tq), lanes_of(gi)], q_aug)
                              + bt_ref[gi, jnp.minimum(i - kt, BT_FAR)])
            s_max = scores[0]
            for sc in scores[1:]:
                s_max = jnp.maximum(s_max, sc)
            m_prev = m_ref[qi, gi]
            m_new = jnp.maximum(m_prev, jnp.max(s_max, axis=0, keepdims=True))
            acc = jnp.exp2(m_prev - m_new) * acc_ref[qi, gi]
            for kt, sc in zip(tiles, scores):
                acc = acc + _dot(vst_ref[0, kt, lanes_of(gi), :], jnp.exp2(sc - m_new).astype(BF16))
            m_ref[qi, gi] = m_new
            acc_ref[qi, gi] = acc

    def sel_pair(j, carry):
        sel_update([2 * j, 2 * j + 1], chains)
        return carry

    assert NSA_QT % 2 == 0
    lax.fori_loop(0, NSA_QT * step // 2, sel_pair, 0)
    last = NSA_QT * step
    for r in range(0, NSA_QT, 2):
        sel_update([last + r], [c for c in chains if c[1] == r])
        sel_update([last + r, last + r + 1], [c for c in chains if c[1] > r])

    eye_q = _eye(tq, BF16)
    for qi in range(NSA_QT):
        outs = []
        for gi in range(n_grp):
            _, cmp_win, gate_sel = prepared[(gi, qi)]
            o_sel = acc_ref[qi, gi, 0:dh, :] / acc_ref[qi, gi, dh:dh + 1, :]
            y_t = (cmp_win + gate_sel * o_sel).astype(BF16)
            outs += [_dot_nt(eye_q, yh) for yh in per_head(y_t)]
        o_ref[0, qi * tq:(qi + 1) * tq, :] = jnp.concatenate(outs, axis=1).astype(BF16)


def _nsa_attention(proj, kv, kv_t, kvc, kvc_t, bias_c, bias_t, *, bsz, seq):
    g, tq, hpg, dh = NSA_KV_GROUPS, NSA_TQ, NSA_HPG, NSA_HEAD_DIM
    n_rows = kvc.shape[2]
    nq = seq // tq
    qt = NSA_QT
    kv_spec = lambda pos: pl.BlockSpec((1, seq, g * LANES), lambda b, i: (b, 0, pos))
    kvt_spec = lambda pos: pl.BlockSpec((1, nq, g * LANES, tq), lambda b, i: (b, 0, pos, 0))
    kv3 = kv.reshape(bsz, seq, KV_COLS)
    proj3 = proj.reshape(bsz, seq, PROJ_COLS)
    out = pl.pallas_call(
        _nsa_body,
        out_shape=jax.ShapeDtypeStruct((bsz, seq, NSA_DIM), BF16),
        grid=(bsz, nq // qt),
        in_specs=[
            pl.BlockSpec((1, qt * tq, NSA_DIM), lambda b, i: (b, i, COL_QNSA // NSA_DIM)),
            pl.BlockSpec((1, g, n_rows, dh), lambda b, i: (0, b, 0, 0)),
            pl.BlockSpec((1, g, dh, n_rows), lambda b, i: (1, b, 0, 0)),
            kv_spec(ROW_K_SEL), kvt_spec(T_V_SEL), kv_spec(ROW_K_WIN), kvt_spec(T_V_WIN),
            pl.BlockSpec((g, qt, n_rows, hpg * tq), lambda b, i: (0, i, 0, 0)),
            pl.BlockSpec((g, BT_COUNT, tq, hpg * tq), lambda b, i: (0, 0, 0, 0),
                         pipeline_mode=pl.Buffered(1)),
            pl.BlockSpec((1, qt * tq, g * GATE_PAD), lambda b, i: (b, i, COL_GNSA // (g * GATE_PAD))),
        ],
        out_specs=pl.BlockSpec((1, qt * tq, NSA_DIM), lambda b, i: (b, i, 0)),
        scratch_shapes=[pltpu.VMEM((qt, g, 1, hpg * tq), F32),
                        pltpu.VMEM((qt, g, LANES, hpg * tq), F32)],
        compiler_params=_params(("parallel", "arbitrary")),
        name="nsa_attention",
    )(proj3, kvc, kvc_t, kv3, kv_t, kv3, kv_t, bias_c, bias_t, proj3)
    return out.reshape(bsz * seq, NSA_DIM)


def _mem_body(q_ref, mem_ref, g_ref, w_ref, o_ref, kv_ref):
    @pl.when(pl.program_id(1) == 0)
    def _():
        kv_ref[...] = _dot(_rms(mem_ref[0], g_ref[...]).astype(BF16), w_ref[...]).astype(BF16)

    outs = []
    for h in range(MEM_HEADS):
        sl = slice(h * MEM_HEAD_DIM, (h + 1) * MEM_HEAD_DIM)
        qh = (q_ref[:, sl] * (MEM_HEAD_DIM ** -0.5)).astype(BF16)
        s = _dot_nt(qh, kv_ref[:, sl])
        e = jnp.exp(s - jnp.max(s, axis=-1, keepdims=True))
        p = e / jnp.sum(e, axis=-1, keepdims=True)
        outs.append(_dot(p.astype(BF16), kv_ref[:, MEM_DIM + h * MEM_HEAD_DIM:
                                                MEM_DIM + (h + 1) * MEM_HEAD_DIM]))
    o_ref[...] = jnp.concatenate(outs, axis=1)


def _mem_attention(proj, mem, gain, w_kv, *, seq, tq=1024):
    t = proj.shape[0]
    bsz, m, d = mem.shape
    per_seq = seq // tq
    return pl.pallas_call(
        _mem_body,
        out_shape=jax.ShapeDtypeStruct((t, MEM_DIM), F32),
        grid=(bsz, per_seq),
        in_specs=[
            pl.BlockSpec((tq, MEM_DIM), lambda b, j: (b * per_seq + j, COL_QMEM // MEM_DIM)),
            pl.BlockSpec((1, m, d), lambda b, j: (b, 0, 0)),
            pl.BlockSpec((1, d), lambda b, j: (0, 0)),
            pl.BlockSpec((d, 2 * MEM_DIM), lambda b, j: (0, 0), pipeline_mode=pl.Buffered(1)),
        ],
        out_specs=pl.BlockSpec((tq, MEM_DIM), lambda b, j: (b * per_seq + j, 0)),
        scratch_shapes=[pltpu.VMEM((m, 2 * MEM_DIM), BF16)],
        compiler_params=_params(("parallel", "arbitrary")),
        name="mem_attention",
    )(proj, mem, gain, w_kv)


def _merge_body(x_ref, gain_ref, yr_ref, yn_ref, ym_ref, wg_ref, wr_ref, wn_ref, wm_ref, wo_ref,
                o_ref):
    d = x_ref.shape[1]
    x = x_ref[...]
    h = _rms(x, gain_ref[...]).astype(BF16)
    merged = jnp.zeros(x.shape, F32)
    for b, (y_ref, w_ref) in enumerate(((yr_ref, wr_ref), (yn_ref, wn_ref), (ym_ref, wm_ref))):
        gate = jax.nn.sigmoid(_dot(h, wg_ref[:, b * d:(b + 1) * d]))
        merged = merged + gate * _dot(y_ref[...].astype(BF16), w_ref[...])
    o_ref[...] = x + _dot(merged.astype(BF16), wo_ref[...])


def _merge(x, gain, y_rwkv, y_nsa, y_mem, w_g, w_r, w_n, w_m, w_o, *, tm=1024):
    t, d = x.shape
    row = lambda i: (i, 0)
    const = lambda i: (0, 0)
    resident = lambda shape: pl.BlockSpec(shape, const, pipeline_mode=pl.Buffered(1))
    return pl.pallas_call(
        _merge_body,
        out_shape=jax.ShapeDtypeStruct((t, d), F32),
        grid=(t // tm,),
        in_specs=[
            pl.BlockSpec((tm, d), row),
            pl.BlockSpec((1, d), const),
            pl.BlockSpec((tm, RWKV_DIM), row),
            pl.BlockSpec((tm, NSA_DIM), row),
            pl.BlockSpec((tm, MEM_DIM), row),
            resident((d, N_BRANCH * d)),
            resident((RWKV_DIM, d)),
            resident((NSA_DIM, d)),
            resident((MEM_DIM, d)),
            resident((d, d)),
        ],
        out_specs=pl.BlockSpec((tm, d), row),
        compiler_params=_params(("parallel",)),
        name="merge",
    )(x, gain, y_rwkv, y_nsa, y_mem, w_g, w_r, w_n, w_m, w_o)


def _row(a):
    return a.reshape(1, -1)


def _in_proj(x, p, l, seq):
    d = x.shape[1]
    g, hpg, dh = NSA_KV_GROUPS, NSA_HPG, NSA_HEAD_DIM
    w_in = p['w_in'][l]
    o = 0
    parts = {}
    for name, size in (('rwkv', RWKV_PROJ), ('q', NSA_DIM), ('kv', KV_KINDS * NSA_KV_DIM),
                       ('g_nsa', 3 * NSA_HEADS), ('q_mem', MEM_DIM)):
        parts[name] = w_in[:, o:o + size]
        o += size
    gates = parts['g_nsa'].reshape(d, 3, g, hpg).transpose(0, 2, 1, 3).reshape(d, g, 3 * hpg)
    gates = jnp.pad(gates, ((0, 0), (0, 0), (0, GATE_PAD - 3 * hpg))).reshape(d, g * GATE_PAD)
    w_other = jnp.concatenate([parts['q'], parts['q_mem'], gates], axis=1).astype(BF16)
    w_kv = jnp.pad(parts['kv'].reshape(d, KV_KINDS, g, dh), ((0, 0), (0, 0), (0, 0), (0, LANES - dh)))
    w_row = w_kv[:, KV_ROW_KINDS, :, :].reshape(d, KV_COLS).astype(BF16)
    w_t = w_kv[:, KV_T_KINDS, :, :].reshape(d, KV_T_ROWS).T.astype(BF16)
    w_lora = jnp.zeros((LORA_DIM, 3 * RWKV_DIM), F32)
    w_lora = w_lora.at[0:DECAY_LORA, 0:RWKV_DIM].set(p['rwkv_w2'][l])
    w_lora = w_lora.at[DECAY_LORA:DECAY_LORA + AAA_LORA, RWKV_DIM:2 * RWKV_DIM].set(p['rwkv_a2'][l])
    w_lora = w_lora.at[DECAY_LORA + AAA_LORA:, 2 * RWKV_DIM:].set(p['rwkv_g2'][l])
    proj, kv, kv_t, *rwkv_in = _in_proj_call(
        x, seq, _row(p['mix_norm'][l]), parts['rwkv'].astype(BF16), w_other, w_row, w_t,
        _row(p['rwkv_mu'][l]), w_lora.astype(BF16), _row(p['rwkv_w0'][l]), _row(p['rwkv_a0'][l]),
        _row(p['rwkv_k_k'][l]), _row(p['rwkv_k_a'][l]))
    return proj, kv, kv_t, rwkv_in


def _rwkv_branch(rwkv_in, p, l, bsz, seq):
    return _rwkv_scan(*rwkv_in, _row(p['rwkv_r_k'][l]), _row(p['rwkv_gn_gain'][l]),
                      _row(p['rwkv_gn_bias'][l]), batch=bsz, seq=seq)


def _nsa_branch(proj, kv, kv_t, bias_c, bias_t, p, l, bsz, seq):
    w1 = jnp.stack([p['cmp_k_w1'][l], p['cmp_v_w1'][l]]).astype(BF16)
    pe = jnp.stack([p['cmp_pe_k'][l].reshape(1, -1), p['cmp_pe_v'][l].reshape(1, -1)])
    w2 = jnp.stack([p['cmp_k_w2'][l], p['cmp_v_w2'][l]]).astype(BF16)
    kvc, kvc_t = _compress(kv, w1, pe, w2, bsz=bsz, seq=seq)
    return _nsa_attention(proj, kv, kv_t, kvc, kvc_t, bias_c, bias_t, bsz=bsz, seq=seq)


def _mem_branch(proj, mem, p, l):
    w_kv = jnp.concatenate([p['mem_w_k'][l], p['mem_w_v'][l]], axis=1).astype(BF16)
    return _mem_attention(proj, mem, _row(p['mem_norm'][l]), w_kv, seq=proj.shape[0] // mem.shape[0])


def _layer(x, mem, l, bias_c, bias_t, p):
    bsz, seq, d = x.shape
    t = bsz * seq
    row = _row
    x = x.reshape(t, d)

    x = _ffn(x, row(p['ffn1_norm'][l]), p['ffn1_w_gate'][l], p['ffn1_w_up'][l], p['ffn1_w_down'][l],
             row(p['final_norm']), final=False)

    proj, kv, kv_t, rwkv_in = _in_proj(x, p, l, seq)
    y_rwkv = _rwkv_branch(rwkv_in, p, l, bsz, seq)
    y_nsa = _nsa_branch(proj, kv, kv_t, bias_c, bias_t, p, l, bsz, seq)
    y_mem = _mem_branch(proj, mem, p, l)

    w_gate = p['w_in'][l][:, -N_BRANCH * d:].astype(BF16)
    x = _merge(x, row(p['mix_norm'][l]), y_rwkv, y_nsa, y_mem, w_gate,
               p['w_br_rwkv'][l].astype(BF16), p['w_br_nsa'][l].astype(BF16),
               p['w_br_mem'][l].astype(BF16), p['w_out'][l].astype(BF16))

    last = l == p['ffn1_norm'].shape[0] - 1
    x = _ffn(x, row(p['ffn2_norm'][l]), p['ffn2_w_gate'][l], p['ffn2_w_up'][l], p['ffn2_w_down'][l],
             row(p['final_norm']), final=last)
    return x.reshape(bsz, seq, d)


def kernel(x, mem, ffn1_norm, ffn1_w_gate, ffn1_w_up, ffn1_w_down, mix_norm, w_in, rwkv_mu, rwkv_w0, rwkv_w2, rwkv_a0, rwkv_a2, rwkv_g2, rwkv_k_k, rwkv_k_a, rwkv_r_k, rwkv_gn_gain, rwkv_gn_bias, cmp_pe_k, cmp_k_w1, cmp_k_w2, cmp_pe_v, cmp_v_w1, cmp_v_w2, rel_bias, mem_norm, mem_w_k, mem_w_v, w_br_rwkv, w_br_nsa, w_br_mem, w_out, ffn2_norm, ffn2_w_gate, ffn2_w_up, ffn2_w_down, final_norm):
    p = dict(ffn1_norm=ffn1_norm, ffn1_w_gate=ffn1_w_gate, ffn1_w_up=ffn1_w_up,
             ffn1_w_down=ffn1_w_down, mix_norm=mix_norm, w_in=w_in, rwkv_mu=rwkv_mu,
             rwkv_w0=rwkv_w0, rwkv_w2=rwkv_w2, rwkv_a0=rwkv_a0, rwkv_a2=rwkv_a2, rwkv_g2=rwkv_g2,
             rwkv_k_k=rwkv_k_k, rwkv_k_a=rwkv_k_a, rwkv_r_k=rwkv_r_k, rwkv_gn_gain=rwkv_gn_gain,
             rwkv_gn_bias=rwkv_gn_bias, cmp_pe_k=cmp_pe_k, cmp_k_w1=cmp_k_w1, cmp_k_w2=cmp_k_w2,
             cmp_pe_v=cmp_pe_v, cmp_v_w1=cmp_v_w1, cmp_v_w2=cmp_v_w2, mem_norm=mem_norm,
             mem_w_k=mem_w_k, mem_w_v=mem_w_v, w_br_rwkv=w_br_rwkv, w_br_nsa=w_br_nsa,
             w_br_mem=w_br_mem, w_out=w_out, ffn2_norm=ffn2_norm, ffn2_w_gate=ffn2_w_gate,
             ffn2_w_up=ffn2_w_up, ffn2_w_down=ffn2_w_down, final_norm=final_norm)
    bias_c, bias_t = _bias_tables(rel_bias, x.shape[1])
    for l in range(ffn1_norm.shape[0]):
        x = _layer(x, mem, l, bias_c, bias_t, p)
    return x
```
